```python
import math
import jax, jax.numpy as jnp
from jax import lax
import numpy as np

D_MODEL = 1024
BATCH = 8
SEQ = 8192
DEPTH = 2
DEC_BATCH = 8
DEC_SEQ = 16
PAST_LEN = 2048

CHUNK = 64
N_MIXERS = 2
N_S5_LAYERS = (DEPTH + 1) // 2
N_MLSTM_LAYERS = DEPTH // 2
S5_GROUP_CH = 16
S5_GROUPS = D_MODEL // S5_GROUP_CH
S5_STATE = 64
MLSTM_PROJ = 2
MLSTM_INNER = MLSTM_PROJ * D_MODEL
MLSTM_HEADS = 4
MLSTM_HEAD_DIM = MLSTM_INNER // MLSTM_HEADS
MLSTM_CONV = 4
QKV_BLOCK = 4
N_QKV_BLOCKS = MLSTM_INNER // QKV_BLOCK
D_FF = 2816
FFN_CONV = 3
NORM_EPS = 1e-6
LN_EPS = 1e-5
F_BIAS_LO = 3.0
F_BIAS_HI = 6.0

kernel_name = "s5_mlstm_interleaved_streaming_step"


def rms_norm(x, g):
    xf = x.astype(jnp.float32)
    y = xf * lax.rsqrt(jnp.mean(xf * xf, axis=-1, keepdims=True) + NORM_EPS)
    return (y * g.astype(jnp.float32)).astype(x.dtype)


def causal_dwconv(x, buf, w, b):
    width = w.shape[0]
    s = x.shape[1]
    xp = jnp.concatenate([buf.astype(x.dtype), x], axis=1)
    y = b + xp[:, 0:s] * w[0]
    for j in range(1, width):
        y = y + xp[:, j:j + s] * w[j]
    return y, xp[:, xp.shape[1] - (width - 1):]


def s5_mixer(u, h_re, h_im, A_re, A_im, log_dt, B_re, B_im, C_re, C_im, d_skip, w_glu):
    f32 = jnp.float32
    bsz, s, _ = u.shape
    L = min(CHUNK, s)
    nc = s // L
    A = lax.complex(A_re.astype(f32), A_im.astype(f32))
    dt = jnp.exp(log_dt.astype(f32))[:, None]
    A_bar = jnp.exp(A * dt)
    B_bar = ((A_bar - 1.0) / A)[..., None] * lax.complex(B_re.astype(f32), B_im.astype(f32))
    C_mat = lax.complex(C_re.astype(f32), C_im.astype(f32))
    u_chunks = jnp.moveaxis(u.astype(f32).reshape(bsz, nc, L, S5_GROUPS, S5_GROUP_CH), 1, 0)

    def combine(e1, e2):
        a1, b1 = e1
        a2, b2 = e2
        return a1 * a2, a2 * b1 + b2

    def step(h, uc):
        bu = jnp.einsum('gnc,blgc->blgn', B_bar, uc.astype(jnp.complex64))
        bu = bu.at[:, 0].add(A_bar * h)
        a = jnp.broadcast_to(A_bar, bu.shape)
        _, hs = lax.associative_scan(combine, (a, bu), axis=1)
        y = jnp.einsum('gcn,blgn->blgc', C_mat, hs).real
        return hs[:, -1], y

    h0 = lax.complex(h_re.astype(f32), h_im.astype(f32))
    hT, y = lax.scan(step, h0, u_chunks)
    y = jnp.moveaxis(y, 0, 1).reshape(bsz, s, D_MODEL) + d_skip.astype(f32) * u.astype(f32)
    z = jax.nn.gelu(y, approximate=True).astype(u.dtype)
    ag = z @ w_glu
    out = ag[..., :D_MODEL] * jax.nn.sigmoid(ag[..., D_MODEL:])
    return out, hT.real, hT.imag


def headwise(x, w):
    xb = x.reshape(x.shape[:-1] + (N_QKV_BLOCKS, QKV_BLOCK))
    return jnp.einsum('bsnc,ncd->bsnd', xb, w).reshape(x.shape)


def mlstm_chunkwise(q, k, v, i_pre, log_f, C0, n0, m0):
    bsz, s, nh, dh = q.shape
    L = min(CHUNK, s)
    nc = s // L
    causal = jnp.tril(jnp.ones((L, L), dtype=bool))

    def to_chunks(t):
        return jnp.moveaxis(t.reshape((bsz, nc, L) + t.shape[2:]), 1, 0)

    def step(carry, xs):
        C, n, m = carry
        qc, kc, vc, ic, fc = xs
        b = jnp.cumsum(fc, axis=1)
        inter = b + m[:, None, :]
        dmat = b[:, :, None, :] - b[:, None, :, :] + ic[:, None, :, :]
        dmat = jnp.where(causal[None, :, :, None], dmat, -jnp.inf)
        m_t = jnp.maximum(inter, jnp.max(dmat, axis=2))
        wts = jnp.exp(dmat - m_t[:, :, None, :])
        sc = jnp.einsum('bqhd,bshd->bqsh', qc, kc) * wts
        a = jnp.exp(inter - m_t)
        num = jnp.einsum('bqsh,bshd->bqhd', sc, vc) + a[..., None] * jnp.einsum('bqhk,bhvk->bqhv', qc, C)
        den = jnp.sum(sc, axis=2) + a * jnp.einsum('bqhk,bhk->bqh', qc, n)
        h = num / jnp.maximum(jnp.abs(den), jnp.exp(-m_t))[..., None]
        bL = b[:, -1, :]
        g = bL[:, None, :] - b + ic
        m_new = jnp.maximum(bL + m, jnp.max(g, axis=1))
        decay = jnp.exp(bL + m - m_new)
        wg = jnp.exp(g - m_new[:, None, :])
        C_new = decay[..., None, None] * C + jnp.einsum('blh,blhv,blhk->bhvk', wg, vc, kc)
        n_new = decay[..., None] * n + jnp.einsum('blh,blhk->bhk', wg, kc)
        return (C_new, n_new, m_new), h

    xs = (to_chunks(q), to_chunks(k), to_chunks(v), to_chunks(i_pre), to_chunks(log_f))
    (C, n, m), h = lax.scan(step, (C0, n0, m0), xs)
    h = jnp.moveaxis(h, 0, 1).reshape(bsz, s, nh, dh)
    return h, C, n, m


def mlstm_mixer(x, C0, n0, m0, conv_buf, w_up, conv_w, conv_b, wq, wk, wv, w_gate, b_gate, norm_g, skip, w_down):
    f32 = jnp.float32
    bsz, s, _ = x.shape
    up = x @ w_up
    xm, z = up[..., :MLSTM_INNER], up[..., MLSTM_INNER:]
    xc_pre, new_buf = causal_dwconv(xm, conv_buf, conv_w, conv_b)
    xc = jax.nn.silu(xc_pre)
    q = headwise(xc, wq)
    k = headwise(xc, wk)
    v = headwise(xm, wv)
    gates = (q @ w_gate[:MLSTM_INNER] + k @ w_gate[MLSTM_INNER:2 * MLSTM_INNER]
             + v @ w_gate[2 * MLSTM_INNER:]).astype(f32) + b_gate.astype(f32)
    i_pre = gates[..., :MLSTM_HEADS]
    log_f = jax.nn.log_sigmoid(gates[..., MLSTM_HEADS:])

    def heads(t):
        return t.astype(f32).reshape(bsz, s, MLSTM_HEADS, MLSTM_HEAD_DIM)

    h, C, n, m = mlstm_chunkwise(heads(q), heads(k) * (MLSTM_HEAD_DIM ** -0.5), heads(v), i_pre, log_f,
                                 C0.astype(f32), n0.astype(f32), m0.astype(f32))
    mu = jnp.mean(h, axis=-1, keepdims=True)
    var = jnp.mean(jnp.square(h - mu), axis=-1, keepdims=True)
    hn = ((h - mu) * lax.rsqrt(var + LN_EPS)).reshape(bsz, s, MLSTM_INNER) * norm_g.astype(f32)
    hs = hn.astype(x.dtype) + skip * xc
    out = (hs * jax.nn.silu(z)) @ w_down
    return out, C, n, m, new_buf


def conv_ffn(x, buf, w_up, conv_w, conv_b, w_down):
    u = x @ w_up
    uc, new_buf = causal_dwconv(u, buf, conv_w, conv_b)
    gate, val = uc[..., :D_FF], uc[..., D_FF:]
    return (jax.nn.gelu(gate, approximate=True) * val) @ w_down, new_buf


def run_trunk(x, s5_re, s5_im, m_C, m_n, m_m, m_conv, f_conv, w):
    o_re, o_im, o_C, o_n, o_m, o_mconv, o_fconv = [], [], [], [], [], [], []
    for layer in range(DEPTH):
        g = w['norm_g'][layer]
        j = layer // N_MIXERS
        h = rms_norm(x, g[0])
        if layer % N_MIXERS == 0:
            mix, hr, hi = s5_mixer(h, s5_re[j], s5_im[j], w['s5_A_re'][j], w['s5_A_im'][j], w['s5_log_dt'][j],
                                   w['s5_B_re'][j], w['s5_B_im'][j], w['s5_C_re'][j], w['s5_C_im'][j],
                                   w['s5_D'][j], w['s5_w_glu'][j])
            o_re.append(hr)
            o_im.append(hi)
        else:
            mix, C, n, m, cb = mlstm_mixer(h, m_C[j], m_n[j], m_m[j], m_conv[j], w['mlstm_w_up'][j],
                                           w['mlstm_conv_w'][j], w['mlstm_conv_b'][j], w['mlstm_wq'][j],
                                           w['mlstm_wk'][j], w['mlstm_wv'][j], w['mlstm_w_gate'][j],
                                           w['mlstm_b_gate'][j], w['mlstm_norm_g'][j], w['mlstm_skip'][j],
                                           w['mlstm_w_down'][j])
            o_C.append(C)
            o_n.append(n)
            o_m.append(m)
            o_mconv.append(cb)
        x = x + rms_norm(mix, g[1])
        f, fb = conv_ffn(rms_norm(x, g[2]), f_conv[layer], w['ffn_w_up'][layer], w['ffn_conv_w'][layer],
                         w['ffn_conv_b'][layer], w['ffn_w_down'][layer])
        o_fconv.append(fb)
        x = x + rms_norm(f, g[3])
    st = lambda t: jnp.stack(t, axis=0)
    return x, st(o_re), st(o_im), st(o_C), st(o_n), st(o_m), st(o_mconv), st(o_fconv)


def setup_inputs(seed: int = 0) -> dict:
    key = jax.random.key(seed)
    ks = jax.random.split(key, 40)
    f32 = jnp.float32

    def nrm(k, shape, scale):
        return jax.random.normal(k, shape, f32) * scale

    NA, NB = N_S5_LAYERS, N_MLSTM_LAYERS
    a_im_base = jnp.pi * jnp.arange(S5_STATE, dtype=f32)
    b_gate = jnp.concatenate([
        nrm(ks[30], (NB, MLSTM_HEADS), 0.1),
        jnp.linspace(F_BIAS_LO, F_BIAS_HI, MLSTM_HEADS, dtype=f32)[None, :] + nrm(ks[31], (NB, MLSTM_HEADS), 0.01),
    ], axis=-1)
    return {
        'x_prompt': nrm(ks[0], (BATCH, SEQ, D_MODEL), 1.0),
        'x_sample': nrm(ks[1], (DEC_BATCH, DEC_SEQ, D_MODEL), 1.0),
        'state_s5_re': nrm(ks[2], (NA, DEC_BATCH, S5_GROUPS, S5_STATE), 0.1),
        'state_s5_im': nrm(ks[3], (NA, DEC_BATCH, S5_GROUPS, S5_STATE), 0.1),
        'state_mlstm_C': nrm(ks[4], (NB, DEC_BATCH, MLSTM_HEADS, MLSTM_HEAD_DIM, MLSTM_HEAD_DIM), 0.1),
        'state_mlstm_n': nrm(ks[5], (NB, DEC_BATCH, MLSTM_HEADS, MLSTM_HEAD_DIM), 0.1),
        'state_mlstm_m': nrm(ks[6], (NB, DEC_BATCH, MLSTM_HEADS), 1.0),
        'state_mlstm_conv': nrm(ks[7], (NB, DEC_BATCH, MLSTM_CONV - 1, MLSTM_INNER), 1.0),
        'state_ffn_conv': nrm(ks[8], (DEPTH, DEC_BATCH, FFN_CONV - 1, 2 * D_FF), 1.0),
        'norm_g': 1.0 + nrm(ks[9], (DEPTH, 4, D_MODEL), 0.05),
        's5_A_re': -0.5 + nrm(ks[10], (NA, S5_GROUPS, S5_STATE), 0.01),
        's5_A_im': a_im_base + nrm(ks[11], (NA, S5_GROUPS, S5_STATE), 0.01),
        's5_log_dt': jax.random.uniform(ks[12], (NA, S5_GROUPS), f32, minval=math.log(1e-3), maxval=math.log(1e-1)),
        's5_B_re': nrm(ks[13], (NA, S5_GROUPS, S5_STATE, S5_GROUP_CH), (2 * S5_GROUP_CH) ** -0.5),
        's5_B_im': nrm(ks[14], (NA, S5_GROUPS, S5_STATE, S5_GROUP_CH), (2 * S5_GROUP_CH) ** -0.5),
        's5_C_re': nrm(ks[15], (NA, S5_GROUPS, S5_GROUP_CH, S5_STATE), (2 * S5_STATE) ** -0.5),
        's5_C_im': nrm(ks[16], (NA, S5_GROUPS, S5_GROUP_CH, S5_STATE), (2 * S5_STATE) ** -0.5),
        's5_D': nrm(ks[17], (NA, D_MODEL), 0.5),
        's5_w_glu': nrm(ks[18], (NA, D_MODEL, 2 * D_MODEL), D_MODEL ** -0.5),
        'mlstm_w_up': nrm(ks[19], (NB, D_MODEL, 2 * MLSTM_INNER), D_MODEL ** -0.5),
        'mlstm_conv_w': nrm(ks[20], (NB, MLSTM_CONV, MLSTM_INNER), MLSTM_CONV ** -0.5),
        'mlstm_conv_b': nrm(ks[21], (NB, MLSTM_INNER), 0.01),
        'mlstm_wq': nrm(ks[22], (NB, N_QKV_BLOCKS, QKV_BLOCK, QKV_BLOCK), QKV_BLOCK ** -0.5),
        'mlstm_wk': nrm(ks[23], (NB, N_QKV_BLOCKS, QKV_BLOCK, QKV_BLOCK), QKV_BLOCK ** -0.5),
        'mlstm_wv': nrm(ks[24], (NB, N_QKV_BLOCKS, QKV_BLOCK, QKV_BLOCK), QKV_BLOCK ** -0.5),
        'mlstm_w_gate': nrm(ks[25], (NB, 3 * MLSTM_INNER, 2 * MLSTM_HEADS), (3 * MLSTM_INNER) ** -0.5),
        'mlstm_b_gate': b_gate,
        'mlstm_norm_g': 1.0 + nrm(ks[26], (NB, MLSTM_INNER), 0.05),
        'mlstm_skip': 1.0 + nrm(ks[27], (NB, MLSTM_INNER), 0.05),
        'mlstm_w_down': nrm(ks[28], (NB, MLSTM_INNER, D_MODEL), MLSTM_INNER ** -0.5),
        'ffn_w_up': nrm(ks[32], (DEPTH, D_MODEL, 2 * D_FF), D_MODEL ** -0.5),
        'ffn_conv_w': nrm(ks[33], (DEPTH, FFN_CONV, 2 * D_FF), FFN_CONV ** -0.5),
        'ffn_conv_b': nrm(ks[34], (DEPTH, 2 * D_FF), 0.01),
        'ffn_w_down': nrm(ks[35], (DEPTH, D_FF, D_MODEL), D_FF ** -0.5),
    }


def reference(x_prompt, x_sample, state_s5_re, state_s5_im, state_mlstm_C, state_mlstm_n, state_mlstm_m,
              state_mlstm_conv, state_ffn_conv, norm_g, s5_A_re, s5_A_im, s5_log_dt, s5_B_re, s5_B_im,
              s5_C_re, s5_C_im, s5_D, s5_w_glu, mlstm_w_up, mlstm_conv_w, mlstm_conv_b, mlstm_wq, mlstm_wk,
              mlstm_wv, mlstm_w_gate, mlstm_b_gate, mlstm_norm_g, mlstm_skip, mlstm_w_down, ffn_w_up,
              ffn_conv_w, ffn_conv_b, ffn_w_down):
    w = {
        'norm_g': norm_g, 's5_A_re': s5_A_re, 's5_A_im': s5_A_im, 's5_log_dt': s5_log_dt,
        's5_B_re': s5_B_re, 's5_B_im': s5_B_im, 's5_C_re': s5_C_re, 's5_C_im': s5_C_im, 's5_D': s5_D,
        's5_w_glu': s5_w_glu, 'mlstm_w_up': mlstm_w_up, 'mlstm_conv_w': mlstm_conv_w,
        'mlstm_conv_b': mlstm_conv_b, 'mlstm_wq': mlstm_wq, 'mlstm_wk': mlstm_wk, 'mlstm_wv': mlstm_wv,
        'mlstm_w_gate': mlstm_w_gate, 'mlstm_b_gate': mlstm_b_gate, 'mlstm_norm_g': mlstm_norm_g,
        'mlstm_skip': mlstm_skip, 'mlstm_w_down': mlstm_w_down, 'ffn_w_up': ffn_w_up,
        'ffn_conv_w': ffn_conv_w, 'ffn_conv_b': ffn_conv_b, 'ffn_w_down': ffn_w_down,
    }
    f32 = jnp.float32
    pb = x_prompt.shape[0]
    z_s5 = jnp.zeros((N_S5_LAYERS, pb, S5_GROUPS, S5_STATE), f32)
    z_C = jnp.zeros((N_MLSTM_LAYERS, pb, MLSTM_HEADS, MLSTM_HEAD_DIM, MLSTM_HEAD_DIM), f32)
    z_n = jnp.zeros((N_MLSTM_LAYERS, pb, MLSTM_HEADS, MLSTM_HEAD_DIM), f32)
    z_m = jnp.zeros((N_MLSTM_LAYERS, pb, MLSTM_HEADS), f32)
    z_mconv = jnp.zeros((N_MLSTM_LAYERS, pb, MLSTM_CONV - 1, MLSTM_INNER), x_prompt.dtype)
    z_fconv = jnp.zeros((DEPTH, pb, FFN_CONV - 1, 2 * D_FF), x_prompt.dtype)
    y_prompt, p_re, p_im, p_C, p_n, p_m, p_mconv, p_fconv = run_trunk(
        x_prompt, z_s5, z_s5, z_C, z_n, z_m, z_mconv, z_fconv, w)
    y_sample, s_re, s_im, s_C, s_n, s_m, s_mconv, s_fconv = run_trunk(
        x_sample, state_s5_re, state_s5_im, state_mlstm_C, state_mlstm_n, state_mlstm_m,
        state_mlstm_conv, state_ffn_conv, w)
    return (y_prompt, y_sample, p_re, p_im, p_C, p_n, p_m, p_mconv, p_fconv,
            s_re, s_im, s_C, s_n, s_m, s_mconv, s_fconv)
```

```python
import functools
import math
from typing import NamedTuple

import jax
import jax.numpy as jnp
from jax import lax
from jax.experimental import pallas as pl
from jax.experimental.pallas import tpu as pltpu

F32 = jnp.float32
BF16 = jnp.bfloat16
HI = lax.Precision.HIGHEST

NORM_EPS = 1e-6
LN_EPS = 1e-5
S5_GROUP_CH = 16
S5_CHUNK = 16
QKV_BLOCK = 4

V7X_LANES = 128
V7X_SUBLANES = 8
V7X_MXU = 256
V7X_VMEM_BYTES = 64 * 1024 * 1024


class Tiles(NamedTuple):
    rows: int
    time: int
    s5_rows: int
    mlstm_chunk: int


def _pick_tiles(batch, seq):
    return Tiles(rows=min(batch * seq, 512), time=min(seq, 512),
                 s5_rows=min(batch * (seq // S5_CHUNK), 512), mlstm_chunk=min(seq, 256))


def _vmem_limit(block_bytes, scratch_bytes):
    want = 2 * block_bytes + scratch_bytes + 16 * 1024 * 1024
    return int(min(want, V7X_VMEM_BYTES - 8 * 1024 * 1024))


def _nbytes(shape, dtype):
    return math.prod(shape) * jnp.dtype(dtype).itemsize


def _rms(x, g):
    return x * lax.rsqrt(jnp.mean(x * x, axis=-1, keepdims=True) + NORM_EPS) * g


def _dot(a, b):
    return jnp.dot(a, b, preferred_element_type=F32)


def _dot_nt(a, b):
    return lax.dot_general(a, b, (((1,), (1,)), ((), ())), preferred_element_type=F32)


def _silu(x):
    return x * jax.nn.sigmoid(x)


def _log_sigmoid(x):
    return jnp.minimum(x, 0.0) - jnp.log1p(jnp.exp(-jnp.abs(x)))


def _norm_cast_kernel(x_ref, g_ref, o_ref):
    o_ref[...] = _rms(x_ref[...], g_ref[...]).astype(o_ref.dtype)


def _norm_cast(x2d, g, tb):
    rows, d = x2d.shape
    return pl.pallas_call(
        _norm_cast_kernel,
        grid=(rows // tb,),
        in_specs=[pl.BlockSpec((tb, d), lambda i: (i, 0)), pl.BlockSpec((1, d), lambda i: (0, 0))],
        out_specs=pl.BlockSpec((tb, d), lambda i: (i, 0)),
        out_shape=jax.ShapeDtypeStruct((rows, d), BF16),
        name="s5_norm",
    )(x2d, g)


def _s5_kernel(u_ref, t_ref, pre_ref, pim_ref, qre_ref, qim_ref, are_ref, aim_ref, h0re_ref, h0im_ref,
               y_ref, hre_ref, him_ref, pure_s, puim_s, hinre_s, hinim_s, stre_s, stim_s, *, nb, rb):
    r = pl.program_id(1)

    @pl.when(r == 0)
    def _():
        stre_s[...] = h0re_ref[0]
        stim_s[...] = h0im_ref[0]

    u = u_ref[...]
    pure_s[...] = _dot(u, pre_ref[0])
    puim_s[...] = _dot(u, pim_ref[0])
    ar = jnp.broadcast_to(are_ref[0], (nb, 2 * 64))
    ai = jnp.broadcast_to(aim_ref[0], (nb, 2 * 64))

    def step(i, carry):
        re, im = carry
        rows = pl.ds(pl.multiple_of(i * nb, nb), nb)
        hinre_s[rows, :] = re
        hinim_s[rows, :] = im
        return (ar * re - ai * im + pure_s[rows, :], ar * im + ai * re + puim_s[rows, :])

    re, im = lax.fori_loop(0, rb // nb, step, (stre_s[...], stim_s[...]))
    stre_s[...] = re
    stim_s[...] = im
    hre_ref[0] = re
    him_ref[0] = im

    carry_in = _dot(hinre_s[...].astype(BF16), qre_ref[0]) + _dot(hinim_s[...].astype(BF16), qim_ref[0])
    w = V7X_MXU
    for k in range(2):
        y_ref[:, k * w:(k + 1) * w] = (_dot(u[:, k * w:(k + 1) * w], t_ref[k])
                                       + carry_in[:, k * w:(k + 1) * w]).astype(y_ref.dtype)


def _s5_core(ut, mats, h0re, h0im, nb, rb):
    rows, cols = ut.shape
    pairs = cols // (2 * V7X_MXU)
    tm, pre, pim, qre, qim, are, aim = mats
    pair3 = lambda p, r: (p, 0, 0)
    kern = functools.partial(_s5_kernel, nb=nb, rb=rb)
    blocks = (_nbytes((rb, 512), BF16) * 2 + _nbytes((2, 256, 256), BF16) + 4 * _nbytes((512, 128), BF16))
    scratch = 4 * _nbytes((rb, 128), F32) + 2 * _nbytes((nb, 128), F32)
    return pl.pallas_call(
        kern,
        grid=(pairs, rows // rb),
        in_specs=[
            pl.BlockSpec((rb, 2 * V7X_MXU), lambda p, r: (r, p)),
            pl.BlockSpec((2, V7X_MXU, V7X_MXU), pair3),
            pl.BlockSpec((1, 2 * V7X_MXU, 128), pair3),
            pl.BlockSpec((1, 2 * V7X_MXU, 128), pair3),
            pl.BlockSpec((1, 128, 2 * V7X_MXU), pair3),
            pl.BlockSpec((1, 128, 2 * V7X_MXU), pair3),
            pl.BlockSpec((1, 1, 128), pair3),
            pl.BlockSpec((1, 1, 128), pair3),
            pl.BlockSpec((1, nb, 128), pair3),
            pl.BlockSpec((1, nb, 128), pair3),
        ],
        out_specs=[
            pl.BlockSpec((rb, 2 * V7X_MXU), lambda p, r: (r, p)),
            pl.BlockSpec((1, nb, 128), pair3),
            pl.BlockSpec((1, nb, 128), pair3),
        ],
        out_shape=[
            jax.ShapeDtypeStruct((rows, cols), BF16),
            jax.ShapeDtypeStruct((pairs, nb, 128), F32),
            jax.ShapeDtypeStruct((pairs, nb, 128), F32),
        ],
        scratch_shapes=[pltpu.VMEM((rb, 128), F32)] * 4 + [pltpu.VMEM((nb, 128), F32)] * 2,
        compiler_params=pltpu.CompilerParams(
            dimension_semantics=("arbitrary", "arbitrary"), vmem_limit_bytes=_vmem_limit(blocks, scratch)),
        name="s5_core",
    )(ut, tm, pre, pim, qre, qim, are, aim, h0re, h0im)


def _s5_matrices(a_re, a_im, log_dt, b_re, b_im, c_re, c_im):
    lc = S5_CHUNK
    groups, n = a_re.shape
    ch = S5_GROUP_CH
    a = lax.complex(a_re.astype(F32), a_im.astype(F32))
    adt = a * jnp.exp(log_dt.astype(F32))[:, None]
    j = jnp.arange(lc + 1, dtype=F32)
    apow = jnp.exp(adt[None] * j[:, None, None])
    b_bar = ((apow[1] - 1.0) / a)[..., None] * lax.complex(b_re.astype(F32), b_im.astype(F32))
    c_mat = lax.complex(c_re.astype(F32), c_im.astype(F32))
    ca = c_mat[None] * apow[:, :, None, :]
    kern = (jnp.einsum('jgcn,gnd->jgcd', ca.real, b_bar.real, precision=HI)
            - jnp.einsum('jgcn,gnd->jgcd', ca.imag, b_bar.imag, precision=HI))
    s_idx = jnp.arange(lc)[:, None]
    t_idx = jnp.arange(lc)[None, :]
    lag = t_idx - s_idx
    kt = kern[jnp.clip(lag, 0, lc)]
    kt = jnp.where((lag >= 0)[:, :, None, None, None], kt, 0.0)
    tm = jnp.transpose(kt, (2, 0, 4, 1, 3)).reshape(groups, lc * ch, lc * ch)
    pw = apow[lc - 1 - jnp.arange(lc)]
    p = pw[:, :, :, None] * b_bar[None]
    p = jnp.transpose(p, (1, 0, 3, 2)).reshape(groups, lc * ch, n)
    q = ca[1:]
    q = jnp.transpose(q, (1, 3, 0, 2)).reshape(groups, n, lc * ch)

    def pair_rows(m):
        g2 = groups // 2
        m = m.reshape(g2, 2, m.shape[1], m.shape[2])
        z = jnp.zeros_like(m[:, 0])
        top = jnp.concatenate([m[:, 0], z], axis=2)
        bot = jnp.concatenate([z, m[:, 1]], axis=2)
        return jnp.concatenate([top, bot], axis=1)

    pre = pair_rows(p.real).astype(BF16)
    pim = pair_rows(p.imag).astype(BF16)
    qre = pair_rows(q.real).astype(BF16)
    qim = pair_rows(-q.imag).astype(BF16)
    a_chunk = apow[lc].reshape(groups // 2, 1, 2 * n)
    return tm.astype(BF16), pre, pim, qre, qim, a_chunk.real, a_chunk.imag


def _glu_kernel(x_ref, y_ref, g0_ref, d_ref, w_ref, g1_ref, o_ref):
    x = x_ref[...]
    d = x.shape[1]
    u = _rms(x, g0_ref[...])
    yy = y_ref[...].astype(F32) + d_ref[...] * u
    z = jax.nn.gelu(yy, approximate=True).astype(BF16)
    ag = _dot(z, w_ref[...])
    out = ag[:, :d] * jax.nn.sigmoid(ag[:, d:])
    o_ref[...] = x + _rms(out, g1_ref[...])


def _const_spec(shape):
    nd = len(shape)
    return pl.BlockSpec(shape, lambda *_: (0,) * nd, pipeline_mode=pl.Buffered(1))


def _glu(x2d, y2d, g0, dskip, w, g1, tb):
    rows, d = x2d.shape
    blocks = 2 * _nbytes((tb, d), F32) + _nbytes((tb, d), BF16)
    scratch = _nbytes(w.shape, BF16) + 3 * _nbytes((tb, 2 * d), F32)
    row = pl.BlockSpec((tb, d), lambda i: (i, 0))
    return pl.pallas_call(
        _glu_kernel,
        grid=(rows // tb,),
        in_specs=[row, row, _const_spec((1, d)), _const_spec((1, d)), _const_spec(w.shape), _const_spec((1, d))],
        out_specs=row,
        out_shape=jax.ShapeDtypeStruct((rows, d), F32),
        compiler_params=pltpu.CompilerParams(
            dimension_semantics=("arbitrary",), vmem_limit_bytes=_vmem_limit(blocks, scratch)),
        name="s5_glu",
    )(x2d, y2d, g0, dskip, w, g1)


def _ffn_kernel(x_ref, buf_ref, g2_ref, wup_ref, cw_ref, cb_ref, wdown_ref, g3_ref,
                o_ref, nbuf_ref, ubuf, carry, act, *, tb, dff, width):
    t = pl.program_id(1)
    pad = V7X_SUBLANES

    @pl.when(t == 0)
    def _():
        carry[...] = jnp.zeros_like(carry)
        carry[pad - (width - 1):pad, :] = buf_ref[0]

    x = x_ref[0]
    hn = _rms(x, g2_ref[...]).astype(BF16)
    fb = V7X_MXU

    def conv(cs):
        u = _dot(hn, wup_ref[:, cs])
        ubuf[0:pad, :] = carry[:, cs]
        ubuf[pad:pad + tb, :] = u
        carry[:, cs] = ubuf[tb:tb + pad, :]
        acc = cb_ref[:, cs] + u * cw_ref[width - 1:width, cs]
        for j in range(width - 1):
            off = pad - (width - 1) + j
            acc = acc + ubuf[off:off + tb, :] * cw_ref[j:j + 1, cs]
        return acc

    for j in range(dff // fb):
        gate = conv(slice(j * fb, (j + 1) * fb))
        val = conv(slice(dff + j * fb, dff + (j + 1) * fb))
        act[:, j * fb:(j + 1) * fb] = (jax.nn.gelu(gate, approximate=True) * val).astype(BF16)

    f = _dot(act[...], wdown_ref[...])
    o_ref[0] = x + _rms(f, g3_ref[...])
    nbuf_ref[0] = carry[pad - (width - 1):pad, :]


def _ffn(x, buf, g2, wup, cw, cb, wdown, g3, tb):
    b, s, d = x.shape
    dff = wdown.shape[0]
    width = cw.shape[0]
    kern = functools.partial(_ffn_kernel, tb=tb, dff=dff, width=width)
    blocks = 2 * _nbytes((tb, d), F32)
    scratch = (_nbytes(wup.shape, BF16) + _nbytes(wdown.shape, BF16) + _nbytes((tb + 8, V7X_MXU), F32)
               + _nbytes((8, 2 * dff), F32) + _nbytes((tb, dff), BF16) + 2 * _nbytes((tb, d), F32))
    xs = pl.BlockSpec((1, tb, d), lambda i, t: (i, t, 0))
    bs = pl.BlockSpec((1, width - 1, 2 * dff), lambda i, t: (i, 0, 0))
    return pl.pallas_call(
        kern,
        grid=(b, s // tb),
        in_specs=[xs, bs, _const_spec((1, d)), _const_spec(wup.shape), _const_spec(cw.shape),
                  _const_spec((1, 2 * dff)), _const_spec(wdown.shape), _const_spec((1, d))],
        out_specs=[xs, bs],
        out_shape=[jax.ShapeDtypeStruct((b, s, d), F32), jax.ShapeDtypeStruct((b, width - 1, 2 * dff), F32)],
        scratch_shapes=[pltpu.VMEM((tb + V7X_SUBLANES, V7X_MXU), F32), pltpu.VMEM((V7X_SUBLANES, 2 * dff), F32),
                        pltpu.VMEM((tb, dff), BF16)],
        compiler_params=pltpu.CompilerParams(
            dimension_semantics=("arbitrary", "arbitrary"), vmem_limit_bytes=_vmem_limit(blocks, scratch)),
        name="conv_ffn",
    )(x, buf, g2, wup, cw, cb, wdown, g3)


def _gate_fold_kernel(wq_ref, wk_ref, wv_ref, gq_ref, gk_ref, gv_ref, gc_ref, gm_ref):
    dotp = functools.partial(jnp.dot, preferred_element_type=F32, precision=HI)
    gc_ref[...] = dotp(wq_ref[0], gq_ref[...]) + dotp(wk_ref[0], gk_ref[...])
    gm_ref[...] = dotp(wv_ref[0], gv_ref[...])


def _gate_fold(wq_t, wk_t, wv_t, w_gate):
    nt, w, _ = wq_t.shape
    inner = nt * w
    ng = w_gate.shape[1]
    tile = pl.BlockSpec((1, w, w), lambda i: (i, 0, 0))
    gs = [pl.BlockSpec((w, ng), lambda i, k=k: (k * nt + i, 0)) for k in range(3)]
    out = pl.BlockSpec((w, ng), lambda i: (i, 0))
    return pl.pallas_call(
        _gate_fold_kernel,
        grid=(nt,),
        in_specs=[tile, tile, tile] + gs,
        out_specs=[out, out],
        out_shape=[jax.ShapeDtypeStruct((inner, ng), F32)] * 2,
        name="mlstm_gate_fold",
    )(wq_t, wk_t, wv_t, w_gate, w_gate, w_gate)


def _mpre_kernel(x_ref, cbuf_ref, g_ref, wup_ref, cw_ref, cb_ref, wq_ref, wkt_ref, wv_ref, gc_ref, gm_ref,
                 bg_ref, q_ref, kt_ref, v_ref, xc_ref, z_ref, gates_ref, nbuf_ref, ubuf, carry,
                 *, tb, inner, width):
    t = pl.program_id(1)
    pad = V7X_SUBLANES

    @pl.when(t == 0)
    def _():
        carry[...] = jnp.zeros_like(carry)
        carry[pad - (width - 1):pad, :] = cbuf_ref[0]

    h = _rms(x_ref[0], g_ref[...]).astype(BF16)
    w = V7X_MXU
    gacc = jnp.zeros((tb, bg_ref.shape[1]), F32) + bg_ref[...]
    for i in range(inner // w):
        cs = slice(i * w, (i + 1) * w)
        xm = _dot(h, wup_ref[:, cs])
        ubuf[0:pad, :] = carry[:, cs]
        ubuf[pad:pad + tb, :] = xm
        carry[:, cs] = ubuf[tb:tb + pad, :]
        pre = cb_ref[:, cs] + xm * cw_ref[width - 1:width, cs]
        for j in range(width - 1):
            off = pad - (width - 1) + j
            pre = pre + ubuf[off:off + tb, :] * cw_ref[j:j + 1, cs]
        xc = _silu(pre).astype(BF16)
        xmb = xm.astype(BF16)
        q_ref[0, :, cs] = _dot(xc, wq_ref[i]).astype(BF16)
        kt_ref[0, cs, :] = _dot_nt(wkt_ref[i], xc).astype(BF16)
        v_ref[0, :, cs] = _dot(xmb, wv_ref[i]).astype(BF16)
        xc_ref[0, :, cs] = xc
        z_ref[0, :, cs] = _silu(_dot(h, wup_ref[:, inner + i * w:inner + (i + 1) * w])).astype(BF16)
        gacc = gacc + _dot(xc, gc_ref[cs, :]) + _dot(xmb, gm_ref[cs, :])
    gates_ref[0] = gacc
    nbuf_ref[0] = carry[pad - (width - 1):pad, :]


def _mpre(x, cbuf, g, wup, cw, cb, wq_t, wkt_t, wv_t, gc, gm, bg, tb):
    b, s, d = x.shape
    inner = cw.shape[1]
    width = cw.shape[0]
    ng = bg.shape[1]
    kern = functools.partial(_mpre_kernel, tb=tb, inner=inner, width=width)
    act = pl.BlockSpec((1, tb, inner), lambda i, t: (i, t, 0))
    blocks = _nbytes((tb, d), F32) + 5 * _nbytes((tb, inner), BF16)
    scratch = (_nbytes(wup.shape, BF16) + 3 * _nbytes(wq_t.shape, BF16) + 2 * _nbytes((inner, 128), BF16)
               + _nbytes((tb + 8, V7X_MXU), F32) + _nbytes((8, inner), F32) + 4 * _nbytes((tb, V7X_MXU), F32))
    return pl.pallas_call(
        kern,
        grid=(b, s // tb),
        in_specs=[pl.BlockSpec((1, tb, d), lambda i, t: (i, t, 0)),
                  pl.BlockSpec((1, width - 1, inner), lambda i, t: (i, 0, 0)),
                  _const_spec((1, d)), _const_spec(wup.shape), _const_spec(cw.shape), _const_spec((1, inner)),
                  _const_spec(wq_t.shape), _const_spec(wkt_t.shape), _const_spec(wv_t.shape),
                  _const_spec(gc.shape), _const_spec(gm.shape), _const_spec((1, ng))],
        out_specs=[act, pl.BlockSpec((1, inner, tb), lambda i, t: (i, 0, t)), act, act, act,
                   pl.BlockSpec((1, tb, ng), lambda i, t: (i, t, 0)),
                   pl.BlockSpec((1, width - 1, inner), lambda i, t: (i, 0, 0))],
        out_shape=[jax.ShapeDtypeStruct((b, s, inner), BF16), jax.ShapeDtypeStruct((b, inner, s), BF16),
                   jax.ShapeDtypeStruct((b, s, inner), BF16), jax.ShapeDtypeStruct((b, s, inner), BF16),
                   jax.ShapeDtypeStruct((b, s, inner), BF16), jax.ShapeDtypeStruct((b, s, ng), F32),
                   jax.ShapeDtypeStruct((b, width - 1, inner), F32)],
        scratch_shapes=[pltpu.VMEM((tb + V7X_SUBLANES, V7X_MXU), F32), pltpu.VMEM((V7X_SUBLANES, inner), F32)],
        compiler_params=pltpu.CompilerParams(
            dimension_semantics=("arbitrary", "arbitrary"), vmem_limit_bytes=_vmem_limit(blocks, scratch)),
        name="mlstm_pre",
    )(x, cbuf, g, wup, cw, cb, wq_t, wkt_t, wv_t, gc, gm, bg)


def _mlstm_kernel(*refs, lc, heads, dh, zero_init):
    if zero_init:
        (q_ref, kt_ref, v_ref, g_ref, gt_ref,
         hn_ref, cout_ref, nout_ref, mout_ref, ct_s, n_s, m_s) = refs
    else:
        (q_ref, kt_ref, v_ref, g_ref, gt_ref, c0_ref, n0_ref, m0_ref,
         hn_ref, cout_ref, nout_ref, mout_ref, ct_s, n_s, m_s) = refs
    c = pl.program_id(1)

    @pl.when(c == 0)
    def _():
        if zero_init:
            ct_s[...] = jnp.zeros_like(ct_s)
            n_s[...] = jnp.zeros_like(n_s)
            m_s[...] = jnp.zeros_like(m_s)
        else:
            for h in range(heads):
                ct_s[h] = c0_ref[0, h].T
            n_s[...] = n0_ref[0]
            m_s[...] = m0_ref[0]

    g = g_ref[0]
    gt = gt_ref[0]
    row = lax.broadcasted_iota(jnp.int32, (lc, lc), 0)
    col = lax.broadcasted_iota(jnp.int32, (lc, lc), 1)
    causal = col <= row
    for h in range(heads):
        hs = slice(h * dh, (h + 1) * dh)
        i_col = g[:, h:h + 1]
        f_col = _log_sigmoid(g[:, heads + h:heads + h + 1])
        i_row = gt[h:h + 1, :]
        f_row = _log_sigmoid(gt[heads + h:heads + h + 1, :])
        b_col = jnp.sum(jnp.where(causal, f_row, 0.0), axis=1, keepdims=True)
        b_row = jnp.sum(jnp.where(row <= col, f_col, 0.0), axis=0, keepdims=True)
        m_prev = m_s[h:h + 1, 0:1]
        dmat = jnp.where(causal, b_col - b_row + i_row, -jnp.inf)
        inter = b_col + m_prev
        m_t = jnp.maximum(inter, jnp.max(dmat, axis=1, keepdims=True))
        wts = jnp.exp(dmat - m_t)
        qh = q_ref[0, :, hs]
        kth = kt_ref[0, hs, :]
        vh = v_ref[0, :, hs]
        sc = _dot(qh, kth) * wts
        a = jnp.exp(inter - m_t)
        ct = ct_s[h]
        n_row = n_s[h:h + 1, :]
        num = _dot(sc.astype(BF16), vh) + a * _dot(qh, ct.astype(BF16))
        qn = jnp.sum(qh.astype(F32) * n_row, axis=1, keepdims=True)
        den = jnp.sum(sc, axis=1, keepdims=True) + a * qn
        hh = num / jnp.maximum(jnp.abs(den), jnp.exp(-m_t))
        mu = jnp.mean(hh, axis=1, keepdims=True)
        dev = hh - mu
        var = jnp.mean(dev * dev, axis=1, keepdims=True)
        hn_ref[0, :, hs] = (dev * lax.rsqrt(var + LN_EPS)).astype(hn_ref.dtype)

        b_last = b_row[:, lc - 1:lc]
        g_row = b_last - b_row + i_row
        m_new = jnp.maximum(b_last + m_prev, jnp.max(g_row, axis=1, keepdims=True))
        decay = jnp.exp(b_last + m_prev - m_new)
        wg_row = jnp.exp(g_row - m_new)
        wg_col = jnp.exp(b_last - b_col + i_col - m_new)
        wv = (vh.astype(F32) * wg_col).astype(BF16)
        ct_s[h] = decay * ct + _dot(kth, wv)
        wg16 = jnp.broadcast_to(wg_row, (16, lc)).astype(BF16)
        n_s[h:h + 1, :] = decay * n_row + _dot_nt(wg16, kth)[0:1, :]
        m_s[h:h + 1, :] = jnp.broadcast_to(m_new, (1, V7X_LANES))

    @pl.when(c == pl.num_programs(1) - 1)
    def _():
        for h in range(heads):
            cout_ref[0, h] = ct_s[h].T
        nout_ref[0] = n_s[...]
        mout_ref[0] = m_s[...]


def _mlstm_core(q, kt, v, gates, gates_t, state, lc, heads):
    b, s, inner = q.shape
    dh = inner // heads
    ng = gates.shape[2]
    zero_init = state is None
    kern = functools.partial(_mlstm_kernel, lc=lc, heads=heads, dh=dh, zero_init=zero_init)
    act = pl.BlockSpec((1, lc, inner), lambda i, c: (i, c, 0))
    cs = pl.BlockSpec((1, heads, dh, dh), lambda i, c: (i, 0, 0, 0))
    ns = pl.BlockSpec((1, heads, dh), lambda i, c: (i, 0, 0))
    ms = pl.BlockSpec((1, heads, V7X_LANES), lambda i, c: (i, 0, 0))
    in_specs = [act, pl.BlockSpec((1, inner, lc), lambda i, c: (i, 0, c)), act,
                pl.BlockSpec((1, lc, ng), lambda i, c: (i, c, 0)),
                pl.BlockSpec((1, ng, lc), lambda i, c: (i, 0, c))]
    args = [q, kt, v, gates, gates_t]
    if not zero_init:
        in_specs += [cs, ns, ms]
        args += list(state)
    blocks = 4 * _nbytes((lc, inner), BF16) + (1 if zero_init else 2) * _nbytes((heads, dh, dh), F32)
    scratch = _nbytes((heads, dh, dh), F32) + 6 * _nbytes((lc, max(lc, dh)), F32) + 2 * _nbytes((dh, dh), F32)
    return pl.pallas_call(
        kern,
        grid=(b, s // lc),
        in_specs=in_specs,
        out_specs=[act, cs, ns, ms],
        out_shape=[jax.ShapeDtypeStruct((b, s, inner), BF16), jax.ShapeDtypeStruct((b, heads, dh, dh), F32),
                   jax.ShapeDtypeStruct((b, heads, dh), F32), jax.ShapeDtypeStruct((b, heads, V7X_LANES), F32)],
        scratch_shapes=[pltpu.VMEM((heads, dh, dh), F32), pltpu.VMEM((heads, dh), F32),
                        pltpu.VMEM((heads, V7X_LANES), F32)],
        compiler_params=pltpu.CompilerParams(
            dimension_semantics=("arbitrary", "arbitrary"), vmem_limit_bytes=_vmem_limit(blocks, scratch)),
        name="mlstm_core",
    )(*args)


def _mout_kernel(hn_ref, xc_ref, z_ref, x_ref, ng_ref, sk_ref, w_ref, g1_ref, o_ref):
    hs = hn_ref[...].astype(F32) * ng_ref[...] + sk_ref[...] * xc_ref[...].astype(F32)
    a = (hs * z_ref[...].astype(F32)).astype(BF16)
    x = x_ref[...]
    o_ref[...] = x + _rms(_dot(a, w_ref[...]), g1_ref[...])


def _mout(hn, xc, sz, x2d, ng, sk, w, g1, tb):
    rows, d = x2d.shape
    inner = hn.shape[1]
    act = pl.BlockSpec((tb, inner), lambda i: (i, 0))
    row = pl.BlockSpec((tb, d), lambda i: (i, 0))
    blocks = 3 * _nbytes((tb, inner), BF16) + 2 * _nbytes((tb, d), F32)
    scratch = _nbytes(w.shape, BF16) + 3 * _nbytes((tb, inner), F32)
    return pl.pallas_call(
        _mout_kernel,
        grid=(rows // tb,),
        in_specs=[act, act, act, row, _const_spec((1, inner)), _const_spec((1, inner)), _const_spec(w.shape),
                  _const_spec((1, d))],
        out_specs=row,
        out_shape=jax.ShapeDtypeStruct((rows, d), F32),
        compiler_params=pltpu.CompilerParams(
            dimension_semantics=("arbitrary",), vmem_limit_bytes=_vmem_limit(blocks, scratch)),
        name="mlstm_out",
    )(hn, xc, sz, x2d, ng, sk, w, g1)


def _block_diag_tiles(w):
    nblk, k, _ = w.shape
    per = V7X_MXU // k
    wt = w.reshape(nblk // per, per, k, k)
    eye = jnp.eye(per, dtype=w.dtype)
    full = wt[:, :, :, None, :] * eye[None, :, None, :, None]
    return full.reshape(nblk // per, per * k, per * k)


def _prep_weights(w):
    row = lambda v: v.astype(F32).reshape(1, -1)
    p = {}
    p['norm_g'] = w['norm_g'].astype(F32)
    p['s5_mats'] = _s5_matrices(w['s5_A_re'][0], w['s5_A_im'][0], w['s5_log_dt'][0], w['s5_B_re'][0],
                                w['s5_B_im'][0], w['s5_C_re'][0], w['s5_C_im'][0])
    p['s5_D'] = row(w['s5_D'][0])
    p['s5_w_glu'] = w['s5_w_glu'][0].astype(BF16)
    inner = w['mlstm_conv_w'].shape[2]
    heads = w['mlstm_b_gate'].shape[1] // 2
    dh = inner // heads
    wq_t = _block_diag_tiles(w['mlstm_wq'][0].astype(F32))
    wk_t = _block_diag_tiles(w['mlstm_wk'][0].astype(F32))
    wv_t = _block_diag_tiles(w['mlstm_wv'][0].astype(F32))
    gc, gm = _gate_fold(wq_t, wk_t, wv_t, w['mlstm_w_gate'][0].astype(F32))
    p['m_wq'] = wq_t.astype(BF16)
    p['m_wkt'] = (jnp.swapaxes(wk_t, 1, 2) * (dh ** -0.5)).astype(BF16)
    p['m_wv'] = wv_t.astype(BF16)
    p['m_gc'] = gc.astype(BF16)
    p['m_gm'] = gm.astype(BF16)
    p['m_bg'] = row(w['mlstm_b_gate'][0])
    p['m_wup'] = w['mlstm_w_up'][0].astype(BF16)
    p['m_cw'] = w['mlstm_conv_w'][0].astype(F32)
    p['m_cb'] = row(w['mlstm_conv_b'][0])
    p['m_ng'] = row(w['mlstm_norm_g'][0])
    p['m_skip'] = row(w['mlstm_skip'][0])
    p['m_wdown'] = w['mlstm_w_down'][0].astype(BF16)
    p['heads'] = heads
    p['f_wup'] = w['ffn_w_up'].astype(BF16)
    p['f_cw'] = w['ffn_conv_w'].astype(F32)
    p['f_cb'] = w['ffn_conv_b'].astype(F32)
    p['f_wdown'] = w['ffn_w_down'].astype(BF16)
    return p


def _to_chunk_major(u2d, b, s):
    d = u2d.shape[1]
    g = d // S5_GROUP_CH
    u = u2d.reshape(b, s // S5_CHUNK, S5_CHUNK, g, S5_GROUP_CH)
    return jnp.transpose(u, (1, 0, 3, 2, 4)).reshape((s // S5_CHUNK) * b, d * S5_CHUNK)


def _to_token_major(yt, b, s):
    d = yt.shape[1] // S5_CHUNK
    g = d // S5_GROUP_CH
    y = yt.reshape(s // S5_CHUNK, b, g, S5_CHUNK, S5_GROUP_CH)
    return jnp.transpose(y, (1, 0, 3, 2, 4)).reshape(b * s, d)


def _trunk(x, st, p, tiles):
    b, s, d = x.shape
    ng = p['norm_g']
    g_of = lambda layer, k: ng[layer, k].reshape(1, d)
    zero_init = st is None
    groups = d // S5_GROUP_CH
    x2d = x.reshape(b * s, d)

    n_state = p['s5_mats'][1].shape[2] // 2
    if zero_init:
        h0re = h0im = jnp.zeros((groups // 2, b, 2 * n_state), F32)
    else:
        pair = lambda h: jnp.transpose(h.astype(F32).reshape(b, groups // 2, 2 * n_state), (1, 0, 2))
        h0re, h0im = pair(st['s5_re']), pair(st['s5_im'])
    u = _norm_cast(x2d, g_of(0, 0), tiles.rows)
    yt, hre, him = _s5_core(_to_chunk_major(u, b, s), p['s5_mats'], h0re, h0im, b, tiles.s5_rows)
    y2d = _to_token_major(yt, b, s)
    unpair = lambda h: jnp.transpose(h, (1, 0, 2)).reshape(1, b, groups, n_state)
    o_re, o_im = unpair(hre), unpair(him)
    x2d = _glu(x2d, y2d, g_of(0, 0), p['s5_D'], p['s5_w_glu'], g_of(0, 1), tiles.rows)

    def ffn(x2d, layer):
        dff2 = p['f_wup'].shape[2]
        width = p['f_cw'].shape[1]
        buf = jnp.zeros((b, width - 1, dff2), F32) if zero_init else st['ffn_conv'][layer].astype(F32)
        xo, nbuf = _ffn(x2d.reshape(b, s, d), buf, g_of(layer, 2), p['f_wup'][layer], p['f_cw'][layer],
                        p['f_cb'][layer].reshape(1, dff2), p['f_wdown'][layer], g_of(layer, 3), tiles.time)
        return xo.reshape(b * s, d), nbuf

    x2d, fbuf0 = ffn(x2d, 0)

    heads = p['heads']
    inner = p['m_cw'].shape[1]
    width = p['m_cw'].shape[0]
    cbuf = jnp.zeros((b, width - 1, inner), F32) if zero_init else st['mlstm_conv'].astype(F32)
    q, kt, v, xc, sz, gates, ncbuf = _mpre(
        x2d.reshape(b, s, d), cbuf, g_of(1, 0), p['m_wup'], p['m_cw'], p['m_cb'], p['m_wq'], p['m_wkt'],
        p['m_wv'], p['m_gc'], p['m_gm'], p['m_bg'], tiles.time)
    if zero_init:
        state = None
    else:
        state = (st['mlstm_C'].astype(F32), st['mlstm_n'].astype(F32),
                 jnp.broadcast_to(st['mlstm_m'].astype(F32)[:, :, None], (b, heads, V7X_LANES)))
    hn, c_out, n_out, m_out = _mlstm_core(q, kt, v, gates, jnp.swapaxes(gates, 1, 2), state,
                                          tiles.mlstm_chunk, heads)
    x2d = _mout(hn.reshape(b * s, inner), xc.reshape(b * s, inner), sz.reshape(b * s, inner), x2d,
                p['m_ng'], p['m_skip'], p['m_wdown'], g_of(1, 1), tiles.rows)
    x2d, fbuf1 = ffn(x2d, 1)

    return (x2d.reshape(b, s, d), o_re, o_im, c_out[None], n_out[None], m_out[None, :, :, 0], ncbuf[None],
            jnp.stack([fbuf0, fbuf1], axis=0))


def kernel(x_prompt, x_sample, state_s5_re, state_s5_im, state_mlstm_C, state_mlstm_n, state_mlstm_m, state_mlstm_conv, state_ffn_conv, norm_g, s5_A_re, s5_A_im, s5_log_dt, s5_B_re, s5_B_im, s5_C_re, s5_C_im, s5_D, s5_w_glu, mlstm_w_up, mlstm_conv_w, mlstm_conv_b, mlstm_wq, mlstm_wk, mlstm_wv, mlstm_w_gate, mlstm_b_gate, mlstm_norm_g, mlstm_skip, mlstm_w_down, ffn_w_up, ffn_conv_w, ffn_conv_b, ffn_w_down):
    w = {
        'norm_g': norm_g, 's5_A_re': s5_A_re, 's5_A_im': s5_A_im, 's5_log_dt': s5_log_dt,
        's5_B_re': s5_B_re, 's5_B_im': s5_B_im, 's5_C_re': s5_C_re, 's5_C_im': s5_C_im, 's5_D': s5_D,
        's5_w_glu': s5_w_glu, 'mlstm_w_up': mlstm_w_up, 'mlstm_conv_w': mlstm_conv_w,
        'mlstm_conv_b': mlstm_conv_b, 'mlstm_wq': mlstm_wq, 'mlstm_wk': mlstm_wk, 'mlstm_wv': mlstm_wv,
        'mlstm_w_gate': mlstm_w_gate, 'mlstm_b_gate': mlstm_b_gate, 'mlstm_norm_g': mlstm_norm_g,
        'mlstm_skip': mlstm_skip, 'mlstm_w_down': mlstm_w_down, 'ffn_w_up': ffn_w_up,
        'ffn_conv_w': ffn_conv_w, 'ffn_conv_b': ffn_conv_b, 'ffn_w_down': ffn_w_down,
    }
    assert norm_g.shape[0] == 2 and s5_A_re.shape[0] == 1 and mlstm_w_up.shape[0] == 1
    assert x_prompt.shape[0] == V7X_SUBLANES and x_sample.shape[0] == V7X_SUBLANES
    p = _prep_weights(w)
    out_p = _trunk(x_prompt.astype(F32), None, p, _pick_tiles(*x_prompt.shape[:2]))
    st = {'s5_re': state_s5_re[0], 's5_im': state_s5_im[0], 'mlstm_C': state_mlstm_C[0],
          'mlstm_n': state_mlstm_n[0], 'mlstm_m': state_mlstm_m[0], 'mlstm_conv': state_mlstm_conv[0],
          'ffn_conv': state_ffn_conv}
    out_s = _trunk(x_sample.astype(F32), st, p, _pick_tiles(*x_sample.shape[:2]))
    return (out_p[0], out_s[0]) + tuple(out_p[1:]) + tuple(out_s[1:])
```

```python
import functools
import math
from typing import NamedTuple

import jax
import jax.numpy as jnp
from jax import lax
from jax.experimental import pallas as pl
from jax.experimental.pallas import tpu as pltpu

F32 = jnp.float32
BF16 = jnp.bfloat16
HI = lax.Precision.HIGHEST

NORM_EPS = 1e-6
LN_EPS = 1e-5
S5_GROUP_CH = 16
S5_CHUNK = 16
QKV_BLOCK = 4

V7X_LANES = 128
V7X_SUBLANES = 8
V7X_MXU = 256
V7X_VMEM_BYTES = 64 * 1024 * 1024


class Tiles(NamedTuple):
    rows: int
    time: int
    s5_rows: int
    mlstm_chunk: int
    s5_time: int


def _pick_tiles(batch, seq):
    s5_time = (V7X_LANES // batch) * S5_CHUNK
    return Tiles(rows=min(batch * seq, 512), time=min(seq, 512),
                 s5_rows=min(batch * (seq // S5_CHUNK), 512), mlstm_chunk=min(seq, 256),
                 s5_time=s5_time if seq % s5_time == 0 else 0)


def _vmem_limit(block_bytes, scratch_bytes):
    want = 2 * block_bytes + scratch_bytes + 16 * 1024 * 1024
    return int(min(want, V7X_VMEM_BYTES - 8 * 1024 * 1024))


def _nbytes(shape, dtype):
    return math.prod(shape) * jnp.dtype(dtype).itemsize


def _rms(x, g):
    return x * lax.rsqrt(jnp.mean(x * x, axis=-1, keepdims=True) + NORM_EPS) * g


def _dot(a, b):
    return jnp.dot(a, b, preferred_element_type=F32)


def _dot_nt(a, b):
    return lax.dot_general(a, b, (((1,), (1,)), ((), ())), preferred_element_type=F32)


def _silu(x):
    return x * jax.nn.sigmoid(x)


def _log_sigmoid(x):
    return jnp.minimum(x, 0.0) - jnp.log1p(jnp.exp(-jnp.abs(x)))


def _norm_cast_kernel(x_ref, g_ref, o_ref):
    o_ref[...] = _rms(x_ref[...], g_ref[...]).astype(o_ref.dtype)


def _norm_cast(x2d, g, tb):
    rows, d = x2d.shape
    return pl.pallas_call(
        _norm_cast_kernel,
        grid=(rows // tb,),
        in_specs=[pl.BlockSpec((tb, d), lambda i: (i, 0)), pl.BlockSpec((1, d), lambda i: (0, 0))],
        out_specs=pl.BlockSpec((tb, d), lambda i: (i, 0)),
        out_shape=jax.ShapeDtypeStruct((rows, d), BF16),
        name="s5_norm",
    )(x2d, g)


def _s5_kernel(u_ref, t_ref, pre_ref, pim_ref, qre_ref, qim_ref, are_ref, aim_ref, h0re_ref, h0im_ref,
               y_ref, hre_ref, him_ref, pure_s, puim_s, hinre_s, hinim_s, stre_s, stim_s, *, nb, rb):
    r = pl.program_id(1)

    @pl.when(r == 0)
    def _():
        stre_s[...] = h0re_ref[0]
        stim_s[...] = h0im_ref[0]

    u = u_ref[...]
    pure_s[...] = _dot(u, pre_ref[0])
    puim_s[...] = _dot(u, pim_ref[0])
    ar = jnp.broadcast_to(are_ref[0], (nb, 2 * 64))
    ai = jnp.broadcast_to(aim_ref[0], (nb, 2 * 64))

    def step(i, carry):
        re, im = carry
        rows = pl.ds(pl.multiple_of(i * nb, nb), nb)
        hinre_s[rows, :] = re
        hinim_s[rows, :] = im
        return (ar * re - ai * im + pure_s[rows, :], ar * im + ai * re + puim_s[rows, :])

    re, im = lax.fori_loop(0, rb // nb, step, (stre_s[...], stim_s[...]))
    stre_s[...] = re
    stim_s[...] = im
    hre_ref[0] = re
    him_ref[0] = im

    carry_in = _dot(hinre_s[...].astype(BF16), qre_ref[0]) + _dot(hinim_s[...].astype(BF16), qim_ref[0])
    w = V7X_MXU
    for k in range(2):
        y_ref[:, k * w:(k + 1) * w] = (_dot(u[:, k * w:(k + 1) * w], t_ref[k])
                                       + carry_in[:, k * w:(k + 1) * w]).astype(y_ref.dtype)


def _s5_core(ut, mats, h0re, h0im, nb, rb):
    rows, cols = ut.shape
    pairs = cols // (2 * V7X_MXU)
    tm, pre, pim, qre, qim, are, aim = mats
    pair3 = lambda p, r: (p, 0, 0)
    kern = functools.partial(_s5_kernel, nb=nb, rb=rb)
    blocks = (_nbytes((rb, 512), BF16) * 2 + _nbytes((2, 256, 256), BF16) + 4 * _nbytes((512, 128), BF16))
    scratch = 4 * _nbytes((rb, 128), F32) + 2 * _nbytes((nb, 128), F32)
    return pl.pallas_call(
        kern,
        grid=(pairs, rows // rb),
        in_specs=[
            pl.BlockSpec((rb, 2 * V7X_MXU), lambda p, r: (r, p)),
            pl.BlockSpec((2, V7X_MXU, V7X_MXU), pair3),
            pl.BlockSpec((1, 2 * V7X_MXU, 128), pair3),
            pl.BlockSpec((1, 2 * V7X_MXU, 128), pair3),
            pl.BlockSpec((1, 128, 2 * V7X_MXU), pair3),
            pl.BlockSpec((1, 128, 2 * V7X_MXU), pair3),
            pl.BlockSpec((1, 1, 128), pair3),
            pl.BlockSpec((1, 1, 128), pair3),
            pl.BlockSpec((1, nb, 128), pair3),
            pl.BlockSpec((1, nb, 128), pair3),
        ],
        out_specs=[
            pl.BlockSpec((rb, 2 * V7X_MXU), lambda p, r: (r, p)),
            pl.BlockSpec((1, nb, 128), pair3),
            pl.BlockSpec((1, nb, 128), pair3),
        ],
        out_shape=[
            jax.ShapeDtypeStruct((rows, cols), BF16),
            jax.ShapeDtypeStruct((pairs, nb, 128), F32),
            jax.ShapeDtypeStruct((pairs, nb, 128), F32),
        ],
        scratch_shapes=[pltpu.VMEM((rb, 128), F32)] * 4 + [pltpu.VMEM((nb, 128), F32)] * 2,
        compiler_params=pltpu.CompilerParams(
            dimension_semantics=("arbitrary", "arbitrary"), vmem_limit_bytes=_vmem_limit(blocks, scratch)),
        name="s5_core",
    )(ut, tm, pre, pim, qre, qim, are, aim, h0re, h0im)


def _s5_fused_kernel(x_ref, g_ref, tt_ref, pt_ref, qt_ref, are_ref, aim_ref, h0re_ref, h0im_ref,
                     y_ref, hre_ref, him_ref,
                     us, ugt, pure, puim, hinre, hinim, ytmp, ys, stre, stim, *, nb, tbt):
    lc = S5_CHUNK
    ch = S5_GROUP_CH
    gps = V7X_LANES // ch
    nch = tbt // lc
    cols = nb * nch
    k = pl.program_id(1)

    @pl.when(k == 0)
    def _():
        for b in range(nb):
            u = _rms(x_ref[b], g_ref[...])
            for kk in range(us.shape[0]):
                us[kk, b * tbt:(b + 1) * tbt, :] = u[:, kk * V7X_LANES:(kk + 1) * V7X_LANES]

    @pl.when(pl.program_id(0) == 0)
    def _():
        stre[k] = h0re_ref[...]
        stim[k] = h0im_ref[...]

    for s in range(lc):
        a = us[k, pl.ds(s, cols, stride=lc), :].T
        for g in range(gps):
            ugt[g, s * ch:(s + 1) * ch, :] = a[g * ch:(g + 1) * ch, :].astype(BF16)

    half = pt_ref.shape[1] // 2
    for pr in range(gps // 2):
        put0 = _dot(pt_ref[2 * pr], ugt[2 * pr])
        put1 = _dot(pt_ref[2 * pr + 1], ugt[2 * pr + 1])
        pure[pr] = jnp.concatenate([put0[:half], put1[:half]], axis=0).T
        puim[pr] = jnp.concatenate([put0[half:], put1[half:]], axis=0).T

    for pr in range(gps // 2):
        re = stre[k, pr]
        im = stim[k, pr]
        ar = jnp.broadcast_to(are_ref[pr], re.shape)
        ai = jnp.broadcast_to(aim_ref[pr], re.shape)
        for j in range(nch):
            rows = pl.ds(j, nb, stride=nch)
            hinre[pr, rows, :] = re
            hinim[pr, rows, :] = im
            re, im = (ar * re - ai * im + pure[pr, rows, :], ar * im + ai * re + puim[pr, rows, :])
        stre[k, pr] = re
        stim[k, pr] = im
        hre_ref[k * (gps // 2) + pr] = re
        him_ref[k * (gps // 2) + pr] = im

    for pr in range(gps // 2):
        hre_t = hinre[pr].T
        him_t = hinim[pr].T
        for i in range(2):
            g = 2 * pr + i
            hin_t = jnp.concatenate([hre_t[i * half:(i + 1) * half], him_t[i * half:(i + 1) * half]],
                                    axis=0).astype(BF16)
            ytmp[g] = _dot(tt_ref[g], ugt[g]) + _dot(qt_ref[g], hin_t)

    for t in range(lc):
        z = jnp.concatenate([ytmp[g, t * ch:(t + 1) * ch, :] for g in range(gps)], axis=0)
        ys[pl.ds(t, cols, stride=lc), :] = z.T
    for b in range(nb):
        y_ref[b] = ys[b * tbt:(b + 1) * tbt, :].astype(y_ref.dtype)


def _s5_fused(x, g, mats_t, h0re, h0im, tbt):
    nb, s, d = x.shape
    tt, pt, qt, are, aim = mats_t
    groups = tt.shape[0]
    gps = V7X_LANES // S5_GROUP_CH
    slabs = groups // gps
    w = S5_CHUNK * S5_GROUP_CH
    nst = pt.shape[1]
    cols = nb * (tbt // S5_CHUNK)
    kern = functools.partial(_s5_fused_kernel, nb=nb, tbt=tbt)
    slab3 = lambda t, k: (k, 0, 0)
    blocks = (_nbytes((nb, tbt, d), F32) + _nbytes((nb, tbt, V7X_LANES), BF16)
              + gps * (_nbytes((w, w), BF16) + 2 * _nbytes((w, nst), BF16)))
    scratch = (_nbytes((slabs, nb * tbt, V7X_LANES), F32) + _nbytes((gps, w, cols), BF16)
               + 4 * _nbytes((gps // 2, cols, nst), F32) + _nbytes((gps, w, cols), F32)
               + _nbytes((nb * tbt, V7X_LANES), F32))
    return pl.pallas_call(
        kern,
        grid=(s // tbt, slabs),
        in_specs=[
            pl.BlockSpec((nb, tbt, d), lambda t, k: (0, t, 0)),
            pl.BlockSpec((1, d), lambda t, k: (0, 0)),
            pl.BlockSpec((gps, w, w), slab3),
            pl.BlockSpec((gps, nst, w), slab3),
            pl.BlockSpec((gps, w, nst), slab3),
            pl.BlockSpec((gps // 2, 1, nst), slab3),
            pl.BlockSpec((gps // 2, 1, nst), slab3),
            pl.BlockSpec((gps // 2, nb, nst), slab3),
            pl.BlockSpec((gps // 2, nb, nst), slab3),
        ],
        out_specs=[
            pl.BlockSpec((nb, tbt, V7X_LANES), lambda t, k: (0, t, k)),
            pl.BlockSpec((groups // 2, nb, nst), lambda t, k: (0, 0, 0)),
            pl.BlockSpec((groups // 2, nb, nst), lambda t, k: (0, 0, 0)),
        ],
        out_shape=[
            jax.ShapeDtypeStruct((nb, s, d), BF16),
            jax.ShapeDtypeStruct((groups // 2, nb, nst), F32),
            jax.ShapeDtypeStruct((groups // 2, nb, nst), F32),
        ],
        scratch_shapes=[
            pltpu.VMEM((slabs, nb * tbt, V7X_LANES), F32),
            pltpu.VMEM((gps, w, cols), BF16),
            pltpu.VMEM((gps // 2, cols, nst), F32),
            pltpu.VMEM((gps // 2, cols, nst), F32),
            pltpu.VMEM((gps // 2, cols, nst), F32),
            pltpu.VMEM((gps // 2, cols, nst), F32),
            pltpu.VMEM((gps, w, cols), F32),
            pltpu.VMEM((nb * tbt, V7X_LANES), F32),
            pltpu.VMEM((slabs, gps // 2, nb, nst), F32),
            pltpu.VMEM((slabs, gps // 2, nb, nst), F32),
        ],
        compiler_params=pltpu.CompilerParams(
            dimension_semantics=("arbitrary", "arbitrary"), vmem_limit_bytes=_vmem_limit(blocks, scratch)),
        name="s5_fused",
    )(x, g, tt, pt, qt, are, aim, h0re, h0im)


def _s5_matrices(a_re, a_im, log_dt, b_re, b_im, c_re, c_im):
    lc = S5_CHUNK
    groups, n = a_re.shape
    ch = S5_GROUP_CH
    a = lax.complex(a_re.astype(F32), a_im.astype(F32))
    adt = a * jnp.exp(log_dt.astype(F32))[:, None]
    j = jnp.arange(lc + 1, dtype=F32)
    apow = jnp.exp(adt[None] * j[:, None, None])
    b_bar = ((apow[1] - 1.0) / a)[..., None] * lax.complex(b_re.astype(F32), b_im.astype(F32))
    c_mat = lax.complex(c_re.astype(F32), c_im.astype(F32))
    ca = c_mat[None] * apow[:, :, None, :]
    kern = (jnp.einsum('jgcn,gnd->jgcd', ca.real, b_bar.real, precision=HI)
            - jnp.einsum('jgcn,gnd->jgcd', ca.imag, b_bar.imag, precision=HI))
    s_idx = jnp.arange(lc)[:, None]
    t_idx = jnp.arange(lc)[None, :]
    lag = t_idx - s_idx
    kt = kern[jnp.clip(lag, 0, lc)]
    kt = jnp.where((lag >= 0)[:, :, None, None, None], kt, 0.0)
    tm = jnp.transpose(kt, (2, 0, 4, 1, 3)).reshape(groups, lc * ch, lc * ch)
    pw = apow[lc - 1 - jnp.arange(lc)]
    p = pw[:, :, :, None] * b_bar[None]
    p = jnp.transpose(p, (1, 0, 3, 2)).reshape(groups, lc * ch, n)
    q = ca[1:]
    q = jnp.transpose(q, (1, 3, 0, 2)).reshape(groups, n, lc * ch)

    def pair_rows(m):
        g2 = groups // 2
        m = m.reshape(g2, 2, m.shape[1], m.shape[2])
        z = jnp.zeros_like(m[:, 0])
        top = jnp.concatenate([m[:, 0], z], axis=2)
        bot = jnp.concatenate([z, m[:, 1]], axis=2)
        return jnp.concatenate([top, bot], axis=1)

    pre = pair_rows(p.real).astype(BF16)
    pim = pair_rows(p.imag).astype(BF16)
    qre = pair_rows(q.real).astype(BF16)
    qim = pair_rows(-q.imag).astype(BF16)
    a_chunk = apow[lc].reshape(groups // 2, 1, 2 * n)
    are, aim = a_chunk.real, a_chunk.imag
    tt = jnp.swapaxes(tm, 1, 2).astype(BF16)
    pt = jnp.concatenate([jnp.swapaxes(p.real, 1, 2), jnp.swapaxes(p.imag, 1, 2)], axis=1).astype(BF16)
    qt = jnp.concatenate([jnp.swapaxes(q.real, 1, 2), jnp.swapaxes(-q.imag, 1, 2)], axis=2).astype(BF16)
    return {'row': (tm.astype(BF16), pre, pim, qre, qim, are, aim), 'col': (tt, pt, qt, are, aim)}


def _glu_kernel(x_ref, y_ref, g0_ref, d_ref, w_ref, g1_ref, o_ref):
    x = x_ref[...]
    d = x.shape[1]
    u = _rms(x, g0_ref[...])
    yy = y_ref[...].astype(F32) + d_ref[...] * u
    z = jax.nn.gelu(yy, approximate=True).astype(BF16)
    ag = _dot(z, w_ref[...])
    out = ag[:, :d] * jax.nn.sigmoid(ag[:, d:])
    o_ref[...] = x + _rms(out, g1_ref[...])


def _const_spec(shape):
    nd = len(shape)
    return pl.BlockSpec(shape, lambda *_: (0,) * nd, pipeline_mode=pl.Buffered(1))


def _glu(x2d, y2d, g0, dskip, w, g1, tb):
    rows, d = x2d.shape
    blocks = 2 * _nbytes((tb, d), F32) + _nbytes((tb, d), BF16)
    scratch = _nbytes(w.shape, BF16) + 3 * _nbytes((tb, 2 * d), F32)
    row = pl.BlockSpec((tb, d), lambda i: (i, 0))
    return pl.pallas_call(
        _glu_kernel,
        grid=(rows // tb,),
        in_specs=[row, row, _const_spec((1, d)), _const_spec((1, d)), _const_spec(w.shape), _const_spec((1, d))],
        out_specs=row,
        out_shape=jax.ShapeDtypeStruct((rows, d), F32),
        compiler_params=pltpu.CompilerParams(
            dimension_semantics=("arbitrary",), vmem_limit_bytes=_vmem_limit(blocks, scratch)),
        name="s5_glu",
    )(x2d, y2d, g0, dskip, w, g1)


def _ffn_kernel(x_ref, buf_ref, g2_ref, wup_ref, cw_ref, cb_ref, wdown_ref, g3_ref,
                o_ref, nbuf_ref, ubuf, carry, act, *, tb, dff, width):
    t = pl.program_id(1)
    pad = V7X_SUBLANES

    @pl.when(t == 0)
    def _():
        carry[...] = jnp.zeros_like(carry)
        carry[pad - (width - 1):pad, :] = buf_ref[0]

    x = x_ref[0]
    hn = _rms(x, g2_ref[...]).astype(BF16)
    fb = V7X_MXU

    def conv(cs):
        u = _dot(hn, wup_ref[:, cs])
        ubuf[0:pad, :] = carry[:, cs]
        ubuf[pad:pad + tb, :] = u
        carry[:, cs] = ubuf[tb:tb + pad, :]
        acc = cb_ref[:, cs] + u * cw_ref[width - 1:width, cs]
        for j in range(width - 1):
            off = pad - (width - 1) + j
            acc = acc + ubuf[off:off + tb, :] * cw_ref[j:j + 1, cs]
        return acc

    for j in range(dff // fb):
        gate = conv(slice(j * fb, (j + 1) * fb))
        val = conv(slice(dff + j * fb, dff + (j + 1) * fb))
        act[:, j * fb:(j + 1) * fb] = (jax.nn.gelu(gate, approximate=True) * val).astype(BF16)

    f = _dot(act[...], wdown_ref[...])
    o_ref[0] = x + _rms(f, g3_ref[...])
    nbuf_ref[0] = carry[pad - (width - 1):pad, :]


def _ffn(x, buf, g2, wup, cw, cb, wdown, g3, tb):
    b, s, d = x.shape
    dff = wdown.shape[0]
    width = cw.shape[0]
    kern = functools.partial(_ffn_kernel, tb=tb, dff=dff, width=width)
    blocks = 2 * _nbytes((tb, d), F32)
    scratch = (_nbytes(wup.shape, BF16) + _nbytes(wdown.shape, BF16) + _nbytes((tb + 8, V7X_MXU), F32)
               + _nbytes((8, 2 * dff), F32) + _nbytes((tb, dff), BF16) + 2 * _nbytes((tb, d), F32))
    xs = pl.BlockSpec((1, tb, d), lambda i, t: (i, t, 0))
    bs = pl.BlockSpec((1, width - 1, 2 * dff), lambda i, t: (i, 0, 0))
    return pl.pallas_call(
        kern,
        grid=(b, s // tb),
        in_specs=[xs, bs, _const_spec((1, d)), _const_spec(wup.shape), _const_spec(cw.shape),
                  _const_spec((1, 2 * dff)), _const_spec(wdown.shape), _const_spec((1, d))],
        out_specs=[xs, bs],
        out_shape=[jax.ShapeDtypeStruct((b, s, d), F32), jax.ShapeDtypeStruct((b, width - 1, 2 * dff), F32)],
        scratch_shapes=[pltpu.VMEM((tb + V7X_SUBLANES, V7X_MXU), F32), pltpu.VMEM((V7X_SUBLANES, 2 * dff), F32),
                        pltpu.VMEM((tb, dff), BF16)],
        compiler_params=pltpu.CompilerParams(
            dimension_semantics=("arbitrary", "arbitrary"), vmem_limit_bytes=_vmem_limit(blocks, scratch)),
        name="conv_ffn",
    )(x, buf, g2, wup, cw, cb, wdown, g3)


def _gate_fold_kernel(wq_ref, wk_ref, wv_ref, gq_ref, gk_ref, gv_ref, gc_ref, gm_ref):
    dotp = functools.partial(jnp.dot, preferred_element_type=F32, precision=HI)
    gc_ref[...] = dotp(wq_ref[0], gq_ref[...]) + dotp(wk_ref[0], gk_ref[...])
    gm_ref[...] = dotp(wv_ref[0], gv_ref[...])


def _gate_fold(wq_t, wk_t, wv_t, w_gate):
    nt, w, _ = wq_t.shape
    inner = nt * w
    ng = w_gate.shape[1]
    tile = pl.BlockSpec((1, w, w), lambda i: (i, 0, 0))
    gs = [pl.BlockSpec((w, ng), lambda i, k=k: (k * nt + i, 0)) for k in range(3)]
    out = pl.BlockSpec((w, ng), lambda i: (i, 0))
    return pl.pallas_call(
        _gate_fold_kernel,
        grid=(nt,),
        in_specs=[tile, tile, tile] + gs,
        out_specs=[out, out],
        out_shape=[jax.ShapeDtypeStruct((inner, ng), F32)] * 2,
        name="mlstm_gate_fold",
    )(wq_t, wk_t, wv_t, w_gate, w_gate, w_gate)


def _mpre_kernel(x_ref, cbuf_ref, g_ref, wup_ref, cw_ref, cb_ref, wq_ref, wkt_ref, wv_ref, gc_ref, gm_ref,
                 bg_ref, q_ref, kt_ref, v_ref, xc_ref, z_ref, gates_ref, nbuf_ref, ubuf, carry,
                 *, tb, inner, width):
    t = pl.program_id(1)
    pad = V7X_SUBLANES

    @pl.when(t == 0)
    def _():
        carry[...] = jnp.zeros_like(carry)
        carry[pad - (width - 1):pad, :] = cbuf_ref[0]

    h = _rms(x_ref[0], g_ref[...]).astype(BF16)
    w = V7X_MXU
    gacc = jnp.zeros((tb, bg_ref.shape[1]), F32) + bg_ref[...]
    for i in range(inner // w):
        cs = slice(i * w, (i + 1) * w)
        xm = _dot(h, wup_ref[:, cs])
        ubuf[0:pad, :] = carry[:, cs]
        ubuf[pad:pad + tb, :] = xm
        carry[:, cs] = ubuf[tb:tb + pad, :]
        pre = cb_ref[:, cs] + xm * cw_ref[width - 1:width, cs]
        for j in range(width - 1):
            off = pad - (width - 1) + j
            pre = pre + ubuf[off:off + tb, :] * cw_ref[j:j + 1, cs]
        xc = _silu(pre).astype(BF16)
        xmb = xm.astype(BF16)
        q_ref[0, :, cs] = _dot(xc, wq_ref[i]).astype(BF16)
        kt_ref[0, cs, :] = _dot_nt(wkt_ref[i], xc).astype(BF16)
        v_ref[0, :, cs] = _dot(xmb, wv_ref[i]).astype(BF16)
        xc_ref[0, :, cs] = xc
        z_ref[0, :, cs] = _silu(_dot(h, wup_ref[:, inner + i * w:inner + (i + 1) * w])).astype(BF16)
        gacc = gacc + _dot(xc, gc_ref[cs, :]) + _dot(xmb, gm_ref[cs, :])
    gates_ref[0] = gacc
    nbuf_ref[0] = carry[pad - (width - 1):pad, :]


def _mpre(x, cbuf, g, wup, cw, cb, wq_t, wkt_t, wv_t, gc, gm, bg, tb):
    b, s, d = x.shape
    inner = cw.shape[1]
    width = cw.shape[0]
    ng = bg.shape[1]
    kern = functools.partial(_mpre_kernel, tb=tb, inner=inner, width=width)
    act = pl.BlockSpec((1, tb, inner), lambda i, t: (i, t, 0))
    blocks = _nbytes((tb, d), F32) + 5 * _nbytes((tb, inner), BF16)
    scratch = (_nbytes(wup.shape, BF16) + 3 * _nbytes(wq_t.shape, BF16) + 2 * _nbytes((inner, 128), BF16)
               + _nbytes((tb + 8, V7X_MXU), F32) + _nbytes((8, inner), F32) + 4 * _nbytes((tb, V7X_MXU), F32))
    return pl.pallas_call(
        kern,
        grid=(b, s // tb),
        in_specs=[pl.BlockSpec((1, tb, d), lambda i, t: (i, t, 0)),
                  pl.BlockSpec((1, width - 1, inner), lambda i, t: (i, 0, 0)),
                  _const_spec((1, d)), _const_spec(wup.shape), _const_spec(cw.shape), _const_spec((1, inner)),
                  _const_spec(wq_t.shape), _const_spec(wkt_t.shape), _const_spec(wv_t.shape),
                  _const_spec(gc.shape), _const_spec(gm.shape), _const_spec((1, ng))],
        out_specs=[act, pl.BlockSpec((1, inner, tb), lambda i, t: (i, 0, t)), act, act, act,
                   pl.BlockSpec((1, tb, ng), lambda i, t: (i, t, 0)),
                   pl.BlockSpec((1, width - 1, inner), lambda i, t: (i, 0, 0))],
        out_shape=[jax.ShapeDtypeStruct((b, s, inner), BF16), jax.ShapeDtypeStruct((b, inner, s), BF16),
                   jax.ShapeDtypeStruct((b, s, inner), BF16), jax.ShapeDtypeStruct((b, s, inner), BF16),
                   jax.ShapeDtypeStruct((b, s, inner), BF16), jax.ShapeDtypeStruct((b, s, ng), F32),
                   jax.ShapeDtypeStruct((b, width - 1, inner), F32)],
        scratch_shapes=[pltpu.VMEM((tb + V7X_SUBLANES, V7X_MXU), F32), pltpu.VMEM((V7X_SUBLANES, inner), F32)],
        compiler_params=pltpu.CompilerParams(
            dimension_semantics=("arbitrary", "arbitrary"), vmem_limit_bytes=_vmem_limit(blocks, scratch)),
        name="mlstm_pre",
    )(x, cbuf, g, wup, cw, cb, wq_t, wkt_t, wv_t, gc, gm, bg)


def _mlstm_kernel(*refs, lc, heads, dh, zero_init):
    if zero_init:
        (q_ref, kt_ref, v_ref, g_ref, gt_ref,
         hn_ref, cout_ref, nout_ref, mout_ref, ct_s, n_s, m_s) = refs
    else:
        (q_ref, kt_ref, v_ref, g_ref, gt_ref, c0_ref, n0_ref, m0_ref,
         hn_ref, cout_ref, nout_ref, mout_ref, ct_s, n_s, m_s) = refs
    c = pl.program_id(1)

    @pl.when(c == 0)
    def _():
        if zero_init:
            ct_s[...] = jnp.zeros_like(ct_s)
            n_s[...] = jnp.zeros_like(n_s)
            m_s[...] = jnp.zeros_like(m_s)
        else:
            for h in range(heads):
                ct_s[h] = c0_ref[0, h].T
            n_s[...] = n0_ref[0]
            m_s[...] = m0_ref[0]

    g = g_ref[0]
    gt = gt_ref[0]
    row = lax.broadcasted_iota(jnp.int32, (lc, lc), 0)
    col = lax.broadcasted_iota(jnp.int32, (lc, lc), 1)
    causal = col <= row
    for h in range(heads):
        hs = slice(h * dh, (h + 1) * dh)
        i_col = g[:, h:h + 1]
        f_col = _log_sigmoid(g[:, heads + h:heads + h + 1])
        i_row = gt[h:h + 1, :]
        f_row = _log_sigmoid(gt[heads + h:heads + h + 1, :])
        b_col = jnp.sum(jnp.where(causal, f_row, 0.0), axis=1, keepdims=True)
        b_row = jnp.sum(jnp.where(row <= col, f_col, 0.0), axis=0, keepdims=True)
        m_prev = m_s[h:h + 1, 0:1]
        dmat = jnp.where(causal, b_col - b_row + i_row, -jnp.inf)
        inter = b_col + m_prev
        m_t = jnp.maximum(inter, jnp.max(dmat, axis=1, keepdims=True))
        wts = jnp.exp(dmat - m_t)
        qh = q_ref[0, :, hs]
        kth = kt_ref[0, hs, :]
        vh = v_ref[0, :, hs]
        sc = _dot(qh, kth) * wts
        a = jnp.exp(inter - m_t)
        ct = ct_s[h]
        n_row = n_s[h:h + 1, :]
        num = _dot(sc.astype(BF16), vh) + a * _dot(qh, ct.astype(BF16))
        qn = jnp.sum(qh.astype(F32) * n_row, axis=1, keepdims=True)
        den = jnp.sum(sc, axis=1, keepdims=True) + a * qn
        hh = num / jnp.maximum(jnp.abs(den), jnp.exp(-m_t))
        mu = jnp.mean(hh, axis=1, keepdims=True)
        dev = hh - mu
        var = jnp.mean(dev * dev, axis=1, keepdims=True)
        hn_ref[0, :, hs] = (dev * lax.rsqrt(var + LN_EPS)).astype(hn_ref.dtype)

        b_last = b_row[:, lc - 1:lc]
        g_row = b_last - b_row + i_row
        m_new = jnp.maximum(b_last + m_prev, jnp.max(g_row, axis=1, keepdims=True))
        decay = jnp.exp(b_last + m_prev - m_new)
        wg_row = jnp.exp(g_row - m_new)
        wg_col = jnp.exp(b_last - b_col + i_col - m_new)
        wv = (vh.astype(F32) * wg_col).astype(BF16)
        ct_s[h] = decay * ct + _dot(kth, wv)
        wg16 = jnp.broadcast_to(wg_row, (16, lc)).astype(BF16)
        n_s[h:h + 1, :] = decay * n_row + _dot_nt(wg16, kth)[0:1, :]
        m_s[h:h + 1, :] = jnp.broadcast_to(m_new, (1, V7X_LANES))

    @pl.when(c == pl.num_programs(1) - 1)
    def _():
        for h in range(heads):
            cout_ref[0, h] = ct_s[h].T
        nout_ref[0] = n_s[...]
        mout_ref[0] = m_s[...]


def _mlstm_core(q, kt, v, gates, gates_t, state, lc, heads):
    b, s, inner = q.shape
    dh = inner // heads
    ng = gates.shape[2]
    zero_init = state is None
    kern = functools.partial(_mlstm_kernel, lc=lc, heads=heads, dh=dh, zero_init=zero_init)
    act = pl.BlockSpec((1, lc, inner), lambda i, c: (i, c, 0))
    cs = pl.BlockSpec((1, heads, dh, dh), lambda i, c: (i, 0, 0, 0))
    ns = pl.BlockSpec((1, heads, dh), lambda i, c: (i, 0, 0))
    ms = pl.BlockSpec((1, heads, V7X_LANES), lambda i, c: (i, 0, 0))
    in_specs = [act, pl.BlockSpec((1, inner, lc), lambda i, c: (i, 0, c)), act,
                pl.BlockSpec((1, lc, ng), lambda i, c: (i, c, 0)),
                pl.BlockSpec((1, ng, lc), lambda i, c: (i, 0, c))]
    args = [q, kt, v, gates, gates_t]
    if not zero_init:
        in_specs += [cs, ns, ms]
        args += list(state)
    blocks = 4 * _nbytes((lc, inner), BF16) + (1 if zero_init else 2) * _nbytes((heads, dh, dh), F32)
    scratch = _nbytes((heads, dh, dh), F32) + 6 * _nbytes((lc, max(lc, dh)), F32) + 2 * _nbytes((dh, dh), F32)
    return pl.pallas_call(
        kern,
        grid=(b, s // lc),
        in_specs=in_specs,
        out_specs=[act, cs, ns, ms],
        out_shape=[jax.ShapeDtypeStruct((b, s, inner), BF16), jax.ShapeDtypeStruct((b, heads, dh, dh), F32),
                   jax.ShapeDtypeStruct((b, heads, dh), F32), jax.ShapeDtypeStruct((b, heads, V7X_LANES), F32)],
        scratch_shapes=[pltpu.VMEM((heads, dh, dh), F32), pltpu.VMEM((heads, dh), F32),
                        pltpu.VMEM((heads, V7X_LANES), F32)],
        compiler_params=pltpu.CompilerParams(
            dimension_semantics=("arbitrary", "arbitrary"), vmem_limit_bytes=_vmem_limit(blocks, scratch)),
        name="mlstm_core",
    )(*args)


def _mout_kernel(hn_ref, xc_ref, z_ref, x_ref, ng_ref, sk_ref, w_ref, g1_ref, o_ref):
    hs = hn_ref[...].astype(F32) * ng_ref[...] + sk_ref[...] * xc_ref[...].astype(F32)
    a = (hs * z_ref[...].astype(F32)).astype(BF16)
    x = x_ref[...]
    o_ref[...] = x + _rms(_dot(a, w_ref[...]), g1_ref[...])


def _mout(hn, xc, sz, x2d, ng, sk, w, g1, tb):
    rows, d = x2d.shape
    inner = hn.shape[1]
    act = pl.BlockSpec((tb, inner), lambda i: (i, 0))
    row = pl.BlockSpec((tb, d), lambda i: (i, 0))
    blocks = 3 * _nbytes((tb, inner), BF16) + 2 * _nbytes((tb, d), F32)
    scratch = _nbytes(w.shape, BF16) + 3 * _nbytes((tb, inner), F32)
    return pl.pallas_call(
        _mout_kernel,
        grid=(rows // tb,),
        in_specs=[act, act, act, row, _const_spec((1, inner)), _const_spec((1, inner)), _const_spec(w.shape),
                  _const_spec((1, d))],
        out_specs=row,
        out_shape=jax.ShapeDtypeStruct((rows, d), F32),
        compiler_params=pltpu.CompilerParams(
            dimension_semantics=("arbitrary",), vmem_limit_bytes=_vmem_limit(blocks, scratch)),
        name="mlstm_out",
    )(hn, xc, sz, x2d, ng, sk, w, g1)


def _block_diag_tiles(w):
    nblk, k, _ = w.shape
    per = V7X_MXU // k
    wt = w.reshape(nblk // per, per, k, k)
    eye = jnp.eye(per, dtype=w.dtype)
    full = wt[:, :, :, None, :] * eye[None, :, None, :, None]
    return full.reshape(nblk // per, per * k, per * k)


def _prep_weights(w):
    row = lambda v: v.astype(F32).reshape(1, -1)
    p = {}
    p['norm_g'] = w['norm_g'].astype(F32)
    p['s5_mats'] = _s5_matrices(w['s5_A_re'][0], w['s5_A_im'][0], w['s5_log_dt'][0], w['s5_B_re'][0],
                                w['s5_B_im'][0], w['s5_C_re'][0], w['s5_C_im'][0])
    p['s5_D'] = row(w['s5_D'][0])
    p['s5_w_glu'] = w['s5_w_glu'][0].astype(BF16)
    inner = w['mlstm_conv_w'].shape[2]
    heads = w['mlstm_b_gate'].shape[1] // 2
    dh = inner // heads
    wq_t = _block_diag_tiles(w['mlstm_wq'][0].astype(F32))
    wk_t = _block_diag_tiles(w['mlstm_wk'][0].astype(F32))
    wv_t = _block_diag_tiles(w['mlstm_wv'][0].astype(F32))
    gc, gm = _gate_fold(wq_t, wk_t, wv_t, w['mlstm_w_gate'][0].astype(F32))
    p['m_wq'] = wq_t.astype(BF16)
    p['m_wkt'] = (jnp.swapaxes(wk_t, 1, 2) * (dh ** -0.5)).astype(BF16)
    p['m_wv'] = wv_t.astype(BF16)
    p['m_gc'] = gc.astype(BF16)
    p['m_gm'] = gm.astype(BF16)
    p['m_bg'] = row(w['mlstm_b_gate'][0])
    p['m_wup'] = w['mlstm_w_up'][0].astype(BF16)
    p['m_cw'] = w['mlstm_conv_w'][0].astype(F32)
    p['m_cb'] = row(w['mlstm_conv_b'][0])
    p['m_ng'] = row(w['mlstm_norm_g'][0])
    p['m_skip'] = row(w['mlstm_skip'][0])
    p['m_wdown'] = w['mlstm_w_down'][0].astype(BF16)
    p['heads'] = heads
    p['f_wup'] = w['ffn_w_up'].astype(BF16)
    p['f_cw'] = w['ffn_conv_w'].astype(F32)
    p['f_cb'] = w['ffn_conv_b'].astype(F32)
    p['f_wdown'] = w['ffn_w_down'].astype(BF16)
    return p


def _to_chunk_major(u2d, b, s):
    d = u2d.shape[1]
    g = d // S5_GROUP_CH
    u = u2d.reshape(b, s // S5_CHUNK, S5_CHUNK, g, S5_GROUP_CH)
    return jnp.transpose(u, (1, 0, 3, 2, 4)).reshape((s // S5_CHUNK) * b, d * S5_CHUNK)


def _to_token_major(yt, b, s):
    d = yt.shape[1] // S5_CHUNK
    g = d // S5_GROUP_CH
    y = yt.reshape(s // S5_CHUNK, b, g, S5_CHUNK, S5_GROUP_CH)
    return jnp.transpose(y, (1, 0, 3, 2, 4)).reshape(b * s, d)


def _trunk(x, st, p, tiles):
    b, s, d = x.shape
    ng = p['norm_g']
    g_of = lambda layer, k: ng[layer, k].reshape(1, d)
    zero_init = st is None
    groups = d // S5_GROUP_CH
    x2d = x.reshape(b * s, d)

    n_state = p['s5_mats']['row'][1].shape[2] // 2
    if zero_init:
        h0re = h0im = jnp.zeros((groups // 2, b, 2 * n_state), F32)
    else:
        pair = lambda h: jnp.transpose(h.astype(F32).reshape(b, groups // 2, 2 * n_state), (1, 0, 2))
        h0re, h0im = pair(st['s5_re']), pair(st['s5_im'])
    if tiles.s5_time:
        y, hre, him = _s5_fused(x, g_of(0, 0), p['s5_mats']['col'], h0re, h0im, tiles.s5_time)
        y2d = y.reshape(b * s, d)
    else:
        u = _norm_cast(x2d, g_of(0, 0), tiles.rows)
        yt, hre, him = _s5_core(_to_chunk_major(u, b, s), p['s5_mats']['row'], h0re, h0im, b, tiles.s5_rows)
        y2d = _to_token_major(yt, b, s)
    unpair = lambda h: jnp.transpose(h, (1, 0, 2)).reshape(1, b, groups, n_state)
    o_re, o_im = unpair(hre), unpair(him)
    x2d = _glu(x2d, y2d, g_of(0, 0), p['s5_D'], p['s5_w_glu'], g_of(0, 1), tiles.rows)

    def ffn(x2d, layer):
        dff2 = p['f_wup'].shape[2]
        width = p['f_cw'].shape[1]
        buf = jnp.zeros((b, width - 1, dff2), F32) if zero_init else st['ffn_conv'][layer].astype(F32)
        xo, nbuf = _ffn(x2d.reshape(b, s, d), buf, g_of(layer, 2), p['f_wup'][layer], p['f_cw'][layer],
                        p['f_cb'][layer].reshape(1, dff2), p['f_wdown'][layer], g_of(layer, 3), tiles.time)
        return xo.reshape(b * s, d), nbuf

    x2d, fbuf0 = ffn(x2d, 0)

    heads = p['heads']
    inner = p['m_cw'].shape[1]
    width = p['m_cw'].shape[0]
    cbuf = jnp.zeros((b, width - 1, inner), F32) if zero_init else st['mlstm_conv'].astype(F32)
    q, kt, v, xc, sz, gates, ncbuf = _mpre(
        x2d.reshape(b, s, d), cbuf, g_of(1, 0), p['m_wup'], p['m_cw'], p['m_cb'], p['m_wq'], p['m_wkt'],
        p['m_wv'], p['m_gc'], p['m_gm'], p['m_bg'], tiles.time)
    if zero_init:
        state = None
    else:
        state = (st['mlstm_C'].astype(F32), st['mlstm_n'].astype(F32),
                 jnp.broadcast_to(st['mlstm_m'].astype(F32)[:, :, None], (b, heads, V7X_LANES)))
    hn, c_out, n_out, m_out = _mlstm_core(q, kt, v, gates, jnp.swapaxes(gates, 1, 2), state,
                                          tiles.mlstm_chunk, heads)
    x2d = _mout(hn.reshape(b * s, inner), xc.reshape(b * s, inner), sz.reshape(b * s, inner), x2d,
                p['m_ng'], p['m_skip'], p['m_wdown'], g_of(1, 1), tiles.rows)
    x2d, fbuf1 = ffn(x2d, 1)

    return (x2d.reshape(b, s, d), o_re, o_im, c_out[None], n_out[None], m_out[None, :, :, 0], ncbuf[None],
            jnp.stack([fbuf0, fbuf1], axis=0))


def kernel(x_prompt, x_sample, state_s5_re, state_s5_im, state_mlstm_C, state_mlstm_n, state_mlstm_m, state_mlstm_conv, state_ffn_conv, norm_g, s5_A_re, s5_A_im, s5_log_dt, s5_B_re, s5_B_im, s5_C_re, s5_C_im, s5_D, s5_w_glu, mlstm_w_up, mlstm_conv_w, mlstm_conv_b, mlstm_wq, mlstm_wk, mlstm_wv, mlstm_w_gate, mlstm_b_gate, mlstm_norm_g, mlstm_skip, mlstm_w_down, ffn_w_up, ffn_conv_w, ffn_conv_b, ffn_w_down):
    w = {
        'norm_g': norm_g, 's5_A_re': s5_A_re, 's5_A_im': s5_A_im, 's5_log_dt': s5_log_dt,
        's5_B_re': s5_B_re, 's5_B_im': s5_B_im, 's5_C_re': s5_C_re, 's5_C_im': s5_C_im, 's5_D': s5_D,
        's5_w_glu': s5_w_glu, 'mlstm_w_up': mlstm_w_up, 'mlstm_conv_w': mlstm_conv_w,
        'mlstm_conv_b': mlstm_conv_b, 'mlstm_wq': mlstm_wq, 'mlstm_wk': mlstm_wk, 'mlstm_wv': mlstm_wv,
        'mlstm_w_gate': mlstm_w_gate, 'mlstm_b_gate': mlstm_b_gate, 'mlstm_norm_g': mlstm_norm_g,
        'mlstm_skip': mlstm_skip, 'mlstm_w_down': mlstm_w_down, 'ffn_w_up': ffn_w_up,
        'ffn_conv_w': ffn_conv_w, 'ffn_conv_b': ffn_conv_b, 'ffn_w_down': ffn_w_down,
    }
    assert norm_g.shape[0] == 2 and s5_A_re.shape[0] == 1 and mlstm_w_up.shape[0] == 1
    assert x_prompt.shape[0] == V7X_SUBLANES and x_sample.shape[0] == V7X_SUBLANES
    p = _prep_weights(w)
    out_p = _trunk(x_prompt.astype(F32), None, p, _pick_tiles(*x_prompt.shape[:2]))
    st = {'s5_re': state_s5_re[0], 's5_im': state_s5_im[0], 'mlstm_C': state_mlstm_C[0],
          'mlstm_n': state_mlstm_n[0], 'mlstm_m': state_mlstm_m[0], 'mlstm_conv': state_mlstm_conv[0],
          'ffn_conv': state_ffn_conv}
    out_s = _trunk(x_sample.astype(F32), st, p, _pick_tiles(*x_sample.shape[:2]))
    return (out_p[0], out_s[0]) + tuple(out_p[1:]) + tuple(out_s[1:])
```

```python
import functools
import math
from typing import NamedTuple

import jax
import jax.numpy as jnp
from jax import lax
from jax.experimental import pallas as pl
from jax.experimental.pallas import tpu as pltpu

F32 = jnp.float32
BF16 = jnp.bfloat16
HI = lax.Precision.HIGHEST

NORM_EPS = 1e-6
LN_EPS = 1e-5
S5_GROUP_CH = 16
S5_CHUNK = 16
QKV_BLOCK = 4

V7X_LANES = 128
V7X_SUBLANES = 8
V7X_MXU = 256
V7X_VMEM_BYTES = 64 * 1024 * 1024


class Tiles(NamedTuple):
    rows: int
    time: int
    s5_rows: int
    mlstm_chunk: int
    s5_time: int


def _pick_tiles(batch, seq):
    s5_time = (V7X_LANES // batch) * S5_CHUNK
    return Tiles(rows=min(batch * seq, 512), time=min(seq, 512),
                 s5_rows=min(batch * (seq // S5_CHUNK), 512), mlstm_chunk=min(seq, 256),
                 s5_time=s5_time if seq % s5_time == 0 else 0)


def _vmem_limit(block_bytes, scratch_bytes):
    want = 2 * block_bytes + scratch_bytes + 16 * 1024 * 1024
    return int(min(want, V7X_VMEM_BYTES - 8 * 1024 * 1024))


def _nbytes(shape, dtype):
    return math.prod(shape) * jnp.dtype(dtype).itemsize


def _rms(x, g):
    return x * lax.rsqrt(jnp.mean(x * x, axis=-1, keepdims=True) + NORM_EPS) * g


def _dot(a, b):
    return jnp.dot(a, b, preferred_element_type=F32)


def _dot_nt(a, b):
    return lax.dot_general(a, b, (((1,), (1,)), ((), ())), preferred_element_type=F32)


def _silu(x):
    return (0.5 * x) * (1.0 + jnp.tanh(0.5 * x))


def _causal_dwconv(u, prev, cw_ref, cb_ref, cs):
    width = cw_ref.shape[0]
    acc = cb_ref[:, cs] + u * cw_ref[width - 1:width, cs]
    rows = lax.broadcasted_iota(jnp.int32, prev.shape, 0)
    for j in range(width - 1):
        d = width - 1 - j
        rolled = pltpu.roll(u, d, 0)
        head = jnp.where(rows < d, pltpu.roll(prev, d, 0), rolled[0:V7X_SUBLANES])
        shifted = jnp.concatenate([head, rolled[V7X_SUBLANES:]], axis=0)
        acc = acc + shifted * cw_ref[j:j + 1, cs]
    return acc


def _log_sigmoid(x):
    return jnp.minimum(x, 0.0) - jnp.log1p(jnp.exp(-jnp.abs(x)))


def _norm_cast_kernel(x_ref, g_ref, o_ref):
    o_ref[...] = _rms(x_ref[...], g_ref[...]).astype(o_ref.dtype)


def _norm_cast(x2d, g, tb):
    rows, d = x2d.shape
    return pl.pallas_call(
        _norm_cast_kernel,
        grid=(rows // tb,),
        in_specs=[pl.BlockSpec((tb, d), lambda i: (i, 0)), pl.BlockSpec((1, d), lambda i: (0, 0))],
        out_specs=pl.BlockSpec((tb, d), lambda i: (i, 0)),
        out_shape=jax.ShapeDtypeStruct((rows, d), BF16),
        name="s5_norm",
    )(x2d, g)


def _s5_kernel(u_ref, t_ref, pre_ref, pim_ref, qre_ref, qim_ref, are_ref, aim_ref, h0re_ref, h0im_ref,
               y_ref, hre_ref, him_ref, pure_s, puim_s, hinre_s, hinim_s, stre_s, stim_s, *, nb, rb):
    r = pl.program_id(1)

    @pl.when(r == 0)
    def _():
        stre_s[...] = h0re_ref[0]
        stim_s[...] = h0im_ref[0]

    u = u_ref[...]
    pure_s[...] = _dot(u, pre_ref[0])
    puim_s[...] = _dot(u, pim_ref[0])
    ar = jnp.broadcast_to(are_ref[0], (nb, 2 * 64))
    ai = jnp.broadcast_to(aim_ref[0], (nb, 2 * 64))

    def step(i, carry):
        re, im = carry
        rows = pl.ds(pl.multiple_of(i * nb, nb), nb)
        hinre_s[rows, :] = re
        hinim_s[rows, :] = im
        return (ar * re - ai * im + pure_s[rows, :], ar * im + ai * re + puim_s[rows, :])

    re, im = lax.fori_loop(0, rb // nb, step, (stre_s[...], stim_s[...]))
    stre_s[...] = re
    stim_s[...] = im
    hre_ref[0] = re
    him_ref[0] = im

    carry_in = _dot(hinre_s[...].astype(BF16), qre_ref[0]) + _dot(hinim_s[...].astype(BF16), qim_ref[0])
    w = V7X_MXU
    for k in range(2):
        y_ref[:, k * w:(k + 1) * w] = (_dot(u[:, k * w:(k + 1) * w], t_ref[k])
                                       + carry_in[:, k * w:(k + 1) * w]).astype(y_ref.dtype)


def _s5_core(ut, mats, h0re, h0im, nb, rb):
    rows, cols = ut.shape
    pairs = cols // (2 * V7X_MXU)
    tm, pre, pim, qre, qim, are, aim = mats
    pair3 = lambda p, r: (p, 0, 0)
    kern = functools.partial(_s5_kernel, nb=nb, rb=rb)
    blocks = (_nbytes((rb, 512), BF16) * 2 + _nbytes((2, 256, 256), BF16) + 4 * _nbytes((512, 128), BF16))
    scratch = 4 * _nbytes((rb, 128), F32) + 2 * _nbytes((nb, 128), F32)
    return pl.pallas_call(
        kern,
        grid=(pairs, rows // rb),
        in_specs=[
            pl.BlockSpec((rb, 2 * V7X_MXU), lambda p, r: (r, p)),
            pl.BlockSpec((2, V7X_MXU, V7X_MXU), pair3),
            pl.BlockSpec((1, 2 * V7X_MXU, 128), pair3),
            pl.BlockSpec((1, 2 * V7X_MXU, 128), pair3),
            pl.BlockSpec((1, 128, 2 * V7X_MXU), pair3),
            pl.BlockSpec((1, 128, 2 * V7X_MXU), pair3),
            pl.BlockSpec((1, 1, 128), pair3),
            pl.BlockSpec((1, 1, 128), pair3),
            pl.BlockSpec((1, nb, 128), pair3),
            pl.BlockSpec((1, nb, 128), pair3),
        ],
        out_specs=[
            pl.BlockSpec((rb, 2 * V7X_MXU), lambda p, r: (r, p)),
            pl.BlockSpec((1, nb, 128), pair3),
            pl.BlockSpec((1, nb, 128), pair3),
        ],
        out_shape=[
            jax.ShapeDtypeStruct((rows, cols), BF16),
            jax.ShapeDtypeStruct((pairs, nb, 128), F32),
            jax.ShapeDtypeStruct((pairs, nb, 128), F32),
        ],
        scratch_shapes=[pltpu.VMEM((rb, 128), F32)] * 4 + [pltpu.VMEM((nb, 128), F32)] * 2,
        compiler_params=pltpu.CompilerParams(
            dimension_semantics=("arbitrary", "arbitrary"), vmem_limit_bytes=_vmem_limit(blocks, scratch)),
        name="s5_core",
    )(ut, tm, pre, pim, qre, qim, are, aim, h0re, h0im)


def _s5_fused_kernel(x_ref, g_ref, tt_ref, pt_ref, qt_ref, are_ref, aim_ref, h0re_ref, h0im_ref,
                     y_ref, hre_ref, him_ref,
                     us, ugt, pure, puim, hinre, hinim, ytmp, ys, stre, stim, *, nb, tbt):
    lc = S5_CHUNK
    ch = S5_GROUP_CH
    gps = V7X_LANES // ch
    nch = tbt // lc
    cols = nb * nch
    k = pl.program_id(1)

    @pl.when(k == 0)
    def _():
        for b in range(nb):
            u = _rms(x_ref[b], g_ref[...])
            for kk in range(us.shape[0]):
                us[kk, b * tbt:(b + 1) * tbt, :] = u[:, kk * V7X_LANES:(kk + 1) * V7X_LANES]

    @pl.when(pl.program_id(0) == 0)
    def _():
        stre[k] = h0re_ref[...]
        stim[k] = h0im_ref[...]

    for s in range(lc):
        a = us[k, pl.ds(s, cols, stride=lc), :].T
        for g in range(gps):
            ugt[g, s * ch:(s + 1) * ch, :] = a[g * ch:(g + 1) * ch, :].astype(BF16)

    half = pt_ref.shape[1] // 2
    for pr in range(gps // 2):
        put0 = _dot(pt_ref[2 * pr], ugt[2 * pr])
        put1 = _dot(pt_ref[2 * pr + 1], ugt[2 * pr + 1])
        pure[pr] = jnp.concatenate([put0[:half], put1[:half]], axis=0).T
        puim[pr] = jnp.concatenate([put0[half:], put1[half:]], axis=0).T

    for pr in range(gps // 2):
        re = stre[k, pr]
        im = stim[k, pr]
        ar = jnp.broadcast_to(are_ref[pr], re.shape)
        ai = jnp.broadcast_to(aim_ref[pr], re.shape)
        for j in range(nch):
            rows = pl.ds(j, nb, stride=nch)
            hinre[pr, rows, :] = re
            hinim[pr, rows, :] = im
            re, im = (ar * re - ai * im + pure[pr, rows, :], ar * im + ai * re + puim[pr, rows, :])
        stre[k, pr] = re
        stim[k, pr] = im
        hre_ref[k * (gps // 2) + pr] = re
        him_ref[k * (gps // 2) + pr] = im

    for pr in range(gps // 2):
        hre_t = hinre[pr].T
        him_t = hinim[pr].T
        for i in range(2):
            g = 2 * pr + i
            hin_t = jnp.concatenate([hre_t[i * half:(i + 1) * half], him_t[i * half:(i + 1) * half]],
                                    axis=0).astype(BF16)
            ytmp[g] = _dot(tt_ref[g], ugt[g]) + _dot(qt_ref[g], hin_t)

    for t in range(lc):
        z = jnp.concatenate([ytmp[g, t * ch:(t + 1) * ch, :] for g in range(gps)], axis=0)
        ys[pl.ds(t, cols, stride=lc), :] = z.T
    for b in range(nb):
        y_ref[b] = ys[b * tbt:(b + 1) * tbt, :].astype(y_ref.dtype)


def _s5_fused(x, g, mats_t, h0re, h0im, tbt):
    nb, s, d = x.shape
    tt, pt, qt, are, aim = mats_t
    groups = tt.shape[0]
    gps = V7X_LANES // S5_GROUP_CH
    slabs = groups // gps
    w = S5_CHUNK * S5_GROUP_CH
    nst = pt.shape[1]
    cols = nb * (tbt // S5_CHUNK)
    kern = functools.partial(_s5_fused_kernel, nb=nb, tbt=tbt)
    slab3 = lambda t, k: (k, 0, 0)
    blocks = (_nbytes((nb, tbt, d), F32) + _nbytes((nb, tbt, V7X_LANES), BF16)
              + gps * (_nbytes((w, w), BF16) + 2 * _nbytes((w, nst), BF16)))
    scratch = (_nbytes((slabs, nb * tbt, V7X_LANES), F32) + _nbytes((gps, w, cols), BF16)
               + 4 * _nbytes((gps // 2, cols, nst), F32) + _nbytes((gps, w, cols), F32)
               + _nbytes((nb * tbt, V7X_LANES), F32))
    return pl.pallas_call(
        kern,
        grid=(s // tbt, slabs),
        in_specs=[
            pl.BlockSpec((nb, tbt, d), lambda t, k: (0, t, 0)),
            pl.BlockSpec((1, d), lambda t, k: (0, 0)),
            pl.BlockSpec((gps, w, w), slab3),
            pl.BlockSpec((gps, nst, w), slab3),
            pl.BlockSpec((gps, w, nst), slab3),
            pl.BlockSpec((gps // 2, 1, nst), slab3),
            pl.BlockSpec((gps // 2, 1, nst), slab3),
            pl.BlockSpec((gps // 2, nb, nst), slab3),
            pl.BlockSpec((gps // 2, nb, nst), slab3),
        ],
        out_specs=[
            pl.BlockSpec((nb, tbt, V7X_LANES), lambda t, k: (0, t, k)),
            pl.BlockSpec((groups // 2, nb, nst), lambda t, k: (0, 0, 0)),
            pl.BlockSpec((groups // 2, nb, nst), lambda t, k: (0, 0, 0)),
        ],
        out_shape=[
            jax.ShapeDtypeStruct((nb, s, d), BF16),
            jax.ShapeDtypeStruct((groups // 2, nb, nst), F32),
            jax.ShapeDtypeStruct((groups // 2, nb, nst), F32),
        ],
        scratch_shapes=[
            pltpu.VMEM((slabs, nb * tbt, V7X_LANES), F32),
            pltpu.VMEM((gps, w, cols), BF16),
            pltpu.VMEM((gps // 2, cols, nst), F32),
            pltpu.VMEM((gps // 2, cols, nst), F32),
            pltpu.VMEM((gps // 2, cols, nst), F32),
            pltpu.VMEM((gps // 2, cols, nst), F32),
            pltpu.VMEM((gps, w, cols), F32),
            pltpu.VMEM((nb * tbt, V7X_LANES), F32),
            pltpu.VMEM((slabs, gps // 2, nb, nst), F32),
            pltpu.VMEM((slabs, gps // 2, nb, nst), F32),
        ],
        compiler_params=pltpu.CompilerParams(
            dimension_semantics=("arbitrary", "arbitrary"), vmem_limit_bytes=_vmem_limit(blocks, scratch)),
        name="s5_fused",
    )(x, g, tt, pt, qt, are, aim, h0re, h0im)


def _s5_matrices(a_re, a_im, log_dt, b_re, b_im, c_re, c_im):
    lc = S5_CHUNK
    groups, n = a_re.shape
    ch = S5_GROUP_CH
    a = lax.complex(a_re.astype(F32), a_im.astype(F32))
    adt = a * jnp.exp(log_dt.astype(F32))[:, None]
    j = jnp.arange(lc + 1, dtype=F32)
    apow = jnp.exp(adt[None] * j[:, None, None])
    b_bar = ((apow[1] - 1.0) / a)[..., None] * lax.complex(b_re.astype(F32), b_im.astype(F32))
    c_mat = lax.complex(c_re.astype(F32), c_im.astype(F32))
    ca = c_mat[None] * apow[:, :, None, :]
    kern = (jnp.einsum('jgcn,gnd->jgcd', ca.real, b_bar.real, precision=HI)
            - jnp.einsum('jgcn,gnd->jgcd', ca.imag, b_bar.imag, precision=HI))
    s_idx = jnp.arange(lc)[:, None]
    t_idx = jnp.arange(lc)[None, :]
    lag = t_idx - s_idx
    kt = kern[jnp.clip(lag, 0, lc)]
    kt = jnp.where((lag >= 0)[:, :, None, None, None], kt, 0.0)
    tm = jnp.transpose(kt, (2, 0, 4, 1, 3)).reshape(groups, lc * ch, lc * ch)
    pw = apow[lc - 1 - jnp.arange(lc)]
    p = pw[:, :, :, None] * b_bar[None]
    p = jnp.transpose(p, (1, 0, 3, 2)).reshape(groups, lc * ch, n)
    q = ca[1:]
    q = jnp.transpose(q, (1, 3, 0, 2)).reshape(groups, n, lc * ch)

    def pair_rows(m):
        g2 = groups // 2
        m = m.reshape(g2, 2, m.shape[1], m.shape[2])
        z = jnp.zeros_like(m[:, 0])
        top = jnp.concatenate([m[:, 0], z], axis=2)
        bot = jnp.concatenate([z, m[:, 1]], axis=2)
        return jnp.concatenate([top, bot], axis=1)

    pre = pair_rows(p.real).astype(BF16)
    pim = pair_rows(p.imag).astype(BF16)
    qre = pair_rows(q.real).astype(BF16)
    qim = pair_rows(-q.imag).astype(BF16)
    a_chunk = apow[lc].reshape(groups // 2, 1, 2 * n)
    are, aim = a_chunk.real, a_chunk.imag
    tt = jnp.swapaxes(tm, 1, 2).astype(BF16)
    pt = jnp.concatenate([jnp.swapaxes(p.real, 1, 2), jnp.swapaxes(p.imag, 1, 2)], axis=1).astype(BF16)
    qt = jnp.concatenate([jnp.swapaxes(q.real, 1, 2), jnp.swapaxes(-q.imag, 1, 2)], axis=2).astype(BF16)
    return {'row': (tm.astype(BF16), pre, pim, qre, qim, are, aim), 'col': (tt, pt, qt, are, aim)}


def _glu_kernel(x_ref, y_ref, g0_ref, d_ref, w_ref, g1_ref, o_ref):
    x = x_ref[...]
    d = x.shape[1]
    u = _rms(x, g0_ref[...])
    yy = y_ref[...].astype(F32) + d_ref[...] * u
    z = jax.nn.gelu(yy, approximate=True).astype(BF16)
    ag = _dot(z, w_ref[...])
    out = ag[:, :d] * jax.nn.sigmoid(ag[:, d:])
    o_ref[...] = x + _rms(out, g1_ref[...])


def _const_spec(shape):
    nd = len(shape)
    return pl.BlockSpec(shape, lambda *_: (0,) * nd, pipeline_mode=pl.Buffered(1))


def _glu(x2d, y2d, g0, dskip, w, g1, tb):
    rows, d = x2d.shape
    blocks = 2 * _nbytes((tb, d), F32) + _nbytes((tb, d), BF16)
    scratch = _nbytes(w.shape, BF16) + 3 * _nbytes((tb, 2 * d), F32)
    row = pl.BlockSpec((tb, d), lambda i: (i, 0))
    return pl.pallas_call(
        _glu_kernel,
        grid=(rows // tb,),
        in_specs=[row, row, _const_spec((1, d)), _const_spec((1, d)), _const_spec(w.shape), _const_spec((1, d))],
        out_specs=row,
        out_shape=jax.ShapeDtypeStruct((rows, d), F32),
        compiler_params=pltpu.CompilerParams(
            dimension_semantics=("arbitrary",), vmem_limit_bytes=_vmem_limit(blocks, scratch)),
        name="s5_glu",
    )(x2d, y2d, g0, dskip, w, g1)


def _ffn_kernel(x_ref, buf_ref, g2_ref, wup_ref, cw_ref, cb_ref, wdown_ref, g3_ref,
                o_ref, nbuf_ref, carry, act, *, tb, dff, width):
    t = pl.program_id(1)
    pad = V7X_SUBLANES

    @pl.when(t == 0)
    def _():
        carry[...] = jnp.zeros_like(carry)
        carry[pad - (width - 1):pad, :] = buf_ref[0]

    x = x_ref[0]
    hn = _rms(x, g2_ref[...]).astype(BF16)
    fb = V7X_MXU

    def conv(cs):
        u = _dot(hn, wup_ref[:, cs])
        prev = carry[:, cs]
        carry[:, cs] = u[tb - pad:tb, :]
        return _causal_dwconv(u, prev, cw_ref, cb_ref, cs)

    for j in range(dff // fb):
        gate = conv(slice(j * fb, (j + 1) * fb))
        val = conv(slice(dff + j * fb, dff + (j + 1) * fb))
        act[:, j * fb:(j + 1) * fb] = (jax.nn.gelu(gate, approximate=True) * val).astype(BF16)

    f = _dot(act[...], wdown_ref[...])
    o_ref[0] = x + _rms(f, g3_ref[...])
    nbuf_ref[0] = carry[pad - (width - 1):pad, :]


def _ffn(x, buf, g2, wup, cw, cb, wdown, g3, tb):
    b, s, d = x.shape
    dff = wdown.shape[0]
    width = cw.shape[0]
    kern = functools.partial(_ffn_kernel, tb=tb, dff=dff, width=width)
    blocks = 2 * _nbytes((tb, d), F32)
    scratch = (_nbytes(wup.shape, BF16) + _nbytes(wdown.shape, BF16) + 6 * _nbytes((tb, V7X_MXU), F32)
               + _nbytes((8, 2 * dff), F32) + _nbytes((tb, dff), BF16) + 2 * _nbytes((tb, d), F32))
    xs = pl.BlockSpec((1, tb, d), lambda i, t: (i, t, 0))
    bs = pl.BlockSpec((1, width - 1, 2 * dff), lambda i, t: (i, 0, 0))
    return pl.pallas_call(
        kern,
        grid=(b, s // tb),
        in_specs=[xs, bs, _const_spec((1, d)), _const_spec(wup.shape), _const_spec(cw.shape),
                  _const_spec((1, 2 * dff)), _const_spec(wdown.shape), _const_spec((1, d))],
        out_specs=[xs, bs],
        out_shape=[jax.ShapeDtypeStruct((b, s, d), F32), jax.ShapeDtypeStruct((b, width - 1, 2 * dff), F32)],
        scratch_shapes=[pltpu.VMEM((V7X_SUBLANES, 2 * dff), F32), pltpu.VMEM((tb, dff), BF16)],
        compiler_params=pltpu.CompilerParams(
            dimension_semantics=("arbitrary", "arbitrary"), vmem_limit_bytes=_vmem_limit(blocks, scratch)),
        name="conv_ffn",
    )(x, buf, g2, wup, cw, cb, wdown, g3)


def _gate_fold_kernel(wq_ref, wk_ref, wv_ref, gq_ref, gk_ref, gv_ref, gc_ref, gm_ref):
    dotp = functools.partial(jnp.dot, preferred_element_type=F32, precision=HI)
    gc_ref[...] = dotp(wq_ref[0], gq_ref[...]) + dotp(wk_ref[0], gk_ref[...])
    gm_ref[...] = dotp(wv_ref[0], gv_ref[...])


def _gate_fold(wq_t, wk_t, wv_t, w_gate):
    nt, w, _ = wq_t.shape
    inner = nt * w
    ng = w_gate.shape[1]
    tile = pl.BlockSpec((1, w, w), lambda i: (i, 0, 0))
    gs = [pl.BlockSpec((w, ng), lambda i, k=k: (k * nt + i, 0)) for k in range(3)]
    out = pl.BlockSpec((w, ng), lambda i: (i, 0))
    return pl.pallas_call(
        _gate_fold_kernel,
        grid=(nt,),
        in_specs=[tile, tile, tile] + gs,
        out_specs=[out, out],
        out_shape=[jax.ShapeDtypeStruct((inner, ng), F32)] * 2,
        name="mlstm_gate_fold",
    )(wq_t, wk_t, wv_t, w_gate, w_gate, w_gate)


def _mpre_kernel(x_ref, cbuf_ref, g_ref, wup_ref, cw_ref, cb_ref, wq_ref, wkt_ref, wv_ref, gc_ref, gm_ref,
                 bg_ref, q_ref, kt_ref, v_ref, xc_ref, z_ref, gates_ref, nbuf_ref, carry,
                 *, tb, inner, width):
    t = pl.program_id(1)
    pad = V7X_SUBLANES

    @pl.when(t == 0)
    def _():
        carry[...] = jnp.zeros_like(carry)
        carry[pad - (width - 1):pad, :] = cbuf_ref[0]

    h = _rms(x_ref[0], g_ref[...]).astype(BF16)
    w = V7X_MXU
    gacc = jnp.zeros((tb, bg_ref.shape[1]), F32) + bg_ref[...]
    for i in range(inner // w):
        cs = slice(i * w, (i + 1) * w)
        xm = _dot(h, wup_ref[:, cs])
        prev = carry[:, cs]
        carry[:, cs] = xm[tb - pad:tb, :]
        xc = _silu(_causal_dwconv(xm, prev, cw_ref, cb_ref, cs)).astype(BF16)
        xmb = xm.astype(BF16)
        q_ref[0, :, cs] = _dot(xc, wq_ref[i]).astype(BF16)
        kt_ref[0, cs, :] = _dot_nt(wkt_ref[i], xc).astype(BF16)
        v_ref[0, :, cs] = _dot(xmb, wv_ref[i]).astype(BF16)
        xc_ref[0, :, cs] = xc
        z_ref[0, :, cs] = _silu(_dot(h, wup_ref[:, inner + i * w:inner + (i + 1) * w])).astype(BF16)
        gacc = gacc + _dot(xc, gc_ref[cs, :]) + _dot(xmb, gm_ref[cs, :])
    gates_ref[0] = gacc
    nbuf_ref[0] = carry[pad - (width - 1):pad, :]


def _mpre(x, cbuf, g, wup, cw, cb, wq_t, wkt_t, wv_t, gc, gm, bg, tb):
    b, s, d = x.shape
    inner = cw.shape[1]
    width = cw.shape[0]
    ng = bg.shape[1]
    kern = functools.partial(_mpre_kernel, tb=tb, inner=inner, width=width)
    act = pl.BlockSpec((1, tb, inner), lambda i, t: (i, t, 0))
    blocks = _nbytes((tb, d), F32) + 5 * _nbytes((tb, inner), BF16)
    scratch = (_nbytes(wup.shape, BF16) + 3 * _nbytes(wq_t.shape, BF16) + 2 * _nbytes((inner, 128), BF16)
               + _nbytes((8, inner), F32) + 8 * _nbytes((tb, V7X_MXU), F32))
    return pl.pallas_call(
        kern,
        grid=(b, s // tb),
        in_specs=[pl.BlockSpec((1, tb, d), lambda i, t: (i, t, 0)),
                  pl.BlockSpec((1, width - 1, inner), lambda i, t: (i, 0, 0)),
                  _const_spec((1, d)), _const_spec(wup.shape), _const_spec(cw.shape), _const_spec((1, inner)),
                  _const_spec(wq_t.shape), _const_spec(wkt_t.shape), _const_spec(wv_t.shape),
                  _const_spec(gc.shape), _const_spec(gm.shape), _const_spec((1, ng))],
        out_specs=[act, pl.BlockSpec((1, inner, tb), lambda i, t: (i, 0, t)), act, act, act,
                   pl.BlockSpec((1, tb, ng), lambda i, t: (i, t, 0)),
                   pl.BlockSpec((1, width - 1, inner), lambda i, t: (i, 0, 0))],
        out_shape=[jax.ShapeDtypeStruct((b, s, inner), BF16), jax.ShapeDtypeStruct((b, inner, s), BF16),
                   jax.ShapeDtypeStruct((b, s, inner), BF16), jax.ShapeDtypeStruct((b, s, inner), BF16),
                   jax.ShapeDtypeStruct((b, s, inner), BF16), jax.ShapeDtypeStruct((b, s, ng), F32),
                   jax.ShapeDtypeStruct((b, width - 1, inner), F32)],
        scratch_shapes=[pltpu.VMEM((V7X_SUBLANES, inner), F32)],
        compiler_params=pltpu.CompilerParams(
            dimension_semantics=("arbitrary", "arbitrary"), vmem_limit_bytes=_vmem_limit(blocks, scratch)),
        name="mlstm_pre",
    )(x, cbuf, g, wup, cw, cb, wq_t, wkt_t, wv_t, gc, gm, bg)


def _mlstm_kernel(*refs, lc, heads, dh, zero_init):
    if zero_init:
        (q_ref, kt_ref, v_ref, g_ref, gt_ref,
         hn_ref, cout_ref, nout_ref, mout_ref, ct_s, n_s, m_s) = refs
    else:
        (q_ref, kt_ref, v_ref, g_ref, gt_ref, c0_ref, n0_ref, m0_ref,
         hn_ref, cout_ref, nout_ref, mout_ref, ct_s, n_s, m_s) = refs
    c = pl.program_id(1)

    @pl.when(c == 0)
    def _():
        if zero_init:
            ct_s[...] = jnp.zeros_like(ct_s)
            n_s[...] = jnp.zeros_like(n_s)
            m_s[...] = jnp.zeros_like(m_s)
        else:
            for h in range(heads):
                ct_s[h] = c0_ref[0, h].T
            n_s[...] = n0_ref[0]
            m_s[...] = m0_ref[0]

    g = g_ref[0]
    gt = gt_ref[0]
    row = lax.broadcasted_iota(jnp.int32, (lc, lc), 0)
    col = lax.broadcasted_iota(jnp.int32, (lc, lc), 1)
    causal = col <= row
    for h in range(heads):
        hs = slice(h * dh, (h + 1) * dh)
        i_col = g[:, h:h + 1]
        f_col = _log_sigmoid(g[:, heads + h:heads + h + 1])
        i_row = gt[h:h + 1, :]
        f_row = _log_sigmoid(gt[heads + h:heads + h + 1, :])
        b_col = jnp.sum(jnp.where(causal, f_row, 0.0), axis=1, keepdims=True)
        b_row = jnp.sum(jnp.where(row <= col, f_col, 0.0), axis=0, keepdims=True)
        m_prev = m_s[h:h + 1, 0:1]
        dmat = jnp.where(causal, b_col - b_row + i_row, -jnp.inf)
        inter = b_col + m_prev
        m_t = jnp.maximum(inter, jnp.max(dmat, axis=1, keepdims=True))
        wts = jnp.exp(dmat - m_t)
        qh = q_ref[0, :, hs]
        kth = kt_ref[0, hs, :]
        vh = v_ref[0, :, hs]
        sc = _dot(qh, kth) * wts
        a = jnp.exp(inter - m_t)
        ct = ct_s[h]
        n_row = n_s[h:h + 1, :]
        num = _dot(sc.astype(BF16), vh) + a * _dot(qh, ct.astype(BF16))
        qn = jnp.sum(qh.astype(F32) * n_row, axis=1, keepdims=True)
        den = jnp.sum(sc, axis=1, keepdims=True) + a * qn
        hh = num / jnp.maximum(jnp.abs(den), jnp.exp(-m_t))
        mu = jnp.mean(hh, axis=1, keepdims=True)
        dev = hh - mu
        var = jnp.mean(dev * dev, axis=1, keepdims=True)
        hn_ref[0, :, hs] = (dev * lax.rsqrt(var + LN_EPS)).astype(hn_ref.dtype)

        b_last = b_row[:, lc - 1:lc]
        g_row = b_last - b_row + i_row
        m_new = jnp.maximum(b_last + m_prev, jnp.max(g_row, axis=1, keepdims=True))
        decay = jnp.exp(b_last + m_prev - m_new)
        wg_row = jnp.exp(g_row - m_new)
        wg_col = jnp.exp(b_last - b_col + i_col - m_new)
        wv = (vh.astype(F32) * wg_col).astype(BF16)
        ct_s[h] = decay * ct + _dot(kth, wv)
        wg16 = jnp.broadcast_to(wg_row, (16, lc)).astype(BF16)
        n_s[h:h + 1, :] = decay * n_row + _dot_nt(wg16, kth)[0:1, :]
        m_s[h:h + 1, :] = jnp.broadcast_to(m_new, (1, V7X_LANES))

    @pl.when(c == pl.num_programs(1) - 1)
    def _():
        for h in range(heads):
            cout_ref[0, h] = ct_s[h].T
        nout_ref[0] = n_s[...]
        mout_ref[0] = m_s[...]


def _mlstm_core(q, kt, v, gates, gates_t, state, lc, heads):
    b, s, inner = q.shape
    dh = inner // heads
    ng = gates.shape[2]
    zero_init = state is None
    kern = functools.partial(_mlstm_kernel, lc=lc, heads=heads, dh=dh, zero_init=zero_init)
    act = pl.BlockSpec((1, lc, inner), lambda i, c: (i, c, 0))
    cs = pl.BlockSpec((1, heads, dh, dh), lambda i, c: (i, 0, 0, 0))
    ns = pl.BlockSpec((1, heads, dh), lambda i, c: (i, 0, 0))
    ms = pl.BlockSpec((1, heads, V7X_LANES), lambda i, c: (i, 0, 0))
    in_specs = [act, pl.BlockSpec((1, inner, lc), lambda i, c: (i, 0, c)), act,
                pl.BlockSpec((1, lc, ng), lambda i, c: (i, c, 0)),
                pl.BlockSpec((1, ng, lc), lambda i, c: (i, 0, c))]
    args = [q, kt, v, gates, gates_t]
    if not zero_init:
        in_specs += [cs, ns, ms]
        args += list(state)
    blocks = 4 * _nbytes((lc, inner), BF16) + (1 if zero_init else 2) * _nbytes((heads, dh, dh), F32)
    scratch = _nbytes((heads, dh, dh), F32) + 6 * _nbytes((lc, max(lc, dh)), F32) + 2 * _nbytes((dh, dh), F32)
    return pl.pallas_call(
        kern,
        grid=(b, s // lc),
        in_specs=in_specs,
        out_specs=[act, cs, ns, ms],
        out_shape=[jax.ShapeDtypeStruct((b, s, inner), BF16), jax.ShapeDtypeStruct((b, heads, dh, dh), F32),
                   jax.ShapeDtypeStruct((b, heads, dh), F32), jax.ShapeDtypeStruct((b, heads, V7X_LANES), F32)],
        scratch_shapes=[pltpu.VMEM((heads, dh, dh), F32), pltpu.VMEM((heads, dh), F32),
                        pltpu.VMEM((heads, V7X_LANES), F32)],
        compiler_params=pltpu.CompilerParams(
            dimension_semantics=("arbitrary", "arbitrary"), vmem_limit_bytes=_vmem_limit(blocks, scratch)),
        name="mlstm_core",
    )(*args)


def _mout_kernel(hn_ref, xc_ref, z_ref, x_ref, ng_ref, sk_ref, w_ref, g1_ref, o_ref):
    hs = hn_ref[...].astype(F32) * ng_ref[...] + sk_ref[...] * xc_ref[...].astype(F32)
    a = (hs * z_ref[...].astype(F32)).astype(BF16)
    x = x_ref[...]
    o_ref[...] = x + _rms(_dot(a, w_ref[...]), g1_ref[...])


def _mout(hn, xc, sz, x2d, ng, sk, w, g1, tb):
    rows, d = x2d.shape
    inner = hn.shape[1]
    act = pl.BlockSpec((tb, inner), lambda i: (i, 0))
    row = pl.BlockSpec((tb, d), lambda i: (i, 0))
    blocks = 3 * _nbytes((tb, inner), BF16) + 2 * _nbytes((tb, d), F32)
    scratch = _nbytes(w.shape, BF16) + 3 * _nbytes((tb, inner), F32)
    return pl.pallas_call(
        _mout_kernel,
        grid=(rows // tb,),
        in_specs=[act, act, act, row, _const_spec((1, inner)), _const_spec((1, inner)), _const_spec(w.shape),
                  _const_spec((1, d))],
        out_specs=row,
        out_shape=jax.ShapeDtypeStruct((rows, d), F32),
        compiler_params=pltpu.CompilerParams(
            dimension_semantics=("arbitrary",), vmem_limit_bytes=_vmem_limit(blocks, scratch)),
        name="mlstm_out",
    )(hn, xc, sz, x2d, ng, sk, w, g1)


def _block_diag_tiles(w):
    nblk, k, _ = w.shape
    per = V7X_MXU // k
    wt = w.reshape(nblk // per, per, k, k)
    eye = jnp.eye(per, dtype=w.dtype)
    full = wt[:, :, :, None, :] * eye[None, :, None, :, None]
    return full.reshape(nblk // per, per * k, per * k)


def _prep_weights(w):
    row = lambda v: v.astype(F32).reshape(1, -1)
    p = {}
    p['norm_g'] = w['norm_g'].astype(F32)
    p['s5_mats'] = _s5_matrices(w['s5_A_re'][0], w['s5_A_im'][0], w['s5_log_dt'][0], w['s5_B_re'][0],
                                w['s5_B_im'][0], w['s5_C_re'][0], w['s5_C_im'][0])
    p['s5_D'] = row(w['s5_D'][0])
    p['s5_w_glu'] = w['s5_w_glu'][0].astype(BF16)
    inner = w['mlstm_conv_w'].shape[2]
    heads = w['mlstm_b_gate'].shape[1] // 2
    dh = inner // heads
    wq_t = _block_diag_tiles(w['mlstm_wq'][0].astype(F32))
    wk_t = _block_diag_tiles(w['mlstm_wk'][0].astype(F32))
    wv_t = _block_diag_tiles(w['mlstm_wv'][0].astype(F32))
    gc, gm = _gate_fold(wq_t, wk_t, wv_t, w['mlstm_w_gate'][0].astype(F32))
    p['m_wq'] = wq_t.astype(BF16)
    p['m_wkt'] = (jnp.swapaxes(wk_t, 1, 2) * (dh ** -0.5)).astype(BF16)
    p['m_wv'] = wv_t.astype(BF16)
    p['m_gc'] = gc.astype(BF16)
    p['m_gm'] = gm.astype(BF16)
    p['m_bg'] = row(w['mlstm_b_gate'][0])
    p['m_wup'] = w['mlstm_w_up'][0].astype(BF16)
    p['m_cw'] = w['mlstm_conv_w'][0].astype(F32)
    p['m_cb'] = row(w['mlstm_conv_b'][0])
    p['m_ng'] = row(w['mlstm_norm_g'][0])
    p['m_skip'] = row(w['mlstm_skip'][0])
    p['m_wdown'] = w['mlstm_w_down'][0].astype(BF16)
    p['heads'] = heads
    p['f_wup'] = w['ffn_w_up'].astype(BF16)
    p['f_cw'] = w['ffn_conv_w'].astype(F32)
    p['f_cb'] = w['ffn_conv_b'].astype(F32)
    p['f_wdown'] = w['ffn_w_down'].astype(BF16)
    return p


def _to_chunk_major(u2d, b, s):
    d = u2d.shape[1]
    g = d // S5_GROUP_CH
    u = u2d.reshape(b, s // S5_CHUNK, S5_CHUNK, g, S5_GROUP_CH)
    return jnp.transpose(u, (1, 0, 3, 2, 4)).reshape((s // S5_CHUNK) * b, d * S5_CHUNK)


def _to_token_major(yt, b, s):
    d = yt.shape[1] // S5_CHUNK
    g = d // S5_GROUP_CH
    y = yt.reshape(s // S5_CHUNK, b, g, S5_CHUNK, S5_GROUP_CH)
    return jnp.transpose(y, (1, 0, 3, 2, 4)).reshape(b * s, d)


def _trunk(x, st, p, tiles):
    b, s, d = x.shape
    ng = p['norm_g']
    g_of = lambda layer, k: ng[layer, k].reshape(1, d)
    zero_init = st is None
    groups = d // S5_GROUP_CH
    x2d = x.reshape(b * s, d)

    n_state = p['s5_mats']['row'][1].shape[2] // 2
    if zero_init:
        h0re = h0im = jnp.zeros((groups // 2, b, 2 * n_state), F32)
    else:
        pair = lambda h: jnp.transpose(h.astype(F32).reshape(b, groups // 2, 2 * n_state), (1, 0, 2))
        h0re, h0im = pair(st['s5_re']), pair(st['s5_im'])
    if tiles.s5_time:
        y, hre, him = _s5_fused(x, g_of(0, 0), p['s5_mats']['col'], h0re, h0im, tiles.s5_time)
        y2d = y.reshape(b * s, d)
    else:
        u = _norm_cast(x2d, g_of(0, 0), tiles.rows)
        yt, hre, him = _s5_core(_to_chunk_major(u, b, s), p['s5_mats']['row'], h0re, h0im, b, tiles.s5_rows)
        y2d = _to_token_major(yt, b, s)
    unpair = lambda h: jnp.transpose(h, (1, 0, 2)).reshape(1, b, groups, n_state)
    o_re, o_im = unpair(hre), unpair(him)
    x2d = _glu(x2d, y2d, g_of(0, 0), p['s5_D'], p['s5_w_glu'], g_of(0, 1), tiles.rows)

    def ffn(x2d, layer):
        dff2 = p['f_wup'].shape[2]
        width = p['f_cw'].shape[1]
        buf = jnp.zeros((b, width - 1, dff2), F32) if zero_init else st['ffn_conv'][layer].astype(F32)
        xo, nbuf = _ffn(x2d.reshape(b, s, d), buf, g_of(layer, 2), p['f_wup'][layer], p['f_cw'][layer],
                        p['f_cb'][layer].reshape(1, dff2), p['f_wdown'][layer], g_of(layer, 3), tiles.time)
        return xo.reshape(b * s, d), nbuf

    x2d, fbuf0 = ffn(x2d, 0)

    heads = p['heads']
    inner = p['m_cw'].shape[1]
    width = p['m_cw'].shape[0]
    cbuf = jnp.zeros((b, width - 1, inner), F32) if zero_init else st['mlstm_conv'].astype(F32)
    q, kt, v, xc, sz, gates, ncbuf = _mpre(
        x2d.reshape(b, s, d), cbuf, g_of(1, 0), p['m_wup'], p['m_cw'], p['m_cb'], p['m_wq'], p['m_wkt'],
        p['m_wv'], p['m_gc'], p['m_gm'], p['m_bg'], tiles.time)
    if zero_init:
        state = None
    else:
        state = (st['mlstm_C'].astype(F32), st['mlstm_n'].astype(F32),
                 jnp.broadcast_to(st['mlstm_m'].astype(F32)[:, :, None], (b, heads, V7X_LANES)))
    hn, c_out, n_out, m_out = _mlstm_core(q, kt, v, gates, jnp.swapaxes(gates, 1, 2), state,
                                          tiles.mlstm_chunk, heads)
    x2d = _mout(hn.reshape(b * s, inner), xc.reshape(b * s, inner), sz.reshape(b * s, inner), x2d,
                p['m_ng'], p['m_skip'], p['m_wdown'], g_of(1, 1), tiles.rows)
    x2d, fbuf1 = ffn(x2d, 1)

    return (x2d.reshape(b, s, d), o_re, o_im, c_out[None], n_out[None], m_out[None, :, :, 0], ncbuf[None],
            jnp.stack([fbuf0, fbuf1], axis=0))


def kernel(x_prompt, x_sample, state_s5_re, state_s5_im, state_mlstm_C, state_mlstm_n, state_mlstm_m, state_mlstm_conv, state_ffn_conv, norm_g, s5_A_re, s5_A_im, s5_log_dt, s5_B_re, s5_B_im, s5_C_re, s5_C_im, s5_D, s5_w_glu, mlstm_w_up, mlstm_conv_w, mlstm_conv_b, mlstm_wq, mlstm_wk, mlstm_wv, mlstm_w_gate, mlstm_b_gate, mlstm_norm_g, mlstm_skip, mlstm_w_down, ffn_w_up, ffn_conv_w, ffn_conv_b, ffn_w_down):
    w = {
        'norm_g': norm_g, 's5_A_re': s5_A_re, 's5_A_im': s5_A_im, 's5_log_dt': s5_log_dt,
        's5_B_re': s5_B_re, 's5_B_im': s5_B_im, 's5_C_re': s5_C_re, 's5_C_im': s5_C_im, 's5_D': s5_D,
        's5_w_glu': s5_w_glu, 'mlstm_w_up': mlstm_w_up, 'mlstm_conv_w': mlstm_conv_w,
        'mlstm_conv_b': mlstm_conv_b, 'mlstm_wq': mlstm_wq, 'mlstm_wk': mlstm_wk, 'mlstm_wv': mlstm_wv,
        'mlstm_w_gate': mlstm_w_gate, 'mlstm_b_gate': mlstm_b_gate, 'mlstm_norm_g': mlstm_norm_g,
        'mlstm_skip': mlstm_skip, 'mlstm_w_down': mlstm_w_down, 'ffn_w_up': ffn_w_up,
        'ffn_conv_w': ffn_conv_w, 'ffn_conv_b': ffn_conv_b, 'ffn_w_down': ffn_w_down,
    }
    assert norm_g.shape[0] == 2 and s5_A_re.shape[0] == 1 and mlstm_w_up.shape[0] == 1
    assert x_prompt.shape[0] == V7X_SUBLANES and x_sample.shape[0] == V7X_SUBLANES
    p = _prep_weights(w)
    out_p = _trunk(x_prompt.astype(F32), None, p, _pick_tiles(*x_prompt.shape[:2]))
    st = {'s5_re': state_s5_re[0], 's5_im': state_s5_im[0], 'mlstm_C': state_mlstm_C[0],
          'mlstm_n': state_mlstm_n[0], 'mlstm_m': state_mlstm_m[0], 'mlstm_conv': state_mlstm_conv[0],
          'ffn_conv': state_ffn_conv}
    out_s = _trunk(x_sample.astype(F32), st, p, _pick_tiles(*x_sample.shape[:2]))
    return (out_p[0], out_s[0]) + tuple(out_p[1:]) + tuple(out_s[1:])
```

```python
import functools
import math
from typing import NamedTuple

import jax
import jax.numpy as jnp
from jax import lax
from jax.experimental import pallas as pl
from jax.experimental.pallas import tpu as pltpu

F32 = jnp.float32
BF16 = jnp.bfloat16
HI = lax.Precision.HIGHEST

NORM_EPS = 1e-6
LN_EPS = 1e-5
S5_GROUP_CH = 16
S5_CHUNK = 16
QKV_BLOCK = 4

V7X_LANES = 128
V7X_SUBLANES = 8
V7X_MXU = 256
V7X_VMEM_BYTES = 64 * 1024 * 1024


class Tiles(NamedTuple):
    rows: int
    time: int
    s5_rows: int
    mlstm_chunk: int
    s5_time: int
    ffn_time: int


def _pick_tiles(batch, seq):
    s5_time = (V7X_LANES // batch) * S5_CHUNK
    return Tiles(rows=min(batch * seq, 512), time=min(seq, 512),
                 s5_rows=min(batch * (seq // S5_CHUNK), 512), mlstm_chunk=min(seq, 256),
                 s5_time=s5_time if seq % s5_time == 0 else 0, ffn_time=min(seq, 1024))


def _vmem_limit(block_bytes, scratch_bytes):
    want = 2 * block_bytes + scratch_bytes + 16 * 1024 * 1024
    return int(min(want, V7X_VMEM_BYTES - 8 * 1024 * 1024))


def _nbytes(shape, dtype):
    return math.prod(shape) * jnp.dtype(dtype).itemsize


def _rms(x, g):
    return x * lax.rsqrt(jnp.mean(x * x, axis=-1, keepdims=True) + NORM_EPS) * g


def _dot(a, b):
    return jnp.dot(a, b, preferred_element_type=F32)


def _dot_nt(a, b):
    return lax.dot_general(a, b, (((1,), (1,)), ((), ())), preferred_element_type=F32)


def _silu(x):
    return (0.5 * x) * (1.0 + jnp.tanh(0.5 * x))


def _causal_dwconv(u, prev, cw_ref, cb_ref, cs):
    width = cw_ref.shape[0]
    acc = cb_ref[:, cs] + u * cw_ref[width - 1:width, cs]
    rows = lax.broadcasted_iota(jnp.int32, prev.shape, 0)
    for j in range(width - 1):
        d = width - 1 - j
        rolled = pltpu.roll(u, d, 0)
        head = jnp.where(rows < d, pltpu.roll(prev, d, 0), rolled[0:V7X_SUBLANES])
        shifted = jnp.concatenate([head, rolled[V7X_SUBLANES:]], axis=0)
        acc = acc + shifted * cw_ref[j:j + 1, cs]
    return acc


def _log_sigmoid(x):
    return jnp.minimum(x, 0.0) - jnp.log1p(jnp.exp(-jnp.abs(x)))


def _split3(x):
    hi = x.astype(BF16)
    r1 = x - hi.astype(F32)
    mid = r1.astype(BF16)
    lo = (r1 - mid.astype(F32)).astype(BF16)
    return hi, mid, lo


def _norm_cast_kernel(x_ref, g_ref, o_ref):
    o_ref[...] = _rms(x_ref[...], g_ref[...]).astype(o_ref.dtype)


def _norm_cast(x2d, g, tb):
    rows, d = x2d.shape
    return pl.pallas_call(
        _norm_cast_kernel,
        grid=(rows // tb,),
        in_specs=[pl.BlockSpec((tb, d), lambda i: (i, 0)), pl.BlockSpec((1, d), lambda i: (0, 0))],
        out_specs=pl.BlockSpec((tb, d), lambda i: (i, 0)),
        out_shape=jax.ShapeDtypeStruct((rows, d), BF16),
        name="s5_norm",
    )(x2d, g)


def _s5_kernel(u_ref, t_ref, pre_ref, pim_ref, qre_ref, qim_ref, are_ref, aim_ref, h0re_ref, h0im_ref,
               y_ref, hre_ref, him_ref, pure_s, puim_s, hinre_s, hinim_s, stre_s, stim_s, *, nb, rb):
    r = pl.program_id(1)

    @pl.when(r == 0)
    def _():
        stre_s[...] = h0re_ref[0]
        stim_s[...] = h0im_ref[0]

    u = u_ref[...]
    pure_s[...] = _dot(u, pre_ref[0])
    puim_s[...] = _dot(u, pim_ref[0])
    ar = jnp.broadcast_to(are_ref[0], (nb, 2 * 64))
    ai = jnp.broadcast_to(aim_ref[0], (nb, 2 * 64))

    def step(i, carry):
        re, im = carry
        rows = pl.ds(pl.multiple_of(i * nb, nb), nb)
        hinre_s[rows, :] = re
        hinim_s[rows, :] = im
        return (ar * re - ai * im + pure_s[rows, :], ar * im + ai * re + puim_s[rows, :])

    re, im = lax.fori_loop(0, rb // nb, step, (stre_s[...], stim_s[...]))
    stre_s[...] = re
    stim_s[...] = im
    hre_ref[0] = re
    him_ref[0] = im

    carry_in = _dot(hinre_s[...].astype(BF16), qre_ref[0]) + _dot(hinim_s[...].astype(BF16), qim_ref[0])
    w = V7X_MXU
    for k in range(2):
        y_ref[:, k * w:(k + 1) * w] = (_dot(u[:, k * w:(k + 1) * w], t_ref[k])
                                       + carry_in[:, k * w:(k + 1) * w]).astype(y_ref.dtype)


def _s5_core(ut, mats, h0re, h0im, nb, rb):
    rows, cols = ut.shape
    pairs = cols // (2 * V7X_MXU)
    tm, pre, pim, qre, qim, are, aim = mats
    pair3 = lambda p, r: (p, 0, 0)
    kern = functools.partial(_s5_kernel, nb=nb, rb=rb)
    blocks = (_nbytes((rb, 512), BF16) * 2 + _nbytes((2, 256, 256), BF16) + 4 * _nbytes((512, 128), BF16))
    scratch = 4 * _nbytes((rb, 128), F32) + 2 * _nbytes((nb, 128), F32)
    return pl.pallas_call(
        kern,
        grid=(pairs, rows // rb),
        in_specs=[
            pl.BlockSpec((rb, 2 * V7X_MXU), lambda p, r: (r, p)),
            pl.BlockSpec((2, V7X_MXU, V7X_MXU), pair3),
            pl.BlockSpec((1, 2 * V7X_MXU, 128), pair3),
            pl.BlockSpec((1, 2 * V7X_MXU, 128), pair3),
            pl.BlockSpec((1, 128, 2 * V7X_MXU), pair3),
            pl.BlockSpec((1, 128, 2 * V7X_MXU), pair3),
            pl.BlockSpec((1, 1, 128), pair3),
            pl.BlockSpec((1, 1, 128), pair3),
            pl.BlockSpec((1, nb, 128), pair3),
            pl.BlockSpec((1, nb, 128), pair3),
        ],
        out_specs=[
            pl.BlockSpec((rb, 2 * V7X_MXU), lambda p, r: (r, p)),
            pl.BlockSpec((1, nb, 128), pair3),
            pl.BlockSpec((1, nb, 128), pair3),
        ],
        out_shape=[
            jax.ShapeDtypeStruct((rows, cols), BF16),
            jax.ShapeDtypeStruct((pairs, nb, 128), F32),
            jax.ShapeDtypeStruct((pairs, nb, 128), F32),
        ],
        scratch_shapes=[pltpu.VMEM((rb, 128), F32)] * 4 + [pltpu.VMEM((nb, 128), F32)] * 2,
        compiler_params=pltpu.CompilerParams(
            dimension_semantics=("arbitrary", "arbitrary"), vmem_limit_bytes=_vmem_limit(blocks, scratch)),
        name="s5_core",
    )(ut, tm, pre, pim, qre, qim, are, aim, h0re, h0im)


def _s5_fused_kernel(x_ref, g_ref, tt_ref, pt_ref, qt_ref, are_ref, aim_ref, h0re_ref, h0im_ref,
                     y_ref, hre_ref, him_ref,
                     us, ugt, pure, puim, hinre, hinim, ytmp, ys, stre, stim, *, nb, tbt):
    lc = S5_CHUNK
    ch = S5_GROUP_CH
    gps = V7X_LANES // ch
    nch = tbt // lc
    cols = nb * nch
    k = pl.program_id(1)

    @pl.when(k == 0)
    def _():
        for b in range(nb):
            u = _rms(x_ref[b], g_ref[...])
            for kk in range(us.shape[0]):
                us[kk, b * tbt:(b + 1) * tbt, :] = u[:, kk * V7X_LANES:(kk + 1) * V7X_LANES]

    @pl.when(pl.program_id(0) == 0)
    def _():
        stre[k] = h0re_ref[...]
        stim[k] = h0im_ref[...]

    for s in range(lc):
        a = us[k, pl.ds(s, cols, stride=lc), :].T
        for g in range(gps):
            ugt[g, s * ch:(s + 1) * ch, :] = a[g * ch:(g + 1) * ch, :].astype(BF16)

    half = pt_ref.shape[1] // 2
    for pr in range(gps // 2):
        put0 = _dot(pt_ref[2 * pr], ugt[2 * pr])
        put1 = _dot(pt_ref[2 * pr + 1], ugt[2 * pr + 1])
        pure[pr] = jnp.concatenate([put0[:half], put1[:half]], axis=0).T
        puim[pr] = jnp.concatenate([put0[half:], put1[half:]], axis=0).T

    for pr in range(gps // 2):
        re = stre[k, pr]
        im = stim[k, pr]
        ar = jnp.broadcast_to(are_ref[pr], re.shape)
        ai = jnp.broadcast_to(aim_ref[pr], re.shape)
        for j in range(nch):
            rows = pl.ds(j, nb, stride=nch)
            hinre[pr, rows, :] = re
            hinim[pr, rows, :] = im
            re, im = (ar * re - ai * im + pure[pr, rows, :], ar * im + ai * re + puim[pr, rows, :])
        stre[k, pr] = re
        stim[k, pr] = im
        hre_ref[k * (gps // 2) + pr] = re
        him_ref[k * (gps // 2) + pr] = im

    for pr in range(gps // 2):
        hre_t = hinre[pr].T
        him_t = hinim[pr].T
        for i in range(2):
            g = 2 * pr + i
            hin_t = jnp.concatenate([hre_t[i * half:(i + 1) * half], him_t[i * half:(i + 1) * half]],
                                    axis=0).astype(BF16)
            ytmp[g] = _dot(tt_ref[g], ugt[g]) + _dot(qt_ref[g], hin_t)

    for t in range(lc):
        z = jnp.concatenate([ytmp[g, t * ch:(t + 1) * ch, :] for g in range(gps)], axis=0)
        ys[pl.ds(t, cols, stride=lc), :] = z.T
    for b in range(nb):
        y_ref[b] = ys[b * tbt:(b + 1) * tbt, :].astype(y_ref.dtype)


def _s5_fused(x, g, mats_t, h0re, h0im, tbt):
    nb, s, d = x.shape
    tt, pt, qt, are, aim = mats_t
    groups = tt.shape[0]
    gps = V7X_LANES // S5_GROUP_CH
    slabs = groups // gps
    w = S5_CHUNK * S5_GROUP_CH
    nst = pt.shape[1]
    cols = nb * (tbt // S5_CHUNK)
    kern = functools.partial(_s5_fused_kernel, nb=nb, tbt=tbt)
    slab3 = lambda t, k: (k, 0, 0)
    blocks = (_nbytes((nb, tbt, d), F32) + _nbytes((nb, tbt, V7X_LANES), BF16)
              + gps * (_nbytes((w, w), BF16) + 2 * _nbytes((w, nst), BF16)))
    scratch = (_nbytes((slabs, nb * tbt, V7X_LANES), F32) + _nbytes((gps, w, cols), BF16)
               + 4 * _nbytes((gps // 2, cols, nst), F32) + _nbytes((gps, w, cols), F32)
               + _nbytes((nb * tbt, V7X_LANES), F32))
    return pl.pallas_call(
        kern,
        grid=(s // tbt, slabs),
        in_specs=[
            pl.BlockSpec((nb, tbt, d), lambda t, k: (0, t, 0)),
            pl.BlockSpec((1, d), lambda t, k: (0, 0)),
            pl.BlockSpec((gps, w, w), slab3),
            pl.BlockSpec((gps, nst, w), slab3),
            pl.BlockSpec((gps, w, nst), slab3),
            pl.BlockSpec((gps // 2, 1, nst), slab3),
            pl.BlockSpec((gps // 2, 1, nst), slab3),
            pl.BlockSpec((gps // 2, nb, nst), slab3),
            pl.BlockSpec((gps // 2, nb, nst), slab3),
        ],
        out_specs=[
            pl.BlockSpec((nb, tbt, V7X_LANES), lambda t, k: (0, t, k)),
            pl.BlockSpec((groups // 2, nb, nst), lambda t, k: (0, 0, 0)),
            pl.BlockSpec((groups // 2, nb, nst), lambda t, k: (0, 0, 0)),
        ],
        out_shape=[
            jax.ShapeDtypeStruct((nb, s, d), BF16),
            jax.ShapeDtypeStruct((groups // 2, nb, nst), F32),
            jax.ShapeDtypeStruct((groups // 2, nb, nst), F32),
        ],
        scratch_shapes=[
            pltpu.VMEM((slabs, nb * tbt, V7X_LANES), F32),
            pltpu.VMEM((gps, w, cols), BF16),
            pltpu.VMEM((gps // 2, cols, nst), F32),
            pltpu.VMEM((gps // 2, cols, nst), F32),
            pltpu.VMEM((gps // 2, cols, nst), F32),
            pltpu.VMEM((gps // 2, cols, nst), F32),
            pltpu.VMEM((gps, w, cols), F32),
            pltpu.VMEM((nb * tbt, V7X_LANES), F32),
            pltpu.VMEM((slabs, gps // 2, nb, nst), F32),
            pltpu.VMEM((slabs, gps // 2, nb, nst), F32),
        ],
        compiler_params=pltpu.CompilerParams(
            dimension_semantics=("arbitrary", "arbitrary"), vmem_limit_bytes=_vmem_limit(blocks, scratch)),
        name="s5_fused",
    )(x, g, tt, pt, qt, are, aim, h0re, h0im)


def _s5_matrices(a_re, a_im, log_dt, b_re, b_im, c_re, c_im):
    lc = S5_CHUNK
    groups, n = a_re.shape
    ch = S5_GROUP_CH
    a = lax.complex(a_re.astype(F32), a_im.astype(F32))
    adt = a * jnp.exp(log_dt.astype(F32))[:, None]
    j = jnp.arange(lc + 1, dtype=F32)
    apow = jnp.exp(adt[None] * j[:, None, None])
    b_bar = ((apow[1] - 1.0) / a)[..., None] * lax.complex(b_re.astype(F32), b_im.astype(F32))
    c_mat = lax.complex(c_re.astype(F32), c_im.astype(F32))
    ca = c_mat[None] * apow[:, :, None, :]
    kern = (jnp.einsum('jgcn,gnd->jgcd', ca.real, b_bar.real, precision=HI)
            - jnp.einsum('jgcn,gnd->jgcd', ca.imag, b_bar.imag, precision=HI))
    s_idx = jnp.arange(lc)[:, None]
    t_idx = jnp.arange(lc)[None, :]
    lag = t_idx - s_idx
    kt = kern[jnp.clip(lag, 0, lc)]
    kt = jnp.where((lag >= 0)[:, :, None, None, None], kt, 0.0)
    tm = jnp.transpose(kt, (2, 0, 4, 1, 3)).reshape(groups, lc * ch, lc * ch)
    pw = apow[lc - 1 - jnp.arange(lc)]
    p = pw[:, :, :, None] * b_bar[None]
    p = jnp.transpose(p, (1, 0, 3, 2)).reshape(groups, lc * ch, n)
    q = ca[1:]
    q = jnp.transpose(q, (1, 3, 0, 2)).reshape(groups, n, lc * ch)

    def pair_rows(m):
        g2 = groups // 2
        m = m.reshape(g2, 2, m.shape[1], m.shape[2])
        z = jnp.zeros_like(m[:, 0])
        top = jnp.concatenate([m[:, 0], z], axis=2)
        bot = jnp.concatenate([z, m[:, 1]], axis=2)
        return jnp.concatenate([top, bot], axis=1)

    pre = pair_rows(p.real).astype(BF16)
    pim = pair_rows(p.imag).astype(BF16)
    qre = pair_rows(q.real).astype(BF16)
    qim = pair_rows(-q.imag).astype(BF16)
    a_chunk = apow[lc].reshape(groups // 2, 1, 2 * n)
    are, aim = a_chunk.real, a_chunk.imag
    tt = jnp.swapaxes(tm, 1, 2).astype(BF16)
    pt = jnp.concatenate([jnp.swapaxes(p.real, 1, 2), jnp.swapaxes(p.imag, 1, 2)], axis=1).astype(BF16)
    qt = jnp.concatenate([jnp.swapaxes(q.real, 1, 2), jnp.swapaxes(-q.imag, 1, 2)], axis=2).astype(BF16)
    return {'row': (tm.astype(BF16), pre, pim, qre, qim, are, aim), 'col': (tt, pt, qt, are, aim)}


def _glu_kernel(x_ref, y_ref, g0_ref, d_ref, w_ref, g1_ref, o_ref):
    x = x_ref[...]
    d = x.shape[1]
    u = _rms(x, g0_ref[...])
    yy = y_ref[...].astype(F32) + d_ref[...] * u
    z = jax.nn.gelu(yy, approximate=True).astype(BF16)
    ag = _dot(z, w_ref[...])
    out = ag[:, :d] * jax.nn.sigmoid(ag[:, d:])
    o_ref[...] = x + _rms(out, g1_ref[...])


def _const_spec(shape):
    nd = len(shape)
    return pl.BlockSpec(shape, lambda *_: (0,) * nd, pipeline_mode=pl.Buffered(1))


def _glu(x2d, y2d, g0, dskip, w, g1, tb):
    rows, d = x2d.shape
    blocks = 2 * _nbytes((tb, d), F32) + _nbytes((tb, d), BF16)
    scratch = _nbytes(w.shape, BF16) + 3 * _nbytes((tb, 2 * d), F32)
    row = pl.BlockSpec((tb, d), lambda i: (i, 0))
    return pl.pallas_call(
        _glu_kernel,
        grid=(rows // tb,),
        in_specs=[row, row, _const_spec((1, d)), _const_spec((1, d)), _const_spec(w.shape), _const_spec((1, d))],
        out_specs=row,
        out_shape=jax.ShapeDtypeStruct((rows, d), F32),
        compiler_params=pltpu.CompilerParams(
            dimension_semantics=("arbitrary",), vmem_limit_bytes=_vmem_limit(blocks, scratch)),
        name="s5_glu",
    )(x2d, y2d, g0, dskip, w, g1)


def _ffn_kernel(x_ref, buf_ref, g2_ref, wup_ref, cw_ref, cb_ref, wdown_ref, g3_ref,
                o_ref, nbuf_ref, carry, act, *, tb, dff, width):
    t = pl.program_id(1)
    pad = V7X_SUBLANES

    @pl.when(t == 0)
    def _():
        carry[...] = jnp.zeros_like(carry)
        carry[pad - (width - 1):pad, :] = buf_ref[0]

    x = x_ref[0]
    hn = _rms(x, g2_ref[...]).astype(BF16)
    fb = V7X_MXU

    def conv(cs):
        u = _dot(hn, wup_ref[:, cs])
        prev = carry[:, cs]
        carry[:, cs] = u[tb - pad:tb, :]
        return _causal_dwconv(u, prev, cw_ref, cb_ref, cs)

    for j in range(dff // fb):
        gate = conv(slice(j * fb, (j + 1) * fb))
        val = conv(slice(dff + j * fb, dff + (j + 1) * fb))
        act[:, j * fb:(j + 1) * fb] = (jax.nn.gelu(gate, approximate=True) * val).astype(BF16)

    f = _dot(act[...], wdown_ref[...])
    o_ref[0] = x + _rms(f, g3_ref[...])
    nbuf_ref[0] = carry[pad - (width - 1):pad, :]


def _ffn(x, buf, g2, wup, cw, cb, wdown, g3, tb):
    b, s, d = x.shape
    dff = wdown.shape[0]
    width = cw.shape[0]
    kern = functools.partial(_ffn_kernel, tb=tb, dff=dff, width=width)
    blocks = 2 * _nbytes((tb, d), F32)
    scratch = (_nbytes(wup.shape, BF16) + _nbytes(wdown.shape, BF16) + 6 * _nbytes((tb, V7X_MXU), F32)
               + _nbytes((8, 2 * dff), F32) + _nbytes((tb, dff), BF16) + 2 * _nbytes((tb, d), F32))
    xs = pl.BlockSpec((1, tb, d), lambda i, t: (i, t, 0))
    bs = pl.BlockSpec((1, width - 1, 2 * dff), lambda i, t: (i, 0, 0))
    return pl.pallas_call(
        kern,
        grid=(b, s // tb),
        in_specs=[xs, bs, _const_spec((1, d)), _const_spec(wup.shape), _const_spec(cw.shape),
                  _const_spec((1, 2 * dff)), _const_spec(wdown.shape), _const_spec((1, d))],
        out_specs=[xs, bs],
        out_shape=[jax.ShapeDtypeStruct((b, s, d), F32), jax.ShapeDtypeStruct((b, width - 1, 2 * dff), F32)],
        scratch_shapes=[pltpu.VMEM((V7X_SUBLANES, 2 * dff), F32), pltpu.VMEM((tb, dff), BF16)],
        compiler_params=pltpu.CompilerParams(
            dimension_semantics=("arbitrary", "arbitrary"), vmem_limit_bytes=_vmem_limit(blocks, scratch)),
        name="conv_ffn",
    )(x, buf, g2, wup, cw, cb, wdown, g3)


def _gate_fold_kernel(wq_ref, wk_ref, wv_ref, gq_ref, gk_ref, gv_ref, gc_ref, gm_ref):
    dotp = functools.partial(jnp.dot, preferred_element_type=F32, precision=HI)
    gc_ref[...] = dotp(wq_ref[0], gq_ref[...]) + dotp(wk_ref[0], gk_ref[...])
    gm_ref[...] = dotp(wv_ref[0], gv_ref[...])


def _gate_fold(wq_t, wk_t, wv_t, w_gate):
    nt, w, _ = wq_t.shape
    inner = nt * w
    ng = w_gate.shape[1]
    tile = pl.BlockSpec((1, w, w), lambda i: (i, 0, 0))
    gs = [pl.BlockSpec((w, ng), lambda i, k=k: (k * nt + i, 0)) for k in range(3)]
    out = pl.BlockSpec((w, ng), lambda i: (i, 0))
    return pl.pallas_call(
        _gate_fold_kernel,
        grid=(nt,),
        in_specs=[tile, tile, tile] + gs,
        out_specs=[out, out],
        out_shape=[jax.ShapeDtypeStruct((inner, ng), F32)] * 2,
        name="mlstm_gate_fold",
    )(wq_t, wk_t, wv_t, w_gate, w_gate, w_gate)


def _mpre_kernel(x_ref, cbuf_ref, g_ref, wup_ref, cw_ref, cb_ref, wq_ref, wkt_ref, wv_ref, gc_ref, gm_ref,
                 bg_ref, q_ref, kt_ref, v_ref, xc_ref, z_ref, gates_ref, nbuf_ref, carry,
                 *, tb, inner, width):
    t = pl.program_id(1)
    pad = V7X_SUBLANES

    @pl.when(t == 0)
    def _():
        carry[...] = jnp.zeros_like(carry)
        carry[pad - (width - 1):pad, :] = cbuf_ref[0]

    h = _rms(x_ref[0], g_ref[...]).astype(BF16)
    w = V7X_MXU
    gacc = jnp.zeros((tb, bg_ref.shape[1]), F32) + bg_ref[...]
    for i in range(inner // w):
        cs = slice(i * w, (i + 1) * w)
        xm = _dot(h, wup_ref[:, cs])
        prev = carry[:, cs]
        carry[:, cs] = xm[tb - pad:tb, :]
        xc = _silu(_causal_dwconv(xm, prev, cw_ref, cb_ref, cs)).astype(BF16)
        xmb = xm.astype(BF16)
        q_ref[0, :, cs] = _dot(xc, wq_ref[i]).astype(BF16)
        kt_ref[0, cs, :] = _dot_nt(wkt_ref[i], xc).astype(BF16)
        v_ref[0, :, cs] = _dot(xmb, wv_ref[i]).astype(BF16)
        xc_ref[0, :, cs] = xc
        z_ref[0, :, cs] = _silu(_dot(h, wup_ref[:, inner + i * w:inner + (i + 1) * w])).astype(BF16)
        gacc = gacc + _dot(xc, gc_ref[cs, :]) + _dot(xmb, gm_ref[cs, :])
    gates_ref[0] = gacc
    nbuf_ref[0] = carry[pad - (width - 1):pad, :]


def _mpre(x, cbuf, g, wup, cw, cb, wq_t, wkt_t, wv_t, gc, gm, bg, tb):
    b, s, d = x.shape
    inner = cw.shape[1]
    width = cw.shape[0]
    ng = bg.shape[1]
    kern = functools.partial(_mpre_kernel, tb=tb, inner=inner, width=width)
    act = pl.BlockSpec((1, tb, inner), lambda i, t: (i, t, 0))
    blocks = _nbytes((tb, d), F32) + 5 * _nbytes((tb, inner), BF16)
    scratch = (_nbytes(wup.shape, BF16) + 3 * _nbytes(wq_t.shape, BF16) + 2 * _nbytes((inner, 128), BF16)
               + _nbytes((8, inner), F32) + 8 * _nbytes((tb, V7X_MXU), F32))
    return pl.pallas_call(
        kern,
        grid=(b, s // tb),
        in_specs=[pl.BlockSpec((1, tb, d), lambda i, t: (i, t, 0)),
                  pl.BlockSpec((1, width - 1, inner), lambda i, t: (i, 0, 0)),
                  _const_spec((1, d)), _const_spec(wup.shape), _const_spec(cw.shape), _const_spec((1, inner)),
                  _const_spec(wq_t.shape), _const_spec(wkt_t.shape), _const_spec(wv_t.shape),
                  _const_spec(gc.shape), _const_spec(gm.shape), _const_spec((1, ng))],
        out_specs=[act, pl.BlockSpec((1, inner, tb), lambda i, t: (i, 0, t)), act, act, act,
                   pl.BlockSpec((1, tb, ng), lambda i, t: (i, t, 0)),
                   pl.BlockSpec((1, width - 1, inner), lambda i, t: (i, 0, 0))],
        out_shape=[jax.ShapeDtypeStruct((b, s, inner), BF16), jax.ShapeDtypeStruct((b, inner, s), BF16),
                   jax.ShapeDtypeStruct((b, s, inner), BF16), jax.ShapeDtypeStruct((b, s, inner), BF16),
                   jax.ShapeDtypeStruct((b, s, inner), BF16), jax.ShapeDtypeStruct((b, s, ng), F32),
                   jax.ShapeDtypeStruct((b, width - 1, inner), F32)],
        scratch_shapes=[pltpu.VMEM((V7X_SUBLANES, inner), F32)],
        compiler_params=pltpu.CompilerParams(
            dimension_semantics=("arbitrary", "arbitrary"), vmem_limit_bytes=_vmem_limit(blocks, scratch)),
        name="mlstm_pre",
    )(x, cbuf, g, wup, cw, cb, wq_t, wkt_t, wv_t, gc, gm, bg)


def _mlstm_kernel(*refs, lc, heads, dh, zero_init):
    if zero_init:
        (q_ref, kt_ref, v_ref, g_ref, gt_ref,
         hn_ref, cout_ref, nout_ref, mout_ref, ct_s, n_s, m_s) = refs
    else:
        (q_ref, kt_ref, v_ref, g_ref, gt_ref, c0_ref, n0_ref, m0_ref,
         hn_ref, cout_ref, nout_ref, mout_ref, ct_s, n_s, m_s) = refs
    c = pl.program_id(1)

    @pl.when(c == 0)
    def _():
        if zero_init:
            ct_s[...] = jnp.zeros_like(ct_s)
            n_s[...] = jnp.zeros_like(n_s)
            m_s[...] = jnp.zeros_like(m_s)
        else:
            for h in range(heads):
                ct_s[h] = c0_ref[0, h].T
            n_s[...] = n0_ref[0]
            m_s[...] = m0_ref[0]

    g = g_ref[0]
    gt = gt_ref[0]
    row = lax.broadcasted_iota(jnp.int32, (lc, lc), 0)
    col = lax.broadcasted_iota(jnp.int32, (lc, lc), 1)
    causal = col <= row
    tri_l = jnp.where(causal, 1.0, 0.0).astype(BF16)
    tri_u = jnp.where(row <= col, 1.0, 0.0).astype(BF16)
    b_cols = sum(_dot(tri_l, part) for part in _split3(_log_sigmoid(g)))
    b_rows = sum(_dot(part, tri_u) for part in _split3(_log_sigmoid(gt)))

    hsl = [slice(h * dh, (h + 1) * dh) for h in range(heads)]
    st = []
    for h in range(heads):
        b_col = b_cols[:, heads + h:heads + h + 1]
        b_row = b_rows[heads + h:heads + h + 1, :]
        i_row = gt[h:h + 1, :]
        m_prev = m_s[h:h + 1, 0:1]
        dmat = jnp.where(causal, b_col - b_row + i_row, -jnp.inf)
        inter = b_col + m_prev
        m_t = jnp.maximum(inter, jnp.max(dmat, axis=1, keepdims=True))
        st.append(dict(b_col=b_col, b_row=b_row, i_row=i_row, m_prev=m_prev, m_t=m_t,
                       wts=jnp.exp(dmat - m_t), a=jnp.exp(inter - m_t)))
    for h in range(heads):
        s = st[h]
        s['sc'] = _dot(q_ref[0, :, hsl[h]], kt_ref[0, hsl[h], :]) * s['wts']
    for h in range(heads):
        s = st[h]
        qh = q_ref[0, :, hsl[h]]
        sc = s['sc']
        a = s['a']
        num = _dot(sc.astype(BF16), v_ref[0, :, hsl[h]]) + a * _dot(qh, ct_s[h].astype(BF16))
        qn = jnp.sum(qh.astype(F32) * n_s[h:h + 1, :], axis=1, keepdims=True)
        den = jnp.sum(sc, axis=1, keepdims=True) + a * qn
        hh = num / jnp.maximum(jnp.abs(den), jnp.exp(-s['m_t']))
        mu = jnp.mean(hh, axis=1, keepdims=True)
        dev = hh - mu
        var = jnp.mean(dev * dev, axis=1, keepdims=True)
        hn_ref[0, :, hsl[h]] = (dev * lax.rsqrt(var + LN_EPS)).astype(hn_ref.dtype)
    for h in range(heads):
        s = st[h]
        kth = kt_ref[0, hsl[h], :]
        b_last = s['b_row'][:, lc - 1:lc]
        g_row = b_last - s['b_row'] + s['i_row']
        m_new = jnp.maximum(b_last + s['m_prev'], jnp.max(g_row, axis=1, keepdims=True))
        decay = jnp.exp(b_last + s['m_prev'] - m_new)
        wg_row = jnp.exp(g_row - m_new)
        wg_col = jnp.exp(b_last - s['b_col'] + g[:, h:h + 1] - m_new)
        wv = (v_ref[0, :, hsl[h]].astype(F32) * wg_col).astype(BF16)
        ct_s[h] = decay * ct_s[h] + _dot(kth, wv)
        wg16 = jnp.broadcast_to(wg_row, (16, lc)).astype(BF16)
        n_s[h:h + 1, :] = decay * n_s[h:h + 1, :] + _dot_nt(wg16, kth)[0:1, :]
        m_s[h:h + 1, :] = jnp.broadcast_to(m_new, (1, V7X_LANES))

    @pl.when(c == pl.num_programs(1) - 1)
    def _():
        for h in range(heads):
            cout_ref[0, h] = ct_s[h].T
        nout_ref[0] = n_s[...]
        mout_ref[0] = m_s[...]


def _mlstm_core(q, kt, v, gates, gates_t, state, lc, heads):
    b, s, inner = q.shape
    dh = inner // heads
    ng = gates.shape[2]
    zero_init = state is None
    kern = functools.partial(_mlstm_kernel, lc=lc, heads=heads, dh=dh, zero_init=zero_init)
    act = pl.BlockSpec((1, lc, inner), lambda i, c: (i, c, 0))
    cs = pl.BlockSpec((1, heads, dh, dh), lambda i, c: (i, 0, 0, 0))
    ns = pl.BlockSpec((1, heads, dh), lambda i, c: (i, 0, 0))
    ms = pl.BlockSpec((1, heads, V7X_LANES), lambda i, c: (i, 0, 0))
    in_specs = [act, pl.BlockSpec((1, inner, lc), lambda i, c: (i, 0, c)), act,
                pl.BlockSpec((1, lc, ng), lambda i, c: (i, c, 0)),
                pl.BlockSpec((1, ng, lc), lambda i, c: (i, 0, c))]
    args = [q, kt, v, gates, gates_t]
    if not zero_init:
        in_specs += [cs, ns, ms]
        args += list(state)
    blocks = 4 * _nbytes((lc, inner), BF16) + (1 if zero_init else 2) * _nbytes((heads, dh, dh), F32)
    scratch = _nbytes((heads, dh, dh), F32) + 6 * _nbytes((lc, max(lc, dh)), F32) + 2 * _nbytes((dh, dh), F32)
    return pl.pallas_call(
        kern,
        grid=(b, s // lc),
        in_specs=in_specs,
        out_specs=[act, cs, ns, ms],
        out_shape=[jax.ShapeDtypeStruct((b, s, inner), BF16), jax.ShapeDtypeStruct((b, heads, dh, dh), F32),
                   jax.ShapeDtypeStruct((b, heads, dh), F32), jax.ShapeDtypeStruct((b, heads, V7X_LANES), F32)],
        scratch_shapes=[pltpu.VMEM((heads, dh, dh), F32), pltpu.VMEM((heads, dh), F32),
                        pltpu.VMEM((heads, V7X_LANES), F32)],
        compiler_params=pltpu.CompilerParams(
            dimension_semantics=("arbitrary", "arbitrary"), vmem_limit_bytes=_vmem_limit(blocks, scratch)),
        name="mlstm_core",
    )(*args)


def _mout_kernel(hn_ref, xc_ref, z_ref, x_ref, ng_ref, sk_ref, w_ref, g1_ref, o_ref):
    hs = hn_ref[...].astype(F32) * ng_ref[...] + sk_ref[...] * xc_ref[...].astype(F32)
    a = (hs * z_ref[...].astype(F32)).astype(BF16)
    x = x_ref[...]
    o_ref[...] = x + _rms(_dot(a, w_ref[...]), g1_ref[...])


def _mout(hn, xc, sz, x2d, ng, sk, w, g1, tb):
    rows, d = x2d.shape
    inner = hn.shape[1]
    act = pl.BlockSpec((tb, inner), lambda i: (i, 0))
    row = pl.BlockSpec((tb, d), lambda i: (i, 0))
    blocks = 3 * _nbytes((tb, inner), BF16) + 2 * _nbytes((tb, d), F32)
    scratch = _nbytes(w.shape, BF16) + 3 * _nbytes((tb, inner), F32)
    return pl.pallas_call(
        _mout_kernel,
        grid=(rows // tb,),
        in_specs=[act, act, act, row, _const_spec((1, inner)), _const_spec((1, inner)), _const_spec(w.shape),
                  _const_spec((1, d))],
        out_specs=row,
        out_shape=jax.ShapeDtypeStruct((rows, d), F32),
        compiler_params=pltpu.CompilerParams(
            dimension_semantics=("arbitrary",), vmem_limit_bytes=_vmem_limit(blocks, scratch)),
        name="mlstm_out",
    )(hn, xc, sz, x2d, ng, sk, w, g1)


def _block_diag_tiles(w):
    nblk, k, _ = w.shape
    per = V7X_MXU // k
    wt = w.reshape(nblk // per, per, k, k)
    eye = jnp.eye(per, dtype=w.dtype)
    full = wt[:, :, :, None, :] * eye[None, :, None, :, None]
    return full.reshape(nblk // per, per * k, per * k)


def _prep_weights(w):
    row = lambda v: v.astype(F32).reshape(1, -1)
    p = {}
    p['norm_g'] = w['norm_g'].astype(F32)
    p['s5_mats'] = _s5_matrices(w['s5_A_re'][0], w['s5_A_im'][0], w['s5_log_dt'][0], w['s5_B_re'][0],
                                w['s5_B_im'][0], w['s5_C_re'][0], w['s5_C_im'][0])
    p['s5_D'] = row(w['s5_D'][0])
    p['s5_w_glu'] = w['s5_w_glu'][0].astype(BF16)
    inner = w['mlstm_conv_w'].shape[2]
    heads = w['mlstm_b_gate'].shape[1] // 2
    dh = inner // heads
    wq_t = _block_diag_tiles(w['mlstm_wq'][0].astype(F32))
    wk_t = _block_diag_tiles(w['mlstm_wk'][0].astype(F32))
    wv_t = _block_diag_tiles(w['mlstm_wv'][0].astype(F32))
    gc, gm = _gate_fold(wq_t, wk_t, wv_t, w['mlstm_w_gate'][0].astype(F32))
    p['m_wq'] = wq_t.astype(BF16)
    p['m_wkt'] = (jnp.swapaxes(wk_t, 1, 2) * (dh ** -0.5)).astype(BF16)
    p['m_wv'] = wv_t.astype(BF16)
    p['m_gc'] = gc.astype(BF16)
    p['m_gm'] = gm.astype(BF16)
    p['m_bg'] = row(w['mlstm_b_gate'][0])
    p['m_wup'] = w['mlstm_w_up'][0].astype(BF16)
    p['m_cw'] = w['mlstm_conv_w'][0].astype(F32)
    p['m_cb'] = row(w['mlstm_conv_b'][0])
    p['m_ng'] = row(w['mlstm_norm_g'][0])
    p['m_skip'] = row(w['mlstm_skip'][0])
    p['m_wdown'] = w['mlstm_w_down'][0].astype(BF16)
    p['heads'] = heads
    p['f_wup'] = w['ffn_w_up'].astype(BF16)
    p['f_cw'] = w['ffn_conv_w'].astype(F32)
    p['f_cb'] = w['ffn_conv_b'].astype(F32)
    p['f_wdown'] = w['ffn_w_down'].astype(BF16)
    return p


def _to_chunk_major(u2d, b, s):
    d = u2d.shape[1]
    g = d // S5_GROUP_CH
    u = u2d.reshape(b, s // S5_CHUNK, S5_CHUNK, g, S5_GROUP_CH)
    return jnp.transpose(u, (1, 0, 3, 2, 4)).reshape((s // S5_CHUNK) * b, d * S5_CHUNK)


def _to_token_major(yt, b, s):
    d = yt.shape[1] // S5_CHUNK
    g = d // S5_GROUP_CH
    y = yt.reshape(s // S5_CHUNK, b, g, S5_CHUNK, S5_GROUP_CH)
    return jnp.transpose(y, (1, 0, 3, 2, 4)).reshape(b * s, d)


def _trunk(x, st, p, tiles):
    b, s, d = x.shape
    ng = p['norm_g']
    g_of = lambda layer, k: ng[layer, k].reshape(1, d)
    zero_init = st is None
    groups = d // S5_GROUP_CH
    x2d = x.reshape(b * s, d)

    n_state = p['s5_mats']['row'][1].shape[2] // 2
    if zero_init:
        h0re = h0im = jnp.zeros((groups // 2, b, 2 * n_state), F32)
    else:
        pair = lambda h: jnp.transpose(h.astype(F32).reshape(b, groups // 2, 2 * n_state), (1, 0, 2))
        h0re, h0im = pair(st['s5_re']), pair(st['s5_im'])
    if tiles.s5_time:
        y, hre, him = _s5_fused(x, g_of(0, 0), p['s5_mats']['col'], h0re, h0im, tiles.s5_time)
        y2d = y.reshape(b * s, d)
    else:
        u = _norm_cast(x2d, g_of(0, 0), tiles.rows)
        yt, hre, him = _s5_core(_to_chunk_major(u, b, s), p['s5_mats']['row'], h0re, h0im, b, tiles.s5_rows)
        y2d = _to_token_major(yt, b, s)
    unpair = lambda h: jnp.transpose(h, (1, 0, 2)).reshape(1, b, groups, n_state)
    o_re, o_im = unpair(hre), unpair(him)
    x2d = _glu(x2d, y2d, g_of(0, 0), p['s5_D'], p['s5_w_glu'], g_of(0, 1), tiles.rows)

    def ffn(x2d, layer):
        dff2 = p['f_wup'].shape[2]
        width = p['f_cw'].shape[1]
        buf = jnp.zeros((b, width - 1, dff2), F32) if zero_init else st['ffn_conv'][layer].astype(F32)
        xo, nbuf = _ffn(x2d.reshape(b, s, d), buf, g_of(layer, 2), p['f_wup'][layer], p['f_cw'][layer],
                        p['f_cb'][layer].reshape(1, dff2), p['f_wdown'][layer], g_of(layer, 3), tiles.ffn_time)
        return xo.reshape(b * s, d), nbuf

    x2d, fbuf0 = ffn(x2d, 0)

    heads = p['heads']
    inner = p['m_cw'].shape[1]
    width = p['m_cw'].shape[0]
    cbuf = jnp.zeros((b, width - 1, inner), F32) if zero_init else st['mlstm_conv'].astype(F32)
    q, kt, v, xc, sz, gates, ncbuf = _mpre(
        x2d.reshape(b, s, d), cbuf, g_of(1, 0), p['m_wup'], p['m_cw'], p['m_cb'], p['m_wq'], p['m_wkt'],
        p['m_wv'], p['m_gc'], p['m_gm'], p['m_bg'], tiles.time)
    if zero_init:
        state = None
    else:
        state = (st['mlstm_C'].astype(F32), st['mlstm_n'].astype(F32),
                 jnp.broadcast_to(st['mlstm_m'].astype(F32)[:, :, None], (b, heads, V7X_LANES)))
    hn, c_out, n_out, m_out = _mlstm_core(q, kt, v, gates, jnp.swapaxes(gates, 1, 2), state,
                                          tiles.mlstm_chunk, heads)
    x2d = _mout(hn.reshape(b * s, inner), xc.reshape(b * s, inner), sz.reshape(b * s, inner), x2d,
                p['m_ng'], p['m_skip'], p['m_wdown'], g_of(1, 1), tiles.rows)
    x2d, fbuf1 = ffn(x2d, 1)

    return (x2d.reshape(b, s, d), o_re, o_im, c_out[None], n_out[None], m_out[None, :, :, 0], ncbuf[None],
            jnp.stack([fbuf0, fbuf1], axis=0))


def kernel(x_prompt, x_sample, state_s5_re, state_s5_im, state_mlstm_C, state_mlstm_n, state_mlstm_m, state_mlstm_conv, state_ffn_conv, norm_g, s5_A_re, s5_A_im, s5_log_dt, s5_B_re, s5_B_im, s5_C_re, s5_C_im, s5_D, s5_w_glu, mlstm_w_up, mlstm_conv_w, mlstm_conv_b, mlstm_wq, mlstm_wk, mlstm_wv, mlstm_w_gate, mlstm_b_gate, mlstm_norm_g, mlstm_skip, mlstm_w_down, ffn_w_up, ffn_conv_w, ffn_conv_b, ffn_w_down):
    w = {
        'norm_g': norm_g, 's5_A_re': s5_A_re, 's5_A_im': s5_A_im, 's5_log_dt': s5_log_dt,
        's5_B_re': s5_B_re, 's5_B_im': s5_B_im, 's5_C_re': s5_C_re, 's5_C_im': s5_C_im, 's5_D': s5_D,
        's5_w_glu': s5_w_glu, 'mlstm_w_up': mlstm_w_up, 'mlstm_conv_w': mlstm_conv_w,
        'mlstm_conv_b': mlstm_conv_b, 'mlstm_wq': mlstm_wq, 'mlstm_wk': mlstm_wk, 'mlstm_wv': mlstm_wv,
        'mlstm_w_gate': mlstm_w_gate, 'mlstm_b_gate': mlstm_b_gate, 'mlstm_norm_g': mlstm_norm_g,
        'mlstm_skip': mlstm_skip, 'mlstm_w_down': mlstm_w_down, 'ffn_w_up': ffn_w_up,
        'ffn_conv_w': ffn_conv_w, 'ffn_conv_b': ffn_conv_b, 'ffn_w_down': ffn_w_down,
    }
    assert norm_g.shape[0] == 2 and s5_A_re.shape[0] == 1 and mlstm_w_up.shape[0] == 1
    assert x_prompt.shape[0] == V7X_SUBLANES and x_sample.shape[0] == V7X_SUBLANES
    p = _prep_weights(w)
    out_p = _trunk(x_prompt.astype(F32), None, p, _pick_tiles(*x_prompt.shape[:2]))
    st = {'s5_re': state_s5_re[0], 's5_im': state_s5_im[0], 'mlstm_C': state_mlstm_C[0],
          'mlstm_n': state_mlstm_n[0], 'mlstm_m': state_mlstm_m[0], 'mlstm_conv': state_mlstm_conv[0],
          'ffn_conv': state_ffn_conv}
    out_s = _trunk(x_sample.astype(F32), st, p, _pick_tiles(*x_sample.shape[:2]))
    return (out_p[0], out_s[0]) + tuple(out_p[1:]) + tuple(out_s[1:])
```

```python
import functools
import math
from typing import NamedTuple

import jax
import jax.numpy as jnp
from jax import lax
from jax.experimental import pallas as pl
from jax.experimental.pallas import tpu as pltpu

F32 = jnp.float32
BF16 = jnp.bfloat16
HI = lax.Precision.HIGHEST

NORM_EPS = 1e-6
LN_EPS = 1e-5
S5_GROUP_CH = 16
S5_CHUNK = 16
QKV_BLOCK = 4

V7X_LANES = 128
V7X_SUBLANES = 8
V7X_MXU = 256
V7X_VMEM_BYTES = 64 * 1024 * 1024


class Tiles(NamedTuple):
    rows: int
    time: int
    s5_rows: int
    mlstm_chunk: int
    s5_time: int
    ffn_time: int


def _pick_tiles(batch, seq):
    s5_time = (V7X_LANES // batch) * S5_CHUNK
    return Tiles(rows=min(batch * seq, 512), time=min(seq, 512),
                 s5_rows=min(batch * (seq // S5_CHUNK), 512), mlstm_chunk=min(seq, 256),
                 s5_time=s5_time if seq % s5_time == 0 else 0, ffn_time=min(seq, 1024))


def _vmem_limit(block_bytes, scratch_bytes):
    want = 2 * block_bytes + scratch_bytes + 16 * 1024 * 1024
    return int(min(want, V7X_VMEM_BYTES - 8 * 1024 * 1024))


def _nbytes(shape, dtype):
    return math.prod(shape) * jnp.dtype(dtype).itemsize


def _rms(x, g):
    return x * lax.rsqrt(jnp.mean(x * x, axis=-1, keepdims=True) + NORM_EPS) * g


def _dot(a, b):
    return jnp.dot(a, b, preferred_element_type=F32)


def _dot_nt(a, b):
    return lax.dot_general(a, b, (((1,), (1,)), ((), ())), preferred_element_type=F32)


def _silu(x):
    return (0.5 * x) * (1.0 + jnp.tanh(0.5 * x))


def _causal_dwconv(u, prev, cw_ref, cb_ref, cs):
    width = cw_ref.shape[0]
    acc = cb_ref[:, cs] + u * cw_ref[width - 1:width, cs]
    rows = lax.broadcasted_iota(jnp.int32, prev.shape, 0)
    for j in range(width - 1):
        d = width - 1 - j
        rolled = pltpu.roll(u, d, 0)
        head = jnp.where(rows < d, pltpu.roll(prev, d, 0), rolled[0:V7X_SUBLANES])
        shifted = jnp.concatenate([head, rolled[V7X_SUBLANES:]], axis=0)
        acc = acc + shifted * cw_ref[j:j + 1, cs]
    return acc


def _log_sigmoid(x):
    return jnp.minimum(x, 0.0) - jnp.log1p(jnp.exp(-jnp.abs(x)))


def _split3(x):
    hi = x.astype(BF16)
    r1 = x - hi.astype(F32)
    mid = r1.astype(BF16)
    lo = (r1 - mid.astype(F32)).astype(BF16)
    return hi, mid, lo


def _norm_cast_kernel(x_ref, g_ref, o_ref):
    o_ref[...] = _rms(x_ref[...], g_ref[...]).astype(o_ref.dtype)


def _norm_cast(x2d, g, tb):
    rows, d = x2d.shape
    return pl.pallas_call(
        _norm_cast_kernel,
        grid=(rows // tb,),
        in_specs=[pl.BlockSpec((tb, d), lambda i: (i, 0)), pl.BlockSpec((1, d), lambda i: (0, 0))],
        out_specs=pl.BlockSpec((tb, d), lambda i: (i, 0)),
        out_shape=jax.ShapeDtypeStruct((rows, d), BF16),
        name="s5_norm",
    )(x2d, g)


def _s5_kernel(u_ref, t_ref, pre_ref, pim_ref, qre_ref, qim_ref, are_ref, aim_ref, h0re_ref, h0im_ref,
               y_ref, hre_ref, him_ref, pure_s, puim_s, hinre_s, hinim_s, stre_s, stim_s, *, nb, rb):
    r = pl.program_id(1)

    @pl.when(r == 0)
    def _():
        stre_s[...] = h0re_ref[0]
        stim_s[...] = h0im_ref[0]

    u = u_ref[...]
    pure_s[...] = _dot(u, pre_ref[0])
    puim_s[...] = _dot(u, pim_ref[0])
    ar = jnp.broadcast_to(are_ref[0], (nb, 2 * 64))
    ai = jnp.broadcast_to(aim_ref[0], (nb, 2 * 64))

    def step(i, carry):
        re, im = carry
        rows = pl.ds(pl.multiple_of(i * nb, nb), nb)
        hinre_s[rows, :] = re
        hinim_s[rows, :] = im
        return (ar * re - ai * im + pure_s[rows, :], ar * im + ai * re + puim_s[rows, :])

    re, im = lax.fori_loop(0, rb // nb, step, (stre_s[...], stim_s[...]))
    stre_s[...] = re
    stim_s[...] = im
    hre_ref[0] = re
    him_ref[0] = im

    carry_in = _dot(hinre_s[...].astype(BF16), qre_ref[0]) + _dot(hinim_s[...].astype(BF16), qim_ref[0])
    w = V7X_MXU
    for k in range(2):
        y_ref[:, k * w:(k + 1) * w] = (_dot(u[:, k * w:(k + 1) * w], t_ref[k])
                                       + carry_in[:, k * w:(k + 1) * w]).astype(y_ref.dtype)


def _s5_core(ut, mats, h0re, h0im, nb, rb):
    rows, cols = ut.shape
    pairs = cols // (2 * V7X_MXU)
    tm, pre, pim, qre, qim, are, aim = mats
    pair3 = lambda p, r: (p, 0, 0)
    kern = functools.partial(_s5_kernel, nb=nb, rb=rb)
    blocks = (_nbytes((rb, 512), BF16) * 2 + _nbytes((2, 256, 256), BF16) + 4 * _nbytes((512, 128), BF16))
    scratch = 4 * _nbytes((rb, 128), F32) + 2 * _nbytes((nb, 128), F32)
    return pl.pallas_call(
        kern,
        grid=(pairs, rows // rb),
        in_specs=[
            pl.BlockSpec((rb, 2 * V7X_MXU), lambda p, r: (r, p)),
            pl.BlockSpec((2, V7X_MXU, V7X_MXU), pair3),
            pl.BlockSpec((1, 2 * V7X_MXU, 128), pair3),
            pl.BlockSpec((1, 2 * V7X_MXU, 128), pair3),
            pl.BlockSpec((1, 128, 2 * V7X_MXU), pair3),
            pl.BlockSpec((1, 128, 2 * V7X_MXU), pair3),
            pl.BlockSpec((1, 1, 128), pair3),
            pl.BlockSpec((1, 1, 128), pair3),
            pl.BlockSpec((1, nb, 128), pair3),
            pl.BlockSpec((1, nb, 128), pair3),
        ],
        out_specs=[
            pl.BlockSpec((rb, 2 * V7X_MXU), lambda p, r: (r, p)),
            pl.BlockSpec((1, nb, 128), pair3),
            pl.BlockSpec((1, nb, 128), pair3),
        ],
        out_shape=[
            jax.ShapeDtypeStruct((rows, cols), BF16),
            jax.ShapeDtypeStruct((pairs, nb, 128), F32),
            jax.ShapeDtypeStruct((pairs, nb, 128), F32),
        ],
        scratch_shapes=[pltpu.VMEM((rb, 128), F32)] * 4 + [pltpu.VMEM((nb, 128), F32)] * 2,
        compiler_params=pltpu.CompilerParams(
            dimension_semantics=("arbitrary", "arbitrary"), vmem_limit_bytes=_vmem_limit(blocks, scratch)),
        name="s5_core",
    )(ut, tm, pre, pim, qre, qim, are, aim, h0re, h0im)


def _s5_fused_kernel(x_ref, g_ref, tt_ref, pt_ref, qt_ref, are_ref, aim_ref, h0re_ref, h0im_ref,
                     y_ref, hre_ref, him_ref,
                     us, ugt, pure, puim, hinre, hinim, ytmp, ys, stre, stim, *, nb, tbt):
    lc = S5_CHUNK
    ch = S5_GROUP_CH
    gps = V7X_LANES // ch
    nch = tbt // lc
    cols = nb * nch
    k = pl.program_id(1)

    @pl.when(k == 0)
    def _():
        for b in range(nb):
            u = _rms(x_ref[b], g_ref[...])
            for kk in range(us.shape[0]):
                us[kk, b * tbt:(b + 1) * tbt, :] = u[:, kk * V7X_LANES:(kk + 1) * V7X_LANES]

    @pl.when(pl.program_id(0) == 0)
    def _():
        stre[k] = h0re_ref[...]
        stim[k] = h0im_ref[...]

    for s in range(lc):
        a = us[k, pl.ds(s, cols, stride=lc), :].T
        for g in range(gps):
            ugt[g, s * ch:(s + 1) * ch, :] = a[g * ch:(g + 1) * ch, :].astype(BF16)

    half = pt_ref.shape[1] // 2
    for pr in range(gps // 2):
        put0 = _dot(pt_ref[2 * pr], ugt[2 * pr])
        put1 = _dot(pt_ref[2 * pr + 1], ugt[2 * pr + 1])
        pure[pr] = jnp.concatenate([put0[:half], put1[:half]], axis=0).T
        puim[pr] = jnp.concatenate([put0[half:], put1[half:]], axis=0).T

    for pr in range(gps // 2):
        re = stre[k, pr]
        im = stim[k, pr]
        ar = jnp.broadcast_to(are_ref[pr], re.shape)
        ai = jnp.broadcast_to(aim_ref[pr], re.shape)
        for j in range(nch):
            rows = pl.ds(j, nb, stride=nch)
            hinre[pr, rows, :] = re
            hinim[pr, rows, :] = im
            re, im = (ar * re - ai * im + pure[pr, rows, :], ar * im + ai * re + puim[pr, rows, :])
        stre[k, pr] = re
        stim[k, pr] = im
        hre_ref[k * (gps // 2) + pr] = re
        him_ref[k * (gps // 2) + pr] = im

    for pr in range(gps // 2):
        hre_t = hinre[pr].T
        him_t = hinim[pr].T
        for i in range(2):
            g = 2 * pr + i
            hin_t = jnp.concatenate([hre_t[i * half:(i + 1) * half], him_t[i * half:(i + 1) * half]],
                                    axis=0).astype(BF16)
            ytmp[g] = _dot(tt_ref[g], ugt[g]) + _dot(qt_ref[g], hin_t)

    for t in range(lc):
        z = jnp.concatenate([ytmp[g, t * ch:(t + 1) * ch, :] for g in range(gps)], axis=0)
        ys[pl.ds(t, cols, stride=lc), :] = z.T
    for b in range(nb):
        y_ref[b] = ys[b * tbt:(b + 1) * tbt, :].astype(y_ref.dtype)


def _s5_fused(x, g, mats_t, h0re, h0im, tbt):
    nb, s, d = x.shape
    tt, pt, qt, are, aim = mats_t
    groups = tt.shape[0]
    gps = V7X_LANES // S5_GROUP_CH
    slabs = groups // gps
    w = S5_CHUNK * S5_GROUP_CH
    nst = pt.shape[1]
    cols = nb * (tbt // S5_CHUNK)
    kern = functools.partial(_s5_fused_kernel, nb=nb, tbt=tbt)
    slab3 = lambda t, k: (k, 0, 0)
    blocks = (_nbytes((nb, tbt, d), F32) + _nbytes((nb, tbt, V7X_LANES), BF16)
              + gps * (_nbytes((w, w), BF16) + 2 * _nbytes((w, nst), BF16)))
    scratch = (_nbytes((slabs, nb * tbt, V7X_LANES), F32) + _nbytes((gps, w, cols), BF16)
               + 4 * _nbytes((gps // 2, cols, nst), F32) + _nbytes((gps, w, cols), F32)
               + _nbytes((nb * tbt, V7X_LANES), F32))
    return pl.pallas_call(
        kern,
        grid=(s // tbt, slabs),
        in_specs=[
            pl.BlockSpec((nb, tbt, d), lambda t, k: (0, t, 0)),
            pl.BlockSpec((1, d), lambda t, k: (0, 0)),
            pl.BlockSpec((gps, w, w), slab3),
            pl.BlockSpec((gps, nst, w), slab3),
            pl.BlockSpec((gps, w, nst), slab3),
            pl.BlockSpec((gps // 2, 1, nst), slab3),
            pl.BlockSpec((gps // 2, 1, nst), slab3),
            pl.BlockSpec((gps // 2, nb, nst), slab3),
            pl.BlockSpec((gps // 2, nb, nst), slab3),
        ],
        out_specs=[
            pl.BlockSpec((nb, tbt, V7X_LANES), lambda t, k: (0, t, k)),
            pl.BlockSpec((groups // 2, nb, nst), lambda t, k: (0, 0, 0)),
            pl.BlockSpec((groups // 2, nb, nst), lambda t, k: (0, 0, 0)),
        ],
        out_shape=[
            jax.ShapeDtypeStruct((nb, s, d), BF16),
            jax.ShapeDtypeStruct((groups // 2, nb, nst), F32),
            jax.ShapeDtypeStruct((groups // 2, nb, nst), F32),
        ],
        scratch_shapes=[
            pltpu.VMEM((slabs, nb * tbt, V7X_LANES), F32),
            pltpu.VMEM((gps, w, cols), BF16),
            pltpu.VMEM((gps // 2, cols, nst), F32),
            pltpu.VMEM((gps // 2, cols, nst), F32),
            pltpu.VMEM((gps // 2, cols, nst), F32),
            pltpu.VMEM((gps // 2, cols, nst), F32),
            pltpu.VMEM((gps, w, cols), F32),
            pltpu.VMEM((nb * tbt, V7X_LANES), F32),
            pltpu.VMEM((slabs, gps // 2, nb, nst), F32),
            pltpu.VMEM((slabs, gps // 2, nb, nst), F32),
        ],
        compiler_params=pltpu.CompilerParams(
            dimension_semantics=("arbitrary", "arbitrary"), vmem_limit_bytes=_vmem_limit(blocks, scratch)),
        name="s5_fused",
    )(x, g, tt, pt, qt, are, aim, h0re, h0im)


def _s5_matrices(a_re, a_im, log_dt, b_re, b_im, c_re, c_im):
    lc = S5_CHUNK
    groups, n = a_re.shape
    ch = S5_GROUP_CH
    a = lax.complex(a_re.astype(F32), a_im.astype(F32))
    adt = a * jnp.exp(log_dt.astype(F32))[:, None]
    j = jnp.arange(lc + 1, dtype=F32)
    apow = jnp.exp(adt[None] * j[:, None, None])
    b_bar = ((apow[1] - 1.0) / a)[..., None] * lax.complex(b_re.astype(F32), b_im.astype(F32))
    c_mat = lax.complex(c_re.astype(F32), c_im.astype(F32))
    ca = c_mat[None] * apow[:, :, None, :]
    kern = (jnp.einsum('jgcn,gnd->jgcd', ca.real, b_bar.real, precision=HI)
            - jnp.einsum('jgcn,gnd->jgcd', ca.imag, b_bar.imag, precision=HI))
    s_idx = jnp.arange(lc)[:, None]
    t_idx = jnp.arange(lc)[None, :]
    lag = t_idx - s_idx
    kt = kern[jnp.clip(lag, 0, lc)]
    kt = jnp.where((lag >= 0)[:, :, None, None, None], kt, 0.0)
    tm = jnp.transpose(kt, (2, 0, 4, 1, 3)).reshape(groups, lc * ch, lc * ch)
    pw = apow[lc - 1 - jnp.arange(lc)]
    p = pw[:, :, :, None] * b_bar[None]
    p = jnp.transpose(p, (1, 0, 3, 2)).reshape(groups, lc * ch, n)
    q = ca[1:]
    q = jnp.transpose(q, (1, 3, 0, 2)).reshape(groups, n, lc * ch)

    def pair_rows(m):
        g2 = groups // 2
        m = m.reshape(g2, 2, m.shape[1], m.shape[2])
        z = jnp.zeros_like(m[:, 0])
        top = jnp.concatenate([m[:, 0], z], axis=2)
        bot = jnp.concatenate([z, m[:, 1]], axis=2)
        return jnp.concatenate([top, bot], axis=1)

    pre = pair_rows(p.real).astype(BF16)
    pim = pair_rows(p.imag).astype(BF16)
    qre = pair_rows(q.real).astype(BF16)
    qim = pair_rows(-q.imag).astype(BF16)
    a_chunk = apow[lc].reshape(groups // 2, 1, 2 * n)
    are, aim = a_chunk.real, a_chunk.imag
    tt = jnp.swapaxes(tm, 1, 2).astype(BF16)
    pt = jnp.concatenate([jnp.swapaxes(p.real, 1, 2), jnp.swapaxes(p.imag, 1, 2)], axis=1).astype(BF16)
    qt = jnp.concatenate([jnp.swapaxes(q.real, 1, 2), jnp.swapaxes(-q.imag, 1, 2)], axis=2).astype(BF16)
    return {'row': (tm.astype(BF16), pre, pim, qre, qim, are, aim), 'col': (tt, pt, qt, are, aim)}


def _glu_kernel(x_ref, y_ref, g0_ref, d_ref, w_ref, g1_ref, o_ref):
    x = x_ref[...]
    d = x.shape[1]
    u = _rms(x, g0_ref[...])
    yy = y_ref[...].astype(F32) + d_ref[...] * u
    z = jax.nn.gelu(yy, approximate=True).astype(BF16)
    ag = _dot(z, w_ref[...])
    out = ag[:, :d] * jax.nn.sigmoid(ag[:, d:])
    o_ref[...] = x + _rms(out, g1_ref[...])


def _const_spec(shape):
    nd = len(shape)
    return pl.BlockSpec(shape, lambda *_: (0,) * nd, pipeline_mode=pl.Buffered(1))


def _glu(x2d, y2d, g0, dskip, w, g1, tb):
    rows, d = x2d.shape
    blocks = 2 * _nbytes((tb, d), F32) + _nbytes((tb, d), BF16)
    scratch = _nbytes(w.shape, BF16) + 3 * _nbytes((tb, 2 * d), F32)
    row = pl.BlockSpec((tb, d), lambda i: (i, 0))
    return pl.pallas_call(
        _glu_kernel,
        grid=(rows // tb,),
        in_specs=[row, row, _const_spec((1, d)), _const_spec((1, d)), _const_spec(w.shape), _const_spec((1, d))],
        out_specs=row,
        out_shape=jax.ShapeDtypeStruct((rows, d), F32),
        compiler_params=pltpu.CompilerParams(
            dimension_semantics=("arbitrary",), vmem_limit_bytes=_vmem_limit(blocks, scratch)),
        name="s5_glu",
    )(x2d, y2d, g0, dskip, w, g1)


def _ffn_kernel(x_ref, buf_ref, g2_ref, wup_ref, cw_ref, cb_ref, wdown_ref, g3_ref,
                o_ref, nbuf_ref, carry, act, *, tb, dff, width):
    t = pl.program_id(1)
    pad = V7X_SUBLANES

    @pl.when(t == 0)
    def _():
        carry[...] = jnp.zeros_like(carry)
        carry[pad - (width - 1):pad, :] = buf_ref[0]

    x = x_ref[0]
    hn = _rms(x, g2_ref[...]).astype(BF16)
    fb = V7X_MXU

    def conv(cs):
        u = _dot(hn, wup_ref[:, cs])
        prev = carry[:, cs]
        carry[:, cs] = u[tb - pad:tb, :]
        return _causal_dwconv(u, prev, cw_ref, cb_ref, cs)

    for j in range(dff // fb):
        gate = conv(slice(j * fb, (j + 1) * fb))
        val = conv(slice(dff + j * fb, dff + (j + 1) * fb))
        act[:, j * fb:(j + 1) * fb] = (jax.nn.gelu(gate, approximate=True) * val).astype(BF16)

    f = _dot(act[...], wdown_ref[...])
    o_ref[0] = x + _rms(f, g3_ref[...])
    nbuf_ref[0] = carry[pad - (width - 1):pad, :]


def _ffn(x, buf, g2, wup, cw, cb, wdown, g3, tb):
    b, s, d = x.shape
    dff = wdown.shape[0]
    width = cw.shape[0]
    kern = functools.partial(_ffn_kernel, tb=tb, dff=dff, width=width)
    blocks = 2 * _nbytes((tb, d), F32)
    scratch = (_nbytes(wup.shape, BF16) + _nbytes(wdown.shape, BF16) + 6 * _nbytes((tb, V7X_MXU), F32)
               + _nbytes((8, 2 * dff), F32) + _nbytes((tb, dff), BF16) + 2 * _nbytes((tb, d), F32))
    xs = pl.BlockSpec((1, tb, d), lambda i, t: (i, t, 0))
    bs = pl.BlockSpec((1, width - 1, 2 * dff), lambda i, t: (i, 0, 0))
    return pl.pallas_call(
        kern,
        grid=(b, s // tb),
        in_specs=[xs, bs, _const_spec((1, d)), _const_spec(wup.shape), _const_spec(cw.shape),
                  _const_spec((1, 2 * dff)), _const_spec(wdown.shape), _const_spec((1, d))],
        out_specs=[xs, bs],
        out_shape=[jax.ShapeDtypeStruct((b, s, d), F32), jax.ShapeDtypeStruct((b, width - 1, 2 * dff), F32)],
        scratch_shapes=[pltpu.VMEM((V7X_SUBLANES, 2 * dff), F32), pltpu.VMEM((tb, dff), BF16)],
        compiler_params=pltpu.CompilerParams(
            dimension_semantics=("arbitrary", "arbitrary"), vmem_limit_bytes=_vmem_limit(blocks, scratch)),
        name="conv_ffn",
    )(x, buf, g2, wup, cw, cb, wdown, g3)


def _gate_fold_kernel(wq_ref, wk_ref, wv_ref, gq_ref, gk_ref, gv_ref, gc_ref, gm_ref):
    dotp = functools.partial(jnp.dot, preferred_element_type=F32, precision=HI)
    gc_ref[...] = dotp(wq_ref[0], gq_ref[...]) + dotp(wk_ref[0], gk_ref[...])
    gm_ref[...] = dotp(wv_ref[0], gv_ref[...])


def _gate_fold(wq_t, wk_t, wv_t, w_gate):
    nt, w, _ = wq_t.shape
    inner = nt * w
    ng = w_gate.shape[1]
    tile = pl.BlockSpec((1, w, w), lambda i: (i, 0, 0))
    gs = [pl.BlockSpec((w, ng), lambda i, k=k: (k * nt + i, 0)) for k in range(3)]
    out = pl.BlockSpec((w, ng), lambda i: (i, 0))
    return pl.pallas_call(
        _gate_fold_kernel,
        grid=(nt,),
        in_specs=[tile, tile, tile] + gs,
        out_specs=[out, out],
        out_shape=[jax.ShapeDtypeStruct((inner, ng), F32)] * 2,
        name="mlstm_gate_fold",
    )(wq_t, wk_t, wv_t, w_gate, w_gate, w_gate)


def _mpre_kernel(x_ref, cbuf_ref, g_ref, wup_ref, cw_ref, cb_ref, wq_ref, wkt_ref, wv_ref, gc_ref, gm_ref,
                 bg_ref, q_ref, kt_ref, v_ref, xc_ref, z_ref, gates_ref, nbuf_ref, carry,
                 *, tb, inner, width):
    t = pl.program_id(1)
    pad = V7X_SUBLANES

    @pl.when(t == 0)
    def _():
        carry[...] = jnp.zeros_like(carry)
        carry[pad - (width - 1):pad, :] = cbuf_ref[0]

    h = _rms(x_ref[0], g_ref[...]).astype(BF16)
    w = V7X_MXU
    gacc = jnp.zeros((tb, bg_ref.shape[1]), F32) + bg_ref[...]
    for i in range(inner // w):
        cs = slice(i * w, (i + 1) * w)
        xm = _dot(h, wup_ref[:, cs])
        prev = carry[:, cs]
        carry[:, cs] = xm[tb - pad:tb, :]
        xc = _silu(_causal_dwconv(xm, prev, cw_ref, cb_ref, cs)).astype(BF16)
        xmb = xm.astype(BF16)
        q_ref[0, :, cs] = _dot(xc, wq_ref[i]).astype(BF16)
        kt_ref[0, cs, :] = _dot_nt(wkt_ref[i], xc).astype(BF16)
        v_ref[0, :, cs] = _dot(xmb, wv_ref[i]).astype(BF16)
        xc_ref[0, :, cs] = xc
        z_ref[0, :, cs] = _silu(_dot(h, wup_ref[:, inner + i * w:inner + (i + 1) * w])).astype(BF16)
        gacc = gacc + _dot(xc, gc_ref[cs, :]) + _dot(xmb, gm_ref[cs, :])
    gates_ref[0] = gacc
    nbuf_ref[0] = carry[pad - (width - 1):pad, :]


def _mpre(x, cbuf, g, wup, cw, cb, wq_t, wkt_t, wv_t, gc, gm, bg, tb):
    b, s, d = x.shape
    inner = cw.shape[1]
    width = cw.shape[0]
    ng = bg.shape[1]
    kern = functools.partial(_mpre_kernel, tb=tb, inner=inner, width=width)
    act = pl.BlockSpec((1, tb, inner), lambda i, t: (i, t, 0))
    blocks = _nbytes((tb, d), F32) + 5 * _nbytes((tb, inner), BF16)
    scratch = (_nbytes(wup.shape, BF16) + 3 * _nbytes(wq_t.shape, BF16) + 2 * _nbytes((inner, 128), BF16)
               + _nbytes((8, inner), F32) + 8 * _nbytes((tb, V7X_MXU), F32))
    return pl.pallas_call(
        kern,
        grid=(b, s // tb),
        in_specs=[pl.BlockSpec((1, tb, d), lambda i, t: (i, t, 0)),
                  pl.BlockSpec((1, width - 1, inner), lambda i, t: (i, 0, 0)),
                  _const_spec((1, d)), _const_spec(wup.shape), _const_spec(cw.shape), _const_spec((1, inner)),
                  _const_spec(wq_t.shape), _const_spec(wkt_t.shape), _const_spec(wv_t.shape),
                  _const_spec(gc.shape), _const_spec(gm.shape), _const_spec((1, ng))],
        out_specs=[act, pl.BlockSpec((1, inner, tb), lambda i, t: (i, 0, t)), act, act, act,
                   pl.BlockSpec((1, tb, ng), lambda i, t: (i, t, 0)),
                   pl.BlockSpec((1, width - 1, inner), lambda i, t: (i, 0, 0))],
        out_shape=[jax.ShapeDtypeStruct((b, s, inner), BF16), jax.ShapeDtypeStruct((b, inner, s), BF16),
                   jax.ShapeDtypeStruct((b, s, inner), BF16), jax.ShapeDtypeStruct((b, s, inner), BF16),
                   jax.ShapeDtypeStruct((b, s, inner), BF16), jax.ShapeDtypeStruct((b, s, ng), F32),
                   jax.ShapeDtypeStruct((b, width - 1, inner), F32)],
        scratch_shapes=[pltpu.VMEM((V7X_SUBLANES, inner), F32)],
        compiler_params=pltpu.CompilerParams(
            dimension_semantics=("arbitrary", "arbitrary"), vmem_limit_bytes=_vmem_limit(blocks, scratch)),
        name="mlstm_pre",
    )(x, cbuf, g, wup, cw, cb, wq_t, wkt_t, wv_t, gc, gm, bg)


def _mlstm_chunk(q_ref, kt_ref, v_ref, g, gt, hn_ref, ct_s, n_s, m_s, *, lc, heads, dh):
    row = lax.broadcasted_iota(jnp.int32, (lc, lc), 0)
    col = lax.broadcasted_iota(jnp.int32, (lc, lc), 1)
    causal = col <= row
    tri_l = jnp.where(causal, 1.0, 0.0).astype(BF16)
    tri_u = jnp.where(row <= col, 1.0, 0.0).astype(BF16)
    b_cols = sum(_dot(tri_l, part) for part in _split3(_log_sigmoid(g)))
    b_rows = sum(_dot(part, tri_u) for part in _split3(_log_sigmoid(gt)))

    hsl = [slice(h * dh, (h + 1) * dh) for h in range(heads)]
    st = []
    for h in range(heads):
        b_col = b_cols[:, heads + h:heads + h + 1]
        b_row = b_rows[heads + h:heads + h + 1, :]
        i_row = gt[h:h + 1, :]
        m_prev = m_s[h:h + 1, 0:1]
        dmat = jnp.where(causal, b_col - b_row + i_row, -jnp.inf)
        inter = b_col + m_prev
        m_t = jnp.maximum(inter, jnp.max(dmat, axis=1, keepdims=True))
        st.append(dict(b_col=b_col, b_row=b_row, i_row=i_row, m_prev=m_prev, m_t=m_t,
                       wts=jnp.exp(dmat - m_t), a=jnp.exp(inter - m_t)))
    for h in range(heads):
        s = st[h]
        s['sc'] = _dot(q_ref[0, :, hsl[h]], kt_ref[0, hsl[h], :]) * s['wts']
    for h in range(heads):
        s = st[h]
        qh = q_ref[0, :, hsl[h]]
        sc = s['sc']
        a = s['a']
        num = _dot(sc.astype(BF16), v_ref[0, :, hsl[h]]) + a * _dot(qh, ct_s[h].astype(BF16))
        qn = jnp.sum(qh.astype(F32) * n_s[h:h + 1, :], axis=1, keepdims=True)
        den = jnp.sum(sc, axis=1, keepdims=True) + a * qn
        hh = num / jnp.maximum(jnp.abs(den), jnp.exp(-s['m_t']))
        mu = jnp.mean(hh, axis=1, keepdims=True)
        dev = hh - mu
        var = jnp.mean(dev * dev, axis=1, keepdims=True)
        hn_ref[0, :, hsl[h]] = (dev * lax.rsqrt(var + LN_EPS)).astype(hn_ref.dtype)
    for h in range(heads):
        s = st[h]
        kth = kt_ref[0, hsl[h], :]
        b_last = s['b_row'][:, lc - 1:lc]
        g_row = b_last - s['b_row'] + s['i_row']
        m_new = jnp.maximum(b_last + s['m_prev'], jnp.max(g_row, axis=1, keepdims=True))
        decay = jnp.exp(b_last + s['m_prev'] - m_new)
        wg_row = jnp.exp(g_row - m_new)
        wg_col = jnp.exp(b_last - s['b_col'] + g[:, h:h + 1] - m_new)
        wv = (v_ref[0, :, hsl[h]].astype(F32) * wg_col).astype(BF16)
        ct_s[h] = decay * ct_s[h] + _dot(kth, wv)
        wg16 = jnp.broadcast_to(wg_row, (16, lc)).astype(BF16)
        n_s[h:h + 1, :] = decay * n_s[h:h + 1, :] + _dot_nt(wg16, kth)[0:1, :]
        m_s[h:h + 1, :] = jnp.broadcast_to(m_new, (1, V7X_LANES))


def _mlstm_layer_kernel(*refs, lc, heads, inner, width, zero_init):
    (x_ref, cbuf_ref, g0_ref, wup_ref, cw_ref, cb_ref, wq_ref, wkt_ref, wv_ref, gc_ref, gm_ref, bg_ref,
     ng_ref, sk_ref, wdown_ref, g1_ref) = refs[:16]
    n_in = 16
    if not zero_init:
        c0_ref, n0_ref, m0_ref = refs[16:19]
        n_in = 19
    o_ref, cout_ref, nout_ref, mout_ref, nbuf_ref = refs[n_in:n_in + 5]
    carry, q_s, kt_s, v_s, xc_s, sz_s, hn_s, ct_s, n_s, m_s = refs[n_in + 5:]
    dh = inner // heads
    pad = V7X_SUBLANES
    c = pl.program_id(1)

    @pl.when(c == 0)
    def _():
        carry[...] = jnp.zeros_like(carry)
        carry[pad - (width - 1):pad, :] = cbuf_ref[0]
        if zero_init:
            ct_s[...] = jnp.zeros_like(ct_s)
            n_s[...] = jnp.zeros_like(n_s)
            m_s[...] = jnp.zeros_like(m_s)
        else:
            for h in range(heads):
                ct_s[h] = c0_ref[0, h].T
            n_s[...] = n0_ref[0]
            m_s[...] = m0_ref[0]

    x = x_ref[0]
    hx = _rms(x, g0_ref[...]).astype(BF16)
    w = V7X_MXU
    gacc = jnp.zeros((lc, bg_ref.shape[1]), F32) + bg_ref[...]
    for i in range(inner // w):
        cs = slice(i * w, (i + 1) * w)
        xm = _dot(hx, wup_ref[:, cs])
        prev = carry[:, cs]
        carry[:, cs] = xm[lc - pad:lc, :]
        xc = _silu(_causal_dwconv(xm, prev, cw_ref, cb_ref, cs)).astype(BF16)
        xmb = xm.astype(BF16)
        q_s[0, :, cs] = _dot(xc, wq_ref[i]).astype(BF16)
        kt_s[0, cs, :] = _dot_nt(wkt_ref[i], xc).astype(BF16)
        v_s[0, :, cs] = _dot(xmb, wv_ref[i]).astype(BF16)
        xc_s[:, cs] = xc
        sz_s[:, cs] = _silu(_dot(hx, wup_ref[:, inner + i * w:inner + (i + 1) * w])).astype(BF16)
        gacc = gacc + _dot(xc, gc_ref[cs, :]) + _dot(xmb, gm_ref[cs, :])
    nbuf_ref[0] = carry[pad - (width - 1):pad, :]

    _mlstm_chunk(q_s, kt_s, v_s, gacc, gacc.T[0:2 * heads, :], hn_s, ct_s, n_s, m_s, lc=lc, heads=heads, dh=dh)

    hs = hn_s[0].astype(F32) * ng_ref[...] + sk_ref[...] * xc_s[...].astype(F32)
    act = (hs * sz_s[...].astype(F32)).astype(BF16)
    o_ref[0] = x + _rms(_dot(act, wdown_ref[...]), g1_ref[...])

    @pl.when(c == pl.num_programs(1) - 1)
    def _():
        for h in range(heads):
            cout_ref[0, h] = ct_s[h].T
        nout_ref[0] = n_s[...]
        mout_ref[0] = m_s[...]


def _mlstm_layer(x, cbuf, state, p, g0, g1, lc):
    b, s, d = x.shape
    heads = p['heads']
    inner = p['m_cw'].shape[1]
    width = p['m_cw'].shape[0]
    dh = inner // heads
    zero_init = state is None
    kern = functools.partial(_mlstm_layer_kernel, lc=lc, heads=heads, inner=inner, width=width,
                             zero_init=zero_init)
    xs = pl.BlockSpec((1, lc, d), lambda i, c: (i, c, 0))
    bufs = pl.BlockSpec((1, width - 1, inner), lambda i, c: (i, 0, 0))
    cs = pl.BlockSpec((1, heads, dh, dh), lambda i, c: (i, 0, 0, 0))
    ns = pl.BlockSpec((1, heads, dh), lambda i, c: (i, 0, 0))
    ms = pl.BlockSpec((1, heads, V7X_LANES), lambda i, c: (i, 0, 0))
    consts = [g0, p['m_wup'], p['m_cw'], p['m_cb'], p['m_wq'], p['m_wkt'], p['m_wv'], p['m_gc'], p['m_gm'],
              p['m_bg'], p['m_ng'], p['m_skip'], p['m_wdown'], g1]
    in_specs = [xs, bufs] + [_const_spec(a.shape) for a in consts]
    args = [x, cbuf] + consts
    if not zero_init:
        in_specs += [cs, ns, ms]
        args += list(state)
    blocks = 2 * _nbytes((lc, d), F32) + (1 if zero_init else 2) * _nbytes((heads, dh, dh), F32)
    scratch = (sum(_nbytes(a.shape, a.dtype) for a in consts) + 6 * _nbytes((lc, inner), BF16)
               + _nbytes((heads, dh, dh), F32) + 8 * _nbytes((lc, max(lc, dh)), F32) + 2 * _nbytes((dh, dh), F32))
    return pl.pallas_call(
        kern,
        grid=(b, s // lc),
        in_specs=in_specs,
        out_specs=[xs, cs, ns, ms, bufs],
        out_shape=[jax.ShapeDtypeStruct((b, s, d), F32), jax.ShapeDtypeStruct((b, heads, dh, dh), F32),
                   jax.ShapeDtypeStruct((b, heads, dh), F32), jax.ShapeDtypeStruct((b, heads, V7X_LANES), F32),
                   jax.ShapeDtypeStruct((b, width - 1, inner), F32)],
        scratch_shapes=[pltpu.VMEM((V7X_SUBLANES, inner), F32),
                        pltpu.VMEM((1, lc, inner), BF16), pltpu.VMEM((1, inner, lc), BF16),
                        pltpu.VMEM((1, lc, inner), BF16), pltpu.VMEM((lc, inner), BF16),
                        pltpu.VMEM((lc, inner), BF16), pltpu.VMEM((1, lc, inner), BF16),
                        pltpu.VMEM((heads, dh, dh), F32), pltpu.VMEM((heads, dh), F32),
                        pltpu.VMEM((heads, V7X_LANES), F32)],
        compiler_params=pltpu.CompilerParams(
            dimension_semantics=("arbitrary", "arbitrary"), vmem_limit_bytes=_vmem_limit(blocks, scratch)),
        name="mlstm_layer",
    )(*args)


def _block_diag_tiles(w):
    nblk, k, _ = w.shape
    per = V7X_MXU // k
    wt = w.reshape(nblk // per, per, k, k)
    eye = jnp.eye(per, dtype=w.dtype)
    full = wt[:, :, :, None, :] * eye[None, :, None, :, None]
    return full.reshape(nblk // per, per * k, per * k)


def _prep_weights(w):
    row = lambda v: v.astype(F32).reshape(1, -1)
    p = {}
    p['norm_g'] = w['norm_g'].astype(F32)
    p['s5_mats'] = _s5_matrices(w['s5_A_re'][0], w['s5_A_im'][0], w['s5_log_dt'][0], w['s5_B_re'][0],
                                w['s5_B_im'][0], w['s5_C_re'][0], w['s5_C_im'][0])
    p['s5_D'] = row(w['s5_D'][0])
    p['s5_w_glu'] = w['s5_w_glu'][0].astype(BF16)
    inner = w['mlstm_conv_w'].shape[2]
    heads = w['mlstm_b_gate'].shape[1] // 2
    dh = inner // heads
    wq_t = _block_diag_tiles(w['mlstm_wq'][0].astype(F32))
    wk_t = _block_diag_tiles(w['mlstm_wk'][0].astype(F32))
    wv_t = _block_diag_tiles(w['mlstm_wv'][0].astype(F32))
    gc, gm = _gate_fold(wq_t, wk_t, wv_t, w['mlstm_w_gate'][0].astype(F32))
    p['m_wq'] = wq_t.astype(BF16)
    p['m_wkt'] = (jnp.swapaxes(wk_t, 1, 2) * (dh ** -0.5)).astype(BF16)
    p['m_wv'] = wv_t.astype(BF16)
    lane_pad = lambda a: jnp.pad(a, ((0, 0), (0, V7X_LANES - a.shape[1])))
    p['m_gc'] = lane_pad(gc).astype(BF16)
    p['m_gm'] = lane_pad(gm).astype(BF16)
    p['m_bg'] = lane_pad(row(w['mlstm_b_gate'][0]))
    p['m_wup'] = w['mlstm_w_up'][0].astype(BF16)
    p['m_cw'] = w['mlstm_conv_w'][0].astype(F32)
    p['m_cb'] = row(w['mlstm_conv_b'][0])
    p['m_ng'] = row(w['mlstm_norm_g'][0])
    p['m_skip'] = row(w['mlstm_skip'][0])
    p['m_wdown'] = w['mlstm_w_down'][0].astype(BF16)
    p['heads'] = heads
    p['f_wup'] = w['ffn_w_up'].astype(BF16)
    p['f_cw'] = w['ffn_conv_w'].astype(F32)
    p['f_cb'] = w['ffn_conv_b'].astype(F32)
    p['f_wdown'] = w['ffn_w_down'].astype(BF16)
    return p


def _to_chunk_major(u2d, b, s):
    d = u2d.shape[1]
    g = d // S5_GROUP_CH
    u = u2d.reshape(b, s // S5_CHUNK, S5_CHUNK, g, S5_GROUP_CH)
    return jnp.transpose(u, (1, 0, 3, 2, 4)).reshape((s // S5_CHUNK) * b, d * S5_CHUNK)


def _to_token_major(yt, b, s):
    d = yt.shape[1] // S5_CHUNK
    g = d // S5_GROUP_CH
    y = yt.reshape(s // S5_CHUNK, b, g, S5_CHUNK, S5_GROUP_CH)
    return jnp.transpose(y, (1, 0, 3, 2, 4)).reshape(b * s, d)


def _trunk(x, st, p, tiles):
    b, s, d = x.shape
    ng = p['norm_g']
    g_of = lambda layer, k: ng[layer, k].reshape(1, d)
    zero_init = st is None
    groups = d // S5_GROUP_CH
    x2d = x.reshape(b * s, d)

    n_state = p['s5_mats']['row'][1].shape[2] // 2
    if zero_init:
        h0re = h0im = jnp.zeros((groups // 2, b, 2 * n_state), F32)
    else:
        pair = lambda h: jnp.transpose(h.astype(F32).reshape(b, groups // 2, 2 * n_state), (1, 0, 2))
        h0re, h0im = pair(st['s5_re']), pair(st['s5_im'])
    if tiles.s5_time:
        y, hre, him = _s5_fused(x, g_of(0, 0), p['s5_mats']['col'], h0re, h0im, tiles.s5_time)
        y2d = y.reshape(b * s, d)
    else:
        u = _norm_cast(x2d, g_of(0, 0), tiles.rows)
        yt, hre, him = _s5_core(_to_chunk_major(u, b, s), p['s5_mats']['row'], h0re, h0im, b, tiles.s5_rows)
        y2d = _to_token_major(yt, b, s)
    unpair = lambda h: jnp.transpose(h, (1, 0, 2)).reshape(1, b, groups, n_state)
    o_re, o_im = unpair(hre), unpair(him)
    x2d = _glu(x2d, y2d, g_of(0, 0), p['s5_D'], p['s5_w_glu'], g_of(0, 1), tiles.rows)

    def ffn(x2d, layer):
        dff2 = p['f_wup'].shape[2]
        width = p['f_cw'].shape[1]
        buf = jnp.zeros((b, width - 1, dff2), F32) if zero_init else st['ffn_conv'][layer].astype(F32)
        xo, nbuf = _ffn(x2d.reshape(b, s, d), buf, g_of(layer, 2), p['f_wup'][layer], p['f_cw'][layer],
                        p['f_cb'][layer].reshape(1, dff2), p['f_wdown'][layer], g_of(layer, 3), tiles.ffn_time)
        return xo.reshape(b * s, d), nbuf

    x2d, fbuf0 = ffn(x2d, 0)

    heads = p['heads']
    inner = p['m_cw'].shape[1]
    width = p['m_cw'].shape[0]
    cbuf = jnp.zeros((b, width - 1, inner), F32) if zero_init else st['mlstm_conv'].astype(F32)
    if zero_init:
        state = None
    else:
        state = (st['mlstm_C'].astype(F32), st['mlstm_n'].astype(F32),
                 jnp.broadcast_to(st['mlstm_m'].astype(F32)[:, :, None], (b, heads, V7X_LANES)))
    x3, c_out, n_out, m_out, ncbuf = _mlstm_layer(x2d.reshape(b, s, d), cbuf, state, p, g_of(1, 0), g_of(1, 1),
                                                  tiles.mlstm_chunk)
    x2d, fbuf1 = ffn(x3.reshape(b * s, d), 1)

    return (x2d.reshape(b, s, d), o_re, o_im, c_out[None], n_out[None], m_out[None, :, :, 0], ncbuf[None],
            jnp.stack([fbuf0, fbuf1], axis=0))


def kernel(x_prompt, x_sample, state_s5_re, state_s5_im, state_mlstm_C, state_mlstm_n, state_mlstm_m, state_mlstm_conv, state_ffn_conv, norm_g, s5_A_re, s5_A_im, s5_log_dt, s5_B_re, s5_B_im, s5_C_re, s5_C_im, s5_D, s5_w_glu, mlstm_w_up, mlstm_conv_w, mlstm_conv_b, mlstm_wq, mlstm_wk, mlstm_wv, mlstm_w_gate, mlstm_b_gate, mlstm_norm_g, mlstm_skip, mlstm_w_down, ffn_w_up, ffn_conv_w, ffn_conv_b, ffn_w_down):
    w = {
        'norm_g': norm_g, 's5_A_re': s5_A_re, 's5_A_im': s5_A_im, 's5_log_dt': s5_log_dt,
        's5_B_re': s5_B_re, 's5_B_im': s5_B_im, 's5_C_re': s5_C_re, 's5_C_im': s5_C_im, 's5_D': s5_D,
        's5_w_glu': s5_w_glu, 'mlstm_w_up': mlstm_w_up, 'mlstm_conv_w': mlstm_conv_w,
        'mlstm_conv_b': mlstm_conv_b, 'mlstm_wq': mlstm_wq, 'mlstm_wk': mlstm_wk, 'mlstm_wv': mlstm_wv,
        'mlstm_w_gate': mlstm_w_gate, 'mlstm_b_gate': mlstm_b_gate, 'mlstm_norm_g': mlstm_norm_g,
        'mlstm_skip': mlstm_skip, 'mlstm_w_down': mlstm_w_down, 'ffn_w_up': ffn_w_up,
        'ffn_conv_w': ffn_conv_w, 'ffn_conv_b': ffn_conv_b, 'ffn_w_down': ffn_w_down,
    }
    assert norm_g.shape[0] == 2 and s5_A_re.shape[0] == 1 and mlstm_w_up.shape[0] == 1
    assert x_prompt.shape[0] == V7X_SUBLANES and x_sample.shape[0] == V7X_SUBLANES
    p = _prep_weights(w)
    out_p = _trunk(x_prompt.astype(F32), None, p, _pick_tiles(*x_prompt.shape[:2]))
    st = {'s5_re': state_s5_re[0], 's5_im': state_s5_im[0], 'mlstm_C': state_mlstm_C[0],
          'mlstm_n': state_mlstm_n[0], 'mlstm_m': state_mlstm_m[0], 'mlstm_conv': state_mlstm_conv[0],
          'ffn_conv': state_ffn_conv}
    out_s = _trunk(x_sample.astype(F32), st, p, _pick_tiles(*x_sample.shape[:2]))
    return (out_p[0], out_s[0]) + tuple(out_p[1:]) + tuple(out_s[1:])
```

```python
import functools
import math
from typing import NamedTuple

import jax
import jax.numpy as jnp
from jax import lax
from jax.experimental import pallas as pl
from jax.experimental.pallas import tpu as pltpu

F32 = jnp.float32
BF16 = jnp.bfloat16
HI = lax.Precision.HIGHEST

NORM_EPS = 1e-6
LN_EPS = 1e-5
S5_GROUP_CH = 16
S5_CHUNK = 16
QKV_BLOCK = 4

V7X_LANES = 128
V7X_SUBLANES = 8
V7X_MXU = 256
V7X_VMEM_BYTES = 64 * 1024 * 1024


class Tiles(NamedTuple):
    rows: int
    time: int
    s5_rows: int
    mlstm_chunk: int
    s5_time: int
    ffn_time: int


def _pick_tiles(batch, seq):
    s5_time = (V7X_LANES // batch) * S5_CHUNK
    return Tiles(rows=min(batch * seq, 512), time=min(seq, 512),
                 s5_rows=min(batch * (seq // S5_CHUNK), 512), mlstm_chunk=min(seq, 256),
                 s5_time=s5_time if seq % s5_time == 0 else 0, ffn_time=min(seq, 1024))


def _vmem_limit(block_bytes, scratch_bytes):
    want = 2 * block_bytes + scratch_bytes + 16 * 1024 * 1024
    return int(min(want, V7X_VMEM_BYTES - 8 * 1024 * 1024))


def _nbytes(shape, dtype):
    return math.prod(shape) * jnp.dtype(dtype).itemsize


def _rms(x, g):
    return x * lax.rsqrt(jnp.mean(x * x, axis=-1, keepdims=True) + NORM_EPS) * g


def _dot(a, b):
    return jnp.dot(a, b, preferred_element_type=F32)


def _dot_nt(a, b):
    return lax.dot_general(a, b, (((1,), (1,)), ((), ())), preferred_element_type=F32)


def _silu(x):
    return (0.5 * x) * (1.0 + jnp.tanh(0.5 * x))


def _causal_dwconv(u, prev, cw_ref, cb_ref, cs):
    width = cw_ref.shape[0]
    acc = cb_ref[:, cs] + u * cw_ref[width - 1:width, cs]
    rows = lax.broadcasted_iota(jnp.int32, prev.shape, 0)
    for j in range(width - 1):
        d = width - 1 - j
        rolled = pltpu.roll(u, d, 0)
        head = jnp.where(rows < d, pltpu.roll(prev, d, 0), rolled[0:V7X_SUBLANES])
        shifted = jnp.concatenate([head, rolled[V7X_SUBLANES:]], axis=0)
        acc = acc + shifted * cw_ref[j:j + 1, cs]
    return acc


def _log_sigmoid(x):
    return jnp.minimum(x, 0.0) - jnp.log1p(jnp.exp(-jnp.abs(x)))


def _split3(x):
    hi = x.astype(BF16)
    r1 = x - hi.astype(F32)
    mid = r1.astype(BF16)
    lo = (r1 - mid.astype(F32)).astype(BF16)
    return hi, mid, lo


def _norm_cast_kernel(x_ref, g_ref, o_ref):
    o_ref[...] = _rms(x_ref[...], g_ref[...]).astype(o_ref.dtype)


def _norm_cast(x2d, g, tb):
    rows, d = x2d.shape
    return pl.pallas_call(
        _norm_cast_kernel,
        grid=(rows // tb,),
        in_specs=[pl.BlockSpec((tb, d), lambda i: (i, 0)), pl.BlockSpec((1, d), lambda i: (0, 0))],
        out_specs=pl.BlockSpec((tb, d), lambda i: (i, 0)),
        out_shape=jax.ShapeDtypeStruct((rows, d), BF16),
        name="s5_norm",
    )(x2d, g)


def _s5_kernel(u_ref, t_ref, pre_ref, pim_ref, qre_ref, qim_ref, are_ref, aim_ref, h0re_ref, h0im_ref,
               y_ref, hre_ref, him_ref, pure_s, puim_s, hinre_s, hinim_s, stre_s, stim_s, *, nb, rb):
    r = pl.program_id(1)

    @pl.when(r == 0)
    def _():
        stre_s[...] = h0re_ref[0]
        stim_s[...] = h0im_ref[0]

    u = u_ref[...]
    pure_s[...] = _dot(u, pre_ref[0])
    puim_s[...] = _dot(u, pim_ref[0])
    ar = jnp.broadcast_to(are_ref[0], (nb, 2 * 64))
    ai = jnp.broadcast_to(aim_ref[0], (nb, 2 * 64))

    def step(i, carry):
        re, im = carry
        rows = pl.ds(pl.multiple_of(i * nb, nb), nb)
        hinre_s[rows, :] = re
        hinim_s[rows, :] = im
        return (ar * re - ai * im + pure_s[rows, :], ar * im + ai * re + puim_s[rows, :])

    re, im = lax.fori_loop(0, rb // nb, step, (stre_s[...], stim_s[...]))
    stre_s[...] = re
    stim_s[...] = im
    hre_ref[0] = re
    him_ref[0] = im

    carry_in = _dot(hinre_s[...].astype(BF16), qre_ref[0]) + _dot(hinim_s[...].astype(BF16), qim_ref[0])
    w = V7X_MXU
    for k in range(2):
        y_ref[:, k * w:(k + 1) * w] = (_dot(u[:, k * w:(k + 1) * w], t_ref[k])
                                       + carry_in[:, k * w:(k + 1) * w]).astype(y_ref.dtype)


def _s5_core(ut, mats, h0re, h0im, nb, rb):
    rows, cols = ut.shape
    pairs = cols // (2 * V7X_MXU)
    tm, pre, pim, qre, qim, are, aim = mats
    pair3 = lambda p, r: (p, 0, 0)
    kern = functools.partial(_s5_kernel, nb=nb, rb=rb)
    blocks = (_nbytes((rb, 512), BF16) * 2 + _nbytes((2, 256, 256), BF16) + 4 * _nbytes((512, 128), BF16))
    scratch = 4 * _nbytes((rb, 128), F32) + 2 * _nbytes((nb, 128), F32)
    return pl.pallas_call(
        kern,
        grid=(pairs, rows // rb),
        in_specs=[
            pl.BlockSpec((rb, 2 * V7X_MXU), lambda p, r: (r, p)),
            pl.BlockSpec((2, V7X_MXU, V7X_MXU), pair3),
            pl.BlockSpec((1, 2 * V7X_MXU, 128), pair3),
            pl.BlockSpec((1, 2 * V7X_MXU, 128), pair3),
            pl.BlockSpec((1, 128, 2 * V7X_MXU), pair3),
            pl.BlockSpec((1, 128, 2 * V7X_MXU), pair3),
            pl.BlockSpec((1, 1, 128), pair3),
            pl.BlockSpec((1, 1, 128), pair3),
            pl.BlockSpec((1, nb, 128), pair3),
            pl.BlockSpec((1, nb, 128), pair3),
        ],
        out_specs=[
            pl.BlockSpec((rb, 2 * V7X_MXU), lambda p, r: (r, p)),
            pl.BlockSpec((1, nb, 128), pair3),
            pl.BlockSpec((1, nb, 128), pair3),
        ],
        out_shape=[
            jax.ShapeDtypeStruct((rows, cols), BF16),
            jax.ShapeDtypeStruct((pairs, nb, 128), F32),
            jax.ShapeDtypeStruct((pairs, nb, 128), F32),
        ],
        scratch_shapes=[pltpu.VMEM((rb, 128), F32)] * 4 + [pltpu.VMEM((nb, 128), F32)] * 2,
        compiler_params=pltpu.CompilerParams(
            dimension_semantics=("arbitrary", "arbitrary"), vmem_limit_bytes=_vmem_limit(blocks, scratch)),
        name="s5_core",
    )(ut, tm, pre, pim, qre, qim, are, aim, h0re, h0im)


def _s5_fused_kernel(x_ref, g_ref, tt_ref, pt_ref, qt_ref, are_ref, aim_ref, h0re_ref, h0im_ref,
                     y_ref, hre_ref, him_ref,
                     us, ugt, pure, puim, hinre, hinim, ytmp, ys, stre, stim, *, nb, tbt):
    lc = S5_CHUNK
    ch = S5_GROUP_CH
    gps = V7X_LANES // ch
    nch = tbt // lc
    cols = nb * nch
    k = pl.program_id(1)

    @pl.when(k == 0)
    def _():
        for b in range(nb):
            u = _rms(x_ref[b], g_ref[...])
            for kk in range(us.shape[0]):
                us[kk, b * tbt:(b + 1) * tbt, :] = u[:, kk * V7X_LANES:(kk + 1) * V7X_LANES]

    @pl.when(pl.program_id(0) == 0)
    def _():
        stre[k] = h0re_ref[...]
        stim[k] = h0im_ref[...]

    for s in range(lc):
        a = us[k, pl.ds(s, cols, stride=lc), :].T
        for g in range(gps):
            ugt[g, s * ch:(s + 1) * ch, :] = a[g * ch:(g + 1) * ch, :].astype(BF16)

    half = pt_ref.shape[1] // 2
    for pr in range(gps // 2):
        put0 = _dot(pt_ref[2 * pr], ugt[2 * pr])
        put1 = _dot(pt_ref[2 * pr + 1], ugt[2 * pr + 1])
        pure[pr] = jnp.concatenate([put0[:half], put1[:half]], axis=0).T
        puim[pr] = jnp.concatenate([put0[half:], put1[half:]], axis=0).T

    for pr in range(gps // 2):
        re = stre[k, pr]
        im = stim[k, pr]
        ar = jnp.broadcast_to(are_ref[pr], re.shape)
        ai = jnp.broadcast_to(aim_ref[pr], re.shape)
        for j in range(nch):
            rows = pl.ds(j, nb, stride=nch)
            hinre[pr, rows, :] = re
            hinim[pr, rows, :] = im
            re, im = (ar * re - ai * im + pure[pr, rows, :], ar * im + ai * re + puim[pr, rows, :])
        stre[k, pr] = re
        stim[k, pr] = im
        hre_ref[k * (gps // 2) + pr] = re
        him_ref[k * (gps // 2) + pr] = im

    for pr in range(gps // 2):
        hre_t = hinre[pr].T
        him_t = hinim[pr].T
        for i in range(2):
            g = 2 * pr + i
            hin_t = jnp.concatenate([hre_t[i * half:(i + 1) * half], him_t[i * half:(i + 1) * half]],
                                    axis=0).astype(BF16)
            ytmp[g] = _dot(tt_ref[g], ugt[g]) + _dot(qt_ref[g], hin_t)

    for t in range(lc):
        z = jnp.concatenate([ytmp[g, t * ch:(t + 1) * ch, :] for g in range(gps)], axis=0)
        ys[pl.ds(t, cols, stride=lc), :] = z.T
    for b in range(nb):
        y_ref[b] = ys[b * tbt:(b + 1) * tbt, :].astype(y_ref.dtype)


def _s5_fused(x, g, mats_t, h0re, h0im, tbt):
    nb, s, d = x.shape
    tt, pt, qt, are, aim = mats_t
    groups = tt.shape[0]
    gps = V7X_LANES // S5_GROUP_CH
    slabs = groups // gps
    w = S5_CHUNK * S5_GROUP_CH
    nst = pt.shape[1]
    cols = nb * (tbt // S5_CHUNK)
    kern = functools.partial(_s5_fused_kernel, nb=nb, tbt=tbt)
    slab3 = lambda t, k: (k, 0, 0)
    blocks = (_nbytes((nb, tbt, d), F32) + _nbytes((nb, tbt, V7X_LANES), BF16)
              + gps * (_nbytes((w, w), BF16) + 2 * _nbytes((w, nst), BF16)))
    scratch = (_nbytes((slabs, nb * tbt, V7X_LANES), F32) + _nbytes((gps, w, cols), BF16)
               + 4 * _nbytes((gps // 2, cols, nst), F32) + _nbytes((gps, w, cols), F32)
               + _nbytes((nb * tbt, V7X_LANES), F32))
    return pl.pallas_call(
        kern,
        grid=(s // tbt, slabs),
        in_specs=[
            pl.BlockSpec((nb, tbt, d), lambda t, k: (0, t, 0)),
            pl.BlockSpec((1, d), lambda t, k: (0, 0)),
            pl.BlockSpec((gps, w, w), slab3),
            pl.BlockSpec((gps, nst, w), slab3),
            pl.BlockSpec((gps, w, nst), slab3),
            pl.BlockSpec((gps // 2, 1, nst), slab3),
            pl.BlockSpec((gps // 2, 1, nst), slab3),
            pl.BlockSpec((gps // 2, nb, nst), slab3),
            pl.BlockSpec((gps // 2, nb, nst), slab3),
        ],
        out_specs=[
            pl.BlockSpec((nb, tbt, V7X_LANES), lambda t, k: (0, t, k)),
            pl.BlockSpec((groups // 2, nb, nst), lambda t, k: (0, 0, 0)),
            pl.BlockSpec((groups // 2, nb, nst), lambda t, k: (0, 0, 0)),
        ],
        out_shape=[
            jax.ShapeDtypeStruct((nb, s, d), BF16),
            jax.ShapeDtypeStruct((groups // 2, nb, nst), F32),
            jax.ShapeDtypeStruct((groups // 2, nb, nst), F32),
        ],
        scratch_shapes=[
            pltpu.VMEM((slabs, nb * tbt, V7X_LANES), F32),
            pltpu.VMEM((gps, w, cols), BF16),
            pltpu.VMEM((gps // 2, cols, nst), F32),
            pltpu.VMEM((gps // 2, cols, nst), F32),
            pltpu.VMEM((gps // 2, cols, nst), F32),
            pltpu.VMEM((gps // 2, cols, nst), F32),
            pltpu.VMEM((gps, w, cols), F32),
            pltpu.VMEM((nb * tbt, V7X_LANES), F32),
            pltpu.VMEM((slabs, gps // 2, nb, nst), F32),
            pltpu.VMEM((slabs, gps // 2, nb, nst), F32),
        ],
        compiler_params=pltpu.CompilerParams(
            dimension_semantics=("arbitrary", "arbitrary"), vmem_limit_bytes=_vmem_limit(blocks, scratch)),
        name="s5_fused",
    )(x, g, tt, pt, qt, are, aim, h0re, h0im)


def _s5_matrices(a_re, a_im, log_dt, b_re, b_im, c_re, c_im):
    lc = S5_CHUNK
    groups, n = a_re.shape
    ch = S5_GROUP_CH
    a = lax.complex(a_re.astype(F32), a_im.astype(F32))
    adt = a * jnp.exp(log_dt.astype(F32))[:, None]
    j = jnp.arange(lc + 1, dtype=F32)
    apow = jnp.exp(adt[None] * j[:, None, None])
    b_bar = ((apow[1] - 1.0) / a)[..., None] * lax.complex(b_re.astype(F32), b_im.astype(F32))
    c_mat = lax.complex(c_re.astype(F32), c_im.astype(F32))
    ca = c_mat[None] * apow[:, :, None, :]
    kern = (jnp.einsum('jgcn,gnd->jgcd', ca.real, b_bar.real, precision=HI)
            - jnp.einsum('jgcn,gnd->jgcd', ca.imag, b_bar.imag, precision=HI))
    s_idx = jnp.arange(lc)[:, None]
    t_idx = jnp.arange(lc)[None, :]
    lag = t_idx - s_idx
    kt = kern[jnp.clip(lag, 0, lc)]
    kt = jnp.where((lag >= 0)[:, :, None, None, None], kt, 0.0)
    tm = jnp.transpose(kt, (2, 0, 4, 1, 3)).reshape(groups, lc * ch, lc * ch)
    pw = apow[lc - 1 - jnp.arange(lc)]
    p = pw[:, :, :, None] * b_bar[None]
    p = jnp.transpose(p, (1, 0, 3, 2)).reshape(groups, lc * ch, n)
    q = ca[1:]
    q = jnp.transpose(q, (1, 3, 0, 2)).reshape(groups, n, lc * ch)

    def pair_rows(m):
        g2 = groups // 2
        m = m.reshape(g2, 2, m.shape[1], m.shape[2])
        z = jnp.zeros_like(m[:, 0])
        top = jnp.concatenate([m[:, 0], z], axis=2)
        bot = jnp.concatenate([z, m[:, 1]], axis=2)
        return jnp.concatenate([top, bot], axis=1)

    pre = pair_rows(p.real).astype(BF16)
    pim = pair_rows(p.imag).astype(BF16)
    qre = pair_rows(q.real).astype(BF16)
    qim = pair_rows(-q.imag).astype(BF16)
    a_chunk = apow[lc].reshape(groups // 2, 1, 2 * n)
    are, aim = a_chunk.real, a_chunk.imag
    tt = jnp.swapaxes(tm, 1, 2).astype(BF16)
    pt = jnp.concatenate([jnp.swapaxes(p.real, 1, 2), jnp.swapaxes(p.imag, 1, 2)], axis=1).astype(BF16)
    qt = jnp.concatenate([jnp.swapaxes(q.real, 1, 2), jnp.swapaxes(-q.imag, 1, 2)], axis=2).astype(BF16)
    return {'row': (tm.astype(BF16), pre, pim, qre, qim, are, aim), 'col': (tt, pt, qt, are, aim)}


def _glu_kernel(x_ref, y_ref, g0_ref, d_ref, w_ref, g1_ref, o_ref):
    x = x_ref[...]
    d = x.shape[1]
    u = _rms(x, g0_ref[...])
    yy = y_ref[...].astype(F32) + d_ref[...] * u
    z = jax.nn.gelu(yy, approximate=True).astype(BF16)
    ag = _dot(z, w_ref[...])
    out = ag[:, :d] * jax.nn.sigmoid(ag[:, d:])
    o_ref[...] = x + _rms(out, g1_ref[...])


def _const_spec(shape):
    nd = len(shape)
    return pl.BlockSpec(shape, lambda *_: (0,) * nd, pipeline_mode=pl.Buffered(1))


def _glu(x2d, y2d, g0, dskip, w, g1, tb):
    rows, d = x2d.shape
    blocks = 2 * _nbytes((tb, d), F32) + _nbytes((tb, d), BF16)
    scratch = _nbytes(w.shape, BF16) + 3 * _nbytes((tb, 2 * d), F32)
    row = pl.BlockSpec((tb, d), lambda i: (i, 0))
    return pl.pallas_call(
        _glu_kernel,
        grid=(rows // tb,),
        in_specs=[row, row, _const_spec((1, d)), _const_spec((1, d)), _const_spec(w.shape), _const_spec((1, d))],
        out_specs=row,
        out_shape=jax.ShapeDtypeStruct((rows, d), F32),
        compiler_params=pltpu.CompilerParams(
            dimension_semantics=("arbitrary",), vmem_limit_bytes=_vmem_limit(blocks, scratch)),
        name="s5_glu",
    )(x2d, y2d, g0, dskip, w, g1)


def _ffn_kernel(x_ref, buf_ref, g2_ref, wup_ref, cw_ref, cb_ref, wdown_ref, g3_ref,
                o_ref, nbuf_ref, carry, act, *, tb, dff, width):
    t = pl.program_id(1)
    pad = V7X_SUBLANES

    @pl.when(t == 0)
    def _():
        carry[...] = jnp.zeros_like(carry)
        carry[pad - (width - 1):pad, :] = buf_ref[0]

    x = x_ref[0]
    hn = _rms(x, g2_ref[...]).astype(BF16)
    fb = V7X_MXU

    def conv(cs):
        u = _dot(hn, wup_ref[:, cs])
        prev = carry[:, cs]
        carry[:, cs] = u[tb - pad:tb, :]
        return _causal_dwconv(u, prev, cw_ref, cb_ref, cs)

    for j in range(dff // fb):
        gate = conv(slice(j * fb, (j + 1) * fb))
        val = conv(slice(dff + j * fb, dff + (j + 1) * fb))
        act[:, j * fb:(j + 1) * fb] = (jax.nn.gelu(gate, approximate=True) * val).astype(BF16)

    f = _dot(act[...], wdown_ref[...])
    o_ref[0] = x + _rms(f, g3_ref[...])
    nbuf_ref[0] = carry[pad - (width - 1):pad, :]


def _ffn(x, buf, g2, wup, cw, cb, wdown, g3, tb):
    b, s, d = x.shape
    dff = wdown.shape[0]
    width = cw.shape[0]
    kern = functools.partial(_ffn_kernel, tb=tb, dff=dff, width=width)
    blocks = 2 * _nbytes((tb, d), F32)
    scratch = (_nbytes(wup.shape, BF16) + _nbytes(wdown.shape, BF16) + 6 * _nbytes((tb, V7X_MXU), F32)
               + _nbytes((8, 2 * dff), F32) + _nbytes((tb, dff), BF16) + 2 * _nbytes((tb, d), F32))
    xs = pl.BlockSpec((1, tb, d), lambda i, t: (i, t, 0))
    bs = pl.BlockSpec((1, width - 1, 2 * dff), lambda i, t: (i, 0, 0))
    return pl.pallas_call(
        kern,
        grid=(b, s // tb),
        in_specs=[xs, bs, _const_spec((1, d)), _const_spec(wup.shape), _const_spec(cw.shape),
                  _const_spec((1, 2 * dff)), _const_spec(wdown.shape), _const_spec((1, d))],
        out_specs=[xs, bs],
        out_shape=[jax.ShapeDtypeStruct((b, s, d), F32), jax.ShapeDtypeStruct((b, width - 1, 2 * dff), F32)],
        scratch_shapes=[pltpu.VMEM((V7X_SUBLANES, 2 * dff), F32), pltpu.VMEM((tb, dff), BF16)],
        compiler_params=pltpu.CompilerParams(
            dimension_semantics=("arbitrary", "arbitrary"), vmem_limit_bytes=_vmem_limit(blocks, scratch)),
        name="conv_ffn",
    )(x, buf, g2, wup, cw, cb, wdown, g3)


def _gate_fold_kernel(wq_ref, wk_ref, wv_ref, gq_ref, gk_ref, gv_ref, gc_ref, gm_ref):
    dotp = functools.partial(jnp.dot, preferred_element_type=F32, precision=HI)
    gc_ref[...] = dotp(wq_ref[0], gq_ref[...]) + dotp(wk_ref[0], gk_ref[...])
    gm_ref[...] = dotp(wv_ref[0], gv_ref[...])


def _gate_fold(wq_t, wk_t, wv_t, w_gate):
    nt, w, _ = wq_t.shape
    inner = nt * w
    ng = w_gate.shape[1]
    tile = pl.BlockSpec((1, w, w), lambda i: (i, 0, 0))
    gs = [pl.BlockSpec((w, ng), lambda i, k=k: (k * nt + i, 0)) for k in range(3)]
    out = pl.BlockSpec((w, ng), lambda i: (i, 0))
    return pl.pallas_call(
        _gate_fold_kernel,
        grid=(nt,),
        in_specs=[tile, tile, tile] + gs,
        out_specs=[out, out],
        out_shape=[jax.ShapeDtypeStruct((inner, ng), F32)] * 2,
        name="mlstm_gate_fold",
    )(wq_t, wk_t, wv_t, w_gate, w_gate, w_gate)


def _mpre_kernel(x_ref, cbuf_ref, g_ref, wup_ref, cw_ref, cb_ref, wq_ref, wkt_ref, wv_ref, gc_ref, gm_ref,
                 bg_ref, q_ref, kt_ref, v_ref, xc_ref, z_ref, gates_ref, nbuf_ref, carry,
                 *, tb, inner, width):
    t = pl.program_id(1)
    pad = V7X_SUBLANES

    @pl.when(t == 0)
    def _():
        carry[...] = jnp.zeros_like(carry)
        carry[pad - (width - 1):pad, :] = cbuf_ref[0]

    h = _rms(x_ref[0], g_ref[...]).astype(BF16)
    w = V7X_MXU
    gacc = jnp.zeros((tb, bg_ref.shape[1]), F32) + bg_ref[...]
    for i in range(inner // w):
        cs = slice(i * w, (i + 1) * w)
        xm = _dot(h, wup_ref[:, cs])
        prev = carry[:, cs]
        carry[:, cs] = xm[tb - pad:tb, :]
        xc = _silu(_causal_dwconv(xm, prev, cw_ref, cb_ref, cs)).astype(BF16)
        xmb = xm.astype(BF16)
        q_ref[0, :, cs] = _dot(xc, wq_ref[i]).astype(BF16)
        kt_ref[0, cs, :] = _dot_nt(wkt_ref[i], xc).astype(BF16)
        v_ref[0, :, cs] = _dot(xmb, wv_ref[i]).astype(BF16)
        xc_ref[0, :, cs] = xc
        z_ref[0, :, cs] = _silu(_dot(h, wup_ref[:, inner + i * w:inner + (i + 1) * w])).astype(BF16)
        gacc = gacc + _dot(xc, gc_ref[cs, :]) + _dot(xmb, gm_ref[cs, :])
    gates_ref[0] = gacc
    nbuf_ref[0] = carry[pad - (width - 1):pad, :]


def _mpre(x, cbuf, g, wup, cw, cb, wq_t, wkt_t, wv_t, gc, gm, bg, tb):
    b, s, d = x.shape
    inner = cw.shape[1]
    width = cw.shape[0]
    ng = bg.shape[1]
    kern = functools.partial(_mpre_kernel, tb=tb, inner=inner, width=width)
    act = pl.BlockSpec((1, tb, inner), lambda i, t: (i, t, 0))
    blocks = _nbytes((tb, d), F32) + 5 * _nbytes((tb, inner), BF16)
    scratch = (_nbytes(wup.shape, BF16) + 3 * _nbytes(wq_t.shape, BF16) + 2 * _nbytes((inner, 128), BF16)
               + _nbytes((8, inner), F32) + 8 * _nbytes((tb, V7X_MXU), F32))
    return pl.pallas_call(
        kern,
        grid=(b, s // tb),
        in_specs=[pl.BlockSpec((1, tb, d), lambda i, t: (i, t, 0)),
                  pl.BlockSpec((1, width - 1, inner), lambda i, t: (i, 0, 0)),
                  _const_spec((1, d)), _const_spec(wup.shape), _const_spec(cw.shape), _const_spec((1, inner)),
                  _const_spec(wq_t.shape), _const_spec(wkt_t.shape), _const_spec(wv_t.shape),
                  _const_spec(gc.shape), _const_spec(gm.shape), _const_spec((1, ng))],
        out_specs=[act, pl.BlockSpec((1, inner, tb), lambda i, t: (i, 0, t)), act, act, act,
                   pl.BlockSpec((1, tb, ng), lambda i, t: (i, t, 0)),
                   pl.BlockSpec((1, width - 1, inner), lambda i, t: (i, 0, 0))],
        out_shape=[jax.ShapeDtypeStruct((b, s, inner), BF16), jax.ShapeDtypeStruct((b, inner, s), BF16),
                   jax.ShapeDtypeStruct((b, s, inner), BF16), jax.ShapeDtypeStruct((b, s, inner), BF16),
                   jax.ShapeDtypeStruct((b, s, inner), BF16), jax.ShapeDtypeStruct((b, s, ng), F32),
                   jax.ShapeDtypeStruct((b, width - 1, inner), F32)],
        scratch_shapes=[pltpu.VMEM((V7X_SUBLANES, inner), F32)],
        compiler_params=pltpu.CompilerParams(
            dimension_semantics=("arbitrary", "arbitrary"), vmem_limit_bytes=_vmem_limit(blocks, scratch)),
        name="mlstm_pre",
    )(x, cbuf, g, wup, cw, cb, wq_t, wkt_t, wv_t, gc, gm, bg)


def _mlstm_chunk(q_ref, kt_ref, v_ref, g, gt, hn_ref, ct_s, n_s, m_s, *, lc, heads, dh):
    row = lax.broadcasted_iota(jnp.int32, (lc, lc), 0)
    col = lax.broadcasted_iota(jnp.int32, (lc, lc), 1)
    causal = col <= row
    tri_l = jnp.where(causal, 1.0, 0.0).astype(BF16)
    tri_u = jnp.where(row <= col, 1.0, 0.0).astype(BF16)
    b_cols = sum(_dot(tri_l, part) for part in _split3(_log_sigmoid(g)))
    b_rows = sum(_dot(part, tri_u) for part in _split3(_log_sigmoid(gt)))

    hsl = [slice(h * dh, (h + 1) * dh) for h in range(heads)]
    st = []
    for h in range(heads):
        b_col = b_cols[:, heads + h:heads + h + 1]
        b_row = b_rows[heads + h:heads + h + 1, :]
        i_row = gt[h:h + 1, :]
        m_prev = m_s[h:h + 1, 0:1]
        dmat = jnp.where(causal, b_col - b_row + i_row, -jnp.inf)
        inter = b_col + m_prev
        m_t = jnp.maximum(inter, jnp.max(dmat, axis=1, keepdims=True))
        st.append(dict(b_col=b_col, b_row=b_row, i_row=i_row, m_prev=m_prev, m_t=m_t,
                       wts=jnp.exp(dmat - m_t), a=jnp.exp(inter - m_t)))
    for h in range(heads):
        s = st[h]
        s['sc'] = _dot(q_ref[0, :, hsl[h]], kt_ref[0, hsl[h], :]) * s['wts']
    for h in range(heads):
        s = st[h]
        qh = q_ref[0, :, hsl[h]]
        sc = s['sc']
        a = s['a']
        num = _dot(sc.astype(BF16), v_ref[0, :, hsl[h]]) + a * _dot(qh, ct_s[h].astype(BF16))
        qn = jnp.sum(qh.astype(F32) * n_s[h:h + 1, :], axis=1, keepdims=True)
        den = jnp.sum(sc, axis=1, keepdims=True) + a * qn
        hh = num / jnp.maximum(jnp.abs(den), jnp.exp(-s['m_t']))
        mu = jnp.mean(hh, axis=1, keepdims=True)
        dev = hh - mu
        var = jnp.mean(dev * dev, axis=1, keepdims=True)
        hn_ref[0, :, hsl[h]] = (dev * lax.rsqrt(var + LN_EPS)).astype(hn_ref.dtype)
    for h in range(heads):
        s = st[h]
        kth = kt_ref[0, hsl[h], :]
        b_last = s['b_row'][:, lc - 1:lc]
        g_row = b_last - s['b_row'] + s['i_row']
        m_new = jnp.maximum(b_last + s['m_prev'], jnp.max(g_row, axis=1, keepdims=True))
        decay = jnp.exp(b_last + s['m_prev'] - m_new)
        wg_row = jnp.exp(g_row - m_new)
        wg_col = jnp.exp(b_last - s['b_col'] + g[:, h:h + 1] - m_new)
        wv = (v_ref[0, :, hsl[h]].astype(F32) * wg_col).astype(BF16)
        ct_s[h] = decay * ct_s[h] + _dot(kth, wv)
        wg16 = jnp.broadcast_to(wg_row, (16, lc)).astype(BF16)
        n_s[h:h + 1, :] = decay * n_s[h:h + 1, :] + _dot_nt(wg16, kth)[0:1, :]
        m_s[h:h + 1, :] = jnp.broadcast_to(m_new, (1, V7X_LANES))


def _mlstm_layer_kernel(*refs, lc, heads, inner, width, zero_init):
    (x_ref, cbuf_ref, g0_ref, wup_ref, cw_ref, cb_ref, wq_ref, wkt_ref, wv_ref, gc_ref, gm_ref, bg_ref,
     ng_ref, sk_ref, wdown_ref, g1_ref) = refs[:16]
    n_in = 16
    if not zero_init:
        c0_ref, n0_ref, m0_ref = refs[16:19]
        n_in = 19
    o_ref, cout_ref, nout_ref, mout_ref, nbuf_ref = refs[n_in:n_in + 5]
    carry, q_s, kt_s, v_s, xc_s, sz_s, hn_s, ct_s, n_s, m_s = refs[n_in + 5:]
    dh = inner // heads
    pad = V7X_SUBLANES
    c = pl.program_id(1)

    @pl.when(c == 0)
    def _():
        carry[...] = jnp.zeros_like(carry)
        carry[pad - (width - 1):pad, :] = cbuf_ref[0]
        if zero_init:
            ct_s[...] = jnp.zeros_like(ct_s)
            n_s[...] = jnp.zeros_like(n_s)
            m_s[...] = jnp.zeros_like(m_s)
        else:
            for h in range(heads):
                ct_s[h] = c0_ref[0, h].T
            n_s[...] = n0_ref[0]
            m_s[...] = m0_ref[0]

    x = x_ref[0]
    hx = _rms(x, g0_ref[...]).astype(BF16)
    w = V7X_MXU
    gacc = jnp.zeros((lc, bg_ref.shape[1]), F32) + bg_ref[...]
    for i in range(inner // w):
        cs = slice(i * w, (i + 1) * w)
        xm = _dot(hx, wup_ref[:, cs])
        prev = carry[:, cs]
        carry[:, cs] = xm[lc - pad:lc, :]
        xc = _silu(_causal_dwconv(xm, prev, cw_ref, cb_ref, cs)).astype(BF16)
        xmb = xm.astype(BF16)
        q_s[0, :, cs] = _dot(xc, wq_ref[i]).astype(BF16)
        kt_s[0, cs, :] = _dot_nt(wkt_ref[i], xc).astype(BF16)
        v_s[0, :, cs] = _dot(xmb, wv_ref[i]).astype(BF16)
        xc_s[:, cs] = xc
        gacc = gacc + _dot(xc, gc_ref[cs, :]) + _dot(xmb, gm_ref[cs, :])
    nbuf_ref[0] = carry[pad - (width - 1):pad, :]

    _mlstm_chunk(q_s, kt_s, v_s, gacc, gacc.T[0:2 * heads, :], hn_s, ct_s, n_s, m_s, lc=lc, heads=heads, dh=dh)

    for i in range(inner // w):
        cs = slice(i * w, (i + 1) * w)
        sz_s[:, cs] = _silu(_dot(hx, wup_ref[:, inner + i * w:inner + (i + 1) * w])).astype(BF16)

    hs = hn_s[0].astype(F32) * ng_ref[...] + sk_ref[...] * xc_s[...].astype(F32)
    act = (hs * sz_s[...].astype(F32)).astype(BF16)
    o_ref[0] = x + _rms(_dot(act, wdown_ref[...]), g1_ref[...])

    @pl.when(c == pl.num_programs(1) - 1)
    def _():
        for h in range(heads):
            cout_ref[0, h] = ct_s[h].T
        nout_ref[0] = n_s[...]
        mout_ref[0] = m_s[...]


def _mlstm_layer(x, cbuf, state, p, g0, g1, lc):
    b, s, d = x.shape
    heads = p['heads']
    inner = p['m_cw'].shape[1]
    width = p['m_cw'].shape[0]
    dh = inner // heads
    zero_init = state is None
    kern = functools.partial(_mlstm_layer_kernel, lc=lc, heads=heads, inner=inner, width=width,
                             zero_init=zero_init)
    xs = pl.BlockSpec((1, lc, d), lambda i, c: (i, c, 0))
    bufs = pl.BlockSpec((1, width - 1, inner), lambda i, c: (i, 0, 0))
    cs = pl.BlockSpec((1, heads, dh, dh), lambda i, c: (i, 0, 0, 0))
    ns = pl.BlockSpec((1, heads, dh), lambda i, c: (i, 0, 0))
    ms = pl.BlockSpec((1, heads, V7X_LANES), lambda i, c: (i, 0, 0))
    consts = [g0, p['m_wup'], p['m_cw'], p['m_cb'], p['m_wq'], p['m_wkt'], p['m_wv'], p['m_gc'], p['m_gm'],
              p['m_bg'], p['m_ng'], p['m_skip'], p['m_wdown'], g1]
    in_specs = [xs, bufs] + [_const_spec(a.shape) for a in consts]
    args = [x, cbuf] + consts
    if not zero_init:
        in_specs += [cs, ns, ms]
        args += list(state)
    blocks = 2 * _nbytes((lc, d), F32) + (1 if zero_init else 2) * _nbytes((heads, dh, dh), F32)
    scratch = (sum(_nbytes(a.shape, a.dtype) for a in consts) + 6 * _nbytes((lc, inner), BF16)
               + _nbytes((heads, dh, dh), F32) + 8 * _nbytes((lc, max(lc, dh)), F32) + 2 * _nbytes((dh, dh), F32))
    return pl.pallas_call(
        kern,
        grid=(b, s // lc),
        in_specs=in_specs,
        out_specs=[xs, cs, ns, ms, bufs],
        out_shape=[jax.ShapeDtypeStruct((b, s, d), F32), jax.ShapeDtypeStruct((b, heads, dh, dh), F32),
                   jax.ShapeDtypeStruct((b, heads, dh), F32), jax.ShapeDtypeStruct((b, heads, V7X_LANES), F32),
                   jax.ShapeDtypeStruct((b, width - 1, inner), F32)],
        scratch_shapes=[pltpu.VMEM((V7X_SUBLANES, inner), F32),
                        pltpu.VMEM((1, lc, inner), BF16), pltpu.VMEM((1, inner, lc), BF16),
                        pltpu.VMEM((1, lc, inner), BF16), pltpu.VMEM((lc, inner), BF16),
                        pltpu.VMEM((lc, inner), BF16), pltpu.VMEM((1, lc, inner), BF16),
                        pltpu.VMEM((heads, dh, dh), F32), pltpu.VMEM((heads, dh), F32),
                        pltpu.VMEM((heads, V7X_LANES), F32)],
        compiler_params=pltpu.CompilerParams(
            dimension_semantics=("arbitrary", "arbitrary"), vmem_limit_bytes=_vmem_limit(blocks, scratch)),
        name="mlstm_layer",
    )(*args)


def _block_diag_tiles(w):
    nblk, k, _ = w.shape
    per = V7X_MXU // k
    wt = w.reshape(nblk // per, per, k, k)
    eye = jnp.eye(per, dtype=w.dtype)
    full = wt[:, :, :, None, :] * eye[None, :, None, :, None]
    return full.reshape(nblk // per, per * k, per * k)


def _prep_weights(w):
    row = lambda v: v.astype(F32).reshape(1, -1)
    p = {}
    p['norm_g'] = w['norm_g'].astype(F32)
    p['s5_mats'] = _s5_matrices(w['s5_A_re'][0], w['s5_A_im'][0], w['s5_log_dt'][0], w['s5_B_re'][0],
                                w['s5_B_im'][0], w['s5_C_re'][0], w['s5_C_im'][0])
    p['s5_D'] = row(w['s5_D'][0])
    p['s5_w_glu'] = w['s5_w_glu'][0].astype(BF16)
    inner = w['mlstm_conv_w'].shape[2]
    heads = w['mlstm_b_gate'].shape[1] // 2
    dh = inner // heads
    wq_t = _block_diag_tiles(w['mlstm_wq'][0].astype(F32))
    wk_t = _block_diag_tiles(w['mlstm_wk'][0].astype(F32))
    wv_t = _block_diag_tiles(w['mlstm_wv'][0].astype(F32))
    gc, gm = _gate_fold(wq_t, wk_t, wv_t, w['mlstm_w_gate'][0].astype(F32))
    p['m_wq'] = wq_t.astype(BF16)
    p['m_wkt'] = (jnp.swapaxes(wk_t, 1, 2) * (dh ** -0.5)).astype(BF16)
    p['m_wv'] = wv_t.astype(BF16)
    lane_pad = lambda a: jnp.pad(a, ((0, 0), (0, V7X_LANES - a.shape[1])))
    p['m_gc'] = lane_pad(gc).astype(BF16)
    p['m_gm'] = lane_pad(gm).astype(BF16)
    p['m_bg'] = lane_pad(row(w['mlstm_b_gate'][0]))
    p['m_wup'] = w['mlstm_w_up'][0].astype(BF16)
    p['m_cw'] = w['mlstm_conv_w'][0].astype(F32)
    p['m_cb'] = row(w['mlstm_conv_b'][0])
    p['m_ng'] = row(w['mlstm_norm_g'][0])
    p['m_skip'] = row(w['mlstm_skip'][0])
    p['m_wdown'] = w['mlstm_w_down'][0].astype(BF16)
    p['heads'] = heads
    p['f_wup'] = w['ffn_w_up'].astype(BF16)
    p['f_cw'] = w['ffn_conv_w'].astype(F32)
    p['f_cb'] = w['ffn_conv_b'].astype(F32)
    p['f_wdown'] = w['ffn_w_down'].astype(BF16)
    return p


def _to_chunk_major(u2d, b, s):
    d = u2d.shape[1]
    g = d // S5_GROUP_CH
    u = u2d.reshape(b, s // S5_CHUNK, S5_CHUNK, g, S5_GROUP_CH)
    return jnp.transpose(u, (1, 0, 3, 2, 4)).reshape((s // S5_CHUNK) * b, d * S5_CHUNK)


def _to_token_major(yt, b, s):
    d = yt.shape[1] // S5_CHUNK
    g = d // S5_GROUP_CH
    y = yt.reshape(s // S5_CHUNK, b, g, S5_CHUNK, S5_GROUP_CH)
    return jnp.transpose(y, (1, 0, 3, 2, 4)).reshape(b * s, d)


def _trunk(x, st, p, tiles):
    b, s, d = x.shape
    ng = p['norm_g']
    g_of = lambda layer, k: ng[layer, k].reshape(1, d)
    zero_init = st is None
    groups = d // S5_GROUP_CH
    x2d = x.reshape(b * s, d)

    n_state = p['s5_mats']['row'][1].shape[2] // 2
    if zero_init:
        h0re = h0im = jnp.zeros((groups // 2, b, 2 * n_state), F32)
    else:
        pair = lambda h: jnp.transpose(h.astype(F32).reshape(b, groups // 2, 2 * n_state), (1, 0, 2))
        h0re, h0im = pair(st['s5_re']), pair(st['s5_im'])
    if tiles.s5_time:
        y, hre, him = _s5_fused(x, g_of(0, 0), p['s5_mats']['col'], h0re, h0im, tiles.s5_time)
        y2d = y.reshape(b * s, d)
    else:
        u = _norm_cast(x2d, g_of(0, 0), tiles.rows)
        yt, hre, him = _s5_core(_to_chunk_major(u, b, s), p['s5_mats']['row'], h0re, h0im, b, tiles.s5_rows)
        y2d = _to_token_major(yt, b, s)
    unpair = lambda h: jnp.transpose(h, (1, 0, 2)).reshape(1, b, groups, n_state)
    o_re, o_im = unpair(hre), unpair(him)
    x2d = _glu(x2d, y2d, g_of(0, 0), p['s5_D'], p['s5_w_glu'], g_of(0, 1), tiles.rows)

    def ffn(x2d, layer):
        dff2 = p['f_wup'].shape[2]
        width = p['f_cw'].shape[1]
        buf = jnp.zeros((b, width - 1, dff2), F32) if zero_init else st['ffn_conv'][layer].astype(F32)
        xo, nbuf = _ffn(x2d.reshape(b, s, d), buf, g_of(layer, 2), p['f_wup'][layer], p['f_cw'][layer],
                        p['f_cb'][layer].reshape(1, dff2), p['f_wdown'][layer], g_of(layer, 3), tiles.ffn_time)
        return xo.reshape(b * s, d), nbuf

    x2d, fbuf0 = ffn(x2d, 0)

    heads = p['heads']
    inner = p['m_cw'].shape[1]
    width = p['m_cw'].shape[0]
    cbuf = jnp.zeros((b, width - 1, inner), F32) if zero_init else st['mlstm_conv'].astype(F32)
    if zero_init:
        state = None
    else:
        state = (st['mlstm_C'].astype(F32), st['mlstm_n'].astype(F32),
                 jnp.broadcast_to(st['mlstm_m'].astype(F32)[:, :, None], (b, heads, V7X_LANES)))
    x3, c_out, n_out, m_out, ncbuf = _mlstm_layer(x2d.reshape(b, s, d), cbuf, state, p, g_of(1, 0), g_of(1, 1),
                                                  tiles.mlstm_chunk)
    x2d, fbuf1 = ffn(x3.reshape(b * s, d), 1)

    return (x2d.reshape(b, s, d), o_re, o_im, c_out[None], n_out[None], m_out[None, :, :, 0], ncbuf[None],
            jnp.stack([fbuf0, fbuf1], axis=0))


def kernel(x_prompt, x_sample, state_s5_re, state_s5_im, state_mlstm_C, state_mlstm_n, state_mlstm_m, state_mlstm_conv, state_ffn_conv, norm_g, s5_A_re, s5_A_im, s5_log_dt, s5_B_re, s5_B_im, s5_C_re, s5_C_im, s5_D, s5_w_glu, mlstm_w_up, mlstm_conv_w, mlstm_conv_b, mlstm_wq, mlstm_wk, mlstm_wv, mlstm_w_gate, mlstm_b_gate, mlstm_norm_g, mlstm_skip, mlstm_w_down, ffn_w_up, ffn_conv_w, ffn_conv_b, ffn_w_down):
    w = {
        'norm_g': norm_g, 's5_A_re': s5_A_re, 's5_A_im': s5_A_im, 's5_log_dt': s5_log_dt,
        's5_B_re': s5_B_re, 's5_B_im': s5_B_im, 's5_C_re': s5_C_re, 's5_C_im': s5_C_im, 's5_D': s5_D,
        's5_w_glu': s5_w_glu, 'mlstm_w_up': mlstm_w_up, 'mlstm_conv_w': mlstm_conv_w,
        'mlstm_conv_b': mlstm_conv_b, 'mlstm_wq': mlstm_wq, 'mlstm_wk': mlstm_wk, 'mlstm_wv': mlstm_wv,
        'mlstm_w_gate': mlstm_w_gate, 'mlstm_b_gate': mlstm_b_gate, 'mlstm_norm_g': mlstm_norm_g,
        'mlstm_skip': mlstm_skip, 'mlstm_w_down': mlstm_w_down, 'ffn_w_up': ffn_w_up,
        'ffn_conv_w': ffn_conv_w, 'ffn_conv_b': ffn_conv_b, 'ffn_w_down': ffn_w_down,
    }
    assert norm_g.shape[0] == 2 and s5_A_re.shape[0] == 1 and mlstm_w_up.shape[0] == 1
    assert x_prompt.shape[0] == V7X_SUBLANES and x_sample.shape[0] == V7X_SUBLANES
    p = _prep_weights(w)
    out_p = _trunk(x_prompt.astype(F32), None, p, _pick_tiles(*x_prompt.shape[:2]))
    st = {'s5_re': state_s5_re[0], 's5_im': state_s5_im[0], 'mlstm_C': state_mlstm_C[0],
          'mlstm_n': state_mlstm_n[0], 'mlstm_m': state_mlstm_m[0], 'mlstm_conv': state_mlstm_conv[0],
          'ffn_conv': state_ffn_conv}
    out_s = _trunk(x_sample.astype(F32), st, p, _pick_tiles(*x_sample.shape[:2]))
    return (out_p[0], out_s[0]) + tuple(out_p[1:]) + tuple(out_s[1:])
```

```python
import functools
import math
from typing import NamedTuple

import jax
import jax.numpy as jnp
from jax import lax
from jax.experimental import pallas as pl
from jax.experimental.pallas import tpu as pltpu

F32 = jnp.float32
BF16 = jnp.bfloat16
HI = lax.Precision.HIGHEST

NORM_EPS = 1e-6
LN_EPS = 1e-5
S5_GROUP_CH = 16
S5_CHUNK = 16
QKV_BLOCK = 4

V7X_LANES = 128
V7X_SUBLANES = 8
V7X_MXU = 256
V7X_VMEM_BYTES = 64 * 1024 * 1024


class Tiles(NamedTuple):
    rows: int
    time: int
    s5_rows: int
    mlstm_chunk: int
    s5_time: int
    ffn_time: int


def _pick_tiles(batch, seq):
    s5_time = (V7X_LANES // batch) * S5_CHUNK
    return Tiles(rows=min(batch * seq, 512), time=min(seq, 512),
                 s5_rows=min(batch * (seq // S5_CHUNK), 512), mlstm_chunk=min(seq, 256),
                 s5_time=s5_time if seq % s5_time == 0 else 0, ffn_time=min(seq, 1024))


def _vmem_limit(block_bytes, scratch_bytes):
    want = 2 * block_bytes + scratch_bytes + 16 * 1024 * 1024
    return int(min(want, V7X_VMEM_BYTES - 8 * 1024 * 1024))


def _nbytes(shape, dtype):
    return math.prod(shape) * jnp.dtype(dtype).itemsize


def _rms(x, g):
    return x * lax.rsqrt(jnp.mean(x * x, axis=-1, keepdims=True) + NORM_EPS) * g


def _dot(a, b):
    return jnp.dot(a, b, preferred_element_type=F32)


def _dot_nt(a, b):
    return lax.dot_general(a, b, (((1,), (1,)), ((), ())), preferred_element_type=F32)


def _silu(x):
    return (0.5 * x) * (1.0 + jnp.tanh(0.5 * x))


def _causal_dwconv(u, prev, cw_ref, cb_ref, cs):
    width = cw_ref.shape[0]
    acc = cb_ref[:, cs] + u * cw_ref[width - 1:width, cs]
    rows = lax.broadcasted_iota(jnp.int32, prev.shape, 0)
    for j in range(width - 1):
        d = width - 1 - j
        rolled = pltpu.roll(u, d, 0)
        head = jnp.where(rows < d, pltpu.roll(prev, d, 0), rolled[0:V7X_SUBLANES])
        shifted = jnp.concatenate([head, rolled[V7X_SUBLANES:]], axis=0)
        acc = acc + shifted * cw_ref[j:j + 1, cs]
    return acc


def _log_sigmoid(x):
    return jnp.minimum(x, 0.0) - jnp.log1p(jnp.exp(-jnp.abs(x)))


def _split3(x):
    hi = x.astype(BF16)
    r1 = x - hi.astype(F32)
    mid = r1.astype(BF16)
    lo = (r1 - mid.astype(F32)).astype(BF16)
    return hi, mid, lo


def _norm_cast_kernel(x_ref, g_ref, o_ref):
    o_ref[...] = _rms(x_ref[...], g_ref[...]).astype(o_ref.dtype)


def _norm_cast(x2d, g, tb):
    rows, d = x2d.shape
    return pl.pallas_call(
        _norm_cast_kernel,
        grid=(rows // tb,),
        in_specs=[pl.BlockSpec((tb, d), lambda i: (i, 0)), pl.BlockSpec((1, d), lambda i: (0, 0))],
        out_specs=pl.BlockSpec((tb, d), lambda i: (i, 0)),
        out_shape=jax.ShapeDtypeStruct((rows, d), BF16),
        name="s5_norm",
    )(x2d, g)


def _s5_kernel(u_ref, t_ref, pre_ref, pim_ref, qre_ref, qim_ref, are_ref, aim_ref, h0re_ref, h0im_ref,
               y_ref, hre_ref, him_ref, pure_s, puim_s, hinre_s, hinim_s, stre_s, stim_s, *, nb, rb):
    r = pl.program_id(1)

    @pl.when(r == 0)
    def _():
        stre_s[...] = h0re_ref[0]
        stim_s[...] = h0im_ref[0]

    u = u_ref[...]
    pure_s[...] = _dot(u, pre_ref[0])
    puim_s[...] = _dot(u, pim_ref[0])
    ar = jnp.broadcast_to(are_ref[0], (nb, 2 * 64))
    ai = jnp.broadcast_to(aim_ref[0], (nb, 2 * 64))

    def step(i, carry):
        re, im = carry
        rows = pl.ds(pl.multiple_of(i * nb, nb), nb)
        hinre_s[rows, :] = re
        hinim_s[rows, :] = im
        return (ar * re - ai * im + pure_s[rows, :], ar * im + ai * re + puim_s[rows, :])

    re, im = lax.fori_loop(0, rb // nb, step, (stre_s[...], stim_s[...]))
    stre_s[...] = re
    stim_s[...] = im
    hre_ref[0] = re
    him_ref[0] = im

    carry_in = _dot(hinre_s[...].astype(BF16), qre_ref[0]) + _dot(hinim_s[...].astype(BF16), qim_ref[0])
    w = V7X_MXU
    for k in range(2):
        y_ref[:, k * w:(k + 1) * w] = (_dot(u[:, k * w:(k + 1) * w], t_ref[k])
                                       + carry_in[:, k * w:(k + 1) * w]).astype(y_ref.dtype)


def _s5_core(ut, mats, h0re, h0im, nb, rb):
    rows, cols = ut.shape
    pairs = cols // (2 * V7X_MXU)
    tm, pre, pim, qre, qim, are, aim = mats
    pair3 = lambda p, r: (p, 0, 0)
    kern = functools.partial(_s5_kernel, nb=nb, rb=rb)
    blocks = (_nbytes((rb, 512), BF16) * 2 + _nbytes((2, 256, 256), BF16) + 4 * _nbytes((512, 128), BF16))
    scratch = 4 * _nbytes((rb, 128), F32) + 2 * _nbytes((nb, 128), F32)
    return pl.pallas_call(
        kern,
        grid=(pairs, rows // rb),
        in_specs=[
            pl.BlockSpec((rb, 2 * V7X_MXU), lambda p, r: (r, p)),
            pl.BlockSpec((2, V7X_MXU, V7X_MXU), pair3),
            pl.BlockSpec((1, 2 * V7X_MXU, 128), pair3),
            pl.BlockSpec((1, 2 * V7X_MXU, 128), pair3),
            pl.BlockSpec((1, 128, 2 * V7X_MXU), pair3),
            pl.BlockSpec((1, 128, 2 * V7X_MXU), pair3),
            pl.BlockSpec((1, 1, 128), pair3),
            pl.BlockSpec((1, 1, 128), pair3),
            pl.BlockSpec((1, nb, 128), pair3),
            pl.BlockSpec((1, nb, 128), pair3),
        ],
        out_specs=[
            pl.BlockSpec((rb, 2 * V7X_MXU), lambda p, r: (r, p)),
            pl.BlockSpec((1, nb, 128), pair3),
            pl.BlockSpec((1, nb, 128), pair3),
        ],
        out_shape=[
            jax.ShapeDtypeStruct((rows, cols), BF16),
            jax.ShapeDtypeStruct((pairs, nb, 128), F32),
            jax.ShapeDtypeStruct((pairs, nb, 128), F32),
        ],
        scratch_shapes=[pltpu.VMEM((rb, 128), F32)] * 4 + [pltpu.VMEM((nb, 128), F32)] * 2,
        compiler_params=pltpu.CompilerParams(
            dimension_semantics=("arbitrary", "arbitrary"), vmem_limit_bytes=_vmem_limit(blocks, scratch)),
        name="s5_core",
    )(ut, tm, pre, pim, qre, qim, are, aim, h0re, h0im)


def _s5_fused_kernel(x_ref, g_ref, tt_ref, pt_ref, qt_ref, are_ref, aim_ref, h0re_ref, h0im_ref,
                     y_ref, hre_ref, him_ref,
                     us, ugt, pure, puim, hinre, hinim, ytmp, ys, stre, stim, *, nb, tbt):
    lc = S5_CHUNK
    ch = S5_GROUP_CH
    gps = V7X_LANES // ch
    nch = tbt // lc
    cols = nb * nch
    k = pl.program_id(1)

    @pl.when(k == 0)
    def _():
        for b in range(nb):
            u = _rms(x_ref[b], g_ref[...])
            for kk in range(us.shape[0]):
                us[kk, b * tbt:(b + 1) * tbt, :] = u[:, kk * V7X_LANES:(kk + 1) * V7X_LANES]

    @pl.when(pl.program_id(0) == 0)
    def _():
        stre[k] = h0re_ref[...]
        stim[k] = h0im_ref[...]

    for s in range(lc):
        a = us[k, pl.ds(s, cols, stride=lc), :].T
        for g in range(gps):
            ugt[g, s * ch:(s + 1) * ch, :] = a[g * ch:(g + 1) * ch, :].astype(BF16)

    half = pt_ref.shape[1] // 2
    for pr in range(gps // 2):
        put0 = _dot(pt_ref[2 * pr], ugt[2 * pr])
        put1 = _dot(pt_ref[2 * pr + 1], ugt[2 * pr + 1])
        pure[pr] = jnp.concatenate([put0[:half], put1[:half]], axis=0).T
        puim[pr] = jnp.concatenate([put0[half:], put1[half:]], axis=0).T

    for pr in range(gps // 2):
        re = stre[k, pr]
        im = stim[k, pr]
        ar = jnp.broadcast_to(are_ref[pr], re.shape)
        ai = jnp.broadcast_to(aim_ref[pr], re.shape)
        for j in range(nch):
            rows = pl.ds(j, nb, stride=nch)
            hinre[pr, rows, :] = re
            hinim[pr, rows, :] = im
            re, im = (ar * re - ai * im + pure[pr, rows, :], ar * im + ai * re + puim[pr, rows, :])
        stre[k, pr] = re
        stim[k, pr] = im
        hre_ref[k * (gps // 2) + pr] = re
        him_ref[k * (gps // 2) + pr] = im

    for pr in range(gps // 2):
        hre_t = hinre[pr].T
        him_t = hinim[pr].T
        for i in range(2):
            g = 2 * pr + i
            hin_t = jnp.concatenate([hre_t[i * half:(i + 1) * half], him_t[i * half:(i + 1) * half]],
                                    axis=0).astype(BF16)
            ytmp[g] = _dot(tt_ref[g], ugt[g]) + _dot(qt_ref[g], hin_t)

    for t in range(lc):
        z = jnp.concatenate([ytmp[g, t * ch:(t + 1) * ch, :] for g in range(gps)], axis=0)
        ys[pl.ds(t, cols, stride=lc), :] = z.T
    for b in range(nb):
        y_ref[b] = ys[b * tbt:(b + 1) * tbt, :].astype(y_ref.dtype)


def _s5_fused(x, g, mats_t, h0re, h0im, tbt):
    nb, s, d = x.shape
    tt, pt, qt, are, aim = mats_t
    groups = tt.shape[0]
    gps = V7X_LANES // S5_GROUP_CH
    slabs = groups // gps
    w = S5_CHUNK * S5_GROUP_CH
    nst = pt.shape[1]
    cols = nb * (tbt // S5_CHUNK)
    kern = functools.partial(_s5_fused_kernel, nb=nb, tbt=tbt)
    slab3 = lambda t, k: (k, 0, 0)
    blocks = (_nbytes((nb, tbt, d), F32) + _nbytes((nb, tbt, V7X_LANES), BF16)
              + gps * (_nbytes((w, w), BF16) + 2 * _nbytes((w, nst), BF16)))
    scratch = (_nbytes((slabs, nb * tbt, V7X_LANES), F32) + _nbytes((gps, w, cols), BF16)
               + 4 * _nbytes((gps // 2, cols, nst), F32) + _nbytes((gps, w, cols), F32)
               + _nbytes((nb * tbt, V7X_LANES), F32))
    return pl.pallas_call(
        kern,
        grid=(s // tbt, slabs),
        in_specs=[
            pl.BlockSpec((nb, tbt, d), lambda t, k: (0, t, 0)),
            pl.BlockSpec((1, d), lambda t, k: (0, 0)),
            pl.BlockSpec((gps, w, w), slab3),
            pl.BlockSpec((gps, nst, w), slab3),
            pl.BlockSpec((gps, w, nst), slab3),
            pl.BlockSpec((gps // 2, 1, nst), slab3),
            pl.BlockSpec((gps // 2, 1, nst), slab3),
            pl.BlockSpec((gps // 2, nb, nst), slab3),
            pl.BlockSpec((gps // 2, nb, nst), slab3),
        ],
        out_specs=[
            pl.BlockSpec((nb, tbt, V7X_LANES), lambda t, k: (0, t, k)),
            pl.BlockSpec((groups // 2, nb, nst), lambda t, k: (0, 0, 0)),
            pl.BlockSpec((groups // 2, nb, nst), lambda t, k: (0, 0, 0)),
        ],
        out_shape=[
            jax.ShapeDtypeStruct((nb, s, d), BF16),
            jax.ShapeDtypeStruct((groups // 2, nb, nst), F32),
            jax.ShapeDtypeStruct((groups // 2, nb, nst), F32),
        ],
        scratch_shapes=[
            pltpu.VMEM((slabs, nb * tbt, V7X_LANES), F32),
            pltpu.VMEM((gps, w, cols), BF16),
            pltpu.VMEM((gps // 2, cols, nst), F32),
            pltpu.VMEM((gps // 2, cols, nst), F32),
            pltpu.VMEM((gps // 2, cols, nst), F32),
            pltpu.VMEM((gps // 2, cols, nst), F32),
            pltpu.VMEM((gps, w, cols), F32),
            pltpu.VMEM((nb * tbt, V7X_LANES), F32),
            pltpu.VMEM((slabs, gps // 2, nb, nst), F32),
            pltpu.VMEM((slabs, gps // 2, nb, nst), F32),
        ],
        compiler_params=pltpu.CompilerParams(
            dimension_semantics=("arbitrary", "arbitrary"), vmem_limit_bytes=_vmem_limit(blocks, scratch)),
        name="s5_fused",
    )(x, g, tt, pt, qt, are, aim, h0re, h0im)


def _s5_matrices(a_re, a_im, log_dt, b_re, b_im, c_re, c_im):
    lc = S5_CHUNK
    groups, n = a_re.shape
    ch = S5_GROUP_CH
    a = lax.complex(a_re.astype(F32), a_im.astype(F32))
    adt = a * jnp.exp(log_dt.astype(F32))[:, None]
    j = jnp.arange(lc + 1, dtype=F32)
    apow = jnp.exp(adt[None] * j[:, None, None])
    b_bar = ((apow[1] - 1.0) / a)[..., None] * lax.complex(b_re.astype(F32), b_im.astype(F32))
    c_mat = lax.complex(c_re.astype(F32), c_im.astype(F32))
    ca = c_mat[None] * apow[:, :, None, :]
    kern = (jnp.einsum('jgcn,gnd->jgcd', ca.real, b_bar.real, precision=HI)
            - jnp.einsum('jgcn,gnd->jgcd', ca.imag, b_bar.imag, precision=HI))
    s_idx = jnp.arange(lc)[:, None]
    t_idx = jnp.arange(lc)[None, :]
    lag = t_idx - s_idx
    kt = kern[jnp.clip(lag, 0, lc)]
    kt = jnp.where((lag >= 0)[:, :, None, None, None], kt, 0.0)
    tm = jnp.transpose(kt, (2, 0, 4, 1, 3)).reshape(groups, lc * ch, lc * ch)
    pw = apow[lc - 1 - jnp.arange(lc)]
    p = pw[:, :, :, None] * b_bar[None]
    p = jnp.transpose(p, (1, 0, 3, 2)).reshape(groups, lc * ch, n)
    q = ca[1:]
    q = jnp.transpose(q, (1, 3, 0, 2)).reshape(groups, n, lc * ch)

    def pair_rows(m):
        g2 = groups // 2
        m = m.reshape(g2, 2, m.shape[1], m.shape[2])
        z = jnp.zeros_like(m[:, 0])
        top = jnp.concatenate([m[:, 0], z], axis=2)
        bot = jnp.concatenate([z, m[:, 1]], axis=2)
        return jnp.concatenate([top, bot], axis=1)

    pre = pair_rows(p.real).astype(BF16)
    pim = pair_rows(p.imag).astype(BF16)
    qre = pair_rows(q.real).astype(BF16)
    qim = pair_rows(-q.imag).astype(BF16)
    a_chunk = apow[lc].reshape(groups // 2, 1, 2 * n)
    are, aim = a_chunk.real, a_chunk.imag
    tt = jnp.swapaxes(tm, 1, 2).astype(BF16)
    pt = jnp.concatenate([jnp.swapaxes(p.real, 1, 2), jnp.swapaxes(p.imag, 1, 2)], axis=1).astype(BF16)
    qt = jnp.concatenate([jnp.swapaxes(q.real, 1, 2), jnp.swapaxes(-q.imag, 1, 2)], axis=2).astype(BF16)
    return {'row': (tm.astype(BF16), pre, pim, qre, qim, are, aim), 'col': (tt, pt, qt, are, aim)}


def _glu_kernel(x_ref, y_ref, g0_ref, d_ref, w_ref, g1_ref, o_ref):
    x = x_ref[...]
    d = x.shape[1]
    u = _rms(x, g0_ref[...])
    yy = y_ref[...].astype(F32) + d_ref[...] * u
    z = jax.nn.gelu(yy, approximate=True).astype(BF16)
    ag = _dot(z, w_ref[...])
    out = ag[:, :d] * jax.nn.sigmoid(ag[:, d:])
    o_ref[...] = x + _rms(out, g1_ref[...])


def _const_spec(shape):
    nd = len(shape)
    return pl.BlockSpec(shape, lambda *_: (0,) * nd, pipeline_mode=pl.Buffered(1))


def _glu(x2d, y2d, g0, dskip, w, g1, tb):
    rows, d = x2d.shape
    blocks = 2 * _nbytes((tb, d), F32) + _nbytes((tb, d), BF16)
    scratch = _nbytes(w.shape, BF16) + 3 * _nbytes((tb, 2 * d), F32)
    row = pl.BlockSpec((tb, d), lambda i: (i, 0))
    return pl.pallas_call(
        _glu_kernel,
        grid=(rows // tb,),
        in_specs=[row, row, _const_spec((1, d)), _const_spec((1, d)), _const_spec(w.shape), _const_spec((1, d))],
        out_specs=row,
        out_shape=jax.ShapeDtypeStruct((rows, d), F32),
        compiler_params=pltpu.CompilerParams(
            dimension_semantics=("arbitrary",), vmem_limit_bytes=_vmem_limit(blocks, scratch)),
        name="s5_glu",
    )(x2d, y2d, g0, dskip, w, g1)


def _ffn_kernel(x_ref, buf_ref, g2_ref, wup_ref, cw_ref, cb_ref, wdown_ref, g3_ref,
                o_ref, nbuf_ref, carry, act, *, tb, dff, width):
    t = pl.program_id(1)
    pad = V7X_SUBLANES

    @pl.when(t == 0)
    def _():
        carry[...] = jnp.zeros_like(carry)
        carry[pad - (width - 1):pad, :] = buf_ref[0]

    x = x_ref[0]
    hn = _rms(x, g2_ref[...]).astype(BF16)
    fb = V7X_MXU

    def conv(cs):
        u = _dot(hn, wup_ref[:, cs])
        prev = carry[:, cs]
        carry[:, cs] = u[tb - pad:tb, :]
        return _causal_dwconv(u, prev, cw_ref, cb_ref, cs)

    for j in range(dff // fb):
        gate = conv(slice(j * fb, (j + 1) * fb))
        val = conv(slice(dff + j * fb, dff + (j + 1) * fb))
        act[:, j * fb:(j + 1) * fb] = (jax.nn.gelu(gate, approximate=True) * val).astype(BF16)

    f = _dot(act[...], wdown_ref[...])
    o_ref[0] = x + _rms(f, g3_ref[...])
    nbuf_ref[0] = carry[pad - (width - 1):pad, :]


def _ffn(x, buf, g2, wup, cw, cb, wdown, g3, tb):
    b, s, d = x.shape
    dff = wdown.shape[0]
    width = cw.shape[0]
    kern = functools.partial(_ffn_kernel, tb=tb, dff=dff, width=width)
    blocks = 2 * _nbytes((tb, d), F32)
    scratch = (_nbytes(wup.shape, BF16) + _nbytes(wdown.shape, BF16) + 6 * _nbytes((tb, V7X_MXU), F32)
               + _nbytes((8, 2 * dff), F32) + _nbytes((tb, dff), BF16) + 2 * _nbytes((tb, d), F32))
    xs = pl.BlockSpec((1, tb, d), lambda i, t: (i, t, 0))
    bs = pl.BlockSpec((1, width - 1, 2 * dff), lambda i, t: (i, 0, 0))
    return pl.pallas_call(
        kern,
        grid=(b, s // tb),
        in_specs=[xs, bs, _const_spec((1, d)), _const_spec(wup.shape), _const_spec(cw.shape),
                  _const_spec((1, 2 * dff)), _const_spec(wdown.shape), _const_spec((1, d))],
        out_specs=[xs, bs],
        out_shape=[jax.ShapeDtypeStruct((b, s, d), F32), jax.ShapeDtypeStruct((b, width - 1, 2 * dff), F32)],
        scratch_shapes=[pltpu.VMEM((V7X_SUBLANES, 2 * dff), F32), pltpu.VMEM((tb, dff), BF16)],
        compiler_params=pltpu.CompilerParams(
            dimension_semantics=("arbitrary", "arbitrary"), vmem_limit_bytes=_vmem_limit(blocks, scratch)),
        name="conv_ffn",
    )(x, buf, g2, wup, cw, cb, wdown, g3)


def _gate_fold_kernel(wq_ref, wk_ref, wv_ref, gq_ref, gk_ref, gv_ref, gc_ref, gm_ref):
    dotp = functools.partial(jnp.dot, preferred_element_type=F32, precision=HI)
    gc_ref[...] = dotp(wq_ref[0], gq_ref[...]) + dotp(wk_ref[0], gk_ref[...])
    gm_ref[...] = dotp(wv_ref[0], gv_ref[...])


def _gate_fold(wq_t, wk_t, wv_t, w_gate):
    nt, w, _ = wq_t.shape
    inner = nt * w
    ng = w_gate.shape[1]
    tile = pl.BlockSpec((1, w, w), lambda i: (i, 0, 0))
    gs = [pl.BlockSpec((w, ng), lambda i, k=k: (k * nt + i, 0)) for k in range(3)]
    out = pl.BlockSpec((w, ng), lambda i: (i, 0))
    return pl.pallas_call(
        _gate_fold_kernel,
        grid=(nt,),
        in_specs=[tile, tile, tile] + gs,
        out_specs=[out, out],
        out_shape=[jax.ShapeDtypeStruct((inner, ng), F32)] * 2,
        name="mlstm_gate_fold",
    )(wq_t, wk_t, wv_t, w_gate, w_gate, w_gate)


def _mpre_kernel(x_ref, cbuf_ref, g_ref, wup_ref, cw_ref, cb_ref, wq_ref, wkt_ref, wv_ref, gc_ref, gm_ref,
                 bg_ref, q_ref, kt_ref, v_ref, xc_ref, z_ref, gates_ref, nbuf_ref, carry,
                 *, tb, inner, width):
    t = pl.program_id(1)
    pad = V7X_SUBLANES

    @pl.when(t == 0)
    def _():
        carry[...] = jnp.zeros_like(carry)
        carry[pad - (width - 1):pad, :] = cbuf_ref[0]

    h = _rms(x_ref[0], g_ref[...]).astype(BF16)
    w = V7X_MXU
    gacc = jnp.zeros((tb, bg_ref.shape[1]), F32) + bg_ref[...]
    for i in range(inner // w):
        cs = slice(i * w, (i + 1) * w)
        xm = _dot(h, wup_ref[:, cs])
        prev = carry[:, cs]
        carry[:, cs] = xm[tb - pad:tb, :]
        xc = _silu(_causal_dwconv(xm, prev, cw_ref, cb_ref, cs)).astype(BF16)
        xmb = xm.astype(BF16)
        q_ref[0, :, cs] = _dot(xc, wq_ref[i]).astype(BF16)
        kt_ref[0, cs, :] = _dot_nt(wkt_ref[i], xc).astype(BF16)
        v_ref[0, :, cs] = _dot(xmb, wv_ref[i]).astype(BF16)
        xc_ref[0, :, cs] = xc
        z_ref[0, :, cs] = _silu(_dot(h, wup_ref[:, inner + i * w:inner + (i + 1) * w])).astype(BF16)
        gacc = gacc + _dot(xc, gc_ref[cs, :]) + _dot(xmb, gm_ref[cs, :])
    gates_ref[0] = gacc
    nbuf_ref[0] = carry[pad - (width - 1):pad, :]


def _mpre(x, cbuf, g, wup, cw, cb, wq_t, wkt_t, wv_t, gc, gm, bg, tb):
    b, s, d = x.shape
    inner = cw.shape[1]
    width = cw.shape[0]
    ng = bg.shape[1]
    kern = functools.partial(_mpre_kernel, tb=tb, inner=inner, width=width)
    act = pl.BlockSpec((1, tb, inner), lambda i, t: (i, t, 0))
    blocks = _nbytes((tb, d), F32) + 5 * _nbytes((tb, inner), BF16)
    scratch = (_nbytes(wup.shape, BF16) + 3 * _nbytes(wq_t.shape, BF16) + 2 * _nbytes((inner, 128), BF16)
               + _nbytes((8, inner), F32) + 8 * _nbytes((tb, V7X_MXU), F32))
    return pl.pallas_call(
        kern,
        grid=(b, s // tb),
        in_specs=[pl.BlockSpec((1, tb, d), lambda i, t: (i, t, 0)),
                  pl.BlockSpec((1, width - 1, inner), lambda i, t: (i, 0, 0)),
                  _const_spec((1, d)), _const_spec(wup.shape), _const_spec(cw.shape), _const_spec((1, inner)),
                  _const_spec(wq_t.shape), _const_spec(wkt_t.shape), _const_spec(wv_t.shape),
                  _const_spec(gc.shape), _const_spec(gm.shape), _const_spec((1, ng))],
        out_specs=[act, pl.BlockSpec((1, inner, tb), lambda i, t: (i, 0, t)), act, act, act,
                   pl.BlockSpec((1, tb, ng), lambda i, t: (i, t, 0)),
                   pl.BlockSpec((1, width - 1, inner), lambda i, t: (i, 0, 0))],
        out_shape=[jax.ShapeDtypeStruct((b, s, inner), BF16), jax.ShapeDtypeStruct((b, inner, s), BF16),
                   jax.ShapeDtypeStruct((b, s, inner), BF16), jax.ShapeDtypeStruct((b, s, inner), BF16),
                   jax.ShapeDtypeStruct((b, s, inner), BF16), jax.ShapeDtypeStruct((b, s, ng), F32),
                   jax.ShapeDtypeStruct((b, width - 1, inner), F32)],
        scratch_shapes=[pltpu.VMEM((V7X_SUBLANES, inner), F32)],
        compiler_params=pltpu.CompilerParams(
            dimension_semantics=("arbitrary", "arbitrary"), vmem_limit_bytes=_vmem_limit(blocks, scratch)),
        name="mlstm_pre",
    )(x, cbuf, g, wup, cw, cb, wq_t, wkt_t, wv_t, gc, gm, bg)


def _mlstm_chunk(q_ref, kt_ref, v_ref, g, gt, hn_ref, ct_s, n_s, m_s, fill, *, lc, heads, dh):
    row = lax.broadcasted_iota(jnp.int32, (lc, lc), 0)
    col = lax.broadcasted_iota(jnp.int32, (lc, lc), 1)
    causal = col <= row
    tri_l = jnp.where(causal, 1.0, 0.0).astype(BF16)
    tri_u = jnp.where(row <= col, 1.0, 0.0).astype(BF16)
    b_cols = sum(_dot(tri_l, part) for part in _split3(_log_sigmoid(g)))
    b_rows = sum(_dot(part, tri_u) for part in _split3(_log_sigmoid(gt)))

    hsl = [slice(h * dh, (h + 1) * dh) for h in range(heads)]
    st = []
    for h in range(heads):
        b_col = b_cols[:, heads + h:heads + h + 1]
        b_row = b_rows[heads + h:heads + h + 1, :]
        i_row = gt[h:h + 1, :]
        m_prev = m_s[h:h + 1, 0:1]
        dmat = jnp.where(causal, b_col - b_row + i_row, -jnp.inf)
        inter = b_col + m_prev
        m_t = jnp.maximum(inter, jnp.max(dmat, axis=1, keepdims=True))
        st.append(dict(b_col=b_col, b_row=b_row, i_row=i_row, m_prev=m_prev, m_t=m_t,
                       wts=jnp.exp(dmat - m_t), a=jnp.exp(inter - m_t)))
        fill()
    for h in range(heads):
        s = st[h]
        s['sc'] = _dot(q_ref[0, :, hsl[h]], kt_ref[0, hsl[h], :]) * s['wts']
    for h in range(heads):
        s = st[h]
        qh = q_ref[0, :, hsl[h]]
        sc = s['sc']
        a = s['a']
        num = _dot(sc.astype(BF16), v_ref[0, :, hsl[h]]) + a * _dot(qh, ct_s[h].astype(BF16))
        qn = jnp.sum(qh.astype(F32) * n_s[h:h + 1, :], axis=1, keepdims=True)
        den = jnp.sum(sc, axis=1, keepdims=True) + a * qn
        hh = num / jnp.maximum(jnp.abs(den), jnp.exp(-s['m_t']))
        mu = jnp.mean(hh, axis=1, keepdims=True)
        dev = hh - mu
        var = jnp.mean(dev * dev, axis=1, keepdims=True)
        hn_ref[0, :, hsl[h]] = (dev * lax.rsqrt(var + LN_EPS)).astype(hn_ref.dtype)
        fill()
    for h in range(heads):
        s = st[h]
        kth = kt_ref[0, hsl[h], :]
        b_last = s['b_row'][:, lc - 1:lc]
        g_row = b_last - s['b_row'] + s['i_row']
        m_new = jnp.maximum(b_last + s['m_prev'], jnp.max(g_row, axis=1, keepdims=True))
        decay = jnp.exp(b_last + s['m_prev'] - m_new)
        wg_row = jnp.exp(g_row - m_new)
        wg_col = jnp.exp(b_last - s['b_col'] + g[:, h:h + 1] - m_new)
        wv = (v_ref[0, :, hsl[h]].astype(F32) * wg_col).astype(BF16)
        ct_s[h] = decay * ct_s[h] + _dot(kth, wv)
        wg16 = jnp.broadcast_to(wg_row, (16, lc)).astype(BF16)
        n_s[h:h + 1, :] = decay * n_s[h:h + 1, :] + _dot_nt(wg16, kth)[0:1, :]
        m_s[h:h + 1, :] = jnp.broadcast_to(m_new, (1, V7X_LANES))


def _mlstm_layer_kernel(*refs, lc, heads, inner, width, zero_init):
    (x_ref, cbuf_ref, g0_ref, wup_ref, cw_ref, cb_ref, wq_ref, wkt_ref, wv_ref, gc_ref, gm_ref, bg_ref,
     ng_ref, sk_ref, wdown_ref, g1_ref) = refs[:16]
    n_in = 16
    if not zero_init:
        c0_ref, n0_ref, m0_ref = refs[16:19]
        n_in = 19
    o_ref, cout_ref, nout_ref, mout_ref, nbuf_ref = refs[n_in:n_in + 5]
    carry, q_s, kt_s, v_s, xc_s, sz_s, hn_s, ct_s, n_s, m_s = refs[n_in + 5:]
    dh = inner // heads
    pad = V7X_SUBLANES
    c = pl.program_id(1)

    @pl.when(c == 0)
    def _():
        carry[...] = jnp.zeros_like(carry)
        carry[pad - (width - 1):pad, :] = cbuf_ref[0]
        if zero_init:
            ct_s[...] = jnp.zeros_like(ct_s)
            n_s[...] = jnp.zeros_like(n_s)
            m_s[...] = jnp.zeros_like(m_s)
        else:
            for h in range(heads):
                ct_s[h] = c0_ref[0, h].T
            n_s[...] = n0_ref[0]
            m_s[...] = m0_ref[0]

    x = x_ref[0]
    hx = _rms(x, g0_ref[...]).astype(BF16)
    w = V7X_MXU
    gacc = jnp.zeros((lc, bg_ref.shape[1]), F32) + bg_ref[...]
    for i in range(inner // w):
        cs = slice(i * w, (i + 1) * w)
        xm = _dot(hx, wup_ref[:, cs])
        prev = carry[:, cs]
        carry[:, cs] = xm[lc - pad:lc, :]
        xc = _silu(_causal_dwconv(xm, prev, cw_ref, cb_ref, cs)).astype(BF16)
        xmb = xm.astype(BF16)
        q_s[0, :, cs] = _dot(xc, wq_ref[i]).astype(BF16)
        kt_s[0, cs, :] = _dot_nt(wkt_ref[i], xc).astype(BF16)
        v_s[0, :, cs] = _dot(xmb, wv_ref[i]).astype(BF16)
        xc_s[:, cs] = xc
        gacc = gacc + _dot(xc, gc_ref[cs, :]) + _dot(xmb, gm_ref[cs, :])
    nbuf_ref[0] = carry[pad - (width - 1):pad, :]

    slabs = iter(range(inner // w))

    def gate_slab():
        i = next(slabs, None)
        if i is not None:
            cs = slice(i * w, (i + 1) * w)
            sz_s[:, cs] = _silu(_dot(hx, wup_ref[:, inner + i * w:inner + (i + 1) * w])).astype(BF16)

    _mlstm_chunk(q_s, kt_s, v_s, gacc, gacc.T[0:2 * heads, :], hn_s, ct_s, n_s, m_s, gate_slab,
                 lc=lc, heads=heads, dh=dh)
    for _ in slabs:
        raise AssertionError("fewer fill points than output-gate slabs")

    hs = hn_s[0].astype(F32) * ng_ref[...] + sk_ref[...] * xc_s[...].astype(F32)
    act = (hs * sz_s[...].astype(F32)).astype(BF16)
    o_ref[0] = x + _rms(_dot(act, wdown_ref[...]), g1_ref[...])

    @pl.when(c == pl.num_programs(1) - 1)
    def _():
        for h in range(heads):
            cout_ref[0, h] = ct_s[h].T
        nout_ref[0] = n_s[...]
        mout_ref[0] = m_s[...]


def _mlstm_layer(x, cbuf, state, p, g0, g1, lc):
    b, s, d = x.shape
    heads = p['heads']
    inner = p['m_cw'].shape[1]
    width = p['m_cw'].shape[0]
    dh = inner // heads
    zero_init = state is None
    kern = functools.partial(_mlstm_layer_kernel, lc=lc, heads=heads, inner=inner, width=width,
                             zero_init=zero_init)
    xs = pl.BlockSpec((1, lc, d), lambda i, c: (i, c, 0))
    bufs = pl.BlockSpec((1, width - 1, inner), lambda i, c: (i, 0, 0))
    cs = pl.BlockSpec((1, heads, dh, dh), lambda i, c: (i, 0, 0, 0))
    ns = pl.BlockSpec((1, heads, dh), lambda i, c: (i, 0, 0))
    ms = pl.BlockSpec((1, heads, V7X_LANES), lambda i, c: (i, 0, 0))
    consts = [g0, p['m_wup'], p['m_cw'], p['m_cb'], p['m_wq'], p['m_wkt'], p['m_wv'], p['m_gc'], p['m_gm'],
              p['m_bg'], p['m_ng'], p['m_skip'], p['m_wdown'], g1]
    in_specs = [xs, bufs] + [_const_spec(a.shape) for a in consts]
    args = [x, cbuf] + consts
    if not zero_init:
        in_specs += [cs, ns, ms]
        args += list(state)
    blocks = 2 * _nbytes((lc, d), F32) + (1 if zero_init else 2) * _nbytes((heads, dh, dh), F32)
    scratch = (sum(_nbytes(a.shape, a.dtype) for a in consts) + 6 * _nbytes((lc, inner), BF16)
               + _nbytes((heads, dh, dh), F32) + 8 * _nbytes((lc, max(lc, dh)), F32) + 2 * _nbytes((dh, dh), F32))
    return pl.pallas_call(
        kern,
        grid=(b, s // lc),
        in_specs=in_specs,
        out_specs=[xs, cs, ns, ms, bufs],
        out_shape=[jax.ShapeDtypeStruct((b, s, d), F32), jax.ShapeDtypeStruct((b, heads, dh, dh), F32),
                   jax.ShapeDtypeStruct((b, heads, dh), F32), jax.ShapeDtypeStruct((b, heads, V7X_LANES), F32),
                   jax.ShapeDtypeStruct((b, width - 1, inner), F32)],
        scratch_shapes=[pltpu.VMEM((V7X_SUBLANES, inner), F32),
                        pltpu.VMEM((1, lc, inner), BF16), pltpu.VMEM((1, inner, lc), BF16),
                        pltpu.VMEM((1, lc, inner), BF16), pltpu.VMEM((lc, inner), BF16),
                        pltpu.VMEM((lc, inner), BF16), pltpu.VMEM((1, lc, inner), BF16),
                        pltpu.VMEM((heads, dh, dh), F32), pltpu.VMEM((heads, dh), F32),
                        pltpu.VMEM((heads, V7X_LANES), F32)],
        compiler_params=pltpu.CompilerParams(
            dimension_semantics=("arbitrary", "arbitrary"), vmem_limit_bytes=_vmem_limit(blocks, scratch)),
        name="mlstm_layer",
    )(*args)


def _block_diag_tiles(w):
    nblk, k, _ = w.shape
    per = V7X_MXU // k
    wt = w.reshape(nblk // per, per, k, k)
    eye = jnp.eye(per, dtype=w.dtype)
    full = wt[:, :, :, None, :] * eye[None, :, None, :, None]
    return full.reshape(nblk // per, per * k, per * k)


def _prep_weights(w):
    row = lambda v: v.astype(F32).reshape(1, -1)
    p = {}
    p['norm_g'] = w['norm_g'].astype(F32)
    p['s5_mats'] = _s5_matrices(w['s5_A_re'][0], w['s5_A_im'][0], w['s5_log_dt'][0], w['s5_B_re'][0],
                                w['s5_B_im'][0], w['s5_C_re'][0], w['s5_C_im'][0])
    p['s5_D'] = row(w['s5_D'][0])
    p['s5_w_glu'] = w['s5_w_glu'][0].astype(BF16)
    inner = w['mlstm_conv_w'].shape[2]
    heads = w['mlstm_b_gate'].shape[1] // 2
    dh = inner // heads
    wq_t = _block_diag_tiles(w['mlstm_wq'][0].astype(F32))
    wk_t = _block_diag_tiles(w['mlstm_wk'][0].astype(F32))
    wv_t = _block_diag_tiles(w['mlstm_wv'][0].astype(F32))
    gc, gm = _gate_fold(wq_t, wk_t, wv_t, w['mlstm_w_gate'][0].astype(F32))
    p['m_wq'] = wq_t.astype(BF16)
    p['m_wkt'] = (jnp.swapaxes(wk_t, 1, 2) * (dh ** -0.5)).astype(BF16)
    p['m_wv'] = wv_t.astype(BF16)
    lane_pad = lambda a: jnp.pad(a, ((0, 0), (0, V7X_LANES - a.shape[1])))
    p['m_gc'] = lane_pad(gc).astype(BF16)
    p['m_gm'] = lane_pad(gm).astype(BF16)
    p['m_bg'] = lane_pad(row(w['mlstm_b_gate'][0]))
    p['m_wup'] = w['mlstm_w_up'][0].astype(BF16)
    p['m_cw'] = w['mlstm_conv_w'][0].astype(F32)
    p['m_cb'] = row(w['mlstm_conv_b'][0])
    p['m_ng'] = row(w['mlstm_norm_g'][0])
    p['m_skip'] = row(w['mlstm_skip'][0])
    p['m_wdown'] = w['mlstm_w_down'][0].astype(BF16)
    p['heads'] = heads
    p['f_wup'] = w['ffn_w_up'].astype(BF16)
    p['f_cw'] = w['ffn_conv_w'].astype(F32)
    p['f_cb'] = w['ffn_conv_b'].astype(F32)
    p['f_wdown'] = w['ffn_w_down'].astype(BF16)
    return p


def _to_chunk_major(u2d, b, s):
    d = u2d.shape[1]
    g = d // S5_GROUP_CH
    u = u2d.reshape(b, s // S5_CHUNK, S5_CHUNK, g, S5_GROUP_CH)
    return jnp.transpose(u, (1, 0, 3, 2, 4)).reshape((s // S5_CHUNK) * b, d * S5_CHUNK)


def _to_token_major(yt, b, s):
    d = yt.shape[1] // S5_CHUNK
    g = d // S5_GROUP_CH
    y = yt.reshape(s // S5_CHUNK, b, g, S5_CHUNK, S5_GROUP_CH)
    return jnp.transpose(y, (1, 0, 3, 2, 4)).reshape(b * s, d)


def _trunk(x, st, p, tiles):
    b, s, d = x.shape
    ng = p['norm_g']
    g_of = lambda layer, k: ng[layer, k].reshape(1, d)
    zero_init = st is None
    groups = d // S5_GROUP_CH
    x2d = x.reshape(b * s, d)

    n_state = p['s5_mats']['row'][1].shape[2] // 2
    if zero_init:
        h0re = h0im = jnp.zeros((groups // 2, b, 2 * n_state), F32)
    else:
        pair = lambda h: jnp.transpose(h.astype(F32).reshape(b, groups // 2, 2 * n_state), (1, 0, 2))
        h0re, h0im = pair(st['s5_re']), pair(st['s5_im'])
    if tiles.s5_time:
        y, hre, him = _s5_fused(x, g_of(0, 0), p['s5_mats']['col'], h0re, h0im, tiles.s5_time)
        y2d = y.reshape(b * s, d)
    else:
        u = _norm_cast(x2d, g_of(0, 0), tiles.rows)
        yt, hre, him = _s5_core(_to_chunk_major(u, b, s), p['s5_mats']['row'], h0re, h0im, b, tiles.s5_rows)
        y2d = _to_token_major(yt, b, s)
    unpair = lambda h: jnp.transpose(h, (1, 0, 2)).reshape(1, b, groups, n_state)
    o_re, o_im = unpair(hre), unpair(him)
    x2d = _glu(x2d, y2d, g_of(0, 0), p['s5_D'], p['s5_w_glu'], g_of(0, 1), tiles.rows)

    def ffn(x2d, layer):
        dff2 = p['f_wup'].shape[2]
        width = p['f_cw'].shape[1]
        buf = jnp.zeros((b, width - 1, dff2), F32) if zero_init else st['ffn_conv'][layer].astype(F32)
        xo, nbuf = _ffn(x2d.reshape(b, s, d), buf, g_of(layer, 2), p['f_wup'][layer], p['f_cw'][layer],
                        p['f_cb'][layer].reshape(1, dff2), p['f_wdown'][layer], g_of(layer, 3), tiles.ffn_time)
        return xo.reshape(b * s, d), nbuf

    x2d, fbuf0 = ffn(x2d, 0)

    heads = p['heads']
    inner = p['m_cw'].shape[1]
    width = p['m_cw'].shape[0]
    cbuf = jnp.zeros((b, width - 1, inner), F32) if zero_init else st['mlstm_conv'].astype(F32)
    if zero_init:
        state = None
    else:
        state = (st['mlstm_C'].astype(F32), st['mlstm_n'].astype(F32),
                 jnp.broadcast_to(st['mlstm_m'].astype(F32)[:, :, None], (b, heads, V7X_LANES)))
    x3, c_out, n_out, m_out, ncbuf = _mlstm_layer(x2d.reshape(b, s, d), cbuf, state, p, g_of(1, 0), g_of(1, 1),
                                                  tiles.mlstm_chunk)
    x2d, fbuf1 = ffn(x3.reshape(b * s, d), 1)

    return (x2d.reshape(b, s, d), o_re, o_im, c_out[None], n_out[None], m_out[None, :, :, 0], ncbuf[None],
            jnp.stack([fbuf0, fbuf1], axis=0))


def kernel(x_prompt, x_sample, state_s5_re, state_s5_im, state_mlstm_C, state_mlstm_n, state_mlstm_m, state_mlstm_conv, state_ffn_conv, norm_g, s5_A_re, s5_A_im, s5_log_dt, s5_B_re, s5_B_im, s5_C_re, s5_C_im, s5_D, s5_w_glu, mlstm_w_up, mlstm_conv_w, mlstm_conv_b, mlstm_wq, mlstm_wk, mlstm_wv, mlstm_w_gate, mlstm_b_gate, mlstm_norm_g, mlstm_skip, mlstm_w_down, ffn_w_up, ffn_conv_w, ffn_conv_b, ffn_w_down):
    w = {
        'norm_g': norm_g, 's5_A_re': s5_A_re, 's5_A_im': s5_A_im, 's5_log_dt': s5_log_dt,
        's5_B_re': s5_B_re, 's5_B_im': s5_B_im, 's5_C_re': s5_C_re, 's5_C_im': s5_C_im, 's5_D': s5_D,
        's5_w_glu': s5_w_glu, 'mlstm_w_up': mlstm_w_up, 'mlstm_conv_w': mlstm_conv_w,
        'mlstm_conv_b': mlstm_conv_b, 'mlstm_wq': mlstm_wq, 'mlstm_wk': mlstm_wk, 'mlstm_wv': mlstm_wv,
        'mlstm_w_gate': mlstm_w_gate, 'mlstm_b_gate': mlstm_b_gate, 'mlstm_norm_g': mlstm_norm_g,
        'mlstm_skip': mlstm_skip, 'mlstm_w_down': mlstm_w_down, 'ffn_w_up': ffn_w_up,
        'ffn_conv_w': ffn_conv_w, 'ffn_conv_b': ffn_conv_b, 'ffn_w_down': ffn_w_down,
    }
    assert norm_g.shape[0] == 2 and s5_A_re.shape[0] == 1 and mlstm_w_up.shape[0] == 1
    assert x_prompt.shape[0] == V7X_SUBLANES and x_sample.shape[0] == V7X_SUBLANES
    p = _prep_weights(w)
    out_p = _trunk(x_prompt.astype(F32), None, p, _pick_tiles(*x_prompt.shape[:2]))
    st = {'s5_re': state_s5_re[0], 's5_im': state_s5_im[0], 'mlstm_C': state_mlstm_C[0],
          'mlstm_n': state_mlstm_n[0], 'mlstm_m': state_mlstm_m[0], 'mlstm_conv': state_mlstm_conv[0],
          'ffn_conv': state_ffn_conv}
    out_s = _trunk(x_sample.astype(F32), st, p, _pick_tiles(*x_sample.shape[:2]))
    return (out_p[0], out_s[0]) + tuple(out_p[1:]) + tuple(out_s[1:])
```

```python
import functools
import math
from typing import NamedTuple

import jax
import jax.numpy as jnp
from jax import lax
from jax.experimental import pallas as pl
from jax.experimental.pallas import tpu as pltpu

F32 = jnp.float32
BF16 = jnp.bfloat16
HI = lax.Precision.HIGHEST

NORM_EPS = 1e-6
LN_EPS = 1e-5
S5_GROUP_CH = 16
S5_CHUNK = 16
QKV_BLOCK = 4

V7X_LANES = 128
V7X_SUBLANES = 8
V7X_MXU = 256
V7X_VMEM_BYTES = 64 * 1024 * 1024


class Tiles(NamedTuple):
    rows: int
    time: int
    s5_rows: int
    mlstm_chunk: int
    s5_time: int
    ffn_time: int


def _pick_tiles(batch, seq):
    s5_time = (V7X_LANES // batch) * S5_CHUNK
    return Tiles(rows=min(batch * seq, 512), time=min(seq, 512),
                 s5_rows=min(batch * (seq // S5_CHUNK), 512), mlstm_chunk=min(seq, 256),
                 s5_time=s5_time if seq % s5_time == 0 else 0, ffn_time=min(seq, 1024))


def _vmem_limit(block_bytes, scratch_bytes):
    want = 2 * block_bytes + scratch_bytes + 16 * 1024 * 1024
    return int(min(want, V7X_VMEM_BYTES - 8 * 1024 * 1024))


def _nbytes(shape, dtype):
    return math.prod(shape) * jnp.dtype(dtype).itemsize


def _rms(x, g):
    return x * lax.rsqrt(jnp.mean(x * x, axis=-1, keepdims=True) + NORM_EPS) * g


def _dot(a, b):
    return jnp.dot(a, b, preferred_element_type=F32)


def _dot_nt(a, b):
    return lax.dot_general(a, b, (((1,), (1,)), ((), ())), preferred_element_type=F32)


def _silu(x):
    return (0.5 * x) * (1.0 + jnp.tanh(0.5 * x))


def _causal_dwconv(u, prev, cw_ref, cb_ref, cs):
    width = cw_ref.shape[0]
    acc = cb_ref[:, cs] + u * cw_ref[width - 1:width, cs]
    rows = lax.broadcasted_iota(jnp.int32, prev.shape, 0)
    for j in range(width - 1):
        d = width - 1 - j
        rolled = pltpu.roll(u, d, 0)
        head = jnp.where(rows < d, pltpu.roll(prev, d, 0), rolled[0:V7X_SUBLANES])
        shifted = jnp.concatenate([head, rolled[V7X_SUBLANES:]], axis=0)
        acc = acc + shifted * cw_ref[j:j + 1, cs]
    return acc


def _log_sigmoid(x):
    return jnp.minimum(x, 0.0) - jnp.log1p(jnp.exp(-jnp.abs(x)))


def _split3(x):
    hi = x.astype(BF16)
    r1 = x - hi.astype(F32)
    mid = r1.astype(BF16)
    lo = (r1 - mid.astype(F32)).astype(BF16)
    return hi, mid, lo


def _norm_cast_kernel(x_ref, g_ref, o_ref):
    o_ref[...] = _rms(x_ref[...], g_ref[...]).astype(o_ref.dtype)


def _norm_cast(x2d, g, tb):
    rows, d = x2d.shape
    return pl.pallas_call(
        _norm_cast_kernel,
        grid=(rows // tb,),
        in_specs=[pl.BlockSpec((tb, d), lambda i: (i, 0)), pl.BlockSpec((1, d), lambda i: (0, 0))],
        out_specs=pl.BlockSpec((tb, d), lambda i: (i, 0)),
        out_shape=jax.ShapeDtypeStruct((rows, d), BF16),
        name="s5_norm",
    )(x2d, g)


def _s5_kernel(u_ref, t_ref, pre_ref, pim_ref, qre_ref, qim_ref, are_ref, aim_ref, h0re_ref, h0im_ref,
               y_ref, hre_ref, him_ref, pure_s, puim_s, hinre_s, hinim_s, stre_s, stim_s, *, nb, rb):
    r = pl.program_id(1)

    @pl.when(r == 0)
    def _():
        stre_s[...] = h0re_ref[0]
        stim_s[...] = h0im_ref[0]

    u = u_ref[...]
    pure_s[...] = _dot(u, pre_ref[0])
    puim_s[...] = _dot(u, pim_ref[0])
    ar = jnp.broadcast_to(are_ref[0], (nb, 2 * 64))
    ai = jnp.broadcast_to(aim_ref[0], (nb, 2 * 64))

    def step(i, carry):
        re, im = carry
        rows = pl.ds(pl.multiple_of(i * nb, nb), nb)
        hinre_s[rows, :] = re
        hinim_s[rows, :] = im
        return (ar * re - ai * im + pure_s[rows, :], ar * im + ai * re + puim_s[rows, :])

    re, im = lax.fori_loop(0, rb // nb, step, (stre_s[...], stim_s[...]))
    stre_s[...] = re
    stim_s[...] = im
    hre_ref[0] = re
    him_ref[0] = im

    carry_in = _dot(hinre_s[...].astype(BF16), qre_ref[0]) + _dot(hinim_s[...].astype(BF16), qim_ref[0])
    w = V7X_MXU
    for k in range(2):
        y_ref[:, k * w:(k + 1) * w] = (_dot(u[:, k * w:(k + 1) * w], t_ref[k])
                                       + carry_in[:, k * w:(k + 1) * w]).astype(y_ref.dtype)


def _s5_core(ut, mats, h0re, h0im, nb, rb):
    rows, cols = ut.shape
    pairs = cols // (2 * V7X_MXU)
    tm, pre, pim, qre, qim, are, aim = mats
    pair3 = lambda p, r: (p, 0, 0)
    kern = functools.partial(_s5_kernel, nb=nb, rb=rb)
    blocks = (_nbytes((rb, 512), BF16) * 2 + _nbytes((2, 256, 256), BF16) + 4 * _nbytes((512, 128), BF16))
    scratch = 4 * _nbytes((rb, 128), F32) + 2 * _nbytes((nb, 128), F32)
    return pl.pallas_call(
        kern,
        grid=(pairs, rows // rb),
        in_specs=[
            pl.BlockSpec((rb, 2 * V7X_MXU), lambda p, r: (r, p)),
            pl.BlockSpec((2, V7X_MXU, V7X_MXU), pair3),
            pl.BlockSpec((1, 2 * V7X_MXU, 128), pair3),
            pl.BlockSpec((1, 2 * V7X_MXU, 128), pair3),
            pl.BlockSpec((1, 128, 2 * V7X_MXU), pair3),
            pl.BlockSpec((1, 128, 2 * V7X_MXU), pair3),
            pl.BlockSpec((1, 1, 128), pair3),
            pl.BlockSpec((1, 1, 128), pair3),
            pl.BlockSpec((1, nb, 128), pair3),
            pl.BlockSpec((1, nb, 128), pair3),
        ],
        out_specs=[
            pl.BlockSpec((rb, 2 * V7X_MXU), lambda p, r: (r, p)),
            pl.BlockSpec((1, nb, 128), pair3),
            pl.BlockSpec((1, nb, 128), pair3),
        ],
        out_shape=[
            jax.ShapeDtypeStruct((rows, cols), BF16),
            jax.ShapeDtypeStruct((pairs, nb, 128), F32),
            jax.ShapeDtypeStruct((pairs, nb, 128), F32),
        ],
        scratch_shapes=[pltpu.VMEM((rb, 128), F32)] * 4 + [pltpu.VMEM((nb, 128), F32)] * 2,
        compiler_params=pltpu.CompilerParams(
            dimension_semantics=("arbitrary", "arbitrary"), vmem_limit_bytes=_vmem_limit(blocks, scratch)),
        name="s5_core",
    )(ut, tm, pre, pim, qre, qim, are, aim, h0re, h0im)


def _s5_fused_kernel(x_ref, g_ref, tt_ref, pt_ref, qt_ref, are_ref, aim_ref, h0re_ref, h0im_ref,
                     y_ref, hre_ref, him_ref,
                     us, ugt, pure, puim, hinre, hinim, ytmp, ys, stre, stim, *, nb, tbt):
    lc = S5_CHUNK
    ch = S5_GROUP_CH
    gps = V7X_LANES // ch
    nch = tbt // lc
    cols = nb * nch
    k = pl.program_id(1)

    @pl.when(k == 0)
    def _():
        for b in range(nb):
            u = _rms(x_ref[b], g_ref[...])
            for kk in range(us.shape[0]):
                us[kk, b * tbt:(b + 1) * tbt, :] = u[:, kk * V7X_LANES:(kk + 1) * V7X_LANES]

    @pl.when(pl.program_id(0) == 0)
    def _():
        stre[k] = h0re_ref[...]
        stim[k] = h0im_ref[...]

    for s in range(lc):
        a = us[k, pl.ds(s, cols, stride=lc), :].T
        for g in range(gps):
            ugt[g, s * ch:(s + 1) * ch, :] = a[g * ch:(g + 1) * ch, :].astype(BF16)

    half = pt_ref.shape[1] // 2
    for pr in range(gps // 2):
        put0 = _dot(pt_ref[2 * pr], ugt[2 * pr])
        put1 = _dot(pt_ref[2 * pr + 1], ugt[2 * pr + 1])
        pure[pr] = jnp.concatenate([put0[:half], put1[:half]], axis=0).T
        puim[pr] = jnp.concatenate([put0[half:], put1[half:]], axis=0).T

    for pr in range(gps // 2):
        re = stre[k, pr]
        im = stim[k, pr]
        ar = jnp.broadcast_to(are_ref[pr], re.shape)
        ai = jnp.broadcast_to(aim_ref[pr], re.shape)
        for j in range(nch):
            rows = pl.ds(j, nb, stride=nch)
            hinre[pr, rows, :] = re
            hinim[pr, rows, :] = im
            re, im = (ar * re - ai * im + pure[pr, rows, :], ar * im + ai * re + puim[pr, rows, :])
        stre[k, pr] = re
        stim[k, pr] = im
        hre_ref[k * (gps // 2) + pr] = re
        him_ref[k * (gps // 2) + pr] = im

    for pr in range(gps // 2):
        hre_t = hinre[pr].T
        him_t = hinim[pr].T
        for i in range(2):
            g = 2 * pr + i
            hin_t = jnp.concatenate([hre_t[i * half:(i + 1) * half], him_t[i * half:(i + 1) * half]],
                                    axis=0).astype(BF16)
            ytmp[g] = _dot(tt_ref[g], ugt[g]) + _dot(qt_ref[g], hin_t)

    for t in range(lc):
        z = jnp.concatenate([ytmp[g, t * ch:(t + 1) * ch, :] for g in range(gps)], axis=0)
        ys[pl.ds(t, cols, stride=lc), :] = z.T
    for b in range(nb):
        y_ref[b] = ys[b * tbt:(b + 1) * tbt, :].astype(y_ref.dtype)


def _s5_fused(x, g, mats_t, h0re, h0im, tbt):
    nb, s, d = x.shape
    tt, pt, qt, are, aim = mats_t
    groups = tt.shape[0]
    gps = V7X_LANES // S5_GROUP_CH
    slabs = groups // gps
    w = S5_CHUNK * S5_GROUP_CH
    nst = pt.shape[1]
    cols = nb * (tbt // S5_CHUNK)
    kern = functools.partial(_s5_fused_kernel, nb=nb, tbt=tbt)
    slab3 = lambda t, k: (k, 0, 0)
    blocks = (_nbytes((nb, tbt, d), F32) + _nbytes((nb, tbt, V7X_LANES), BF16)
              + gps * (_nbytes((w, w), BF16) + 2 * _nbytes((w, nst), BF16)))
    scratch = (_nbytes((slabs, nb * tbt, V7X_LANES), F32) + _nbytes((gps, w, cols), BF16)
               + 4 * _nbytes((gps // 2, cols, nst), F32) + _nbytes((gps, w, cols), F32)
               + _nbytes((nb * tbt, V7X_LANES), F32))
    return pl.pallas_call(
        kern,
        grid=(s // tbt, slabs),
        in_specs=[
            pl.BlockSpec((nb, tbt, d), lambda t, k: (0, t, 0)),
            pl.BlockSpec((1, d), lambda t, k: (0, 0)),
            pl.BlockSpec((gps, w, w), slab3),
            pl.BlockSpec((gps, nst, w), slab3),
            pl.BlockSpec((gps, w, nst), slab3),
            pl.BlockSpec((gps // 2, 1, nst), slab3),
            pl.BlockSpec((gps // 2, 1, nst), slab3),
            pl.BlockSpec((gps // 2, nb, nst), slab3),
            pl.BlockSpec((gps // 2, nb, nst), slab3),
        ],
        out_specs=[
            pl.BlockSpec((nb, tbt, V7X_LANES), lambda t, k: (0, t, k)),
            pl.BlockSpec((groups // 2, nb, nst), lambda t, k: (0, 0, 0)),
            pl.BlockSpec((groups // 2, nb, nst), lambda t, k: (0, 0, 0)),
        ],
        out_shape=[
            jax.ShapeDtypeStruct((nb, s, d), BF16),
            jax.ShapeDtypeStruct((groups // 2, nb, nst), F32),
            jax.ShapeDtypeStruct((groups // 2, nb, nst), F32),
        ],
        scratch_shapes=[
            pltpu.VMEM((slabs, nb * tbt, V7X_LANES), F32),
            pltpu.VMEM((gps, w, cols), BF16),
            pltpu.VMEM((gps // 2, cols, nst), F32),
            pltpu.VMEM((gps // 2, cols, nst), F32),
            pltpu.VMEM((gps // 2, cols, nst), F32),
            pltpu.VMEM((gps // 2, cols, nst), F32),
            pltpu.VMEM((gps, w, cols), F32),
            pltpu.VMEM((nb * tbt, V7X_LANES), F32),
            pltpu.VMEM((slabs, gps // 2, nb, nst), F32),
            pltpu.VMEM((slabs, gps // 2, nb, nst), F32),
        ],
        compiler_params=pltpu.CompilerParams(
            dimension_semantics=("arbitrary", "arbitrary"), vmem_limit_bytes=_vmem_limit(blocks, scratch)),
        name="s5_fused",
    )(x, g, tt, pt, qt, are, aim, h0re, h0im)


def _s5_matrices(a_re, a_im, log_dt, b_re, b_im, c_re, c_im):
    lc = S5_CHUNK
    groups, n = a_re.shape
    ch = S5_GROUP_CH
    a = lax.complex(a_re.astype(F32), a_im.astype(F32))
    adt = a * jnp.exp(log_dt.astype(F32))[:, None]
    j = jnp.arange(lc + 1, dtype=F32)
    apow = jnp.exp(adt[None] * j[:, None, None])
    b_bar = ((apow[1] - 1.0) / a)[..., None] * lax.complex(b_re.astype(F32), b_im.astype(F32))
    c_mat = lax.complex(c_re.astype(F32), c_im.astype(F32))
    ca = c_mat[None] * apow[:, :, None, :]
    kern = (jnp.einsum('jgcn,gnd->jgcd', ca.real, b_bar.real, precision=HI)
            - jnp.einsum('jgcn,gnd->jgcd', ca.imag, b_bar.imag, precision=HI))
    s_idx = jnp.arange(lc)[:, None]
    t_idx = jnp.arange(lc)[None, :]
    lag = t_idx - s_idx
    kt = kern[jnp.clip(lag, 0, lc)]
    kt = jnp.where((lag >= 0)[:, :, None, None, None], kt, 0.0)
    tm = jnp.transpose(kt, (2, 0, 4, 1, 3)).reshape(groups, lc * ch, lc * ch)
    pw = apow[lc - 1 - jnp.arange(lc)]
    p = pw[:, :, :, None] * b_bar[None]
    p = jnp.transpose(p, (1, 0, 3, 2)).reshape(groups, lc * ch, n)
    q = ca[1:]
    q = jnp.transpose(q, (1, 3, 0, 2)).reshape(groups, n, lc * ch)

    def pair_rows(m):
        g2 = groups // 2
        m = m.reshape(g2, 2, m.shape[1], m.shape[2])
        z = jnp.zeros_like(m[:, 0])
        top = jnp.concatenate([m[:, 0], z], axis=2)
        bot = jnp.concatenate([z, m[:, 1]], axis=2)
        return jnp.concatenate([top, bot], axis=1)

    pre = pair_rows(p.real).astype(BF16)
    pim = pair_rows(p.imag).astype(BF16)
    qre = pair_rows(q.real).astype(BF16)
    qim = pair_rows(-q.imag).astype(BF16)
    a_chunk = apow[lc].reshape(groups // 2, 1, 2 * n)
    are, aim = a_chunk.real, a_chunk.imag
    tt = jnp.swapaxes(tm, 1, 2).astype(BF16)
    pt = jnp.concatenate([jnp.swapaxes(p.real, 1, 2), jnp.swapaxes(p.imag, 1, 2)], axis=1).astype(BF16)
    qt = jnp.concatenate([jnp.swapaxes(q.real, 1, 2), jnp.swapaxes(-q.imag, 1, 2)], axis=2).astype(BF16)
    return {'row': (tm.astype(BF16), pre, pim, qre, qim, are, aim), 'col': (tt, pt, qt, are, aim)}


def _glu_kernel(x_ref, y_ref, g0_ref, d_ref, w_ref, g1_ref, o_ref):
    x = x_ref[...]
    d = x.shape[1]
    u = _rms(x, g0_ref[...])
    yy = y_ref[...].astype(F32) + d_ref[...] * u
    z = jax.nn.gelu(yy, approximate=True).astype(BF16)
    ag = _dot(z, w_ref[...])
    out = ag[:, :d] * jax.nn.sigmoid(ag[:, d:])
    o_ref[...] = x + _rms(out, g1_ref[...])


def _const_spec(shape):
    nd = len(shape)
    return pl.BlockSpec(shape, lambda *_: (0,) * nd, pipeline_mode=pl.Buffered(1))


def _glu(x2d, y2d, g0, dskip, w, g1, tb):
    rows, d = x2d.shape
    blocks = 2 * _nbytes((tb, d), F32) + _nbytes((tb, d), BF16)
    scratch = _nbytes(w.shape, BF16) + 3 * _nbytes((tb, 2 * d), F32)
    row = pl.BlockSpec((tb, d), lambda i: (i, 0))
    return pl.pallas_call(
        _glu_kernel,
        grid=(rows // tb,),
        in_specs=[row, row, _const_spec((1, d)), _const_spec((1, d)), _const_spec(w.shape), _const_spec((1, d))],
        out_specs=row,
        out_shape=jax.ShapeDtypeStruct((rows, d), F32),
        compiler_params=pltpu.CompilerParams(
            dimension_semantics=("arbitrary",), vmem_limit_bytes=_vmem_limit(blocks, scratch)),
        name="s5_glu",
    )(x2d, y2d, g0, dskip, w, g1)


def _ffn_kernel(x_ref, buf_ref, g2_ref, wup_ref, cw_ref, cb_ref, wdown_ref, g3_ref,
                o_ref, nbuf_ref, carry, act, *, tb, dff, width):
    t = pl.program_id(1)
    pad = V7X_SUBLANES

    @pl.when(t == 0)
    def _():
        carry[...] = jnp.zeros_like(carry)
        carry[pad - (width - 1):pad, :] = buf_ref[0]

    x = x_ref[0]
    hn = _rms(x, g2_ref[...]).astype(BF16)
    fb = V7X_MXU

    def conv(cs):
        u = _dot(hn, wup_ref[:, cs])
        prev = carry[:, cs]
        carry[:, cs] = u[tb - pad:tb, :]
        return _causal_dwconv(u, prev, cw_ref, cb_ref, cs)

    for j in range(dff // fb):
        gate = conv(slice(j * fb, (j + 1) * fb))
        val = conv(slice(dff + j * fb, dff + (j + 1) * fb))
        act[:, j * fb:(j + 1) * fb] = (jax.nn.gelu(gate, approximate=True) * val).astype(BF16)

    f = _dot(act[...], wdown_ref[...])
    o_ref[0] = x + _rms(f, g3_ref[...])
    nbuf_ref[0] = carry[pad - (width - 1):pad, :]


def _ffn(x, buf, g2, wup, cw, cb, wdown, g3, tb):
    b, s, d = x.shape
    dff = wdown.shape[0]
    width = cw.shape[0]
    kern = functools.partial(_ffn_kernel, tb=tb, dff=dff, width=width)
    blocks = 2 * _nbytes((tb, d), F32)
    scratch = (_nbytes(wup.shape, BF16) + _nbytes(wdown.shape, BF16) + 6 * _nbytes((tb, V7X_MXU), F32)
               + _nbytes((8, 2 * dff), F32) + _nbytes((tb, dff), BF16) + 2 * _nbytes((tb, d), F32))
    xs = pl.BlockSpec((1, tb, d), lambda i, t: (i, t, 0))
    bs = pl.BlockSpec((1, width - 1, 2 * dff), lambda i, t: (i, 0, 0))
    return pl.pallas_call(
        kern,
        grid=(b, s // tb),
        in_specs=[xs, bs, _const_spec((1, d)), _const_spec(wup.shape), _const_spec(cw.shape),
                  _const_spec((1, 2 * dff)), _const_spec(wdown.shape), _const_spec((1, d))],
        out_specs=[xs, bs],
        out_shape=[jax.ShapeDtypeStruct((b, s, d), F32), jax.ShapeDtypeStruct((b, width - 1, 2 * dff), F32)],
        scratch_shapes=[pltpu.VMEM((V7X_SUBLANES, 2 * dff), F32), pltpu.VMEM((tb, dff), BF16)],
        compiler_params=pltpu.CompilerParams(
            dimension_semantics=("arbitrary", "arbitrary"), vmem_limit_bytes=_vmem_limit(blocks, scratch)),
        name="conv_ffn",
    )(x, buf, g2, wup, cw, cb, wdown, g3)


def _gate_fold_kernel(wq_ref, wk_ref, wv_ref, gq_ref, gk_ref, gv_ref, gc_ref, gm_ref):
    dotp = functools.partial(jnp.dot, preferred_element_type=F32, precision=HI)
    gc_ref[...] = dotp(wq_ref[0], gq_ref[...]) + dotp(wk_ref[0], gk_ref[...])
    gm_ref[...] = dotp(wv_ref[0], gv_ref[...])


def _gate_fold(wq_t, wk_t, wv_t, w_gate):
    nt, w, _ = wq_t.shape
    inner = nt * w
    ng = w_gate.shape[1]
    tile = pl.BlockSpec((1, w, w), lambda i: (i, 0, 0))
    gs = [pl.BlockSpec((w, ng), lambda i, k=k: (k * nt + i, 0)) for k in range(3)]
    out = pl.BlockSpec((w, ng), lambda i: (i, 0))
    return pl.pallas_call(
        _gate_fold_kernel,
        grid=(nt,),
        in_specs=[tile, tile, tile] + gs,
        out_specs=[out, out],
        out_shape=[jax.ShapeDtypeStruct((inner, ng), F32)] * 2,
        name="mlstm_gate_fold",
    )(wq_t, wk_t, wv_t, w_gate, w_gate, w_gate)


def _mpre_kernel(x_ref, cbuf_ref, g_ref, wup_ref, cw_ref, cb_ref, wq_ref, wkt_ref, wv_ref, gc_ref, gm_ref,
                 bg_ref, q_ref, kt_ref, v_ref, xc_ref, z_ref, gates_ref, nbuf_ref, carry,
                 *, tb, inner, width):
    t = pl.program_id(1)
    pad = V7X_SUBLANES

    @pl.when(t == 0)
    def _():
        carry[...] = jnp.zeros_like(carry)
        carry[pad - (width - 1):pad, :] = cbuf_ref[0]

    h = _rms(x_ref[0], g_ref[...]).astype(BF16)
    w = V7X_MXU
    gacc = jnp.zeros((tb, bg_ref.shape[1]), F32) + bg_ref[...]
    for i in range(inner // w):
        cs = slice(i * w, (i + 1) * w)
        xm = _dot(h, wup_ref[:, cs])
        prev = carry[:, cs]
        carry[:, cs] = xm[tb - pad:tb, :]
        xc = _silu(_causal_dwconv(xm, prev, cw_ref, cb_ref, cs)).astype(BF16)
        xmb = xm.astype(BF16)
        q_ref[0, :, cs] = _dot(xc, wq_ref[i]).astype(BF16)
        kt_ref[0, cs, :] = _dot_nt(wkt_ref[i], xc).astype(BF16)
        v_ref[0, :, cs] = _dot(xmb, wv_ref[i]).astype(BF16)
        xc_ref[0, :, cs] = xc
        z_ref[0, :, cs] = _silu(_dot(h, wup_ref[:, inner + i * w:inner + (i + 1) * w])).astype(BF16)
        gacc = gacc + _dot(xc, gc_ref[cs, :]) + _dot(xmb, gm_ref[cs, :])
    gates_ref[0] = gacc
    nbuf_ref[0] = carry[pad - (width - 1):pad, :]


def _mpre(x, cbuf, g, wup, cw, cb, wq_t, wkt_t, wv_t, gc, gm, bg, tb):
    b, s, d = x.shape
    inner = cw.shape[1]
    width = cw.shape[0]
    ng = bg.shape[1]
    kern = functools.partial(_mpre_kernel, tb=tb, inner=inner, width=width)
    act = pl.BlockSpec((1, tb, inner), lambda i, t: (i, t, 0))
    blocks = _nbytes((tb, d), F32) + 5 * _nbytes((tb, inner), BF16)
    scratch = (_nbytes(wup.shape, BF16) + 3 * _nbytes(wq_t.shape, BF16) + 2 * _nbytes((inner, 128), BF16)
               + _nbytes((8, inner), F32) + 8 * _nbytes((tb, V7X_MXU), F32))
    return pl.pallas_call(
        kern,
        grid=(b, s // tb),
        in_specs=[pl.BlockSpec((1, tb, d), lambda i, t: (i, t, 0)),
                  pl.BlockSpec((1, width - 1, inner), lambda i, t: (i, 0, 0)),
                  _const_spec((1, d)), _const_spec(wup.shape), _const_spec(cw.shape), _const_spec((1, inner)),
                  _const_spec(wq_t.shape), _const_spec(wkt_t.shape), _const_spec(wv_t.shape),
                  _const_spec(gc.shape), _const_spec(gm.shape), _const_spec((1, ng))],
        out_specs=[act, pl.BlockSpec((1, inner, tb), lambda i, t: (i, 0, t)), act, act, act,
                   pl.BlockSpec((1, tb, ng), lambda i, t: (i, t, 0)),
                   pl.BlockSpec((1, width - 1, inner), lambda i, t: (i, 0, 0))],
        out_shape=[jax.ShapeDtypeStruct((b, s, inner), BF16), jax.ShapeDtypeStruct((b, inner, s), BF16),
                   jax.ShapeDtypeStruct((b, s, inner), BF16), jax.ShapeDtypeStruct((b, s, inner), BF16),
                   jax.ShapeDtypeStruct((b, s, inner), BF16), jax.ShapeDtypeStruct((b, s, ng), F32),
                   jax.ShapeDtypeStruct((b, width - 1, inner), F32)],
        scratch_shapes=[pltpu.VMEM((V7X_SUBLANES, inner), F32)],
        compiler_params=pltpu.CompilerParams(
            dimension_semantics=("arbitrary", "arbitrary"), vmem_limit_bytes=_vmem_limit(blocks, scratch)),
        name="mlstm_pre",
    )(x, cbuf, g, wup, cw, cb, wq_t, wkt_t, wv_t, gc, gm, bg)


def _mlstm_chunk(q_ref, kt_ref, v_ref, g, gt, hn_ref, ct_s, n_s, m_s, fill, *, lc, heads, dh):
    row = lax.broadcasted_iota(jnp.int32, (lc, lc), 0)
    col = lax.broadcasted_iota(jnp.int32, (lc, lc), 1)
    causal = col <= row
    tri_l = jnp.where(causal, 1.0, 0.0).astype(BF16)
    tri_u = jnp.where(row <= col, 1.0, 0.0).astype(BF16)
    b_cols = sum(_dot(tri_l, part) for part in _split3(_log_sigmoid(g)))
    b_rows = sum(_dot(part, tri_u) for part in _split3(_log_sigmoid(gt)))

    hsl = [slice(h * dh, (h + 1) * dh) for h in range(heads)]
    st = []
    for h in range(heads):
        b_col = b_cols[:, heads + h:heads + h + 1]
        b_row = b_rows[heads + h:heads + h + 1, :]
        i_row = gt[h:h + 1, :]
        m_prev = m_s[h:h + 1, 0:1]
        dmat = jnp.where(causal, b_col - b_row + i_row, -jnp.inf)
        inter = b_col + m_prev
        m_t = jnp.maximum(inter, jnp.max(dmat, axis=1, keepdims=True))
        st.append(dict(b_col=b_col, b_row=b_row, i_row=i_row, m_prev=m_prev, m_t=m_t,
                       wts=jnp.exp(dmat - m_t), a=jnp.exp(inter - m_t)))
        fill()
    for h in range(heads):
        s = st[h]
        s['sc'] = _dot(q_ref[0, :, hsl[h]], kt_ref[0, hsl[h], :]) * s['wts']
    for h in range(heads):
        s = st[h]
        qh = q_ref[0, :, hsl[h]]
        sc = s['sc']
        a = s['a']
        num = _dot(sc.astype(BF16), v_ref[0, :, hsl[h]]) + a * _dot(qh, ct_s[h].astype(BF16))
        qn = jnp.sum(qh.astype(F32) * n_s[h:h + 1, :], axis=1, keepdims=True)
        den = jnp.sum(sc, axis=1, keepdims=True) + a * qn
        hh = num / jnp.maximum(jnp.abs(den), jnp.exp(-s['m_t']))
        mu = jnp.mean(hh, axis=1, keepdims=True)
        dev = hh - mu
        var = jnp.mean(dev * dev, axis=1, keepdims=True)
        hn_ref[0, :, hsl[h]] = (dev * lax.rsqrt(var + LN_EPS)).astype(hn_ref.dtype)
        fill()
    for h in range(heads):
        s = st[h]
        kth = kt_ref[0, hsl[h], :]
        b_last = s['b_row'][:, lc - 1:lc]
        g_row = b_last - s['b_row'] + s['i_row']
        m_new = jnp.maximum(b_last + s['m_prev'], jnp.max(g_row, axis=1, keepdims=True))
        decay = jnp.exp(b_last + s['m_prev'] - m_new)
        wg_row = jnp.exp(g_row - m_new)
        wg_col = jnp.exp(b_last - s['b_col'] + g[:, h:h + 1] - m_new)
        wv = (v_ref[0, :, hsl[h]].astype(F32) * wg_col).astype(BF16)
        ct_s[h] = decay * ct_s[h] + _dot(kth, wv)
        wg16 = jnp.broadcast_to(wg_row, (16, lc)).astype(BF16)
        n_s[h:h + 1, :] = decay * n_s[h:h + 1, :] + _dot_nt(wg16, kth)[0:1, :]
        m_s[h:h + 1, :] = jnp.broadcast_to(m_new, (1, V7X_LANES))


def _mlstm_layer_kernel(*refs, lc, heads, inner, width, zero_init):
    (x_ref, cbuf_ref, g0_ref, wup_ref, cw_ref, cb_ref, wq_ref, wkt_ref, wv_ref, gc_ref, gm_ref, bg_ref,
     ng_ref, sk_ref, wdown_ref, g1_ref) = refs[:16]
    n_in = 16
    if not zero_init:
        c0_ref, n0_ref, m0_ref = refs[16:19]
        n_in = 19
    o_ref, cout_ref, nout_ref, mout_ref, nbuf_ref = refs[n_in:n_in + 5]
    carry, q_s, kt_s, v_s, xc_s, sz_s, hn_s, ct_s, n_s, m_s = refs[n_in + 5:]
    dh = inner // heads
    pad = V7X_SUBLANES
    c = pl.program_id(1)

    @pl.when(c == 0)
    def _():
        carry[...] = jnp.zeros_like(carry)
        carry[pad - (width - 1):pad, :] = cbuf_ref[0]
        if zero_init:
            ct_s[...] = jnp.zeros_like(ct_s)
            n_s[...] = jnp.zeros_like(n_s)
            m_s[...] = jnp.zeros_like(m_s)
        else:
            for h in range(heads):
                ct_s[h] = c0_ref[0, h].T
            n_s[...] = n0_ref[0]
            m_s[...] = m0_ref[0]

    x = x_ref[0]
    hx = _rms(x, g0_ref[...]).astype(BF16)
    w = V7X_MXU
    gacc = jnp.zeros((lc, bg_ref.shape[1]), F32) + bg_ref[...]
    nslab = inner // w
    col = lambda i: slice(i * w, (i + 1) * w)

    def project(i, xc, xmb, gacc):
        q_s[0, :, col(i)] = _dot(xc, wq_ref[i]).astype(BF16)
        kt_s[0, col(i), :] = _dot_nt(wkt_ref[i], xc).astype(BF16)
        v_s[0, :, col(i)] = _dot(xmb, wv_ref[i]).astype(BF16)
        xc_s[:, col(i)] = xc
        return gacc + _dot(xc, gc_ref[col(i), :]) + _dot(xmb, gm_ref[col(i), :])

    xm_next = _dot(hx, wup_ref[:, col(0)])
    pending = None
    for i in range(nslab):
        xm = xm_next
        if i + 1 < nslab:
            xm_next = _dot(hx, wup_ref[:, col(i + 1)])
        if pending is not None:
            gacc = project(*pending, gacc)
        prev = carry[:, col(i)]
        carry[:, col(i)] = xm[lc - pad:lc, :]
        xc = _silu(_causal_dwconv(xm, prev, cw_ref, cb_ref, col(i))).astype(BF16)
        pending = (i, xc, xm.astype(BF16))
    gacc = project(*pending, gacc)
    nbuf_ref[0] = carry[pad - (width - 1):pad, :]

    slabs = iter(range(inner // w))

    def gate_slab():
        i = next(slabs, None)
        if i is not None:
            cs = slice(i * w, (i + 1) * w)
            sz_s[:, cs] = _silu(_dot(hx, wup_ref[:, inner + i * w:inner + (i + 1) * w])).astype(BF16)

    _mlstm_chunk(q_s, kt_s, v_s, gacc, gacc.T[0:2 * heads, :], hn_s, ct_s, n_s, m_s, gate_slab,
                 lc=lc, heads=heads, dh=dh)
    for _ in slabs:
        raise AssertionError("fewer fill points than output-gate slabs")

    hs = hn_s[0].astype(F32) * ng_ref[...] + sk_ref[...] * xc_s[...].astype(F32)
    act = (hs * sz_s[...].astype(F32)).astype(BF16)
    o_ref[0] = x + _rms(_dot(act, wdown_ref[...]), g1_ref[...])

    @pl.when(c == pl.num_programs(1) - 1)
    def _():
        for h in range(heads):
            cout_ref[0, h] = ct_s[h].T
        nout_ref[0] = n_s[...]
        mout_ref[0] = m_s[...]


def _mlstm_layer(x, cbuf, state, p, g0, g1, lc):
    b, s, d = x.shape
    heads = p['heads']
    inner = p['m_cw'].shape[1]
    width = p['m_cw'].shape[0]
    dh = inner // heads
    zero_init = state is None
    kern = functools.partial(_mlstm_layer_kernel, lc=lc, heads=heads, inner=inner, width=width,
                             zero_init=zero_init)
    xs = pl.BlockSpec((1, lc, d), lambda i, c: (i, c, 0))
    bufs = pl.BlockSpec((1, width - 1, inner), lambda i, c: (i, 0, 0))
    cs = pl.BlockSpec((1, heads, dh, dh), lambda i, c: (i, 0, 0, 0))
    ns = pl.BlockSpec((1, heads, dh), lambda i, c: (i, 0, 0))
    ms = pl.BlockSpec((1, heads, V7X_LANES), lambda i, c: (i, 0, 0))
    consts = [g0, p['m_wup'], p['m_cw'], p['m_cb'], p['m_wq'], p['m_wkt'], p['m_wv'], p['m_gc'], p['m_gm'],
              p['m_bg'], p['m_ng'], p['m_skip'], p['m_wdown'], g1]
    in_specs = [xs, bufs] + [_const_spec(a.shape) for a in consts]
    args = [x, cbuf] + consts
    if not zero_init:
        in_specs += [cs, ns, ms]
        args += list(state)
    blocks = 2 * _nbytes((lc, d), F32) + (1 if zero_init else 2) * _nbytes((heads, dh, dh), F32)
    scratch = (sum(_nbytes(a.shape, a.dtype) for a in consts) + 6 * _nbytes((lc, inner), BF16)
               + _nbytes((heads, dh, dh), F32) + 8 * _nbytes((lc, max(lc, dh)), F32) + 2 * _nbytes((dh, dh), F32))
    return pl.pallas_call(
        kern,
        grid=(b, s // lc),
        in_specs=in_specs,
        out_specs=[xs, cs, ns, ms, bufs],
        out_shape=[jax.ShapeDtypeStruct((b, s, d), F32), jax.ShapeDtypeStruct((b, heads, dh, dh), F32),
                   jax.ShapeDtypeStruct((b, heads, dh), F32), jax.ShapeDtypeStruct((b, heads, V7X_LANES), F32),
                   jax.ShapeDtypeStruct((b, width - 1, inner), F32)],
        scratch_shapes=[pltpu.VMEM((V7X_SUBLANES, inner), F32),
                        pltpu.VMEM((1, lc, inner), BF16), pltpu.VMEM((1, inner, lc), BF16),
                        pltpu.VMEM((1, lc, inner), BF16), pltpu.VMEM((lc, inner), BF16),
                        pltpu.VMEM((lc, inner), BF16), pltpu.VMEM((1, lc, inner), BF16),
                        pltpu.VMEM((heads, dh, dh), F32), pltpu.VMEM((heads, dh), F32),
                        pltpu.VMEM((heads, V7X_LANES), F32)],
        compiler_params=pltpu.CompilerParams(
            dimension_semantics=("arbitrary", "arbitrary"), vmem_limit_bytes=_vmem_limit(blocks, scratch)),
        name="mlstm_layer",
    )(*args)


def _block_diag_tiles(w):
    nblk, k, _ = w.shape
    per = V7X_MXU // k
    wt = w.reshape(nblk // per, per, k, k)
    eye = jnp.eye(per, dtype=w.dtype)
    full = wt[:, :, :, None, :] * eye[None, :, None, :, None]
    return full.reshape(nblk // per, per * k, per * k)


def _prep_weights(w):
    row = lambda v: v.astype(F32).reshape(1, -1)
    p = {}
    p['norm_g'] = w['norm_g'].astype(F32)
    p['s5_mats'] = _s5_matrices(w['s5_A_re'][0], w['s5_A_im'][0], w['s5_log_dt'][0], w['s5_B_re'][0],
                                w['s5_B_im'][0], w['s5_C_re'][0], w['s5_C_im'][0])
    p['s5_D'] = row(w['s5_D'][0])
    p['s5_w_glu'] = w['s5_w_glu'][0].astype(BF16)
    inner = w['mlstm_conv_w'].shape[2]
    heads = w['mlstm_b_gate'].shape[1] // 2
    dh = inner // heads
    wq_t = _block_diag_tiles(w['mlstm_wq'][0].astype(F32))
    wk_t = _block_diag_tiles(w['mlstm_wk'][0].astype(F32))
    wv_t = _block_diag_tiles(w['mlstm_wv'][0].astype(F32))
    gc, gm = _gate_fold(wq_t, wk_t, wv_t, w['mlstm_w_gate'][0].astype(F32))
    p['m_wq'] = wq_t.astype(BF16)
    p['m_wkt'] = (jnp.swapaxes(wk_t, 1, 2) * (dh ** -0.5)).astype(BF16)
    p['m_wv'] = wv_t.astype(BF16)
    lane_pad = lambda a: jnp.pad(a, ((0, 0), (0, V7X_LANES - a.shape[1])))
    p['m_gc'] = lane_pad(gc).astype(BF16)
    p['m_gm'] = lane_pad(gm).astype(BF16)
    p['m_bg'] = lane_pad(row(w['mlstm_b_gate'][0]))
    p['m_wup'] = w['mlstm_w_up'][0].astype(BF16)
    p['m_cw'] = w['mlstm_conv_w'][0].astype(F32)
    p['m_cb'] = row(w['mlstm_conv_b'][0])
    p['m_ng'] = row(w['mlstm_norm_g'][0])
    p['m_skip'] = row(w['mlstm_skip'][0])
    p['m_wdown'] = w['mlstm_w_down'][0].astype(BF16)
    p['heads'] = heads
    p['f_wup'] = w['ffn_w_up'].astype(BF16)
    p['f_cw'] = w['ffn_conv_w'].astype(F32)
    p['f_cb'] = w['ffn_conv_b'].astype(F32)
    p['f_wdown'] = w['ffn_w_down'].astype(BF16)
    return p


def _to_chunk_major(u2d, b, s):
    d = u2d.shape[1]
    g = d // S5_GROUP_CH
    u = u2d.reshape(b, s // S5_CHUNK, S5_CHUNK, g, S5_GROUP_CH)
    return jnp.transpose(u, (1, 0, 3, 2, 4)).reshape((s // S5_CHUNK) * b, d * S5_CHUNK)


def _to_token_major(yt, b, s):
    d = yt.shape[1] // S5_CHUNK
    g = d // S5_GROUP_CH
    y = yt.reshape(s // S5_CHUNK, b, g, S5_CHUNK, S5_GROUP_CH)
    return jnp.transpose(y, (1, 0, 3, 2, 4)).reshape(b * s, d)


def _trunk(x, st, p, tiles):
    b, s, d = x.shape
    ng = p['norm_g']
    g_of = lambda layer, k: ng[layer, k].reshape(1, d)
    zero_init = st is None
    groups = d // S5_GROUP_CH
    x2d = x.reshape(b * s, d)

    n_state = p['s5_mats']['row'][1].shape[2] // 2
    if zero_init:
        h0re = h0im = jnp.zeros((groups // 2, b, 2 * n_state), F32)
    else:
        pair = lambda h: jnp.transpose(h.astype(F32).reshape(b, groups // 2, 2 * n_state), (1, 0, 2))
        h0re, h0im = pair(st['s5_re']), pair(st['s5_im'])
    if tiles.s5_time:
        y, hre, him = _s5_fused(x, g_of(0, 0), p['s5_mats']['col'], h0re, h0im, tiles.s5_time)
        y2d = y.reshape(b * s, d)
    else:
        u = _norm_cast(x2d, g_of(0, 0), tiles.rows)
        yt, hre, him = _s5_core(_to_chunk_major(u, b, s), p['s5_mats']['row'], h0re, h0im, b, tiles.s5_rows)
        y2d = _to_token_major(yt, b, s)
    unpair = lambda h: jnp.transpose(h, (1, 0, 2)).reshape(1, b, groups, n_state)
    o_re, o_im = unpair(hre), unpair(him)
    x2d = _glu(x2d, y2d, g_of(0, 0), p['s5_D'], p['s5_w_glu'], g_of(0, 1), tiles.rows)

    def ffn(x2d, layer):
        dff2 = p['f_wup'].shape[2]
        width = p['f_cw'].shape[1]
        buf = jnp.zeros((b, width - 1, dff2), F32) if zero_init else st['ffn_conv'][layer].astype(F32)
        xo, nbuf = _ffn(x2d.reshape(b, s, d), buf, g_of(layer, 2), p['f_wup'][layer], p['f_cw'][layer],
                        p['f_cb'][layer].reshape(1, dff2), p['f_wdown'][layer], g_of(layer, 3), tiles.ffn_time)
        return xo.reshape(b * s, d), nbuf

    x2d, fbuf0 = ffn(x2d, 0)

    heads = p['heads']
    inner = p['m_cw'].shape[1]
    width = p['m_cw'].shape[0]
    cbuf = jnp.zeros((b, width - 1, inner), F32) if zero_init else st['mlstm_conv'].astype(F32)
    if zero_init:
        state = None
    else:
        state = (st['mlstm_C'].astype(F32), st['mlstm_n'].astype(F32),
                 jnp.broadcast_to(st['mlstm_m'].astype(F32)[:, :, None], (b, heads, V7X_LANES)))
    x3, c_out, n_out, m_out, ncbuf = _mlstm_layer(x2d.reshape(b, s, d), cbuf, state, p, g_of(1, 0), g_of(1, 1),
                                                  tiles.mlstm_chunk)
    x2d, fbuf1 = ffn(x3.reshape(b * s, d), 1)

    return (x2d.reshape(b, s, d), o_re, o_im, c_out[None], n_out[None], m_out[None, :, :, 0], ncbuf[None],
            jnp.stack([fbuf0, fbuf1], axis=0))


def kernel(x_prompt, x_sample, state_s5_re, state_s5_im, state_mlstm_C, state_mlstm_n, state_mlstm_m, state_mlstm_conv, state_ffn_conv, norm_g, s5_A_re, s5_A_im, s5_log_dt, s5_B_re, s5_B_im, s5_C_re, s5_C_im, s5_D, s5_w_glu, mlstm_w_up, mlstm_conv_w, mlstm_conv_b, mlstm_wq, mlstm_wk, mlstm_wv, mlstm_w_gate, mlstm_b_gate, mlstm_norm_g, mlstm_skip, mlstm_w_down, ffn_w_up, ffn_conv_w, ffn_conv_b, ffn_w_down):
    w = {
        'norm_g': norm_g, 's5_A_re': s5_A_re, 's5_A_im': s5_A_im, 's5_log_dt': s5_log_dt,
        's5_B_re': s5_B_re, 's5_B_im': s5_B_im, 's5_C_re': s5_C_re, 's5_C_im': s5_C_im, 's5_D': s5_D,
        's5_w_glu': s5_w_glu, 'mlstm_w_up': mlstm_w_up, 'mlstm_conv_w': mlstm_conv_w,
        'mlstm_conv_b': mlstm_conv_b, 'mlstm_wq': mlstm_wq, 'mlstm_wk': mlstm_wk, 'mlstm_wv': mlstm_wv,
        'mlstm_w_gate': mlstm_w_gate, 'mlstm_b_gate': mlstm_b_gate, 'mlstm_norm_g': mlstm_norm_g,
        'mlstm_skip': mlstm_skip, 'mlstm_w_down': mlstm_w_down, 'ffn_w_up': ffn_w_up,
        'ffn_conv_w': ffn_conv_w, 'ffn_conv_b': ffn_conv_b, 'ffn_w_down': ffn_w_down,
    }
    assert norm_g.shape[0] == 2 and s5_A_re.shape[0] == 1 and mlstm_w_up.shape[0] == 1
    assert x_prompt.shape[0] == V7X_SUBLANES and x_sample.shape[0] == V7X_SUBLANES
    p = _prep_weights(w)
    out_p = _trunk(x_prompt.astype(F32), None, p, _pick_tiles(*x_prompt.shape[:2]))
    st = {'s5_re': state_s5_re[0], 's5_im': state_s5_im[0], 'mlstm_C': state_mlstm_C[0],
          'mlstm_n': state_mlstm_n[0], 'mlstm_m': state_mlstm_m[0], 'mlstm_conv': state_mlstm_conv[0],
          'ffn_conv': state_ffn_conv}
    out_s = _trunk(x_sample.astype(F32), st, p, _pick_tiles(*x_sample.shape[:2]))
    return (out_p[0], out_s[0]) + tuple(out_p[1:]) + tuple(out_s[1:])
```

```python
import functools
import math
from typing import NamedTuple

import jax
import jax.numpy as jnp
from jax import lax
from jax.experimental import pallas as pl
from jax.experimental.pallas import tpu as pltpu

F32 = jnp.float32
BF16 = jnp.bfloat16
HI = lax.Precision.HIGHEST

NORM_EPS = 1e-6
LN_EPS = 1e-5
S5_GROUP_CH = 16
S5_CHUNK = 16
QKV_BLOCK = 4

V7X_LANES = 128
V7X_SUBLANES = 8
V7X_MXU = 256
V7X_VMEM_BYTES = 64 * 1024 * 1024


class Tiles(NamedTuple):
    rows: int
    time: int
    s5_rows: int
    mlstm_chunk: int
    s5_time: int
    ffn_time: int


def _pick_tiles(batch, seq):
    s5_time = (V7X_LANES // batch) * S5_CHUNK
    return Tiles(rows=min(batch * seq, 512), time=min(seq, 512),
                 s5_rows=min(batch * (seq // S5_CHUNK), 512), mlstm_chunk=min(seq, 256),
                 s5_time=s5_time if seq % s5_time == 0 else 0, ffn_time=min(seq, 1024))


def _vmem_limit(block_bytes, scratch_bytes):
    want = 2 * block_bytes + scratch_bytes + 16 * 1024 * 1024
    return int(min(want, V7X_VMEM_BYTES - 8 * 1024 * 1024))


def _nbytes(shape, dtype):
    return math.prod(shape) * jnp.dtype(dtype).itemsize


def _rms(x, g):
    return x * lax.rsqrt(jnp.mean(x * x, axis=-1, keepdims=True) + NORM_EPS) * g


def _dot(a, b):
    return jnp.dot(a, b, preferred_element_type=F32)


def _dot_nt(a, b):
    return lax.dot_general(a, b, (((1,), (1,)), ((), ())), preferred_element_type=F32)


def _silu(x):
    return (0.5 * x) * (1.0 + jnp.tanh(0.5 * x))


def _causal_dwconv(u, prev, cw_ref, cb_ref, cs):
    width = cw_ref.shape[0]
    acc = cb_ref[:, cs] + u * cw_ref[width - 1:width, cs]
    rows = lax.broadcasted_iota(jnp.int32, prev.shape, 0)
    for j in range(width - 1):
        d = width - 1 - j
        rolled = pltpu.roll(u, d, 0)
        head = jnp.where(rows < d, pltpu.roll(prev, d, 0), rolled[0:V7X_SUBLANES])
        shifted = jnp.concatenate([head, rolled[V7X_SUBLANES:]], axis=0)
        acc = acc + shifted * cw_ref[j:j + 1, cs]
    return acc


def _log_sigmoid(x):
    return jnp.minimum(x, 0.0) - jnp.log1p(jnp.exp(-jnp.abs(x)))


def _split3(x):
    hi = x.astype(BF16)
    r1 = x - hi.astype(F32)
    mid = r1.astype(BF16)
    lo = (r1 - mid.astype(F32)).astype(BF16)
    return hi, mid, lo


def _norm_cast_kernel(x_ref, g_ref, o_ref):
    o_ref[...] = _rms(x_ref[...], g_ref[...]).astype(o_ref.dtype)


def _norm_cast(x2d, g, tb):
    rows, d = x2d.shape
    return pl.pallas_call(
        _norm_cast_kernel,
        grid=(rows // tb,),
        in_specs=[pl.BlockSpec((tb, d), lambda i: (i, 0)), pl.BlockSpec((1, d), lambda i: (0, 0))],
        out_specs=pl.BlockSpec((tb, d), lambda i: (i, 0)),
        out_shape=jax.ShapeDtypeStruct((rows, d), BF16),
        name="s5_norm",
    )(x2d, g)


def _s5_kernel(u_ref, t_ref, pre_ref, pim_ref, qre_ref, qim_ref, are_ref, aim_ref, h0re_ref, h0im_ref,
               y_ref, hre_ref, him_ref, pure_s, puim_s, hinre_s, hinim_s, stre_s, stim_s, *, nb, rb):
    r = pl.program_id(1)

    @pl.when(r == 0)
    def _():
        stre_s[...] = h0re_ref[0]
        stim_s[...] = h0im_ref[0]

    u = u_ref[...]
    pure_s[...] = _dot(u, pre_ref[0])
    puim_s[...] = _dot(u, pim_ref[0])
    ar = jnp.broadcast_to(are_ref[0], (nb, 2 * 64))
    ai = jnp.broadcast_to(aim_ref[0], (nb, 2 * 64))

    def step(i, carry):
        re, im = carry
        rows = pl.ds(pl.multiple_of(i * nb, nb), nb)
        hinre_s[rows, :] = re
        hinim_s[rows, :] = im
        return (ar * re - ai * im + pure_s[rows, :], ar * im + ai * re + puim_s[rows, :])

    re, im = lax.fori_loop(0, rb // nb, step, (stre_s[...], stim_s[...]))
    stre_s[...] = re
    stim_s[...] = im
    hre_ref[0] = re
    him_ref[0] = im

    carry_in = _dot(hinre_s[...].astype(BF16), qre_ref[0]) + _dot(hinim_s[...].astype(BF16), qim_ref[0])
    w = V7X_MXU
    for k in range(2):
        y_ref[:, k * w:(k + 1) * w] = (_dot(u[:, k * w:(k + 1) * w], t_ref[k])
                                       + carry_in[:, k * w:(k + 1) * w]).astype(y_ref.dtype)


def _s5_core(ut, mats, h0re, h0im, nb, rb):
    rows, cols = ut.shape
    pairs = cols // (2 * V7X_MXU)
    tm, pre, pim, qre, qim, are, aim = mats
    pair3 = lambda p, r: (p, 0, 0)
    kern = functools.partial(_s5_kernel, nb=nb, rb=rb)
    blocks = (_nbytes((rb, 512), BF16) * 2 + _nbytes((2, 256, 256), BF16) + 4 * _nbytes((512, 128), BF16))
    scratch = 4 * _nbytes((rb, 128), F32) + 2 * _nbytes((nb, 128), F32)
    return pl.pallas_call(
        kern,
        grid=(pairs, rows // rb),
        in_specs=[
            pl.BlockSpec((rb, 2 * V7X_MXU), lambda p, r: (r, p)),
            pl.BlockSpec((2, V7X_MXU, V7X_MXU), pair3),
            pl.BlockSpec((1, 2 * V7X_MXU, 128), pair3),
            pl.BlockSpec((1, 2 * V7X_MXU, 128), pair3),
            pl.BlockSpec((1, 128, 2 * V7X_MXU), pair3),
            pl.BlockSpec((1, 128, 2 * V7X_MXU), pair3),
            pl.BlockSpec((1, 1, 128), pair3),
            pl.BlockSpec((1, 1, 128), pair3),
            pl.BlockSpec((1, nb, 128), pair3),
            pl.BlockSpec((1, nb, 128), pair3),
        ],
        out_specs=[
            pl.BlockSpec((rb, 2 * V7X_MXU), lambda p, r: (r, p)),
            pl.BlockSpec((1, nb, 128), pair3),
            pl.BlockSpec((1, nb, 128), pair3),
        ],
        out_shape=[
            jax.ShapeDtypeStruct((rows, cols), BF16),
            jax.ShapeDtypeStruct((pairs, nb, 128), F32),
            jax.ShapeDtypeStruct((pairs, nb, 128), F32),
        ],
        scratch_shapes=[pltpu.VMEM((rb, 128), F32)] * 4 + [pltpu.VMEM((nb, 128), F32)] * 2,
        compiler_params=pltpu.CompilerParams(
            dimension_semantics=("arbitrary", "arbitrary"), vmem_limit_bytes=_vmem_limit(blocks, scratch)),
        name="s5_core",
    )(ut, tm, pre, pim, qre, qim, are, aim, h0re, h0im)


def _s5_fused_kernel(x_ref, g_ref, tt_ref, pt_ref, qt_ref, are_ref, aim_ref, h0re_ref, h0im_ref,
                     y_ref, hre_ref, him_ref,
                     us, ugt, pure, puim, hinre, hinim, ytmp, ys, stre, stim, *, nb, tbt):
    lc = S5_CHUNK
    ch = S5_GROUP_CH
    gps = V7X_LANES // ch
    nch = tbt // lc
    cols = nb * nch
    k = pl.program_id(1)

    @pl.when(k == 0)
    def _():
        for b in range(nb):
            u = _rms(x_ref[b], g_ref[...])
            for kk in range(us.shape[0]):
                us[kk, b * tbt:(b + 1) * tbt, :] = u[:, kk * V7X_LANES:(kk + 1) * V7X_LANES]

    @pl.when(pl.program_id(0) == 0)
    def _():
        stre[k] = h0re_ref[...]
        stim[k] = h0im_ref[...]

    for s in range(lc):
        a = us[k, pl.ds(s, cols, stride=lc), :].T
        for g in range(gps):
            ugt[g, s * ch:(s + 1) * ch, :] = a[g * ch:(g + 1) * ch, :].astype(BF16)

    half = pt_ref.shape[1] // 2
    for pr in range(gps // 2):
        put0 = _dot(pt_ref[2 * pr], ugt[2 * pr])
        put1 = _dot(pt_ref[2 * pr + 1], ugt[2 * pr + 1])
        pure[pr] = jnp.concatenate([put0[:half], put1[:half]], axis=0).T
        puim[pr] = jnp.concatenate([put0[half:], put1[half:]], axis=0).T

    for pr in range(gps // 2):
        re = stre[k, pr]
        im = stim[k, pr]
        ar = jnp.broadcast_to(are_ref[pr], re.shape)
        ai = jnp.broadcast_to(aim_ref[pr], re.shape)
        for j in range(nch):
            rows = pl.ds(j, nb, stride=nch)
            hinre[pr, rows, :] = re
            hinim[pr, rows, :] = im
            re, im = (ar * re - ai * im + pure[pr, rows, :], ar * im + ai * re + puim[pr, rows, :])
        stre[k, pr] = re
        stim[k, pr] = im
        hre_ref[k * (gps // 2) + pr] = re
        him_ref[k * (gps // 2) + pr] = im

    for pr in range(gps // 2):
        hre_t = hinre[pr].T
        him_t = hinim[pr].T
        for i in range(2):
            g = 2 * pr + i
            hin_t = jnp.concatenate([hre_t[i * half:(i + 1) * half], him_t[i * half:(i + 1) * half]],
                                    axis=0).astype(BF16)
            ytmp[g] = _dot(tt_ref[g], ugt[g]) + _dot(qt_ref[g], hin_t)

    for t in range(lc):
        z = jnp.concatenate([ytmp[g, t * ch:(t + 1) * ch, :] for g in range(gps)], axis=0)
        ys[pl.ds(t, cols, stride=lc), :] = z.T
    for b in range(nb):
        y_ref[b] = ys[b * tbt:(b + 1) * tbt, :].astype(y_ref.dtype)


def _s5_fused(x, g, mats_t, h0re, h0im, tbt):
    nb, s, d = x.shape
    tt, pt, qt, are, aim = mats_t
    groups = tt.shape[0]
    gps = V7X_LANES // S5_GROUP_CH
    slabs = groups // gps
    w = S5_CHUNK * S5_GROUP_CH
    nst = pt.shape[1]
    cols = nb * (tbt // S5_CHUNK)
    kern = functools.partial(_s5_fused_kernel, nb=nb, tbt=tbt)
    slab3 = lambda t, k: (k, 0, 0)
    blocks = (_nbytes((nb, tbt, d), F32) + _nbytes((nb, tbt, V7X_LANES), BF16)
              + gps * (_nbytes((w, w), BF16) + 2 * _nbytes((w, nst), BF16)))
    scratch = (_nbytes((slabs, nb * tbt, V7X_LANES), F32) + _nbytes((gps, w, cols), BF16)
               + 4 * _nbytes((gps // 2, cols, nst), F32) + _nbytes((gps, w, cols), F32)
               + _nbytes((nb * tbt, V7X_LANES), F32))
    return pl.pallas_call(
        kern,
        grid=(s // tbt, slabs),
        in_specs=[
            pl.BlockSpec((nb, tbt, d), lambda t, k: (0, t, 0)),
            pl.BlockSpec((1, d), lambda t, k: (0, 0)),
            pl.BlockSpec((gps, w, w), slab3),
            pl.BlockSpec((gps, nst, w), slab3),
            pl.BlockSpec((gps, w, nst), slab3),
            pl.BlockSpec((gps // 2, 1, nst), slab3),
            pl.BlockSpec((gps // 2, 1, nst), slab3),
            pl.BlockSpec((gps // 2, nb, nst), slab3),
            pl.BlockSpec((gps // 2, nb, nst), slab3),
        ],
        out_specs=[
            pl.BlockSpec((nb, tbt, V7X_LANES), lambda t, k: (0, t, k)),
            pl.BlockSpec((groups // 2, nb, nst), lambda t, k: (0, 0, 0)),
            pl.BlockSpec((groups // 2, nb, nst), lambda t, k: (0, 0, 0)),
        ],
        out_shape=[
            jax.ShapeDtypeStruct((nb, s, d), BF16),
            jax.ShapeDtypeStruct((groups // 2, nb, nst), F32),
            jax.ShapeDtypeStruct((groups // 2, nb, nst), F32),
        ],
        scratch_shapes=[
            pltpu.VMEM((slabs, nb * tbt, V7X_LANES), F32),
            pltpu.VMEM((gps, w, cols), BF16),
            pltpu.VMEM((gps // 2, cols, nst), F32),
            pltpu.VMEM((gps // 2, cols, nst), F32),
            pltpu.VMEM((gps // 2, cols, nst), F32),
            pltpu.VMEM((gps // 2, cols, nst), F32),
            pltpu.VMEM((gps, w, cols), F32),
            pltpu.VMEM((nb * tbt, V7X_LANES), F32),
            pltpu.VMEM((slabs, gps // 2, nb, nst), F32),
            pltpu.VMEM((slabs, gps // 2, nb, nst), F32),
        ],
        compiler_params=pltpu.CompilerParams(
            dimension_semantics=("arbitrary", "arbitrary"), vmem_limit_bytes=_vmem_limit(blocks, scratch)),
        name="s5_fused",
    )(x, g, tt, pt, qt, are, aim, h0re, h0im)


def _s5_matrices(a_re, a_im, log_dt, b_re, b_im, c_re, c_im):
    lc = S5_CHUNK
    groups, n = a_re.shape
    ch = S5_GROUP_CH
    a = lax.complex(a_re.astype(F32), a_im.astype(F32))
    adt = a * jnp.exp(log_dt.astype(F32))[:, None]
    j = jnp.arange(lc + 1, dtype=F32)
    apow = jnp.exp(adt[None] * j[:, None, None])
    b_bar = ((apow[1] - 1.0) / a)[..., None] * lax.complex(b_re.astype(F32), b_im.astype(F32))
    c_mat = lax.complex(c_re.astype(F32), c_im.astype(F32))
    ca = c_mat[None] * apow[:, :, None, :]
    kern = (jnp.einsum('jgcn,gnd->jgcd', ca.real, b_bar.real, precision=HI)
            - jnp.einsum('jgcn,gnd->jgcd', ca.imag, b_bar.imag, precision=HI))
    s_idx = jnp.arange(lc)[:, None]
    t_idx = jnp.arange(lc)[None, :]
    lag = t_idx - s_idx
    kt = kern[jnp.clip(lag, 0, lc)]
    kt = jnp.where((lag >= 0)[:, :, None, None, None], kt, 0.0)
    tm = jnp.transpose(kt, (2, 0, 4, 1, 3)).reshape(groups, lc * ch, lc * ch)
    pw = apow[lc - 1 - jnp.arange(lc)]
    p = pw[:, :, :, None] * b_bar[None]
    p = jnp.transpose(p, (1, 0, 3, 2)).reshape(groups, lc * ch, n)
    q = ca[1:]
    q = jnp.transpose(q, (1, 3, 0, 2)).reshape(groups, n, lc * ch)

    def pair_rows(m):
        g2 = groups // 2
        m = m.reshape(g2, 2, m.shape[1], m.shape[2])
        z = jnp.zeros_like(m[:, 0])
        top = jnp.concatenate([m[:, 0], z], axis=2)
        bot = jnp.concatenate([z, m[:, 1]], axis=2)
        return jnp.concatenate([top, bot], axis=1)

    pre = pair_rows(p.real).astype(BF16)
    pim = pair_rows(p.imag).astype(BF16)
    qre = pair_rows(q.real).astype(BF16)
    qim = pair_rows(-q.imag).astype(BF16)
    a_chunk = apow[lc].reshape(groups // 2, 1, 2 * n)
    are, aim = a_chunk.real, a_chunk.imag
    tt = jnp.swapaxes(tm, 1, 2).astype(BF16)
    pt = jnp.concatenate([jnp.swapaxes(p.real, 1, 2), jnp.swapaxes(p.imag, 1, 2)], axis=1).astype(BF16)
    qt = jnp.concatenate([jnp.swapaxes(q.real, 1, 2), jnp.swapaxes(-q.imag, 1, 2)], axis=2).astype(BF16)
    return {'row': (tm.astype(BF16), pre, pim, qre, qim, are, aim), 'col': (tt, pt, qt, are, aim)}


def _glu_kernel(x_ref, y_ref, g0_ref, d_ref, w_ref, g1_ref, o_ref, out_s):
    x = x_ref[...]
    d = x.shape[1]
    u = _rms(x, g0_ref[...])
    yy = y_ref[...].astype(F32) + d_ref[...] * u
    z = jax.nn.gelu(yy, approximate=True).astype(BF16)
    w = V7X_MXU
    nslab = d // w
    pair = lambda j: (_dot(z, w_ref[:, j * w:(j + 1) * w]), _dot(z, w_ref[:, d + j * w:d + (j + 1) * w]))
    nxt = pair(0)
    ssq = jnp.zeros((x.shape[0], 1), F32)
    for j in range(nslab):
        a, b = nxt
        if j + 1 < nslab:
            nxt = pair(j + 1)
        out = a * (0.5 * (1.0 + jnp.tanh(0.5 * b)))
        out_s[:, j * w:(j + 1) * w] = out
        ssq = ssq + jnp.sum(out * out, axis=-1, keepdims=True)
    scale = lax.rsqrt(ssq * (1.0 / d) + NORM_EPS)
    o_ref[...] = x + out_s[...] * scale * g1_ref[...]


def _const_spec(shape):
    nd = len(shape)
    return pl.BlockSpec(shape, lambda *_: (0,) * nd, pipeline_mode=pl.Buffered(1))


def _glu(x2d, y2d, g0, dskip, w, g1, tb):
    rows, d = x2d.shape
    blocks = 2 * _nbytes((tb, d), F32) + _nbytes((tb, d), BF16)
    scratch = _nbytes(w.shape, BF16) + 3 * _nbytes((tb, 2 * d), F32)
    row = pl.BlockSpec((tb, d), lambda i: (i, 0))
    return pl.pallas_call(
        _glu_kernel,
        grid=(rows // tb,),
        in_specs=[row, row, _const_spec((1, d)), _const_spec((1, d)), _const_spec(w.shape), _const_spec((1, d))],
        out_specs=row,
        out_shape=jax.ShapeDtypeStruct((rows, d), F32),
        scratch_shapes=[pltpu.VMEM((tb, d), F32)],
        compiler_params=pltpu.CompilerParams(
            dimension_semantics=("arbitrary",), vmem_limit_bytes=_vmem_limit(blocks, scratch)),
        name="s5_glu",
    )(x2d, y2d, g0, dskip, w, g1)


def _ffn_kernel(x_ref, buf_ref, g2_ref, wup_ref, cw_ref, cb_ref, wdown_ref, g3_ref,
                o_ref, nbuf_ref, carry, act, *, tb, dff, width):
    t = pl.program_id(1)
    pad = V7X_SUBLANES

    @pl.when(t == 0)
    def _():
        carry[...] = jnp.zeros_like(carry)
        carry[pad - (width - 1):pad, :] = buf_ref[0]

    x = x_ref[0]
    hn = _rms(x, g2_ref[...]).astype(BF16)
    fb = V7X_MXU

    def conv(u, cs):
        prev = carry[:, cs]
        carry[:, cs] = u[tb - pad:tb, :]
        return _causal_dwconv(u, prev, cw_ref, cb_ref, cs)

    nslab = dff // fb
    cols = lambda j: (slice(j * fb, (j + 1) * fb), slice(dff + j * fb, dff + (j + 1) * fb))
    up = lambda j: tuple(_dot(hn, wup_ref[:, cs]) for cs in cols(j))
    nxt = up(0)
    for j in range(nslab):
        ug, uv = nxt
        if j + 1 < nslab:
            nxt = up(j + 1)
        gate = conv(ug, cols(j)[0])
        val = conv(uv, cols(j)[1])
        act[:, j * fb:(j + 1) * fb] = (jax.nn.gelu(gate, approximate=True) * val).astype(BF16)

    f = _dot(act[...], wdown_ref[...])
    o_ref[0] = x + _rms(f, g3_ref[...])
    nbuf_ref[0] = carry[pad - (width - 1):pad, :]


def _ffn(x, buf, g2, wup, cw, cb, wdown, g3, tb):
    b, s, d = x.shape
    dff = wdown.shape[0]
    width = cw.shape[0]
    kern = functools.partial(_ffn_kernel, tb=tb, dff=dff, width=width)
    blocks = 2 * _nbytes((tb, d), F32)
    scratch = (_nbytes(wup.shape, BF16) + _nbytes(wdown.shape, BF16) + 6 * _nbytes((tb, V7X_MXU), F32)
               + _nbytes((8, 2 * dff), F32) + _nbytes((tb, dff), BF16) + 2 * _nbytes((tb, d), F32))
    xs = pl.BlockSpec((1, tb, d), lambda i, t: (i, t, 0))
    bs = pl.BlockSpec((1, width - 1, 2 * dff), lambda i, t: (i, 0, 0))
    return pl.pallas_call(
        kern,
        grid=(b, s // tb),
        in_specs=[xs, bs, _const_spec((1, d)), _const_spec(wup.shape), _const_spec(cw.shape),
                  _const_spec((1, 2 * dff)), _const_spec(wdown.shape), _const_spec((1, d))],
        out_specs=[xs, bs],
        out_shape=[jax.ShapeDtypeStruct((b, s, d), F32), jax.ShapeDtypeStruct((b, width - 1, 2 * dff), F32)],
        scratch_shapes=[pltpu.VMEM((V7X_SUBLANES, 2 * dff), F32), pltpu.VMEM((tb, dff), BF16)],
        compiler_params=pltpu.CompilerParams(
            dimension_semantics=("arbitrary", "arbitrary"), vmem_limit_bytes=_vmem_limit(blocks, scratch)),
        name="conv_ffn",
    )(x, buf, g2, wup, cw, cb, wdown, g3)


def _gate_fold_kernel(wq_ref, wk_ref, wv_ref, gq_ref, gk_ref, gv_ref, gc_ref, gm_ref):
    dotp = functools.partial(jnp.dot, preferred_element_type=F32, precision=HI)
    gc_ref[...] = dotp(wq_ref[0], gq_ref[...]) + dotp(wk_ref[0], gk_ref[...])
    gm_ref[...] = dotp(wv_ref[0], gv_ref[...])


def _gate_fold(wq_t, wk_t, wv_t, w_gate):
    nt, w, _ = wq_t.shape
    inner = nt * w
    ng = w_gate.shape[1]
    tile = pl.BlockSpec((1, w, w), lambda i: (i, 0, 0))
    gs = [pl.BlockSpec((w, ng), lambda i, k=k: (k * nt + i, 0)) for k in range(3)]
    out = pl.BlockSpec((w, ng), lambda i: (i, 0))
    return pl.pallas_call(
        _gate_fold_kernel,
        grid=(nt,),
        in_specs=[tile, tile, tile] + gs,
        out_specs=[out, out],
        out_shape=[jax.ShapeDtypeStruct((inner, ng), F32)] * 2,
        name="mlstm_gate_fold",
    )(wq_t, wk_t, wv_t, w_gate, w_gate, w_gate)


def _mpre_kernel(x_ref, cbuf_ref, g_ref, wup_ref, cw_ref, cb_ref, wq_ref, wkt_ref, wv_ref, gc_ref, gm_ref,
                 bg_ref, q_ref, kt_ref, v_ref, xc_ref, z_ref, gates_ref, nbuf_ref, carry,
                 *, tb, inner, width):
    t = pl.program_id(1)
    pad = V7X_SUBLANES

    @pl.when(t == 0)
    def _():
        carry[...] = jnp.zeros_like(carry)
        carry[pad - (width - 1):pad, :] = cbuf_ref[0]

    h = _rms(x_ref[0], g_ref[...]).astype(BF16)
    w = V7X_MXU
    gacc = jnp.zeros((tb, bg_ref.shape[1]), F32) + bg_ref[...]
    for i in range(inner // w):
        cs = slice(i * w, (i + 1) * w)
        xm = _dot(h, wup_ref[:, cs])
        prev = carry[:, cs]
        carry[:, cs] = xm[tb - pad:tb, :]
        xc = _silu(_causal_dwconv(xm, prev, cw_ref, cb_ref, cs)).astype(BF16)
        xmb = xm.astype(BF16)
        q_ref[0, :, cs] = _dot(xc, wq_ref[i]).astype(BF16)
        kt_ref[0, cs, :] = _dot_nt(wkt_ref[i], xc).astype(BF16)
        v_ref[0, :, cs] = _dot(xmb, wv_ref[i]).astype(BF16)
        xc_ref[0, :, cs] = xc
        z_ref[0, :, cs] = _silu(_dot(h, wup_ref[:, inner + i * w:inner + (i + 1) * w])).astype(BF16)
        gacc = gacc + _dot(xc, gc_ref[cs, :]) + _dot(xmb, gm_ref[cs, :])
    gates_ref[0] = gacc
    nbuf_ref[0] = carry[pad - (width - 1):pad, :]


def _mpre(x, cbuf, g, wup, cw, cb, wq_t, wkt_t, wv_t, gc, gm, bg, tb):
    b, s, d = x.shape
    inner = cw.shape[1]
    width = cw.shape[0]
    ng = bg.shape[1]
    kern = functools.partial(_mpre_kernel, tb=tb, inner=inner, width=width)
    act = pl.BlockSpec((1, tb, inner), lambda i, t: (i, t, 0))
    blocks = _nbytes((tb, d), F32) + 5 * _nbytes((tb, inner), BF16)
    scratch = (_nbytes(wup.shape, BF16) + 3 * _nbytes(wq_t.shape, BF16) + 2 * _nbytes((inner, 128), BF16)
               + _nbytes((8, inner), F32) + 8 * _nbytes((tb, V7X_MXU), F32))
    return pl.pallas_call(
        kern,
        grid=(b, s // tb),
        in_specs=[pl.BlockSpec((1, tb, d), lambda i, t: (i, t, 0)),
                  pl.BlockSpec((1, width - 1, inner), lambda i, t: (i, 0, 0)),
                  _const_spec((1, d)), _const_spec(wup.shape), _const_spec(cw.shape), _const_spec((1, inner)),
                  _const_spec(wq_t.shape), _const_spec(wkt_t.shape), _const_spec(wv_t.shape),
                  _const_spec(gc.shape), _const_spec(gm.shape), _const_spec((1, ng))],
        out_specs=[act, pl.BlockSpec((1, inner, tb), lambda i, t: (i, 0, t)), act, act, act,
                   pl.BlockSpec((1, tb, ng), lambda i, t: (i, t, 0)),
                   pl.BlockSpec((1, width - 1, inner), lambda i, t: (i, 0, 0))],
        out_shape=[jax.ShapeDtypeStruct((b, s, inner), BF16), jax.ShapeDtypeStruct((b, inner, s), BF16),
                   jax.ShapeDtypeStruct((b, s, inner), BF16), jax.ShapeDtypeStruct((b, s, inner), BF16),
                   jax.ShapeDtypeStruct((b, s, inner), BF16), jax.ShapeDtypeStruct((b, s, ng), F32),
                   jax.ShapeDtypeStruct((b, width - 1, inner), F32)],
        scratch_shapes=[pltpu.VMEM((V7X_SUBLANES, inner), F32)],
        compiler_params=pltpu.CompilerParams(
            dimension_semantics=("arbitrary", "arbitrary"), vmem_limit_bytes=_vmem_limit(blocks, scratch)),
        name="mlstm_pre",
    )(x, cbuf, g, wup, cw, cb, wq_t, wkt_t, wv_t, gc, gm, bg)


def _mlstm_chunk(q_ref, kt_ref, v_ref, g, gt, hn_ref, ct_s, n_s, m_s, fill, *, lc, heads, dh):
    row = lax.broadcasted_iota(jnp.int32, (lc, lc), 0)
    col = lax.broadcasted_iota(jnp.int32, (lc, lc), 1)
    causal = col <= row
    tri_l = jnp.where(causal, 1.0, 0.0).astype(BF16)
    tri_u = jnp.where(row <= col, 1.0, 0.0).astype(BF16)
    b_cols = sum(_dot(tri_l, part) for part in _split3(_log_sigmoid(g)))
    b_rows = sum(_dot(part, tri_u) for part in _split3(_log_sigmoid(gt)))

    hsl = [slice(h * dh, (h + 1) * dh) for h in range(heads)]
    st = []
    for h in range(heads):
        b_col = b_cols[:, heads + h:heads + h + 1]
        b_row = b_rows[heads + h:heads + h + 1, :]
        i_row = gt[h:h + 1, :]
        m_prev = m_s[h:h + 1, 0:1]
        dmat = jnp.where(causal, b_col - b_row + i_row, -jnp.inf)
        inter = b_col + m_prev
        m_t = jnp.maximum(inter, jnp.max(dmat, axis=1, keepdims=True))
        st.append(dict(b_col=b_col, b_row=b_row, i_row=i_row, m_prev=m_prev, m_t=m_t,
                       wts=jnp.exp(dmat - m_t), a=jnp.exp(inter - m_t)))
        fill()
    for h in range(heads):
        s = st[h]
        s['sc'] = _dot(q_ref[0, :, hsl[h]], kt_ref[0, hsl[h], :]) * s['wts']
    for h in range(heads):
        s = st[h]
        qh = q_ref[0, :, hsl[h]]
        sc = s['sc']
        a = s['a']
        num = _dot(sc.astype(BF16), v_ref[0, :, hsl[h]]) + a * _dot(qh, ct_s[h].astype(BF16))
        qn = jnp.sum(qh.astype(F32) * n_s[h:h + 1, :], axis=1, keepdims=True)
        den = jnp.sum(sc, axis=1, keepdims=True) + a * qn
        hh = num / jnp.maximum(jnp.abs(den), jnp.exp(-s['m_t']))
        mu = jnp.mean(hh, axis=1, keepdims=True)
        dev = hh - mu
        var = jnp.mean(dev * dev, axis=1, keepdims=True)
        hn_ref[0, :, hsl[h]] = (dev * lax.rsqrt(var + LN_EPS)).astype(hn_ref.dtype)
        fill()
    for h in range(heads):
        s = st[h]
        kth = kt_ref[0, hsl[h], :]
        b_last = s['b_row'][:, lc - 1:lc]
        g_row = b_last - s['b_row'] + s['i_row']
        m_new = jnp.maximum(b_last + s['m_prev'], jnp.max(g_row, axis=1, keepdims=True))
        decay = jnp.exp(b_last + s['m_prev'] - m_new)
        wg_row = jnp.exp(g_row - m_new)
        wg_col = jnp.exp(b_last - s['b_col'] + g[:, h:h + 1] - m_new)
        wv = (v_ref[0, :, hsl[h]].astype(F32) * wg_col).astype(BF16)
        ct_s[h] = decay * ct_s[h] + _dot(kth, wv)
        wg16 = jnp.broadcast_to(wg_row, (16, lc)).astype(BF16)
        n_s[h:h + 1, :] = decay * n_s[h:h + 1, :] + _dot_nt(wg16, kth)[0:1, :]
        m_s[h:h + 1, :] = jnp.broadcast_to(m_new, (1, V7X_LANES))


def _mlstm_layer_kernel(*refs, lc, heads, inner, width, zero_init):
    (x_ref, cbuf_ref, g0_ref, wup_ref, cw_ref, cb_ref, wq_ref, wkt_ref, wv_ref, gc_ref, gm_ref, bg_ref,
     ng_ref, sk_ref, wdown_ref, g1_ref) = refs[:16]
    n_in = 16
    if not zero_init:
        c0_ref, n0_ref, m0_ref = refs[16:19]
        n_in = 19
    o_ref, cout_ref, nout_ref, mout_ref, nbuf_ref = refs[n_in:n_in + 5]
    carry, q_s, kt_s, v_s, xc_s, sz_s, hn_s, ct_s, n_s, m_s = refs[n_in + 5:]
    dh = inner // heads
    pad = V7X_SUBLANES
    c = pl.program_id(1)

    @pl.when(c == 0)
    def _():
        carry[...] = jnp.zeros_like(carry)
        carry[pad - (width - 1):pad, :] = cbuf_ref[0]
        if zero_init:
            ct_s[...] = jnp.zeros_like(ct_s)
            n_s[...] = jnp.zeros_like(n_s)
            m_s[...] = jnp.zeros_like(m_s)
        else:
            for h in range(heads):
                ct_s[h] = c0_ref[0, h].T
            n_s[...] = n0_ref[0]
            m_s[...] = m0_ref[0]

    x = x_ref[0]
    hx = _rms(x, g0_ref[...]).astype(BF16)
    w = V7X_MXU
    gacc = jnp.zeros((lc, bg_ref.shape[1]), F32) + bg_ref[...]
    nslab = inner // w
    col = lambda i: slice(i * w, (i + 1) * w)

    def project(i, xc, xmb, gacc):
        q_s[0, :, col(i)] = _dot(xc, wq_ref[i]).astype(BF16)
        kt_s[0, col(i), :] = _dot_nt(wkt_ref[i], xc).astype(BF16)
        v_s[0, :, col(i)] = _dot(xmb, wv_ref[i]).astype(BF16)
        xc_s[:, col(i)] = xc
        return gacc + _dot(xc, gc_ref[col(i), :]) + _dot(xmb, gm_ref[col(i), :])

    xm_next = _dot(hx, wup_ref[:, col(0)])
    pending = None
    for i in range(nslab):
        xm = xm_next
        if i + 1 < nslab:
            xm_next = _dot(hx, wup_ref[:, col(i + 1)])
        if pending is not None:
            gacc = project(*pending, gacc)
        prev = carry[:, col(i)]
        carry[:, col(i)] = xm[lc - pad:lc, :]
        xc = _silu(_causal_dwconv(xm, prev, cw_ref, cb_ref, col(i))).astype(BF16)
        pending = (i, xc, xm.astype(BF16))
    gacc = project(*pending, gacc)
    nbuf_ref[0] = carry[pad - (width - 1):pad, :]

    slabs = iter(range(inner // w))

    def gate_slab():
        i = next(slabs, None)
        if i is not None:
            cs = slice(i * w, (i + 1) * w)
            sz_s[:, cs] = _silu(_dot(hx, wup_ref[:, inner + i * w:inner + (i + 1) * w])).astype(BF16)

    _mlstm_chunk(q_s, kt_s, v_s, gacc, gacc.T[0:2 * heads, :], hn_s, ct_s, n_s, m_s, gate_slab,
                 lc=lc, heads=heads, dh=dh)
    for _ in slabs:
        raise AssertionError("fewer fill points than output-gate slabs")

    hs = hn_s[0].astype(F32) * ng_ref[...] + sk_ref[...] * xc_s[...].astype(F32)
    act = (hs * sz_s[...].astype(F32)).astype(BF16)
    o_ref[0] = x + _rms(_dot(act, wdown_ref[...]), g1_ref[...])

    @pl.when(c == pl.num_programs(1) - 1)
    def _():
        for h in range(heads):
            cout_ref[0, h] = ct_s[h].T
        nout_ref[0] = n_s[...]
        mout_ref[0] = m_s[...]


def _mlstm_layer(x, cbuf, state, p, g0, g1, lc):
    b, s, d = x.shape
    heads = p['heads']
    inner = p['m_cw'].shape[1]
    width = p['m_cw'].shape[0]
    dh = inner // heads
    zero_init = state is None
    kern = functools.partial(_mlstm_layer_kernel, lc=lc, heads=heads, inner=inner, width=width,
                             zero_init=zero_init)
    xs = pl.BlockSpec((1, lc, d), lambda i, c: (i, c, 0))
    bufs = pl.BlockSpec((1, width - 1, inner), lambda i, c: (i, 0, 0))
    cs = pl.BlockSpec((1, heads, dh, dh), lambda i, c: (i, 0, 0, 0))
    ns = pl.BlockSpec((1, heads, dh), lambda i, c: (i, 0, 0))
    ms = pl.BlockSpec((1, heads, V7X_LANES), lambda i, c: (i, 0, 0))
    consts = [g0, p['m_wup'], p['m_cw'], p['m_cb'], p['m_wq'], p['m_wkt'], p['m_wv'], p['m_gc'], p['m_gm'],
              p['m_bg'], p['m_ng'], p['m_skip'], p['m_wdown'], g1]
    in_specs = [xs, bufs] + [_const_spec(a.shape) for a in consts]
    args = [x, cbuf] + consts
    if not zero_init:
        in_specs += [cs, ns, ms]
        args += list(state)
    blocks = 2 * _nbytes((lc, d), F32) + (1 if zero_init else 2) * _nbytes((heads, dh, dh), F32)
    scratch = (sum(_nbytes(a.shape, a.dtype) for a in consts) + 6 * _nbytes((lc, inner), BF16)
               + _nbytes((heads, dh, dh), F32) + 8 * _nbytes((lc, max(lc, dh)), F32) + 2 * _nbytes((dh, dh), F32))
    return pl.pallas_call(
        kern,
        grid=(b, s // lc),
        in_specs=in_specs,
        out_specs=[xs, cs, ns, ms, bufs],
        out_shape=[jax.ShapeDtypeStruct((b, s, d), F32), jax.ShapeDtypeStruct((b, heads, dh, dh), F32),
                   jax.ShapeDtypeStruct((b, heads, dh), F32), jax.ShapeDtypeStruct((b, heads, V7X_LANES), F32),
                   jax.ShapeDtypeStruct((b, width - 1, inner), F32)],
        scratch_shapes=[pltpu.VMEM((V7X_SUBLANES, inner), F32),
                        pltpu.VMEM((1, lc, inner), BF16), pltpu.VMEM((1, inner, lc), BF16),
                        pltpu.VMEM((1, lc, inner), BF16), pltpu.VMEM((lc, inner), BF16),
                        pltpu.VMEM((lc, inner), BF16), pltpu.VMEM((1, lc, inner), BF16),
                        pltpu.VMEM((heads, dh, dh), F32), pltpu.VMEM((heads, dh), F32),
                        pltpu.VMEM((heads, V7X_LANES), F32)],
        compiler_params=pltpu.CompilerParams(
            dimension_semantics=("arbitrary", "arbitrary"), vmem_limit_bytes=_vmem_limit(blocks, scratch)),
        name="mlstm_layer",
    )(*args)


def _block_diag_tiles(w):
    nblk, k, _ = w.shape
    per = V7X_MXU // k
    wt = w.reshape(nblk // per, per, k, k)
    eye = jnp.eye(per, dtype=w.dtype)
    full = wt[:, :, :, None, :] * eye[None, :, None, :, None]
    return full.reshape(nblk // per, per * k, per * k)


def _prep_weights(w):
    row = lambda v: v.astype(F32).reshape(1, -1)
    p = {}
    p['norm_g'] = w['norm_g'].astype(F32)
    p['s5_mats'] = _s5_matrices(w['s5_A_re'][0], w['s5_A_im'][0], w['s5_log_dt'][0], w['s5_B_re'][0],
                                w['s5_B_im'][0], w['s5_C_re'][0], w['s5_C_im'][0])
    p['s5_D'] = row(w['s5_D'][0])
    p['s5_w_glu'] = w['s5_w_glu'][0].astype(BF16)
    inner = w['mlstm_conv_w'].shape[2]
    heads = w['mlstm_b_gate'].shape[1] // 2
    dh = inner // heads
    wq_t = _block_diag_tiles(w['mlstm_wq'][0].astype(F32))
    wk_t = _block_diag_tiles(w['mlstm_wk'][0].astype(F32))
    wv_t = _block_diag_tiles(w['mlstm_wv'][0].astype(F32))
    gc, gm = _gate_fold(wq_t, wk_t, wv_t, w['mlstm_w_gate'][0].astype(F32))
    p['m_wq'] = wq_t.astype(BF16)
    p['m_wkt'] = (jnp.swapaxes(wk_t, 1, 2) * (dh ** -0.5)).astype(BF16)
    p['m_wv'] = wv_t.astype(BF16)
    lane_pad = lambda a: jnp.pad(a, ((0, 0), (0, V7X_LANES - a.shape[1])))
    p['m_gc'] = lane_pad(gc).astype(BF16)
    p['m_gm'] = lane_pad(gm).astype(BF16)
    p['m_bg'] = lane_pad(row(w['mlstm_b_gate'][0]))
    p['m_wup'] = w['mlstm_w_up'][0].astype(BF16)
    p['m_cw'] = w['mlstm_conv_w'][0].astype(F32)
    p['m_cb'] = row(w['mlstm_conv_b'][0])
    p['m_ng'] = row(w['mlstm_norm_g'][0])
    p['m_skip'] = row(w['mlstm_skip'][0])
    p['m_wdown'] = w['mlstm_w_down'][0].astype(BF16)
    p['heads'] = heads
    p['f_wup'] = w['ffn_w_up'].astype(BF16)
    p['f_cw'] = w['ffn_conv_w'].astype(F32)
    p['f_cb'] = w['ffn_conv_b'].astype(F32)
    p['f_wdown'] = w['ffn_w_down'].astype(BF16)
    return p


def _to_chunk_major(u2d, b, s):
    d = u2d.shape[1]
    g = d // S5_GROUP_CH
    u = u2d.reshape(b, s // S5_CHUNK, S5_CHUNK, g, S5_GROUP_CH)
    return jnp.transpose(u, (1, 0, 3, 2, 4)).reshape((s // S5_CHUNK) * b, d * S5_CHUNK)


def _to_token_major(yt, b, s):
    d = yt.shape[1] // S5_CHUNK
    g = d // S5_GROUP_CH
    y = yt.reshape(s // S5_CHUNK, b, g, S5_CHUNK, S5_GROUP_CH)
    return jnp.transpose(y, (1, 0, 3, 2, 4)).reshape(b * s, d)


def _trunk(x, st, p, tiles):
    b, s, d = x.shape
    ng = p['norm_g']
    g_of = lambda layer, k: ng[layer, k].reshape(1, d)
    zero_init = st is None
    groups = d // S5_GROUP_CH
    x2d = x.reshape(b * s, d)

    n_state = p['s5_mats']['row'][1].shape[2] // 2
    if zero_init:
        h0re = h0im = jnp.zeros((groups // 2, b, 2 * n_state), F32)
    else:
        pair = lambda h: jnp.transpose(h.astype(F32).reshape(b, groups // 2, 2 * n_state), (1, 0, 2))
        h0re, h0im = pair(st['s5_re']), pair(st['s5_im'])
    if tiles.s5_time:
        y, hre, him = _s5_fused(x, g_of(0, 0), p['s5_mats']['col'], h0re, h0im, tiles.s5_time)
        y2d = y.reshape(b * s, d)
    else:
        u = _norm_cast(x2d, g_of(0, 0), tiles.rows)
        yt, hre, him = _s5_core(_to_chunk_major(u, b, s), p['s5_mats']['row'], h0re, h0im, b, tiles.s5_rows)
        y2d = _to_token_major(yt, b, s)
    unpair = lambda h: jnp.transpose(h, (1, 0, 2)).reshape(1, b, groups, n_state)
    o_re, o_im = unpair(hre), unpair(him)
    x2d = _glu(x2d, y2d, g_of(0, 0), p['s5_D'], p['s5_w_glu'], g_of(0, 1), tiles.rows)

    def ffn(x2d, layer):
        dff2 = p['f_wup'].shape[2]
        width = p['f_cw'].shape[1]
        buf = jnp.zeros((b, width - 1, dff2), F32) if zero_init else st['ffn_conv'][layer].astype(F32)
        xo, nbuf = _ffn(x2d.reshape(b, s, d), buf, g_of(layer, 2), p['f_wup'][layer], p['f_cw'][layer],
                        p['f_cb'][layer].reshape(1, dff2), p['f_wdown'][layer], g_of(layer, 3), tiles.ffn_time)
        return xo.reshape(b * s, d), nbuf

    x2d, fbuf0 = ffn(x2d, 0)

    heads = p['heads']
    inner = p['m_cw'].shape[1]
    width = p['m_cw'].shape[0]
    cbuf = jnp.zeros((b, width - 1, inner), F32) if zero_init else st['mlstm_conv'].astype(F32)
    if zero_init:
        state = None
    else:
        state = (st['mlstm_C'].astype(F32), st['mlstm_n'].astype(F32),
                 jnp.broadcast_to(st['mlstm_m'].astype(F32)[:, :, None], (b, heads, V7X_LANES)))
    x3, c_out, n_out, m_out, ncbuf = _mlstm_layer(x2d.reshape(b, s, d), cbuf, state, p, g_of(1, 0), g_of(1, 1),
                                                  tiles.mlstm_chunk)
    x2d, fbuf1 = ffn(x3.reshape(b * s, d), 1)

    return (x2d.reshape(b, s, d), o_re, o_im, c_out[None], n_out[None], m_out[None, :, :, 0], ncbuf[None],
            jnp.stack([fbuf0, fbuf1], axis=0))


def kernel(x_prompt, x_sample, state_s5_re, state_s5_im, state_mlstm_C, state_mlstm_n, state_mlstm_m, state_mlstm_conv, state_ffn_conv, norm_g, s5_A_re, s5_A_im, s5_log_dt, s5_B_re, s5_B_im, s5_C_re, s5_C_im, s5_D, s5_w_glu, mlstm_w_up, mlstm_conv_w, mlstm_conv_b, mlstm_wq, mlstm_wk, mlstm_wv, mlstm_w_gate, mlstm_b_gate, mlstm_norm_g, mlstm_skip, mlstm_w_down, ffn_w_up, ffn_conv_w, ffn_conv_b, ffn_w_down):
    w = {
        'norm_g': norm_g, 's5_A_re': s5_A_re, 's5_A_im': s5_A_im, 's5_log_dt': s5_log_dt,
        's5_B_re': s5_B_re, 's5_B_im': s5_B_im, 's5_C_re': s5_C_re, 's5_C_im': s5_C_im, 's5_D': s5_D,
        's5_w_glu': s5_w_glu, 'mlstm_w_up': mlstm_w_up, 'mlstm_conv_w': mlstm_conv_w,
        'mlstm_conv_b': mlstm_conv_b, 'mlstm_wq': mlstm_wq, 'mlstm_wk': mlstm_wk, 'mlstm_wv': mlstm_wv,
        'mlstm_w_gate': mlstm_w_gate, 'mlstm_b_gate': mlstm_b_gate, 'mlstm_norm_g': mlstm_norm_g,
        'mlstm_skip': mlstm_skip, 'mlstm_w_down': mlstm_w_down, 'ffn_w_up': ffn_w_up,
        'ffn_conv_w': ffn_conv_w, 'ffn_conv_b': ffn_conv_b, 'ffn_w_down': ffn_w_down,
    }
    assert norm_g.shape[0] == 2 and s5_A_re.shape[0] == 1 and mlstm_w_up.shape[0] == 1
    assert x_prompt.shape[0] == V7X_SUBLANES and x_sample.shape[0] == V7X_SUBLANES
    p = _prep_weights(w)
    out_p = _trunk(x_prompt.astype(F32), None, p, _pick_tiles(*x_prompt.shape[:2]))
    st = {'s5_re': state_s5_re[0], 's5_im': state_s5_im[0], 'mlstm_C': state_mlstm_C[0],
          'mlstm_n': state_mlstm_n[0], 'mlstm_m': state_mlstm_m[0], 'mlstm_conv': state_mlstm_conv[0],
          'ffn_conv': state_ffn_conv}
    out_s = _trunk(x_sample.astype(F32), st, p, _pick_tiles(*x_sample.shape[:2]))
    return (out_p[0], out_s[0]) + tuple(out_p[1:]) + tuple(out_s[1:])
```

```python
import functools
import math
from typing import NamedTuple

import jax
import jax.numpy as jnp
from jax import lax
from jax.experimental import pallas as pl
from jax.experimental.pallas import tpu as pltpu

F32 = jnp.float32
BF16 = jnp.bfloat16
HI = lax.Precision.HIGHEST

NORM_EPS = 1e-6
LN_EPS = 1e-5
S5_GROUP_CH = 16
S5_CHUNK = 16
QKV_BLOCK = 4

V7X_LANES = 128
V7X_SUBLANES = 8
V7X_MXU = 256
V7X_VMEM_BYTES = 64 * 1024 * 1024


class Tiles(NamedTuple):
    rows: int
    time: int
    s5_rows: int
    mlstm_chunk: int
    s5_time: int
    ffn_time: int


def _pick_tiles(batch, seq):
    s5_time = (V7X_LANES // batch) * S5_CHUNK
    return Tiles(rows=min(batch * seq, 512), time=min(seq, 512),
                 s5_rows=min(batch * (seq // S5_CHUNK), 512), mlstm_chunk=min(seq, 256),
                 s5_time=s5_time if seq % s5_time == 0 else 0, ffn_time=min(seq, 1024))


def _vmem_limit(block_bytes, scratch_bytes):
    want = 2 * block_bytes + scratch_bytes + 16 * 1024 * 1024
    return int(min(want, V7X_VMEM_BYTES - 8 * 1024 * 1024))


def _nbytes(shape, dtype):
    return math.prod(shape) * jnp.dtype(dtype).itemsize


def _rms(x, g):
    return x * lax.rsqrt(jnp.mean(x * x, axis=-1, keepdims=True) + NORM_EPS) * g


def _dot(a, b):
    return jnp.dot(a, b, preferred_element_type=F32)


def _dot_nt(a, b):
    return lax.dot_general(a, b, (((1,), (1,)), ((), ())), preferred_element_type=F32)


def _silu(x):
    return (0.5 * x) * (1.0 + jnp.tanh(0.5 * x))


def _causal_dwconv(u, prev, cw_ref, cb_ref, cs):
    width = cw_ref.shape[0]
    acc = cb_ref[:, cs] + u * cw_ref[width - 1:width, cs]
    rows = lax.broadcasted_iota(jnp.int32, prev.shape, 0)
    for j in range(width - 1):
        d = width - 1 - j
        rolled = pltpu.roll(u, d, 0)
        head = jnp.where(rows < d, pltpu.roll(prev, d, 0), rolled[0:V7X_SUBLANES])
        shifted = jnp.concatenate([head, rolled[V7X_SUBLANES:]], axis=0)
        acc = acc + shifted * cw_ref[j:j + 1, cs]
    return acc


def _log_sigmoid(x):
    return jnp.minimum(x, 0.0) - jnp.log1p(jnp.exp(-jnp.abs(x)))


def _split3(x):
    hi = x.astype(BF16)
    r1 = x - hi.astype(F32)
    mid = r1.astype(BF16)
    lo = (r1 - mid.astype(F32)).astype(BF16)
    return hi, mid, lo


def _norm_cast_kernel(x_ref, g_ref, o_ref):
    o_ref[...] = _rms(x_ref[...], g_ref[...]).astype(o_ref.dtype)


def _norm_cast(x2d, g, tb):
    rows, d = x2d.shape
    return pl.pallas_call(
        _norm_cast_kernel,
        grid=(rows // tb,),
        in_specs=[pl.BlockSpec((tb, d), lambda i: (i, 0)), pl.BlockSpec((1, d), lambda i: (0, 0))],
        out_specs=pl.BlockSpec((tb, d), lambda i: (i, 0)),
        out_shape=jax.ShapeDtypeStruct((rows, d), BF16),
        name="s5_norm",
    )(x2d, g)


def _s5_kernel(u_ref, t_ref, pre_ref, pim_ref, qre_ref, qim_ref, are_ref, aim_ref, h0re_ref, h0im_ref,
               y_ref, hre_ref, him_ref, pure_s, puim_s, hinre_s, hinim_s, stre_s, stim_s, *, nb, rb):
    r = pl.program_id(1)

    @pl.when(r == 0)
    def _():
        stre_s[...] = h0re_ref[0]
        stim_s[...] = h0im_ref[0]

    u = u_ref[...]
    pure_s[...] = _dot(u, pre_ref[0])
    puim_s[...] = _dot(u, pim_ref[0])
    ar = jnp.broadcast_to(are_ref[0], (nb, 2 * 64))
    ai = jnp.broadcast_to(aim_ref[0], (nb, 2 * 64))

    def step(i, carry):
        re, im = carry
        rows = pl.ds(pl.multiple_of(i * nb, nb), nb)
        hinre_s[rows, :] = re
        hinim_s[rows, :] = im
        return (ar * re - ai * im + pure_s[rows, :], ar * im + ai * re + puim_s[rows, :])

    re, im = lax.fori_loop(0, rb // nb, step, (stre_s[...], stim_s[...]))
    stre_s[...] = re
    stim_s[...] = im
    hre_ref[0] = re
    him_ref[0] = im

    carry_in = _dot(hinre_s[...].astype(BF16), qre_ref[0]) + _dot(hinim_s[...].astype(BF16), qim_ref[0])
    w = V7X_MXU
    for k in range(2):
        y_ref[:, k * w:(k + 1) * w] = (_dot(u[:, k * w:(k + 1) * w], t_ref[k])
                                       + carry_in[:, k * w:(k + 1) * w]).astype(y_ref.dtype)


def _s5_core(ut, mats, h0re, h0im, nb, rb):
    rows, cols = ut.shape
    pairs = cols // (2 * V7X_MXU)
    tm, pre, pim, qre, qim, are, aim = mats
    pair3 = lambda p, r: (p, 0, 0)
    kern = functools.partial(_s5_kernel, nb=nb, rb=rb)
    blocks = (_nbytes((rb, 512), BF16) * 2 + _nbytes((2, 256, 256), BF16) + 4 * _nbytes((512, 128), BF16))
    scratch = 4 * _nbytes((rb, 128), F32) + 2 * _nbytes((nb, 128), F32)
    return pl.pallas_call(
        kern,
        grid=(pairs, rows // rb),
        in_specs=[
            pl.BlockSpec((rb, 2 * V7X_MXU), lambda p, r: (r, p)),
            pl.BlockSpec((2, V7X_MXU, V7X_MXU), pair3),
            pl.BlockSpec((1, 2 * V7X_MXU, 128), pair3),
            pl.BlockSpec((1, 2 * V7X_MXU, 128), pair3),
            pl.BlockSpec((1, 128, 2 * V7X_MXU), pair3),
            pl.BlockSpec((1, 128, 2 * V7X_MXU), pair3),
            pl.BlockSpec((1, 1, 128), pair3),
            pl.BlockSpec((1, 1, 128), pair3),
            pl.BlockSpec((1, nb, 128), pair3),
            pl.BlockSpec((1, nb, 128), pair3),
        ],
        out_specs=[
            pl.BlockSpec((rb, 2 * V7X_MXU), lambda p, r: (r, p)),
            pl.BlockSpec((1, nb, 128), pair3),
            pl.BlockSpec((1, nb, 128), pair3),
        ],
        out_shape=[
            jax.ShapeDtypeStruct((rows, cols), BF16),
            jax.ShapeDtypeStruct((pairs, nb, 128), F32),
            jax.ShapeDtypeStruct((pairs, nb, 128), F32),
        ],
        scratch_shapes=[pltpu.VMEM((rb, 128), F32)] * 4 + [pltpu.VMEM((nb, 128), F32)] * 2,
        compiler_params=pltpu.CompilerParams(
            dimension_semantics=("arbitrary", "arbitrary"), vmem_limit_bytes=_vmem_limit(blocks, scratch)),
        name="s5_core",
    )(ut, tm, pre, pim, qre, qim, are, aim, h0re, h0im)


def _s5_fused_kernel(x_ref, g_ref, tt_ref, pt_ref, qt_ref, are_ref, aim_ref, h0re_ref, h0im_ref,
                     y_ref, hre_ref, him_ref,
                     us, ugt, pure, puim, hinre, hinim, ytmp, ys, stre, stim, *, nb, tbt):
    lc = S5_CHUNK
    ch = S5_GROUP_CH
    gps = V7X_LANES // ch
    nch = tbt // lc
    cols = nb * nch
    k = pl.program_id(1)

    @pl.when(k == 0)
    def _():
        for b in range(nb):
            u = _rms(x_ref[b], g_ref[...])
            for kk in range(us.shape[0]):
                us[kk, b * tbt:(b + 1) * tbt, :] = u[:, kk * V7X_LANES:(kk + 1) * V7X_LANES]

    @pl.when(pl.program_id(0) == 0)
    def _():
        stre[k] = h0re_ref[...]
        stim[k] = h0im_ref[...]

    for s in range(lc):
        a = us[k, pl.ds(s, cols, stride=lc), :].T
        for g in range(gps):
            ugt[g, s * ch:(s + 1) * ch, :] = a[g * ch:(g + 1) * ch, :].astype(BF16)

    half = pt_ref.shape[1] // 2
    for pr in range(gps // 2):
        put0 = _dot(pt_ref[2 * pr], ugt[2 * pr])
        put1 = _dot(pt_ref[2 * pr + 1], ugt[2 * pr + 1])
        pure[pr] = jnp.concatenate([put0[:half], put1[:half]], axis=0).T
        puim[pr] = jnp.concatenate([put0[half:], put1[half:]], axis=0).T

    for pr in range(gps // 2):
        re = stre[k, pr]
        im = stim[k, pr]
        ar = jnp.broadcast_to(are_ref[pr], re.shape)
        ai = jnp.broadcast_to(aim_ref[pr], re.shape)
        for j in range(nch):
            rows = pl.ds(j, nb, stride=nch)
            hinre[pr, rows, :] = re
            hinim[pr, rows, :] = im
            re, im = (ar * re - ai * im + pure[pr, rows, :], ar * im + ai * re + puim[pr, rows, :])
        stre[k, pr] = re
        stim[k, pr] = im
        hre_ref[k * (gps // 2) + pr] = re
        him_ref[k * (gps // 2) + pr] = im

    for pr in range(gps // 2):
        hre_t = hinre[pr].T
        him_t = hinim[pr].T
        for i in range(2):
            g = 2 * pr + i
            hin_t = jnp.concatenate([hre_t[i * half:(i + 1) * half], him_t[i * half:(i + 1) * half]],
                                    axis=0).astype(BF16)
            ytmp[g] = _dot(tt_ref[g], ugt[g]) + _dot(qt_ref[g], hin_t)

    for t in range(lc):
        z = jnp.concatenate([ytmp[g, t * ch:(t + 1) * ch, :] for g in range(gps)], axis=0)
        ys[pl.ds(t, cols, stride=lc), :] = z.T
    for b in range(nb):
        y_ref[b] = ys[b * tbt:(b + 1) * tbt, :].astype(y_ref.dtype)


def _s5_fused(x, g, mats_t, h0re, h0im, tbt):
    nb, s, d = x.shape
    tt, pt, qt, are, aim = mats_t
    groups = tt.shape[0]
    gps = V7X_LANES // S5_GROUP_CH
    slabs = groups // gps
    w = S5_CHUNK * S5_GROUP_CH
    nst = pt.shape[1]
    cols = nb * (tbt // S5_CHUNK)
    kern = functools.partial(_s5_fused_kernel, nb=nb, tbt=tbt)
    slab3 = lambda t, k: (k, 0, 0)
    blocks = (_nbytes((nb, tbt, d), F32) + _nbytes((nb, tbt, V7X_LANES), BF16)
              + gps * (_nbytes((w, w), BF16) + 2 * _nbytes((w, nst), BF16)))
    scratch = (_nbytes((slabs, nb * tbt, V7X_LANES), F32) + _nbytes((gps, w, cols), BF16)
               + 4 * _nbytes((gps // 2, cols, nst), F32) + _nbytes((gps, w, cols), F32)
               + _nbytes((nb * tbt, V7X_LANES), F32))
    return pl.pallas_call(
        kern,
        grid=(s // tbt, slabs),
        in_specs=[
            pl.BlockSpec((nb, tbt, d), lambda t, k: (0, t, 0)),
            pl.BlockSpec((1, d), lambda t, k: (0, 0)),
            pl.BlockSpec((gps, w, w), slab3),
            pl.BlockSpec((gps, nst, w), slab3),
            pl.BlockSpec((gps, w, nst), slab3),
            pl.BlockSpec((gps // 2, 1, nst), slab3),
            pl.BlockSpec((gps // 2, 1, nst), slab3),
            pl.BlockSpec((gps // 2, nb, nst), slab3),
            pl.BlockSpec((gps // 2, nb, nst), slab3),
        ],
        out_specs=[
            pl.BlockSpec((nb, tbt, V7X_LANES), lambda t, k: (0, t, k)),
            pl.BlockSpec((groups // 2, nb, nst), lambda t, k: (0, 0, 0)),
            pl.BlockSpec((groups // 2, nb, nst), lambda t, k: (0, 0, 0)),
        ],
        out_shape=[
            jax.ShapeDtypeStruct((nb, s, d), BF16),
            jax.ShapeDtypeStruct((groups // 2, nb, nst), F32),
            jax.ShapeDtypeStruct((groups // 2, nb, nst), F32),
        ],
        scratch_shapes=[
            pltpu.VMEM((slabs, nb * tbt, V7X_LANES), F32),
            pltpu.VMEM((gps, w, cols), BF16),
            pltpu.VMEM((gps // 2, cols, nst), F32),
            pltpu.VMEM((gps // 2, cols, nst), F32),
            pltpu.VMEM((gps // 2, cols, nst), F32),
            pltpu.VMEM((gps // 2, cols, nst), F32),
            pltpu.VMEM((gps, w, cols), F32),
            pltpu.VMEM((nb * tbt, V7X_LANES), F32),
            pltpu.VMEM((slabs, gps // 2, nb, nst), F32),
            pltpu.VMEM((slabs, gps // 2, nb, nst), F32),
        ],
        compiler_params=pltpu.CompilerParams(
            dimension_semantics=("arbitrary", "arbitrary"), vmem_limit_bytes=_vmem_limit(blocks, scratch)),
        name="s5_fused",
    )(x, g, tt, pt, qt, are, aim, h0re, h0im)


def _s5_matrices(a_re, a_im, log_dt, b_re, b_im, c_re, c_im):
    lc = S5_CHUNK
    groups, n = a_re.shape
    ch = S5_GROUP_CH
    a = lax.complex(a_re.astype(F32), a_im.astype(F32))
    adt = a * jnp.exp(log_dt.astype(F32))[:, None]
    j = jnp.arange(lc + 1, dtype=F32)
    apow = jnp.exp(adt[None] * j[:, None, None])
    b_bar = ((apow[1] - 1.0) / a)[..., None] * lax.complex(b_re.astype(F32), b_im.astype(F32))
    c_mat = lax.complex(c_re.astype(F32), c_im.astype(F32))
    ca = c_mat[None] * apow[:, :, None, :]
    kern = (jnp.einsum('jgcn,gnd->jgcd', ca.real, b_bar.real, precision=HI)
            - jnp.einsum('jgcn,gnd->jgcd', ca.imag, b_bar.imag, precision=HI))
    s_idx = jnp.arange(lc)[:, None]
    t_idx = jnp.arange(lc)[None, :]
    lag = t_idx - s_idx
    kt = kern[jnp.clip(lag, 0, lc)]
    kt = jnp.where((lag >= 0)[:, :, None, None, None], kt, 0.0)
    tm = jnp.transpose(kt, (2, 0, 4, 1, 3)).reshape(groups, lc * ch, lc * ch)
    pw = apow[lc - 1 - jnp.arange(lc)]
    p = pw[:, :, :, None] * b_bar[None]
    p = jnp.transpose(p, (1, 0, 3, 2)).reshape(groups, lc * ch, n)
    q = ca[1:]
    q = jnp.transpose(q, (1, 3, 0, 2)).reshape(groups, n, lc * ch)

    def pair_rows(m):
        g2 = groups // 2
        m = m.reshape(g2, 2, m.shape[1], m.shape[2])
        z = jnp.zeros_like(m[:, 0])
        top = jnp.concatenate([m[:, 0], z], axis=2)
        bot = jnp.concatenate([z, m[:, 1]], axis=2)
        return jnp.concatenate([top, bot], axis=1)

    pre = pair_rows(p.real).astype(BF16)
    pim = pair_rows(p.imag).astype(BF16)
    qre = pair_rows(q.real).astype(BF16)
    qim = pair_rows(-q.imag).astype(BF16)
    a_chunk = apow[lc].reshape(groups // 2, 1, 2 * n)
    are, aim = a_chunk.real, a_chunk.imag
    tt = jnp.swapaxes(tm, 1, 2).astype(BF16)
    pt = jnp.concatenate([jnp.swapaxes(p.real, 1, 2), jnp.swapaxes(p.imag, 1, 2)], axis=1).astype(BF16)
    qt = jnp.concatenate([jnp.swapaxes(q.real, 1, 2), jnp.swapaxes(-q.imag, 1, 2)], axis=2).astype(BF16)
    return {'row': (tm.astype(BF16), pre, pim, qre, qim, are, aim), 'col': (tt, pt, qt, are, aim)}


def _glu_kernel(x_ref, y_ref, g0_ref, d_ref, w_ref, g1_ref, o_ref, out_s):
    x = x_ref[...]
    d = x.shape[1]
    u = _rms(x, g0_ref[...])
    yy = y_ref[...].astype(F32) + d_ref[...] * u
    z = jax.nn.gelu(yy, approximate=True).astype(BF16)
    w = V7X_MXU
    nslab = d // w
    pair = lambda j: (_dot(z, w_ref[:, j * w:(j + 1) * w]), _dot(z, w_ref[:, d + j * w:d + (j + 1) * w]))
    nxt = pair(0)
    ssq = jnp.zeros((x.shape[0], 1), F32)
    for j in range(nslab):
        a, b = nxt
        if j + 1 < nslab:
            nxt = pair(j + 1)
        out = a * (0.5 * (1.0 + jnp.tanh(0.5 * b)))
        out_s[:, j * w:(j + 1) * w] = out
        ssq = ssq + jnp.sum(out * out, axis=-1, keepdims=True)
    scale = lax.rsqrt(ssq * (1.0 / d) + NORM_EPS)
    o_ref[...] = x + out_s[...] * scale * g1_ref[...]


def _const_spec(shape):
    nd = len(shape)
    return pl.BlockSpec(shape, lambda *_: (0,) * nd, pipeline_mode=pl.Buffered(1))


def _glu(x2d, y2d, g0, dskip, w, g1, tb):
    rows, d = x2d.shape
    blocks = 2 * _nbytes((tb, d), F32) + _nbytes((tb, d), BF16)
    scratch = _nbytes(w.shape, BF16) + 3 * _nbytes((tb, 2 * d), F32)
    row = pl.BlockSpec((tb, d), lambda i: (i, 0))
    return pl.pallas_call(
        _glu_kernel,
        grid=(rows // tb,),
        in_specs=[row, row, _const_spec((1, d)), _const_spec((1, d)), _const_spec(w.shape), _const_spec((1, d))],
        out_specs=row,
        out_shape=jax.ShapeDtypeStruct((rows, d), F32),
        scratch_shapes=[pltpu.VMEM((tb, d), F32)],
        compiler_params=pltpu.CompilerParams(
            dimension_semantics=("arbitrary",), vmem_limit_bytes=_vmem_limit(blocks, scratch)),
        name="s5_glu",
    )(x2d, y2d, g0, dskip, w, g1)


def _ffn_kernel(x_ref, buf_ref, g2_ref, wup_ref, cw_ref, cb_ref, wdown_ref, g3_ref,
                o_ref, nbuf_ref, carry, act, *, tb, dff, width):
    t = pl.program_id(1)
    pad = V7X_SUBLANES

    @pl.when(t == 0)
    def _():
        carry[...] = jnp.zeros_like(carry)
        carry[pad - (width - 1):pad, :] = buf_ref[0]

    x = x_ref[0]
    hn = _rms(x, g2_ref[...]).astype(BF16)
    fb = V7X_MXU

    def conv(u, cs):
        prev = carry[:, cs]
        carry[:, cs] = u[tb - pad:tb, :]
        return _causal_dwconv(u, prev, cw_ref, cb_ref, cs)

    nslab = dff // fb
    cols = lambda j: (slice(j * fb, (j + 1) * fb), slice(dff + j * fb, dff + (j + 1) * fb))
    up = lambda j: tuple(_dot(hn, wup_ref[:, cs]) for cs in cols(j))
    nxt = up(0)
    for j in range(nslab):
        ug, uv = nxt
        if j + 1 < nslab:
            nxt = up(j + 1)
        gate = conv(ug, cols(j)[0])
        val = conv(uv, cols(j)[1])
        act[:, j * fb:(j + 1) * fb] = (jax.nn.gelu(gate, approximate=True) * val).astype(BF16)

    f = _dot(act[...], wdown_ref[...])
    o_ref[0] = x + _rms(f, g3_ref[...])
    nbuf_ref[0] = carry[pad - (width - 1):pad, :]


def _ffn(x, buf, g2, wup, cw, cb, wdown, g3, tb):
    b, s, d = x.shape
    dff = wdown.shape[0]
    width = cw.shape[0]
    kern = functools.partial(_ffn_kernel, tb=tb, dff=dff, width=width)
    blocks = 2 * _nbytes((tb, d), F32)
    scratch = (_nbytes(wup.shape, BF16) + _nbytes(wdown.shape, BF16) + 6 * _nbytes((tb, V7X_MXU), F32)
               + _nbytes((8, 2 * dff), F32) + _nbytes((tb, dff), BF16) + 2 * _nbytes((tb, d), F32))
    xs = pl.BlockSpec((1, tb, d), lambda i, t: (i, t, 0))
    bs = pl.BlockSpec((1, width - 1, 2 * dff), lambda i, t: (i, 0, 0))
    return pl.pallas_call(
        kern,
        grid=(b, s // tb),
        in_specs=[xs, bs, _const_spec((1, d)), _const_spec(wup.shape), _const_spec(cw.shape),
                  _const_spec((1, 2 * dff)), _const_spec(wdown.shape), _const_spec((1, d))],
        out_specs=[xs, bs],
        out_shape=[jax.ShapeDtypeStruct((b, s, d), F32), jax.ShapeDtypeStruct((b, width - 1, 2 * dff), F32)],
        scratch_shapes=[pltpu.VMEM((V7X_SUBLANES, 2 * dff), F32), pltpu.VMEM((tb, dff), BF16)],
        compiler_params=pltpu.CompilerParams(
            dimension_semantics=("arbitrary", "arbitrary"), vmem_limit_bytes=_vmem_limit(blocks, scratch)),
        name="conv_ffn",
    )(x, buf, g2, wup, cw, cb, wdown, g3)


def _gate_fold_kernel(wq_ref, wk_ref, wv_ref, gq_ref, gk_ref, gv_ref, gc_ref, gm_ref):
    dotp = functools.partial(jnp.dot, preferred_element_type=F32, precision=HI)
    gc_ref[...] = dotp(wq_ref[0], gq_ref[...]) + dotp(wk_ref[0], gk_ref[...])
    gm_ref[...] = dotp(wv_ref[0], gv_ref[...])


def _gate_fold(wq_t, wk_t, wv_t, w_gate):
    nt, w, _ = wq_t.shape
    inner = nt * w
    ng = w_gate.shape[1]
    tile = pl.BlockSpec((1, w, w), lambda i: (i, 0, 0))
    gs = [pl.BlockSpec((w, ng), lambda i, k=k: (k * nt + i, 0)) for k in range(3)]
    out = pl.BlockSpec((w, ng), lambda i: (i, 0))
    return pl.pallas_call(
        _gate_fold_kernel,
        grid=(nt,),
        in_specs=[tile, tile, tile] + gs,
        out_specs=[out, out],
        out_shape=[jax.ShapeDtypeStruct((inner, ng), F32)] * 2,
        name="mlstm_gate_fold",
    )(wq_t, wk_t, wv_t, w_gate, w_gate, w_gate)


def _mpre_kernel(x_ref, cbuf_ref, g_ref, wup_ref, cw_ref, cb_ref, wq_ref, wkt_ref, wv_ref, gc_ref, gm_ref,
                 bg_ref, q_ref, kt_ref, v_ref, xc_ref, z_ref, gates_ref, nbuf_ref, carry,
                 *, tb, inner, width):
    t = pl.program_id(1)
    pad = V7X_SUBLANES

    @pl.when(t == 0)
    def _():
        carry[...] = jnp.zeros_like(carry)
        carry[pad - (width - 1):pad, :] = cbuf_ref[0]

    h = _rms(x_ref[0], g_ref[...]).astype(BF16)
    w = V7X_MXU
    gacc = jnp.zeros((tb, bg_ref.shape[1]), F32) + bg_ref[...]
    for i in range(inner // w):
        cs = slice(i * w, (i + 1) * w)
        xm = _dot(h, wup_ref[:, cs])
        prev = carry[:, cs]
        carry[:, cs] = xm[tb - pad:tb, :]
        xc = _silu(_causal_dwconv(xm, prev, cw_ref, cb_ref, cs)).astype(BF16)
        xmb = xm.astype(BF16)
        q_ref[0, :, cs] = _dot(xc, wq_ref[i]).astype(BF16)
        kt_ref[0, cs, :] = _dot_nt(wkt_ref[i], xc).astype(BF16)
        v_ref[0, :, cs] = _dot(xmb, wv_ref[i]).astype(BF16)
        xc_ref[0, :, cs] = xc
        z_ref[0, :, cs] = _silu(_dot(h, wup_ref[:, inner + i * w:inner + (i + 1) * w])).astype(BF16)
        gacc = gacc + _dot(xc, gc_ref[cs, :]) + _dot(xmb, gm_ref[cs, :])
    gates_ref[0] = gacc
    nbuf_ref[0] = carry[pad - (width - 1):pad, :]


def _mpre(x, cbuf, g, wup, cw, cb, wq_t, wkt_t, wv_t, gc, gm, bg, tb):
    b, s, d = x.shape
    inner = cw.shape[1]
    width = cw.shape[0]
    ng = bg.shape[1]
    kern = functools.partial(_mpre_kernel, tb=tb, inner=inner, width=width)
    act = pl.BlockSpec((1, tb, inner), lambda i, t: (i, t, 0))
    blocks = _nbytes((tb, d), F32) + 5 * _nbytes((tb, inner), BF16)
    scratch = (_nbytes(wup.shape, BF16) + 3 * _nbytes(wq_t.shape, BF16) + 2 * _nbytes((inner, 128), BF16)
               + _nbytes((8, inner), F32) + 8 * _nbytes((tb, V7X_MXU), F32))
    return pl.pallas_call(
        kern,
        grid=(b, s // tb),
        in_specs=[pl.BlockSpec((1, tb, d), lambda i, t: (i, t, 0)),
                  pl.BlockSpec((1, width - 1, inner), lambda i, t: (i, 0, 0)),
                  _const_spec((1, d)), _const_spec(wup.shape), _const_spec(cw.shape), _const_spec((1, inner)),
                  _const_spec(wq_t.shape), _const_spec(wkt_t.shape), _const_spec(wv_t.shape),
                  _const_spec(gc.shape), _const_spec(gm.shape), _const_spec((1, ng))],
        out_specs=[act, pl.BlockSpec((1, inner, tb), lambda i, t: (i, 0, t)), act, act, act,
                   pl.BlockSpec((1, tb, ng), lambda i, t: (i, t, 0)),
                   pl.BlockSpec((1, width - 1, inner), lambda i, t: (i, 0, 0))],
        out_shape=[jax.ShapeDtypeStruct((b, s, inner), BF16), jax.ShapeDtypeStruct((b, inner, s), BF16),
                   jax.ShapeDtypeStruct((b, s, inner), BF16), jax.ShapeDtypeStruct((b, s, inner), BF16),
                   jax.ShapeDtypeStruct((b, s, inner), BF16), jax.ShapeDtypeStruct((b, s, ng), F32),
                   jax.ShapeDtypeStruct((b, width - 1, inner), F32)],
        scratch_shapes=[pltpu.VMEM((V7X_SUBLANES, inner), F32)],
        compiler_params=pltpu.CompilerParams(
            dimension_semantics=("arbitrary", "arbitrary"), vmem_limit_bytes=_vmem_limit(blocks, scratch)),
        name="mlstm_pre",
    )(x, cbuf, g, wup, cw, cb, wq_t, wkt_t, wv_t, gc, gm, bg)


def _mlstm_chunk(q_ref, kt_ref, v_ref, g, gt, ct_s, n_s, m_s, fill, emit_hn, fill_update, *, lc, heads, dh):
    row = lax.broadcasted_iota(jnp.int32, (lc, lc), 0)
    col = lax.broadcasted_iota(jnp.int32, (lc, lc), 1)
    causal = col <= row
    tri_l = jnp.where(causal, 1.0, 0.0).astype(BF16)
    tri_u = jnp.where(row <= col, 1.0, 0.0).astype(BF16)
    b_cols = sum(_dot(tri_l, part) for part in _split3(_log_sigmoid(g)))
    b_rows = sum(_dot(part, tri_u) for part in _split3(_log_sigmoid(gt)))

    hsl = [slice(h * dh, (h + 1) * dh) for h in range(heads)]
    st = []
    for h in range(heads):
        b_col = b_cols[:, heads + h:heads + h + 1]
        b_row = b_rows[heads + h:heads + h + 1, :]
        i_row = gt[h:h + 1, :]
        m_prev = m_s[h:h + 1, 0:1]
        dmat = jnp.where(causal, b_col - b_row + i_row, -jnp.inf)
        inter = b_col + m_prev
        m_t = jnp.maximum(inter, jnp.max(dmat, axis=1, keepdims=True))
        st.append(dict(b_col=b_col, b_row=b_row, i_row=i_row, m_prev=m_prev, m_t=m_t,
                       wts=jnp.exp(dmat - m_t), a=jnp.exp(inter - m_t)))
        fill()
    for h in range(heads):
        s = st[h]
        s['sc'] = _dot(q_ref[0, :, hsl[h]], kt_ref[0, hsl[h], :]) * s['wts']
    for h in range(heads):
        fill()
        s = st[h]
        qh = q_ref[0, :, hsl[h]]
        sc = s['sc']
        a = s['a']
        num = _dot(sc.astype(BF16), v_ref[0, :, hsl[h]]) + a * _dot(qh, ct_s[h].astype(BF16))
        qn = jnp.sum(qh.astype(F32) * n_s[h:h + 1, :], axis=1, keepdims=True)
        den = jnp.sum(sc, axis=1, keepdims=True) + a * qn
        hh = num / jnp.maximum(jnp.abs(den), jnp.exp(-s['m_t']))
        mu = jnp.mean(hh, axis=1, keepdims=True)
        dev = hh - mu
        var = jnp.mean(dev * dev, axis=1, keepdims=True)
        emit_hn(h, dev * lax.rsqrt(var + LN_EPS))
    for h in range(heads):
        fill_update(h)
        s = st[h]
        kth = kt_ref[0, hsl[h], :]
        b_last = s['b_row'][:, lc - 1:lc]
        g_row = b_last - s['b_row'] + s['i_row']
        m_new = jnp.maximum(b_last + s['m_prev'], jnp.max(g_row, axis=1, keepdims=True))
        decay = jnp.exp(b_last + s['m_prev'] - m_new)
        wg_row = jnp.exp(g_row - m_new)
        wg_col = jnp.exp(b_last - s['b_col'] + g[:, h:h + 1] - m_new)
        wv = (v_ref[0, :, hsl[h]].astype(F32) * wg_col).astype(BF16)
        ct_s[h] = decay * ct_s[h] + _dot(kth, wv)
        wg16 = jnp.broadcast_to(wg_row, (16, lc)).astype(BF16)
        n_s[h:h + 1, :] = decay * n_s[h:h + 1, :] + _dot_nt(wg16, kth)[0:1, :]
        m_s[h:h + 1, :] = jnp.broadcast_to(m_new, (1, V7X_LANES))


def _mlstm_layer_kernel(*refs, lc, heads, inner, width, zero_init):
    (x_ref, cbuf_ref, g0_ref, wup_ref, cw_ref, cb_ref, wq_ref, wkt_ref, wv_ref, gc_ref, gm_ref, bg_ref,
     ng_ref, sk_ref, wdown_ref, g1_ref) = refs[:16]
    n_in = 16
    if not zero_init:
        c0_ref, n0_ref, m0_ref = refs[16:19]
        n_in = 19
    o_ref, cout_ref, nout_ref, mout_ref, nbuf_ref = refs[n_in:n_in + 5]
    carry, q_s, kt_s, v_s, xc_s, sz_s, act_s, ct_s, n_s, m_s = refs[n_in + 5:]
    dh = inner // heads
    pad = V7X_SUBLANES
    c = pl.program_id(1)

    @pl.when(c == 0)
    def _():
        carry[...] = jnp.zeros_like(carry)
        carry[pad - (width - 1):pad, :] = cbuf_ref[0]
        if zero_init:
            ct_s[...] = jnp.zeros_like(ct_s)
            n_s[...] = jnp.zeros_like(n_s)
            m_s[...] = jnp.zeros_like(m_s)
        else:
            for h in range(heads):
                ct_s[h] = c0_ref[0, h].T
            n_s[...] = n0_ref[0]
            m_s[...] = m0_ref[0]

    x = x_ref[0]
    hx = _rms(x, g0_ref[...]).astype(BF16)
    w = V7X_MXU
    gacc = jnp.zeros((lc, bg_ref.shape[1]), F32) + bg_ref[...]
    nslab = inner // w
    col = lambda i: slice(i * w, (i + 1) * w)

    def project(i, xc, xmb, gacc):
        q_s[0, :, col(i)] = _dot(xc, wq_ref[i]).astype(BF16)
        kt_s[0, col(i), :] = _dot_nt(wkt_ref[i], xc).astype(BF16)
        v_s[0, :, col(i)] = _dot(xmb, wv_ref[i]).astype(BF16)
        xc_s[:, col(i)] = xc
        return gacc + _dot(xc, gc_ref[col(i), :]) + _dot(xmb, gm_ref[col(i), :])

    xm_next = _dot(hx, wup_ref[:, col(0)])
    pending = None
    for i in range(nslab):
        xm = xm_next
        if i + 1 < nslab:
            xm_next = _dot(hx, wup_ref[:, col(i + 1)])
        if pending is not None:
            gacc = project(*pending, gacc)
        prev = carry[:, col(i)]
        carry[:, col(i)] = xm[lc - pad:lc, :]
        xc = _silu(_causal_dwconv(xm, prev, cw_ref, cb_ref, col(i))).astype(BF16)
        pending = (i, xc, xm.astype(BF16))
    gacc = project(*pending, gacc)
    nbuf_ref[0] = carry[pad - (width - 1):pad, :]

    slabs = iter(range(inner // w))

    def gate_slab():
        i = next(slabs, None)
        if i is not None:
            cs = slice(i * w, (i + 1) * w)
            sz_s[:, cs] = _silu(_dot(hx, wup_ref[:, inner + i * w:inner + (i + 1) * w])).astype(BF16)

    hcols = lambda h: slice(h * dh, (h + 1) * dh)
    down_parts = []

    def gated_out(h, hn):
        hs = hn * ng_ref[:, hcols(h)] + sk_ref[:, hcols(h)] * xc_s[:, hcols(h)].astype(F32)
        act_s[:, hcols(h)] = (hs * sz_s[:, hcols(h)].astype(F32)).astype(BF16)

    def down_part(h):
        down_parts.append(_dot(act_s[:, hcols(h)], wdown_ref[hcols(h), :]))

    _mlstm_chunk(q_s, kt_s, v_s, gacc, gacc.T[0:2 * heads, :], ct_s, n_s, m_s, gate_slab, gated_out, down_part,
                 lc=lc, heads=heads, dh=dh)
    for _ in slabs:
        raise AssertionError("fewer fill points than output-gate slabs")
    o_ref[0] = x + _rms(sum(down_parts), g1_ref[...])

    @pl.when(c == pl.num_programs(1) - 1)
    def _():
        for h in range(heads):
            cout_ref[0, h] = ct_s[h].T
        nout_ref[0] = n_s[...]
        mout_ref[0] = m_s[...]


def _mlstm_layer(x, cbuf, state, p, g0, g1, lc):
    b, s, d = x.shape
    heads = p['heads']
    inner = p['m_cw'].shape[1]
    width = p['m_cw'].shape[0]
    dh = inner // heads
    zero_init = state is None
    kern = functools.partial(_mlstm_layer_kernel, lc=lc, heads=heads, inner=inner, width=width,
                             zero_init=zero_init)
    xs = pl.BlockSpec((1, lc, d), lambda i, c: (i, c, 0))
    bufs = pl.BlockSpec((1, width - 1, inner), lambda i, c: (i, 0, 0))
    cs = pl.BlockSpec((1, heads, dh, dh), lambda i, c: (i, 0, 0, 0))
    ns = pl.BlockSpec((1, heads, dh), lambda i, c: (i, 0, 0))
    ms = pl.BlockSpec((1, heads, V7X_LANES), lambda i, c: (i, 0, 0))
    consts = [g0, p['m_wup'], p['m_cw'], p['m_cb'], p['m_wq'], p['m_wkt'], p['m_wv'], p['m_gc'], p['m_gm'],
              p['m_bg'], p['m_ng'], p['m_skip'], p['m_wdown'], g1]
    in_specs = [xs, bufs] + [_const_spec(a.shape) for a in consts]
    args = [x, cbuf] + consts
    if not zero_init:
        in_specs += [cs, ns, ms]
        args += list(state)
    blocks = 2 * _nbytes((lc, d), F32) + (1 if zero_init else 2) * _nbytes((heads, dh, dh), F32)
    scratch = (sum(_nbytes(a.shape, a.dtype) for a in consts) + 6 * _nbytes((lc, inner), BF16)
               + _nbytes((heads, dh, dh), F32) + 8 * _nbytes((lc, max(lc, dh)), F32) + 2 * _nbytes((dh, dh), F32))
    return pl.pallas_call(
        kern,
        grid=(b, s // lc),
        in_specs=in_specs,
        out_specs=[xs, cs, ns, ms, bufs],
        out_shape=[jax.ShapeDtypeStruct((b, s, d), F32), jax.ShapeDtypeStruct((b, heads, dh, dh), F32),
                   jax.ShapeDtypeStruct((b, heads, dh), F32), jax.ShapeDtypeStruct((b, heads, V7X_LANES), F32),
                   jax.ShapeDtypeStruct((b, width - 1, inner), F32)],
        scratch_shapes=[pltpu.VMEM((V7X_SUBLANES, inner), F32),
                        pltpu.VMEM((1, lc, inner), BF16), pltpu.VMEM((1, inner, lc), BF16),
                        pltpu.VMEM((1, lc, inner), BF16), pltpu.VMEM((lc, inner), BF16),
                        pltpu.VMEM((lc, inner), BF16), pltpu.VMEM((lc, inner), BF16),
                        pltpu.VMEM((heads, dh, dh), F32), pltpu.VMEM((heads, dh), F32),
                        pltpu.VMEM((heads, V7X_LANES), F32)],
        compiler_params=pltpu.CompilerParams(
            dimension_semantics=("arbitrary", "arbitrary"), vmem_limit_bytes=_vmem_limit(blocks, scratch)),
        name="mlstm_layer",
    )(*args)


def _block_diag_tiles(w):
    nblk, k, _ = w.shape
    per = V7X_MXU // k
    wt = w.reshape(nblk // per, per, k, k)
    eye = jnp.eye(per, dtype=w.dtype)
    full = wt[:, :, :, None, :] * eye[None, :, None, :, None]
    return full.reshape(nblk // per, per * k, per * k)


def _prep_weights(w):
    row = lambda v: v.astype(F32).reshape(1, -1)
    p = {}
    p['norm_g'] = w['norm_g'].astype(F32)
    p['s5_mats'] = _s5_matrices(w['s5_A_re'][0], w['s5_A_im'][0], w['s5_log_dt'][0], w['s5_B_re'][0],
                                w['s5_B_im'][0], w['s5_C_re'][0], w['s5_C_im'][0])
    p['s5_D'] = row(w['s5_D'][0])
    p['s5_w_glu'] = w['s5_w_glu'][0].astype(BF16)
    inner = w['mlstm_conv_w'].shape[2]
    heads = w['mlstm_b_gate'].shape[1] // 2
    dh = inner // heads
    wq_t = _block_diag_tiles(w['mlstm_wq'][0].astype(F32))
    wk_t = _block_diag_tiles(w['mlstm_wk'][0].astype(F32))
    wv_t = _block_diag_tiles(w['mlstm_wv'][0].astype(F32))
    gc, gm = _gate_fold(wq_t, wk_t, wv_t, w['mlstm_w_gate'][0].astype(F32))
    p['m_wq'] = wq_t.astype(BF16)
    p['m_wkt'] = (jnp.swapaxes(wk_t, 1, 2) * (dh ** -0.5)).astype(BF16)
    p['m_wv'] = wv_t.astype(BF16)
    lane_pad = lambda a: jnp.pad(a, ((0, 0), (0, V7X_LANES - a.shape[1])))
    p['m_gc'] = lane_pad(gc).astype(BF16)
    p['m_gm'] = lane_pad(gm).astype(BF16)
    p['m_bg'] = lane_pad(row(w['mlstm_b_gate'][0]))
    p['m_wup'] = w['mlstm_w_up'][0].astype(BF16)
    p['m_cw'] = w['mlstm_conv_w'][0].astype(F32)
    p['m_cb'] = row(w['mlstm_conv_b'][0])
    p['m_ng'] = row(w['mlstm_norm_g'][0])
    p['m_skip'] = row(w['mlstm_skip'][0])
    p['m_wdown'] = w['mlstm_w_down'][0].astype(BF16)
    p['heads'] = heads
    p['f_wup'] = w['ffn_w_up'].astype(BF16)
    p['f_cw'] = w['ffn_conv_w'].astype(F32)
    p['f_cb'] = w['ffn_conv_b'].astype(F32)
    p['f_wdown'] = w['ffn_w_down'].astype(BF16)
    return p


def _to_chunk_major(u2d, b, s):
    d = u2d.shape[1]
    g = d // S5_GROUP_CH
    u = u2d.reshape(b, s // S5_CHUNK, S5_CHUNK, g, S5_GROUP_CH)
    return jnp.transpose(u, (1, 0, 3, 2, 4)).reshape((s // S5_CHUNK) * b, d * S5_CHUNK)


def _to_token_major(yt, b, s):
    d = yt.shape[1] // S5_CHUNK
    g = d // S5_GROUP_CH
    y = yt.reshape(s // S5_CHUNK, b, g, S5_CHUNK, S5_GROUP_CH)
    return jnp.transpose(y, (1, 0, 3, 2, 4)).reshape(b * s, d)


def _trunk(x, st, p, tiles):
    b, s, d = x.shape
    ng = p['norm_g']
    g_of = lambda layer, k: ng[layer, k].reshape(1, d)
    zero_init = st is None
    groups = d // S5_GROUP_CH
    x2d = x.reshape(b * s, d)

    n_state = p['s5_mats']['row'][1].shape[2] // 2
    if zero_init:
        h0re = h0im = jnp.zeros((groups // 2, b, 2 * n_state), F32)
    else:
        pair = lambda h: jnp.transpose(h.astype(F32).reshape(b, groups // 2, 2 * n_state), (1, 0, 2))
        h0re, h0im = pair(st['s5_re']), pair(st['s5_im'])
    if tiles.s5_time:
        y, hre, him = _s5_fused(x, g_of(0, 0), p['s5_mats']['col'], h0re, h0im, tiles.s5_time)
        y2d = y.reshape(b * s, d)
    else:
        u = _norm_cast(x2d, g_of(0, 0), tiles.rows)
        yt, hre, him = _s5_core(_to_chunk_major(u, b, s), p['s5_mats']['row'], h0re, h0im, b, tiles.s5_rows)
        y2d = _to_token_major(yt, b, s)
    unpair = lambda h: jnp.transpose(h, (1, 0, 2)).reshape(1, b, groups, n_state)
    o_re, o_im = unpair(hre), unpair(him)
    x2d = _glu(x2d, y2d, g_of(0, 0), p['s5_D'], p['s5_w_glu'], g_of(0, 1), tiles.rows)

    def ffn(x2d, layer):
        dff2 = p['f_wup'].shape[2]
        width = p['f_cw'].shape[1]
        buf = jnp.zeros((b, width - 1, dff2), F32) if zero_init else st['ffn_conv'][layer].astype(F32)
        xo, nbuf = _ffn(x2d.reshape(b, s, d), buf, g_of(layer, 2), p['f_wup'][layer], p['f_cw'][layer],
                        p['f_cb'][layer].reshape(1, dff2), p['f_wdown'][layer], g_of(layer, 3), tiles.ffn_time)
        return xo.reshape(b * s, d), nbuf

    x2d, fbuf0 = ffn(x2d, 0)

    heads = p['heads']
    inner = p['m_cw'].shape[1]
    width = p['m_cw'].shape[0]
    cbuf = jnp.zeros((b, width - 1, inner), F32) if zero_init else st['mlstm_conv'].astype(F32)
    if zero_init:
        state = None
    else:
        state = (st['mlstm_C'].astype(F32), st['mlstm_n'].astype(F32),
                 jnp.broadcast_to(st['mlstm_m'].astype(F32)[:, :, None], (b, heads, V7X_LANES)))
    x3, c_out, n_out, m_out, ncbuf = _mlstm_layer(x2d.reshape(b, s, d), cbuf, state, p, g_of(1, 0), g_of(1, 1),
                                                  tiles.mlstm_chunk)
    x2d, fbuf1 = ffn(x3.reshape(b * s, d), 1)

    return (x2d.reshape(b, s, d), o_re, o_im, c_out[None], n_out[None], m_out[None, :, :, 0], ncbuf[None],
            jnp.stack([fbuf0, fbuf1], axis=0))


def kernel(x_prompt, x_sample, state_s5_re, state_s5_im, state_mlstm_C, state_mlstm_n, state_mlstm_m, state_mlstm_conv, state_ffn_conv, norm_g, s5_A_re, s5_A_im, s5_log_dt, s5_B_re, s5_B_im, s5_C_re, s5_C_im, s5_D, s5_w_glu, mlstm_w_up, mlstm_conv_w, mlstm_conv_b, mlstm_wq, mlstm_wk, mlstm_wv, mlstm_w_gate, mlstm_b_gate, mlstm_norm_g, mlstm_skip, mlstm_w_down, ffn_w_up, ffn_conv_w, ffn_conv_b, ffn_w_down):
    w = {
        'norm_g': norm_g, 's5_A_re': s5_A_re, 's5_A_im': s5_A_im, 's5_log_dt': s5_log_dt,
        's5_B_re': s5_B_re, 's5_B_im': s5_B_im, 's5_C_re': s5_C_re, 's5_C_im': s5_C_im, 's5_D': s5_D,
        's5_w_glu': s5_w_glu, 'mlstm_w_up': mlstm_w_up, 'mlstm_conv_w': mlstm_conv_w,
        'mlstm_conv_b': mlstm_conv_b, 'mlstm_wq': mlstm_wq, 'mlstm_wk': mlstm_wk, 'mlstm_wv': mlstm_wv,
        'mlstm_w_gate': mlstm_w_gate, 'mlstm_b_gate': mlstm_b_gate, 'mlstm_norm_g': mlstm_norm_g,
        'mlstm_skip': mlstm_skip, 'mlstm_w_down': mlstm_w_down, 'ffn_w_up': ffn_w_up,
        'ffn_conv_w': ffn_conv_w, 'ffn_conv_b': ffn_conv_b, 'ffn_w_down': ffn_w_down,
    }
    assert norm_g.shape[0] == 2 and s5_A_re.shape[0] == 1 and mlstm_w_up.shape[0] == 1
    assert x_prompt.shape[0] == V7X_SUBLANES and x_sample.shape[0] == V7X_SUBLANES
    p = _prep_weights(w)
    out_p = _trunk(x_prompt.astype(F32), None, p, _pick_tiles(*x_prompt.shape[:2]))
    st = {'s5_re': state_s5_re[0], 's5_im': state_s5_im[0], 'mlstm_C': state_mlstm_C[0],
          'mlstm_n': state_mlstm_n[0], 'mlstm_m': state_mlstm_m[0], 'mlstm_conv': state_mlstm_conv[0],
          'ffn_conv': state_ffn_conv}
    out_s = _trunk(x_sample.astype(F32), st, p, _pick_tiles(*x_sample.shape[:2]))
    return (out_p[0], out_s[0]) + tuple(out_p[1:]) + tuple(out_s[1:])
```

```python
import functools
import math
from typing import NamedTuple

import jax
import jax.numpy as jnp
from jax import lax
from jax.experimental import pallas as pl
from jax.experimental.pallas import tpu as pltpu

F32 = jnp.float32
BF16 = jnp.bfloat16
HI = lax.Precision.HIGHEST

NORM_EPS = 1e-6
LN_EPS = 1e-5
S5_GROUP_CH = 16
S5_CHUNK = 16
QKV_BLOCK = 4

V7X_LANES = 128
V7X_SUBLANES = 8
V7X_MXU = 256
V7X_VMEM_BYTES = 64 * 1024 * 1024


class Tiles(NamedTuple):
    rows: int
    time: int
    s5_rows: int
    mlstm_chunk: int
    s5_time: int
    ffn_time: int


def _pick_tiles(batch, seq):
    s5_time = (V7X_LANES // batch) * S5_CHUNK
    return Tiles(rows=min(batch * seq, 512), time=min(seq, 512),
                 s5_rows=min(batch * (seq // S5_CHUNK), 512), mlstm_chunk=min(seq, 256),
                 s5_time=s5_time if seq % s5_time == 0 else 0, ffn_time=min(seq, 1024))


def _vmem_limit(block_bytes, scratch_bytes):
    want = 2 * block_bytes + scratch_bytes + 16 * 1024 * 1024
    return int(min(want, V7X_VMEM_BYTES - 8 * 1024 * 1024))


def _nbytes(shape, dtype):
    return math.prod(shape) * jnp.dtype(dtype).itemsize


def _rms(x, g):
    return x * lax.rsqrt(jnp.mean(x * x, axis=-1, keepdims=True) + NORM_EPS) * g


def _dot(a, b):
    return jnp.dot(a, b, preferred_element_type=F32)


def _dot_nt(a, b):
    return lax.dot_general(a, b, (((1,), (1,)), ((), ())), preferred_element_type=F32)


def _silu(x):
    return (0.5 * x) * (1.0 + jnp.tanh(0.5 * x))


def _causal_dwconv(u, prev, cw_ref, cb_ref, cs):
    width = cw_ref.shape[0]
    acc = cb_ref[:, cs] + u * cw_ref[width - 1:width, cs]
    rows = lax.broadcasted_iota(jnp.int32, prev.shape, 0)
    for j in range(width - 1):
        d = width - 1 - j
        rolled = pltpu.roll(u, d, 0)
        head = jnp.where(rows < d, pltpu.roll(prev, d, 0), rolled[0:V7X_SUBLANES])
        shifted = jnp.concatenate([head, rolled[V7X_SUBLANES:]], axis=0)
        acc = acc + shifted * cw_ref[j:j + 1, cs]
    return acc


def _log_sigmoid(x):
    return jnp.minimum(x, 0.0) - jnp.log1p(jnp.exp(-jnp.abs(x)))


def _split3(x):
    hi = x.astype(BF16)
    r1 = x - hi.astype(F32)
    mid = r1.astype(BF16)
    lo = (r1 - mid.astype(F32)).astype(BF16)
    return hi, mid, lo


def _norm_cast_kernel(x_ref, g_ref, o_ref):
    o_ref[...] = _rms(x_ref[...], g_ref[...]).astype(o_ref.dtype)


def _norm_cast(x2d, g, tb):
    rows, d = x2d.shape
    return pl.pallas_call(
        _norm_cast_kernel,
        grid=(rows // tb,),
        in_specs=[pl.BlockSpec((tb, d), lambda i: (i, 0)), pl.BlockSpec((1, d), lambda i: (0, 0))],
        out_specs=pl.BlockSpec((tb, d), lambda i: (i, 0)),
        out_shape=jax.ShapeDtypeStruct((rows, d), BF16),
        name="s5_norm",
    )(x2d, g)


def _s5_kernel(u_ref, t_ref, pre_ref, pim_ref, qre_ref, qim_ref, are_ref, aim_ref, h0re_ref, h0im_ref,
               y_ref, hre_ref, him_ref, pure_s, puim_s, hinre_s, hinim_s, stre_s, stim_s, *, nb, rb):
    r = pl.program_id(1)

    @pl.when(r == 0)
    def _():
        stre_s[...] = h0re_ref[0]
        stim_s[...] = h0im_ref[0]

    u = u_ref[...]
    pure_s[...] = _dot(u, pre_ref[0])
    puim_s[...] = _dot(u, pim_ref[0])
    ar = jnp.broadcast_to(are_ref[0], (nb, 2 * 64))
    ai = jnp.broadcast_to(aim_ref[0], (nb, 2 * 64))

    def step(i, carry):
        re, im = carry
        rows = pl.ds(pl.multiple_of(i * nb, nb), nb)
        hinre_s[rows, :] = re
        hinim_s[rows, :] = im
        return (ar * re - ai * im + pure_s[rows, :], ar * im + ai * re + puim_s[rows, :])

    re, im = lax.fori_loop(0, rb // nb, step, (stre_s[...], stim_s[...]))
    stre_s[...] = re
    stim_s[...] = im
    hre_ref[0] = re
    him_ref[0] = im

    carry_in = _dot(hinre_s[...].astype(BF16), qre_ref[0]) + _dot(hinim_s[...].astype(BF16), qim_ref[0])
    w = V7X_MXU
    for k in range(2):
        y_ref[:, k * w:(k + 1) * w] = (_dot(u[:, k * w:(k + 1) * w], t_ref[k])
                                       + carry_in[:, k * w:(k + 1) * w]).astype(y_ref.dtype)


def _s5_core(ut, mats, h0re, h0im, nb, rb):
    rows, cols = ut.shape
    pairs = cols // (2 * V7X_MXU)
    tm, pre, pim, qre, qim, are, aim = mats
    pair3 = lambda p, r: (p, 0, 0)
    kern = functools.partial(_s5_kernel, nb=nb, rb=rb)
    blocks = (_nbytes((rb, 512), BF16) * 2 + _nbytes((2, 256, 256), BF16) + 4 * _nbytes((512, 128), BF16))
    scratch = 4 * _nbytes((rb, 128), F32) + 2 * _nbytes((nb, 128), F32)
    return pl.pallas_call(
        kern,
        grid=(pairs, rows // rb),
        in_specs=[
            pl.BlockSpec((rb, 2 * V7X_MXU), lambda p, r: (r, p)),
            pl.BlockSpec((2, V7X_MXU, V7X_MXU), pair3),
            pl.BlockSpec((1, 2 * V7X_MXU, 128), pair3),
            pl.BlockSpec((1, 2 * V7X_MXU, 128), pair3),
            pl.BlockSpec((1, 128, 2 * V7X_MXU), pair3),
            pl.BlockSpec((1, 128, 2 * V7X_MXU), pair3),
            pl.BlockSpec((1, 1, 128), pair3),
            pl.BlockSpec((1, 1, 128), pair3),
            pl.BlockSpec((1, nb, 128), pair3),
            pl.BlockSpec((1, nb, 128), pair3),
        ],
        out_specs=[
            pl.BlockSpec((rb, 2 * V7X_MXU), lambda p, r: (r, p)),
            pl.BlockSpec((1, nb, 128), pair3),
            pl.BlockSpec((1, nb, 128), pair3),
        ],
        out_shape=[
            jax.ShapeDtypeStruct((rows, cols), BF16),
            jax.ShapeDtypeStruct((pairs, nb, 128), F32),
            jax.ShapeDtypeStruct((pairs, nb, 128), F32),
        ],
        scratch_shapes=[pltpu.VMEM((rb, 128), F32)] * 4 + [pltpu.VMEM((nb, 128), F32)] * 2,
        compiler_params=pltpu.CompilerParams(
            dimension_semantics=("arbitrary", "arbitrary"), vmem_limit_bytes=_vmem_limit(blocks, scratch)),
        name="s5_core",
    )(ut, tm, pre, pim, qre, qim, are, aim, h0re, h0im)


def _s5_fused_kernel(x_ref, g_ref, tt_ref, pt_ref, qt_ref, are_ref, aim_ref, h0re_ref, h0im_ref,
                     y_ref, hre_ref, him_ref,
                     us, ugt, pure, puim, hinre, hinim, ytmp, ys, stre, stim, *, nb, tbt):
    lc = S5_CHUNK
    ch = S5_GROUP_CH
    gps = V7X_LANES // ch
    nch = tbt // lc
    cols = nb * nch
    k = pl.program_id(1)

    @pl.when(k == 0)
    def _():
        for b in range(nb):
            u = _rms(x_ref[b], g_ref[...])
            for kk in range(us.shape[0]):
                us[kk, b * tbt:(b + 1) * tbt, :] = u[:, kk * V7X_LANES:(kk + 1) * V7X_LANES]

    @pl.when(pl.program_id(0) == 0)
    def _():
        stre[k] = h0re_ref[...]
        stim[k] = h0im_ref[...]

    for s in range(lc):
        a = us[k, pl.ds(s, cols, stride=lc), :].T
        for g in range(gps):
            ugt[g, s * ch:(s + 1) * ch, :] = a[g * ch:(g + 1) * ch, :].astype(BF16)

    half = pt_ref.shape[1] // 2
    for pr in range(gps // 2):
        put0 = _dot(pt_ref[2 * pr], ugt[2 * pr])
        put1 = _dot(pt_ref[2 * pr + 1], ugt[2 * pr + 1])
        pure[pr] = jnp.concatenate([put0[:half], put1[:half]], axis=0).T
        puim[pr] = jnp.concatenate([put0[half:], put1[half:]], axis=0).T

    for pr in range(gps // 2):
        re = stre[k, pr]
        im = stim[k, pr]
        ar = jnp.broadcast_to(are_ref[pr], re.shape)
        ai = jnp.broadcast_to(aim_ref[pr], re.shape)
        for j in range(nch):
            rows = pl.ds(j, nb, stride=nch)
            hinre[pr, rows, :] = re
            hinim[pr, rows, :] = im
            re, im = (ar * re - ai * im + pure[pr, rows, :], ar * im + ai * re + puim[pr, rows, :])
        stre[k, pr] = re
        stim[k, pr] = im
        hre_ref[k * (gps // 2) + pr] = re
        him_ref[k * (gps // 2) + pr] = im

    for pr in range(gps // 2):
        hre_t = hinre[pr].T
        him_t = hinim[pr].T
        for i in range(2):
            g = 2 * pr + i
            hin_t = jnp.concatenate([hre_t[i * half:(i + 1) * half], him_t[i * half:(i + 1) * half]],
                                    axis=0).astype(BF16)
            ytmp[g] = _dot(tt_ref[g], ugt[g]) + _dot(qt_ref[g], hin_t)

    for t in range(lc):
        z = jnp.concatenate([ytmp[g, t * ch:(t + 1) * ch, :] for g in range(gps)], axis=0)
        ys[pl.ds(t, cols, stride=lc), :] = z.T
    for b in range(nb):
        y_ref[b] = ys[b * tbt:(b + 1) * tbt, :].astype(y_ref.dtype)


def _s5_fused(x, g, mats_t, h0re, h0im, tbt):
    nb, s, d = x.shape
    tt, pt, qt, are, aim = mats_t
    groups = tt.shape[0]
    gps = V7X_LANES // S5_GROUP_CH
    slabs = groups // gps
    w = S5_CHUNK * S5_GROUP_CH
    nst = pt.shape[1]
    cols = nb * (tbt // S5_CHUNK)
    kern = functools.partial(_s5_fused_kernel, nb=nb, tbt=tbt)
    slab3 = lambda t, k: (k, 0, 0)
    blocks = (_nbytes((nb, tbt, d), F32) + _nbytes((nb, tbt, V7X_LANES), BF16)
              + gps * (_nbytes((w, w), BF16) + 2 * _nbytes((w, nst), BF16)))
    scratch = (_nbytes((slabs, nb * tbt, V7X_LANES), F32) + _nbytes((gps, w, cols), BF16)
               + 4 * _nbytes((gps // 2, cols, nst), F32) + _nbytes((gps, w, cols), F32)
               + _nbytes((nb * tbt, V7X_LANES), F32))
    return pl.pallas_call(
        kern,
        grid=(s // tbt, slabs),
        in_specs=[
            pl.BlockSpec((nb, tbt, d), lambda t, k: (0, t, 0)),
            pl.BlockSpec((1, d), lambda t, k: (0, 0)),
            pl.BlockSpec((gps, w, w), slab3),
            pl.BlockSpec((gps, nst, w), slab3),
            pl.BlockSpec((gps, w, nst), slab3),
            pl.BlockSpec((gps // 2, 1, nst), slab3),
            pl.BlockSpec((gps // 2, 1, nst), slab3),
            pl.BlockSpec((gps // 2, nb, nst), slab3),
            pl.BlockSpec((gps // 2, nb, nst), slab3),
        ],
        out_specs=[
            pl.BlockSpec((nb, tbt, V7X_LANES), lambda t, k: (0, t, k)),
            pl.BlockSpec((groups // 2, nb, nst), lambda t, k: (0, 0, 0)),
            pl.BlockSpec((groups // 2, nb, nst), lambda t, k: (0, 0, 0)),
        ],
        out_shape=[
            jax.ShapeDtypeStruct((nb, s, d), BF16),
            jax.ShapeDtypeStruct((groups // 2, nb, nst), F32),
            jax.ShapeDtypeStruct((groups // 2, nb, nst), F32),
        ],
        scratch_shapes=[
            pltpu.VMEM((slabs, nb * tbt, V7X_LANES), F32),
            pltpu.VMEM((gps, w, cols), BF16),
            pltpu.VMEM((gps // 2, cols, nst), F32),
            pltpu.VMEM((gps // 2, cols, nst), F32),
            pltpu.VMEM((gps // 2, cols, nst), F32),
            pltpu.VMEM((gps // 2, cols, nst), F32),
            pltpu.VMEM((gps, w, cols), F32),
            pltpu.VMEM((nb * tbt, V7X_LANES), F32),
            pltpu.VMEM((slabs, gps // 2, nb, nst), F32),
            pltpu.VMEM((slabs, gps // 2, nb, nst), F32),
        ],
        compiler_params=pltpu.CompilerParams(
            dimension_semantics=("arbitrary", "arbitrary"), vmem_limit_bytes=_vmem_limit(blocks, scratch)),
        name="s5_fused",
    )(x, g, tt, pt, qt, are, aim, h0re, h0im)


def _s5_matrices(a_re, a_im, log_dt, b_re, b_im, c_re, c_im):
    lc = S5_CHUNK
    groups, n = a_re.shape
    ch = S5_GROUP_CH
    a = lax.complex(a_re.astype(F32), a_im.astype(F32))
    adt = a * jnp.exp(log_dt.astype(F32))[:, None]
    j = jnp.arange(lc + 1, dtype=F32)
    apow = jnp.exp(adt[None] * j[:, None, None])
    b_bar = ((apow[1] - 1.0) / a)[..., None] * lax.complex(b_re.astype(F32), b_im.astype(F32))
    c_mat = lax.complex(c_re.astype(F32), c_im.astype(F32))
    ca = c_mat[None] * apow[:, :, None, :]
    kern = (jnp.einsum('jgcn,gnd->jgcd', ca.real, b_bar.real, precision=HI)
            - jnp.einsum('jgcn,gnd->jgcd', ca.imag, b_bar.imag, precision=HI))
    s_idx = jnp.arange(lc)[:, None]
    t_idx = jnp.arange(lc)[None, :]
    lag = t_idx - s_idx
    kt = kern[jnp.clip(lag, 0, lc)]
    kt = jnp.where((lag >= 0)[:, :, None, None, None], kt, 0.0)
    tm = jnp.transpose(kt, (2, 0, 4, 1, 3)).reshape(groups, lc * ch, lc * ch)
    pw = apow[lc - 1 - jnp.arange(lc)]
    p = pw[:, :, :, None] * b_bar[None]
    p = jnp.transpose(p, (1, 0, 3, 2)).reshape(groups, lc * ch, n)
    q = ca[1:]
    q = jnp.transpose(q, (1, 3, 0, 2)).reshape(groups, n, lc * ch)

    def pair_rows(m):
        g2 = groups // 2
        m = m.reshape(g2, 2, m.shape[1], m.shape[2])
        z = jnp.zeros_like(m[:, 0])
        top = jnp.concatenate([m[:, 0], z], axis=2)
        bot = jnp.concatenate([z, m[:, 1]], axis=2)
        return jnp.concatenate([top, bot], axis=1)

    pre = pair_rows(p.real).astype(BF16)
    pim = pair_rows(p.imag).astype(BF16)
    qre = pair_rows(q.real).astype(BF16)
    qim = pair_rows(-q.imag).astype(BF16)
    a_chunk = apow[lc].reshape(groups // 2, 1, 2 * n)
    are, aim = a_chunk.real, a_chunk.imag
    tt = jnp.swapaxes(tm, 1, 2).astype(BF16)
    pt = jnp.concatenate([jnp.swapaxes(p.real, 1, 2), jnp.swapaxes(p.imag, 1, 2)], axis=1).astype(BF16)
    qt = jnp.concatenate([jnp.swapaxes(q.real, 1, 2), jnp.swapaxes(-q.imag, 1, 2)], axis=2).astype(BF16)
    return {'row': (tm.astype(BF16), pre, pim, qre, qim, are, aim), 'col': (tt, pt, qt, are, aim)}


def _glu_kernel(x_ref, y_ref, g0_ref, d_ref, w_ref, g1_ref, o_ref, out_s):
    x = x_ref[...]
    d = x.shape[1]
    u = _rms(x, g0_ref[...])
    yy = y_ref[...].astype(F32) + d_ref[...] * u
    z = jax.nn.gelu(yy, approximate=True).astype(BF16)
    w = V7X_MXU
    nslab = d // w
    pair = lambda j: (_dot(z, w_ref[:, j * w:(j + 1) * w]), _dot(z, w_ref[:, d + j * w:d + (j + 1) * w]))
    nxt = pair(0)
    ssq = jnp.zeros((x.shape[0], 1), F32)
    for j in range(nslab):
        a, b = nxt
        if j + 1 < nslab:
            nxt = pair(j + 1)
        out = a * (0.5 * (1.0 + jnp.tanh(0.5 * b)))
        out_s[:, j * w:(j + 1) * w] = out
        ssq = ssq + jnp.sum(out * out, axis=-1, keepdims=True)
    scale = lax.rsqrt(ssq * (1.0 / d) + NORM_EPS)
    o_ref[...] = x + out_s[...] * scale * g1_ref[...]


def _const_spec(shape):
    nd = len(shape)
    return pl.BlockSpec(shape, lambda *_: (0,) * nd, pipeline_mode=pl.Buffered(1))


def _glu(x2d, y2d, g0, dskip, w, g1, tb):
    rows, d = x2d.shape
    blocks = 2 * _nbytes((tb, d), F32) + _nbytes((tb, d), BF16)
    scratch = _nbytes(w.shape, BF16) + 3 * _nbytes((tb, 2 * d), F32)
    row = pl.BlockSpec((tb, d), lambda i: (i, 0))
    return pl.pallas_call(
        _glu_kernel,
        grid=(rows // tb,),
        in_specs=[row, row, _const_spec((1, d)), _const_spec((1, d)), _const_spec(w.shape), _const_spec((1, d))],
        out_specs=row,
        out_shape=jax.ShapeDtypeStruct((rows, d), F32),
        scratch_shapes=[pltpu.VMEM((tb, d), F32)],
        compiler_params=pltpu.CompilerParams(
            dimension_semantics=("arbitrary",), vmem_limit_bytes=_vmem_limit(blocks, scratch)),
        name="s5_glu",
    )(x2d, y2d, g0, dskip, w, g1)


def _ffn_kernel(x_ref, buf_ref, g2_ref, wup_ref, cw_ref, cb_ref, wdown_ref, g3_ref,
                o_ref, nbuf_ref, carry, act, *, tb, dff, width):
    t = pl.program_id(1)
    pad = V7X_SUBLANES

    @pl.when(t == 0)
    def _():
        carry[...] = jnp.zeros_like(carry)
        carry[pad - (width - 1):pad, :] = buf_ref[0]

    x = x_ref[0]
    hn = _rms(x, g2_ref[...]).astype(BF16)
    fb = V7X_MXU

    def conv(u, cs):
        prev = carry[:, cs]
        carry[:, cs] = u[tb - pad:tb, :]
        return _causal_dwconv(u, prev, cw_ref, cb_ref, cs)

    nslab = dff // fb
    cols = lambda j: (slice(j * fb, (j + 1) * fb), slice(dff + j * fb, dff + (j + 1) * fb))
    up = lambda j: tuple(_dot(hn, wup_ref[:, cs]) for cs in cols(j))
    nxt = up(0)
    for j in range(nslab):
        ug, uv = nxt
        if j + 1 < nslab:
            nxt = up(j + 1)
        gate = conv(ug, cols(j)[0])
        val = conv(uv, cols(j)[1])
        act[:, j * fb:(j + 1) * fb] = (jax.nn.gelu(gate, approximate=True) * val).astype(BF16)

    f = _dot(act[...], wdown_ref[...])
    o_ref[0] = x + _rms(f, g3_ref[...])
    nbuf_ref[0] = carry[pad - (width - 1):pad, :]


def _ffn(x, buf, g2, wup, cw, cb, wdown, g3, tb):
    b, s, d = x.shape
    dff = wdown.shape[0]
    width = cw.shape[0]
    kern = functools.partial(_ffn_kernel, tb=tb, dff=dff, width=width)
    blocks = 2 * _nbytes((tb, d), F32)
    scratch = (_nbytes(wup.shape, BF16) + _nbytes(wdown.shape, BF16) + 6 * _nbytes((tb, V7X_MXU), F32)
               + _nbytes((8, 2 * dff), F32) + _nbytes((tb, dff), BF16) + 2 * _nbytes((tb, d), F32))
    xs = pl.BlockSpec((1, tb, d), lambda i, t: (i, t, 0))
    bs = pl.BlockSpec((1, width - 1, 2 * dff), lambda i, t: (i, 0, 0))
    return pl.pallas_call(
        kern,
        grid=(b, s // tb),
        in_specs=[xs, bs, _const_spec((1, d)), _const_spec(wup.shape), _const_spec(cw.shape),
                  _const_spec((1, 2 * dff)), _const_spec(wdown.shape), _const_spec((1, d))],
        out_specs=[xs, bs],
        out_shape=[jax.ShapeDtypeStruct((b, s, d), F32), jax.ShapeDtypeStruct((b, width - 1, 2 * dff), F32)],
        scratch_shapes=[pltpu.VMEM((V7X_SUBLANES, 2 * dff), F32), pltpu.VMEM((tb, dff), BF16)],
        compiler_params=pltpu.CompilerParams(
            dimension_semantics=("arbitrary", "arbitrary"), vmem_limit_bytes=_vmem_limit(blocks, scratch)),
        name="conv_ffn",
    )(x, buf, g2, wup, cw, cb, wdown, g3)


def _gate_fold_kernel(wq_ref, wk_ref, wv_ref, gq_ref, gk_ref, gv_ref, gc_ref, gm_ref):
    dotp = functools.partial(jnp.dot, preferred_element_type=F32, precision=HI)
    gc_ref[...] = dotp(wq_ref[0], gq_ref[...]) + dotp(wk_ref[0], gk_ref[...])
    gm_ref[...] = dotp(wv_ref[0], gv_ref[...])


def _gate_fold(wq_t, wk_t, wv_t, w_gate):
    nt, w, _ = wq_t.shape
    inner = nt * w
    ng = w_gate.shape[1]
    tile = pl.BlockSpec((1, w, w), lambda i: (i, 0, 0))
    gs = [pl.BlockSpec((w, ng), lambda i, k=k: (k * nt + i, 0)) for k in range(3)]
    out = pl.BlockSpec((w, ng), lambda i: (i, 0))
    return pl.pallas_call(
        _gate_fold_kernel,
        grid=(nt,),
        in_specs=[tile, tile, tile] + gs,
        out_specs=[out, out],
        out_shape=[jax.ShapeDtypeStruct((inner, ng), F32)] * 2,
        name="mlstm_gate_fold",
    )(wq_t, wk_t, wv_t, w_gate, w_gate, w_gate)


def _mpre_kernel(x_ref, cbuf_ref, g_ref, wup_ref, cw_ref, cb_ref, wq_ref, wkt_ref, wv_ref, gc_ref, gm_ref,
                 bg_ref, q_ref, kt_ref, v_ref, xc_ref, z_ref, gates_ref, nbuf_ref, carry,
                 *, tb, inner, width):
    t = pl.program_id(1)
    pad = V7X_SUBLANES

    @pl.when(t == 0)
    def _():
        carry[...] = jnp.zeros_like(carry)
        carry[pad - (width - 1):pad, :] = cbuf_ref[0]

    h = _rms(x_ref[0], g_ref[...]).astype(BF16)
    w = V7X_MXU
    gacc = jnp.zeros((tb, bg_ref.shape[1]), F32) + bg_ref[...]
    for i in range(inner // w):
        cs = slice(i * w, (i + 1) * w)
        xm = _dot(h, wup_ref[:, cs])
        prev = carry[:, cs]
        carry[:, cs] = xm[tb - pad:tb, :]
        xc = _silu(_causal_dwconv(xm, prev, cw_ref, cb_ref, cs)).astype(BF16)
        xmb = xm.astype(BF16)
        q_ref[0, :, cs] = _dot(xc, wq_ref[i]).astype(BF16)
        kt_ref[0, cs, :] = _dot_nt(wkt_ref[i], xc).astype(BF16)
        v_ref[0, :, cs] = _dot(xmb, wv_ref[i]).astype(BF16)
        xc_ref[0, :, cs] = xc
        z_ref[0, :, cs] = _silu(_dot(h, wup_ref[:, inner + i * w:inner + (i + 1) * w])).astype(BF16)
        gacc = gacc + _dot(xc, gc_ref[cs, :]) + _dot(xmb, gm_ref[cs, :])
    gates_ref[0] = gacc
    nbuf_ref[0] = carry[pad - (width - 1):pad, :]


def _mpre(x, cbuf, g, wup, cw, cb, wq_t, wkt_t, wv_t, gc, gm, bg, tb):
    b, s, d = x.shape
    inner = cw.shape[1]
    width = cw.shape[0]
    ng = bg.shape[1]
    kern = functools.partial(_mpre_kernel, tb=tb, inner=inner, width=width)
    act = pl.BlockSpec((1, tb, inner), lambda i, t: (i, t, 0))
    blocks = _nbytes((tb, d), F32) + 5 * _nbytes((tb, inner), BF16)
    scratch = (_nbytes(wup.shape, BF16) + 3 * _nbytes(wq_t.shape, BF16) + 2 * _nbytes((inner, 128), BF16)
               + _nbytes((8, inner), F32) + 8 * _nbytes((tb, V7X_MXU), F32))
    return pl.pallas_call(
        kern,
        grid=(b, s // tb),
        in_specs=[pl.BlockSpec((1, tb, d), lambda i, t: (i, t, 0)),
                  pl.BlockSpec((1, width - 1, inner), lambda i, t: (i, 0, 0)),
                  _const_spec((1, d)), _const_spec(wup.shape), _const_spec(cw.shape), _const_spec((1, inner)),
                  _const_spec(wq_t.shape), _const_spec(wkt_t.shape), _const_spec(wv_t.shape),
                  _const_spec(gc.shape), _const_spec(gm.shape), _const_spec((1, ng))],
        out_specs=[act, pl.BlockSpec((1, inner, tb), lambda i, t: (i, 0, t)), act, act, act,
                   pl.BlockSpec((1, tb, ng), lambda i, t: (i, t, 0)),
                   pl.BlockSpec((1, width - 1, inner), lambda i, t: (i, 0, 0))],
        out_shape=[jax.ShapeDtypeStruct((b, s, inner), BF16), jax.ShapeDtypeStruct((b, inner, s), BF16),
                   jax.ShapeDtypeStruct((b, s, inner), BF16), jax.ShapeDtypeStruct((b, s, inner), BF16),
                   jax.ShapeDtypeStruct((b, s, inner), BF16), jax.ShapeDtypeStruct((b, s, ng), F32),
                   jax.ShapeDtypeStruct((b, width - 1, inner), F32)],
        scratch_shapes=[pltpu.VMEM((V7X_SUBLANES, inner), F32)],
        compiler_params=pltpu.CompilerParams(
            dimension_semantics=("arbitrary", "arbitrary"), vmem_limit_bytes=_vmem_limit(blocks, scratch)),
        name="mlstm_pre",
    )(x, cbuf, g, wup, cw, cb, wq_t, wkt_t, wv_t, gc, gm, bg)


def _mlstm_chunk(q_ref, kt_ref, v_ref, g, gt, ct_s, n_s, m_s, fill, emit_hn, fill_update, *, lc, heads, dh):
    row = lax.broadcasted_iota(jnp.int32, (lc, lc), 0)
    col = lax.broadcasted_iota(jnp.int32, (lc, lc), 1)
    causal = col <= row
    tri_l = jnp.where(causal, 1.0, 0.0).astype(BF16)
    tri_u = jnp.where(row <= col, 1.0, 0.0).astype(BF16)
    b_cols = sum(_dot(tri_l, part) for part in _split3(_log_sigmoid(g)))
    b_rows = sum(_dot(part, tri_u) for part in _split3(_log_sigmoid(gt)))

    hsl = [slice(h * dh, (h + 1) * dh) for h in range(heads)]
    st = []
    for h in range(heads):
        b_col = b_cols[:, heads + h:heads + h + 1]
        b_row = b_rows[heads + h:heads + h + 1, :]
        i_row = gt[h:h + 1, :]
        m_prev = m_s[h:h + 1, 0:1]
        dmat = jnp.where(causal, b_col - b_row + i_row, -jnp.inf)
        inter = b_col + m_prev
        m_t = jnp.maximum(inter, jnp.max(dmat, axis=1, keepdims=True))
        st.append(dict(b_col=b_col, b_row=b_row, i_row=i_row, m_prev=m_prev, m_t=m_t,
                       wts=jnp.exp(dmat - m_t), a=jnp.exp(inter - m_t)))
        fill()
    for h in range(heads):
        s = st[h]
        s['sc'] = _dot(q_ref[0, :, hsl[h]], kt_ref[0, hsl[h], :]) * s['wts']
    for h in range(heads):
        fill()
        s = st[h]
        qh = q_ref[0, :, hsl[h]]
        sc = s['sc']
        a = s['a']
        num = _dot(sc.astype(BF16), v_ref[0, :, hsl[h]]) + a * _dot(qh, ct_s[h].astype(BF16))
        qn = jnp.sum(qh.astype(F32) * n_s[h:h + 1, :], axis=1, keepdims=True)
        den = jnp.sum(sc, axis=1, keepdims=True) + a * qn
        hh = num / jnp.maximum(jnp.abs(den), jnp.exp(-s['m_t']))
        mu = jnp.mean(hh, axis=1, keepdims=True)
        dev = hh - mu
        var = jnp.mean(dev * dev, axis=1, keepdims=True)
        emit_hn(h, dev * lax.rsqrt(var + LN_EPS))
    for h in range(heads):
        fill_update(h)
        s = st[h]
        kth = kt_ref[0, hsl[h], :]
        b_last = s['b_row'][:, lc - 1:lc]
        g_row = b_last - s['b_row'] + s['i_row']
        m_new = jnp.maximum(b_last + s['m_prev'], jnp.max(g_row, axis=1, keepdims=True))
        decay = jnp.exp(b_last + s['m_prev'] - m_new)
        wg_row = jnp.exp(g_row - m_new)
        wg_col = jnp.exp(b_last - s['b_col'] + g[:, h:h + 1] - m_new)
        wv = (v_ref[0, :, hsl[h]].astype(F32) * wg_col).astype(BF16)
        ct_s[h] = decay * ct_s[h] + _dot(kth, wv)
        wg16 = jnp.broadcast_to(wg_row, (16, lc)).astype(BF16)
        n_s[h:h + 1, :] = decay * n_s[h:h + 1, :] + _dot_nt(wg16, kth)[0:1, :]
        m_s[h:h + 1, :] = jnp.broadcast_to(m_new, (1, V7X_LANES))


def _mlstm_layer_kernel(*refs, lc, heads, inner, width, zero_init):
    (x_ref, cbuf_ref, g0_ref, wup_ref, cw_ref, cb_ref, wq_ref, wkt_ref, wv_ref, gc_ref, gm_ref, bg_ref,
     ng_ref, sk_ref, wdown_ref, g1_ref) = refs[:16]
    n_in = 16
    if not zero_init:
        c0_ref, n0_ref, m0_ref = refs[16:19]
        n_in = 19
    o_ref, cout_ref, nout_ref, mout_ref, nbuf_ref = refs[n_in:n_in + 5]
    carry, q_s, kt_s, v_s, xc_s, sz_s, act_s, ct_s, n_s, m_s = refs[n_in + 5:]
    dh = inner // heads
    pad = V7X_SUBLANES
    c = pl.program_id(1)

    @pl.when(c == 0)
    def _():
        carry[...] = jnp.zeros_like(carry)
        carry[pad - (width - 1):pad, :] = cbuf_ref[0]
        if zero_init:
            ct_s[...] = jnp.zeros_like(ct_s)
            n_s[...] = jnp.zeros_like(n_s)
            m_s[...] = jnp.zeros_like(m_s)
        else:
            for h in range(heads):
                ct_s[h] = c0_ref[0, h].T
            n_s[...] = n0_ref[0]
            m_s[...] = m0_ref[0]

    x = x_ref[0]
    hx = _rms(x, g0_ref[...]).astype(BF16)
    w = V7X_MXU
    gacc = jnp.zeros((lc, bg_ref.shape[1]), F32) + bg_ref[...]
    nslab = inner // w
    col = lambda i: slice(i * w, (i + 1) * w)

    def project(i, xc, xmb, gacc):
        q_s[0, :, col(i)] = _dot(xc, wq_ref[i]).astype(BF16)
        kt_s[0, col(i), :] = _dot_nt(wkt_ref[i], xc).astype(BF16)
        v_s[0, :, col(i)] = _dot(xmb, wv_ref[i]).astype(BF16)
        xc_s[:, col(i)] = xc
        return gacc + _dot(xc, gc_ref[col(i), :]) + _dot(xmb, gm_ref[col(i), :])

    xm_next = _dot(hx, wup_ref[:, col(0)])
    pending = None
    for i in range(nslab):
        xm = xm_next
        if i + 1 < nslab:
            xm_next = _dot(hx, wup_ref[:, col(i + 1)])
        if pending is not None:
            gacc = project(*pending, gacc)
        prev = carry[:, col(i)]
        carry[:, col(i)] = xm[lc - pad:lc, :]
        xc = _silu(_causal_dwconv(xm, prev, cw_ref, cb_ref, col(i))).astype(BF16)
        pending = (i, xc, xm.astype(BF16))
    gacc = project(*pending, gacc)
    nbuf_ref[0] = carry[pad - (width - 1):pad, :]

    slabs = iter(range(inner // w))

    def gate_slab():
        i = next(slabs, None)
        if i is not None:
            cs = slice(i * w, (i + 1) * w)
            sz_s[:, cs] = _silu(_dot(hx, wup_ref[:, inner + i * w:inner + (i + 1) * w])).astype(BF16)

    hcols = lambda h: slice(h * dh, (h + 1) * dh)
    down = []

    def gated_out(h, hn):
        hs = hn * ng_ref[:, hcols(h)] + sk_ref[:, hcols(h)] * xc_s[:, hcols(h)].astype(F32)
        act_s[:, hcols(h)] = (hs * sz_s[:, hcols(h)].astype(F32)).astype(BF16)

    def down_proj(h):
        if h == 0:
            down.append(_dot(act_s[...], wdown_ref[...]))

    _mlstm_chunk(q_s, kt_s, v_s, gacc, gacc.T[0:2 * heads, :], ct_s, n_s, m_s, gate_slab, gated_out, down_proj,
                 lc=lc, heads=heads, dh=dh)
    for _ in slabs:
        raise AssertionError("fewer fill points than output-gate slabs")
    o_ref[0] = x + _rms(down[0], g1_ref[...])

    @pl.when(c == pl.num_programs(1) - 1)
    def _():
        for h in range(heads):
            cout_ref[0, h] = ct_s[h].T
        nout_ref[0] = n_s[...]
        mout_ref[0] = m_s[...]


def _mlstm_layer(x, cbuf, state, p, g0, g1, lc):
    b, s, d = x.shape
    heads = p['heads']
    inner = p['m_cw'].shape[1]
    width = p['m_cw'].shape[0]
    dh = inner // heads
    zero_init = state is None
    kern = functools.partial(_mlstm_layer_kernel, lc=lc, heads=heads, inner=inner, width=width,
                             zero_init=zero_init)
    xs = pl.BlockSpec((1, lc, d), lambda i, c: (i, c, 0))
    bufs = pl.BlockSpec((1, width - 1, inner), lambda i, c: (i, 0, 0))
    cs = pl.BlockSpec((1, heads, dh, dh), lambda i, c: (i, 0, 0, 0))
    ns = pl.BlockSpec((1, heads, dh), lambda i, c: (i, 0, 0))
    ms = pl.BlockSpec((1, heads, V7X_LANES), lambda i, c: (i, 0, 0))
    consts = [g0, p['m_wup'], p['m_cw'], p['m_cb'], p['m_wq'], p['m_wkt'], p['m_wv'], p['m_gc'], p['m_gm'],
              p['m_bg'], p['m_ng'], p['m_skip'], p['m_wdown'], g1]
    in_specs = [xs, bufs] + [_const_spec(a.shape) for a in consts]
    args = [x, cbuf] + consts
    if not zero_init:
        in_specs += [cs, ns, ms]
        args += list(state)
    blocks = 2 * _nbytes((lc, d), F32) + (1 if zero_init else 2) * _nbytes((heads, dh, dh), F32)
    scratch = (sum(_nbytes(a.shape, a.dtype) for a in consts) + 6 * _nbytes((lc, inner), BF16)
               + _nbytes((heads, dh, dh), F32) + 8 * _nbytes((lc, max(lc, dh)), F32) + 2 * _nbytes((dh, dh), F32))
    return pl.pallas_call(
        kern,
        grid=(b, s // lc),
        in_specs=in_specs,
        out_specs=[xs, cs, ns, ms, bufs],
        out_shape=[jax.ShapeDtypeStruct((b, s, d), F32), jax.ShapeDtypeStruct((b, heads, dh, dh), F32),
                   jax.ShapeDtypeStruct((b, heads, dh), F32), jax.ShapeDtypeStruct((b, heads, V7X_LANES), F32),
                   jax.ShapeDtypeStruct((b, width - 1, inner), F32)],
        scratch_shapes=[pltpu.VMEM((V7X_SUBLANES, inner), F32),
                        pltpu.VMEM((1, lc, inner), BF16), pltpu.VMEM((1, inner, lc), BF16),
                        pltpu.VMEM((1, lc, inner), BF16), pltpu.VMEM((lc, inner), BF16),
                        pltpu.VMEM((lc, inner), BF16), pltpu.VMEM((lc, inner), BF16),
                        pltpu.VMEM((heads, dh, dh), F32), pltpu.VMEM((heads, dh), F32),
                        pltpu.VMEM((heads, V7X_LANES), F32)],
        compiler_params=pltpu.CompilerParams(
            dimension_semantics=("arbitrary", "arbitrary"), vmem_limit_bytes=_vmem_limit(blocks, scratch)),
        name="mlstm_layer",
    )(*args)


def _block_diag_tiles(w):
    nblk, k, _ = w.shape
    per = V7X_MXU // k
    wt = w.reshape(nblk // per, per, k, k)
    eye = jnp.eye(per, dtype=w.dtype)
    full = wt[:, :, :, None, :] * eye[None, :, None, :, None]
    return full.reshape(nblk // per, per * k, per * k)


def _prep_weights(w):
    row = lambda v: v.astype(F32).reshape(1, -1)
    p = {}
    p['norm_g'] = w['norm_g'].astype(F32)
    p['s5_mats'] = _s5_matrices(w['s5_A_re'][0], w['s5_A_im'][0], w['s5_log_dt'][0], w['s5_B_re'][0],
                                w['s5_B_im'][0], w['s5_C_re'][0], w['s5_C_im'][0])
    p['s5_D'] = row(w['s5_D'][0])
    p['s5_w_glu'] = w['s5_w_glu'][0].astype(BF16)
    inner = w['mlstm_conv_w'].shape[2]
    heads = w['mlstm_b_gate'].shape[1] // 2
    dh = inner // heads
    wq_t = _block_diag_tiles(w['mlstm_wq'][0].astype(F32))
    wk_t = _block_diag_tiles(w['mlstm_wk'][0].astype(F32))
    wv_t = _block_diag_tiles(w['mlstm_wv'][0].astype(F32))
    gc, gm = _gate_fold(wq_t, wk_t, wv_t, w['mlstm_w_gate'][0].astype(F32))
    p['m_wq'] = wq_t.astype(BF16)
    p['m_wkt'] = (jnp.swapaxes(wk_t, 1, 2) * (dh ** -0.5)).astype(BF16)
    p['m_wv'] = wv_t.astype(BF16)
    lane_pad = lambda a: jnp.pad(a, ((0, 0), (0, V7X_LANES - a.shape[1])))
    p['m_gc'] = lane_pad(gc).astype(BF16)
    p['m_gm'] = lane_pad(gm).astype(BF16)
    p['m_bg'] = lane_pad(row(w['mlstm_b_gate'][0]))
    p['m_wup'] = w['mlstm_w_up'][0].astype(BF16)
    p['m_cw'] = w['mlstm_conv_w'][0].astype(F32)
    p['m_cb'] = row(w['mlstm_conv_b'][0])
    p['m_ng'] = row(w['mlstm_norm_g'][0])
    p['m_skip'] = row(w['mlstm_skip'][0])
    p['m_wdown'] = w['mlstm_w_down'][0].astype(BF16)
    p['heads'] = heads
    p['f_wup'] = w['ffn_w_up'].astype(BF16)
    p['f_cw'] = w['ffn_conv_w'].astype(F32)
    p['f_cb'] = w['ffn_conv_b'].astype(F32)
    p['f_wdown'] = w['ffn_w_down'].astype(BF16)
    return p


def _to_chunk_major(u2d, b, s):
    d = u2d.shape[1]
    g = d // S5_GROUP_CH
    u = u2d.reshape(b, s // S5_CHUNK, S5_CHUNK, g, S5_GROUP_CH)
    return jnp.transpose(u, (1, 0, 3, 2, 4)).reshape((s // S5_CHUNK) * b, d * S5_CHUNK)


def _to_token_major(yt, b, s):
    d = yt.shape[1] // S5_CHUNK
    g = d // S5_GROUP_CH
    y = yt.reshape(s // S5_CHUNK, b, g, S5_CHUNK, S5_GROUP_CH)
    return jnp.transpose(y, (1, 0, 3, 2, 4)).reshape(b * s, d)


def _trunk(x, st, p, tiles):
    b, s, d = x.shape
    ng = p['norm_g']
    g_of = lambda layer, k: ng[layer, k].reshape(1, d)
    zero_init = st is None
    groups = d // S5_GROUP_CH
    x2d = x.reshape(b * s, d)

    n_state = p['s5_mats']['row'][1].shape[2] // 2
    if zero_init:
        h0re = h0im = jnp.zeros((groups // 2, b, 2 * n_state), F32)
    else:
        pair = lambda h: jnp.transpose(h.astype(F32).reshape(b, groups // 2, 2 * n_state), (1, 0, 2))
        h0re, h0im = pair(st['s5_re']), pair(st['s5_im'])
    if tiles.s5_time:
        y, hre, him = _s5_fused(x, g_of(0, 0), p['s5_mats']['col'], h0re, h0im, tiles.s5_time)
        y2d = y.reshape(b * s, d)
    else:
        u = _norm_cast(x2d, g_of(0, 0), tiles.rows)
        yt, hre, him = _s5_core(_to_chunk_major(u, b, s), p['s5_mats']['row'], h0re, h0im, b, tiles.s5_rows)
        y2d = _to_token_major(yt, b, s)
    unpair = lambda h: jnp.transpose(h, (1, 0, 2)).reshape(1, b, groups, n_state)
    o_re, o_im = unpair(hre), unpair(him)
    x2d = _glu(x2d, y2d, g_of(0, 0), p['s5_D'], p['s5_w_glu'], g_of(0, 1), tiles.rows)

    def ffn(x2d, layer):
        dff2 = p['f_wup'].shape[2]
        width = p['f_cw'].shape[1]
        buf = jnp.zeros((b, width - 1, dff2), F32) if zero_init else st['ffn_conv'][layer].astype(F32)
        xo, nbuf = _ffn(x2d.reshape(b, s, d), buf, g_of(layer, 2), p['f_wup'][layer], p['f_cw'][layer],
                        p['f_cb'][layer].reshape(1, dff2), p['f_wdown'][layer], g_of(layer, 3), tiles.ffn_time)
        return xo.reshape(b * s, d), nbuf

    x2d, fbuf0 = ffn(x2d, 0)

    heads = p['heads']
    inner = p['m_cw'].shape[1]
    width = p['m_cw'].shape[0]
    cbuf = jnp.zeros((b, width - 1, inner), F32) if zero_init else st['mlstm_conv'].astype(F32)
    if zero_init:
        state = None
    else:
        state = (st['mlstm_C'].astype(F32), st['mlstm_n'].astype(F32),
                 jnp.broadcast_to(st['mlstm_m'].astype(F32)[:, :, None], (b, heads, V7X_LANES)))
    x3, c_out, n_out, m_out, ncbuf = _mlstm_layer(x2d.reshape(b, s, d), cbuf, state, p, g_of(1, 0), g_of(1, 1),
                                                  tiles.mlstm_chunk)
    x2d, fbuf1 = ffn(x3.reshape(b * s, d), 1)

    return (x2d.reshape(b, s, d), o_re, o_im, c_out[None], n_out[None], m_out[None, :, :, 0], ncbuf[None],
            jnp.stack([fbuf0, fbuf1], axis=0))


def kernel(x_prompt, x_sample, state_s5_re, state_s5_im, state_mlstm_C, state_mlstm_n, state_mlstm_m, state_mlstm_conv, state_ffn_conv, norm_g, s5_A_re, s5_A_im, s5_log_dt, s5_B_re, s5_B_im, s5_C_re, s5_C_im, s5_D, s5_w_glu, mlstm_w_up, mlstm_conv_w, mlstm_conv_b, mlstm_wq, mlstm_wk, mlstm_wv, mlstm_w_gate, mlstm_b_gate, mlstm_norm_g, mlstm_skip, mlstm_w_down, ffn_w_up, ffn_conv_w, ffn_conv_b, ffn_w_down):
    w = {
        'norm_g': norm_g, 's5_A_re': s5_A_re, 's5_A_im': s5_A_im, 's5_log_dt': s5_log_dt,
        's5_B_re': s5_B_re, 's5_B_im': s5_B_im, 's5_C_re': s5_C_re, 's5_C_im': s5_C_im, 's5_D': s5_D,
        's5_w_glu': s5_w_glu, 'mlstm_w_up': mlstm_w_up, 'mlstm_conv_w': mlstm_conv_w,
        'mlstm_conv_b': mlstm_conv_b, 'mlstm_wq': mlstm_wq, 'mlstm_wk': mlstm_wk, 'mlstm_wv': mlstm_wv,
        'mlstm_w_gate': mlstm_w_gate, 'mlstm_b_gate': mlstm_b_gate, 'mlstm_norm_g': mlstm_norm_g,
        'mlstm_skip': mlstm_skip, 'mlstm_w_down': mlstm_w_down, 'ffn_w_up': ffn_w_up,
        'ffn_conv_w': ffn_conv_w, 'ffn_conv_b': ffn_conv_b, 'ffn_w_down': ffn_w_down,
    }
    assert norm_g.shape[0] == 2 and s5_A_re.shape[0] == 1 and mlstm_w_up.shape[0] == 1
    assert x_prompt.shape[0] == V7X_SUBLANES and x_sample.shape[0] == V7X_SUBLANES
    p = _prep_weights(w)
    out_p = _trunk(x_prompt.astype(F32), None, p, _pick_tiles(*x_prompt.shape[:2]))
    st = {'s5_re': state_s5_re[0], 's5_im': state_s5_im[0], 'mlstm_C': state_mlstm_C[0],
          'mlstm_n': state_mlstm_n[0], 'mlstm_m': state_mlstm_m[0], 'mlstm_conv': state_mlstm_conv[0],
          'ffn_conv': state_ffn_conv}
    out_s = _trunk(x_sample.astype(F32), st, p, _pick_tiles(*x_sample.shape[:2]))
    return (out_p[0], out_s[0]) + tuple(out_p[1:]) + tuple(out_s[1:])
```

```python
import functools
import math
from typing import NamedTuple

import jax
import jax.numpy as jnp
from jax import lax
from jax.experimental import pallas as pl
from jax.experimental.pallas import tpu as pltpu

F32 = jnp.float32
BF16 = jnp.bfloat16
HI = lax.Precision.HIGHEST

NORM_EPS = 1e-6
LN_EPS = 1e-5
S5_GROUP_CH = 16
S5_CHUNK = 16
QKV_BLOCK = 4

V7X_LANES = 128
V7X_SUBLANES = 8
V7X_MXU = 256
V7X_VMEM_BYTES = 64 * 1024 * 1024


class Tiles(NamedTuple):
    rows: int
    time: int
    s5_rows: int
    mlstm_chunk: int
    s5_time: int
    ffn_time: int


def _pick_tiles(batch, seq):
    s5_time = (V7X_LANES // batch) * S5_CHUNK
    return Tiles(rows=min(batch * seq, 512), time=min(seq, 512),
                 s5_rows=min(batch * (seq // S5_CHUNK), 512), mlstm_chunk=min(seq, 256),
                 s5_time=s5_time if seq % s5_time == 0 else 0, ffn_time=min(seq, 1024))


def _vmem_limit(block_bytes, scratch_bytes):
    want = 2 * block_bytes + scratch_bytes + 16 * 1024 * 1024
    return int(min(want, V7X_VMEM_BYTES - 8 * 1024 * 1024))


def _nbytes(shape, dtype):
    return math.prod(shape) * jnp.dtype(dtype).itemsize


def _rms(x, g):
    return x * lax.rsqrt(jnp.mean(x * x, axis=-1, keepdims=True) + NORM_EPS) * g


def _dot(a, b):
    return jnp.dot(a, b, preferred_element_type=F32)


def _dot_nt(a, b):
    return lax.dot_general(a, b, (((1,), (1,)), ((), ())), preferred_element_type=F32)


def _silu(x):
    return (0.5 * x) * (1.0 + jnp.tanh(0.5 * x))


def _causal_dwconv(u, prev, cw_ref, cb_ref, cs):
    width = cw_ref.shape[0]
    acc = cb_ref[:, cs] + u * cw_ref[width - 1:width, cs]
    rows = lax.broadcasted_iota(jnp.int32, prev.shape, 0)
    for j in range(width - 1):
        d = width - 1 - j
        rolled = pltpu.roll(u, d, 0)
        head = jnp.where(rows < d, pltpu.roll(prev, d, 0), rolled[0:V7X_SUBLANES])
        shifted = jnp.concatenate([head, rolled[V7X_SUBLANES:]], axis=0)
        acc = acc + shifted * cw_ref[j:j + 1, cs]
    return acc


def _log_sigmoid(x):
    return jnp.minimum(x, 0.0) - jnp.log1p(jnp.exp(-jnp.abs(x)))


def _split3(x):
    hi = x.astype(BF16)
    r1 = x - hi.astype(F32)
    mid = r1.astype(BF16)
    lo = (r1 - mid.astype(F32)).astype(BF16)
    return hi, mid, lo


def _norm_cast_kernel(x_ref, g_ref, o_ref):
    o_ref[...] = _rms(x_ref[...], g_ref[...]).astype(o_ref.dtype)


def _norm_cast(x2d, g, tb):
    rows, d = x2d.shape
    return pl.pallas_call(
        _norm_cast_kernel,
        grid=(rows // tb,),
        in_specs=[pl.BlockSpec((tb, d), lambda i: (i, 0)), pl.BlockSpec((1, d), lambda i: (0, 0))],
        out_specs=pl.BlockSpec((tb, d), lambda i: (i, 0)),
        out_shape=jax.ShapeDtypeStruct((rows, d), BF16),
        name="s5_norm",
    )(x2d, g)


def _s5_kernel(u_ref, t_ref, pre_ref, pim_ref, qre_ref, qim_ref, are_ref, aim_ref, h0re_ref, h0im_ref,
               y_ref, hre_ref, him_ref, pure_s, puim_s, hinre_s, hinim_s, stre_s, stim_s, *, nb, rb):
    r = pl.program_id(1)

    @pl.when(r == 0)
    def _():
        stre_s[...] = h0re_ref[0]
        stim_s[...] = h0im_ref[0]

    u = u_ref[...]
    pure_s[...] = _dot(u, pre_ref[0])
    puim_s[...] = _dot(u, pim_ref[0])
    ar = jnp.broadcast_to(are_ref[0], (nb, 2 * 64))
    ai = jnp.broadcast_to(aim_ref[0], (nb, 2 * 64))

    def step(i, carry):
        re, im = carry
        rows = pl.ds(pl.multiple_of(i * nb, nb), nb)
        hinre_s[rows, :] = re
        hinim_s[rows, :] = im
        return (ar * re - ai * im + pure_s[rows, :], ar * im + ai * re + puim_s[rows, :])

    re, im = lax.fori_loop(0, rb // nb, step, (stre_s[...], stim_s[...]))
    stre_s[...] = re
    stim_s[...] = im
    hre_ref[0] = re
    him_ref[0] = im

    carry_in = _dot(hinre_s[...].astype(BF16), qre_ref[0]) + _dot(hinim_s[...].astype(BF16), qim_ref[0])
    w = V7X_MXU
    for k in range(2):
        y_ref[:, k * w:(k + 1) * w] = (_dot(u[:, k * w:(k + 1) * w], t_ref[k])
                                       + carry_in[:, k * w:(k + 1) * w]).astype(y_ref.dtype)


def _s5_core(ut, mats, h0re, h0im, nb, rb):
    rows, cols = ut.shape
    pairs = cols // (2 * V7X_MXU)
    tm, pre, pim, qre, qim, are, aim = mats
    pair3 = lambda p, r: (p, 0, 0)
    kern = functools.partial(_s5_kernel, nb=nb, rb=rb)
    blocks = (_nbytes((rb, 512), BF16) * 2 + _nbytes((2, 256, 256), BF16) + 4 * _nbytes((512, 128), BF16))
    scratch = 4 * _nbytes((rb, 128), F32) + 2 * _nbytes((nb, 128), F32)
    return pl.pallas_call(
        kern,
        grid=(pairs, rows // rb),
        in_specs=[
            pl.BlockSpec((rb, 2 * V7X_MXU), lambda p, r: (r, p)),
            pl.BlockSpec((2, V7X_MXU, V7X_MXU), pair3),
            pl.BlockSpec((1, 2 * V7X_MXU, 128), pair3),
            pl.BlockSpec((1, 2 * V7X_MXU, 128), pair3),
            pl.BlockSpec((1, 128, 2 * V7X_MXU), pair3),
            pl.BlockSpec((1, 128, 2 * V7X_MXU), pair3),
            pl.BlockSpec((1, 1, 128), pair3),
            pl.BlockSpec((1, 1, 128), pair3),
            pl.BlockSpec((1, nb, 128), pair3),
            pl.BlockSpec((1, nb, 128), pair3),
        ],
        out_specs=[
            pl.BlockSpec((rb, 2 * V7X_MXU), lambda p, r: (r, p)),
            pl.BlockSpec((1, nb, 128), pair3),
            pl.BlockSpec((1, nb, 128), pair3),
        ],
        out_shape=[
            jax.ShapeDtypeStruct((rows, cols), BF16),
            jax.ShapeDtypeStruct((pairs, nb, 128), F32),
            jax.ShapeDtypeStruct((pairs, nb, 128), F32),
        ],
        scratch_shapes=[pltpu.VMEM((rb, 128), F32)] * 4 + [pltpu.VMEM((nb, 128), F32)] * 2,
        compiler_params=pltpu.CompilerParams(
            dimension_semantics=("arbitrary", "arbitrary"), vmem_limit_bytes=_vmem_limit(blocks, scratch)),
        name="s5_core",
    )(ut, tm, pre, pim, qre, qim, are, aim, h0re, h0im)


def _s5_fused_kernel(x_ref, g_ref, tt_ref, pt_ref, qt_ref, are_ref, aim_ref, h0re_ref, h0im_ref,
                     y_ref, hre_ref, him_ref,
                     us, ugt, pure, puim, hinre, hinim, ytmp, ys, stre, stim, *, nb, tbt):
    lc = S5_CHUNK
    ch = S5_GROUP_CH
    gps = V7X_LANES // ch
    nch = tbt // lc
    cols = nb * nch
    k = pl.program_id(1)

    @pl.when(k == 0)
    def _():
        for b in range(nb):
            u = _rms(x_ref[b], g_ref[...])
            for kk in range(us.shape[0]):
                us[kk, pl.ds(b, tbt, stride=nb), :] = u[:, kk * V7X_LANES:(kk + 1) * V7X_LANES]

    @pl.when(pl.program_id(0) == 0)
    def _():
        stre[k] = h0re_ref[...]
        stim[k] = h0im_ref[...]

    tile = lambda j, s: slice((j * lc + s) * nb, (j * lc + s + 1) * nb)
    for s in range(lc):
        a = jnp.concatenate([us[k, tile(j, s), :] for j in range(nch)], axis=0).T
        for g in range(gps):
            ugt[g, s * ch:(s + 1) * ch, :] = a[g * ch:(g + 1) * ch, :].astype(BF16)

    half = pt_ref.shape[1] // 2
    for pr in range(gps // 2):
        put0 = _dot(pt_ref[2 * pr], ugt[2 * pr])
        put1 = _dot(pt_ref[2 * pr + 1], ugt[2 * pr + 1])
        pure[pr] = jnp.concatenate([put0[:half], put1[:half]], axis=0).T
        puim[pr] = jnp.concatenate([put0[half:], put1[half:]], axis=0).T

    for pr in range(gps // 2):
        re = stre[k, pr]
        im = stim[k, pr]
        ar = jnp.broadcast_to(are_ref[pr], re.shape)
        ai = jnp.broadcast_to(aim_ref[pr], re.shape)
        for j in range(nch):
            rows = slice(j * nb, (j + 1) * nb)
            hinre[pr, rows, :] = re
            hinim[pr, rows, :] = im
            re, im = (ar * re - ai * im + pure[pr, rows, :], ar * im + ai * re + puim[pr, rows, :])
        stre[k, pr] = re
        stim[k, pr] = im
        hre_ref[k * (gps // 2) + pr] = re
        him_ref[k * (gps // 2) + pr] = im

    for pr in range(gps // 2):
        hre_t = hinre[pr].T
        him_t = hinim[pr].T
        for i in range(2):
            g = 2 * pr + i
            hin_t = jnp.concatenate([hre_t[i * half:(i + 1) * half], him_t[i * half:(i + 1) * half]],
                                    axis=0).astype(BF16)
            ytmp[g] = _dot(tt_ref[g], ugt[g]) + _dot(qt_ref[g], hin_t)

    for t in range(lc):
        zt = jnp.concatenate([ytmp[g, t * ch:(t + 1) * ch, :] for g in range(gps)], axis=0).T
        for j in range(nch):
            ys[tile(j, t), :] = zt[j * nb:(j + 1) * nb, :]
    for b in range(nb):
        y_ref[b] = ys[pl.ds(b, tbt, stride=nb), :].astype(y_ref.dtype)


def _s5_fused(x, g, mats_t, h0re, h0im, tbt):
    nb, s, d = x.shape
    tt, pt, qt, are, aim = mats_t
    groups = tt.shape[0]
    gps = V7X_LANES // S5_GROUP_CH
    slabs = groups // gps
    w = S5_CHUNK * S5_GROUP_CH
    nst = pt.shape[1]
    cols = nb * (tbt // S5_CHUNK)
    kern = functools.partial(_s5_fused_kernel, nb=nb, tbt=tbt)
    slab3 = lambda t, k: (k, 0, 0)
    blocks = (_nbytes((nb, tbt, d), F32) + _nbytes((nb, tbt, V7X_LANES), BF16)
              + gps * (_nbytes((w, w), BF16) + 2 * _nbytes((w, nst), BF16)))
    scratch = (_nbytes((slabs, nb * tbt, V7X_LANES), F32) + _nbytes((gps, w, cols), BF16)
               + 4 * _nbytes((gps // 2, cols, nst), F32) + _nbytes((gps, w, cols), F32)
               + _nbytes((nb * tbt, V7X_LANES), F32))
    return pl.pallas_call(
        kern,
        grid=(s // tbt, slabs),
        in_specs=[
            pl.BlockSpec((nb, tbt, d), lambda t, k: (0, t, 0)),
            pl.BlockSpec((1, d), lambda t, k: (0, 0)),
            pl.BlockSpec((gps, w, w), slab3),
            pl.BlockSpec((gps, nst, w), slab3),
            pl.BlockSpec((gps, w, nst), slab3),
            pl.BlockSpec((gps // 2, 1, nst), slab3),
            pl.BlockSpec((gps // 2, 1, nst), slab3),
            pl.BlockSpec((gps // 2, nb, nst), slab3),
            pl.BlockSpec((gps // 2, nb, nst), slab3),
        ],
        out_specs=[
            pl.BlockSpec((nb, tbt, V7X_LANES), lambda t, k: (0, t, k)),
            pl.BlockSpec((groups // 2, nb, nst), lambda t, k: (0, 0, 0)),
            pl.BlockSpec((groups // 2, nb, nst), lambda t, k: (0, 0, 0)),
        ],
        out_shape=[
            jax.ShapeDtypeStruct((nb, s, d), BF16),
            jax.ShapeDtypeStruct((groups // 2, nb, nst), F32),
            jax.ShapeDtypeStruct((groups // 2, nb, nst), F32),
        ],
        scratch_shapes=[
            pltpu.VMEM((slabs, nb * tbt, V7X_LANES), F32),
            pltpu.VMEM((gps, w, cols), BF16),
            pltpu.VMEM((gps // 2, cols, nst), F32),
            pltpu.VMEM((gps // 2, cols, nst), F32),
            pltpu.VMEM((gps // 2, cols, nst), F32),
            pltpu.VMEM((gps // 2, cols, nst), F32),
            pltpu.VMEM((gps, w, cols), F32),
            pltpu.VMEM((nb * tbt, V7X_LANES), F32),
            pltpu.VMEM((slabs, gps // 2, nb, nst), F32),
            pltpu.VMEM((slabs, gps // 2, nb, nst), F32),
        ],
        compiler_params=pltpu.CompilerParams(
            dimension_semantics=("arbitrary", "arbitrary"), vmem_limit_bytes=_vmem_limit(blocks, scratch)),
        name="s5_fused",
    )(x, g, tt, pt, qt, are, aim, h0re, h0im)


def _s5_matrices(a_re, a_im, log_dt, b_re, b_im, c_re, c_im):
    lc = S5_CHUNK
    groups, n = a_re.shape
    ch = S5_GROUP_CH
    a = lax.complex(a_re.astype(F32), a_im.astype(F32))
    adt = a * jnp.exp(log_dt.astype(F32))[:, None]
    j = jnp.arange(lc + 1, dtype=F32)
    apow = jnp.exp(adt[None] * j[:, None, None])
    b_bar = ((apow[1] - 1.0) / a)[..., None] * lax.complex(b_re.astype(F32), b_im.astype(F32))
    c_mat = lax.complex(c_re.astype(F32), c_im.astype(F32))
    ca = c_mat[None] * apow[:, :, None, :]
    kern = (jnp.einsum('jgcn,gnd->jgcd', ca.real, b_bar.real, precision=HI)
            - jnp.einsum('jgcn,gnd->jgcd', ca.imag, b_bar.imag, precision=HI))
    s_idx = jnp.arange(lc)[:, None]
    t_idx = jnp.arange(lc)[None, :]
    lag = t_idx - s_idx
    kt = kern[jnp.clip(lag, 0, lc)]
    kt = jnp.where((lag >= 0)[:, :, None, None, None], kt, 0.0)
    tm = jnp.transpose(kt, (2, 0, 4, 1, 3)).reshape(groups, lc * ch, lc * ch)
    pw = apow[lc - 1 - jnp.arange(lc)]
    p = pw[:, :, :, None] * b_bar[None]
    p = jnp.transpose(p, (1, 0, 3, 2)).reshape(groups, lc * ch, n)
    q = ca[1:]
    q = jnp.transpose(q, (1, 3, 0, 2)).reshape(groups, n, lc * ch)

    def pair_rows(m):
        g2 = groups // 2
        m = m.reshape(g2, 2, m.shape[1], m.shape[2])
        z = jnp.zeros_like(m[:, 0])
        top = jnp.concatenate([m[:, 0], z], axis=2)
        bot = jnp.concatenate([z, m[:, 1]], axis=2)
        return jnp.concatenate([top, bot], axis=1)

    pre = pair_rows(p.real).astype(BF16)
    pim = pair_rows(p.imag).astype(BF16)
    qre = pair_rows(q.real).astype(BF16)
    qim = pair_rows(-q.imag).astype(BF16)
    a_chunk = apow[lc].reshape(groups // 2, 1, 2 * n)
    are, aim = a_chunk.real, a_chunk.imag
    tt = jnp.swapaxes(tm, 1, 2).astype(BF16)
    pt = jnp.concatenate([jnp.swapaxes(p.real, 1, 2), jnp.swapaxes(p.imag, 1, 2)], axis=1).astype(BF16)
    qt = jnp.concatenate([jnp.swapaxes(q.real, 1, 2), jnp.swapaxes(-q.imag, 1, 2)], axis=2).astype(BF16)
    return {'row': (tm.astype(BF16), pre, pim, qre, qim, are, aim), 'col': (tt, pt, qt, are, aim)}


def _glu_kernel(x_ref, y_ref, g0_ref, d_ref, w_ref, g1_ref, o_ref, out_s):
    x = x_ref[...]
    d = x.shape[1]
    u = _rms(x, g0_ref[...])
    yy = y_ref[...].astype(F32) + d_ref[...] * u
    z = jax.nn.gelu(yy, approximate=True).astype(BF16)
    w = V7X_MXU
    nslab = d // w
    pair = lambda j: (_dot(z, w_ref[:, j * w:(j + 1) * w]), _dot(z, w_ref[:, d + j * w:d + (j + 1) * w]))
    nxt = pair(0)
    ssq = jnp.zeros((x.shape[0], 1), F32)
    for j in range(nslab):
        a, b = nxt
        if j + 1 < nslab:
            nxt = pair(j + 1)
        out = a * (0.5 * (1.0 + jnp.tanh(0.5 * b)))
        out_s[:, j * w:(j + 1) * w] = out
        ssq = ssq + jnp.sum(out * out, axis=-1, keepdims=True)
    scale = lax.rsqrt(ssq * (1.0 / d) + NORM_EPS)
    o_ref[...] = x + out_s[...] * scale * g1_ref[...]


def _const_spec(shape):
    nd = len(shape)
    return pl.BlockSpec(shape, lambda *_: (0,) * nd, pipeline_mode=pl.Buffered(1))


def _glu(x2d, y2d, g0, dskip, w, g1, tb):
    rows, d = x2d.shape
    blocks = 2 * _nbytes((tb, d), F32) + _nbytes((tb, d), BF16)
    scratch = _nbytes(w.shape, BF16) + 3 * _nbytes((tb, 2 * d), F32)
    row = pl.BlockSpec((tb, d), lambda i: (i, 0))
    return pl.pallas_call(
        _glu_kernel,
        grid=(rows // tb,),
        in_specs=[row, row, _const_spec((1, d)), _const_spec((1, d)), _const_spec(w.shape), _const_spec((1, d))],
        out_specs=row,
        out_shape=jax.ShapeDtypeStruct((rows, d), F32),
        scratch_shapes=[pltpu.VMEM((tb, d), F32)],
        compiler_params=pltpu.CompilerParams(
            dimension_semantics=("arbitrary",), vmem_limit_bytes=_vmem_limit(blocks, scratch)),
        name="s5_glu",
    )(x2d, y2d, g0, dskip, w, g1)


def _ffn_kernel(x_ref, buf_ref, g2_ref, wup_ref, cw_ref, cb_ref, wdown_ref, g3_ref,
                o_ref, nbuf_ref, carry, act, *, tb, dff, width):
    t = pl.program_id(1)
    pad = V7X_SUBLANES

    @pl.when(t == 0)
    def _():
        carry[...] = jnp.zeros_like(carry)
        carry[pad - (width - 1):pad, :] = buf_ref[0]

    x = x_ref[0]
    hn = _rms(x, g2_ref[...]).astype(BF16)
    fb = V7X_MXU

    def conv(u, cs):
        prev = carry[:, cs]
        carry[:, cs] = u[tb - pad:tb, :]
        return _causal_dwconv(u, prev, cw_ref, cb_ref, cs)

    nslab = dff // fb
    cols = lambda j: (slice(j * fb, (j + 1) * fb), slice(dff + j * fb, dff + (j + 1) * fb))
    up = lambda j: tuple(_dot(hn, wup_ref[:, cs]) for cs in cols(j))
    nxt = up(0)
    for j in range(nslab):
        ug, uv = nxt
        if j + 1 < nslab:
            nxt = up(j + 1)
        gate = conv(ug, cols(j)[0])
        val = conv(uv, cols(j)[1])
        act[:, j * fb:(j + 1) * fb] = (jax.nn.gelu(gate, approximate=True) * val).astype(BF16)

    f = _dot(act[...], wdown_ref[...])
    o_ref[0] = x + _rms(f, g3_ref[...])
    nbuf_ref[0] = carry[pad - (width - 1):pad, :]


def _ffn(x, buf, g2, wup, cw, cb, wdown, g3, tb):
    b, s, d = x.shape
    dff = wdown.shape[0]
    width = cw.shape[0]
    kern = functools.partial(_ffn_kernel, tb=tb, dff=dff, width=width)
    blocks = 2 * _nbytes((tb, d), F32)
    scratch = (_nbytes(wup.shape, BF16) + _nbytes(wdown.shape, BF16) + 6 * _nbytes((tb, V7X_MXU), F32)
               + _nbytes((8, 2 * dff), F32) + _nbytes((tb, dff), BF16) + 2 * _nbytes((tb, d), F32))
    xs = pl.BlockSpec((1, tb, d), lambda i, t: (i, t, 0))
    bs = pl.BlockSpec((1, width - 1, 2 * dff), lambda i, t: (i, 0, 0))
    return pl.pallas_call(
        kern,
        grid=(b, s // tb),
        in_specs=[xs, bs, _const_spec((1, d)), _const_spec(wup.shape), _const_spec(cw.shape),
                  _const_spec((1, 2 * dff)), _const_spec(wdown.shape), _const_spec((1, d))],
        out_specs=[xs, bs],
        out_shape=[jax.ShapeDtypeStruct((b, s, d), F32), jax.ShapeDtypeStruct((b, width - 1, 2 * dff), F32)],
        scratch_shapes=[pltpu.VMEM((V7X_SUBLANES, 2 * dff), F32), pltpu.VMEM((tb, dff), BF16)],
        compiler_params=pltpu.CompilerParams(
            dimension_semantics=("arbitrary", "arbitrary"), vmem_limit_bytes=_vmem_limit(blocks, scratch)),
        name="conv_ffn",
    )(x, buf, g2, wup, cw, cb, wdown, g3)


def _gate_fold_kernel(wq_ref, wk_ref, wv_ref, gq_ref, gk_ref, gv_ref, gc_ref, gm_ref):
    dotp = functools.partial(jnp.dot, preferred_element_type=F32, precision=HI)
    gc_ref[...] = dotp(wq_ref[0], gq_ref[...]) + dotp(wk_ref[0], gk_ref[...])
    gm_ref[...] = dotp(wv_ref[0], gv_ref[...])


def _gate_fold(wq_t, wk_t, wv_t, w_gate):
    nt, w, _ = wq_t.shape
    inner = nt * w
    ng = w_gate.shape[1]
    tile = pl.BlockSpec((1, w, w), lambda i: (i, 0, 0))
    gs = [pl.BlockSpec((w, ng), lambda i, k=k: (k * nt + i, 0)) for k in range(3)]
    out = pl.BlockSpec((w, ng), lambda i: (i, 0))
    return pl.pallas_call(
        _gate_fold_kernel,
        grid=(nt,),
        in_specs=[tile, tile, tile] + gs,
        out_specs=[out, out],
        out_shape=[jax.ShapeDtypeStruct((inner, ng), F32)] * 2,
        name="mlstm_gate_fold",
    )(wq_t, wk_t, wv_t, w_gate, w_gate, w_gate)


def _mpre_kernel(x_ref, cbuf_ref, g_ref, wup_ref, cw_ref, cb_ref, wq_ref, wkt_ref, wv_ref, gc_ref, gm_ref,
                 bg_ref, q_ref, kt_ref, v_ref, xc_ref, z_ref, gates_ref, nbuf_ref, carry,
                 *, tb, inner, width):
    t = pl.program_id(1)
    pad = V7X_SUBLANES

    @pl.when(t == 0)
    def _():
        carry[...] = jnp.zeros_like(carry)
        carry[pad - (width - 1):pad, :] = cbuf_ref[0]

    h = _rms(x_ref[0], g_ref[...]).astype(BF16)
    w = V7X_MXU
    gacc = jnp.zeros((tb, bg_ref.shape[1]), F32) + bg_ref[...]
    for i in range(inner // w):
        cs = slice(i * w, (i + 1) * w)
        xm = _dot(h, wup_ref[:, cs])
        prev = carry[:, cs]
        carry[:, cs] = xm[tb - pad:tb, :]
        xc = _silu(_causal_dwconv(xm, prev, cw_ref, cb_ref, cs)).astype(BF16)
        xmb = xm.astype(BF16)
        q_ref[0, :, cs] = _dot(xc, wq_ref[i]).astype(BF16)
        kt_ref[0, cs, :] = _dot_nt(wkt_ref[i], xc).astype(BF16)
        v_ref[0, :, cs] = _dot(xmb, wv_ref[i]).astype(BF16)
        xc_ref[0, :, cs] = xc
        z_ref[0, :, cs] = _silu(_dot(h, wup_ref[:, inner + i * w:inner + (i + 1) * w])).astype(BF16)
        gacc = gacc + _dot(xc, gc_ref[cs, :]) + _dot(xmb, gm_ref[cs, :])
    gates_ref[0] = gacc
    nbuf_ref[0] = carry[pad - (width - 1):pad, :]


def _mpre(x, cbuf, g, wup, cw, cb, wq_t, wkt_t, wv_t, gc, gm, bg, tb):
    b, s, d = x.shape
    inner = cw.shape[1]
    width = cw.shape[0]
    ng = bg.shape[1]
    kern = functools.partial(_mpre_kernel, tb=tb, inner=inner, width=width)
    act = pl.BlockSpec((1, tb, inner), lambda i, t: (i, t, 0))
    blocks = _nbytes((tb, d), F32) + 5 * _nbytes((tb, inner), BF16)
    scratch = (_nbytes(wup.shape, BF16) + 3 * _nbytes(wq_t.shape, BF16) + 2 * _nbytes((inner, 128), BF16)
               + _nbytes((8, inner), F32) + 8 * _nbytes((tb, V7X_MXU), F32))
    return pl.pallas_call(
        kern,
        grid=(b, s // tb),
        in_specs=[pl.BlockSpec((1, tb, d), lambda i, t: (i, t, 0)),
                  pl.BlockSpec((1, width - 1, inner), lambda i, t: (i, 0, 0)),
                  _const_spec((1, d)), _const_spec(wup.shape), _const_spec(cw.shape), _const_spec((1, inner)),
                  _const_spec(wq_t.shape), _const_spec(wkt_t.shape), _const_spec(wv_t.shape),
                  _const_spec(gc.shape), _const_spec(gm.shape), _const_spec((1, ng))],
        out_specs=[act, pl.BlockSpec((1, inner, tb), lambda i, t: (i, 0, t)), act, act, act,
                   pl.BlockSpec((1, tb, ng), lambda i, t: (i, t, 0)),
                   pl.BlockSpec((1, width - 1, inner), lambda i, t: (i, 0, 0))],
        out_shape=[jax.ShapeDtypeStruct((b, s, inner), BF16), jax.ShapeDtypeStruct((b, inner, s), BF16),
                   jax.ShapeDtypeStruct((b, s, inner), BF16), jax.ShapeDtypeStruct((b, s, inner), BF16),
                   jax.ShapeDtypeStruct((b, s, inner), BF16), jax.ShapeDtypeStruct((b, s, ng), F32),
                   jax.ShapeDtypeStruct((b, width - 1, inner), F32)],
        scratch_shapes=[pltpu.VMEM((V7X_SUBLANES, inner), F32)],
        compiler_params=pltpu.CompilerParams(
            dimension_semantics=("arbitrary", "arbitrary"), vmem_limit_bytes=_vmem_limit(blocks, scratch)),
        name="mlstm_pre",
    )(x, cbuf, g, wup, cw, cb, wq_t, wkt_t, wv_t, gc, gm, bg)


def _mlstm_chunk(q_ref, kt_ref, v_ref, g, gt, ct_s, n_s, m_s, fill, emit_hn, fill_update, *, lc, heads, dh):
    row = lax.broadcasted_iota(jnp.int32, (lc, lc), 0)
    col = lax.broadcasted_iota(jnp.int32, (lc, lc), 1)
    causal = col <= row
    tri_l = jnp.where(causal, 1.0, 0.0).astype(BF16)
    tri_u = jnp.where(row <= col, 1.0, 0.0).astype(BF16)
    b_cols = sum(_dot(tri_l, part) for part in _split3(_log_sigmoid(g)))
    b_rows = sum(_dot(part, tri_u) for part in _split3(_log_sigmoid(gt)))

    hsl = [slice(h * dh, (h + 1) * dh) for h in range(heads)]
    st = []
    for h in range(heads):
        b_col = b_cols[:, heads + h:heads + h + 1]
        b_row = b_rows[heads + h:heads + h + 1, :]
        i_row = gt[h:h + 1, :]
        m_prev = m_s[h:h + 1, 0:1]
        dmat = jnp.where(causal, b_col - b_row + i_row, -jnp.inf)
        inter = b_col + m_prev
        m_t = jnp.maximum(inter, jnp.max(dmat, axis=1, keepdims=True))
        st.append(dict(b_col=b_col, b_row=b_row, i_row=i_row, m_prev=m_prev, m_t=m_t,
                       wts=jnp.exp(dmat - m_t), a=jnp.exp(inter - m_t)))
        fill()
    for h in range(heads):
        s = st[h]
        s['sc'] = _dot(q_ref[0, :, hsl[h]], kt_ref[0, hsl[h], :]) * s['wts']
    for h in range(heads):
        fill()
        s = st[h]
        qh = q_ref[0, :, hsl[h]]
        sc = s['sc']
        a = s['a']
        num = _dot(sc.astype(BF16), v_ref[0, :, hsl[h]]) + a * _dot(qh, ct_s[h].astype(BF16))
        qn = jnp.sum(qh.astype(F32) * n_s[h:h + 1, :], axis=1, keepdims=True)
        den = jnp.sum(sc, axis=1, keepdims=True) + a * qn
        hh = num / jnp.maximum(jnp.abs(den), jnp.exp(-s['m_t']))
        mu = jnp.mean(hh, axis=1, keepdims=True)
        dev = hh - mu
        var = jnp.mean(dev * dev, axis=1, keepdims=True)
        emit_hn(h, dev * lax.rsqrt(var + LN_EPS))
    for h in range(heads):
        fill_update(h)
        s = st[h]
        kth = kt_ref[0, hsl[h], :]
        b_last = s['b_row'][:, lc - 1:lc]
        g_row = b_last - s['b_row'] + s['i_row']
        m_new = jnp.maximum(b_last + s['m_prev'], jnp.max(g_row, axis=1, keepdims=True))
        decay = jnp.exp(b_last + s['m_prev'] - m_new)
        wg_row = jnp.exp(g_row - m_new)
        wg_col = jnp.exp(b_last - s['b_col'] + g[:, h:h + 1] - m_new)
        wv = (v_ref[0, :, hsl[h]].astype(F32) * wg_col).astype(BF16)
        ct_s[h] = decay * ct_s[h] + _dot(kth, wv)
        wg16 = jnp.broadcast_to(wg_row, (16, lc)).astype(BF16)
        n_s[h:h + 1, :] = decay * n_s[h:h + 1, :] + _dot_nt(wg16, kth)[0:1, :]
        m_s[h:h + 1, :] = jnp.broadcast_to(m_new, (1, V7X_LANES))


def _mlstm_layer_kernel(*refs, lc, heads, inner, width, zero_init):
    (x_ref, cbuf_ref, g0_ref, wup_ref, cw_ref, cb_ref, wq_ref, wkt_ref, wv_ref, gc_ref, gm_ref, bg_ref,
     ng_ref, sk_ref, wdown_ref, g1_ref) = refs[:16]
    n_in = 16
    if not zero_init:
        c0_ref, n0_ref, m0_ref = refs[16:19]
        n_in = 19
    o_ref, cout_ref, nout_ref, mout_ref, nbuf_ref = refs[n_in:n_in + 5]
    carry, q_s, kt_s, v_s, xc_s, sz_s, act_s, ct_s, n_s, m_s = refs[n_in + 5:]
    dh = inner // heads
    pad = V7X_SUBLANES
    c = pl.program_id(1)

    @pl.when(c == 0)
    def _():
        carry[...] = jnp.zeros_like(carry)
        carry[pad - (width - 1):pad, :] = cbuf_ref[0]
        if zero_init:
            ct_s[...] = jnp.zeros_like(ct_s)
            n_s[...] = jnp.zeros_like(n_s)
            m_s[...] = jnp.zeros_like(m_s)
        else:
            for h in range(heads):
                ct_s[h] = c0_ref[0, h].T
            n_s[...] = n0_ref[0]
            m_s[...] = m0_ref[0]

    x = x_ref[0]
    hx = _rms(x, g0_ref[...]).astype(BF16)
    w = V7X_MXU
    gacc = jnp.zeros((lc, bg_ref.shape[1]), F32) + bg_ref[...]
    nslab = inner // w
    col = lambda i: slice(i * w, (i + 1) * w)

    def project(i, xc, xmb, gacc):
        q_s[0, :, col(i)] = _dot(xc, wq_ref[i]).astype(BF16)
        kt_s[0, col(i), :] = _dot_nt(wkt_ref[i], xc).astype(BF16)
        v_s[0, :, col(i)] = _dot(xmb, wv_ref[i]).astype(BF16)
        xc_s[:, col(i)] = xc
        return gacc + _dot(xc, gc_ref[col(i), :]) + _dot(xmb, gm_ref[col(i), :])

    xm_next = _dot(hx, wup_ref[:, col(0)])
    pending = None
    for i in range(nslab):
        xm = xm_next
        if i + 1 < nslab:
            xm_next = _dot(hx, wup_ref[:, col(i + 1)])
        if pending is not None:
            gacc = project(*pending, gacc)
        prev = carry[:, col(i)]
        carry[:, col(i)] = xm[lc - pad:lc, :]
        xc = _silu(_causal_dwconv(xm, prev, cw_ref, cb_ref, col(i))).astype(BF16)
        pending = (i, xc, xm.astype(BF16))
    gacc = project(*pending, gacc)
    nbuf_ref[0] = carry[pad - (width - 1):pad, :]

    slabs = iter(range(inner // w))

    def gate_slab():
        i = next(slabs, None)
        if i is not None:
            cs = slice(i * w, (i + 1) * w)
            sz_s[:, cs] = _silu(_dot(hx, wup_ref[:, inner + i * w:inner + (i + 1) * w])).astype(BF16)

    hcols = lambda h: slice(h * dh, (h + 1) * dh)
    down = []

    def gated_out(h, hn):
        hs = hn * ng_ref[:, hcols(h)] + sk_ref[:, hcols(h)] * xc_s[:, hcols(h)].astype(F32)
        act_s[:, hcols(h)] = (hs * sz_s[:, hcols(h)].astype(F32)).astype(BF16)

    def down_proj(h):
        if h == 0:
            down.append(_dot(act_s[...], wdown_ref[...]))

    _mlstm_chunk(q_s, kt_s, v_s, gacc, gacc.T[0:2 * heads, :], ct_s, n_s, m_s, gate_slab, gated_out, down_proj,
                 lc=lc, heads=heads, dh=dh)
    for _ in slabs:
        raise AssertionError("fewer fill points than output-gate slabs")
    o_ref[0] = x + _rms(down[0], g1_ref[...])

    @pl.when(c == pl.num_programs(1) - 1)
    def _():
        for h in range(heads):
            cout_ref[0, h] = ct_s[h].T
        nout_ref[0] = n_s[...]
        mout_ref[0] = m_s[...]


def _mlstm_layer(x, cbuf, state, p, g0, g1, lc):
    b, s, d = x.shape
    heads = p['heads']
    inner = p['m_cw'].shape[1]
    width = p['m_cw'].shape[0]
    dh = inner // heads
    zero_init = state is None
    kern = functools.partial(_mlstm_layer_kernel, lc=lc, heads=heads, inner=inner, width=width,
                             zero_init=zero_init)
    xs = pl.BlockSpec((1, lc, d), lambda i, c: (i, c, 0))
    bufs = pl.BlockSpec((1, width - 1, inner), lambda i, c: (i, 0, 0))
    cs = pl.BlockSpec((1, heads, dh, dh), lambda i, c: (i, 0, 0, 0))
    ns = pl.BlockSpec((1, heads, dh), lambda i, c: (i, 0, 0))
    ms = pl.BlockSpec((1, heads, V7X_LANES), lambda i, c: (i, 0, 0))
    consts = [g0, p['m_wup'], p['m_cw'], p['m_cb'], p['m_wq'], p['m_wkt'], p['m_wv'], p['m_gc'], p['m_gm'],
              p['m_bg'], p['m_ng'], p['m_skip'], p['m_wdown'], g1]
    in_specs = [xs, bufs] + [_const_spec(a.shape) for a in consts]
    args = [x, cbuf] + consts
    if not zero_init:
        in_specs += [cs, ns, ms]
        args += list(state)
    blocks = 2 * _nbytes((lc, d), F32) + (1 if zero_init else 2) * _nbytes((heads, dh, dh), F32)
    scratch = (sum(_nbytes(a.shape, a.dtype) for a in consts) + 6 * _nbytes((lc, inner), BF16)
               + _nbytes((heads, dh, dh), F32) + 8 * _nbytes((lc, max(lc, dh)), F32) + 2 * _nbytes((dh, dh), F32))
    return pl.pallas_call(
        kern,
        grid=(b, s // lc),
        in_specs=in_specs,
        out_specs=[xs, cs, ns, ms, bufs],
        out_shape=[jax.ShapeDtypeStruct((b, s, d), F32), jax.ShapeDtypeStruct((b, heads, dh, dh), F32),
                   jax.ShapeDtypeStruct((b, heads, dh), F32), jax.ShapeDtypeStruct((b, heads, V7X_LANES), F32),
                   jax.ShapeDtypeStruct((b, width - 1, inner), F32)],
        scratch_shapes=[pltpu.VMEM((V7X_SUBLANES, inner), F32),
                        pltpu.VMEM((1, lc, inner), BF16), pltpu.VMEM((1, inner, lc), BF16),
                        pltpu.VMEM((1, lc, inner), BF16), pltpu.VMEM((lc, inner), BF16),
                        pltpu.VMEM((lc, inner), BF16), pltpu.VMEM((lc, inner), BF16),
                        pltpu.VMEM((heads, dh, dh), F32), pltpu.VMEM((heads, dh), F32),
                        pltpu.VMEM((heads, V7X_LANES), F32)],
        compiler_params=pltpu.CompilerParams(
            dimension_semantics=("arbitrary", "arbitrary"), vmem_limit_bytes=_vmem_limit(blocks, scratch)),
        name="mlstm_layer",
    )(*args)


def _block_diag_tiles(w):
    nblk, k, _ = w.shape
    per = V7X_MXU // k
    wt = w.reshape(nblk // per, per, k, k)
    eye = jnp.eye(per, dtype=w.dtype)
    full = wt[:, :, :, None, :] * eye[None, :, None, :, None]
    return full.reshape(nblk // per, per * k, per * k)


def _prep_weights(w):
    row = lambda v: v.astype(F32).reshape(1, -1)
    p = {}
    p['norm_g'] = w['norm_g'].astype(F32)
    p['s5_mats'] = _s5_matrices(w['s5_A_re'][0], w['s5_A_im'][0], w['s5_log_dt'][0], w['s5_B_re'][0],
                                w['s5_B_im'][0], w['s5_C_re'][0], w['s5_C_im'][0])
    p['s5_D'] = row(w['s5_D'][0])
    p['s5_w_glu'] = w['s5_w_glu'][0].astype(BF16)
    inner = w['mlstm_conv_w'].shape[2]
    heads = w['mlstm_b_gate'].shape[1] // 2
    dh = inner // heads
    wq_t = _block_diag_tiles(w['mlstm_wq'][0].astype(F32))
    wk_t = _block_diag_tiles(w['mlstm_wk'][0].astype(F32))
    wv_t = _block_diag_tiles(w['mlstm_wv'][0].astype(F32))
    gc, gm = _gate_fold(wq_t, wk_t, wv_t, w['mlstm_w_gate'][0].astype(F32))
    p['m_wq'] = wq_t.astype(BF16)
    p['m_wkt'] = (jnp.swapaxes(wk_t, 1, 2) * (dh ** -0.5)).astype(BF16)
    p['m_wv'] = wv_t.astype(BF16)
    lane_pad = lambda a: jnp.pad(a, ((0, 0), (0, V7X_LANES - a.shape[1])))
    p['m_gc'] = lane_pad(gc).astype(BF16)
    p['m_gm'] = lane_pad(gm).astype(BF16)
    p['m_bg'] = lane_pad(row(w['mlstm_b_gate'][0]))
    p['m_wup'] = w['mlstm_w_up'][0].astype(BF16)
    p['m_cw'] = w['mlstm_conv_w'][0].astype(F32)
    p['m_cb'] = row(w['mlstm_conv_b'][0])
    p['m_ng'] = row(w['mlstm_norm_g'][0])
    p['m_skip'] = row(w['mlstm_skip'][0])
    p['m_wdown'] = w['mlstm_w_down'][0].astype(BF16)
    p['heads'] = heads
    p['f_wup'] = w['ffn_w_up'].astype(BF16)
    p['f_cw'] = w['ffn_conv_w'].astype(F32)
    p['f_cb'] = w['ffn_conv_b'].astype(F32)
    p['f_wdown'] = w['ffn_w_down'].astype(BF16)
    return p


def _to_chunk_major(u2d, b, s):
    d = u2d.shape[1]
    g = d // S5_GROUP_CH
    u = u2d.reshape(b, s // S5_CHUNK, S5_CHUNK, g, S5_GROUP_CH)
    return jnp.transpose(u, (1, 0, 3, 2, 4)).reshape((s // S5_CHUNK) * b, d * S5_CHUNK)


def _to_token_major(yt, b, s):
    d = yt.shape[1] // S5_CHUNK
    g = d // S5_GROUP_CH
    y = yt.reshape(s // S5_CHUNK, b, g, S5_CHUNK, S5_GROUP_CH)
    return jnp.transpose(y, (1, 0, 3, 2, 4)).reshape(b * s, d)


def _trunk(x, st, p, tiles):
    b, s, d = x.shape
    ng = p['norm_g']
    g_of = lambda layer, k: ng[layer, k].reshape(1, d)
    zero_init = st is None
    groups = d // S5_GROUP_CH
    x2d = x.reshape(b * s, d)

    n_state = p['s5_mats']['row'][1].shape[2] // 2
    if zero_init:
        h0re = h0im = jnp.zeros((groups // 2, b, 2 * n_state), F32)
    else:
        pair = lambda h: jnp.transpose(h.astype(F32).reshape(b, groups // 2, 2 * n_state), (1, 0, 2))
        h0re, h0im = pair(st['s5_re']), pair(st['s5_im'])
    if tiles.s5_time:
        y, hre, him = _s5_fused(x, g_of(0, 0), p['s5_mats']['col'], h0re, h0im, tiles.s5_time)
        y2d = y.reshape(b * s, d)
    else:
        u = _norm_cast(x2d, g_of(0, 0), tiles.rows)
        yt, hre, him = _s5_core(_to_chunk_major(u, b, s), p['s5_mats']['row'], h0re, h0im, b, tiles.s5_rows)
        y2d = _to_token_major(yt, b, s)
    unpair = lambda h: jnp.transpose(h, (1, 0, 2)).reshape(1, b, groups, n_state)
    o_re, o_im = unpair(hre), unpair(him)
    x2d = _glu(x2d, y2d, g_of(0, 0), p['s5_D'], p['s5_w_glu'], g_of(0, 1), tiles.rows)

    def ffn(x2d, layer):
        dff2 = p['f_wup'].shape[2]
        width = p['f_cw'].shape[1]
        buf = jnp.zeros((b, width - 1, dff2), F32) if zero_init else st['ffn_conv'][layer].astype(F32)
        xo, nbuf = _ffn(x2d.reshape(b, s, d), buf, g_of(layer, 2), p['f_wup'][layer], p['f_cw'][layer],
                        p['f_cb'][layer].reshape(1, dff2), p['f_wdown'][layer], g_of(layer, 3), tiles.ffn_time)
        return xo.reshape(b * s, d), nbuf

    x2d, fbuf0 = ffn(x2d, 0)

    heads = p['heads']
    inner = p['m_cw'].shape[1]
    width = p['m_cw'].shape[0]
    cbuf = jnp.zeros((b, width - 1, inner), F32) if zero_init else st['mlstm_conv'].astype(F32)
    if zero_init:
        state = None
    else:
        state = (st['mlstm_C'].astype(F32), st['mlstm_n'].astype(F32),
                 jnp.broadcast_to(st['mlstm_m'].astype(F32)[:, :, None], (b, heads, V7X_LANES)))
    x3, c_out, n_out, m_out, ncbuf = _mlstm_layer(x2d.reshape(b, s, d), cbuf, state, p, g_of(1, 0), g_of(1, 1),
                                                  tiles.mlstm_chunk)
    x2d, fbuf1 = ffn(x3.reshape(b * s, d), 1)

    return (x2d.reshape(b, s, d), o_re, o_im, c_out[None], n_out[None], m_out[None, :, :, 0], ncbuf[None],
            jnp.stack([fbuf0, fbuf1], axis=0))


def kernel(x_prompt, x_sample, state_s5_re, state_s5_im, state_mlstm_C, state_mlstm_n, state_mlstm_m, state_mlstm_conv, state_ffn_conv, norm_g, s5_A_re, s5_A_im, s5_log_dt, s5_B_re, s5_B_im, s5_C_re, s5_C_im, s5_D, s5_w_glu, mlstm_w_up, mlstm_conv_w, mlstm_conv_b, mlstm_wq, mlstm_wk, mlstm_wv, mlstm_w_gate, mlstm_b_gate, mlstm_norm_g, mlstm_skip, mlstm_w_down, ffn_w_up, ffn_conv_w, ffn_conv_b, ffn_w_down):
    w = {
        'norm_g': norm_g, 's5_A_re': s5_A_re, 's5_A_im': s5_A_im, 's5_log_dt': s5_log_dt,
        's5_B_re': s5_B_re, 's5_B_im': s5_B_im, 's5_C_re': s5_C_re, 's5_C_im': s5_C_im, 's5_D': s5_D,
        's5_w_glu': s5_w_glu, 'mlstm_w_up': mlstm_w_up, 'mlstm_conv_w': mlstm_conv_w,
        'mlstm_conv_b': mlstm_conv_b, 'mlstm_wq': mlstm_wq, 'mlstm_wk': mlstm_wk, 'mlstm_wv': mlstm_wv,
        'mlstm_w_gate': mlstm_w_gate, 'mlstm_b_gate': mlstm_b_gate, 'mlstm_norm_g': mlstm_norm_g,
        'mlstm_skip': mlstm_skip, 'mlstm_w_down': mlstm_w_down, 'ffn_w_up': ffn_w_up,
        'ffn_conv_w': ffn_conv_w, 'ffn_conv_b': ffn_conv_b, 'ffn_w_down': ffn_w_down,
    }
    assert norm_g.shape[0] == 2 and s5_A_re.shape[0] == 1 and mlstm_w_up.shape[0] == 1
    assert x_prompt.shape[0] == V7X_SUBLANES and x_sample.shape[0] == V7X_SUBLANES
    p = _prep_weights(w)
    out_p = _trunk(x_prompt.astype(F32), None, p, _pick_tiles(*x_prompt.shape[:2]))
    st = {'s5_re': state_s5_re[0], 's5_im': state_s5_im[0], 'mlstm_C': state_mlstm_C[0],
          'mlstm_n': state_mlstm_n[0], 'mlstm_m': state_mlstm_m[0], 'mlstm_conv': state_mlstm_conv[0],
          'ffn_conv': state_ffn_conv}
    out_s = _trunk(x_sample.astype(F32), st, p, _pick_tiles(*x_sample.shape[:2]))
    return (out_p[0], out_s[0]) + tuple(out_p[1:]) + tuple(out_s[1:])
```

```python
import functools
import math
from typing import NamedTuple

import jax
import jax.numpy as jnp
from jax import lax
from jax.experimental import pallas as pl
from jax.experimental.pallas import tpu as pltpu

F32 = jnp.float32
BF16 = jnp.bfloat16
HI = lax.Precision.HIGHEST

NORM_EPS = 1e-6
LN_EPS = 1e-5
S5_GROUP_CH = 16
S5_CHUNK = 16
QKV_BLOCK = 4

V7X_LANES = 128
V7X_SUBLANES = 8
V7X_MXU = 256
V7X_VMEM_BYTES = 64 * 1024 * 1024


class Tiles(NamedTuple):
    rows: int
    time: int
    s5_rows: int
    mlstm_chunk: int
    s5_time: int
    ffn_time: int


def _pick_tiles(batch, seq):
    s5_time = (V7X_LANES // batch) * S5_CHUNK
    return Tiles(rows=min(batch * seq, 512), time=min(seq, 512),
                 s5_rows=min(batch * (seq // S5_CHUNK), 512), mlstm_chunk=min(seq, 256),
                 s5_time=s5_time if seq % s5_time == 0 else 0, ffn_time=min(seq, 1024))


def _vmem_limit(block_bytes, scratch_bytes):
    want = 2 * block_bytes + scratch_bytes + 16 * 1024 * 1024
    return int(min(want, V7X_VMEM_BYTES - 8 * 1024 * 1024))


def _nbytes(shape, dtype):
    return math.prod(shape) * jnp.dtype(dtype).itemsize


def _rms(x, g):
    return x * lax.rsqrt(jnp.mean(x * x, axis=-1, keepdims=True) + NORM_EPS) * g


def _dot(a, b):
    return jnp.dot(a, b, preferred_element_type=F32)


def _dot_nt(a, b):
    return lax.dot_general(a, b, (((1,), (1,)), ((), ())), preferred_element_type=F32)


def _silu(x):
    return (0.5 * x) * (1.0 + jnp.tanh(0.5 * x))


def _causal_dwconv(u, prev, cw_ref, cb_ref, cs):
    width = cw_ref.shape[0]
    acc = cb_ref[:, cs] + u * cw_ref[width - 1:width, cs]
    rows = lax.broadcasted_iota(jnp.int32, prev.shape, 0)
    for j in range(width - 1):
        d = width - 1 - j
        rolled = pltpu.roll(u, d, 0)
        head = jnp.where(rows < d, pltpu.roll(prev, d, 0), rolled[0:V7X_SUBLANES])
        shifted = jnp.concatenate([head, rolled[V7X_SUBLANES:]], axis=0)
        acc = acc + shifted * cw_ref[j:j + 1, cs]
    return acc


def _log_sigmoid(x):
    return jnp.minimum(x, 0.0) - jnp.log1p(jnp.exp(-jnp.abs(x)))


def _split3(x):
    hi = x.astype(BF16)
    r1 = x - hi.astype(F32)
    mid = r1.astype(BF16)
    lo = (r1 - mid.astype(F32)).astype(BF16)
    return hi, mid, lo


def _norm_cast_kernel(x_ref, g_ref, o_ref):
    o_ref[...] = _rms(x_ref[...], g_ref[...]).astype(o_ref.dtype)


def _norm_cast(x2d, g, tb):
    rows, d = x2d.shape
    return pl.pallas_call(
        _norm_cast_kernel,
        grid=(rows // tb,),
        in_specs=[pl.BlockSpec((tb, d), lambda i: (i, 0)), pl.BlockSpec((1, d), lambda i: (0, 0))],
        out_specs=pl.BlockSpec((tb, d), lambda i: (i, 0)),
        out_shape=jax.ShapeDtypeStruct((rows, d), BF16),
        name="s5_norm",
    )(x2d, g)


def _s5_kernel(u_ref, tt_ref, pt_ref, qt_ref, are_ref, aim_ref, h0re_ref, h0im_ref,
               y_ref, hre_ref, him_ref, pure_s, puim_s, hinre_s, hinim_s, stre_s, stim_s, *, nb, rb):
    r = pl.program_id(1)

    @pl.when(r == 0)
    def _():
        stre_s[...] = h0re_ref[0]
        stim_s[...] = h0im_ref[0]

    u = u_ref[...]
    w = V7X_MXU
    half = pt_ref.shape[1] // 2
    pu = [_dot_nt(u[:, i * w:(i + 1) * w], pt_ref[i]) for i in range(2)]
    pure_s[...] = jnp.concatenate([pu[0][:, :half], pu[1][:, :half]], axis=1)
    puim_s[...] = jnp.concatenate([pu[0][:, half:], pu[1][:, half:]], axis=1)
    ar = jnp.broadcast_to(are_ref[0], (nb, 2 * half))
    ai = jnp.broadcast_to(aim_ref[0], (nb, 2 * half))

    def step(i, carry):
        re, im = carry
        rows = pl.ds(pl.multiple_of(i * nb, nb), nb)
        hinre_s[rows, :] = re
        hinim_s[rows, :] = im
        return (ar * re - ai * im + pure_s[rows, :], ar * im + ai * re + puim_s[rows, :])

    re, im = lax.fori_loop(0, rb // nb, step, (stre_s[...], stim_s[...]))
    stre_s[...] = re
    stim_s[...] = im
    hre_ref[0] = re
    him_ref[0] = im

    hre = hinre_s[...]
    him = hinim_s[...]
    for i in range(2):
        hin = jnp.concatenate([hre[:, i * half:(i + 1) * half], him[:, i * half:(i + 1) * half]], axis=1)
        y_ref[:, i * w:(i + 1) * w] = (_dot_nt(u[:, i * w:(i + 1) * w], tt_ref[i])
                                       + _dot_nt(hin.astype(BF16), qt_ref[i])).astype(y_ref.dtype)


def _s5_core(ut, mats, h0re, h0im, nb, rb):
    rows, cols = ut.shape
    pairs = cols // (2 * V7X_MXU)
    tt, pt, qt, are, aim = mats
    nst = pt.shape[1]
    pair3 = lambda p, r: (p, 0, 0)
    kern = functools.partial(_s5_kernel, nb=nb, rb=rb)
    blocks = (_nbytes((rb, 512), BF16) * 2 + _nbytes((2, 256, 256), BF16) + 4 * _nbytes((256, 128), BF16))
    scratch = 4 * _nbytes((rb, 128), F32) + 2 * _nbytes((nb, 128), F32)
    return pl.pallas_call(
        kern,
        grid=(pairs, rows // rb),
        in_specs=[
            pl.BlockSpec((rb, 2 * V7X_MXU), lambda p, r: (r, p)),
            pl.BlockSpec((2, V7X_MXU, V7X_MXU), pair3),
            pl.BlockSpec((2, nst, V7X_MXU), pair3),
            pl.BlockSpec((2, V7X_MXU, nst), pair3),
            pl.BlockSpec((1, 1, 128), pair3),
            pl.BlockSpec((1, 1, 128), pair3),
            pl.BlockSpec((1, nb, 128), pair3),
            pl.BlockSpec((1, nb, 128), pair3),
        ],
        out_specs=[
            pl.BlockSpec((rb, 2 * V7X_MXU), lambda p, r: (r, p)),
            pl.BlockSpec((1, nb, 128), pair3),
            pl.BlockSpec((1, nb, 128), pair3),
        ],
        out_shape=[
            jax.ShapeDtypeStruct((rows, cols), BF16),
            jax.ShapeDtypeStruct((pairs, nb, 128), F32),
            jax.ShapeDtypeStruct((pairs, nb, 128), F32),
        ],
        scratch_shapes=[pltpu.VMEM((rb, 128), F32)] * 4 + [pltpu.VMEM((nb, 128), F32)] * 2,
        compiler_params=pltpu.CompilerParams(
            dimension_semantics=("arbitrary", "arbitrary"), vmem_limit_bytes=_vmem_limit(blocks, scratch)),
        name="s5_core",
    )(ut, tt, pt, qt, are, aim, h0re, h0im)


def _s5_fused_kernel(x_ref, g_ref, tt_ref, pt_ref, qt_ref, are_ref, aim_ref, h0re_ref, h0im_ref,
                     y_ref, hre_ref, him_ref,
                     us, ugt, pure, puim, hinre, hinim, ytmp, ys, stre, stim, *, nb, tbt):
    lc = S5_CHUNK
    ch = S5_GROUP_CH
    gps = V7X_LANES // ch
    nch = tbt // lc
    cols = nb * nch
    k = pl.program_id(1)

    @pl.when(k == 0)
    def _():
        for b in range(nb):
            u = _rms(x_ref[b], g_ref[...])
            for kk in range(us.shape[0]):
                us[kk, pl.ds(b, tbt, stride=nb), :] = u[:, kk * V7X_LANES:(kk + 1) * V7X_LANES]

    @pl.when(pl.program_id(0) == 0)
    def _():
        stre[k] = h0re_ref[...]
        stim[k] = h0im_ref[...]

    tile = lambda j, s: slice((j * lc + s) * nb, (j * lc + s + 1) * nb)
    for s in range(lc):
        a = jnp.concatenate([us[k, tile(j, s), :] for j in range(nch)], axis=0).T
        for g in range(gps):
            ugt[g, (lc - 1 - s) * ch:(lc - s) * ch, :] = a[g * ch:(g + 1) * ch, :].astype(BF16)

    half = pt_ref.shape[1] // 2
    for pr in range(gps // 2):
        put0 = _dot(pt_ref[2 * pr], ugt[2 * pr])
        put1 = _dot(pt_ref[2 * pr + 1], ugt[2 * pr + 1])
        pure[pr] = jnp.concatenate([put0[:half], put1[:half]], axis=0).T
        puim[pr] = jnp.concatenate([put0[half:], put1[half:]], axis=0).T

    for pr in range(gps // 2):
        re = stre[k, pr]
        im = stim[k, pr]
        ar = jnp.broadcast_to(are_ref[pr], re.shape)
        ai = jnp.broadcast_to(aim_ref[pr], re.shape)
        for j in range(nch):
            rows = slice(j * nb, (j + 1) * nb)
            hinre[pr, rows, :] = re
            hinim[pr, rows, :] = im
            re, im = (ar * re - ai * im + pure[pr, rows, :], ar * im + ai * re + puim[pr, rows, :])
        stre[k, pr] = re
        stim[k, pr] = im
        hre_ref[k * (gps // 2) + pr] = re
        him_ref[k * (gps // 2) + pr] = im

    for pr in range(gps // 2):
        hre_t = hinre[pr].T
        him_t = hinim[pr].T
        for i in range(2):
            g = 2 * pr + i
            hin_t = jnp.concatenate([hre_t[i * half:(i + 1) * half], him_t[i * half:(i + 1) * half]],
                                    axis=0).astype(BF16)
            ytmp[g] = _dot(tt_ref[g], ugt[g]) + _dot(qt_ref[g], hin_t)

    for t in range(lc):
        zt = jnp.concatenate([ytmp[g, t * ch:(t + 1) * ch, :] for g in range(gps)], axis=0).T
        for j in range(nch):
            ys[tile(j, t), :] = zt[j * nb:(j + 1) * nb, :]
    for b in range(nb):
        y_ref[b] = ys[pl.ds(b, tbt, stride=nb), :].astype(y_ref.dtype)


def _s5_fused(x, g, mats_t, h0re, h0im, tbt):
    nb, s, d = x.shape
    tt, pt, qt, are, aim = mats_t
    groups = tt.shape[0]
    gps = V7X_LANES // S5_GROUP_CH
    slabs = groups // gps
    w = S5_CHUNK * S5_GROUP_CH
    nst = pt.shape[1]
    cols = nb * (tbt // S5_CHUNK)
    kern = functools.partial(_s5_fused_kernel, nb=nb, tbt=tbt)
    slab3 = lambda t, k: (k, 0, 0)
    blocks = (_nbytes((nb, tbt, d), F32) + _nbytes((nb, tbt, V7X_LANES), BF16)
              + gps * (_nbytes((w, w), BF16) + 2 * _nbytes((w, nst), BF16)))
    scratch = (_nbytes((slabs, nb * tbt, V7X_LANES), F32) + _nbytes((gps, w, cols), BF16)
               + 4 * _nbytes((gps // 2, cols, nst), F32) + _nbytes((gps, w, cols), F32)
               + _nbytes((nb * tbt, V7X_LANES), F32))
    return pl.pallas_call(
        kern,
        grid=(s // tbt, slabs),
        in_specs=[
            pl.BlockSpec((nb, tbt, d), lambda t, k: (0, t, 0)),
            pl.BlockSpec((1, d), lambda t, k: (0, 0)),
            pl.BlockSpec((gps, w, w), slab3),
            pl.BlockSpec((gps, nst, w), slab3),
            pl.BlockSpec((gps, w, nst), slab3),
            pl.BlockSpec((gps // 2, 1, nst), slab3),
            pl.BlockSpec((gps // 2, 1, nst), slab3),
            pl.BlockSpec((gps // 2, nb, nst), slab3),
            pl.BlockSpec((gps // 2, nb, nst), slab3),
        ],
        out_specs=[
            pl.BlockSpec((nb, tbt, V7X_LANES), lambda t, k: (0, t, k)),
            pl.BlockSpec((groups // 2, nb, nst), lambda t, k: (0, 0, 0)),
            pl.BlockSpec((groups // 2, nb, nst), lambda t, k: (0, 0, 0)),
        ],
        out_shape=[
            jax.ShapeDtypeStruct((nb, s, d), BF16),
            jax.ShapeDtypeStruct((groups // 2, nb, nst), F32),
            jax.ShapeDtypeStruct((groups // 2, nb, nst), F32),
        ],
        scratch_shapes=[
            pltpu.VMEM((slabs, nb * tbt, V7X_LANES), F32),
            pltpu.VMEM((gps, w, cols), BF16),
            pltpu.VMEM((gps // 2, cols, nst), F32),
            pltpu.VMEM((gps // 2, cols, nst), F32),
            pltpu.VMEM((gps // 2, cols, nst), F32),
            pltpu.VMEM((gps // 2, cols, nst), F32),
            pltpu.VMEM((gps, w, cols), F32),
            pltpu.VMEM((nb * tbt, V7X_LANES), F32),
            pltpu.VMEM((slabs, gps // 2, nb, nst), F32),
            pltpu.VMEM((slabs, gps // 2, nb, nst), F32),
        ],
        compiler_params=pltpu.CompilerParams(
            dimension_semantics=("arbitrary", "arbitrary"), vmem_limit_bytes=_vmem_limit(blocks, scratch)),
        name="s5_fused",
    )(x, g, tt, pt, qt, are, aim, h0re, h0im)


def _s5_matrices(a_re, a_im, log_dt, b_re, b_im, c_re, c_im):
    lc = S5_CHUNK
    groups, n = a_re.shape
    ch = S5_GROUP_CH
    f = lambda v: v.astype(F32)
    a_re, a_im, b_re, b_im, c_re, c_im = map(f, (a_re, a_im, b_re, b_im, c_re, c_im))
    dt = jnp.exp(f(log_dt))[:, None, None]
    j = jnp.arange(lc + 1, dtype=F32)
    mag = jnp.exp((a_re[:, :, None] * dt) * j)
    ang = (a_im[:, :, None] * dt) * j
    pw_re, pw_im = mag * jnp.cos(ang), mag * jnp.sin(ang)
    e_re, e_im = pw_re[:, :, 1] - 1.0, pw_im[:, :, 1]
    inv = 1.0 / (a_re * a_re + a_im * a_im)
    f_re, f_im = (e_re * a_re + e_im * a_im) * inv, (e_im * a_re - e_re * a_im) * inv
    bb_re = f_re[..., None] * b_re - f_im[..., None] * b_im
    bb_im = f_re[..., None] * b_im + f_im[..., None] * b_re
    rep = lambda v: jnp.repeat(v, ch, axis=2)
    til = lambda v: jnp.tile(v, (1, 1, lc + 1))
    lb_re = rep(pw_re) * til(bb_re) - rep(pw_im) * til(bb_im)
    lb_im = rep(pw_re) * til(bb_im) + rep(pw_im) * til(bb_re)
    kern = jnp.einsum('gcm,gmx->gcx', jnp.concatenate([c_re, c_im], axis=2),
                      jnp.concatenate([lb_re, -lb_im], axis=1), precision=HI)
    w = lc * ch
    kfwd = jnp.pad(kern[:, :, :w], ((0, 0), (0, 0), ((lc - 1) * ch, 0)))
    tt = jnp.stack([kfwd[:, :, t * ch:t * ch + w] for t in range(lc)], axis=1).reshape(groups, w, w)
    pt = jnp.concatenate([lb_re[:, :, :w], lb_im[:, :, :w]], axis=1)
    pwt_re = jnp.swapaxes(pw_re, 1, 2)[:, 1:, None, :]
    pwt_im = jnp.swapaxes(pw_im, 1, 2)[:, 1:, None, :]
    ca_re = c_re[:, None] * pwt_re - c_im[:, None] * pwt_im
    ca_im = c_re[:, None] * pwt_im + c_im[:, None] * pwt_re
    qt = jnp.concatenate([ca_re, -ca_im], axis=3).reshape(groups, w, 2 * n)
    are = pw_re[:, :, lc].reshape(groups // 2, 1, 2 * n)
    aim = pw_im[:, :, lc].reshape(groups // 2, 1, 2 * n)
    return tt.astype(BF16), pt.astype(BF16), qt.astype(BF16), are, aim


def _glu_kernel(x_ref, y_ref, g0_ref, d_ref, w_ref, g1_ref, o_ref, out_s):
    x = x_ref[...]
    d = x.shape[1]
    u = _rms(x, g0_ref[...])
    yy = y_ref[...].astype(F32) + d_ref[...] * u
    z = jax.nn.gelu(yy, approximate=True).astype(BF16)
    w = V7X_MXU
    nslab = d // w
    pair = lambda j: (_dot(z, w_ref[:, j * w:(j + 1) * w]), _dot(z, w_ref[:, d + j * w:d + (j + 1) * w]))
    nxt = pair(0)
    ssq = jnp.zeros((x.shape[0], 1), F32)
    for j in range(nslab):
        a, b = nxt
        if j + 1 < nslab:
            nxt = pair(j + 1)
        out = a * (0.5 * (1.0 + jnp.tanh(0.5 * b)))
        out_s[:, j * w:(j + 1) * w] = out
        ssq = ssq + jnp.sum(out * out, axis=-1, keepdims=True)
    scale = lax.rsqrt(ssq * (1.0 / d) + NORM_EPS)
    o_ref[...] = x + out_s[...] * scale * g1_ref[...]


def _const_spec(shape):
    nd = len(shape)
    return pl.BlockSpec(shape, lambda *_: (0,) * nd, pipeline_mode=pl.Buffered(1))


def _glu(x2d, y2d, g0, dskip, w, g1, tb):
    rows, d = x2d.shape
    blocks = 2 * _nbytes((tb, d), F32) + _nbytes((tb, d), BF16)
    scratch = _nbytes(w.shape, BF16) + 3 * _nbytes((tb, 2 * d), F32)
    row = pl.BlockSpec((tb, d), lambda i: (i, 0))
    return pl.pallas_call(
        _glu_kernel,
        grid=(rows // tb,),
        in_specs=[row, row, _const_spec((1, d)), _const_spec((1, d)), _const_spec(w.shape), _const_spec((1, d))],
        out_specs=row,
        out_shape=jax.ShapeDtypeStruct((rows, d), F32),
        scratch_shapes=[pltpu.VMEM((tb, d), F32)],
        compiler_params=pltpu.CompilerParams(
            dimension_semantics=("arbitrary",), vmem_limit_bytes=_vmem_limit(blocks, scratch)),
        name="s5_glu",
    )(x2d, y2d, g0, dskip, w, g1)


def _ffn_kernel(x_ref, buf_ref, g2_ref, wup_ref, cw_ref, cb_ref, wdown_ref, g3_ref,
                o_ref, nbuf_ref, carry, act, *, tb, dff, width):
    t = pl.program_id(1)
    pad = V7X_SUBLANES

    @pl.when(t == 0)
    def _():
        carry[...] = jnp.zeros_like(carry)
        carry[pad - (width - 1):pad, :] = buf_ref[0]

    x = x_ref[0]
    hn = _rms(x, g2_ref[...]).astype(BF16)
    fb = V7X_MXU

    def conv(u, cs):
        prev = carry[:, cs]
        carry[:, cs] = u[tb - pad:tb, :]
        return _causal_dwconv(u, prev, cw_ref, cb_ref, cs)

    nslab = dff // fb
    cols = lambda j: (slice(j * fb, (j + 1) * fb), slice(dff + j * fb, dff + (j + 1) * fb))
    up = lambda j: tuple(_dot(hn, wup_ref[:, cs]) for cs in cols(j))
    nxt = up(0)
    for j in range(nslab):
        ug, uv = nxt
        if j + 1 < nslab:
            nxt = up(j + 1)
        gate = conv(ug, cols(j)[0])
        val = conv(uv, cols(j)[1])
        act[:, j * fb:(j + 1) * fb] = (jax.nn.gelu(gate, approximate=True) * val).astype(BF16)

    f = _dot(act[...], wdown_ref[...])
    o_ref[0] = x + _rms(f, g3_ref[...])
    nbuf_ref[0] = carry[pad - (width - 1):pad, :]


def _layer_spec(shape, layer):
    nd = len(shape) - 1
    return pl.BlockSpec((None,) + tuple(shape[1:]), lambda *_: (layer,) + (0,) * nd, pipeline_mode=pl.Buffered(1))


def _ffn(x, buf, g2, wup, cw, cb, wdown, g3, layer, tb):
    b, s, d = x.shape
    dff = wdown.shape[1]
    width = cw.shape[1]
    kern = functools.partial(_ffn_kernel, tb=tb, dff=dff, width=width)
    blocks = 2 * _nbytes((tb, d), F32)
    scratch = (_nbytes(wup.shape[1:], BF16) + _nbytes(wdown.shape[1:], BF16) + 6 * _nbytes((tb, V7X_MXU), F32)
               + _nbytes((8, 2 * dff), F32) + _nbytes((tb, dff), BF16) + 2 * _nbytes((tb, d), F32))
    xs = pl.BlockSpec((1, tb, d), lambda i, t: (i, t, 0))
    bs = pl.BlockSpec((1, width - 1, 2 * dff), lambda i, t: (i, 0, 0))
    return pl.pallas_call(
        kern,
        grid=(b, s // tb),
        in_specs=[xs, bs, _const_spec((1, d)), _layer_spec(wup.shape, layer), _layer_spec(cw.shape, layer),
                  _layer_spec(cb.shape, layer), _layer_spec(wdown.shape, layer), _const_spec((1, d))],
        out_specs=[xs, bs],
        out_shape=[jax.ShapeDtypeStruct((b, s, d), F32), jax.ShapeDtypeStruct((b, width - 1, 2 * dff), F32)],
        scratch_shapes=[pltpu.VMEM((V7X_SUBLANES, 2 * dff), F32), pltpu.VMEM((tb, dff), BF16)],
        compiler_params=pltpu.CompilerParams(
            dimension_semantics=("arbitrary", "arbitrary"), vmem_limit_bytes=_vmem_limit(blocks, scratch)),
        name="conv_ffn",
    )(x, buf, g2, wup, cw, cb, wdown, g3)


def _mlstm_weights_kernel(aq_ref, ak_ref, av_ref, gq_ref, gk_ref, gv_ref,
                          wq_ref, wkt_ref, wv_ref, gc_ref, gm_ref, *, kscale):
    w = V7X_MXU
    kb = QKV_BLOCK
    dotp = functools.partial(jnp.dot, preferred_element_type=F32, precision=HI)
    lane = lax.broadcasted_iota(jnp.int32, (V7X_LANES, w), 1)
    src = lax.broadcasted_iota(jnp.int32, (V7X_LANES, w), 0)
    spread = jnp.where(jnp.bitwise_and(lane, kb - 1) == src, 1.0, 0.0)
    shift = kb.bit_length() - 1
    rblk = lax.shift_right_logical(lax.broadcasted_iota(jnp.int32, (w, w), 0), shift)
    cblk = lax.shift_right_logical(lax.broadcasted_iota(jnp.int32, (w, w), 1), shift)
    tile = lambda a_ref: jnp.where(rblk == cblk, dotp(a_ref[0], spread), 0.0)
    tq, tk, tv = tile(aq_ref), tile(ak_ref), tile(av_ref)
    wq_ref[0] = tq.astype(BF16)
    wkt_ref[0] = (tk.T * kscale).astype(BF16)
    wv_ref[0] = tv.astype(BF16)
    gc_ref[...] = (dotp(tq, gq_ref[...]) + dotp(tk, gk_ref[...])).astype(BF16)
    gm_ref[...] = dotp(tv, gv_ref[...]).astype(BF16)


def _mlstm_weights(wq, wk, wv, w_gate, kscale):
    nblk, kb, _ = wq.shape
    assert kb == QKV_BLOCK and kb & (kb - 1) == 0
    w = V7X_MXU
    per = w // kb
    nt = nblk // per
    inner = nt * w
    compact = lambda m: jnp.pad(m.astype(F32).reshape(nt, w, kb), ((0, 0), (0, 0), (0, V7X_LANES - kb)))
    wg = jnp.pad(w_gate.astype(F32), ((0, 0), (0, V7X_LANES - w_gate.shape[1])))
    cm = pl.BlockSpec((1, w, V7X_LANES), lambda i: (i, 0, 0))
    gs = [pl.BlockSpec((w, V7X_LANES), lambda i, k=k: (k * nt + i, 0)) for k in range(3)]
    tile = pl.BlockSpec((1, w, w), lambda i: (i, 0, 0))
    fold = pl.BlockSpec((w, V7X_LANES), lambda i: (i, 0))
    return pl.pallas_call(
        functools.partial(_mlstm_weights_kernel, kscale=kscale),
        grid=(nt,),
        in_specs=[cm, cm, cm] + gs,
        out_specs=[tile, tile, tile, fold, fold],
        out_shape=[jax.ShapeDtypeStruct((nt, w, w), BF16)] * 3 + [jax.ShapeDtypeStruct((inner, V7X_LANES), BF16)] * 2,
        name="mlstm_weights",
    )(compact(wq), compact(wk), compact(wv), wg, wg, wg)


def _mpre_kernel(x_ref, cbuf_ref, g_ref, wup_ref, cw_ref, cb_ref, wq_ref, wkt_ref, wv_ref, gc_ref, gm_ref,
                 bg_ref, q_ref, kt_ref, v_ref, xc_ref, z_ref, gates_ref, nbuf_ref, carry,
                 *, tb, inner, width):
    t = pl.program_id(1)
    pad = V7X_SUBLANES

    @pl.when(t == 0)
    def _():
        carry[...] = jnp.zeros_like(carry)
        carry[pad - (width - 1):pad, :] = cbuf_ref[0]

    h = _rms(x_ref[0], g_ref[...]).astype(BF16)
    w = V7X_MXU
    gacc = jnp.zeros((tb, bg_ref.shape[1]), F32) + bg_ref[...]
    for i in range(inner // w):
        cs = slice(i * w, (i + 1) * w)
        xm = _dot(h, wup_ref[:, cs])
        prev = carry[:, cs]
        carry[:, cs] = xm[tb - pad:tb, :]
        xc = _silu(_causal_dwconv(xm, prev, cw_ref, cb_ref, cs)).astype(BF16)
        xmb = xm.astype(BF16)
        q_ref[0, :, cs] = _dot(xc, wq_ref[i]).astype(BF16)
        kt_ref[0, cs, :] = _dot_nt(wkt_ref[i], xc).astype(BF16)
        v_ref[0, :, cs] = _dot(xmb, wv_ref[i]).astype(BF16)
        xc_ref[0, :, cs] = xc
        z_ref[0, :, cs] = _silu(_dot(h, wup_ref[:, inner + i * w:inner + (i + 1) * w])).astype(BF16)
        gacc = gacc + _dot(xc, gc_ref[cs, :]) + _dot(xmb, gm_ref[cs, :])
    gates_ref[0] = gacc
    nbuf_ref[0] = carry[pad - (width - 1):pad, :]


def _mpre(x, cbuf, g, wup, cw, cb, wq_t, wkt_t, wv_t, gc, gm, bg, tb):
    b, s, d = x.shape
    inner = cw.shape[1]
    width = cw.shape[0]
    ng = bg.shape[1]
    kern = functools.partial(_mpre_kernel, tb=tb, inner=inner, width=width)
    act = pl.BlockSpec((1, tb, inner), lambda i, t: (i, t, 0))
    blocks = _nbytes((tb, d), F32) + 5 * _nbytes((tb, inner), BF16)
    scratch = (_nbytes(wup.shape, BF16) + 3 * _nbytes(wq_t.shape, BF16) + 2 * _nbytes((inner, 128), BF16)
               + _nbytes((8, inner), F32) + 8 * _nbytes((tb, V7X_MXU), F32))
    return pl.pallas_call(
        kern,
        grid=(b, s // tb),
        in_specs=[pl.BlockSpec((1, tb, d), lambda i, t: (i, t, 0)),
                  pl.BlockSpec((1, width - 1, inner), lambda i, t: (i, 0, 0)),
                  _const_spec((1, d)), _const_spec(wup.shape), _const_spec(cw.shape), _const_spec((1, inner)),
                  _const_spec(wq_t.shape), _const_spec(wkt_t.shape), _const_spec(wv_t.shape),
                  _const_spec(gc.shape), _const_spec(gm.shape), _const_spec((1, ng))],
        out_specs=[act, pl.BlockSpec((1, inner, tb), lambda i, t: (i, 0, t)), act, act, act,
                   pl.BlockSpec((1, tb, ng), lambda i, t: (i, t, 0)),
                   pl.BlockSpec((1, width - 1, inner), lambda i, t: (i, 0, 0))],
        out_shape=[jax.ShapeDtypeStruct((b, s, inner), BF16), jax.ShapeDtypeStruct((b, inner, s), BF16),
                   jax.ShapeDtypeStruct((b, s, inner), BF16), jax.ShapeDtypeStruct((b, s, inner), BF16),
                   jax.ShapeDtypeStruct((b, s, inner), BF16), jax.ShapeDtypeStruct((b, s, ng), F32),
                   jax.ShapeDtypeStruct((b, width - 1, inner), F32)],
        scratch_shapes=[pltpu.VMEM((V7X_SUBLANES, inner), F32)],
        compiler_params=pltpu.CompilerParams(
            dimension_semantics=("arbitrary", "arbitrary"), vmem_limit_bytes=_vmem_limit(blocks, scratch)),
        name="mlstm_pre",
    )(x, cbuf, g, wup, cw, cb, wq_t, wkt_t, wv_t, gc, gm, bg)


def _mlstm_chunk(q_ref, kt_ref, v_ref, g, gt, ct_s, n_s, m_s, fill, emit_hn, fill_update, *, lc, heads, dh):
    row = lax.broadcasted_iota(jnp.int32, (lc, lc), 0)
    col = lax.broadcasted_iota(jnp.int32, (lc, lc), 1)
    causal = col <= row
    tri_l = jnp.where(causal, 1.0, 0.0).astype(BF16)
    tri_u = jnp.where(row <= col, 1.0, 0.0).astype(BF16)
    b_cols = sum(_dot(tri_l, part) for part in _split3(_log_sigmoid(g)))
    b_rows = sum(_dot(part, tri_u) for part in _split3(_log_sigmoid(gt)))

    hsl = [slice(h * dh, (h + 1) * dh) for h in range(heads)]
    st = []
    for h in range(heads):
        b_col = b_cols[:, heads + h:heads + h + 1]
        b_row = b_rows[heads + h:heads + h + 1, :]
        i_row = gt[h:h + 1, :]
        m_prev = m_s[h:h + 1, 0:1]
        dmat = jnp.where(causal, b_col - b_row + i_row, -jnp.inf)
        inter = b_col + m_prev
        m_t = jnp.maximum(inter, jnp.max(dmat, axis=1, keepdims=True))
        st.append(dict(b_col=b_col, b_row=b_row, i_row=i_row, m_prev=m_prev, m_t=m_t,
                       wts=jnp.exp(dmat - m_t), a=jnp.exp(inter - m_t)))
        fill()
    for h in range(heads):
        s = st[h]
        s['sc'] = _dot(q_ref[0, :, hsl[h]], kt_ref[0, hsl[h], :]) * s['wts']
    for h in range(heads):
        fill()
        s = st[h]
        qh = q_ref[0, :, hsl[h]]
        sc = s['sc']
        a = s['a']
        num = _dot(sc.astype(BF16), v_ref[0, :, hsl[h]]) + a * _dot(qh, ct_s[h].astype(BF16))
        qn = jnp.sum(qh.astype(F32) * n_s[h:h + 1, :], axis=1, keepdims=True)
        den = jnp.sum(sc, axis=1, keepdims=True) + a * qn
        hh = num / jnp.maximum(jnp.abs(den), jnp.exp(-s['m_t']))
        mu = jnp.mean(hh, axis=1, keepdims=True)
        dev = hh - mu
        var = jnp.mean(dev * dev, axis=1, keepdims=True)
        emit_hn(h, dev * lax.rsqrt(var + LN_EPS))
    for h in range(heads):
        fill_update(h)
        s = st[h]
        kth = kt_ref[0, hsl[h], :]
        b_last = s['b_row'][:, lc - 1:lc]
        g_row = b_last - s['b_row'] + s['i_row']
        m_new = jnp.maximum(b_last + s['m_prev'], jnp.max(g_row, axis=1, keepdims=True))
        decay = jnp.exp(b_last + s['m_prev'] - m_new)
        wg_row = jnp.exp(g_row - m_new)
        wg_col = jnp.exp(b_last - s['b_col'] + g[:, h:h + 1] - m_new)
        wv = (v_ref[0, :, hsl[h]].astype(F32) * wg_col).astype(BF16)
        ct_s[h] = decay * ct_s[h] + _dot(kth, wv)
        wg16 = jnp.broadcast_to(wg_row, (16, lc)).astype(BF16)
        n_s[h:h + 1, :] = decay * n_s[h:h + 1, :] + _dot_nt(wg16, kth)[0:1, :]
        m_s[h:h + 1, :] = jnp.broadcast_to(m_new, (1, V7X_LANES))


def _mlstm_layer_kernel(*refs, lc, heads, inner, width, zero_init):
    (x_ref, cbuf_ref, g0_ref, wup_ref, cw_ref, cb_ref, wq_ref, wkt_ref, wv_ref, gc_ref, gm_ref, bg_ref,
     ng_ref, sk_ref, wdown_ref, g1_ref) = refs[:16]
    n_in = 16
    if not zero_init:
        c0_ref, n0_ref, m0_ref = refs[16:19]
        n_in = 19
    o_ref, cout_ref, nout_ref, mout_ref, nbuf_ref = refs[n_in:n_in + 5]
    carry, q_s, kt_s, v_s, xc_s, sz_s, act_s, ct_s, n_s, m_s = refs[n_in + 5:]
    dh = inner // heads
    pad = V7X_SUBLANES
    c = pl.program_id(1)

    @pl.when(c == 0)
    def _():
        carry[...] = jnp.zeros_like(carry)
        carry[pad - (width - 1):pad, :] = cbuf_ref[0]
        if zero_init:
            ct_s[...] = jnp.zeros_like(ct_s)
            n_s[...] = jnp.zeros_like(n_s)
            m_s[...] = jnp.zeros_like(m_s)
        else:
            for h in range(heads):
                ct_s[h] = c0_ref[0, h].T
            n_s[...] = n0_ref[0]
            m_s[...] = m0_ref[0]

    x = x_ref[0]
    hx = _rms(x, g0_ref[...]).astype(BF16)
    w = V7X_MXU
    gacc = jnp.zeros((lc, bg_ref.shape[1]), F32) + bg_ref[...]
    nslab = inner // w
    col = lambda i: slice(i * w, (i + 1) * w)

    def project(i, xc, xmb, gacc):
        q_s[0, :, col(i)] = _dot(xc, wq_ref[i]).astype(BF16)
        kt_s[0, col(i), :] = _dot_nt(wkt_ref[i], xc).astype(BF16)
        v_s[0, :, col(i)] = _dot(xmb, wv_ref[i]).astype(BF16)
        xc_s[:, col(i)] = xc
        return gacc + _dot(xc, gc_ref[col(i), :]) + _dot(xmb, gm_ref[col(i), :])

    xm_next = _dot(hx, wup_ref[:, col(0)])
    pending = None
    for i in range(nslab):
        xm = xm_next
        if i + 1 < nslab:
            xm_next = _dot(hx, wup_ref[:, col(i + 1)])
        if pending is not None:
            gacc = project(*pending, gacc)
        prev = carry[:, col(i)]
        carry[:, col(i)] = xm[lc - pad:lc, :]
        xc = _silu(_causal_dwconv(xm, prev, cw_ref, cb_ref, col(i))).astype(BF16)
        pending = (i, xc, xm.astype(BF16))
    gacc = project(*pending, gacc)
    nbuf_ref[0] = carry[pad - (width - 1):pad, :]

    slabs = iter(range(inner // w))

    def gate_slab():
        i = next(slabs, None)
        if i is not None:
            cs = slice(i * w, (i + 1) * w)
            sz_s[:, cs] = _silu(_dot(hx, wup_ref[:, inner + i * w:inner + (i + 1) * w])).astype(BF16)

    hcols = lambda h: slice(h * dh, (h + 1) * dh)
    down = []

    def gated_out(h, hn):
        hs = hn * ng_ref[:, hcols(h)] + sk_ref[:, hcols(h)] * xc_s[:, hcols(h)].astype(F32)
        act_s[:, hcols(h)] = (hs * sz_s[:, hcols(h)].astype(F32)).astype(BF16)

    def down_proj(h):
        if h == 0:
            down.append(_dot(act_s[...], wdown_ref[...]))

    _mlstm_chunk(q_s, kt_s, v_s, gacc, gacc.T[0:2 * heads, :], ct_s, n_s, m_s, gate_slab, gated_out, down_proj,
                 lc=lc, heads=heads, dh=dh)
    for _ in slabs:
        raise AssertionError("fewer fill points than output-gate slabs")
    o_ref[0] = x + _rms(down[0], g1_ref[...])

    @pl.when(c == pl.num_programs(1) - 1)
    def _():
        for h in range(heads):
            cout_ref[0, h] = ct_s[h].T
        nout_ref[0] = n_s[...]
        mout_ref[0] = m_s[...]


def _mlstm_layer(x, cbuf, state, p, g0, g1, lc):
    b, s, d = x.shape
    heads = p['heads']
    inner = p['m_cw'].shape[1]
    width = p['m_cw'].shape[0]
    dh = inner // heads
    zero_init = state is None
    kern = functools.partial(_mlstm_layer_kernel, lc=lc, heads=heads, inner=inner, width=width,
                             zero_init=zero_init)
    xs = pl.BlockSpec((1, lc, d), lambda i, c: (i, c, 0))
    bufs = pl.BlockSpec((1, width - 1, inner), lambda i, c: (i, 0, 0))
    cs = pl.BlockSpec((1, heads, dh, dh), lambda i, c: (i, 0, 0, 0))
    ns = pl.BlockSpec((1, heads, dh), lambda i, c: (i, 0, 0))
    ms = pl.BlockSpec((1, heads, V7X_LANES), lambda i, c: (i, 0, 0))
    consts = [g0, p['m_wup'], p['m_cw'], p['m_cb'], p['m_wq'], p['m_wkt'], p['m_wv'], p['m_gc'], p['m_gm'],
              p['m_bg'], p['m_ng'], p['m_skip'], p['m_wdown'], g1]
    in_specs = [xs, bufs] + [_const_spec(a.shape) for a in consts]
    args = [x, cbuf] + consts
    if not zero_init:
        in_specs += [cs, ns, ms]
        args += list(state)
    blocks = 2 * _nbytes((lc, d), F32) + (1 if zero_init else 2) * _nbytes((heads, dh, dh), F32)
    scratch = (sum(_nbytes(a.shape, a.dtype) for a in consts) + 6 * _nbytes((lc, inner), BF16)
               + _nbytes((heads, dh, dh), F32) + 8 * _nbytes((lc, max(lc, dh)), F32) + 2 * _nbytes((dh, dh), F32))
    return pl.pallas_call(
        kern,
        grid=(b, s // lc),
        in_specs=in_specs,
        out_specs=[xs, cs, ns, ms, bufs],
        out_shape=[jax.ShapeDtypeStruct((b, s, d), F32), jax.ShapeDtypeStruct((b, heads, dh, dh), F32),
                   jax.ShapeDtypeStruct((b, heads, dh), F32), jax.ShapeDtypeStruct((b, heads, V7X_LANES), F32),
                   jax.ShapeDtypeStruct((b, width - 1, inner), F32)],
        scratch_shapes=[pltpu.VMEM((V7X_SUBLANES, inner), F32),
                        pltpu.VMEM((1, lc, inner), BF16), pltpu.VMEM((1, inner, lc), BF16),
                        pltpu.VMEM((1, lc, inner), BF16), pltpu.VMEM((lc, inner), BF16),
                        pltpu.VMEM((lc, inner), BF16), pltpu.VMEM((lc, inner), BF16),
                        pltpu.VMEM((heads, dh, dh), F32), pltpu.VMEM((heads, dh), F32),
                        pltpu.VMEM((heads, V7X_LANES), F32)],
        compiler_params=pltpu.CompilerParams(
            dimension_semantics=("arbitrary", "arbitrary"), vmem_limit_bytes=_vmem_limit(blocks, scratch)),
        name="mlstm_layer",
    )(*args)


def _prep_weights(w):
    row = lambda v: v.astype(F32).reshape(1, -1)
    p = {}
    p['norm_g'] = w['norm_g'].astype(F32)
    p['s5_mats'] = _s5_matrices(w['s5_A_re'][0], w['s5_A_im'][0], w['s5_log_dt'][0], w['s5_B_re'][0],
                                w['s5_B_im'][0], w['s5_C_re'][0], w['s5_C_im'][0])
    p['s5_D'] = row(w['s5_D'][0])
    p['s5_w_glu'] = w['s5_w_glu'][0].astype(BF16)
    inner = w['mlstm_conv_w'].shape[2]
    heads = w['mlstm_b_gate'].shape[1] // 2
    dh = inner // heads
    p['m_wq'], p['m_wkt'], p['m_wv'], p['m_gc'], p['m_gm'] = _mlstm_weights(
        w['mlstm_wq'][0], w['mlstm_wk'][0], w['mlstm_wv'][0], w['mlstm_w_gate'][0], dh ** -0.5)
    bg = row(w['mlstm_b_gate'][0])
    p['m_bg'] = jnp.pad(bg, ((0, 0), (0, V7X_LANES - bg.shape[1])))
    p['m_wup'] = w['mlstm_w_up'][0].astype(BF16)
    p['m_cw'] = w['mlstm_conv_w'][0].astype(F32)
    p['m_cb'] = row(w['mlstm_conv_b'][0])
    p['m_ng'] = row(w['mlstm_norm_g'][0])
    p['m_skip'] = row(w['mlstm_skip'][0])
    p['m_wdown'] = w['mlstm_w_down'][0].astype(BF16)
    p['heads'] = heads
    p['f_wup'] = w['ffn_w_up'].astype(BF16)
    p['f_cw'] = w['ffn_conv_w'].astype(F32)
    p['f_cb'] = w['ffn_conv_b'].astype(F32)[:, None, :]
    p['f_wdown'] = w['ffn_w_down'].astype(BF16)
    return p


def _to_chunk_major(u2d, b, s):
    d = u2d.shape[1]
    g = d // S5_GROUP_CH
    u = u2d.reshape(b, s // S5_CHUNK, S5_CHUNK, g, S5_GROUP_CH)[:, :, ::-1]
    return jnp.transpose(u, (1, 0, 3, 2, 4)).reshape((s // S5_CHUNK) * b, d * S5_CHUNK)


def _to_token_major(yt, b, s):
    d = yt.shape[1] // S5_CHUNK
    g = d // S5_GROUP_CH
    y = yt.reshape(s // S5_CHUNK, b, g, S5_CHUNK, S5_GROUP_CH)
    return jnp.transpose(y, (1, 0, 3, 2, 4)).reshape(b * s, d)


def _trunk(x, st, p, tiles):
    b, s, d = x.shape
    ng = p['norm_g']
    g_of = lambda layer, k: ng[layer, k].reshape(1, d)
    zero_init = st is None
    groups = d // S5_GROUP_CH
    x2d = x.reshape(b * s, d)

    n_state = p['s5_mats'][1].shape[1] // 2
    if zero_init:
        h0re = h0im = jnp.zeros((groups // 2, b, 2 * n_state), F32)
    else:
        pair = lambda h: jnp.transpose(h.astype(F32).reshape(b, groups // 2, 2 * n_state), (1, 0, 2))
        h0re, h0im = pair(st['s5_re']), pair(st['s5_im'])
    if tiles.s5_time:
        y, hre, him = _s5_fused(x, g_of(0, 0), p['s5_mats'], h0re, h0im, tiles.s5_time)
        y2d = y.reshape(b * s, d)
    else:
        u = _norm_cast(x2d, g_of(0, 0), tiles.rows)
        yt, hre, him = _s5_core(_to_chunk_major(u, b, s), p['s5_mats'], h0re, h0im, b, tiles.s5_rows)
        y2d = _to_token_major(yt, b, s)
    unpair = lambda h: jnp.transpose(h, (1, 0, 2)).reshape(1, b, groups, n_state)
    o_re, o_im = unpair(hre), unpair(him)
    x2d = _glu(x2d, y2d, g_of(0, 0), p['s5_D'], p['s5_w_glu'], g_of(0, 1), tiles.rows)

    def ffn(x2d, layer):
        dff2 = p['f_wup'].shape[2]
        width = p['f_cw'].shape[1]
        buf = jnp.zeros((b, width - 1, dff2), F32) if zero_init else st['ffn_conv'][layer].astype(F32)
        xo, nbuf = _ffn(x2d.reshape(b, s, d), buf, g_of(layer, 2), p['f_wup'], p['f_cw'], p['f_cb'],
                        p['f_wdown'], g_of(layer, 3), layer, tiles.ffn_time)
        return xo.reshape(b * s, d), nbuf

    x2d, fbuf0 = ffn(x2d, 0)

    heads = p['heads']
    inner = p['m_cw'].shape[1]
    width = p['m_cw'].shape[0]
    cbuf = jnp.zeros((b, width - 1, inner), F32) if zero_init else st['mlstm_conv'].astype(F32)
    if zero_init:
        state = None
    else:
        state = (st['mlstm_C'].astype(F32), st['mlstm_n'].astype(F32),
                 jnp.broadcast_to(st['mlstm_m'].astype(F32)[:, :, None], (b, heads, V7X_LANES)))
    x3, c_out, n_out, m_out, ncbuf = _mlstm_layer(x2d.reshape(b, s, d), cbuf, state, p, g_of(1, 0), g_of(1, 1),
                                                  tiles.mlstm_chunk)
    x2d, fbuf1 = ffn(x3.reshape(b * s, d), 1)

    return (x2d.reshape(b, s, d), o_re, o_im, c_out[None], n_out[None], m_out[None, :, :, 0], ncbuf[None],
            jnp.stack([fbuf0, fbuf1], axis=0))


def kernel(x_prompt, x_sample, state_s5_re, state_s5_im, state_mlstm_C, state_mlstm_n, state_mlstm_m, state_mlstm_conv, state_ffn_conv, norm_g, s5_A_re, s5_A_im, s5_log_dt, s5_B_re, s5_B_im, s5_C_re, s5_C_im, s5_D, s5_w_glu, mlstm_w_up, mlstm_conv_w, mlstm_conv_b, mlstm_wq, mlstm_wk, mlstm_wv, mlstm_w_gate, mlstm_b_gate, mlstm_norm_g, mlstm_skip, mlstm_w_down, ffn_w_up, ffn_conv_w, ffn_conv_b, ffn_w_down):
    w = {
        'norm_g': norm_g, 's5_A_re': s5_A_re, 's5_A_im': s5_A_im, 's5_log_dt': s5_log_dt,
        's5_B_re': s5_B_re, 's5_B_im': s5_B_im, 's5_C_re': s5_C_re, 's5_C_im': s5_C_im, 's5_D': s5_D,
        's5_w_glu': s5_w_glu, 'mlstm_w_up': mlstm_w_up, 'mlstm_conv_w': mlstm_conv_w,
        'mlstm_conv_b': mlstm_conv_b, 'mlstm_wq': mlstm_wq, 'mlstm_wk': mlstm_wk, 'mlstm_wv': mlstm_wv,
        'mlstm_w_gate': mlstm_w_gate, 'mlstm_b_gate': mlstm_b_gate, 'mlstm_norm_g': mlstm_norm_g,
        'mlstm_skip': mlstm_skip, 'mlstm_w_down': mlstm_w_down, 'ffn_w_up': ffn_w_up,
        'ffn_conv_w': ffn_conv_w, 'ffn_conv_b': ffn_conv_b, 'ffn_w_down': ffn_w_down,
    }
    assert norm_g.shape[0] == 2 and s5_A_re.shape[0] == 1 and mlstm_w_up.shape[0] == 1
    assert x_prompt.shape[0] == V7X_SUBLANES and x_sample.shape[0] == V7X_SUBLANES
    p = _prep_weights(w)
    out_p = _trunk(x_prompt.astype(F32), None, p, _pick_tiles(*x_prompt.shape[:2]))
    st = {'s5_re': state_s5_re[0], 's5_im': state_s5_im[0], 'mlstm_C': state_mlstm_C[0],
          'mlstm_n': state_mlstm_n[0], 'mlstm_m': state_mlstm_m[0], 'mlstm_conv': state_mlstm_conv[0],
          'ffn_conv': state_ffn_conv}
    out_s = _trunk(x_sample.astype(F32), st, p, _pick_tiles(*x_sample.shape[:2]))
    return (out_p[0], out_s[0]) + tuple(out_p[1:]) + tuple(out_s[1:])
```

```python
import functools
import math
from typing import NamedTuple

import jax
import jax.numpy as jnp
from jax import lax
from jax.experimental import pallas as pl
from jax.experimental.pallas import tpu as pltpu

F32 = jnp.float32
BF16 = jnp.bfloat16
HI = lax.Precision.HIGHEST

NORM_EPS = 1e-6
LN_EPS = 1e-5
S5_GROUP_CH = 16
S5_CHUNK = 16
QKV_BLOCK = 4
FFN_ROW_BLOCK = 256

V7X_LANES = 128
V7X_SUBLANES = 8
V7X_MXU = 256
V7X_VMEM_BYTES = 64 * 1024 * 1024


class Tiles(NamedTuple):
    rows: int
    time: int
    s5_rows: int
    mlstm_chunk: int
    s5_time: int
    ffn_time: int


def _pick_tiles(batch, seq):
    s5_time = (V7X_LANES // batch) * S5_CHUNK
    return Tiles(rows=min(batch * seq, 512), time=min(seq, 512),
                 s5_rows=min(batch * (seq // S5_CHUNK), 512), mlstm_chunk=min(seq, 256),
                 s5_time=s5_time if seq % s5_time == 0 else 0, ffn_time=min(seq, 1024))


def _vmem_limit(block_bytes, scratch_bytes):
    want = 2 * block_bytes + scratch_bytes + 16 * 1024 * 1024
    return int(min(want, V7X_VMEM_BYTES - 8 * 1024 * 1024))


def _nbytes(shape, dtype):
    return math.prod(shape) * jnp.dtype(dtype).itemsize


def _rms(x, g):
    return x * lax.rsqrt(jnp.mean(x * x, axis=-1, keepdims=True) + NORM_EPS) * g


def _dot(a, b):
    return jnp.dot(a, b, preferred_element_type=F32)


def _dot_nt(a, b):
    return lax.dot_general(a, b, (((1,), (1,)), ((), ())), preferred_element_type=F32)


def _silu(x):
    return (0.5 * x) * (1.0 + jnp.tanh(0.5 * x))


def _causal_dwconv(u, prev, cw_ref, cb_ref, cs):
    width = cw_ref.shape[0]
    acc = cb_ref[:, cs] + u * cw_ref[width - 1:width, cs]
    rows = lax.broadcasted_iota(jnp.int32, prev.shape, 0)
    for j in range(width - 1):
        d = width - 1 - j
        rolled = pltpu.roll(u, d, 0)
        head = jnp.where(rows < d, pltpu.roll(prev, d, 0), rolled[0:V7X_SUBLANES])
        shifted = jnp.concatenate([head, rolled[V7X_SUBLANES:]], axis=0)
        acc = acc + shifted * cw_ref[j:j + 1, cs]
    return acc


def _log_sigmoid(x):
    return jnp.minimum(x, 0.0) - jnp.log1p(jnp.exp(-jnp.abs(x)))


def _split3(x):
    hi = x.astype(BF16)
    r1 = x - hi.astype(F32)
    mid = r1.astype(BF16)
    lo = (r1 - mid.astype(F32)).astype(BF16)
    return hi, mid, lo


def _norm_cast_kernel(x_ref, g_ref, o_ref):
    o_ref[...] = _rms(x_ref[...], g_ref[...]).astype(o_ref.dtype)


def _norm_cast(x2d, g, tb):
    rows, d = x2d.shape
    return pl.pallas_call(
        _norm_cast_kernel,
        grid=(rows // tb,),
        in_specs=[pl.BlockSpec((tb, d), lambda i: (i, 0)), pl.BlockSpec((1, d), lambda i: (0, 0))],
        out_specs=pl.BlockSpec((tb, d), lambda i: (i, 0)),
        out_shape=jax.ShapeDtypeStruct((rows, d), BF16),
        name="s5_norm",
    )(x2d, g)


def _s5_kernel(u_ref, tt_ref, pt_ref, qt_ref, are_ref, aim_ref, h0re_ref, h0im_ref,
               y_ref, hre_ref, him_ref, pure_s, puim_s, hinre_s, hinim_s, stre_s, stim_s, *, nb, rb):
    r = pl.program_id(1)

    @pl.when(r == 0)
    def _():
        stre_s[...] = h0re_ref[0]
        stim_s[...] = h0im_ref[0]

    u = u_ref[...]
    w = V7X_MXU
    half = pt_ref.shape[1] // 2
    pu = [_dot_nt(u[:, i * w:(i + 1) * w], pt_ref[i]) for i in range(2)]
    pure_s[...] = jnp.concatenate([pu[0][:, :half], pu[1][:, :half]], axis=1)
    puim_s[...] = jnp.concatenate([pu[0][:, half:], pu[1][:, half:]], axis=1)
    ar = jnp.broadcast_to(are_ref[0], (nb, 2 * half))
    ai = jnp.broadcast_to(aim_ref[0], (nb, 2 * half))

    def step(i, carry):
        re, im = carry
        rows = pl.ds(pl.multiple_of(i * nb, nb), nb)
        hinre_s[rows, :] = re
        hinim_s[rows, :] = im
        return (ar * re - ai * im + pure_s[rows, :], ar * im + ai * re + puim_s[rows, :])

    re, im = lax.fori_loop(0, rb // nb, step, (stre_s[...], stim_s[...]))
    stre_s[...] = re
    stim_s[...] = im
    hre_ref[0] = re
    him_ref[0] = im

    hre = hinre_s[...]
    him = hinim_s[...]
    for i in range(2):
        hin = jnp.concatenate([hre[:, i * half:(i + 1) * half], him[:, i * half:(i + 1) * half]], axis=1)
        y_ref[:, i * w:(i + 1) * w] = (_dot_nt(u[:, i * w:(i + 1) * w], tt_ref[i])
                                       + _dot_nt(hin.astype(BF16), qt_ref[i])).astype(y_ref.dtype)


def _s5_core(ut, mats, h0re, h0im, nb, rb):
    rows, cols = ut.shape
    pairs = cols // (2 * V7X_MXU)
    tt, pt, qt, are, aim = mats
    nst = pt.shape[1]
    pair3 = lambda p, r: (p, 0, 0)
    kern = functools.partial(_s5_kernel, nb=nb, rb=rb)
    blocks = (_nbytes((rb, 512), BF16) * 2 + _nbytes((2, 256, 256), BF16) + 4 * _nbytes((256, 128), BF16))
    scratch = 4 * _nbytes((rb, 128), F32) + 2 * _nbytes((nb, 128), F32)
    return pl.pallas_call(
        kern,
        grid=(pairs, rows // rb),
        in_specs=[
            pl.BlockSpec((rb, 2 * V7X_MXU), lambda p, r: (r, p)),
            pl.BlockSpec((2, V7X_MXU, V7X_MXU), pair3),
            pl.BlockSpec((2, nst, V7X_MXU), pair3),
            pl.BlockSpec((2, V7X_MXU, nst), pair3),
            pl.BlockSpec((1, 1, 128), pair3),
            pl.BlockSpec((1, 1, 128), pair3),
            pl.BlockSpec((1, nb, 128), pair3),
            pl.BlockSpec((1, nb, 128), pair3),
        ],
        out_specs=[
            pl.BlockSpec((rb, 2 * V7X_MXU), lambda p, r: (r, p)),
            pl.BlockSpec((1, nb, 128), pair3),
            pl.BlockSpec((1, nb, 128), pair3),
        ],
        out_shape=[
            jax.ShapeDtypeStruct((rows, cols), BF16),
            jax.ShapeDtypeStruct((pairs, nb, 128), F32),
            jax.ShapeDtypeStruct((pairs, nb, 128), F32),
        ],
        scratch_shapes=[pltpu.VMEM((rb, 128), F32)] * 4 + [pltpu.VMEM((nb, 128), F32)] * 2,
        compiler_params=pltpu.CompilerParams(
            dimension_semantics=("arbitrary", "arbitrary"), vmem_limit_bytes=_vmem_limit(blocks, scratch)),
        name="s5_core",
    )(ut, tt, pt, qt, are, aim, h0re, h0im)


def _s5_fused_kernel(x_ref, g_ref, tt_ref, pt_ref, qt_ref, are_ref, aim_ref, h0re_ref, h0im_ref,
                     y_ref, hre_ref, him_ref,
                     us, ugt, pure, puim, hinre, hinim, ytmp, ys, stre, stim, *, nb, tbt):
    lc = S5_CHUNK
    ch = S5_GROUP_CH
    gps = V7X_LANES // ch
    nch = tbt // lc
    cols = nb * nch
    k = pl.program_id(1)

    @pl.when(k == 0)
    def _():
        for b in range(nb):
            u = _rms(x_ref[b], g_ref[...])
            for kk in range(us.shape[0]):
                us[kk, pl.ds(b, tbt, stride=nb), :] = u[:, kk * V7X_LANES:(kk + 1) * V7X_LANES]

    @pl.when(pl.program_id(0) == 0)
    def _():
        stre[k] = h0re_ref[...]
        stim[k] = h0im_ref[...]

    tile = lambda j, s: slice((j * lc + s) * nb, (j * lc + s + 1) * nb)
    for s in range(lc):
        a = jnp.concatenate([us[k, tile(j, s), :] for j in range(nch)], axis=0).T
        for g in range(gps):
            ugt[g, (lc - 1 - s) * ch:(lc - s) * ch, :] = a[g * ch:(g + 1) * ch, :].astype(BF16)

    half = pt_ref.shape[1] // 2
    for pr in range(gps // 2):
        put0 = _dot(pt_ref[2 * pr], ugt[2 * pr])
        put1 = _dot(pt_ref[2 * pr + 1], ugt[2 * pr + 1])
        pure[pr] = jnp.concatenate([put0[:half], put1[:half]], axis=0).T
        puim[pr] = jnp.concatenate([put0[half:], put1[half:]], axis=0).T

    for pr in range(gps // 2):
        re = stre[k, pr]
        im = stim[k, pr]
        ar = jnp.broadcast_to(are_ref[pr], re.shape)
        ai = jnp.broadcast_to(aim_ref[pr], re.shape)
        for j in range(nch):
            rows = slice(j * nb, (j + 1) * nb)
            hinre[pr, rows, :] = re
            hinim[pr, rows, :] = im
            re, im = (ar * re - ai * im + pure[pr, rows, :], ar * im + ai * re + puim[pr, rows, :])
        stre[k, pr] = re
        stim[k, pr] = im
        hre_ref[k * (gps // 2) + pr] = re
        him_ref[k * (gps // 2) + pr] = im

    for pr in range(gps // 2):
        hre_t = hinre[pr].T
        him_t = hinim[pr].T
        for i in range(2):
            g = 2 * pr + i
            hin_t = jnp.concatenate([hre_t[i * half:(i + 1) * half], him_t[i * half:(i + 1) * half]],
                                    axis=0).astype(BF16)
            ytmp[g] = _dot(tt_ref[g], ugt[g]) + _dot(qt_ref[g], hin_t)

    for t in range(lc):
        zt = jnp.concatenate([ytmp[g, t * ch:(t + 1) * ch, :] for g in range(gps)], axis=0).T
        for j in range(nch):
            ys[tile(j, t), :] = zt[j * nb:(j + 1) * nb, :]
    for b in range(nb):
        y_ref[b] = ys[pl.ds(b, tbt, stride=nb), :].astype(y_ref.dtype)


def _s5_fused(x, g, mats_t, h0re, h0im, tbt):
    nb, s, d = x.shape
    tt, pt, qt, are, aim = mats_t
    groups = tt.shape[0]
    gps = V7X_LANES // S5_GROUP_CH
    slabs = groups // gps
    w = S5_CHUNK * S5_GROUP_CH
    nst = pt.shape[1]
    cols = nb * (tbt // S5_CHUNK)
    kern = functools.partial(_s5_fused_kernel, nb=nb, tbt=tbt)
    slab3 = lambda t, k: (k, 0, 0)
    blocks = (_nbytes((nb, tbt, d), F32) + _nbytes((nb, tbt, V7X_LANES), BF16)
              + gps * (_nbytes((w, w), BF16) + 2 * _nbytes((w, nst), BF16)))
    scratch = (_nbytes((slabs, nb * tbt, V7X_LANES), F32) + _nbytes((gps, w, cols), BF16)
               + 4 * _nbytes((gps // 2, cols, nst), F32) + _nbytes((gps, w, cols), F32)
               + _nbytes((nb * tbt, V7X_LANES), F32))
    return pl.pallas_call(
        kern,
        grid=(s // tbt, slabs),
        in_specs=[
            pl.BlockSpec((nb, tbt, d), lambda t, k: (0, t, 0)),
            pl.BlockSpec((1, d), lambda t, k: (0, 0)),
            pl.BlockSpec((gps, w, w), slab3),
            pl.BlockSpec((gps, nst, w), slab3),
            pl.BlockSpec((gps, w, nst), slab3),
            pl.BlockSpec((gps // 2, 1, nst), slab3),
            pl.BlockSpec((gps // 2, 1, nst), slab3),
            pl.BlockSpec((gps // 2, nb, nst), slab3),
            pl.BlockSpec((gps // 2, nb, nst), slab3),
        ],
        out_specs=[
            pl.BlockSpec((nb, tbt, V7X_LANES), lambda t, k: (0, t, k)),
            pl.BlockSpec((groups // 2, nb, nst), lambda t, k: (0, 0, 0)),
            pl.BlockSpec((groups // 2, nb, nst), lambda t, k: (0, 0, 0)),
        ],
        out_shape=[
            jax.ShapeDtypeStruct((nb, s, d), BF16),
            jax.ShapeDtypeStruct((groups // 2, nb, nst), F32),
            jax.ShapeDtypeStruct((groups // 2, nb, nst), F32),
        ],
        scratch_shapes=[
            pltpu.VMEM((slabs, nb * tbt, V7X_LANES), F32),
            pltpu.VMEM((gps, w, cols), BF16),
            pltpu.VMEM((gps // 2, cols, nst), F32),
            pltpu.VMEM((gps // 2, cols, nst), F32),
            pltpu.VMEM((gps // 2, cols, nst), F32),
            pltpu.VMEM((gps // 2, cols, nst), F32),
            pltpu.VMEM((gps, w, cols), F32),
            pltpu.VMEM((nb * tbt, V7X_LANES), F32),
            pltpu.VMEM((slabs, gps // 2, nb, nst), F32),
            pltpu.VMEM((slabs, gps // 2, nb, nst), F32),
        ],
        compiler_params=pltpu.CompilerParams(
            dimension_semantics=("arbitrary", "arbitrary"), vmem_limit_bytes=_vmem_limit(blocks, scratch)),
        name="s5_fused",
    )(x, g, tt, pt, qt, are, aim, h0re, h0im)


def _s5_matrices(a_re, a_im, log_dt, b_re, b_im, c_re, c_im):
    lc = S5_CHUNK
    groups, n = a_re.shape
    ch = S5_GROUP_CH
    f = lambda v: v.astype(F32)
    a_re, a_im, b_re, b_im, c_re, c_im = map(f, (a_re, a_im, b_re, b_im, c_re, c_im))
    dt = jnp.exp(f(log_dt))[:, None, None]
    j = jnp.arange(lc + 1, dtype=F32)
    mag = jnp.exp((a_re[:, :, None] * dt) * j)
    ang = (a_im[:, :, None] * dt) * j
    pw_re, pw_im = mag * jnp.cos(ang), mag * jnp.sin(ang)
    e_re, e_im = pw_re[:, :, 1] - 1.0, pw_im[:, :, 1]
    inv = 1.0 / (a_re * a_re + a_im * a_im)
    f_re, f_im = (e_re * a_re + e_im * a_im) * inv, (e_im * a_re - e_re * a_im) * inv
    bb_re = f_re[..., None] * b_re - f_im[..., None] * b_im
    bb_im = f_re[..., None] * b_im + f_im[..., None] * b_re
    rep = lambda v: jnp.repeat(v, ch, axis=2)
    til = lambda v: jnp.tile(v, (1, 1, lc + 1))
    lb_re = rep(pw_re) * til(bb_re) - rep(pw_im) * til(bb_im)
    lb_im = rep(pw_re) * til(bb_im) + rep(pw_im) * til(bb_re)
    kern = jnp.einsum('gcm,gmx->gcx', jnp.concatenate([c_re, c_im], axis=2),
                      jnp.concatenate([lb_re, -lb_im], axis=1), precision=HI)
    w = lc * ch
    kfwd = jnp.pad(kern[:, :, :w], ((0, 0), (0, 0), ((lc - 1) * ch, 0)))
    tt = jnp.stack([kfwd[:, :, t * ch:t * ch + w] for t in range(lc)], axis=1).reshape(groups, w, w)
    pt = jnp.concatenate([lb_re[:, :, :w], lb_im[:, :, :w]], axis=1)
    pwt_re = jnp.swapaxes(pw_re, 1, 2)[:, 1:, None, :]
    pwt_im = jnp.swapaxes(pw_im, 1, 2)[:, 1:, None, :]
    ca_re = c_re[:, None] * pwt_re - c_im[:, None] * pwt_im
    ca_im = c_re[:, None] * pwt_im + c_im[:, None] * pwt_re
    qt = jnp.concatenate([ca_re, -ca_im], axis=3).reshape(groups, w, 2 * n)
    are = pw_re[:, :, lc].reshape(groups // 2, 1, 2 * n)
    aim = pw_im[:, :, lc].reshape(groups // 2, 1, 2 * n)
    return tt.astype(BF16), pt.astype(BF16), qt.astype(BF16), are, aim


def _glu_kernel(x_ref, y_ref, g0_ref, d_ref, w_ref, g1_ref, o_ref, out_s):
    x = x_ref[...]
    d = x.shape[1]
    u = _rms(x, g0_ref[...])
    yy = y_ref[...].astype(F32) + d_ref[...] * u
    z = jax.nn.gelu(yy, approximate=True).astype(BF16)
    w = V7X_MXU
    nslab = d // w
    pair = lambda j: (_dot(z, w_ref[:, j * w:(j + 1) * w]), _dot(z, w_ref[:, d + j * w:d + (j + 1) * w]))
    nxt = pair(0)
    ssq = jnp.zeros((x.shape[0], 1), F32)
    for j in range(nslab):
        a, b = nxt
        if j + 1 < nslab:
            nxt = pair(j + 1)
        out = a * (0.5 * (1.0 + jnp.tanh(0.5 * b)))
        out_s[:, j * w:(j + 1) * w] = out
        ssq = ssq + jnp.sum(out * out, axis=-1, keepdims=True)
    scale = lax.rsqrt(ssq * (1.0 / d) + NORM_EPS)
    o_ref[...] = x + out_s[...] * scale * g1_ref[...]


def _const_spec(shape):
    nd = len(shape)
    return pl.BlockSpec(shape, lambda *_: (0,) * nd, pipeline_mode=pl.Buffered(1))


def _glu(x2d, y2d, g0, dskip, w, g1, tb):
    rows, d = x2d.shape
    blocks = 2 * _nbytes((tb, d), F32) + _nbytes((tb, d), BF16)
    scratch = _nbytes(w.shape, BF16) + 3 * _nbytes((tb, 2 * d), F32)
    row = pl.BlockSpec((tb, d), lambda i: (i, 0))
    return pl.pallas_call(
        _glu_kernel,
        grid=(rows // tb,),
        in_specs=[row, row, _const_spec((1, d)), _const_spec((1, d)), _const_spec(w.shape), _const_spec((1, d))],
        out_specs=row,
        out_shape=jax.ShapeDtypeStruct((rows, d), F32),
        scratch_shapes=[pltpu.VMEM((tb, d), F32)],
        compiler_params=pltpu.CompilerParams(
            dimension_semantics=("arbitrary",), vmem_limit_bytes=_vmem_limit(blocks, scratch)),
        name="s5_glu",
    )(x2d, y2d, g0, dskip, w, g1)


def _ffn_kernel(x_ref, buf_ref, g2_ref, wup_ref, cw_ref, cb_ref, wdown_ref, g3_ref,
                o_ref, nbuf_ref, carry, act, *, tb, sub, dff, width):
    t = pl.program_id(1)
    pad = V7X_SUBLANES

    @pl.when(t == 0)
    def _():
        carry[...] = jnp.zeros_like(carry)
        carry[pad - (width - 1):pad, :] = buf_ref[0]

    fb = V7X_MXU
    nslab = dff // fb
    cols = lambda j: (slice(j * fb, (j + 1) * fb), slice(dff + j * fb, dff + (j + 1) * fb))

    def conv(u, cs):
        prev = carry[:, cs]
        carry[:, cs] = u[sub - pad:sub, :]
        return _causal_dwconv(u, prev, cw_ref, cb_ref, cs)

    def up_phase(r):
        hn = _rms(x_ref[0, r * sub:(r + 1) * sub, :], g2_ref[...]).astype(BF16)
        up = lambda j: tuple(_dot(hn, wup_ref[:, cs]) for cs in cols(j))
        nxt = up(0)
        for j in range(nslab):
            ug, uv = nxt
            if j + 1 < nslab:
                nxt = up(j + 1)
            gate = conv(ug, cols(j)[0])
            val = conv(uv, cols(j)[1])
            act[r % 2, :, j * fb:(j + 1) * fb] = (jax.nn.gelu(gate, approximate=True) * val).astype(BF16)

    nsub = tb // sub
    up_phase(0)
    for r in range(nsub):
        f = _dot(act[r % 2], wdown_ref[...])
        if r + 1 < nsub:
            up_phase(r + 1)
        rows = slice(r * sub, (r + 1) * sub)
        o_ref[0, rows, :] = x_ref[0, rows, :] + _rms(f, g3_ref[...])
    nbuf_ref[0] = carry[pad - (width - 1):pad, :]


def _layer_spec(shape, layer):
    nd = len(shape) - 1
    return pl.BlockSpec((None,) + tuple(shape[1:]), lambda *_: (layer,) + (0,) * nd, pipeline_mode=pl.Buffered(1))


def _ffn(x, buf, g2, wup, cw, cb, wdown, g3, layer, tb):
    b, s, d = x.shape
    dff = wdown.shape[1]
    width = cw.shape[1]
    sub = min(tb, FFN_ROW_BLOCK)
    kern = functools.partial(_ffn_kernel, tb=tb, sub=sub, dff=dff, width=width)
    blocks = 2 * _nbytes((tb, d), F32)
    scratch = (_nbytes(wup.shape[1:], BF16) + _nbytes(wdown.shape[1:], BF16) + 6 * _nbytes((sub, V7X_MXU), F32)
               + _nbytes((8, 2 * dff), F32) + 2 * _nbytes((sub, dff), BF16) + 3 * _nbytes((sub, d), F32))
    xs = pl.BlockSpec((1, tb, d), lambda i, t: (i, t, 0))
    bs = pl.BlockSpec((1, width - 1, 2 * dff), lambda i, t: (i, 0, 0))
    return pl.pallas_call(
        kern,
        grid=(b, s // tb),
        in_specs=[xs, bs, _const_spec((1, d)), _layer_spec(wup.shape, layer), _layer_spec(cw.shape, layer),
                  _layer_spec(cb.shape, layer), _layer_spec(wdown.shape, layer), _const_spec((1, d))],
        out_specs=[xs, bs],
        out_shape=[jax.ShapeDtypeStruct((b, s, d), F32), jax.ShapeDtypeStruct((b, width - 1, 2 * dff), F32)],
        scratch_shapes=[pltpu.VMEM((V7X_SUBLANES, 2 * dff), F32), pltpu.VMEM((2, sub, dff), BF16)],
        compiler_params=pltpu.CompilerParams(
            dimension_semantics=("arbitrary", "arbitrary"), vmem_limit_bytes=_vmem_limit(blocks, scratch)),
        name="conv_ffn",
    )(x, buf, g2, wup, cw, cb, wdown, g3)


def _mlstm_weights_kernel(aq_ref, ak_ref, av_ref, gq_ref, gk_ref, gv_ref,
                          wq_ref, wkt_ref, wv_ref, gc_ref, gm_ref, *, kscale):
    w = V7X_MXU
    kb = QKV_BLOCK
    dotp = functools.partial(jnp.dot, preferred_element_type=F32, precision=HI)
    lane = lax.broadcasted_iota(jnp.int32, (V7X_LANES, w), 1)
    src = lax.broadcasted_iota(jnp.int32, (V7X_LANES, w), 0)
    spread = jnp.where(jnp.bitwise_and(lane, kb - 1) == src, 1.0, 0.0)
    shift = kb.bit_length() - 1
    rblk = lax.shift_right_logical(lax.broadcasted_iota(jnp.int32, (w, w), 0), shift)
    cblk = lax.shift_right_logical(lax.broadcasted_iota(jnp.int32, (w, w), 1), shift)
    tile = lambda a_ref: jnp.where(rblk == cblk, dotp(a_ref[0], spread), 0.0)
    tq, tk, tv = tile(aq_ref), tile(ak_ref), tile(av_ref)
    wq_ref[0] = tq.astype(BF16)
    wkt_ref[0] = (tk.T * kscale).astype(BF16)
    wv_ref[0] = tv.astype(BF16)
    gc_ref[...] = (dotp(tq, gq_ref[...]) + dotp(tk, gk_ref[...])).astype(BF16)
    gm_ref[...] = dotp(tv, gv_ref[...]).astype(BF16)


def _mlstm_weights(wq, wk, wv, w_gate, kscale):
    nblk, kb, _ = wq.shape
    assert kb == QKV_BLOCK and kb & (kb - 1) == 0
    w = V7X_MXU
    per = w // kb
    nt = nblk // per
    inner = nt * w
    compact = lambda m: jnp.pad(m.astype(F32).reshape(nt, w, kb), ((0, 0), (0, 0), (0, V7X_LANES - kb)))
    wg = jnp.pad(w_gate.astype(F32), ((0, 0), (0, V7X_LANES - w_gate.shape[1])))
    cm = pl.BlockSpec((1, w, V7X_LANES), lambda i: (i, 0, 0))
    gs = [pl.BlockSpec((w, V7X_LANES), lambda i, k=k: (k * nt + i, 0)) for k in range(3)]
    tile = pl.BlockSpec((1, w, w), lambda i: (i, 0, 0))
    fold = pl.BlockSpec((w, V7X_LANES), lambda i: (i, 0))
    return pl.pallas_call(
        functools.partial(_mlstm_weights_kernel, kscale=kscale),
        grid=(nt,),
        in_specs=[cm, cm, cm] + gs,
        out_specs=[tile, tile, tile, fold, fold],
        out_shape=[jax.ShapeDtypeStruct((nt, w, w), BF16)] * 3 + [jax.ShapeDtypeStruct((inner, V7X_LANES), BF16)] * 2,
        name="mlstm_weights",
    )(compact(wq), compact(wk), compact(wv), wg, wg, wg)


def _mpre_kernel(x_ref, cbuf_ref, g_ref, wup_ref, cw_ref, cb_ref, wq_ref, wkt_ref, wv_ref, gc_ref, gm_ref,
                 bg_ref, q_ref, kt_ref, v_ref, xc_ref, z_ref, gates_ref, nbuf_ref, carry,
                 *, tb, inner, width):
    t = pl.program_id(1)
    pad = V7X_SUBLANES

    @pl.when(t == 0)
    def _():
        carry[...] = jnp.zeros_like(carry)
        carry[pad - (width - 1):pad, :] = cbuf_ref[0]

    h = _rms(x_ref[0], g_ref[...]).astype(BF16)
    w = V7X_MXU
    gacc = jnp.zeros((tb, bg_ref.shape[1]), F32) + bg_ref[...]
    for i in range(inner // w):
        cs = slice(i * w, (i + 1) * w)
        xm = _dot(h, wup_ref[:, cs])
        prev = carry[:, cs]
        carry[:, cs] = xm[tb - pad:tb, :]
        xc = _silu(_causal_dwconv(xm, prev, cw_ref, cb_ref, cs)).astype(BF16)
        xmb = xm.astype(BF16)
        q_ref[0, :, cs] = _dot(xc, wq_ref[i]).astype(BF16)
        kt_ref[0, cs, :] = _dot_nt(wkt_ref[i], xc).astype(BF16)
        v_ref[0, :, cs] = _dot(xmb, wv_ref[i]).astype(BF16)
        xc_ref[0, :, cs] = xc
        z_ref[0, :, cs] = _silu(_dot(h, wup_ref[:, inner + i * w:inner + (i + 1) * w])).astype(BF16)
        gacc = gacc + _dot(xc, gc_ref[cs, :]) + _dot(xmb, gm_ref[cs, :])
    gates_ref[0] = gacc
    nbuf_ref[0] = carry[pad - (width - 1):pad, :]


def _mpre(x, cbuf, g, wup, cw, cb, wq_t, wkt_t, wv_t, gc, gm, bg, tb):
    b, s, d = x.shape
    inner = cw.shape[1]
    width = cw.shape[0]
    ng = bg.shape[1]
    kern = functools.partial(_mpre_kernel, tb=tb, inner=inner, width=width)
    act = pl.BlockSpec((1, tb, inner), lambda i, t: (i, t, 0))
    blocks = _nbytes((tb, d), F32) + 5 * _nbytes((tb, inner), BF16)
    scratch = (_nbytes(wup.shape, BF16) + 3 * _nbytes(wq_t.shape, BF16) + 2 * _nbytes((inner, 128), BF16)
               + _nbytes((8, inner), F32) + 8 * _nbytes((tb, V7X_MXU), F32))
    return pl.pallas_call(
        kern,
        grid=(b, s // tb),
        in_specs=[pl.BlockSpec((1, tb, d), lambda i, t: (i, t, 0)),
                  pl.BlockSpec((1, width - 1, inner), lambda i, t: (i, 0, 0)),
                  _const_spec((1, d)), _const_spec(wup.shape), _const_spec(cw.shape), _const_spec((1, inner)),
                  _const_spec(wq_t.shape), _const_spec(wkt_t.shape), _const_spec(wv_t.shape),
                  _const_spec(gc.shape), _const_spec(gm.shape), _const_spec((1, ng))],
        out_specs=[act, pl.BlockSpec((1, inner, tb), lambda i, t: (i, 0, t)), act, act, act,
                   pl.BlockSpec((1, tb, ng), lambda i, t: (i, t, 0)),
                   pl.BlockSpec((1, width - 1, inner), lambda i, t: (i, 0, 0))],
        out_shape=[jax.ShapeDtypeStruct((b, s, inner), BF16), jax.ShapeDtypeStruct((b, inner, s), BF16),
                   jax.ShapeDtypeStruct((b, s, inner), BF16), jax.ShapeDtypeStruct((b, s, inner), BF16),
                   jax.ShapeDtypeStruct((b, s, inner), BF16), jax.ShapeDtypeStruct((b, s, ng), F32),
                   jax.ShapeDtypeStruct((b, width - 1, inner), F32)],
        scratch_shapes=[pltpu.VMEM((V7X_SUBLANES, inner), F32)],
        compiler_params=pltpu.CompilerParams(
            dimension_semantics=("arbitrary", "arbitrary"), vmem_limit_bytes=_vmem_limit(blocks, scratch)),
        name="mlstm_pre",
    )(x, cbuf, g, wup, cw, cb, wq_t, wkt_t, wv_t, gc, gm, bg)


def _mlstm_chunk(q_ref, kt_ref, v_ref, g, gt, ct_s, n_s, m_s, fill, emit_hn, fill_update, *, lc, heads, dh):
    row = lax.broadcasted_iota(jnp.int32, (lc, lc), 0)
    col = lax.broadcasted_iota(jnp.int32, (lc, lc), 1)
    causal = col <= row
    tri_l = jnp.where(causal, 1.0, 0.0).astype(BF16)
    tri_u = jnp.where(row <= col, 1.0, 0.0).astype(BF16)
    b_cols = sum(_dot(tri_l, part) for part in _split3(_log_sigmoid(g)))
    b_rows = sum(_dot(part, tri_u) for part in _split3(_log_sigmoid(gt)))

    hsl = [slice(h * dh, (h + 1) * dh) for h in range(heads)]
    st = []
    for h in range(heads):
        b_col = b_cols[:, heads + h:heads + h + 1]
        b_row = b_rows[heads + h:heads + h + 1, :]
        i_row = gt[h:h + 1, :]
        m_prev = m_s[h:h + 1, 0:1]
        dmat = jnp.where(causal, b_col - b_row + i_row, -jnp.inf)
        inter = b_col + m_prev
        m_t = jnp.maximum(inter, jnp.max(dmat, axis=1, keepdims=True))
        st.append(dict(b_col=b_col, b_row=b_row, i_row=i_row, m_prev=m_prev, m_t=m_t,
                       wts=jnp.exp(dmat - m_t), a=jnp.exp(inter - m_t)))
        fill()
    for h in range(heads):
        s = st[h]
        s['sc'] = _dot(q_ref[0, :, hsl[h]], kt_ref[0, hsl[h], :]) * s['wts']
    for h in range(heads):
        fill()
        s = st[h]
        qh = q_ref[0, :, hsl[h]]
        sc = s['sc']
        a = s['a']
        num = _dot(sc.astype(BF16), v_ref[0, :, hsl[h]]) + a * _dot(qh, ct_s[h].astype(BF16))
        qn = jnp.sum(qh.astype(F32) * n_s[h:h + 1, :], axis=1, keepdims=True)
        den = jnp.sum(sc, axis=1, keepdims=True) + a * qn
        hh = num / jnp.maximum(jnp.abs(den), jnp.exp(-s['m_t']))
        mu = jnp.mean(hh, axis=1, keepdims=True)
        dev = hh - mu
        var = jnp.mean(dev * dev, axis=1, keepdims=True)
        emit_hn(h, dev * lax.rsqrt(var + LN_EPS))
    for h in range(heads):
        fill_update(h)
        s = st[h]
        kth = kt_ref[0, hsl[h], :]
        b_last = s['b_row'][:, lc - 1:lc]
        g_row = b_last - s['b_row'] + s['i_row']
        m_new = jnp.maximum(b_last + s['m_prev'], jnp.max(g_row, axis=1, keepdims=True))
        decay = jnp.exp(b_last + s['m_prev'] - m_new)
        wg_row = jnp.exp(g_row - m_new)
        wg_col = jnp.exp(b_last - s['b_col'] + g[:, h:h + 1] - m_new)
        wv = (v_ref[0, :, hsl[h]].astype(F32) * wg_col).astype(BF16)
        ct_s[h] = decay * ct_s[h] + _dot(kth, wv)
        wg16 = jnp.broadcast_to(wg_row, (16, lc)).astype(BF16)
        n_s[h:h + 1, :] = decay * n_s[h:h + 1, :] + _dot_nt(wg16, kth)[0:1, :]
        m_s[h:h + 1, :] = jnp.broadcast_to(m_new, (1, V7X_LANES))


def _mlstm_layer_kernel(*refs, lc, heads, inner, width, zero_init):
    (x_ref, cbuf_ref, g0_ref, wup_ref, cw_ref, cb_ref, wq_ref, wkt_ref, wv_ref, gc_ref, gm_ref, bg_ref,
     ng_ref, sk_ref, wdown_ref, g1_ref) = refs[:16]
    n_in = 16
    if not zero_init:
        c0_ref, n0_ref, m0_ref = refs[16:19]
        n_in = 19
    o_ref, cout_ref, nout_ref, mout_ref, nbuf_ref = refs[n_in:n_in + 5]
    carry, q_s, kt_s, v_s, xc_s, sz_s, act_s, ct_s, n_s, m_s = refs[n_in + 5:]
    dh = inner // heads
    pad = V7X_SUBLANES
    c = pl.program_id(1)

    @pl.when(c == 0)
    def _():
        carry[...] = jnp.zeros_like(carry)
        carry[pad - (width - 1):pad, :] = cbuf_ref[0]
        if zero_init:
            ct_s[...] = jnp.zeros_like(ct_s)
            n_s[...] = jnp.zeros_like(n_s)
            m_s[...] = jnp.zeros_like(m_s)
        else:
            for h in range(heads):
                ct_s[h] = c0_ref[0, h].T
            n_s[...] = n0_ref[0]
            m_s[...] = m0_ref[0]

    x = x_ref[0]
    hx = _rms(x, g0_ref[...]).astype(BF16)
    w = V7X_MXU
    gacc = jnp.zeros((lc, bg_ref.shape[1]), F32) + bg_ref[...]
    nslab = inner // w
    col = lambda i: slice(i * w, (i + 1) * w)

    def project(i, xc, xmb, gacc):
        q_s[0, :, col(i)] = _dot(xc, wq_ref[i]).astype(BF16)
        kt_s[0, col(i), :] = _dot_nt(wkt_ref[i], xc).astype(BF16)
        v_s[0, :, col(i)] = _dot(xmb, wv_ref[i]).astype(BF16)
        xc_s[:, col(i)] = xc
        return gacc + _dot(xc, gc_ref[col(i), :]) + _dot(xmb, gm_ref[col(i), :])

    xm_next = _dot(hx, wup_ref[:, col(0)])
    pending = None
    for i in range(nslab):
        xm = xm_next
        if i + 1 < nslab:
            xm_next = _dot(hx, wup_ref[:, col(i + 1)])
        if pending is not None:
            gacc = project(*pending, gacc)
        prev = carry[:, col(i)]
        carry[:, col(i)] = xm[lc - pad:lc, :]
        xc = _silu(_causal_dwconv(xm, prev, cw_ref, cb_ref, col(i))).astype(BF16)
        pending = (i, xc, xm.astype(BF16))
    gacc = project(*pending, gacc)
    nbuf_ref[0] = carry[pad - (width - 1):pad, :]

    slabs = iter(range(inner // w))

    def gate_slab():
        i = next(slabs, None)
        if i is not None:
            cs = slice(i * w, (i + 1) * w)
            sz_s[:, cs] = _silu(_dot(hx, wup_ref[:, inner + i * w:inner + (i + 1) * w])).astype(BF16)

    hcols = lambda h: slice(h * dh, (h + 1) * dh)
    down = []

    def gated_out(h, hn):
        hs = hn * ng_ref[:, hcols(h)] + sk_ref[:, hcols(h)] * xc_s[:, hcols(h)].astype(F32)
        act_s[:, hcols(h)] = (hs * sz_s[:, hcols(h)].astype(F32)).astype(BF16)

    def down_proj(h):
        if h == 0:
            down.append(_dot(act_s[...], wdown_ref[...]))

    _mlstm_chunk(q_s, kt_s, v_s, gacc, gacc.T[0:2 * heads, :], ct_s, n_s, m_s, gate_slab, gated_out, down_proj,
                 lc=lc, heads=heads, dh=dh)
    for _ in slabs:
        raise AssertionError("fewer fill points than output-gate slabs")
    o_ref[0] = x + _rms(down[0], g1_ref[...])

    @pl.when(c == pl.num_programs(1) - 1)
    def _():
        for h in range(heads):
            cout_ref[0, h] = ct_s[h].T
        nout_ref[0] = n_s[...]
        mout_ref[0] = m_s[...]


def _mlstm_layer(x, cbuf, state, p, g0, g1, lc):
    b, s, d = x.shape
    heads = p['heads']
    inner = p['m_cw'].shape[1]
    width = p['m_cw'].shape[0]
    dh = inner // heads
    zero_init = state is None
    kern = functools.partial(_mlstm_layer_kernel, lc=lc, heads=heads, inner=inner, width=width,
                             zero_init=zero_init)
    xs = pl.BlockSpec((1, lc, d), lambda i, c: (i, c, 0))
    bufs = pl.BlockSpec((1, width - 1, inner), lambda i, c: (i, 0, 0))
    cs = pl.BlockSpec((1, heads, dh, dh), lambda i, c: (i, 0, 0, 0))
    ns = pl.BlockSpec((1, heads, dh), lambda i, c: (i, 0, 0))
    ms = pl.BlockSpec((1, heads, V7X_LANES), lambda i, c: (i, 0, 0))
    consts = [g0, p['m_wup'], p['m_cw'], p['m_cb'], p['m_wq'], p['m_wkt'], p['m_wv'], p['m_gc'], p['m_gm'],
              p['m_bg'], p['m_ng'], p['m_skip'], p['m_wdown'], g1]
    in_specs = [xs, bufs] + [_const_spec(a.shape) for a in consts]
    args = [x, cbuf] + consts
    if not zero_init:
        in_specs += [cs, ns, ms]
        args += list(state)
    blocks = 2 * _nbytes((lc, d), F32) + (1 if zero_init else 2) * _nbytes((heads, dh, dh), F32)
    scratch = (sum(_nbytes(a.shape, a.dtype) for a in consts) + 6 * _nbytes((lc, inner), BF16)
               + _nbytes((heads, dh, dh), F32) + 8 * _nbytes((lc, max(lc, dh)), F32) + 2 * _nbytes((dh, dh), F32))
    return pl.pallas_call(
        kern,
        grid=(b, s // lc),
        in_specs=in_specs,
        out_specs=[xs, cs, ns, ms, bufs],
        out_shape=[jax.ShapeDtypeStruct((b, s, d), F32), jax.ShapeDtypeStruct((b, heads, dh, dh), F32),
                   jax.ShapeDtypeStruct((b, heads, dh), F32), jax.ShapeDtypeStruct((b, heads, V7X_LANES), F32),
                   jax.ShapeDtypeStruct((b, width - 1, inner), F32)],
        scratch_shapes=[pltpu.VMEM((V7X_SUBLANES, inner), F32),
                        pltpu.VMEM((1, lc, inner), BF16), pltpu.VMEM((1, inner, lc), BF16),
                        pltpu.VMEM((1, lc, inner), BF16), pltpu.VMEM((lc, inner), BF16),
                        pltpu.VMEM((lc, inner), BF16), pltpu.VMEM((lc, inner), BF16),
                        pltpu.VMEM((heads, dh, dh), F32), pltpu.VMEM((heads, dh), F32),
                        pltpu.VMEM((heads, V7X_LANES), F32)],
        compiler_params=pltpu.CompilerParams(
            dimension_semantics=("arbitrary", "arbitrary"), vmem_limit_bytes=_vmem_limit(blocks, scratch)),
        name="mlstm_layer",
    )(*args)


def _prep_weights(w):
    row = lambda v: v.astype(F32).reshape(1, -1)
    p = {}
    p['norm_g'] = w['norm_g'].astype(F32)
    p['s5_mats'] = _s5_matrices(w['s5_A_re'][0], w['s5_A_im'][0], w['s5_log_dt'][0], w['s5_B_re'][0],
                                w['s5_B_im'][0], w['s5_C_re'][0], w['s5_C_im'][0])
    p['s5_D'] = row(w['s5_D'][0])
    p['s5_w_glu'] = w['s5_w_glu'][0].astype(BF16)
    inner = w['mlstm_conv_w'].shape[2]
    heads = w['mlstm_b_gate'].shape[1] // 2
    dh = inner // heads
    p['m_wq'], p['m_wkt'], p['m_wv'], p['m_gc'], p['m_gm'] = _mlstm_weights(
        w['mlstm_wq'][0], w['mlstm_wk'][0], w['mlstm_wv'][0], w['mlstm_w_gate'][0], dh ** -0.5)
    bg = row(w['mlstm_b_gate'][0])
    p['m_bg'] = jnp.pad(bg, ((0, 0), (0, V7X_LANES - bg.shape[1])))
    p['m_wup'] = w['mlstm_w_up'][0].astype(BF16)
    p['m_cw'] = w['mlstm_conv_w'][0].astype(F32)
    p['m_cb'] = row(w['mlstm_conv_b'][0])
    p['m_ng'] = row(w['mlstm_norm_g'][0])
    p['m_skip'] = row(w['mlstm_skip'][0])
    p['m_wdown'] = w['mlstm_w_down'][0].astype(BF16)
    p['heads'] = heads
    p['f_wup'] = w['ffn_w_up'].astype(BF16)
    p['f_cw'] = w['ffn_conv_w'].astype(F32)
    p['f_cb'] = w['ffn_conv_b'].astype(F32)[:, None, :]
    p['f_wdown'] = w['ffn_w_down'].astype(BF16)
    return p


def _to_chunk_major(u2d, b, s):
    d = u2d.shape[1]
    g = d // S5_GROUP_CH
    u = u2d.reshape(b, s // S5_CHUNK, S5_CHUNK, g, S5_GROUP_CH)[:, :, ::-1]
    return jnp.transpose(u, (1, 0, 3, 2, 4)).reshape((s // S5_CHUNK) * b, d * S5_CHUNK)


def _to_token_major(yt, b, s):
    d = yt.shape[1] // S5_CHUNK
    g = d // S5_GROUP_CH
    y = yt.reshape(s // S5_CHUNK, b, g, S5_CHUNK, S5_GROUP_CH)
    return jnp.transpose(y, (1, 0, 3, 2, 4)).reshape(b * s, d)


def _trunk(x, st, p, tiles):
    b, s, d = x.shape
    ng = p['norm_g']
    g_of = lambda layer, k: ng[layer, k].reshape(1, d)
    zero_init = st is None
    groups = d // S5_GROUP_CH
    x2d = x.reshape(b * s, d)

    n_state = p['s5_mats'][1].shape[1] // 2
    if zero_init:
        h0re = h0im = jnp.zeros((groups // 2, b, 2 * n_state), F32)
    else:
        pair = lambda h: jnp.transpose(h.astype(F32).reshape(b, groups // 2, 2 * n_state), (1, 0, 2))
        h0re, h0im = pair(st['s5_re']), pair(st['s5_im'])
    if tiles.s5_time:
        y, hre, him = _s5_fused(x, g_of(0, 0), p['s5_mats'], h0re, h0im, tiles.s5_time)
        y2d = y.reshape(b * s, d)
    else:
        u = _norm_cast(x2d, g_of(0, 0), tiles.rows)
        yt, hre, him = _s5_core(_to_chunk_major(u, b, s), p['s5_mats'], h0re, h0im, b, tiles.s5_rows)
        y2d = _to_token_major(yt, b, s)
    unpair = lambda h: jnp.transpose(h, (1, 0, 2)).reshape(1, b, groups, n_state)
    o_re, o_im = unpair(hre), unpair(him)
    x2d = _glu(x2d, y2d, g_of(0, 0), p['s5_D'], p['s5_w_glu'], g_of(0, 1), tiles.rows)

    def ffn(x2d, layer):
        dff2 = p['f_wup'].shape[2]
        width = p['f_cw'].shape[1]
        buf = jnp.zeros((b, width - 1, dff2), F32) if zero_init else st['ffn_conv'][layer].astype(F32)
        xo, nbuf = _ffn(x2d.reshape(b, s, d), buf, g_of(layer, 2), p['f_wup'], p['f_cw'], p['f_cb'],
                        p['f_wdown'], g_of(layer, 3), layer, tiles.ffn_time)
        return xo.reshape(b * s, d), nbuf

    x2d, fbuf0 = ffn(x2d, 0)

    heads = p['heads']
    inner = p['m_cw'].shape[1]
    width = p['m_cw'].shape[0]
    cbuf = jnp.zeros((b, width - 1, inner), F32) if zero_init else st['mlstm_conv'].astype(F32)
    if zero_init:
        state = None
    else:
        state = (st['mlstm_C'].astype(F32), st['mlstm_n'].astype(F32),
                 jnp.broadcast_to(st['mlstm_m'].astype(F32)[:, :, None], (b, heads, V7X_LANES)))
    x3, c_out, n_out, m_out, ncbuf = _mlstm_layer(x2d.reshape(b, s, d), cbuf, state, p, g_of(1, 0), g_of(1, 1),
                                                  tiles.mlstm_chunk)
    x2d, fbuf1 = ffn(x3.reshape(b * s, d), 1)

    return (x2d.reshape(b, s, d), o_re, o_im, c_out[None], n_out[None], m_out[None, :, :, 0], ncbuf[None],
            jnp.stack([fbuf0, fbuf1], axis=0))


def kernel(x_prompt, x_sample, state_s5_re, state_s5_im, state_mlstm_C, state_mlstm_n, state_mlstm_m, state_mlstm_conv, state_ffn_conv, norm_g, s5_A_re, s5_A_im, s5_log_dt, s5_B_re, s5_B_im, s5_C_re, s5_C_im, s5_D, s5_w_glu, mlstm_w_up, mlstm_conv_w, mlstm_conv_b, mlstm_wq, mlstm_wk, mlstm_wv, mlstm_w_gate, mlstm_b_gate, mlstm_norm_g, mlstm_skip, mlstm_w_down, ffn_w_up, ffn_conv_w, ffn_conv_b, ffn_w_down):
    w = {
        'norm_g': norm_g, 's5_A_re': s5_A_re, 's5_A_im': s5_A_im, 's5_log_dt': s5_log_dt,
        's5_B_re': s5_B_re, 's5_B_im': s5_B_im, 's5_C_re': s5_C_re, 's5_C_im': s5_C_im, 's5_D': s5_D,
        's5_w_glu': s5_w_glu, 'mlstm_w_up': mlstm_w_up, 'mlstm_conv_w': mlstm_conv_w,
        'mlstm_conv_b': mlstm_conv_b, 'mlstm_wq': mlstm_wq, 'mlstm_wk': mlstm_wk, 'mlstm_wv': mlstm_wv,
        'mlstm_w_gate': mlstm_w_gate, 'mlstm_b_gate': mlstm_b_gate, 'mlstm_norm_g': mlstm_norm_g,
        'mlstm_skip': mlstm_skip, 'mlstm_w_down': mlstm_w_down, 'ffn_w_up': ffn_w_up,
        'ffn_conv_w': ffn_conv_w, 'ffn_conv_b': ffn_conv_b, 'ffn_w_down': ffn_w_down,
    }
    assert norm_g.shape[0] == 2 and s5_A_re.shape[0] == 1 and mlstm_w_up.shape[0] == 1
    assert x_prompt.shape[0] == V7X_SUBLANES and x_sample.shape[0] == V7X_SUBLANES
    p = _prep_weights(w)
    out_p = _trunk(x_prompt.astype(F32), None, p, _pick_tiles(*x_prompt.shape[:2]))
    st = {'s5_re': state_s5_re[0], 's5_im': state_s5_im[0], 'mlstm_C': state_mlstm_C[0],
          'mlstm_n': state_mlstm_n[0], 'mlstm_m': state_mlstm_m[0], 'mlstm_conv': state_mlstm_conv[0],
          'ffn_conv': state_ffn_conv}
    out_s = _trunk(x_sample.astype(F32), st, p, _pick_tiles(*x_sample.shape[:2]))
    return (out_p[0], out_s[0]) + tuple(out_p[1:]) + tuple(out_s[1:])
```

```python
import functools
import math
from typing import NamedTuple

import jax
import jax.numpy as jnp
from jax import lax
from jax.experimental import pallas as pl
from jax.experimental.pallas import tpu as pltpu

F32 = jnp.float32
BF16 = jnp.bfloat16
HI = lax.Precision.HIGHEST

NORM_EPS = 1e-6
LN_EPS = 1e-5
S5_GROUP_CH = 16
S5_CHUNK = 16
QKV_BLOCK = 4
FFN_ROW_BLOCK = 512

V7X_LANES = 128
V7X_SUBLANES = 8
V7X_MXU = 256
V7X_VMEM_BYTES = 64 * 1024 * 1024


class Tiles(NamedTuple):
    rows: int
    time: int
    s5_rows: int
    mlstm_chunk: int
    s5_time: int
    ffn_time: int


def _pick_tiles(batch, seq):
    s5_time = (V7X_LANES // batch) * S5_CHUNK
    return Tiles(rows=min(batch * seq, 512), time=min(seq, 512),
                 s5_rows=min(batch * (seq // S5_CHUNK), 512), mlstm_chunk=min(seq, 256),
                 s5_time=s5_time if seq % s5_time == 0 else 0, ffn_time=min(seq, 1024))


def _vmem_limit(block_bytes, scratch_bytes):
    want = 2 * block_bytes + scratch_bytes + 16 * 1024 * 1024
    return int(min(want, V7X_VMEM_BYTES - 8 * 1024 * 1024))


def _nbytes(shape, dtype):
    return math.prod(shape) * jnp.dtype(dtype).itemsize


def _rms(x, g):
    return x * lax.rsqrt(jnp.mean(x * x, axis=-1, keepdims=True) + NORM_EPS) * g


def _dot(a, b):
    return jnp.dot(a, b, preferred_element_type=F32)


def _dot_nt(a, b):
    return lax.dot_general(a, b, (((1,), (1,)), ((), ())), preferred_element_type=F32)


def _silu(x):
    return (0.5 * x) * (1.0 + jnp.tanh(0.5 * x))


def _causal_dwconv(u, prev, cw_ref, cb_ref, cs):
    width = cw_ref.shape[0]
    acc = cb_ref[:, cs] + u * cw_ref[width - 1:width, cs]
    rows = lax.broadcasted_iota(jnp.int32, prev.shape, 0)
    for j in range(width - 1):
        d = width - 1 - j
        rolled = pltpu.roll(u, d, 0)
        head = jnp.where(rows < d, pltpu.roll(prev, d, 0), rolled[0:V7X_SUBLANES])
        shifted = jnp.concatenate([head, rolled[V7X_SUBLANES:]], axis=0)
        acc = acc + shifted * cw_ref[j:j + 1, cs]
    return acc


def _log_sigmoid(x):
    return jnp.minimum(x, 0.0) - jnp.log1p(jnp.exp(-jnp.abs(x)))


def _split3(x):
    hi = x.astype(BF16)
    r1 = x - hi.astype(F32)
    mid = r1.astype(BF16)
    lo = (r1 - mid.astype(F32)).astype(BF16)
    return hi, mid, lo


def _norm_cast_kernel(x_ref, g_ref, o_ref):
    o_ref[...] = _rms(x_ref[...], g_ref[...]).astype(o_ref.dtype)


def _norm_cast(x2d, g, tb):
    rows, d = x2d.shape
    return pl.pallas_call(
        _norm_cast_kernel,
        grid=(rows // tb,),
        in_specs=[pl.BlockSpec((tb, d), lambda i: (i, 0)), pl.BlockSpec((1, d), lambda i: (0, 0))],
        out_specs=pl.BlockSpec((tb, d), lambda i: (i, 0)),
        out_shape=jax.ShapeDtypeStruct((rows, d), BF16),
        name="s5_norm",
    )(x2d, g)


def _s5_kernel(u_ref, tt_ref, pt_ref, qt_ref, are_ref, aim_ref, h0re_ref, h0im_ref,
               y_ref, hre_ref, him_ref, pure_s, puim_s, hinre_s, hinim_s, stre_s, stim_s, *, nb, rb):
    r = pl.program_id(1)

    @pl.when(r == 0)
    def _():
        stre_s[...] = h0re_ref[0]
        stim_s[...] = h0im_ref[0]

    u = u_ref[...]
    w = V7X_MXU
    half = pt_ref.shape[1] // 2
    pu = [_dot_nt(u[:, i * w:(i + 1) * w], pt_ref[i]) for i in range(2)]
    pure_s[...] = jnp.concatenate([pu[0][:, :half], pu[1][:, :half]], axis=1)
    puim_s[...] = jnp.concatenate([pu[0][:, half:], pu[1][:, half:]], axis=1)
    ar = jnp.broadcast_to(are_ref[0], (nb, 2 * half))
    ai = jnp.broadcast_to(aim_ref[0], (nb, 2 * half))

    def step(i, carry):
        re, im = carry
        rows = pl.ds(pl.multiple_of(i * nb, nb), nb)
        hinre_s[rows, :] = re
        hinim_s[rows, :] = im
        return (ar * re - ai * im + pure_s[rows, :], ar * im + ai * re + puim_s[rows, :])

    re, im = lax.fori_loop(0, rb // nb, step, (stre_s[...], stim_s[...]))
    stre_s[...] = re
    stim_s[...] = im
    hre_ref[0] = re
    him_ref[0] = im

    hre = hinre_s[...]
    him = hinim_s[...]
    for i in range(2):
        hin = jnp.concatenate([hre[:, i * half:(i + 1) * half], him[:, i * half:(i + 1) * half]], axis=1)
        y_ref[:, i * w:(i + 1) * w] = (_dot_nt(u[:, i * w:(i + 1) * w], tt_ref[i])
                                       + _dot_nt(hin.astype(BF16), qt_ref[i])).astype(y_ref.dtype)


def _s5_core(ut, mats, h0re, h0im, nb, rb):
    rows, cols = ut.shape
    pairs = cols // (2 * V7X_MXU)
    tt, pt, qt, are, aim = mats
    nst = pt.shape[1]
    pair3 = lambda p, r: (p, 0, 0)
    kern = functools.partial(_s5_kernel, nb=nb, rb=rb)
    blocks = (_nbytes((rb, 512), BF16) * 2 + _nbytes((2, 256, 256), BF16) + 4 * _nbytes((256, 128), BF16))
    scratch = 4 * _nbytes((rb, 128), F32) + 2 * _nbytes((nb, 128), F32)
    return pl.pallas_call(
        kern,
        grid=(pairs, rows // rb),
        in_specs=[
            pl.BlockSpec((rb, 2 * V7X_MXU), lambda p, r: (r, p)),
            pl.BlockSpec((2, V7X_MXU, V7X_MXU), pair3),
            pl.BlockSpec((2, nst, V7X_MXU), pair3),
            pl.BlockSpec((2, V7X_MXU, nst), pair3),
            pl.BlockSpec((1, 1, 128), pair3),
            pl.BlockSpec((1, 1, 128), pair3),
            pl.BlockSpec((1, nb, 128), pair3),
            pl.BlockSpec((1, nb, 128), pair3),
        ],
        out_specs=[
            pl.BlockSpec((rb, 2 * V7X_MXU), lambda p, r: (r, p)),
            pl.BlockSpec((1, nb, 128), pair3),
            pl.BlockSpec((1, nb, 128), pair3),
        ],
        out_shape=[
            jax.ShapeDtypeStruct((rows, cols), BF16),
            jax.ShapeDtypeStruct((pairs, nb, 128), F32),
            jax.ShapeDtypeStruct((pairs, nb, 128), F32),
        ],
        scratch_shapes=[pltpu.VMEM((rb, 128), F32)] * 4 + [pltpu.VMEM((nb, 128), F32)] * 2,
        compiler_params=pltpu.CompilerParams(
            dimension_semantics=("arbitrary", "arbitrary"), vmem_limit_bytes=_vmem_limit(blocks, scratch)),
        name="s5_core",
    )(ut, tt, pt, qt, are, aim, h0re, h0im)


def _s5_fused_kernel(x_ref, g_ref, tt_ref, pt_ref, qt_ref, are_ref, aim_ref, h0re_ref, h0im_ref,
                     y_ref, hre_ref, him_ref,
                     us, ugt, pure, puim, hinre, hinim, ytmp, ys, stre, stim, *, nb, tbt):
    lc = S5_CHUNK
    ch = S5_GROUP_CH
    gps = V7X_LANES // ch
    nch = tbt // lc
    cols = nb * nch
    k = pl.program_id(1)

    @pl.when(k == 0)
    def _():
        for b in range(nb):
            u = _rms(x_ref[b], g_ref[...])
            for kk in range(us.shape[0]):
                us[kk, pl.ds(b, tbt, stride=nb), :] = u[:, kk * V7X_LANES:(kk + 1) * V7X_LANES]

    @pl.when(pl.program_id(0) == 0)
    def _():
        stre[k] = h0re_ref[...]
        stim[k] = h0im_ref[...]

    tile = lambda j, s: slice((j * lc + s) * nb, (j * lc + s + 1) * nb)
    for s in range(lc):
        a = jnp.concatenate([us[k, tile(j, s), :] for j in range(nch)], axis=0).T
        for g in range(gps):
            ugt[g, (lc - 1 - s) * ch:(lc - s) * ch, :] = a[g * ch:(g + 1) * ch, :].astype(BF16)

    half = pt_ref.shape[1] // 2
    for pr in range(gps // 2):
        put0 = _dot(pt_ref[2 * pr], ugt[2 * pr])
        put1 = _dot(pt_ref[2 * pr + 1], ugt[2 * pr + 1])
        pure[pr] = jnp.concatenate([put0[:half], put1[:half]], axis=0).T
        puim[pr] = jnp.concatenate([put0[half:], put1[half:]], axis=0).T

    for pr in range(gps // 2):
        re = stre[k, pr]
        im = stim[k, pr]
        ar = jnp.broadcast_to(are_ref[pr], re.shape)
        ai = jnp.broadcast_to(aim_ref[pr], re.shape)
        for j in range(nch):
            rows = slice(j * nb, (j + 1) * nb)
            hinre[pr, rows, :] = re
            hinim[pr, rows, :] = im
            re, im = (ar * re - ai * im + pure[pr, rows, :], ar * im + ai * re + puim[pr, rows, :])
        stre[k, pr] = re
        stim[k, pr] = im
        hre_ref[k * (gps // 2) + pr] = re
        him_ref[k * (gps // 2) + pr] = im

    for pr in range(gps // 2):
        hre_t = hinre[pr].T
        him_t = hinim[pr].T
        for i in range(2):
            g = 2 * pr + i
            hin_t = jnp.concatenate([hre_t[i * half:(i + 1) * half], him_t[i * half:(i + 1) * half]],
                                    axis=0).astype(BF16)
            ytmp[g] = _dot(tt_ref[g], ugt[g]) + _dot(qt_ref[g], hin_t)

    for t in range(lc):
        zt = jnp.concatenate([ytmp[g, t * ch:(t + 1) * ch, :] for g in range(gps)], axis=0).T
        for j in range(nch):
            ys[tile(j, t), :] = zt[j * nb:(j + 1) * nb, :]
    for b in range(nb):
        y_ref[b] = ys[pl.ds(b, tbt, stride=nb), :].astype(y_ref.dtype)


def _s5_fused(x, g, mats_t, h0re, h0im, tbt):
    nb, s, d = x.shape
    tt, pt, qt, are, aim = mats_t
    groups = tt.shape[0]
    gps = V7X_LANES // S5_GROUP_CH
    slabs = groups // gps
    w = S5_CHUNK * S5_GROUP_CH
    nst = pt.shape[1]
    cols = nb * (tbt // S5_CHUNK)
    kern = functools.partial(_s5_fused_kernel, nb=nb, tbt=tbt)
    slab3 = lambda t, k: (k, 0, 0)
    blocks = (_nbytes((nb, tbt, d), F32) + _nbytes((nb, tbt, V7X_LANES), BF16)
              + gps * (_nbytes((w, w), BF16) + 2 * _nbytes((w, nst), BF16)))
    scratch = (_nbytes((slabs, nb * tbt, V7X_LANES), F32) + _nbytes((gps, w, cols), BF16)
               + 4 * _nbytes((gps // 2, cols, nst), F32) + _nbytes((gps, w, cols), F32)
               + _nbytes((nb * tbt, V7X_LANES), F32))
    return pl.pallas_call(
        kern,
        grid=(s // tbt, slabs),
        in_specs=[
            pl.BlockSpec((nb, tbt, d), lambda t, k: (0, t, 0)),
            pl.BlockSpec((1, d), lambda t, k: (0, 0)),
            pl.BlockSpec((gps, w, w), slab3),
            pl.BlockSpec((gps, nst, w), slab3),
            pl.BlockSpec((gps, w, nst), slab3),
            pl.BlockSpec((gps // 2, 1, nst), slab3),
            pl.BlockSpec((gps // 2, 1, nst), slab3),
            pl.BlockSpec((gps // 2, nb, nst), slab3),
            pl.BlockSpec((gps // 2, nb, nst), slab3),
        ],
        out_specs=[
            pl.BlockSpec((nb, tbt, V7X_LANES), lambda t, k: (0, t, k)),
            pl.BlockSpec((groups // 2, nb, nst), lambda t, k: (0, 0, 0)),
            pl.BlockSpec((groups // 2, nb, nst), lambda t, k: (0, 0, 0)),
        ],
        out_shape=[
            jax.ShapeDtypeStruct((nb, s, d), BF16),
            jax.ShapeDtypeStruct((groups // 2, nb, nst), F32),
            jax.ShapeDtypeStruct((groups // 2, nb, nst), F32),
        ],
        scratch_shapes=[
            pltpu.VMEM((slabs, nb * tbt, V7X_LANES), F32),
            pltpu.VMEM((gps, w, cols), BF16),
            pltpu.VMEM((gps // 2, cols, nst), F32),
            pltpu.VMEM((gps // 2, cols, nst), F32),
            pltpu.VMEM((gps // 2, cols, nst), F32),
            pltpu.VMEM((gps // 2, cols, nst), F32),
            pltpu.VMEM((gps, w, cols), F32),
            pltpu.VMEM((nb * tbt, V7X_LANES), F32),
            pltpu.VMEM((slabs, gps // 2, nb, nst), F32),
            pltpu.VMEM((slabs, gps // 2, nb, nst), F32),
        ],
        compiler_params=pltpu.CompilerParams(
            dimension_semantics=("arbitrary", "arbitrary"), vmem_limit_bytes=_vmem_limit(blocks, scratch)),
        name="s5_fused",
    )(x, g, tt, pt, qt, are, aim, h0re, h0im)


def _s5_matrices(a_re, a_im, log_dt, b_re, b_im, c_re, c_im):
    lc = S5_CHUNK
    groups, n = a_re.shape
    ch = S5_GROUP_CH
    f = lambda v: v.astype(F32)
    a_re, a_im, b_re, b_im, c_re, c_im = map(f, (a_re, a_im, b_re, b_im, c_re, c_im))
    dt = jnp.exp(f(log_dt))[:, None, None]
    j = jnp.arange(lc + 1, dtype=F32)
    mag = jnp.exp((a_re[:, :, None] * dt) * j)
    ang = (a_im[:, :, None] * dt) * j
    pw_re, pw_im = mag * jnp.cos(ang), mag * jnp.sin(ang)
    e_re, e_im = pw_re[:, :, 1] - 1.0, pw_im[:, :, 1]
    inv = 1.0 / (a_re * a_re + a_im * a_im)
    f_re, f_im = (e_re * a_re + e_im * a_im) * inv, (e_im * a_re - e_re * a_im) * inv
    bb_re = f_re[..., None] * b_re - f_im[..., None] * b_im
    bb_im = f_re[..., None] * b_im + f_im[..., None] * b_re
    rep = lambda v: jnp.repeat(v, ch, axis=2)
    til = lambda v: jnp.tile(v, (1, 1, lc + 1))
    lb_re = rep(pw_re) * til(bb_re) - rep(pw_im) * til(bb_im)
    lb_im = rep(pw_re) * til(bb_im) + rep(pw_im) * til(bb_re)
    kern = jnp.einsum('gcm,gmx->gcx', jnp.concatenate([c_re, c_im], axis=2),
                      jnp.concatenate([lb_re, -lb_im], axis=1), precision=HI)
    w = lc * ch
    kfwd = jnp.pad(kern[:, :, :w], ((0, 0), (0, 0), ((lc - 1) * ch, 0)))
    tt = jnp.stack([kfwd[:, :, t * ch:t * ch + w] for t in range(lc)], axis=1).reshape(groups, w, w)
    pt = jnp.concatenate([lb_re[:, :, :w], lb_im[:, :, :w]], axis=1)
    pwt_re = jnp.swapaxes(pw_re, 1, 2)[:, 1:, None, :]
    pwt_im = jnp.swapaxes(pw_im, 1, 2)[:, 1:, None, :]
    ca_re = c_re[:, None] * pwt_re - c_im[:, None] * pwt_im
    ca_im = c_re[:, None] * pwt_im + c_im[:, None] * pwt_re
    qt = jnp.concatenate([ca_re, -ca_im], axis=3).reshape(groups, w, 2 * n)
    are = pw_re[:, :, lc].reshape(groups // 2, 1, 2 * n)
    aim = pw_im[:, :, lc].reshape(groups // 2, 1, 2 * n)
    return tt.astype(BF16), pt.astype(BF16), qt.astype(BF16), are, aim


def _glu_kernel(x_ref, y_ref, g0_ref, d_ref, w_ref, g1_ref, o_ref, out_s):
    x = x_ref[...]
    d = x.shape[1]
    u = _rms(x, g0_ref[...])
    yy = y_ref[...].astype(F32) + d_ref[...] * u
    z = jax.nn.gelu(yy, approximate=True).astype(BF16)
    w = V7X_MXU
    nslab = d // w
    pair = lambda j: (_dot(z, w_ref[:, j * w:(j + 1) * w]), _dot(z, w_ref[:, d + j * w:d + (j + 1) * w]))
    nxt = pair(0)
    ssq = jnp.zeros((x.shape[0], 1), F32)
    for j in range(nslab):
        a, b = nxt
        if j + 1 < nslab:
            nxt = pair(j + 1)
        out = a * (0.5 * (1.0 + jnp.tanh(0.5 * b)))
        out_s[:, j * w:(j + 1) * w] = out
        ssq = ssq + jnp.sum(out * out, axis=-1, keepdims=True)
    scale = lax.rsqrt(ssq * (1.0 / d) + NORM_EPS)
    o_ref[...] = x + out_s[...] * scale * g1_ref[...]


def _const_spec(shape):
    nd = len(shape)
    return pl.BlockSpec(shape, lambda *_: (0,) * nd, pipeline_mode=pl.Buffered(1))


def _glu(x2d, y2d, g0, dskip, w, g1, tb):
    rows, d = x2d.shape
    blocks = 2 * _nbytes((tb, d), F32) + _nbytes((tb, d), BF16)
    scratch = _nbytes(w.shape, BF16) + 3 * _nbytes((tb, 2 * d), F32)
    row = pl.BlockSpec((tb, d), lambda i: (i, 0))
    return pl.pallas_call(
        _glu_kernel,
        grid=(rows // tb,),
        in_specs=[row, row, _const_spec((1, d)), _const_spec((1, d)), _const_spec(w.shape), _const_spec((1, d))],
        out_specs=row,
        out_shape=jax.ShapeDtypeStruct((rows, d), F32),
        scratch_shapes=[pltpu.VMEM((tb, d), F32)],
        compiler_params=pltpu.CompilerParams(
            dimension_semantics=("arbitrary",), vmem_limit_bytes=_vmem_limit(blocks, scratch)),
        name="s5_glu",
    )(x2d, y2d, g0, dskip, w, g1)


def _ffn_kernel(x_ref, buf_ref, g2_ref, wup_ref, cw_ref, cb_ref, wdown_ref, g3_ref,
                o_ref, nbuf_ref, carry, act, *, tb, sub, dff, width):
    t = pl.program_id(1)
    pad = V7X_SUBLANES

    @pl.when(t == 0)
    def _():
        carry[...] = jnp.zeros_like(carry)
        carry[pad - (width - 1):pad, :] = buf_ref[0]

    fb = V7X_MXU
    nslab = dff // fb
    cols = lambda j: (slice(j * fb, (j + 1) * fb), slice(dff + j * fb, dff + (j + 1) * fb))

    def conv(u, cs):
        prev = carry[:, cs]
        carry[:, cs] = u[sub - pad:sub, :]
        return _causal_dwconv(u, prev, cw_ref, cb_ref, cs)

    def up_phase(r):
        hn = _rms(x_ref[0, r * sub:(r + 1) * sub, :], g2_ref[...]).astype(BF16)
        up = lambda j: tuple(_dot(hn, wup_ref[:, cs]) for cs in cols(j))
        nxt = up(0)
        for j in range(nslab):
            ug, uv = nxt
            if j + 1 < nslab:
                nxt = up(j + 1)
            gate = conv(ug, cols(j)[0])
            val = conv(uv, cols(j)[1])
            act[r % 2, :, j * fb:(j + 1) * fb] = (jax.nn.gelu(gate, approximate=True) * val).astype(BF16)

    nsub = tb // sub
    up_phase(0)
    for r in range(nsub):
        f = _dot(act[r % 2], wdown_ref[...])
        if r + 1 < nsub:
            up_phase(r + 1)
        rows = slice(r * sub, (r + 1) * sub)
        o_ref[0, rows, :] = x_ref[0, rows, :] + _rms(f, g3_ref[...])
    nbuf_ref[0] = carry[pad - (width - 1):pad, :]


def _layer_spec(shape, layer):
    nd = len(shape) - 1
    return pl.BlockSpec((None,) + tuple(shape[1:]), lambda *_: (layer,) + (0,) * nd, pipeline_mode=pl.Buffered(1))


def _ffn(x, buf, g2, wup, cw, cb, wdown, g3, layer, tb):
    b, s, d = x.shape
    dff = wdown.shape[1]
    width = cw.shape[1]
    sub = min(tb, FFN_ROW_BLOCK)
    kern = functools.partial(_ffn_kernel, tb=tb, sub=sub, dff=dff, width=width)
    blocks = 2 * _nbytes((tb, d), F32)
    scratch = (_nbytes(wup.shape[1:], BF16) + _nbytes(wdown.shape[1:], BF16) + 6 * _nbytes((sub, V7X_MXU), F32)
               + _nbytes((8, 2 * dff), F32) + 2 * _nbytes((sub, dff), BF16) + 3 * _nbytes((sub, d), F32))
    xs = pl.BlockSpec((1, tb, d), lambda i, t: (i, t, 0))
    bs = pl.BlockSpec((1, width - 1, 2 * dff), lambda i, t: (i, 0, 0))
    return pl.pallas_call(
        kern,
        grid=(b, s // tb),
        in_specs=[xs, bs, _const_spec((1, d)), _layer_spec(wup.shape, layer), _layer_spec(cw.shape, layer),
                  _layer_spec(cb.shape, layer), _layer_spec(wdown.shape, layer), _const_spec((1, d))],
        out_specs=[xs, bs],
        out_shape=[jax.ShapeDtypeStruct((b, s, d), F32), jax.ShapeDtypeStruct((b, width - 1, 2 * dff), F32)],
        scratch_shapes=[pltpu.VMEM((V7X_SUBLANES, 2 * dff), F32), pltpu.VMEM((2, sub, dff), BF16)],
        compiler_params=pltpu.CompilerParams(
            dimension_semantics=("arbitrary", "arbitrary"), vmem_limit_bytes=_vmem_limit(blocks, scratch)),
        name="conv_ffn",
    )(x, buf, g2, wup, cw, cb, wdown, g3)


def _mlstm_weights_kernel(aq_ref, ak_ref, av_ref, gq_ref, gk_ref, gv_ref,
                          wq_ref, wkt_ref, wv_ref, gc_ref, gm_ref, *, kscale):
    w = V7X_MXU
    kb = QKV_BLOCK
    dotp = functools.partial(jnp.dot, preferred_element_type=F32, precision=HI)
    lane = lax.broadcasted_iota(jnp.int32, (V7X_LANES, w), 1)
    src = lax.broadcasted_iota(jnp.int32, (V7X_LANES, w), 0)
    spread = jnp.where(jnp.bitwise_and(lane, kb - 1) == src, 1.0, 0.0)
    shift = kb.bit_length() - 1
    rblk = lax.shift_right_logical(lax.broadcasted_iota(jnp.int32, (w, w), 0), shift)
    cblk = lax.shift_right_logical(lax.broadcasted_iota(jnp.int32, (w, w), 1), shift)
    tile = lambda a_ref: jnp.where(rblk == cblk, dotp(a_ref[0], spread), 0.0)
    tq, tk, tv = tile(aq_ref), tile(ak_ref), tile(av_ref)
    wq_ref[0] = tq.astype(BF16)
    wkt_ref[0] = (tk.T * kscale).astype(BF16)
    wv_ref[0] = tv.astype(BF16)
    gc_ref[...] = (dotp(tq, gq_ref[...]) + dotp(tk, gk_ref[...])).astype(BF16)
    gm_ref[...] = dotp(tv, gv_ref[...]).astype(BF16)


def _mlstm_weights(wq, wk, wv, w_gate, kscale):
    nblk, kb, _ = wq.shape
    assert kb == QKV_BLOCK and kb & (kb - 1) == 0
    w = V7X_MXU
    per = w // kb
    nt = nblk // per
    inner = nt * w
    compact = lambda m: jnp.pad(m.astype(F32).reshape(nt, w, kb), ((0, 0), (0, 0), (0, V7X_LANES - kb)))
    wg = jnp.pad(w_gate.astype(F32), ((0, 0), (0, V7X_LANES - w_gate.shape[1])))
    cm = pl.BlockSpec((1, w, V7X_LANES), lambda i: (i, 0, 0))
    gs = [pl.BlockSpec((w, V7X_LANES), lambda i, k=k: (k * nt + i, 0)) for k in range(3)]
    tile = pl.BlockSpec((1, w, w), lambda i: (i, 0, 0))
    fold = pl.BlockSpec((w, V7X_LANES), lambda i: (i, 0))
    return pl.pallas_call(
        functools.partial(_mlstm_weights_kernel, kscale=kscale),
        grid=(nt,),
        in_specs=[cm, cm, cm] + gs,
        out_specs=[tile, tile, tile, fold, fold],
        out_shape=[jax.ShapeDtypeStruct((nt, w, w), BF16)] * 3 + [jax.ShapeDtypeStruct((inner, V7X_LANES), BF16)] * 2,
        name="mlstm_weights",
    )(compact(wq), compact(wk), compact(wv), wg, wg, wg)


def _mpre_kernel(x_ref, cbuf_ref, g_ref, wup_ref, cw_ref, cb_ref, wq_ref, wkt_ref, wv_ref, gc_ref, gm_ref,
                 bg_ref, q_ref, kt_ref, v_ref, xc_ref, z_ref, gates_ref, nbuf_ref, carry,
                 *, tb, inner, width):
    t = pl.program_id(1)
    pad = V7X_SUBLANES

    @pl.when(t == 0)
    def _():
        carry[...] = jnp.zeros_like(carry)
        carry[pad - (width - 1):pad, :] = cbuf_ref[0]

    h = _rms(x_ref[0], g_ref[...]).astype(BF16)
    w = V7X_MXU
    gacc = jnp.zeros((tb, bg_ref.shape[1]), F32) + bg_ref[...]
    for i in range(inner // w):
        cs = slice(i * w, (i + 1) * w)
        xm = _dot(h, wup_ref[:, cs])
        prev = carry[:, cs]
        carry[:, cs] = xm[tb - pad:tb, :]
        xc = _silu(_causal_dwconv(xm, prev, cw_ref, cb_ref, cs)).astype(BF16)
        xmb = xm.astype(BF16)
        q_ref[0, :, cs] = _dot(xc, wq_ref[i]).astype(BF16)
        kt_ref[0, cs, :] = _dot_nt(wkt_ref[i], xc).astype(BF16)
        v_ref[0, :, cs] = _dot(xmb, wv_ref[i]).astype(BF16)
        xc_ref[0, :, cs] = xc
        z_ref[0, :, cs] = _silu(_dot(h, wup_ref[:, inner + i * w:inner + (i + 1) * w])).astype(BF16)
        gacc = gacc + _dot(xc, gc_ref[cs, :]) + _dot(xmb, gm_ref[cs, :])
    gates_ref[0] = gacc
    nbuf_ref[0] = carry[pad - (width - 1):pad, :]


def _mpre(x, cbuf, g, wup, cw, cb, wq_t, wkt_t, wv_t, gc, gm, bg, tb):
    b, s, d = x.shape
    inner = cw.shape[1]
    width = cw.shape[0]
    ng = bg.shape[1]
    kern = functools.partial(_mpre_kernel, tb=tb, inner=inner, width=width)
    act = pl.BlockSpec((1, tb, inner), lambda i, t: (i, t, 0))
    blocks = _nbytes((tb, d), F32) + 5 * _nbytes((tb, inner), BF16)
    scratch = (_nbytes(wup.shape, BF16) + 3 * _nbytes(wq_t.shape, BF16) + 2 * _nbytes((inner, 128), BF16)
               + _nbytes((8, inner), F32) + 8 * _nbytes((tb, V7X_MXU), F32))
    return pl.pallas_call(
        kern,
        grid=(b, s // tb),
        in_specs=[pl.BlockSpec((1, tb, d), lambda i, t: (i, t, 0)),
                  pl.BlockSpec((1, width - 1, inner), lambda i, t: (i, 0, 0)),
                  _const_spec((1, d)), _const_spec(wup.shape), _const_spec(cw.shape), _const_spec((1, inner)),
                  _const_spec(wq_t.shape), _const_spec(wkt_t.shape), _const_spec(wv_t.shape),
                  _const_spec(gc.shape), _const_spec(gm.shape), _const_spec((1, ng))],
        out_specs=[act, pl.BlockSpec((1, inner, tb), lambda i, t: (i, 0, t)), act, act, act,
                   pl.BlockSpec((1, tb, ng), lambda i, t: (i, t, 0)),
                   pl.BlockSpec((1, width - 1, inner), lambda i, t: (i, 0, 0))],
        out_shape=[jax.ShapeDtypeStruct((b, s, inner), BF16), jax.ShapeDtypeStruct((b, inner, s), BF16),
                   jax.ShapeDtypeStruct((b, s, inner), BF16), jax.ShapeDtypeStruct((b, s, inner), BF16),
                   jax.ShapeDtypeStruct((b, s, inner), BF16), jax.ShapeDtypeStruct((b, s, ng), F32),
                   jax.ShapeDtypeStruct((b, width - 1, inner), F32)],
        scratch_shapes=[pltpu.VMEM((V7X_SUBLANES, inner), F32)],
        compiler_params=pltpu.CompilerParams(
            dimension_semantics=("arbitrary", "arbitrary"), vmem_limit_bytes=_vmem_limit(blocks, scratch)),
        name="mlstm_pre",
    )(x, cbuf, g, wup, cw, cb, wq_t, wkt_t, wv_t, gc, gm, bg)


def _mlstm_chunk(q_ref, kt_ref, v_ref, g, gt, ct_s, n_s, m_s, fill, emit_hn, fill_update, *, lc, heads, dh):
    row = lax.broadcasted_iota(jnp.int32, (lc, lc), 0)
    col = lax.broadcasted_iota(jnp.int32, (lc, lc), 1)
    causal = col <= row
    tri_l = jnp.where(causal, 1.0, 0.0).astype(BF16)
    tri_u = jnp.where(row <= col, 1.0, 0.0).astype(BF16)
    b_cols = sum(_dot(tri_l, part) for part in _split3(_log_sigmoid(g)))
    b_rows = sum(_dot(part, tri_u) for part in _split3(_log_sigmoid(gt)))

    hsl = [slice(h * dh, (h + 1) * dh) for h in range(heads)]
    st = []
    for h in range(heads):
        b_col = b_cols[:, heads + h:heads + h + 1]
        b_row = b_rows[heads + h:heads + h + 1, :]
        i_row = gt[h:h + 1, :]
        m_prev = m_s[h:h + 1, 0:1]
        dmat = jnp.where(causal, b_col - b_row + i_row, -jnp.inf)
        inter = b_col + m_prev
        m_t = jnp.maximum(inter, jnp.max(dmat, axis=1, keepdims=True))
        st.append(dict(b_col=b_col, b_row=b_row, i_row=i_row, m_prev=m_prev, m_t=m_t,
                       wts=jnp.exp(dmat - m_t), a=jnp.exp(inter - m_t)))
        fill()
    for h in range(heads):
        s = st[h]
        s['sc'] = _dot(q_ref[0, :, hsl[h]], kt_ref[0, hsl[h], :]) * s['wts']
    for h in range(heads):
        fill()
        s = st[h]
        qh = q_ref[0, :, hsl[h]]
        sc = s['sc']
        a = s['a']
        num = _dot(sc.astype(BF16), v_ref[0, :, hsl[h]]) + a * _dot(qh, ct_s[h].astype(BF16))
        qn = jnp.sum(qh.astype(F32) * n_s[h:h + 1, :], axis=1, keepdims=True)
        den = jnp.sum(sc, axis=1, keepdims=True) + a * qn
        hh = num / jnp.maximum(jnp.abs(den), jnp.exp(-s['m_t']))
        mu = jnp.mean(hh, axis=1, keepdims=True)
        dev = hh - mu
        var = jnp.mean(dev * dev, axis=1, keepdims=True)
        emit_hn(h, dev * lax.rsqrt(var + LN_EPS))
    for h in range(heads):
        fill_update(h)
        s = st[h]
        kth = kt_ref[0, hsl[h], :]
        b_last = s['b_row'][:, lc - 1:lc]
        g_row = b_last - s['b_row'] + s['i_row']
        m_new = jnp.maximum(b_last + s['m_prev'], jnp.max(g_row, axis=1, keepdims=True))
        decay = jnp.exp(b_last + s['m_prev'] - m_new)
        wg_row = jnp.exp(g_row - m_new)
        wg_col = jnp.exp(b_last - s['b_col'] + g[:, h:h + 1] - m_new)
        wv = (v_ref[0, :, hsl[h]].astype(F32) * wg_col).astype(BF16)
        ct_s[h] = decay * ct_s[h] + _dot(kth, wv)
        wg16 = jnp.broadcast_to(wg_row, (16, lc)).astype(BF16)
        n_s[h:h + 1, :] = decay * n_s[h:h + 1, :] + _dot_nt(wg16, kth)[0:1, :]
        m_s[h:h + 1, :] = jnp.broadcast_to(m_new, (1, V7X_LANES))


def _mlstm_layer_kernel(*refs, lc, heads, inner, width, zero_init):
    (x_ref, cbuf_ref, g0_ref, wup_ref, cw_ref, cb_ref, wq_ref, wkt_ref, wv_ref, gc_ref, gm_ref, bg_ref,
     ng_ref, sk_ref, wdown_ref, g1_ref) = refs[:16]
    n_in = 16
    if not zero_init:
        c0_ref, n0_ref, m0_ref = refs[16:19]
        n_in = 19
    o_ref, cout_ref, nout_ref, mout_ref, nbuf_ref = refs[n_in:n_in + 5]
    carry, q_s, kt_s, v_s, xc_s, sz_s, act_s, ct_s, n_s, m_s = refs[n_in + 5:]
    dh = inner // heads
    pad = V7X_SUBLANES
    c = pl.program_id(1)

    @pl.when(c == 0)
    def _():
        carry[...] = jnp.zeros_like(carry)
        carry[pad - (width - 1):pad, :] = cbuf_ref[0]
        if zero_init:
            ct_s[...] = jnp.zeros_like(ct_s)
            n_s[...] = jnp.zeros_like(n_s)
            m_s[...] = jnp.zeros_like(m_s)
        else:
            for h in range(heads):
                ct_s[h] = c0_ref[0, h].T
            n_s[...] = n0_ref[0]
            m_s[...] = m0_ref[0]

    x = x_ref[0]
    hx = _rms(x, g0_ref[...]).astype(BF16)
    w = V7X_MXU
    gacc = jnp.zeros((lc, bg_ref.shape[1]), F32) + bg_ref[...]
    nslab = inner // w
    col = lambda i: slice(i * w, (i + 1) * w)

    def project(i, xc, xmb, gacc):
        q_s[0, :, col(i)] = _dot(xc, wq_ref[i]).astype(BF16)
        kt_s[0, col(i), :] = _dot_nt(wkt_ref[i], xc).astype(BF16)
        v_s[0, :, col(i)] = _dot(xmb, wv_ref[i]).astype(BF16)
        xc_s[:, col(i)] = xc
        return gacc + _dot(xc, gc_ref[col(i), :]) + _dot(xmb, gm_ref[col(i), :])

    xm_next = _dot(hx, wup_ref[:, col(0)])
    pending = None
    for i in range(nslab):
        xm = xm_next
        if i + 1 < nslab:
            xm_next = _dot(hx, wup_ref[:, col(i + 1)])
        if pending is not None:
            gacc = project(*pending, gacc)
        prev = carry[:, col(i)]
        carry[:, col(i)] = xm[lc - pad:lc, :]
        xc = _silu(_causal_dwconv(xm, prev, cw_ref, cb_ref, col(i))).astype(BF16)
        pending = (i, xc, xm.astype(BF16))
    gacc = project(*pending, gacc)
    nbuf_ref[0] = carry[pad - (width - 1):pad, :]

    slabs = iter(range(inner // w))

    def gate_slab():
        i = next(slabs, None)
        if i is not None:
            cs = slice(i * w, (i + 1) * w)
            sz_s[:, cs] = _silu(_dot(hx, wup_ref[:, inner + i * w:inner + (i + 1) * w])).astype(BF16)

    hcols = lambda h: slice(h * dh, (h + 1) * dh)
    down = []

    def gated_out(h, hn):
        hs = hn * ng_ref[:, hcols(h)] + sk_ref[:, hcols(h)] * xc_s[:, hcols(h)].astype(F32)
        act_s[:, hcols(h)] = (hs * sz_s[:, hcols(h)].astype(F32)).astype(BF16)

    def down_proj(h):
        if h == 0:
            down.append(_dot(act_s[...], wdown_ref[...]))

    _mlstm_chunk(q_s, kt_s, v_s, gacc, gacc.T[0:2 * heads, :], ct_s, n_s, m_s, gate_slab, gated_out, down_proj,
                 lc=lc, heads=heads, dh=dh)
    for _ in slabs:
        raise AssertionError("fewer fill points than output-gate slabs")
    o_ref[0] = x + _rms(down[0], g1_ref[...])

    @pl.when(c == pl.num_programs(1) - 1)
    def _():
        for h in range(heads):
            cout_ref[0, h] = ct_s[h].T
        nout_ref[0] = n_s[...]
        mout_ref[0] = m_s[...]


def _mlstm_layer(x, cbuf, state, p, g0, g1, lc):
    b, s, d = x.shape
    heads = p['heads']
    inner = p['m_cw'].shape[1]
    width = p['m_cw'].shape[0]
    dh = inner // heads
    zero_init = state is None
    kern = functools.partial(_mlstm_layer_kernel, lc=lc, heads=heads, inner=inner, width=width,
                             zero_init=zero_init)
    xs = pl.BlockSpec((1, lc, d), lambda i, c: (i, c, 0))
    bufs = pl.BlockSpec((1, width - 1, inner), lambda i, c: (i, 0, 0))
    cs = pl.BlockSpec((1, heads, dh, dh), lambda i, c: (i, 0, 0, 0))
    ns = pl.BlockSpec((1, heads, dh), lambda i, c: (i, 0, 0))
    ms = pl.BlockSpec((1, heads, V7X_LANES), lambda i, c: (i, 0, 0))
    consts = [g0, p['m_wup'], p['m_cw'], p['m_cb'], p['m_wq'], p['m_wkt'], p['m_wv'], p['m_gc'], p['m_gm'],
              p['m_bg'], p['m_ng'], p['m_skip'], p['m_wdown'], g1]
    in_specs = [xs, bufs] + [_const_spec(a.shape) for a in consts]
    args = [x, cbuf] + consts
    if not zero_init:
        in_specs += [cs, ns, ms]
        args += list(state)
    blocks = 2 * _nbytes((lc, d), F32) + (1 if zero_init else 2) * _nbytes((heads, dh, dh), F32)
    scratch = (sum(_nbytes(a.shape, a.dtype) for a in consts) + 6 * _nbytes((lc, inner), BF16)
               + _nbytes((heads, dh, dh), F32) + 8 * _nbytes((lc, max(lc, dh)), F32) + 2 * _nbytes((dh, dh), F32))
    return pl.pallas_call(
        kern,
        grid=(b, s // lc),
        in_specs=in_specs,
        out_specs=[xs, cs, ns, ms, bufs],
        out_shape=[jax.ShapeDtypeStruct((b, s, d), F32), jax.ShapeDtypeStruct((b, heads, dh, dh), F32),
                   jax.ShapeDtypeStruct((b, heads, dh), F32), jax.ShapeDtypeStruct((b, heads, V7X_LANES), F32),
                   jax.ShapeDtypeStruct((b, width - 1, inner), F32)],
        scratch_shapes=[pltpu.VMEM((V7X_SUBLANES, inner), F32),
                        pltpu.VMEM((1, lc, inner), BF16), pltpu.VMEM((1, inner, lc), BF16),
                        pltpu.VMEM((1, lc, inner), BF16), pltpu.VMEM((lc, inner), BF16),
                        pltpu.VMEM((lc, inner), BF16), pltpu.VMEM((lc, inner), BF16),
                        pltpu.VMEM((heads, dh, dh), F32), pltpu.VMEM((heads, dh), F32),
                        pltpu.VMEM((heads, V7X_LANES), F32)],
        compiler_params=pltpu.CompilerParams(
            dimension_semantics=("arbitrary", "arbitrary"), vmem_limit_bytes=_vmem_limit(blocks, scratch)),
        name="mlstm_layer",
    )(*args)


def _prep_weights(w):
    row = lambda v: v.astype(F32).reshape(1, -1)
    p = {}
    p['norm_g'] = w['norm_g'].astype(F32)
    p['s5_mats'] = _s5_matrices(w['s5_A_re'][0], w['s5_A_im'][0], w['s5_log_dt'][0], w['s5_B_re'][0],
                                w['s5_B_im'][0], w['s5_C_re'][0], w['s5_C_im'][0])
    p['s5_D'] = row(w['s5_D'][0])
    p['s5_w_glu'] = w['s5_w_glu'][0].astype(BF16)
    inner = w['mlstm_conv_w'].shape[2]
    heads = w['mlstm_b_gate'].shape[1] // 2
    dh = inner // heads
    p['m_wq'], p['m_wkt'], p['m_wv'], p['m_gc'], p['m_gm'] = _mlstm_weights(
        w['mlstm_wq'][0], w['mlstm_wk'][0], w['mlstm_wv'][0], w['mlstm_w_gate'][0], dh ** -0.5)
    bg = row(w['mlstm_b_gate'][0])
    p['m_bg'] = jnp.pad(bg, ((0, 0), (0, V7X_LANES - bg.shape[1])))
    p['m_wup'] = w['mlstm_w_up'][0].astype(BF16)
    p['m_cw'] = w['mlstm_conv_w'][0].astype(F32)
    p['m_cb'] = row(w['mlstm_conv_b'][0])
    p['m_ng'] = row(w['mlstm_norm_g'][0])
    p['m_skip'] = row(w['mlstm_skip'][0])
    p['m_wdown'] = w['mlstm_w_down'][0].astype(BF16)
    p['heads'] = heads
    p['f_wup'] = w['ffn_w_up'].astype(BF16)
    p['f_cw'] = w['ffn_conv_w'].astype(F32)
    p['f_cb'] = w['ffn_conv_b'].astype(F32)[:, None, :]
    p['f_wdown'] = w['ffn_w_down'].astype(BF16)
    return p


def _to_chunk_major(u2d, b, s):
    d = u2d.shape[1]
    g = d // S5_GROUP_CH
    u = u2d.reshape(b, s // S5_CHUNK, S5_CHUNK, g, S5_GROUP_CH)[:, :, ::-1]
    return jnp.transpose(u, (1, 0, 3, 2, 4)).reshape((s // S5_CHUNK) * b, d * S5_CHUNK)


def _to_token_major(yt, b, s):
    d = yt.shape[1] // S5_CHUNK
    g = d // S5_GROUP_CH
    y = yt.reshape(s // S5_CHUNK, b, g, S5_CHUNK, S5_GROUP_CH)
    return jnp.transpose(y, (1, 0, 3, 2, 4)).reshape(b * s, d)


def _trunk(x, st, p, tiles):
    b, s, d = x.shape
    ng = p['norm_g']
    g_of = lambda layer, k: ng[layer, k].reshape(1, d)
    zero_init = st is None
    groups = d // S5_GROUP_CH
    x2d = x.reshape(b * s, d)

    n_state = p['s5_mats'][1].shape[1] // 2
    if zero_init:
        h0re = h0im = jnp.zeros((groups // 2, b, 2 * n_state), F32)
    else:
        pair = lambda h: jnp.transpose(h.astype(F32).reshape(b, groups // 2, 2 * n_state), (1, 0, 2))
        h0re, h0im = pair(st['s5_re']), pair(st['s5_im'])
    if tiles.s5_time:
        y, hre, him = _s5_fused(x, g_of(0, 0), p['s5_mats'], h0re, h0im, tiles.s5_time)
        y2d = y.reshape(b * s, d)
    else:
        u = _norm_cast(x2d, g_of(0, 0), tiles.rows)
        yt, hre, him = _s5_core(_to_chunk_major(u, b, s), p['s5_mats'], h0re, h0im, b, tiles.s5_rows)
        y2d = _to_token_major(yt, b, s)
    unpair = lambda h: jnp.transpose(h, (1, 0, 2)).reshape(1, b, groups, n_state)
    o_re, o_im = unpair(hre), unpair(him)
    x2d = _glu(x2d, y2d, g_of(0, 0), p['s5_D'], p['s5_w_glu'], g_of(0, 1), tiles.rows)

    def ffn(x2d, layer):
        dff2 = p['f_wup'].shape[2]
        width = p['f_cw'].shape[1]
        buf = jnp.zeros((b, width - 1, dff2), F32) if zero_init else st['ffn_conv'][layer].astype(F32)
        xo, nbuf = _ffn(x2d.reshape(b, s, d), buf, g_of(layer, 2), p['f_wup'], p['f_cw'], p['f_cb'],
                        p['f_wdown'], g_of(layer, 3), layer, tiles.ffn_time)
        return xo.reshape(b * s, d), nbuf

    x2d, fbuf0 = ffn(x2d, 0)

    heads = p['heads']
    inner = p['m_cw'].shape[1]
    width = p['m_cw'].shape[0]
    cbuf = jnp.zeros((b, width - 1, inner), F32) if zero_init else st['mlstm_conv'].astype(F32)
    if zero_init:
        state = None
    else:
        state = (st['mlstm_C'].astype(F32), st['mlstm_n'].astype(F32),
                 jnp.broadcast_to(st['mlstm_m'].astype(F32)[:, :, None], (b, heads, V7X_LANES)))
    x3, c_out, n_out, m_out, ncbuf = _mlstm_layer(x2d.reshape(b, s, d), cbuf, state, p, g_of(1, 0), g_of(1, 1),
                                                  tiles.mlstm_chunk)
    x2d, fbuf1 = ffn(x3.reshape(b * s, d), 1)

    return (x2d.reshape(b, s, d), o_re, o_im, c_out[None], n_out[None], m_out[None, :, :, 0], ncbuf[None],
            jnp.stack([fbuf0, fbuf1], axis=0))


def kernel(x_prompt, x_sample, state_s5_re, state_s5_im, state_mlstm_C, state_mlstm_n, state_mlstm_m, state_mlstm_conv, state_ffn_conv, norm_g, s5_A_re, s5_A_im, s5_log_dt, s5_B_re, s5_B_im, s5_C_re, s5_C_im, s5_D, s5_w_glu, mlstm_w_up, mlstm_conv_w, mlstm_conv_b, mlstm_wq, mlstm_wk, mlstm_wv, mlstm_w_gate, mlstm_b_gate, mlstm_norm_g, mlstm_skip, mlstm_w_down, ffn_w_up, ffn_conv_w, ffn_conv_b, ffn_w_down):
    w = {
        'norm_g': norm_g, 's5_A_re': s5_A_re, 's5_A_im': s5_A_im, 's5_log_dt': s5_log_dt,
        's5_B_re': s5_B_re, 's5_B_im': s5_B_im, 's5_C_re': s5_C_re, 's5_C_im': s5_C_im, 's5_D': s5_D,
        's5_w_glu': s5_w_glu, 'mlstm_w_up': mlstm_w_up, 'mlstm_conv_w': mlstm_conv_w,
        'mlstm_conv_b': mlstm_conv_b, 'mlstm_wq': mlstm_wq, 'mlstm_wk': mlstm_wk, 'mlstm_wv': mlstm_wv,
        'mlstm_w_gate': mlstm_w_gate, 'mlstm_b_gate': mlstm_b_gate, 'mlstm_norm_g': mlstm_norm_g,
        'mlstm_skip': mlstm_skip, 'mlstm_w_down': mlstm_w_down, 'ffn_w_up': ffn_w_up,
        'ffn_conv_w': ffn_conv_w, 'ffn_conv_b': ffn_conv_b, 'ffn_w_down': ffn_w_down,
    }
    assert norm_g.shape[0] == 2 and s5_A_re.shape[0] == 1 and mlstm_w_up.shape[0] == 1
    assert x_prompt.shape[0] == V7X_SUBLANES and x_sample.shape[0] == V7X_SUBLANES
    p = _prep_weights(w)
    out_p = _trunk(x_prompt.astype(F32), None, p, _pick_tiles(*x_prompt.shape[:2]))
    st = {'s5_re': state_s5_re[0], 's5_im': state_s5_im[0], 'mlstm_C': state_mlstm_C[0],
          'mlstm_n': state_mlstm_n[0], 'mlstm_m': state_mlstm_m[0], 'mlstm_conv': state_mlstm_conv[0],
          'ffn_conv': state_ffn_conv}
    out_s = _trunk(x_sample.astype(F32), st, p, _pick_tiles(*x_sample.shape[:2]))
    return (out_p[0], out_s[0]) + tuple(out_p[1:]) + tuple(out_s[1:])
```

```python
import functools
import math
from typing import NamedTuple

import jax
import jax.numpy as jnp
from jax import lax
from jax.experimental import pallas as pl
from jax.experimental.pallas import tpu as pltpu

F32 = jnp.float32
BF16 = jnp.bfloat16
HI = lax.Precision.HIGHEST

NORM_EPS = 1e-6
LN_EPS = 1e-5
S5_GROUP_CH = 16
S5_CHUNK = 16
QKV_BLOCK = 4
FFN_ROW_BLOCK = 1024

V7X_LANES = 128
V7X_SUBLANES = 8
V7X_MXU = 256
V7X_VMEM_BYTES = 64 * 1024 * 1024


class Tiles(NamedTuple):
    rows: int
    time: int
    s5_rows: int
    mlstm_chunk: int
    s5_time: int
    ffn_time: int


def _pick_tiles(batch, seq):
    s5_time = (V7X_LANES // batch) * S5_CHUNK
    return Tiles(rows=min(batch * seq, 512), time=min(seq, 512),
                 s5_rows=min(batch * (seq // S5_CHUNK), 512), mlstm_chunk=min(seq, 256),
                 s5_time=s5_time if seq % s5_time == 0 else 0, ffn_time=min(seq, 1024))


def _vmem_limit(block_bytes, scratch_bytes):
    want = 2 * block_bytes + scratch_bytes + 16 * 1024 * 1024
    return int(min(want, V7X_VMEM_BYTES - 8 * 1024 * 1024))


def _nbytes(shape, dtype):
    return math.prod(shape) * jnp.dtype(dtype).itemsize


def _rms(x, g):
    return x * lax.rsqrt(jnp.mean(x * x, axis=-1, keepdims=True) + NORM_EPS) * g


def _dot(a, b):
    return jnp.dot(a, b, preferred_element_type=F32)


def _dot_nt(a, b):
    return lax.dot_general(a, b, (((1,), (1,)), ((), ())), preferred_element_type=F32)


def _silu(x):
    return (0.5 * x) * (1.0 + jnp.tanh(0.5 * x))


def _causal_dwconv(u, prev, cw_ref, cb_ref, cs):
    width = cw_ref.shape[0]
    acc = cb_ref[:, cs] + u * cw_ref[width - 1:width, cs]
    rows = lax.broadcasted_iota(jnp.int32, prev.shape, 0)
    for j in range(width - 1):
        d = width - 1 - j
        rolled = pltpu.roll(u, d, 0)
        head = jnp.where(rows < d, pltpu.roll(prev, d, 0), rolled[0:V7X_SUBLANES])
        shifted = jnp.concatenate([head, rolled[V7X_SUBLANES:]], axis=0)
        acc = acc + shifted * cw_ref[j:j + 1, cs]
    return acc


def _log_sigmoid(x):
    return jnp.minimum(x, 0.0) - jnp.log1p(jnp.exp(-jnp.abs(x)))


def _split3(x):
    hi = x.astype(BF16)
    r1 = x - hi.astype(F32)
    mid = r1.astype(BF16)
    lo = (r1 - mid.astype(F32)).astype(BF16)
    return hi, mid, lo


def _norm_cast_kernel(x_ref, g_ref, o_ref):
    o_ref[...] = _rms(x_ref[...], g_ref[...]).astype(o_ref.dtype)


def _norm_cast(x2d, g, tb):
    rows, d = x2d.shape
    return pl.pallas_call(
        _norm_cast_kernel,
        grid=(rows // tb,),
        in_specs=[pl.BlockSpec((tb, d), lambda i: (i, 0)), pl.BlockSpec((1, d), lambda i: (0, 0))],
        out_specs=pl.BlockSpec((tb, d), lambda i: (i, 0)),
        out_shape=jax.ShapeDtypeStruct((rows, d), BF16),
        name="s5_norm",
    )(x2d, g)


def _s5_kernel(u_ref, tt_ref, pt_ref, qt_ref, are_ref, aim_ref, h0re_ref, h0im_ref,
               y_ref, hre_ref, him_ref, pure_s, puim_s, hinre_s, hinim_s, stre_s, stim_s, *, nb, rb):
    r = pl.program_id(1)

    @pl.when(r == 0)
    def _():
        stre_s[...] = h0re_ref[0]
        stim_s[...] = h0im_ref[0]

    u = u_ref[...]
    w = V7X_MXU
    half = pt_ref.shape[1] // 2
    pu = [_dot_nt(u[:, i * w:(i + 1) * w], pt_ref[i]) for i in range(2)]
    pure_s[...] = jnp.concatenate([pu[0][:, :half], pu[1][:, :half]], axis=1)
    puim_s[...] = jnp.concatenate([pu[0][:, half:], pu[1][:, half:]], axis=1)
    ar = jnp.broadcast_to(are_ref[0], (nb, 2 * half))
    ai = jnp.broadcast_to(aim_ref[0], (nb, 2 * half))

    def step(i, carry):
        re, im = carry
        rows = pl.ds(pl.multiple_of(i * nb, nb), nb)
        hinre_s[rows, :] = re
        hinim_s[rows, :] = im
        return (ar * re - ai * im + pure_s[rows, :], ar * im + ai * re + puim_s[rows, :])

    re, im = lax.fori_loop(0, rb // nb, step, (stre_s[...], stim_s[...]))
    stre_s[...] = re
    stim_s[...] = im
    hre_ref[0] = re
    him_ref[0] = im

    hre = hinre_s[...]
    him = hinim_s[...]
    for i in range(2):
        hin = jnp.concatenate([hre[:, i * half:(i + 1) * half], him[:, i * half:(i + 1) * half]], axis=1)
        y_ref[:, i * w:(i + 1) * w] = (_dot_nt(u[:, i * w:(i + 1) * w], tt_ref[i])
                                       + _dot_nt(hin.astype(BF16), qt_ref[i])).astype(y_ref.dtype)


def _s5_core(ut, mats, h0re, h0im, nb, rb):
    rows, cols = ut.shape
    pairs = cols // (2 * V7X_MXU)
    tt, pt, qt, are, aim = mats
    nst = pt.shape[1]
    pair3 = lambda p, r: (p, 0, 0)
    kern = functools.partial(_s5_kernel, nb=nb, rb=rb)
    blocks = (_nbytes((rb, 512), BF16) * 2 + _nbytes((2, 256, 256), BF16) + 4 * _nbytes((256, 128), BF16))
    scratch = 4 * _nbytes((rb, 128), F32) + 2 * _nbytes((nb, 128), F32)
    return pl.pallas_call(
        kern,
        grid=(pairs, rows // rb),
        in_specs=[
            pl.BlockSpec((rb, 2 * V7X_MXU), lambda p, r: (r, p)),
            pl.BlockSpec((2, V7X_MXU, V7X_MXU), pair3),
            pl.BlockSpec((2, nst, V7X_MXU), pair3),
            pl.BlockSpec((2, V7X_MXU, nst), pair3),
            pl.BlockSpec((1, 1, 128), pair3),
            pl.BlockSpec((1, 1, 128), pair3),
            pl.BlockSpec((1, nb, 128), pair3),
            pl.BlockSpec((1, nb, 128), pair3),
        ],
        out_specs=[
            pl.BlockSpec((rb, 2 * V7X_MXU), lambda p, r: (r, p)),
            pl.BlockSpec((1, nb, 128), pair3),
            pl.BlockSpec((1, nb, 128), pair3),
        ],
        out_shape=[
            jax.ShapeDtypeStruct((rows, cols), BF16),
            jax.ShapeDtypeStruct((pairs, nb, 128), F32),
            jax.ShapeDtypeStruct((pairs, nb, 128), F32),
        ],
        scratch_shapes=[pltpu.VMEM((rb, 128), F32)] * 4 + [pltpu.VMEM((nb, 128), F32)] * 2,
        compiler_params=pltpu.CompilerParams(
            dimension_semantics=("arbitrary", "arbitrary"), vmem_limit_bytes=_vmem_limit(blocks, scratch)),
        name="s5_core",
    )(ut, tt, pt, qt, are, aim, h0re, h0im)


def _s5_fused_kernel(x_ref, g_ref, tt_ref, pt_ref, qt_ref, are_ref, aim_ref, h0re_ref, h0im_ref,
                     y_ref, hre_ref, him_ref,
                     us, ugt, pure, puim, hinre, hinim, ytmp, ys, stre, stim, *, nb, tbt):
    lc = S5_CHUNK
    ch = S5_GROUP_CH
    gps = V7X_LANES // ch
    nch = tbt // lc
    cols = nb * nch
    k = pl.program_id(1)

    @pl.when(k == 0)
    def _():
        for b in range(nb):
            u = _rms(x_ref[b], g_ref[...])
            for kk in range(us.shape[0]):
                us[kk, pl.ds(b, tbt, stride=nb), :] = u[:, kk * V7X_LANES:(kk + 1) * V7X_LANES]

    @pl.when(pl.program_id(0) == 0)
    def _():
        stre[k] = h0re_ref[...]
        stim[k] = h0im_ref[...]

    tile = lambda j, s: slice((j * lc + s) * nb, (j * lc + s + 1) * nb)
    for s in range(lc):
        a = jnp.concatenate([us[k, tile(j, s), :] for j in range(nch)], axis=0).T
        for g in range(gps):
            ugt[g, (lc - 1 - s) * ch:(lc - s) * ch, :] = a[g * ch:(g + 1) * ch, :].astype(BF16)

    half = pt_ref.shape[1] // 2
    for pr in range(gps // 2):
        put0 = _dot(pt_ref[2 * pr], ugt[2 * pr])
        put1 = _dot(pt_ref[2 * pr + 1], ugt[2 * pr + 1])
        pure[pr] = jnp.concatenate([put0[:half], put1[:half]], axis=0).T
        puim[pr] = jnp.concatenate([put0[half:], put1[half:]], axis=0).T

    for pr in range(gps // 2):
        re = stre[k, pr]
        im = stim[k, pr]
        ar = jnp.broadcast_to(are_ref[pr], re.shape)
        ai = jnp.broadcast_to(aim_ref[pr], re.shape)
        for j in range(nch):
            rows = slice(j * nb, (j + 1) * nb)
            hinre[pr, rows, :] = re
            hinim[pr, rows, :] = im
            re, im = (ar * re - ai * im + pure[pr, rows, :], ar * im + ai * re + puim[pr, rows, :])
        stre[k, pr] = re
        stim[k, pr] = im
        hre_ref[k * (gps // 2) + pr] = re
        him_ref[k * (gps // 2) + pr] = im

    for pr in range(gps // 2):
        hre_t = hinre[pr].T
        him_t = hinim[pr].T
        for i in range(2):
            g = 2 * pr + i
            hin_t = jnp.concatenate([hre_t[i * half:(i + 1) * half], him_t[i * half:(i + 1) * half]],
                                    axis=0).astype(BF16)
            ytmp[g] = _dot(tt_ref[g], ugt[g]) + _dot(qt_ref[g], hin_t)

    for t in range(lc):
        zt = jnp.concatenate([ytmp[g, t * ch:(t + 1) * ch, :] for g in range(gps)], axis=0).T
        for j in range(nch):
            ys[tile(j, t), :] = zt[j * nb:(j + 1) * nb, :]
    for b in range(nb):
        y_ref[b] = ys[pl.ds(b, tbt, stride=nb), :].astype(y_ref.dtype)


def _s5_fused(x, g, mats_t, h0re, h0im, tbt):
    nb, s, d = x.shape
    tt, pt, qt, are, aim = mats_t
    groups = tt.shape[0]
    gps = V7X_LANES // S5_GROUP_CH
    slabs = groups // gps
    w = S5_CHUNK * S5_GROUP_CH
    nst = pt.shape[1]
    cols = nb * (tbt // S5_CHUNK)
    kern = functools.partial(_s5_fused_kernel, nb=nb, tbt=tbt)
    slab3 = lambda t, k: (k, 0, 0)
    blocks = (_nbytes((nb, tbt, d), F32) + _nbytes((nb, tbt, V7X_LANES), BF16)
              + gps * (_nbytes((w, w), BF16) + 2 * _nbytes((w, nst), BF16)))
    scratch = (_nbytes((slabs, nb * tbt, V7X_LANES), F32) + _nbytes((gps, w, cols), BF16)
               + 4 * _nbytes((gps // 2, cols, nst), F32) + _nbytes((gps, w, cols), F32)
               + _nbytes((nb * tbt, V7X_LANES), F32))
    return pl.pallas_call(
        kern,
        grid=(s // tbt, slabs),
        in_specs=[
            pl.BlockSpec((nb, tbt, d), lambda t, k: (0, t, 0)),
            pl.BlockSpec((1, d), lambda t, k: (0, 0)),
            pl.BlockSpec((gps, w, w), slab3),
            pl.BlockSpec((gps, nst, w), slab3),
            pl.BlockSpec((gps, w, nst), slab3),
            pl.BlockSpec((gps // 2, 1, nst), slab3),
            pl.BlockSpec((gps // 2, 1, nst), slab3),
            pl.BlockSpec((gps // 2, nb, nst), slab3),
            pl.BlockSpec((gps // 2, nb, nst), slab3),
        ],
        out_specs=[
            pl.BlockSpec((nb, tbt, V7X_LANES), lambda t, k: (0, t, k)),
            pl.BlockSpec((groups // 2, nb, nst), lambda t, k: (0, 0, 0)),
            pl.BlockSpec((groups // 2, nb, nst), lambda t, k: (0, 0, 0)),
        ],
        out_shape=[
            jax.ShapeDtypeStruct((nb, s, d), BF16),
            jax.ShapeDtypeStruct((groups // 2, nb, nst), F32),
            jax.ShapeDtypeStruct((groups // 2, nb, nst), F32),
        ],
        scratch_shapes=[
            pltpu.VMEM((slabs, nb * tbt, V7X_LANES), F32),
            pltpu.VMEM((gps, w, cols), BF16),
            pltpu.VMEM((gps // 2, cols, nst), F32),
            pltpu.VMEM((gps // 2, cols, nst), F32),
            pltpu.VMEM((gps // 2, cols, nst), F32),
            pltpu.VMEM((gps // 2, cols, nst), F32),
            pltpu.VMEM((gps, w, cols), F32),
            pltpu.VMEM((nb * tbt, V7X_LANES), F32),
            pltpu.VMEM((slabs, gps // 2, nb, nst), F32),
            pltpu.VMEM((slabs, gps // 2, nb, nst), F32),
        ],
        compiler_params=pltpu.CompilerParams(
            dimension_semantics=("arbitrary", "arbitrary"), vmem_limit_bytes=_vmem_limit(blocks, scratch)),
        name="s5_fused",
    )(x, g, tt, pt, qt, are, aim, h0re, h0im)


def _s5_matrices(a_re, a_im, log_dt, b_re, b_im, c_re, c_im):
    lc = S5_CHUNK
    groups, n = a_re.shape
    ch = S5_GROUP_CH
    f = lambda v: v.astype(F32)
    a_re, a_im, b_re, b_im, c_re, c_im = map(f, (a_re, a_im, b_re, b_im, c_re, c_im))
    dt = jnp.exp(f(log_dt))[:, None, None]
    j = jnp.arange(lc + 1, dtype=F32)
    mag = jnp.exp((a_re[:, :, None] * dt) * j)
    ang = (a_im[:, :, None] * dt) * j
    pw_re, pw_im = mag * jnp.cos(ang), mag * jnp.sin(ang)
    e_re, e_im = pw_re[:, :, 1] - 1.0, pw_im[:, :, 1]
    inv = 1.0 / (a_re * a_re + a_im * a_im)
    f_re, f_im = (e_re * a_re + e_im * a_im) * inv, (e_im * a_re - e_re * a_im) * inv
    bb_re = f_re[..., None] * b_re - f_im[..., None] * b_im
    bb_im = f_re[..., None] * b_im + f_im[..., None] * b_re
    rep = lambda v: jnp.repeat(v, ch, axis=2)
    til = lambda v: jnp.tile(v, (1, 1, lc + 1))
    lb_re = rep(pw_re) * til(bb_re) - rep(pw_im) * til(bb_im)
    lb_im = rep(pw_re) * til(bb_im) + rep(pw_im) * til(bb_re)
    kern = jnp.einsum('gcm,gmx->gcx', jnp.concatenate([c_re, c_im], axis=2),
                      jnp.concatenate([lb_re, -lb_im], axis=1), precision=HI)
    w = lc * ch
    kfwd = jnp.pad(kern[:, :, :w], ((0, 0), (0, 0), ((lc - 1) * ch, 0)))
    tt = jnp.stack([kfwd[:, :, t * ch:t * ch + w] for t in range(lc)], axis=1).reshape(groups, w, w)
    pt = jnp.concatenate([lb_re[:, :, :w], lb_im[:, :, :w]], axis=1)
    pwt_re = jnp.swapaxes(pw_re, 1, 2)[:, 1:, None, :]
    pwt_im = jnp.swapaxes(pw_im, 1, 2)[:, 1:, None, :]
    ca_re = c_re[:, None] * pwt_re - c_im[:, None] * pwt_im
    ca_im = c_re[:, None] * pwt_im + c_im[:, None] * pwt_re
    qt = jnp.concatenate([ca_re, -ca_im], axis=3).reshape(groups, w, 2 * n)
    are = pw_re[:, :, lc].reshape(groups // 2, 1, 2 * n)
    aim = pw_im[:, :, lc].reshape(groups // 2, 1, 2 * n)
    return tt.astype(BF16), pt.astype(BF16), qt.astype(BF16), are, aim


def _glu_kernel(x_ref, y_ref, g0_ref, d_ref, w_ref, g1_ref, o_ref, out_s):
    x = x_ref[...]
    d = x.shape[1]
    u = _rms(x, g0_ref[...])
    yy = y_ref[...].astype(F32) + d_ref[...] * u
    z = jax.nn.gelu(yy, approximate=True).astype(BF16)
    w = V7X_MXU
    nslab = d // w
    pair = lambda j: (_dot(z, w_ref[:, j * w:(j + 1) * w]), _dot(z, w_ref[:, d + j * w:d + (j + 1) * w]))
    nxt = pair(0)
    ssq = jnp.zeros((x.shape[0], 1), F32)
    for j in range(nslab):
        a, b = nxt
        if j + 1 < nslab:
            nxt = pair(j + 1)
        out = a * (0.5 * (1.0 + jnp.tanh(0.5 * b)))
        out_s[:, j * w:(j + 1) * w] = out
        ssq = ssq + jnp.sum(out * out, axis=-1, keepdims=True)
    scale = lax.rsqrt(ssq * (1.0 / d) + NORM_EPS)
    o_ref[...] = x + out_s[...] * scale * g1_ref[...]


def _const_spec(shape):
    nd = len(shape)
    return pl.BlockSpec(shape, lambda *_: (0,) * nd, pipeline_mode=pl.Buffered(1))


def _glu(x2d, y2d, g0, dskip, w, g1, tb):
    rows, d = x2d.shape
    blocks = 2 * _nbytes((tb, d), F32) + _nbytes((tb, d), BF16)
    scratch = _nbytes(w.shape, BF16) + 3 * _nbytes((tb, 2 * d), F32)
    row = pl.BlockSpec((tb, d), lambda i: (i, 0))
    return pl.pallas_call(
        _glu_kernel,
        grid=(rows // tb,),
        in_specs=[row, row, _const_spec((1, d)), _const_spec((1, d)), _const_spec(w.shape), _const_spec((1, d))],
        out_specs=row,
        out_shape=jax.ShapeDtypeStruct((rows, d), F32),
        scratch_shapes=[pltpu.VMEM((tb, d), F32)],
        compiler_params=pltpu.CompilerParams(
            dimension_semantics=("arbitrary",), vmem_limit_bytes=_vmem_limit(blocks, scratch)),
        name="s5_glu",
    )(x2d, y2d, g0, dskip, w, g1)


def _ffn_kernel(x_ref, buf_ref, g2_ref, wup_ref, cw_ref, cb_ref, wdown_ref, g3_ref,
                o_ref, nbuf_ref, carry, act, *, tb, sub, dff, width):
    t = pl.program_id(1)
    pad = V7X_SUBLANES

    @pl.when(t == 0)
    def _():
        carry[...] = jnp.zeros_like(carry)
        carry[pad - (width - 1):pad, :] = buf_ref[0]

    fb = V7X_MXU
    nslab = dff // fb
    cols = lambda j: (slice(j * fb, (j + 1) * fb), slice(dff + j * fb, dff + (j + 1) * fb))

    def conv(u, cs):
        prev = carry[:, cs]
        carry[:, cs] = u[sub - pad:sub, :]
        return _causal_dwconv(u, prev, cw_ref, cb_ref, cs)

    def up_phase(r):
        hn = _rms(x_ref[0, r * sub:(r + 1) * sub, :], g2_ref[...]).astype(BF16)
        up = lambda j: tuple(_dot(hn, wup_ref[:, cs]) for cs in cols(j))
        nxt = up(0)
        for j in range(nslab):
            ug, uv = nxt
            if j + 1 < nslab:
                nxt = up(j + 1)
            gate = conv(ug, cols(j)[0])
            val = conv(uv, cols(j)[1])
            act[r % act.shape[0], :, j * fb:(j + 1) * fb] = (jax.nn.gelu(gate, approximate=True) * val).astype(BF16)

    nsub = tb // sub
    up_phase(0)
    for r in range(nsub):
        f = _dot(act[r % act.shape[0]], wdown_ref[...])
        if r + 1 < nsub:
            up_phase(r + 1)
        rows = slice(r * sub, (r + 1) * sub)
        o_ref[0, rows, :] = x_ref[0, rows, :] + _rms(f, g3_ref[...])
    nbuf_ref[0] = carry[pad - (width - 1):pad, :]


def _layer_spec(shape, layer):
    nd = len(shape) - 1
    return pl.BlockSpec((None,) + tuple(shape[1:]), lambda *_: (layer,) + (0,) * nd, pipeline_mode=pl.Buffered(1))


def _ffn(x, buf, g2, wup, cw, cb, wdown, g3, layer, tb):
    b, s, d = x.shape
    dff = wdown.shape[1]
    width = cw.shape[1]
    sub = min(tb, FFN_ROW_BLOCK)
    kern = functools.partial(_ffn_kernel, tb=tb, sub=sub, dff=dff, width=width)
    blocks = 2 * _nbytes((tb, d), F32)
    scratch = (_nbytes(wup.shape[1:], BF16) + _nbytes(wdown.shape[1:], BF16) + 6 * _nbytes((sub, V7X_MXU), F32)
               + _nbytes((8, 2 * dff), F32) + 2 * _nbytes((sub, dff), BF16) + 3 * _nbytes((sub, d), F32))
    xs = pl.BlockSpec((1, tb, d), lambda i, t: (i, t, 0))
    bs = pl.BlockSpec((1, width - 1, 2 * dff), lambda i, t: (i, 0, 0))
    return pl.pallas_call(
        kern,
        grid=(b, s // tb),
        in_specs=[xs, bs, _const_spec((1, d)), _layer_spec(wup.shape, layer), _layer_spec(cw.shape, layer),
                  _layer_spec(cb.shape, layer), _layer_spec(wdown.shape, layer), _const_spec((1, d))],
        out_specs=[xs, bs],
        out_shape=[jax.ShapeDtypeStruct((b, s, d), F32), jax.ShapeDtypeStruct((b, width - 1, 2 * dff), F32)],
        scratch_shapes=[pltpu.VMEM((V7X_SUBLANES, 2 * dff), F32), pltpu.VMEM((min(2, tb // sub), sub, dff), BF16)],
        compiler_params=pltpu.CompilerParams(
            dimension_semantics=("arbitrary", "arbitrary"), vmem_limit_bytes=_vmem_limit(blocks, scratch)),
        name="conv_ffn",
    )(x, buf, g2, wup, cw, cb, wdown, g3)


def _mlstm_weights_kernel(aq_ref, ak_ref, av_ref, gq_ref, gk_ref, gv_ref,
                          wq_ref, wkt_ref, wv_ref, gc_ref, gm_ref, *, kscale):
    w = V7X_MXU
    kb = QKV_BLOCK
    dotp = functools.partial(jnp.dot, preferred_element_type=F32, precision=HI)
    lane = lax.broadcasted_iota(jnp.int32, (V7X_LANES, w), 1)
    src = lax.broadcasted_iota(jnp.int32, (V7X_LANES, w), 0)
    spread = jnp.where(jnp.bitwise_and(lane, kb - 1) == src, 1.0, 0.0)
    shift = kb.bit_length() - 1
    rblk = lax.shift_right_logical(lax.broadcasted_iota(jnp.int32, (w, w), 0), shift)
    cblk = lax.shift_right_logical(lax.broadcasted_iota(jnp.int32, (w, w), 1), shift)
    tile = lambda a_ref: jnp.where(rblk == cblk, dotp(a_ref[0], spread), 0.0)
    tq, tk, tv = tile(aq_ref), tile(ak_ref), tile(av_ref)
    wq_ref[0] = tq.astype(BF16)
    wkt_ref[0] = (tk.T * kscale).astype(BF16)
    wv_ref[0] = tv.astype(BF16)
    gc_ref[...] = (dotp(tq, gq_ref[...]) + dotp(tk, gk_ref[...])).astype(BF16)
    gm_ref[...] = dotp(tv, gv_ref[...]).astype(BF16)


def _mlstm_weights(wq, wk, wv, w_gate, kscale):
    nblk, kb, _ = wq.shape
    assert kb == QKV_BLOCK and kb & (kb - 1) == 0
    w = V7X_MXU
    per = w // kb
    nt = nblk // per
    inner = nt * w
    compact = lambda m: jnp.pad(m.astype(F32).reshape(nt, w, kb), ((0, 0), (0, 0), (0, V7X_LANES - kb)))
    wg = jnp.pad(w_gate.astype(F32), ((0, 0), (0, V7X_LANES - w_gate.shape[1])))
    cm = pl.BlockSpec((1, w, V7X_LANES), lambda i: (i, 0, 0))
    gs = [pl.BlockSpec((w, V7X_LANES), lambda i, k=k: (k * nt + i, 0)) for k in range(3)]
    tile = pl.BlockSpec((1, w, w), lambda i: (i, 0, 0))
    fold = pl.BlockSpec((w, V7X_LANES), lambda i: (i, 0))
    return pl.pallas_call(
        functools.partial(_mlstm_weights_kernel, kscale=kscale),
        grid=(nt,),
        in_specs=[cm, cm, cm] + gs,
        out_specs=[tile, tile, tile, fold, fold],
        out_shape=[jax.ShapeDtypeStruct((nt, w, w), BF16)] * 3 + [jax.ShapeDtypeStruct((inner, V7X_LANES), BF16)] * 2,
        name="mlstm_weights",
    )(compact(wq), compact(wk), compact(wv), wg, wg, wg)


def _mpre_kernel(x_ref, cbuf_ref, g_ref, wup_ref, cw_ref, cb_ref, wq_ref, wkt_ref, wv_ref, gc_ref, gm_ref,
                 bg_ref, q_ref, kt_ref, v_ref, xc_ref, z_ref, gates_ref, nbuf_ref, carry,
                 *, tb, inner, width):
    t = pl.program_id(1)
    pad = V7X_SUBLANES

    @pl.when(t == 0)
    def _():
        carry[...] = jnp.zeros_like(carry)
        carry[pad - (width - 1):pad, :] = cbuf_ref[0]

    h = _rms(x_ref[0], g_ref[...]).astype(BF16)
    w = V7X_MXU
    gacc = jnp.zeros((tb, bg_ref.shape[1]), F32) + bg_ref[...]
    for i in range(inner // w):
        cs = slice(i * w, (i + 1) * w)
        xm = _dot(h, wup_ref[:, cs])
        prev = carry[:, cs]
        carry[:, cs] = xm[tb - pad:tb, :]
        xc = _silu(_causal_dwconv(xm, prev, cw_ref, cb_ref, cs)).astype(BF16)
        xmb = xm.astype(BF16)
        q_ref[0, :, cs] = _dot(xc, wq_ref[i]).astype(BF16)
        kt_ref[0, cs, :] = _dot_nt(wkt_ref[i], xc).astype(BF16)
        v_ref[0, :, cs] = _dot(xmb, wv_ref[i]).astype(BF16)
        xc_ref[0, :, cs] = xc
        z_ref[0, :, cs] = _silu(_dot(h, wup_ref[:, inner + i * w:inner + (i + 1) * w])).astype(BF16)
        gacc = gacc + _dot(xc, gc_ref[cs, :]) + _dot(xmb, gm_ref[cs, :])
    gates_ref[0] = gacc
    nbuf_ref[0] = carry[pad - (width - 1):pad, :]


def _mpre(x, cbuf, g, wup, cw, cb, wq_t, wkt_t, wv_t, gc, gm, bg, tb):
    b, s, d = x.shape
    inner = cw.shape[1]
    width = cw.shape[0]
    ng = bg.shape[1]
    kern = functools.partial(_mpre_kernel, tb=tb, inner=inner, width=width)
    act = pl.BlockSpec((1, tb, inner), lambda i, t: (i, t, 0))
    blocks = _nbytes((tb, d), F32) + 5 * _nbytes((tb, inner), BF16)
    scratch = (_nbytes(wup.shape, BF16) + 3 * _nbytes(wq_t.shape, BF16) + 2 * _nbytes((inner, 128), BF16)
               + _nbytes((8, inner), F32) + 8 * _nbytes((tb, V7X_MXU), F32))
    return pl.pallas_call(
        kern,
        grid=(b, s // tb),
        in_specs=[pl.BlockSpec((1, tb, d), lambda i, t: (i, t, 0)),
                  pl.BlockSpec((1, width - 1, inner), lambda i, t: (i, 0, 0)),
                  _const_spec((1, d)), _const_spec(wup.shape), _const_spec(cw.shape), _const_spec((1, inner)),
                  _const_spec(wq_t.shape), _const_spec(wkt_t.shape), _const_spec(wv_t.shape),
                  _const_spec(gc.shape), _const_spec(gm.shape), _const_spec((1, ng))],
        out_specs=[act, pl.BlockSpec((1, inner, tb), lambda i, t: (i, 0, t)), act, act, act,
                   pl.BlockSpec((1, tb, ng), lambda i, t: (i, t, 0)),
                   pl.BlockSpec((1, width - 1, inner), lambda i, t: (i, 0, 0))],
        out_shape=[jax.ShapeDtypeStruct((b, s, inner), BF16), jax.ShapeDtypeStruct((b, inner, s), BF16),
                   jax.ShapeDtypeStruct((b, s, inner), BF16), jax.ShapeDtypeStruct((b, s, inner), BF16),
                   jax.ShapeDtypeStruct((b, s, inner), BF16), jax.ShapeDtypeStruct((b, s, ng), F32),
                   jax.ShapeDtypeStruct((b, width - 1, inner), F32)],
        scratch_shapes=[pltpu.VMEM((V7X_SUBLANES, inner), F32)],
        compiler_params=pltpu.CompilerParams(
            dimension_semantics=("arbitrary", "arbitrary"), vmem_limit_bytes=_vmem_limit(blocks, scratch)),
        name="mlstm_pre",
    )(x, cbuf, g, wup, cw, cb, wq_t, wkt_t, wv_t, gc, gm, bg)


def _mlstm_chunk(q_ref, kt_ref, v_ref, g, gt, ct_s, n_s, m_s, fill, emit_hn, fill_update, *, lc, heads, dh):
    row = lax.broadcasted_iota(jnp.int32, (lc, lc), 0)
    col = lax.broadcasted_iota(jnp.int32, (lc, lc), 1)
    causal = col <= row
    tri_l = jnp.where(causal, 1.0, 0.0).astype(BF16)
    tri_u = jnp.where(row <= col, 1.0, 0.0).astype(BF16)
    b_cols = sum(_dot(tri_l, part) for part in _split3(_log_sigmoid(g)))
    b_rows = sum(_dot(part, tri_u) for part in _split3(_log_sigmoid(gt)))

    hsl = [slice(h * dh, (h + 1) * dh) for h in range(heads)]
    st = []
    for h in range(heads):
        b_col = b_cols[:, heads + h:heads + h + 1]
        b_row = b_rows[heads + h:heads + h + 1, :]
        i_row = gt[h:h + 1, :]
        m_prev = m_s[h:h + 1, 0:1]
        dmat = jnp.where(causal, b_col - b_row + i_row, -jnp.inf)
        inter = b_col + m_prev
        m_t = jnp.maximum(inter, jnp.max(dmat, axis=1, keepdims=True))
        st.append(dict(b_col=b_col, b_row=b_row, i_row=i_row, m_prev=m_prev, m_t=m_t,
                       wts=jnp.exp(dmat - m_t), a=jnp.exp(inter - m_t)))
        fill()
    for h in range(heads):
        s = st[h]
        s['sc'] = _dot(q_ref[0, :, hsl[h]], kt_ref[0, hsl[h], :]) * s['wts']
    out_mm = lambda h: (_dot(st[h]['sc'].astype(BF16), v_ref[0, :, hsl[h]]),
                        _dot(q_ref[0, :, hsl[h]], ct_s[h].astype(BF16)))
    nxt = out_mm(0)
    for h in range(heads):
        intra, inter_mm = nxt
        if h + 1 < heads:
            nxt = out_mm(h + 1)
        fill()
        s = st[h]
        qh = q_ref[0, :, hsl[h]]
        sc = s['sc']
        a = s['a']
        num = intra + a * inter_mm
        qn = jnp.sum(qh.astype(F32) * n_s[h:h + 1, :], axis=1, keepdims=True)
        den = jnp.sum(sc, axis=1, keepdims=True) + a * qn
        hh = num / jnp.maximum(jnp.abs(den), jnp.exp(-s['m_t']))
        mu = jnp.mean(hh, axis=1, keepdims=True)
        dev = hh - mu
        var = jnp.mean(dev * dev, axis=1, keepdims=True)
        emit_hn(h, dev * lax.rsqrt(var + LN_EPS))
    def update_mm(h):
        s = st[h]
        kth = kt_ref[0, hsl[h], :]
        b_last = s['b_row'][:, lc - 1:lc]
        g_row = b_last - s['b_row'] + s['i_row']
        m_new = jnp.maximum(b_last + s['m_prev'], jnp.max(g_row, axis=1, keepdims=True))
        decay = jnp.exp(b_last + s['m_prev'] - m_new)
        wg_row = jnp.exp(g_row - m_new)
        wg_col = jnp.exp(b_last - s['b_col'] + g[:, h:h + 1] - m_new)
        wv = (v_ref[0, :, hsl[h]].astype(F32) * wg_col).astype(BF16)
        wg16 = jnp.broadcast_to(wg_row, (16, lc)).astype(BF16)
        return m_new, decay, _dot(kth, wv), _dot_nt(wg16, kth)[0:1, :]

    fill_update(0)
    nxt = update_mm(0)
    for h in range(heads):
        m_new, decay, c_upd, n_upd = nxt
        if h + 1 < heads:
            fill_update(h + 1)
            nxt = update_mm(h + 1)
        ct_s[h] = decay * ct_s[h] + c_upd
        n_s[h:h + 1, :] = decay * n_s[h:h + 1, :] + n_upd
        m_s[h:h + 1, :] = jnp.broadcast_to(m_new, (1, V7X_LANES))


def _mlstm_layer_kernel(*refs, lc, heads, inner, width, zero_init):
    (x_ref, cbuf_ref, g0_ref, wup_ref, cw_ref, cb_ref, wq_ref, wkt_ref, wv_ref, gc_ref, gm_ref, bg_ref,
     ng_ref, sk_ref, wdown_ref, g1_ref) = refs[:16]
    n_in = 16
    if not zero_init:
        c0_ref, n0_ref, m0_ref = refs[16:19]
        n_in = 19
    o_ref, cout_ref, nout_ref, mout_ref, nbuf_ref = refs[n_in:n_in + 5]
    carry, q_s, kt_s, v_s, xc_s, sz_s, act_s, ct_s, n_s, m_s = refs[n_in + 5:]
    dh = inner // heads
    pad = V7X_SUBLANES
    c = pl.program_id(1)

    @pl.when(c == 0)
    def _():
        carry[...] = jnp.zeros_like(carry)
        carry[pad - (width - 1):pad, :] = cbuf_ref[0]
        if zero_init:
            ct_s[...] = jnp.zeros_like(ct_s)
            n_s[...] = jnp.zeros_like(n_s)
            m_s[...] = jnp.zeros_like(m_s)
        else:
            for h in range(heads):
                ct_s[h] = c0_ref[0, h].T
            n_s[...] = n0_ref[0]
            m_s[...] = m0_ref[0]

    x = x_ref[0]
    hx = _rms(x, g0_ref[...]).astype(BF16)
    w = V7X_MXU
    gacc = jnp.zeros((lc, bg_ref.shape[1]), F32) + bg_ref[...]
    nslab = inner // w
    col = lambda i: slice(i * w, (i + 1) * w)

    def project(i, xc, xmb, gacc):
        q_s[0, :, col(i)] = _dot(xc, wq_ref[i]).astype(BF16)
        kt_s[0, col(i), :] = _dot_nt(wkt_ref[i], xc).astype(BF16)
        v_s[0, :, col(i)] = _dot(xmb, wv_ref[i]).astype(BF16)
        xc_s[:, col(i)] = xc
        return gacc + _dot(xc, gc_ref[col(i), :]) + _dot(xmb, gm_ref[col(i), :])

    xm_next = _dot(hx, wup_ref[:, col(0)])
    pending = None
    for i in range(nslab):
        xm = xm_next
        if i + 1 < nslab:
            xm_next = _dot(hx, wup_ref[:, col(i + 1)])
        if pending is not None:
            gacc = project(*pending, gacc)
        prev = carry[:, col(i)]
        carry[:, col(i)] = xm[lc - pad:lc, :]
        xc = _silu(_causal_dwconv(xm, prev, cw_ref, cb_ref, col(i))).astype(BF16)
        pending = (i, xc, xm.astype(BF16))
    gacc = project(*pending, gacc)
    nbuf_ref[0] = carry[pad - (width - 1):pad, :]

    slabs = iter(range(inner // w))

    def gate_slab():
        i = next(slabs, None)
        if i is not None:
            cs = slice(i * w, (i + 1) * w)
            sz_s[:, cs] = _silu(_dot(hx, wup_ref[:, inner + i * w:inner + (i + 1) * w])).astype(BF16)

    hcols = lambda h: slice(h * dh, (h + 1) * dh)
    down = []

    def gated_out(h, hn):
        hs = hn * ng_ref[:, hcols(h)] + sk_ref[:, hcols(h)] * xc_s[:, hcols(h)].astype(F32)
        act_s[:, hcols(h)] = (hs * sz_s[:, hcols(h)].astype(F32)).astype(BF16)

    def down_proj(h):
        if h == 0:
            down.append(_dot(act_s[...], wdown_ref[...]))

    _mlstm_chunk(q_s, kt_s, v_s, gacc, gacc.T[0:2 * heads, :], ct_s, n_s, m_s, gate_slab, gated_out, down_proj,
                 lc=lc, heads=heads, dh=dh)
    for _ in slabs:
        raise AssertionError("fewer fill points than output-gate slabs")
    o_ref[0] = x + _rms(down[0], g1_ref[...])

    @pl.when(c == pl.num_programs(1) - 1)
    def _():
        for h in range(heads):
            cout_ref[0, h] = ct_s[h].T
        nout_ref[0] = n_s[...]
        mout_ref[0] = m_s[...]


def _mlstm_layer(x, cbuf, state, p, g0, g1, lc):
    b, s, d = x.shape
    heads = p['heads']
    inner = p['m_cw'].shape[1]
    width = p['m_cw'].shape[0]
    dh = inner // heads
    zero_init = state is None
    kern = functools.partial(_mlstm_layer_kernel, lc=lc, heads=heads, inner=inner, width=width,
                             zero_init=zero_init)
    xs = pl.BlockSpec((1, lc, d), lambda i, c: (i, c, 0))
    bufs = pl.BlockSpec((1, width - 1, inner), lambda i, c: (i, 0, 0))
    cs = pl.BlockSpec((1, heads, dh, dh), lambda i, c: (i, 0, 0, 0))
    ns = pl.BlockSpec((1, heads, dh), lambda i, c: (i, 0, 0))
    ms = pl.BlockSpec((1, heads, V7X_LANES), lambda i, c: (i, 0, 0))
    consts = [g0, p['m_wup'], p['m_cw'], p['m_cb'], p['m_wq'], p['m_wkt'], p['m_wv'], p['m_gc'], p['m_gm'],
              p['m_bg'], p['m_ng'], p['m_skip'], p['m_wdown'], g1]
    in_specs = [xs, bufs] + [_const_spec(a.shape) for a in consts]
    args = [x, cbuf] + consts
    if not zero_init:
        in_specs += [cs, ns, ms]
        args += list(state)
    blocks = 2 * _nbytes((lc, d), F32) + (1 if zero_init else 2) * _nbytes((heads, dh, dh), F32)
    scratch = (sum(_nbytes(a.shape, a.dtype) for a in consts) + 6 * _nbytes((lc, inner), BF16)
               + _nbytes((heads, dh, dh), F32) + 8 * _nbytes((lc, max(lc, dh)), F32) + 2 * _nbytes((dh, dh), F32))
    return pl.pallas_call(
        kern,
        grid=(b, s // lc),
        in_specs=in_specs,
        out_specs=[xs, cs, ns, ms, bufs],
        out_shape=[jax.ShapeDtypeStruct((b, s, d), F32), jax.ShapeDtypeStruct((b, heads, dh, dh), F32),
                   jax.ShapeDtypeStruct((b, heads, dh), F32), jax.ShapeDtypeStruct((b, heads, V7X_LANES), F32),
                   jax.ShapeDtypeStruct((b, width - 1, inner), F32)],
        scratch_shapes=[pltpu.VMEM((V7X_SUBLANES, inner), F32),
                        pltpu.VMEM((1, lc, inner), BF16), pltpu.VMEM((1, inner, lc), BF16),
                        pltpu.VMEM((1, lc, inner), BF16), pltpu.VMEM((lc, inner), BF16),
                        pltpu.VMEM((lc, inner), BF16), pltpu.VMEM((lc, inner), BF16),
                        pltpu.VMEM((heads, dh, dh), F32), pltpu.VMEM((heads, dh), F32),
                        pltpu.VMEM((heads, V7X_LANES), F32)],
        compiler_params=pltpu.CompilerParams(
            dimension_semantics=("arbitrary", "arbitrary"), vmem_limit_bytes=_vmem_limit(blocks, scratch)),
        name="mlstm_layer",
    )(*args)


def _prep_weights(w):
    row = lambda v: v.astype(F32).reshape(1, -1)
    p = {}
    p['norm_g'] = w['norm_g'].astype(F32)
    p['s5_mats'] = _s5_matrices(w['s5_A_re'][0], w['s5_A_im'][0], w['s5_log_dt'][0], w['s5_B_re'][0],
                                w['s5_B_im'][0], w['s5_C_re'][0], w['s5_C_im'][0])
    p['s5_D'] = row(w['s5_D'][0])
    p['s5_w_glu'] = w['s5_w_glu'][0].astype(BF16)
    inner = w['mlstm_conv_w'].shape[2]
    heads = w['mlstm_b_gate'].shape[1] // 2
    dh = inner // heads
    p['m_wq'], p['m_wkt'], p['m_wv'], p['m_gc'], p['m_gm'] = _mlstm_weights(
        w['mlstm_wq'][0], w['mlstm_wk'][0], w['mlstm_wv'][0], w['mlstm_w_gate'][0], dh ** -0.5)
    bg = row(w['mlstm_b_gate'][0])
    p['m_bg'] = jnp.pad(bg, ((0, 0), (0, V7X_LANES - bg.shape[1])))
    p['m_wup'] = w['mlstm_w_up'][0].astype(BF16)
    p['m_cw'] = w['mlstm_conv_w'][0].astype(F32)
    p['m_cb'] = row(w['mlstm_conv_b'][0])
    p['m_ng'] = row(w['mlstm_norm_g'][0])
    p['m_skip'] = row(w['mlstm_skip'][0])
    p['m_wdown'] = w['mlstm_w_down'][0].astype(BF16)
    p['heads'] = heads
    p['f_wup'] = w['ffn_w_up'].astype(BF16)
    p['f_cw'] = w['ffn_conv_w'].astype(F32)
    p['f_cb'] = w['ffn_conv_b'].astype(F32)[:, None, :]
    p['f_wdown'] = w['ffn_w_down'].astype(BF16)
    return p


def _to_chunk_major(u2d, b, s):
    d = u2d.shape[1]
    g = d // S5_GROUP_CH
    u = u2d.reshape(b, s // S5_CHUNK, S5_CHUNK, g, S5_GROUP_CH)[:, :, ::-1]
    return jnp.transpose(u, (1, 0, 3, 2, 4)).reshape((s // S5_CHUNK) * b, d * S5_CHUNK)


def _to_token_major(yt, b, s):
    d = yt.shape[1] // S5_CHUNK
    g = d // S5_GROUP_CH
    y = yt.reshape(s // S5_CHUNK, b, g, S5_CHUNK, S5_GROUP_CH)
    return jnp.transpose(y, (1, 0, 3, 2, 4)).reshape(b * s, d)


def _trunk(x, st, p, tiles):
    b, s, d = x.shape
    ng = p['norm_g']
    g_of = lambda layer, k: ng[layer, k].reshape(1, d)
    zero_init = st is None
    groups = d // S5_GROUP_CH
    x2d = x.reshape(b * s, d)

    n_state = p['s5_mats'][1].shape[1] // 2
    if zero_init:
        h0re = h0im = jnp.zeros((groups // 2, b, 2 * n_state), F32)
    else:
        pair = lambda h: jnp.transpose(h.astype(F32).reshape(b, groups // 2, 2 * n_state), (1, 0, 2))
        h0re, h0im = pair(st['s5_re']), pair(st['s5_im'])
    if tiles.s5_time:
        y, hre, him = _s5_fused(x, g_of(0, 0), p['s5_mats'], h0re, h0im, tiles.s5_time)
        y2d = y.reshape(b * s, d)
    else:
        u = _norm_cast(x2d, g_of(0, 0), tiles.rows)
        yt, hre, him = _s5_core(_to_chunk_major(u, b, s), p['s5_mats'], h0re, h0im, b, tiles.s5_rows)
        y2d = _to_token_major(yt, b, s)
    unpair = lambda h: jnp.transpose(h, (1, 0, 2)).reshape(1, b, groups, n_state)
    o_re, o_im = unpair(hre), unpair(him)
    x2d = _glu(x2d, y2d, g_of(0, 0), p['s5_D'], p['s5_w_glu'], g_of(0, 1), tiles.rows)

    def ffn(x2d, layer):
        dff2 = p['f_wup'].shape[2]
        width = p['f_cw'].shape[1]
        buf = jnp.zeros((b, width - 1, dff2), F32) if zero_init else st['ffn_conv'][layer].astype(F32)
        xo, nbuf = _ffn(x2d.reshape(b, s, d), buf, g_of(layer, 2), p['f_wup'], p['f_cw'], p['f_cb'],
                        p['f_wdown'], g_of(layer, 3), layer, tiles.ffn_time)
        return xo.reshape(b * s, d), nbuf

    x2d, fbuf0 = ffn(x2d, 0)

    heads = p['heads']
    inner = p['m_cw'].shape[1]
    width = p['m_cw'].shape[0]
    cbuf = jnp.zeros((b, width - 1, inner), F32) if zero_init else st['mlstm_conv'].astype(F32)
    if zero_init:
        state = None
    else:
        state = (st['mlstm_C'].astype(F32), st['mlstm_n'].astype(F32),
                 jnp.broadcast_to(st['mlstm_m'].astype(F32)[:, :, None], (b, heads, V7X_LANES)))
    x3, c_out, n_out, m_out, ncbuf = _mlstm_layer(x2d.reshape(b, s, d), cbuf, state, p, g_of(1, 0), g_of(1, 1),
                                                  tiles.mlstm_chunk)
    x2d, fbuf1 = ffn(x3.reshape(b * s, d), 1)

    return (x2d.reshape(b, s, d), o_re, o_im, c_out[None], n_out[None], m_out[None, :, :, 0], ncbuf[None],
            jnp.stack([fbuf0, fbuf1], axis=0))


def kernel(x_prompt, x_sample, state_s5_re, state_s5_im, state_mlstm_C, state_mlstm_n, state_mlstm_m, state_mlstm_conv, state_ffn_conv, norm_g, s5_A_re, s5_A_im, s5_log_dt, s5_B_re, s5_B_im, s5_C_re, s5_C_im, s5_D, s5_w_glu, mlstm_w_up, mlstm_conv_w, mlstm_conv_b, mlstm_wq, mlstm_wk, mlstm_wv, mlstm_w_gate, mlstm_b_gate, mlstm_norm_g, mlstm_skip, mlstm_w_down, ffn_w_up, ffn_conv_w, ffn_conv_b, ffn_w_down):
    w = {
        'norm_g': norm_g, 's5_A_re': s5_A_re, 's5_A_im': s5_A_im, 's5_log_dt': s5_log_dt,
        's5_B_re': s5_B_re, 's5_B_im': s5_B_im, 's5_C_re': s5_C_re, 's5_C_im': s5_C_im, 's5_D': s5_D,
        's5_w_glu': s5_w_glu, 'mlstm_w_up': mlstm_w_up, 'mlstm_conv_w': mlstm_conv_w,
        'mlstm_conv_b': mlstm_conv_b, 'mlstm_wq': mlstm_wq, 'mlstm_wk': mlstm_wk, 'mlstm_wv': mlstm_wv,
        'mlstm_w_gate': mlstm_w_gate, 'mlstm_b_gate': mlstm_b_gate, 'mlstm_norm_g': mlstm_norm_g,
        'mlstm_skip': mlstm_skip, 'mlstm_w_down': mlstm_w_down, 'ffn_w_up': ffn_w_up,
        'ffn_conv_w': ffn_conv_w, 'ffn_conv_b': ffn_conv_b, 'ffn_w_down': ffn_w_down,
    }
    assert norm_g.shape[0] == 2 and s5_A_re.shape[0] == 1 and mlstm_w_up.shape[0] == 1
    assert x_prompt.shape[0] == V7X_SUBLANES and x_sample.shape[0] == V7X_SUBLANES
    p = _prep_weights(w)
    out_p = _trunk(x_prompt.astype(F32), None, p, _pick_tiles(*x_prompt.shape[:2]))
    st = {'s5_re': state_s5_re[0], 's5_im': state_s5_im[0], 'mlstm_C': state_mlstm_C[0],
          'mlstm_n': state_mlstm_n[0], 'mlstm_m': state_mlstm_m[0], 'mlstm_conv': state_mlstm_conv[0],
          'ffn_conv': state_ffn_conv}
    out_s = _trunk(x_sample.astype(F32), st, p, _pick_tiles(*x_sample.shape[:2]))
    return (out_p[0], out_s[0]) + tuple(out_p[1:]) + tuple(out_s[1:])
```

```python
import functools
import math
from typing import NamedTuple

import jax
import jax.numpy as jnp
from jax import lax
from jax.experimental import pallas as pl
from jax.experimental.pallas import tpu as pltpu

F32 = jnp.float32
BF16 = jnp.bfloat16
HI = lax.Precision.HIGHEST

NORM_EPS = 1e-6
LN_EPS = 1e-5
S5_GROUP_CH = 16
S5_CHUNK = 16
QKV_BLOCK = 4
S5_SLABS_PER_STEP = 2
FFN_ROW_BLOCK = 1024

V7X_LANES = 128
V7X_SUBLANES = 8
V7X_MXU = 256
V7X_VMEM_BYTES = 64 * 1024 * 1024


class Tiles(NamedTuple):
    rows: int
    time: int
    s5_rows: int
    mlstm_chunk: int
    s5_time: int
    ffn_time: int


def _pick_tiles(batch, seq):
    s5_time = (V7X_LANES // batch) * S5_CHUNK
    return Tiles(rows=min(batch * seq, 512), time=min(seq, 512),
                 s5_rows=min(batch * (seq // S5_CHUNK), 512), mlstm_chunk=min(seq, 256),
                 s5_time=s5_time if seq % s5_time == 0 else 0, ffn_time=min(seq, 1024))


def _vmem_limit(block_bytes, scratch_bytes):
    want = 2 * block_bytes + scratch_bytes + 16 * 1024 * 1024
    return int(min(want, V7X_VMEM_BYTES - 8 * 1024 * 1024))


def _nbytes(shape, dtype):
    return math.prod(shape) * jnp.dtype(dtype).itemsize


def _rms(x, g):
    return x * lax.rsqrt(jnp.mean(x * x, axis=-1, keepdims=True) + NORM_EPS) * g


def _dot(a, b):
    return jnp.dot(a, b, preferred_element_type=F32)


def _dot_nt(a, b):
    return lax.dot_general(a, b, (((1,), (1,)), ((), ())), preferred_element_type=F32)


def _silu(x):
    return (0.5 * x) * (1.0 + jnp.tanh(0.5 * x))


def _causal_dwconv(u, prev, cw_ref, cb_ref, cs):
    width = cw_ref.shape[0]
    acc = cb_ref[:, cs] + u * cw_ref[width - 1:width, cs]
    rows = lax.broadcasted_iota(jnp.int32, prev.shape, 0)
    for j in range(width - 1):
        d = width - 1 - j
        rolled = pltpu.roll(u, d, 0)
        head = jnp.where(rows < d, pltpu.roll(prev, d, 0), rolled[0:V7X_SUBLANES])
        shifted = jnp.concatenate([head, rolled[V7X_SUBLANES:]], axis=0)
        acc = acc + shifted * cw_ref[j:j + 1, cs]
    return acc


def _log_sigmoid(x):
    return jnp.minimum(x, 0.0) - jnp.log1p(jnp.exp(-jnp.abs(x)))


def _split3(x):
    hi = x.astype(BF16)
    r1 = x - hi.astype(F32)
    mid = r1.astype(BF16)
    lo = (r1 - mid.astype(F32)).astype(BF16)
    return hi, mid, lo


def _norm_cast_kernel(x_ref, g_ref, o_ref):
    o_ref[...] = _rms(x_ref[...], g_ref[...]).astype(o_ref.dtype)


def _norm_cast(x2d, g, tb):
    rows, d = x2d.shape
    return pl.pallas_call(
        _norm_cast_kernel,
        grid=(rows // tb,),
        in_specs=[pl.BlockSpec((tb, d), lambda i: (i, 0)), pl.BlockSpec((1, d), lambda i: (0, 0))],
        out_specs=pl.BlockSpec((tb, d), lambda i: (i, 0)),
        out_shape=jax.ShapeDtypeStruct((rows, d), BF16),
        name="s5_norm",
    )(x2d, g)


def _s5_kernel(u_ref, tt_ref, pt_ref, qt_ref, are_ref, aim_ref, h0re_ref, h0im_ref,
               y_ref, hre_ref, him_ref, pure_s, puim_s, hinre_s, hinim_s, stre_s, stim_s, *, nb, rb):
    r = pl.program_id(1)

    @pl.when(r == 0)
    def _():
        stre_s[...] = h0re_ref[0]
        stim_s[...] = h0im_ref[0]

    u = u_ref[...]
    w = V7X_MXU
    half = pt_ref.shape[1] // 2
    pu = [_dot_nt(u[:, i * w:(i + 1) * w], pt_ref[i]) for i in range(2)]
    pure_s[...] = jnp.concatenate([pu[0][:, :half], pu[1][:, :half]], axis=1)
    puim_s[...] = jnp.concatenate([pu[0][:, half:], pu[1][:, half:]], axis=1)
    ar = jnp.broadcast_to(are_ref[0], (nb, 2 * half))
    ai = jnp.broadcast_to(aim_ref[0], (nb, 2 * half))

    def step(i, carry):
        re, im = carry
        rows = pl.ds(pl.multiple_of(i * nb, nb), nb)
        hinre_s[rows, :] = re
        hinim_s[rows, :] = im
        return (ar * re - ai * im + pure_s[rows, :], ar * im + ai * re + puim_s[rows, :])

    re, im = lax.fori_loop(0, rb // nb, step, (stre_s[...], stim_s[...]))
    stre_s[...] = re
    stim_s[...] = im
    hre_ref[0] = re
    him_ref[0] = im

    hre = hinre_s[...]
    him = hinim_s[...]
    for i in range(2):
        hin = jnp.concatenate([hre[:, i * half:(i + 1) * half], him[:, i * half:(i + 1) * half]], axis=1)
        y_ref[:, i * w:(i + 1) * w] = (_dot_nt(u[:, i * w:(i + 1) * w], tt_ref[i])
                                       + _dot_nt(hin.astype(BF16), qt_ref[i])).astype(y_ref.dtype)


def _s5_core(ut, mats, h0re, h0im, nb, rb):
    rows, cols = ut.shape
    pairs = cols // (2 * V7X_MXU)
    tt, pt, qt, are, aim = mats
    nst = pt.shape[1]
    pair3 = lambda p, r: (p, 0, 0)
    kern = functools.partial(_s5_kernel, nb=nb, rb=rb)
    blocks = (_nbytes((rb, 512), BF16) * 2 + _nbytes((2, 256, 256), BF16) + 4 * _nbytes((256, 128), BF16))
    scratch = 4 * _nbytes((rb, 128), F32) + 2 * _nbytes((nb, 128), F32)
    return pl.pallas_call(
        kern,
        grid=(pairs, rows // rb),
        in_specs=[
            pl.BlockSpec((rb, 2 * V7X_MXU), lambda p, r: (r, p)),
            pl.BlockSpec((2, V7X_MXU, V7X_MXU), pair3),
            pl.BlockSpec((2, nst, V7X_MXU), pair3),
            pl.BlockSpec((2, V7X_MXU, nst), pair3),
            pl.BlockSpec((1, 1, 128), pair3),
            pl.BlockSpec((1, 1, 128), pair3),
            pl.BlockSpec((1, nb, 128), pair3),
            pl.BlockSpec((1, nb, 128), pair3),
        ],
        out_specs=[
            pl.BlockSpec((rb, 2 * V7X_MXU), lambda p, r: (r, p)),
            pl.BlockSpec((1, nb, 128), pair3),
            pl.BlockSpec((1, nb, 128), pair3),
        ],
        out_shape=[
            jax.ShapeDtypeStruct((rows, cols), BF16),
            jax.ShapeDtypeStruct((pairs, nb, 128), F32),
            jax.ShapeDtypeStruct((pairs, nb, 128), F32),
        ],
        scratch_shapes=[pltpu.VMEM((rb, 128), F32)] * 4 + [pltpu.VMEM((nb, 128), F32)] * 2,
        compiler_params=pltpu.CompilerParams(
            dimension_semantics=("arbitrary", "arbitrary"), vmem_limit_bytes=_vmem_limit(blocks, scratch)),
        name="s5_core",
    )(ut, tt, pt, qt, are, aim, h0re, h0im)


def _s5_fused_kernel(x_ref, g_ref, tt_ref, pt_ref, qt_ref, are_ref, aim_ref, h0re_ref, h0im_ref,
                     y_ref, hre_ref, him_ref,
                     us, ugt, pure, puim, hinre, hinim, ytmp, ys, stre, stim, *, nb, tbt, spb):
    lc = S5_CHUNK
    ch = S5_GROUP_CH
    gps = V7X_LANES // ch
    nch = tbt // lc
    cols = nb * nch
    k = pl.program_id(1)
    ppg = gps // 2
    slab = lambda sl: k * spb + sl

    @pl.when(k == 0)
    def _():
        for b in range(nb):
            u = _rms(x_ref[b], g_ref[...])
            for kk in range(us.shape[0]):
                us[kk, pl.ds(b, tbt, stride=nb), :] = u[:, kk * V7X_LANES:(kk + 1) * V7X_LANES]

    @pl.when(pl.program_id(0) == 0)
    def _():
        for sl in range(spb):
            stre[slab(sl)] = h0re_ref[sl * ppg:(sl + 1) * ppg]
            stim[slab(sl)] = h0im_ref[sl * ppg:(sl + 1) * ppg]

    tile = lambda j, s: slice((j * lc + s) * nb, (j * lc + s + 1) * nb)
    for sl in range(spb):
        for s in range(lc):
            a = jnp.concatenate([us[slab(sl), tile(j, s), :] for j in range(nch)], axis=0).T
            for g in range(gps):
                ugt[sl * gps + g, (lc - 1 - s) * ch:(lc - s) * ch, :] = a[g * ch:(g + 1) * ch, :].astype(BF16)

    half = pt_ref.shape[1] // 2
    for p in range(spb * ppg):
        put0 = _dot(pt_ref[2 * p], ugt[2 * p])
        put1 = _dot(pt_ref[2 * p + 1], ugt[2 * p + 1])
        pure[p] = jnp.concatenate([put0[:half], put1[:half]], axis=0).T
        puim[p] = jnp.concatenate([put0[half:], put1[half:]], axis=0).T

    chains = [(sl, pr) for sl in range(spb) for pr in range(ppg)]
    state = [(stre[slab(sl), pr], stim[slab(sl), pr]) for sl, pr in chains]
    coef = [(jnp.broadcast_to(are_ref[sl * ppg + pr], (nb, 2 * half)),
             jnp.broadcast_to(aim_ref[sl * ppg + pr], (nb, 2 * half))) for sl, pr in chains]
    for j in range(nch):
        rows = slice(j * nb, (j + 1) * nb)
        for c, (sl, pr) in enumerate(chains):
            p = sl * ppg + pr
            re, im = state[c]
            ar, ai = coef[c]
            hinre[p, rows, :] = re
            hinim[p, rows, :] = im
            state[c] = (ar * re - ai * im + pure[p, rows, :], ar * im + ai * re + puim[p, rows, :])
    for c, (sl, pr) in enumerate(chains):
        re, im = state[c]
        stre[slab(sl), pr] = re
        stim[slab(sl), pr] = im
        hre_ref[slab(sl) * ppg + pr] = re
        him_ref[slab(sl) * ppg + pr] = im

    for p in range(spb * ppg):
        hre_t = hinre[p].T
        him_t = hinim[p].T
        for i in range(2):
            g = 2 * p + i
            hin_t = jnp.concatenate([hre_t[i * half:(i + 1) * half], him_t[i * half:(i + 1) * half]],
                                    axis=0).astype(BF16)
            ytmp[g] = _dot(tt_ref[g], ugt[g]) + _dot(qt_ref[g], hin_t)

    for sl in range(spb):
        for t in range(lc):
            zt = jnp.concatenate([ytmp[sl * gps + g, t * ch:(t + 1) * ch, :] for g in range(gps)], axis=0).T
            for j in range(nch):
                ys[sl, tile(j, t), :] = zt[j * nb:(j + 1) * nb, :]
    for sl in range(spb):
        for b in range(nb):
            y_ref[b, :, sl * V7X_LANES:(sl + 1) * V7X_LANES] = ys[sl, pl.ds(b, tbt, stride=nb), :].astype(y_ref.dtype)


def _s5_fused(x, g, mats_t, h0re, h0im, tbt):
    nb, s, d = x.shape
    tt, pt, qt, are, aim = mats_t
    groups = tt.shape[0]
    gps = V7X_LANES // S5_GROUP_CH
    slabs = groups // gps
    w = S5_CHUNK * S5_GROUP_CH
    nst = pt.shape[1]
    cols = nb * (tbt // S5_CHUNK)
    spb = S5_SLABS_PER_STEP
    ng, npair = spb * gps, spb * gps // 2
    kern = functools.partial(_s5_fused_kernel, nb=nb, tbt=tbt, spb=spb)
    slab3 = lambda t, k: (k, 0, 0)
    blocks = (_nbytes((nb, tbt, d), F32) + _nbytes((nb, tbt, spb * V7X_LANES), BF16)
              + ng * (_nbytes((w, w), BF16) + 2 * _nbytes((w, nst), BF16)))
    scratch = (_nbytes((slabs, nb * tbt, V7X_LANES), F32) + _nbytes((ng, w, cols), BF16)
               + 4 * _nbytes((npair, cols, nst), F32) + _nbytes((ng, w, cols), F32)
               + _nbytes((spb, nb * tbt, V7X_LANES), F32))
    return pl.pallas_call(
        kern,
        grid=(s // tbt, slabs // spb),
        in_specs=[
            pl.BlockSpec((nb, tbt, d), lambda t, k: (0, t, 0)),
            pl.BlockSpec((1, d), lambda t, k: (0, 0)),
            pl.BlockSpec((ng, w, w), slab3),
            pl.BlockSpec((ng, nst, w), slab3),
            pl.BlockSpec((ng, w, nst), slab3),
            pl.BlockSpec((npair, 1, nst), slab3),
            pl.BlockSpec((npair, 1, nst), slab3),
            pl.BlockSpec((npair, nb, nst), slab3),
            pl.BlockSpec((npair, nb, nst), slab3),
        ],
        out_specs=[
            pl.BlockSpec((nb, tbt, spb * V7X_LANES), lambda t, k: (0, t, k)),
            pl.BlockSpec((groups // 2, nb, nst), lambda t, k: (0, 0, 0)),
            pl.BlockSpec((groups // 2, nb, nst), lambda t, k: (0, 0, 0)),
        ],
        out_shape=[
            jax.ShapeDtypeStruct((nb, s, d), BF16),
            jax.ShapeDtypeStruct((groups // 2, nb, nst), F32),
            jax.ShapeDtypeStruct((groups // 2, nb, nst), F32),
        ],
        scratch_shapes=[
            pltpu.VMEM((slabs, nb * tbt, V7X_LANES), F32),
            pltpu.VMEM((ng, w, cols), BF16),
            pltpu.VMEM((npair, cols, nst), F32),
            pltpu.VMEM((npair, cols, nst), F32),
            pltpu.VMEM((npair, cols, nst), F32),
            pltpu.VMEM((npair, cols, nst), F32),
            pltpu.VMEM((ng, w, cols), F32),
            pltpu.VMEM((spb, nb * tbt, V7X_LANES), F32),
            pltpu.VMEM((slabs, gps // 2, nb, nst), F32),
            pltpu.VMEM((slabs, gps // 2, nb, nst), F32),
        ],
        compiler_params=pltpu.CompilerParams(
            dimension_semantics=("arbitrary", "arbitrary"), vmem_limit_bytes=_vmem_limit(blocks, scratch)),
        name="s5_fused",
    )(x, g, tt, pt, qt, are, aim, h0re, h0im)


def _s5_matrices(a_re, a_im, log_dt, b_re, b_im, c_re, c_im):
    lc = S5_CHUNK
    groups, n = a_re.shape
    ch = S5_GROUP_CH
    f = lambda v: v.astype(F32)
    a_re, a_im, b_re, b_im, c_re, c_im = map(f, (a_re, a_im, b_re, b_im, c_re, c_im))
    dt = jnp.exp(f(log_dt))[:, None, None]
    j = jnp.arange(lc + 1, dtype=F32)
    mag = jnp.exp((a_re[:, :, None] * dt) * j)
    ang = (a_im[:, :, None] * dt) * j
    pw_re, pw_im = mag * jnp.cos(ang), mag * jnp.sin(ang)
    e_re, e_im = pw_re[:, :, 1] - 1.0, pw_im[:, :, 1]
    inv = 1.0 / (a_re * a_re + a_im * a_im)
    f_re, f_im = (e_re * a_re + e_im * a_im) * inv, (e_im * a_re - e_re * a_im) * inv
    bb_re = f_re[..., None] * b_re - f_im[..., None] * b_im
    bb_im = f_re[..., None] * b_im + f_im[..., None] * b_re
    rep = lambda v: jnp.repeat(v, ch, axis=2)
    til = lambda v: jnp.tile(v, (1, 1, lc + 1))
    lb_re = rep(pw_re) * til(bb_re) - rep(pw_im) * til(bb_im)
    lb_im = rep(pw_re) * til(bb_im) + rep(pw_im) * til(bb_re)
    kern = jnp.einsum('gcm,gmx->gcx', jnp.concatenate([c_re, c_im], axis=2),
                      jnp.concatenate([lb_re, -lb_im], axis=1), precision=HI)
    w = lc * ch
    kfwd = jnp.pad(kern[:, :, :w], ((0, 0), (0, 0), ((lc - 1) * ch, 0)))
    tt = jnp.stack([kfwd[:, :, t * ch:t * ch + w] for t in range(lc)], axis=1).reshape(groups, w, w)
    pt = jnp.concatenate([lb_re[:, :, :w], lb_im[:, :, :w]], axis=1)
    pwt_re = jnp.swapaxes(pw_re, 1, 2)[:, 1:, None, :]
    pwt_im = jnp.swapaxes(pw_im, 1, 2)[:, 1:, None, :]
    ca_re = c_re[:, None] * pwt_re - c_im[:, None] * pwt_im
    ca_im = c_re[:, None] * pwt_im + c_im[:, None] * pwt_re
    qt = jnp.concatenate([ca_re, -ca_im], axis=3).reshape(groups, w, 2 * n)
    are = pw_re[:, :, lc].reshape(groups // 2, 1, 2 * n)
    aim = pw_im[:, :, lc].reshape(groups // 2, 1, 2 * n)
    return tt.astype(BF16), pt.astype(BF16), qt.astype(BF16), are, aim


def _glu_kernel(x_ref, y_ref, g0_ref, d_ref, w_ref, g1_ref, o_ref, out_s):
    x = x_ref[...]
    d = x.shape[1]
    u = _rms(x, g0_ref[...])
    yy = y_ref[...].astype(F32) + d_ref[...] * u
    z = jax.nn.gelu(yy, approximate=True).astype(BF16)
    w = V7X_MXU
    nslab = d // w
    pair = lambda j: (_dot(z, w_ref[:, j * w:(j + 1) * w]), _dot(z, w_ref[:, d + j * w:d + (j + 1) * w]))
    nxt = pair(0)
    ssq = jnp.zeros((x.shape[0], 1), F32)
    for j in range(nslab):
        a, b = nxt
        if j + 1 < nslab:
            nxt = pair(j + 1)
        out = a * (0.5 * (1.0 + jnp.tanh(0.5 * b)))
        out_s[:, j * w:(j + 1) * w] = out
        ssq = ssq + jnp.sum(out * out, axis=-1, keepdims=True)
    scale = lax.rsqrt(ssq * (1.0 / d) + NORM_EPS)
    o_ref[...] = x + out_s[...] * scale * g1_ref[...]


def _const_spec(shape):
    nd = len(shape)
    return pl.BlockSpec(shape, lambda *_: (0,) * nd, pipeline_mode=pl.Buffered(1))


def _glu(x2d, y2d, g0, dskip, w, g1, tb):
    rows, d = x2d.shape
    blocks = 2 * _nbytes((tb, d), F32) + _nbytes((tb, d), BF16)
    scratch = _nbytes(w.shape, BF16) + 3 * _nbytes((tb, 2 * d), F32)
    row = pl.BlockSpec((tb, d), lambda i: (i, 0))
    return pl.pallas_call(
        _glu_kernel,
        grid=(rows // tb,),
        in_specs=[row, row, _const_spec((1, d)), _const_spec((1, d)), _const_spec(w.shape), _const_spec((1, d))],
        out_specs=row,
        out_shape=jax.ShapeDtypeStruct((rows, d), F32),
        scratch_shapes=[pltpu.VMEM((tb, d), F32)],
        compiler_params=pltpu.CompilerParams(
            dimension_semantics=("arbitrary",), vmem_limit_bytes=_vmem_limit(blocks, scratch)),
        name="s5_glu",
    )(x2d, y2d, g0, dskip, w, g1)


def _ffn_kernel(x_ref, buf_ref, g2_ref, wup_ref, cw_ref, cb_ref, wdown_ref, g3_ref,
                o_ref, nbuf_ref, carry, act, *, tb, sub, dff, width):
    t = pl.program_id(1)
    pad = V7X_SUBLANES

    @pl.when(t == 0)
    def _():
        carry[...] = jnp.zeros_like(carry)
        carry[pad - (width - 1):pad, :] = buf_ref[0]

    fb = V7X_MXU
    nslab = dff // fb
    cols = lambda j: (slice(j * fb, (j + 1) * fb), slice(dff + j * fb, dff + (j + 1) * fb))

    def conv(u, cs):
        prev = carry[:, cs]
        carry[:, cs] = u[sub - pad:sub, :]
        return _causal_dwconv(u, prev, cw_ref, cb_ref, cs)

    def up_phase(r):
        hn = _rms(x_ref[0, r * sub:(r + 1) * sub, :], g2_ref[...]).astype(BF16)
        up = lambda j: tuple(_dot(hn, wup_ref[:, cs]) for cs in cols(j))
        nxt = up(0)
        for j in range(nslab):
            ug, uv = nxt
            if j + 1 < nslab:
                nxt = up(j + 1)
            gate = conv(ug, cols(j)[0])
            val = conv(uv, cols(j)[1])
            act[r % act.shape[0], :, j * fb:(j + 1) * fb] = (jax.nn.gelu(gate, approximate=True) * val).astype(BF16)

    nsub = tb // sub
    up_phase(0)
    for r in range(nsub):
        f = _dot(act[r % act.shape[0]], wdown_ref[...])
        if r + 1 < nsub:
            up_phase(r + 1)
        rows = slice(r * sub, (r + 1) * sub)
        o_ref[0, rows, :] = x_ref[0, rows, :] + _rms(f, g3_ref[...])
    nbuf_ref[0] = carry[pad - (width - 1):pad, :]


def _layer_spec(shape, layer):
    nd = len(shape) - 1
    return pl.BlockSpec((None,) + tuple(shape[1:]), lambda *_: (layer,) + (0,) * nd, pipeline_mode=pl.Buffered(1))


def _ffn(x, buf, g2, wup, cw, cb, wdown, g3, layer, tb):
    b, s, d = x.shape
    dff = wdown.shape[1]
    width = cw.shape[1]
    sub = min(tb, FFN_ROW_BLOCK)
    kern = functools.partial(_ffn_kernel, tb=tb, sub=sub, dff=dff, width=width)
    blocks = 2 * _nbytes((tb, d), F32)
    scratch = (_nbytes(wup.shape[1:], BF16) + _nbytes(wdown.shape[1:], BF16) + 6 * _nbytes((sub, V7X_MXU), F32)
               + _nbytes((8, 2 * dff), F32) + 2 * _nbytes((sub, dff), BF16) + 3 * _nbytes((sub, d), F32))
    xs = pl.BlockSpec((1, tb, d), lambda i, t: (i, t, 0))
    bs = pl.BlockSpec((1, width - 1, 2 * dff), lambda i, t: (i, 0, 0))
    return pl.pallas_call(
        kern,
        grid=(b, s // tb),
        in_specs=[xs, bs, _const_spec((1, d)), _layer_spec(wup.shape, layer), _layer_spec(cw.shape, layer),
                  _layer_spec(cb.shape, layer), _layer_spec(wdown.shape, layer), _const_spec((1, d))],
        out_specs=[xs, bs],
        out_shape=[jax.ShapeDtypeStruct((b, s, d), F32), jax.ShapeDtypeStruct((b, width - 1, 2 * dff), F32)],
        scratch_shapes=[pltpu.VMEM((V7X_SUBLANES, 2 * dff), F32), pltpu.VMEM((min(2, tb // sub), sub, dff), BF16)],
        compiler_params=pltpu.CompilerParams(
            dimension_semantics=("arbitrary", "arbitrary"), vmem_limit_bytes=_vmem_limit(blocks, scratch)),
        name="conv_ffn",
    )(x, buf, g2, wup, cw, cb, wdown, g3)


def _mlstm_weights_kernel(aq_ref, ak_ref, av_ref, gq_ref, gk_ref, gv_ref,
                          wq_ref, wkt_ref, wv_ref, gc_ref, gm_ref, *, kscale):
    w = V7X_MXU
    kb = QKV_BLOCK
    dotp = functools.partial(jnp.dot, preferred_element_type=F32, precision=HI)
    lane = lax.broadcasted_iota(jnp.int32, (V7X_LANES, w), 1)
    src = lax.broadcasted_iota(jnp.int32, (V7X_LANES, w), 0)
    spread = jnp.where(jnp.bitwise_and(lane, kb - 1) == src, 1.0, 0.0)
    shift = kb.bit_length() - 1
    rblk = lax.shift_right_logical(lax.broadcasted_iota(jnp.int32, (w, w), 0), shift)
    cblk = lax.shift_right_logical(lax.broadcasted_iota(jnp.int32, (w, w), 1), shift)
    tile = lambda a_ref: jnp.where(rblk == cblk, dotp(a_ref[0], spread), 0.0)
    tq, tk, tv = tile(aq_ref), tile(ak_ref), tile(av_ref)
    wq_ref[0] = tq.astype(BF16)
    wkt_ref[0] = (tk.T * kscale).astype(BF16)
    wv_ref[0] = tv.astype(BF16)
    gc_ref[...] = (dotp(tq, gq_ref[...]) + dotp(tk, gk_ref[...])).astype(BF16)
    gm_ref[...] = dotp(tv, gv_ref[...]).astype(BF16)


def _mlstm_weights(wq, wk, wv, w_gate, kscale):
    nblk, kb, _ = wq.shape
    assert kb == QKV_BLOCK and kb & (kb - 1) == 0
    w = V7X_MXU
    per = w // kb
    nt = nblk // per
    inner = nt * w
    compact = lambda m: jnp.pad(m.astype(F32).reshape(nt, w, kb), ((0, 0), (0, 0), (0, V7X_LANES - kb)))
    wg = jnp.pad(w_gate.astype(F32), ((0, 0), (0, V7X_LANES - w_gate.shape[1])))
    cm = pl.BlockSpec((1, w, V7X_LANES), lambda i: (i, 0, 0))
    gs = [pl.BlockSpec((w, V7X_LANES), lambda i, k=k: (k * nt + i, 0)) for k in range(3)]
    tile = pl.BlockSpec((1, w, w), lambda i: (i, 0, 0))
    fold = pl.BlockSpec((w, V7X_LANES), lambda i: (i, 0))
    return pl.pallas_call(
        functools.partial(_mlstm_weights_kernel, kscale=kscale),
        grid=(nt,),
        in_specs=[cm, cm, cm] + gs,
        out_specs=[tile, tile, tile, fold, fold],
        out_shape=[jax.ShapeDtypeStruct((nt, w, w), BF16)] * 3 + [jax.ShapeDtypeStruct((inner, V7X_LANES), BF16)] * 2,
        name="mlstm_weights",
    )(compact(wq), compact(wk), compact(wv), wg, wg, wg)


def _mpre_kernel(x_ref, cbuf_ref, g_ref, wup_ref, cw_ref, cb_ref, wq_ref, wkt_ref, wv_ref, gc_ref, gm_ref,
                 bg_ref, q_ref, kt_ref, v_ref, xc_ref, z_ref, gates_ref, nbuf_ref, carry,
                 *, tb, inner, width):
    t = pl.program_id(1)
    pad = V7X_SUBLANES

    @pl.when(t == 0)
    def _():
        carry[...] = jnp.zeros_like(carry)
        carry[pad - (width - 1):pad, :] = cbuf_ref[0]

    h = _rms(x_ref[0], g_ref[...]).astype(BF16)
    w = V7X_MXU
    gacc = jnp.zeros((tb, bg_ref.shape[1]), F32) + bg_ref[...]
    for i in range(inner // w):
        cs = slice(i * w, (i + 1) * w)
        xm = _dot(h, wup_ref[:, cs])
        prev = carry[:, cs]
        carry[:, cs] = xm[tb - pad:tb, :]
        xc = _silu(_causal_dwconv(xm, prev, cw_ref, cb_ref, cs)).astype(BF16)
        xmb = xm.astype(BF16)
        q_ref[0, :, cs] = _dot(xc, wq_ref[i]).astype(BF16)
        kt_ref[0, cs, :] = _dot_nt(wkt_ref[i], xc).astype(BF16)
        v_ref[0, :, cs] = _dot(xmb, wv_ref[i]).astype(BF16)
        xc_ref[0, :, cs] = xc
        z_ref[0, :, cs] = _silu(_dot(h, wup_ref[:, inner + i * w:inner + (i + 1) * w])).astype(BF16)
        gacc = gacc + _dot(xc, gc_ref[cs, :]) + _dot(xmb, gm_ref[cs, :])
    gates_ref[0] = gacc
    nbuf_ref[0] = carry[pad - (width - 1):pad, :]


def _mpre(x, cbuf, g, wup, cw, cb, wq_t, wkt_t, wv_t, gc, gm, bg, tb):
    b, s, d = x.shape
    inner = cw.shape[1]
    width = cw.shape[0]
    ng = bg.shape[1]
    kern = functools.partial(_mpre_kernel, tb=tb, inner=inner, width=width)
    act = pl.BlockSpec((1, tb, inner), lambda i, t: (i, t, 0))
    blocks = _nbytes((tb, d), F32) + 5 * _nbytes((tb, inner), BF16)
    scratch = (_nbytes(wup.shape, BF16) + 3 * _nbytes(wq_t.shape, BF16) + 2 * _nbytes((inner, 128), BF16)
               + _nbytes((8, inner), F32) + 8 * _nbytes((tb, V7X_MXU), F32))
    return pl.pallas_call(
        kern,
        grid=(b, s // tb),
        in_specs=[pl.BlockSpec((1, tb, d), lambda i, t: (i, t, 0)),
                  pl.BlockSpec((1, width - 1, inner), lambda i, t: (i, 0, 0)),
                  _const_spec((1, d)), _const_spec(wup.shape), _const_spec(cw.shape), _const_spec((1, inner)),
                  _const_spec(wq_t.shape), _const_spec(wkt_t.shape), _const_spec(wv_t.shape),
                  _const_spec(gc.shape), _const_spec(gm.shape), _const_spec((1, ng))],
        out_specs=[act, pl.BlockSpec((1, inner, tb), lambda i, t: (i, 0, t)), act, act, act,
                   pl.BlockSpec((1, tb, ng), lambda i, t: (i, t, 0)),
                   pl.BlockSpec((1, width - 1, inner), lambda i, t: (i, 0, 0))],
        out_shape=[jax.ShapeDtypeStruct((b, s, inner), BF16), jax.ShapeDtypeStruct((b, inner, s), BF16),
                   jax.ShapeDtypeStruct((b, s, inner), BF16), jax.ShapeDtypeStruct((b, s, inner), BF16),
                   jax.ShapeDtypeStruct((b, s, inner), BF16), jax.ShapeDtypeStruct((b, s, ng), F32),
                   jax.ShapeDtypeStruct((b, width - 1, inner), F32)],
        scratch_shapes=[pltpu.VMEM((V7X_SUBLANES, inner), F32)],
        compiler_params=pltpu.CompilerParams(
            dimension_semantics=("arbitrary", "arbitrary"), vmem_limit_bytes=_vmem_limit(blocks, scratch)),
        name="mlstm_pre",
    )(x, cbuf, g, wup, cw, cb, wq_t, wkt_t, wv_t, gc, gm, bg)


def _mlstm_chunk(q_ref, kt_ref, v_ref, g, gt, ct_s, n_s, m_s, fill, emit_hn, fill_update, *, lc, heads, dh):
    row = lax.broadcasted_iota(jnp.int32, (lc, lc), 0)
    col = lax.broadcasted_iota(jnp.int32, (lc, lc), 1)
    causal = col <= row
    tri_l = jnp.where(causal, 1.0, 0.0).astype(BF16)
    tri_u = jnp.where(row <= col, 1.0, 0.0).astype(BF16)
    b_cols = sum(_dot(tri_l, part) for part in _split3(_log_sigmoid(g)))
    b_rows = sum(_dot(part, tri_u) for part in _split3(_log_sigmoid(gt)))

    hsl = [slice(h * dh, (h + 1) * dh) for h in range(heads)]
    st = []
    for h in range(heads):
        b_col = b_cols[:, heads + h:heads + h + 1]
        b_row = b_rows[heads + h:heads + h + 1, :]
        i_row = gt[h:h + 1, :]
        m_prev = m_s[h:h + 1, 0:1]
        dmat = jnp.where(causal, b_col - b_row + i_row, -jnp.inf)
        inter = b_col + m_prev
        m_t = jnp.maximum(inter, jnp.max(dmat, axis=1, keepdims=True))
        st.append(dict(b_col=b_col, b_row=b_row, i_row=i_row, m_prev=m_prev, m_t=m_t,
                       wts=jnp.exp(dmat - m_t), a=jnp.exp(inter - m_t)))
        fill()
    for h in range(heads):
        s = st[h]
        s['sc'] = _dot(q_ref[0, :, hsl[h]], kt_ref[0, hsl[h], :]) * s['wts']
    out_mm = lambda h: (_dot(st[h]['sc'].astype(BF16), v_ref[0, :, hsl[h]]),
                        _dot(q_ref[0, :, hsl[h]], ct_s[h].astype(BF16)))
    nxt = out_mm(0)
    for h in range(heads):
        intra, inter_mm = nxt
        if h + 1 < heads:
            nxt = out_mm(h + 1)
        fill()
        s = st[h]
        qh = q_ref[0, :, hsl[h]]
        sc = s['sc']
        a = s['a']
        num = intra + a * inter_mm
        qn = jnp.sum(qh.astype(F32) * n_s[h:h + 1, :], axis=1, keepdims=True)
        den = jnp.sum(sc, axis=1, keepdims=True) + a * qn
        hh = num / jnp.maximum(jnp.abs(den), jnp.exp(-s['m_t']))
        mu = jnp.mean(hh, axis=1, keepdims=True)
        dev = hh - mu
        var = jnp.mean(dev * dev, axis=1, keepdims=True)
        emit_hn(h, dev * lax.rsqrt(var + LN_EPS))
    def update_mm(h):
        s = st[h]
        kth = kt_ref[0, hsl[h], :]
        b_last = s['b_row'][:, lc - 1:lc]
        g_row = b_last - s['b_row'] + s['i_row']
        m_new = jnp.maximum(b_last + s['m_prev'], jnp.max(g_row, axis=1, keepdims=True))
        decay = jnp.exp(b_last + s['m_prev'] - m_new)
        wg_row = jnp.exp(g_row - m_new)
        wg_col = jnp.exp(b_last - s['b_col'] + g[:, h:h + 1] - m_new)
        wv = (v_ref[0, :, hsl[h]].astype(F32) * wg_col).astype(BF16)
        wg16 = jnp.broadcast_to(wg_row, (16, lc)).astype(BF16)
        return m_new, decay, _dot(kth, wv), _dot_nt(wg16, kth)[0:1, :]

    fill_update(0)
    nxt = update_mm(0)
    for h in range(heads):
        m_new, decay, c_upd, n_upd = nxt
        if h + 1 < heads:
            fill_update(h + 1)
            nxt = update_mm(h + 1)
        ct_s[h] = decay * ct_s[h] + c_upd
        n_s[h:h + 1, :] = decay * n_s[h:h + 1, :] + n_upd
        m_s[h:h + 1, :] = jnp.broadcast_to(m_new, (1, V7X_LANES))


def _mlstm_layer_kernel(*refs, lc, heads, inner, width, zero_init):
    (x_ref, cbuf_ref, g0_ref, wup_ref, cw_ref, cb_ref, wq_ref, wkt_ref, wv_ref, gc_ref, gm_ref, bg_ref,
     ng_ref, sk_ref, wdown_ref, g1_ref) = refs[:16]
    n_in = 16
    if not zero_init:
        c0_ref, n0_ref, m0_ref = refs[16:19]
        n_in = 19
    o_ref, cout_ref, nout_ref, mout_ref, nbuf_ref = refs[n_in:n_in + 5]
    carry, q_s, kt_s, v_s, xc_s, sz_s, act_s, ct_s, n_s, m_s = refs[n_in + 5:]
    dh = inner // heads
    pad = V7X_SUBLANES
    c = pl.program_id(1)

    @pl.when(c == 0)
    def _():
        carry[...] = jnp.zeros_like(carry)
        carry[pad - (width - 1):pad, :] = cbuf_ref[0]
        if zero_init:
            ct_s[...] = jnp.zeros_like(ct_s)
            n_s[...] = jnp.zeros_like(n_s)
            m_s[...] = jnp.zeros_like(m_s)
        else:
            for h in range(heads):
                ct_s[h] = c0_ref[0, h].T
            n_s[...] = n0_ref[0]
            m_s[...] = m0_ref[0]

    x = x_ref[0]
    hx = _rms(x, g0_ref[...]).astype(BF16)
    w = V7X_MXU
    gacc = jnp.zeros((lc, bg_ref.shape[1]), F32) + bg_ref[...]
    nslab = inner // w
    col = lambda i: slice(i * w, (i + 1) * w)

    def project(i, xc, xmb, gacc):
        q_s[0, :, col(i)] = _dot(xc, wq_ref[i]).astype(BF16)
        kt_s[0, col(i), :] = _dot_nt(wkt_ref[i], xc).astype(BF16)
        v_s[0, :, col(i)] = _dot(xmb, wv_ref[i]).astype(BF16)
        xc_s[:, col(i)] = xc
        return gacc + _dot(xc, gc_ref[col(i), :]) + _dot(xmb, gm_ref[col(i), :])

    xm_next = _dot(hx, wup_ref[:, col(0)])
    pending = None
    for i in range(nslab):
        xm = xm_next
        if i + 1 < nslab:
            xm_next = _dot(hx, wup_ref[:, col(i + 1)])
        if pending is not None:
            gacc = project(*pending, gacc)
        prev = carry[:, col(i)]
        carry[:, col(i)] = xm[lc - pad:lc, :]
        xc = _silu(_causal_dwconv(xm, prev, cw_ref, cb_ref, col(i))).astype(BF16)
        pending = (i, xc, xm.astype(BF16))
    gacc = project(*pending, gacc)
    nbuf_ref[0] = carry[pad - (width - 1):pad, :]

    slabs = iter(range(inner // w))

    def gate_slab():
        i = next(slabs, None)
        if i is not None:
            cs = slice(i * w, (i + 1) * w)
            sz_s[:, cs] = _silu(_dot(hx, wup_ref[:, inner + i * w:inner + (i + 1) * w])).astype(BF16)

    hcols = lambda h: slice(h * dh, (h + 1) * dh)
    down = []

    def gated_out(h, hn):
        hs = hn * ng_ref[:, hcols(h)] + sk_ref[:, hcols(h)] * xc_s[:, hcols(h)].astype(F32)
        act_s[:, hcols(h)] = (hs * sz_s[:, hcols(h)].astype(F32)).astype(BF16)

    def down_proj(h):
        if h == 0:
            down.append(_dot(act_s[...], wdown_ref[...]))

    _mlstm_chunk(q_s, kt_s, v_s, gacc, gacc.T[0:2 * heads, :], ct_s, n_s, m_s, gate_slab, gated_out, down_proj,
                 lc=lc, heads=heads, dh=dh)
    for _ in slabs:
        raise AssertionError("fewer fill points than output-gate slabs")
    o_ref[0] = x + _rms(down[0], g1_ref[...])

    @pl.when(c == pl.num_programs(1) - 1)
    def _():
        for h in range(heads):
            cout_ref[0, h] = ct_s[h].T
        nout_ref[0] = n_s[...]
        mout_ref[0] = m_s[...]


def _mlstm_layer(x, cbuf, state, p, g0, g1, lc):
    b, s, d = x.shape
    heads = p['heads']
    inner = p['m_cw'].shape[1]
    width = p['m_cw'].shape[0]
    dh = inner // heads
    zero_init = state is None
    kern = functools.partial(_mlstm_layer_kernel, lc=lc, heads=heads, inner=inner, width=width,
                             zero_init=zero_init)
    xs = pl.BlockSpec((1, lc, d), lambda i, c: (i, c, 0))
    bufs = pl.BlockSpec((1, width - 1, inner), lambda i, c: (i, 0, 0))
    cs = pl.BlockSpec((1, heads, dh, dh), lambda i, c: (i, 0, 0, 0))
    ns = pl.BlockSpec((1, heads, dh), lambda i, c: (i, 0, 0))
    ms = pl.BlockSpec((1, heads, V7X_LANES), lambda i, c: (i, 0, 0))
    consts = [g0, p['m_wup'], p['m_cw'], p['m_cb'], p['m_wq'], p['m_wkt'], p['m_wv'], p['m_gc'], p['m_gm'],
              p['m_bg'], p['m_ng'], p['m_skip'], p['m_wdown'], g1]
    in_specs = [xs, bufs] + [_const_spec(a.shape) for a in consts]
    args = [x, cbuf] + consts
    if not zero_init:
        in_specs += [cs, ns, ms]
        args += list(state)
    blocks = 2 * _nbytes((lc, d), F32) + (1 if zero_init else 2) * _nbytes((heads, dh, dh), F32)
    scratch = (sum(_nbytes(a.shape, a.dtype) for a in consts) + 6 * _nbytes((lc, inner), BF16)
               + _nbytes((heads, dh, dh), F32) + 8 * _nbytes((lc, max(lc, dh)), F32) + 2 * _nbytes((dh, dh), F32))
    return pl.pallas_call(
        kern,
        grid=(b, s // lc),
        in_specs=in_specs,
        out_specs=[xs, cs, ns, ms, bufs],
        out_shape=[jax.ShapeDtypeStruct((b, s, d), F32), jax.ShapeDtypeStruct((b, heads, dh, dh), F32),
                   jax.ShapeDtypeStruct((b, heads, dh), F32), jax.ShapeDtypeStruct((b, heads, V7X_LANES), F32),
                   jax.ShapeDtypeStruct((b, width - 1, inner), F32)],
        scratch_shapes=[pltpu.VMEM((V7X_SUBLANES, inner), F32),
                        pltpu.VMEM((1, lc, inner), BF16), pltpu.VMEM((1, inner, lc), BF16),
                        pltpu.VMEM((1, lc, inner), BF16), pltpu.VMEM((lc, inner), BF16),
                        pltpu.VMEM((lc, inner), BF16), pltpu.VMEM((lc, inner), BF16),
                        pltpu.VMEM((heads, dh, dh), F32), pltpu.VMEM((heads, dh), F32),
                        pltpu.VMEM((heads, V7X_LANES), F32)],
        compiler_params=pltpu.CompilerParams(
            dimension_semantics=("arbitrary", "arbitrary"), vmem_limit_bytes=_vmem_limit(blocks, scratch)),
        name="mlstm_layer",
    )(*args)


def _prep_weights(w):
    row = lambda v: v.astype(F32).reshape(1, -1)
    p = {}
    p['norm_g'] = w['norm_g'].astype(F32)
    p['s5_mats'] = _s5_matrices(w['s5_A_re'][0], w['s5_A_im'][0], w['s5_log_dt'][0], w['s5_B_re'][0],
                                w['s5_B_im'][0], w['s5_C_re'][0], w['s5_C_im'][0])
    p['s5_D'] = row(w['s5_D'][0])
    p['s5_w_glu'] = w['s5_w_glu'][0].astype(BF16)
    inner = w['mlstm_conv_w'].shape[2]
    heads = w['mlstm_b_gate'].shape[1] // 2
    dh = inner // heads
    p['m_wq'], p['m_wkt'], p['m_wv'], p['m_gc'], p['m_gm'] = _mlstm_weights(
        w['mlstm_wq'][0], w['mlstm_wk'][0], w['mlstm_wv'][0], w['mlstm_w_gate'][0], dh ** -0.5)
    bg = row(w['mlstm_b_gate'][0])
    p['m_bg'] = jnp.pad(bg, ((0, 0), (0, V7X_LANES - bg.shape[1])))
    p['m_wup'] = w['mlstm_w_up'][0].astype(BF16)
    p['m_cw'] = w['mlstm_conv_w'][0].astype(F32)
    p['m_cb'] = row(w['mlstm_conv_b'][0])
    p['m_ng'] = row(w['mlstm_norm_g'][0])
    p['m_skip'] = row(w['mlstm_skip'][0])
    p['m_wdown'] = w['mlstm_w_down'][0].astype(BF16)
    p['heads'] = heads
    p['f_wup'] = w['ffn_w_up'].astype(BF16)
    p['f_cw'] = w['ffn_conv_w'].astype(F32)
    p['f_cb'] = w['ffn_conv_b'].astype(F32)[:, None, :]
    p['f_wdown'] = w['ffn_w_down'].astype(BF16)
    return p


def _to_chunk_major(u2d, b, s):
    d = u2d.shape[1]
    g = d // S5_GROUP_CH
    u = u2d.reshape(b, s // S5_CHUNK, S5_CHUNK, g, S5_GROUP_CH)[:, :, ::-1]
    return jnp.transpose(u, (1, 0, 3, 2, 4)).reshape((s // S5_CHUNK) * b, d * S5_CHUNK)


def _to_token_major(yt, b, s):
    d = yt.shape[1] // S5_CHUNK
    g = d // S5_GROUP_CH
    y = yt.reshape(s // S5_CHUNK, b, g, S5_CHUNK, S5_GROUP_CH)
    return jnp.transpose(y, (1, 0, 3, 2, 4)).reshape(b * s, d)


def _trunk(x, st, p, tiles):
    b, s, d = x.shape
    ng = p['norm_g']
    g_of = lambda layer, k: ng[layer, k].reshape(1, d)
    zero_init = st is None
    groups = d // S5_GROUP_CH
    x2d = x.reshape(b * s, d)

    n_state = p['s5_mats'][1].shape[1] // 2
    if zero_init:
        h0re = h0im = jnp.zeros((groups // 2, b, 2 * n_state), F32)
    else:
        pair = lambda h: jnp.transpose(h.astype(F32).reshape(b, groups // 2, 2 * n_state), (1, 0, 2))
        h0re, h0im = pair(st['s5_re']), pair(st['s5_im'])
    if tiles.s5_time:
        y, hre, him = _s5_fused(x, g_of(0, 0), p['s5_mats'], h0re, h0im, tiles.s5_time)
        y2d = y.reshape(b * s, d)
    else:
        u = _norm_cast(x2d, g_of(0, 0), tiles.rows)
        yt, hre, him = _s5_core(_to_chunk_major(u, b, s), p['s5_mats'], h0re, h0im, b, tiles.s5_rows)
        y2d = _to_token_major(yt, b, s)
    unpair = lambda h: jnp.transpose(h, (1, 0, 2)).reshape(1, b, groups, n_state)
    o_re, o_im = unpair(hre), unpair(him)
    x2d = _glu(x2d, y2d, g_of(0, 0), p['s5_D'], p['s5_w_glu'], g_of(0, 1), tiles.rows)

    def ffn(x2d, layer):
        dff2 = p['f_wup'].shape[2]
        width = p['f_cw'].shape[1]
        buf = jnp.zeros((b, width - 1, dff2), F32) if zero_init else st['ffn_conv'][layer].astype(F32)
        xo, nbuf = _ffn(x2d.reshape(b, s, d), buf, g_of(layer, 2), p['f_wup'], p['f_cw'], p['f_cb'],
                        p['f_wdown'], g_of(layer, 3), layer, tiles.ffn_time)
        return xo.reshape(b * s, d), nbuf

    x2d, fbuf0 = ffn(x2d, 0)

    heads = p['heads']
    inner = p['m_cw'].shape[1]
    width = p['m_cw'].shape[0]
    cbuf = jnp.zeros((b, width - 1, inner), F32) if zero_init else st['mlstm_conv'].astype(F32)
    if zero_init:
        state = None
    else:
        state = (st['mlstm_C'].astype(F32), st['mlstm_n'].astype(F32),
                 jnp.broadcast_to(st['mlstm_m'].astype(F32)[:, :, None], (b, heads, V7X_LANES)))
    x3, c_out, n_out, m_out, ncbuf = _mlstm_layer(x2d.reshape(b, s, d), cbuf, state, p, g_of(1, 0), g_of(1, 1),
                                                  tiles.mlstm_chunk)
    x2d, fbuf1 = ffn(x3.reshape(b * s, d), 1)

    return (x2d.reshape(b, s, d), o_re, o_im, c_out[None], n_out[None], m_out[None, :, :, 0], ncbuf[None],
            jnp.stack([fbuf0, fbuf1], axis=0))


def kernel(x_prompt, x_sample, state_s5_re, state_s5_im, state_mlstm_C, state_mlstm_n, state_mlstm_m, state_mlstm_conv, state_ffn_conv, norm_g, s5_A_re, s5_A_im, s5_log_dt, s5_B_re, s5_B_im, s5_C_re, s5_C_im, s5_D, s5_w_glu, mlstm_w_up, mlstm_conv_w, mlstm_conv_b, mlstm_wq, mlstm_wk, mlstm_wv, mlstm_w_gate, mlstm_b_gate, mlstm_norm_g, mlstm_skip, mlstm_w_down, ffn_w_up, ffn_conv_w, ffn_conv_b, ffn_w_down):
    w = {
        'norm_g': norm_g, 's5_A_re': s5_A_re, 's5_A_im': s5_A_im, 's5_log_dt': s5_log_dt,
        's5_B_re': s5_B_re, 's5_B_im': s5_B_im, 's5_C_re': s5_C_re, 's5_C_im': s5_C_im, 's5_D': s5_D,
        's5_w_glu': s5_w_glu, 'mlstm_w_up': mlstm_w_up, 'mlstm_conv_w': mlstm_conv_w,
        'mlstm_conv_b': mlstm_conv_b, 'mlstm_wq': mlstm_wq, 'mlstm_wk': mlstm_wk, 'mlstm_wv': mlstm_wv,
        'mlstm_w_gate': mlstm_w_gate, 'mlstm_b_gate': mlstm_b_gate, 'mlstm_norm_g': mlstm_norm_g,
        'mlstm_skip': mlstm_skip, 'mlstm_w_down': mlstm_w_down, 'ffn_w_up': ffn_w_up,
        'ffn_conv_w': ffn_conv_w, 'ffn_conv_b': ffn_conv_b, 'ffn_w_down': ffn_w_down,
    }
    assert norm_g.shape[0] == 2 and s5_A_re.shape[0] == 1 and mlstm_w_up.shape[0] == 1
    assert x_prompt.shape[0] == V7X_SUBLANES and x_sample.shape[0] == V7X_SUBLANES
    p = _prep_weights(w)
    out_p = _trunk(x_prompt.astype(F32), None, p, _pick_tiles(*x_prompt.shape[:2]))
    st = {'s5_re': state_s5_re[0], 's5_im': state_s5_im[0], 'mlstm_C': state_mlstm_C[0],
          'mlstm_n': state_mlstm_n[0], 'mlstm_m': state_mlstm_m[0], 'mlstm_conv': state_mlstm_conv[0],
          'ffn_conv': state_ffn_conv}
    out_s = _trunk(x_sample.astype(F32), st, p, _pick_tiles(*x_sample.shape[:2]))
    return (out_p[0], out_s[0]) + tuple(out_p[1:]) + tuple(out_s[1:])
```

```python
import functools
import math
from typing import NamedTuple

import jax
import jax.numpy as jnp
from jax import lax
from jax.experimental import pallas as pl
from jax.experimental.pallas import tpu as pltpu

F32 = jnp.float32
BF16 = jnp.bfloat16
HI = lax.Precision.HIGHEST

NORM_EPS = 1e-6
LN_EPS = 1e-5
S5_GROUP_CH = 16
S5_CHUNK = 16
QKV_BLOCK = 4
S5_SLABS_PER_STEP = 2
GLU_ROW_BLOCK = 512
FFN_ROW_BLOCK = 1024

V7X_LANES = 128
V7X_SUBLANES = 8
V7X_MXU = 256
V7X_VMEM_BYTES = 64 * 1024 * 1024


class Tiles(NamedTuple):
    rows: int
    time: int
    s5_rows: int
    mlstm_chunk: int
    s5_time: int
    ffn_time: int


def _pick_tiles(batch, seq):
    s5_time = (V7X_LANES // batch) * S5_CHUNK
    return Tiles(rows=min(batch * seq, 1024), time=min(seq, 512),
                 s5_rows=min(batch * (seq // S5_CHUNK), 512), mlstm_chunk=min(seq, 256),
                 s5_time=s5_time if seq % s5_time == 0 else 0, ffn_time=min(seq, 1024))


def _vmem_limit(block_bytes, scratch_bytes):
    want = 2 * block_bytes + scratch_bytes + 16 * 1024 * 1024
    return int(min(want, V7X_VMEM_BYTES - 8 * 1024 * 1024))


def _nbytes(shape, dtype):
    return math.prod(shape) * jnp.dtype(dtype).itemsize


def _rms(x, g):
    return x * lax.rsqrt(jnp.mean(x * x, axis=-1, keepdims=True) + NORM_EPS) * g


def _dot(a, b):
    return jnp.dot(a, b, preferred_element_type=F32)


def _dot_nt(a, b):
    return lax.dot_general(a, b, (((1,), (1,)), ((), ())), preferred_element_type=F32)


def _silu(x):
    return (0.5 * x) * (1.0 + jnp.tanh(0.5 * x))


def _causal_dwconv(u, prev, cw_ref, cb_ref, cs):
    width = cw_ref.shape[0]
    acc = cb_ref[:, cs] + u * cw_ref[width - 1:width, cs]
    rows = lax.broadcasted_iota(jnp.int32, prev.shape, 0)
    for j in range(width - 1):
        d = width - 1 - j
        rolled = pltpu.roll(u, d, 0)
        head = jnp.where(rows < d, pltpu.roll(prev, d, 0), rolled[0:V7X_SUBLANES])
        shifted = jnp.concatenate([head, rolled[V7X_SUBLANES:]], axis=0)
        acc = acc + shifted * cw_ref[j:j + 1, cs]
    return acc


def _log_sigmoid(x):
    return jnp.minimum(x, 0.0) - jnp.log1p(jnp.exp(-jnp.abs(x)))


def _split3(x):
    hi = x.astype(BF16)
    r1 = x - hi.astype(F32)
    mid = r1.astype(BF16)
    lo = (r1 - mid.astype(F32)).astype(BF16)
    return hi, mid, lo


def _norm_cast_kernel(x_ref, g_ref, o_ref):
    o_ref[...] = _rms(x_ref[...], g_ref[...]).astype(o_ref.dtype)


def _norm_cast(x2d, g, tb):
    rows, d = x2d.shape
    return pl.pallas_call(
        _norm_cast_kernel,
        grid=(rows // tb,),
        in_specs=[pl.BlockSpec((tb, d), lambda i: (i, 0)), pl.BlockSpec((1, d), lambda i: (0, 0))],
        out_specs=pl.BlockSpec((tb, d), lambda i: (i, 0)),
        out_shape=jax.ShapeDtypeStruct((rows, d), BF16),
        name="s5_norm",
    )(x2d, g)


def _s5_kernel(u_ref, tt_ref, pt_ref, qt_ref, are_ref, aim_ref, h0re_ref, h0im_ref,
               y_ref, hre_ref, him_ref, pure_s, puim_s, hinre_s, hinim_s, stre_s, stim_s, *, nb, rb):
    r = pl.program_id(1)

    @pl.when(r == 0)
    def _():
        stre_s[...] = h0re_ref[0]
        stim_s[...] = h0im_ref[0]

    u = u_ref[...]
    w = V7X_MXU
    half = pt_ref.shape[1] // 2
    pu = [_dot_nt(u[:, i * w:(i + 1) * w], pt_ref[i]) for i in range(2)]
    pure_s[...] = jnp.concatenate([pu[0][:, :half], pu[1][:, :half]], axis=1)
    puim_s[...] = jnp.concatenate([pu[0][:, half:], pu[1][:, half:]], axis=1)
    ar = jnp.broadcast_to(are_ref[0], (nb, 2 * half))
    ai = jnp.broadcast_to(aim_ref[0], (nb, 2 * half))

    def step(i, carry):
        re, im = carry
        rows = pl.ds(pl.multiple_of(i * nb, nb), nb)
        hinre_s[rows, :] = re
        hinim_s[rows, :] = im
        return (ar * re - ai * im + pure_s[rows, :], ar * im + ai * re + puim_s[rows, :])

    re, im = lax.fori_loop(0, rb // nb, step, (stre_s[...], stim_s[...]))
    stre_s[...] = re
    stim_s[...] = im
    hre_ref[0] = re
    him_ref[0] = im

    hre = hinre_s[...]
    him = hinim_s[...]
    for i in range(2):
        hin = jnp.concatenate([hre[:, i * half:(i + 1) * half], him[:, i * half:(i + 1) * half]], axis=1)
        y_ref[:, i * w:(i + 1) * w] = (_dot_nt(u[:, i * w:(i + 1) * w], tt_ref[i])
                                       + _dot_nt(hin.astype(BF16), qt_ref[i])).astype(y_ref.dtype)


def _s5_core(ut, mats, h0re, h0im, nb, rb):
    rows, cols = ut.shape
    pairs = cols // (2 * V7X_MXU)
    tt, pt, qt, are, aim = mats
    nst = pt.shape[1]
    pair3 = lambda p, r: (p, 0, 0)
    kern = functools.partial(_s5_kernel, nb=nb, rb=rb)
    blocks = (_nbytes((rb, 512), BF16) * 2 + _nbytes((2, 256, 256), BF16) + 4 * _nbytes((256, 128), BF16))
    scratch = 4 * _nbytes((rb, 128), F32) + 2 * _nbytes((nb, 128), F32)
    return pl.pallas_call(
        kern,
        grid=(pairs, rows // rb),
        in_specs=[
            pl.BlockSpec((rb, 2 * V7X_MXU), lambda p, r: (r, p)),
            pl.BlockSpec((2, V7X_MXU, V7X_MXU), pair3),
            pl.BlockSpec((2, nst, V7X_MXU), pair3),
            pl.BlockSpec((2, V7X_MXU, nst), pair3),
            pl.BlockSpec((1, 1, 128), pair3),
            pl.BlockSpec((1, 1, 128), pair3),
            pl.BlockSpec((1, nb, 128), pair3),
            pl.BlockSpec((1, nb, 128), pair3),
        ],
        out_specs=[
            pl.BlockSpec((rb, 2 * V7X_MXU), lambda p, r: (r, p)),
            pl.BlockSpec((1, nb, 128), pair3),
            pl.BlockSpec((1, nb, 128), pair3),
        ],
        out_shape=[
            jax.ShapeDtypeStruct((rows, cols), BF16),
            jax.ShapeDtypeStruct((pairs, nb, 128), F32),
            jax.ShapeDtypeStruct((pairs, nb, 128), F32),
        ],
        scratch_shapes=[pltpu.VMEM((rb, 128), F32)] * 4 + [pltpu.VMEM((nb, 128), F32)] * 2,
        compiler_params=pltpu.CompilerParams(
            dimension_semantics=("arbitrary", "arbitrary"), vmem_limit_bytes=_vmem_limit(blocks, scratch)),
        name="s5_core",
    )(ut, tt, pt, qt, are, aim, h0re, h0im)


def _s5_fused_kernel(x_ref, g_ref, tt_ref, pt_ref, qt_ref, are_ref, aim_ref, h0re_ref, h0im_ref,
                     y_ref, hre_ref, him_ref,
                     us, ugt, pure, puim, hinre, hinim, ytmp, ys, stre, stim, *, nb, tbt, spb):
    lc = S5_CHUNK
    ch = S5_GROUP_CH
    gps = V7X_LANES // ch
    nch = tbt // lc
    cols = nb * nch
    k = pl.program_id(1)
    ppg = gps // 2
    slab = lambda sl: k * spb + sl

    @pl.when(k == 0)
    def _():
        for b in range(nb):
            u = _rms(x_ref[b], g_ref[...])
            for kk in range(us.shape[0]):
                us[kk, pl.ds(b, tbt, stride=nb), :] = u[:, kk * V7X_LANES:(kk + 1) * V7X_LANES]

    @pl.when(pl.program_id(0) == 0)
    def _():
        for sl in range(spb):
            stre[slab(sl)] = h0re_ref[sl * ppg:(sl + 1) * ppg]
            stim[slab(sl)] = h0im_ref[sl * ppg:(sl + 1) * ppg]

    tile = lambda j, s: slice((j * lc + s) * nb, (j * lc + s + 1) * nb)
    for sl in range(spb):
        for s in range(lc):
            a = jnp.concatenate([us[slab(sl), tile(j, s), :] for j in range(nch)], axis=0).T
            for g in range(gps):
                ugt[sl * gps + g, (lc - 1 - s) * ch:(lc - s) * ch, :] = a[g * ch:(g + 1) * ch, :].astype(BF16)

    half = pt_ref.shape[1] // 2
    for p in range(spb * ppg):
        put0 = _dot(pt_ref[2 * p], ugt[2 * p])
        put1 = _dot(pt_ref[2 * p + 1], ugt[2 * p + 1])
        pure[p] = jnp.concatenate([put0[:half], put1[:half]], axis=0).T
        puim[p] = jnp.concatenate([put0[half:], put1[half:]], axis=0).T

    chains = [(sl, pr) for sl in range(spb) for pr in range(ppg)]
    state = [(stre[slab(sl), pr], stim[slab(sl), pr]) for sl, pr in chains]
    coef = [(jnp.broadcast_to(are_ref[sl * ppg + pr], (nb, 2 * half)),
             jnp.broadcast_to(aim_ref[sl * ppg + pr], (nb, 2 * half))) for sl, pr in chains]
    for j in range(nch):
        rows = slice(j * nb, (j + 1) * nb)
        for c, (sl, pr) in enumerate(chains):
            p = sl * ppg + pr
            re, im = state[c]
            ar, ai = coef[c]
            hinre[p, rows, :] = re
            hinim[p, rows, :] = im
            state[c] = (ar * re - ai * im + pure[p, rows, :], ar * im + ai * re + puim[p, rows, :])
    for c, (sl, pr) in enumerate(chains):
        re, im = state[c]
        stre[slab(sl), pr] = re
        stim[slab(sl), pr] = im
        hre_ref[slab(sl) * ppg + pr] = re
        him_ref[slab(sl) * ppg + pr] = im

    for p in range(spb * ppg):
        hre_t = hinre[p].T
        him_t = hinim[p].T
        for i in range(2):
            g = 2 * p + i
            hin_t = jnp.concatenate([hre_t[i * half:(i + 1) * half], him_t[i * half:(i + 1) * half]],
                                    axis=0).astype(BF16)
            ytmp[g] = _dot(tt_ref[g], ugt[g]) + _dot(qt_ref[g], hin_t)

    for sl in range(spb):
        for t in range(lc):
            zt = jnp.concatenate([ytmp[sl * gps + g, t * ch:(t + 1) * ch, :] for g in range(gps)], axis=0).T
            for j in range(nch):
                ys[sl, tile(j, t), :] = zt[j * nb:(j + 1) * nb, :]
    for sl in range(spb):
        for b in range(nb):
            y_ref[b, :, sl * V7X_LANES:(sl + 1) * V7X_LANES] = ys[sl, pl.ds(b, tbt, stride=nb), :].astype(y_ref.dtype)


def _s5_fused(x, g, mats_t, h0re, h0im, tbt):
    nb, s, d = x.shape
    tt, pt, qt, are, aim = mats_t
    groups = tt.shape[0]
    gps = V7X_LANES // S5_GROUP_CH
    slabs = groups // gps
    w = S5_CHUNK * S5_GROUP_CH
    nst = pt.shape[1]
    cols = nb * (tbt // S5_CHUNK)
    spb = S5_SLABS_PER_STEP
    ng, npair = spb * gps, spb * gps // 2
    kern = functools.partial(_s5_fused_kernel, nb=nb, tbt=tbt, spb=spb)
    slab3 = lambda t, k: (k, 0, 0)
    blocks = (_nbytes((nb, tbt, d), F32) + _nbytes((nb, tbt, spb * V7X_LANES), BF16)
              + ng * (_nbytes((w, w), BF16) + 2 * _nbytes((w, nst), BF16)))
    scratch = (_nbytes((slabs, nb * tbt, V7X_LANES), F32) + _nbytes((ng, w, cols), BF16)
               + 4 * _nbytes((npair, cols, nst), F32) + _nbytes((ng, w, cols), F32)
               + _nbytes((spb, nb * tbt, V7X_LANES), F32))
    return pl.pallas_call(
        kern,
        grid=(s // tbt, slabs // spb),
        in_specs=[
            pl.BlockSpec((nb, tbt, d), lambda t, k: (0, t, 0)),
            pl.BlockSpec((1, d), lambda t, k: (0, 0)),
            pl.BlockSpec((ng, w, w), slab3),
            pl.BlockSpec((ng, nst, w), slab3),
            pl.BlockSpec((ng, w, nst), slab3),
            pl.BlockSpec((npair, 1, nst), slab3),
            pl.BlockSpec((npair, 1, nst), slab3),
            pl.BlockSpec((npair, nb, nst), slab3),
            pl.BlockSpec((npair, nb, nst), slab3),
        ],
        out_specs=[
            pl.BlockSpec((nb, tbt, spb * V7X_LANES), lambda t, k: (0, t, k)),
            pl.BlockSpec((groups // 2, nb, nst), lambda t, k: (0, 0, 0)),
            pl.BlockSpec((groups // 2, nb, nst), lambda t, k: (0, 0, 0)),
        ],
        out_shape=[
            jax.ShapeDtypeStruct((nb, s, d), BF16),
            jax.ShapeDtypeStruct((groups // 2, nb, nst), F32),
            jax.ShapeDtypeStruct((groups // 2, nb, nst), F32),
        ],
        scratch_shapes=[
            pltpu.VMEM((slabs, nb * tbt, V7X_LANES), F32),
            pltpu.VMEM((ng, w, cols), BF16),
            pltpu.VMEM((npair, cols, nst), F32),
            pltpu.VMEM((npair, cols, nst), F32),
            pltpu.VMEM((npair, cols, nst), F32),
            pltpu.VMEM((npair, cols, nst), F32),
            pltpu.VMEM((ng, w, cols), F32),
            pltpu.VMEM((spb, nb * tbt, V7X_LANES), F32),
            pltpu.VMEM((slabs, gps // 2, nb, nst), F32),
            pltpu.VMEM((slabs, gps // 2, nb, nst), F32),
        ],
        compiler_params=pltpu.CompilerParams(
            dimension_semantics=("arbitrary", "arbitrary"), vmem_limit_bytes=_vmem_limit(blocks, scratch)),
        name="s5_fused",
    )(x, g, tt, pt, qt, are, aim, h0re, h0im)


def _s5_matrices(a_re, a_im, log_dt, b_re, b_im, c_re, c_im):
    lc = S5_CHUNK
    groups, n = a_re.shape
    ch = S5_GROUP_CH
    f = lambda v: v.astype(F32)
    a_re, a_im, b_re, b_im, c_re, c_im = map(f, (a_re, a_im, b_re, b_im, c_re, c_im))
    dt = jnp.exp(f(log_dt))[:, None, None]
    j = jnp.arange(lc + 1, dtype=F32)
    mag = jnp.exp((a_re[:, :, None] * dt) * j)
    ang = (a_im[:, :, None] * dt) * j
    pw_re, pw_im = mag * jnp.cos(ang), mag * jnp.sin(ang)
    e_re, e_im = pw_re[:, :, 1] - 1.0, pw_im[:, :, 1]
    inv = 1.0 / (a_re * a_re + a_im * a_im)
    f_re, f_im = (e_re * a_re + e_im * a_im) * inv, (e_im * a_re - e_re * a_im) * inv
    bb_re = f_re[..., None] * b_re - f_im[..., None] * b_im
    bb_im = f_re[..., None] * b_im + f_im[..., None] * b_re
    rep = lambda v: jnp.repeat(v, ch, axis=2)
    til = lambda v: jnp.tile(v, (1, 1, lc + 1))
    lb_re = rep(pw_re) * til(bb_re) - rep(pw_im) * til(bb_im)
    lb_im = rep(pw_re) * til(bb_im) + rep(pw_im) * til(bb_re)
    kern = jnp.einsum('gcm,gmx->gcx', jnp.concatenate([c_re, c_im], axis=2),
                      jnp.concatenate([lb_re, -lb_im], axis=1), precision=HI)
    w = lc * ch
    kfwd = jnp.pad(kern[:, :, :w], ((0, 0), (0, 0), ((lc - 1) * ch, 0)))
    tt = jnp.stack([kfwd[:, :, t * ch:t * ch + w] for t in range(lc)], axis=1).reshape(groups, w, w)
    pt = jnp.concatenate([lb_re[:, :, :w], lb_im[:, :, :w]], axis=1)
    pwt_re = jnp.swapaxes(pw_re, 1, 2)[:, 1:, None, :]
    pwt_im = jnp.swapaxes(pw_im, 1, 2)[:, 1:, None, :]
    ca_re = c_re[:, None] * pwt_re - c_im[:, None] * pwt_im
    ca_im = c_re[:, None] * pwt_im + c_im[:, None] * pwt_re
    qt = jnp.concatenate([ca_re, -ca_im], axis=3).reshape(groups, w, 2 * n)
    are = pw_re[:, :, lc].reshape(groups // 2, 1, 2 * n)
    aim = pw_im[:, :, lc].reshape(groups // 2, 1, 2 * n)
    return tt.astype(BF16), pt.astype(BF16), qt.astype(BF16), are, aim


def _glu_kernel(x_ref, y_ref, g0_ref, d_ref, w_ref, g1_ref, o_ref, z_s, out_s, *, sub):
    tb, d = x_ref.shape
    w = V7X_MXU
    nslab = d // w
    rows = lambda r: slice(r * sub, (r + 1) * sub)
    csl = lambda j: slice(j * w, (j + 1) * w)

    def row_scale(r):
        x = x_ref[rows(r), :]
        return lax.rsqrt(jnp.mean(x * x, axis=-1, keepdims=True) + NORM_EPS)

    def gelu_in(r, rs, j):
        u = x_ref[rows(r), csl(j)] * rs * g0_ref[:, csl(j)]
        yy = y_ref[rows(r), csl(j)].astype(F32) + d_ref[:, csl(j)] * u
        z_s[r % 2, :, csl(j)] = jax.nn.gelu(yy, approximate=True).astype(BF16)

    rs = row_scale(0)
    for j in range(nslab):
        gelu_in(0, rs, j)
    for r in range(tb // sub):
        z = z_s[r % 2]
        more = (r + 1) * sub < tb
        if more:
            rs = row_scale(r + 1)
        pair = lambda j: (_dot(z, w_ref[:, csl(j)]), _dot(z, w_ref[:, d + j * w:d + (j + 1) * w]))
        nxt = pair(0)
        ssq = jnp.zeros((sub, 1), F32)
        for j in range(nslab):
            a, b = nxt
            if j + 1 < nslab:
                nxt = pair(j + 1)
            if more:
                gelu_in(r + 1, rs, j)
            out = a * (0.5 * (1.0 + jnp.tanh(0.5 * b)))
            out_s[rows(r), csl(j)] = out
            ssq = ssq + jnp.sum(out * out, axis=-1, keepdims=True)
        scale = lax.rsqrt(ssq * (1.0 / d) + NORM_EPS)
        o_ref[rows(r), :] = x_ref[rows(r), :] + out_s[rows(r), :] * scale * g1_ref[...]


def _const_spec(shape):
    nd = len(shape)
    return pl.BlockSpec(shape, lambda *_: (0,) * nd, pipeline_mode=pl.Buffered(1))


def _glu(x2d, y2d, g0, dskip, w, g1, tb):
    rows, d = x2d.shape
    blocks = 2 * _nbytes((tb, d), F32) + _nbytes((tb, d), BF16)
    scratch = _nbytes(w.shape, BF16) + 3 * _nbytes((tb, 2 * d), F32)
    row = pl.BlockSpec((tb, d), lambda i: (i, 0))
    sub = min(tb, GLU_ROW_BLOCK)
    return pl.pallas_call(
        functools.partial(_glu_kernel, sub=sub),
        grid=(rows // tb,),
        in_specs=[row, row, _const_spec((1, d)), _const_spec((1, d)), _const_spec(w.shape), _const_spec((1, d))],
        out_specs=row,
        out_shape=jax.ShapeDtypeStruct((rows, d), F32),
        scratch_shapes=[pltpu.VMEM((2, sub, d), BF16), pltpu.VMEM((tb, d), F32)],
        compiler_params=pltpu.CompilerParams(
            dimension_semantics=("arbitrary",), vmem_limit_bytes=_vmem_limit(blocks, scratch)),
        name="s5_glu",
    )(x2d, y2d, g0, dskip, w, g1)


def _ffn_kernel(x_ref, buf_ref, g2_ref, wup_ref, cw_ref, cb_ref, wdown_ref, g3_ref,
                o_ref, nbuf_ref, carry, act, *, tb, sub, dff, width):
    t = pl.program_id(1)
    pad = V7X_SUBLANES

    @pl.when(t == 0)
    def _():
        carry[...] = jnp.zeros_like(carry)
        carry[pad - (width - 1):pad, :] = buf_ref[0]

    fb = V7X_MXU
    nslab = dff // fb
    cols = lambda j: (slice(j * fb, (j + 1) * fb), slice(dff + j * fb, dff + (j + 1) * fb))

    def conv(u, cs):
        prev = carry[:, cs]
        carry[:, cs] = u[sub - pad:sub, :]
        return _causal_dwconv(u, prev, cw_ref, cb_ref, cs)

    def up_phase(r):
        hn = _rms(x_ref[0, r * sub:(r + 1) * sub, :], g2_ref[...]).astype(BF16)
        up = lambda j: tuple(_dot(hn, wup_ref[:, cs]) for cs in cols(j))
        nxt = up(0)
        for j in range(nslab):
            ug, uv = nxt
            if j + 1 < nslab:
                nxt = up(j + 1)
            gate = conv(ug, cols(j)[0])
            val = conv(uv, cols(j)[1])
            act[r % act.shape[0], :, j * fb:(j + 1) * fb] = (jax.nn.gelu(gate, approximate=True) * val).astype(BF16)

    nsub = tb // sub
    up_phase(0)
    for r in range(nsub):
        f = _dot(act[r % act.shape[0]], wdown_ref[...])
        if r + 1 < nsub:
            up_phase(r + 1)
        rows = slice(r * sub, (r + 1) * sub)
        o_ref[0, rows, :] = x_ref[0, rows, :] + _rms(f, g3_ref[...])
    nbuf_ref[0] = carry[pad - (width - 1):pad, :]


def _layer_spec(shape, layer):
    nd = len(shape) - 1
    return pl.BlockSpec((None,) + tuple(shape[1:]), lambda *_: (layer,) + (0,) * nd, pipeline_mode=pl.Buffered(1))


def _ffn(x, buf, g2, wup, cw, cb, wdown, g3, layer, tb):
    b, s, d = x.shape
    dff = wdown.shape[1]
    width = cw.shape[1]
    sub = min(tb, FFN_ROW_BLOCK)
    kern = functools.partial(_ffn_kernel, tb=tb, sub=sub, dff=dff, width=width)
    blocks = 2 * _nbytes((tb, d), F32)
    scratch = (_nbytes(wup.shape[1:], BF16) + _nbytes(wdown.shape[1:], BF16) + 6 * _nbytes((sub, V7X_MXU), F32)
               + _nbytes((8, 2 * dff), F32) + 2 * _nbytes((sub, dff), BF16) + 3 * _nbytes((sub, d), F32))
    xs = pl.BlockSpec((1, tb, d), lambda i, t: (i, t, 0))
    bs = pl.BlockSpec((1, width - 1, 2 * dff), lambda i, t: (i, 0, 0))
    return pl.pallas_call(
        kern,
        grid=(b, s // tb),
        in_specs=[xs, bs, _const_spec((1, d)), _layer_spec(wup.shape, layer), _layer_spec(cw.shape, layer),
                  _layer_spec(cb.shape, layer), _layer_spec(wdown.shape, layer), _const_spec((1, d))],
        out_specs=[xs, bs],
        out_shape=[jax.ShapeDtypeStruct((b, s, d), F32), jax.ShapeDtypeStruct((b, width - 1, 2 * dff), F32)],
        scratch_shapes=[pltpu.VMEM((V7X_SUBLANES, 2 * dff), F32), pltpu.VMEM((min(2, tb // sub), sub, dff), BF16)],
        compiler_params=pltpu.CompilerParams(
            dimension_semantics=("arbitrary", "arbitrary"), vmem_limit_bytes=_vmem_limit(blocks, scratch)),
        name="conv_ffn",
    )(x, buf, g2, wup, cw, cb, wdown, g3)


def _mlstm_weights_kernel(aq_ref, ak_ref, av_ref, gq_ref, gk_ref, gv_ref,
                          wq_ref, wkt_ref, wv_ref, gc_ref, gm_ref, *, kscale):
    w = V7X_MXU
    kb = QKV_BLOCK
    dotp = functools.partial(jnp.dot, preferred_element_type=F32, precision=HI)
    lane = lax.broadcasted_iota(jnp.int32, (V7X_LANES, w), 1)
    src = lax.broadcasted_iota(jnp.int32, (V7X_LANES, w), 0)
    spread = jnp.where(jnp.bitwise_and(lane, kb - 1) == src, 1.0, 0.0)
    shift = kb.bit_length() - 1
    rblk = lax.shift_right_logical(lax.broadcasted_iota(jnp.int32, (w, w), 0), shift)
    cblk = lax.shift_right_logical(lax.broadcasted_iota(jnp.int32, (w, w), 1), shift)
    tile = lambda a_ref: jnp.where(rblk == cblk, dotp(a_ref[0], spread), 0.0)
    tq, tk, tv = tile(aq_ref), tile(ak_ref), tile(av_ref)
    wq_ref[0] = tq.astype(BF16)
    wkt_ref[0] = (tk.T * kscale).astype(BF16)
    wv_ref[0] = tv.astype(BF16)
    gc_ref[...] = (dotp(tq, gq_ref[...]) + dotp(tk, gk_ref[...])).astype(BF16)
    gm_ref[...] = dotp(tv, gv_ref[...]).astype(BF16)


def _mlstm_weights(wq, wk, wv, w_gate, kscale):
    nblk, kb, _ = wq.shape
    assert kb == QKV_BLOCK and kb & (kb - 1) == 0
    w = V7X_MXU
    per = w // kb
    nt = nblk // per
    inner = nt * w
    compact = lambda m: jnp.pad(m.astype(F32).reshape(nt, w, kb), ((0, 0), (0, 0), (0, V7X_LANES - kb)))
    wg = jnp.pad(w_gate.astype(F32), ((0, 0), (0, V7X_LANES - w_gate.shape[1])))
    cm = pl.BlockSpec((1, w, V7X_LANES), lambda i: (i, 0, 0))
    gs = [pl.BlockSpec((w, V7X_LANES), lambda i, k=k: (k * nt + i, 0)) for k in range(3)]
    tile = pl.BlockSpec((1, w, w), lambda i: (i, 0, 0))
    fold = pl.BlockSpec((w, V7X_LANES), lambda i: (i, 0))
    return pl.pallas_call(
        functools.partial(_mlstm_weights_kernel, kscale=kscale),
        grid=(nt,),
        in_specs=[cm, cm, cm] + gs,
        out_specs=[tile, tile, tile, fold, fold],
        out_shape=[jax.ShapeDtypeStruct((nt, w, w), BF16)] * 3 + [jax.ShapeDtypeStruct((inner, V7X_LANES), BF16)] * 2,
        name="mlstm_weights",
    )(compact(wq), compact(wk), compact(wv), wg, wg, wg)


def _mpre_kernel(x_ref, cbuf_ref, g_ref, wup_ref, cw_ref, cb_ref, wq_ref, wkt_ref, wv_ref, gc_ref, gm_ref,
                 bg_ref, q_ref, kt_ref, v_ref, xc_ref, z_ref, gates_ref, nbuf_ref, carry,
                 *, tb, inner, width):
    t = pl.program_id(1)
    pad = V7X_SUBLANES

    @pl.when(t == 0)
    def _():
        carry[...] = jnp.zeros_like(carry)
        carry[pad - (width - 1):pad, :] = cbuf_ref[0]

    h = _rms(x_ref[0], g_ref[...]).astype(BF16)
    w = V7X_MXU
    gacc = jnp.zeros((tb, bg_ref.shape[1]), F32) + bg_ref[...]
    for i in range(inner // w):
        cs = slice(i * w, (i + 1) * w)
        xm = _dot(h, wup_ref[:, cs])
        prev = carry[:, cs]
        carry[:, cs] = xm[tb - pad:tb, :]
        xc = _silu(_causal_dwconv(xm, prev, cw_ref, cb_ref, cs)).astype(BF16)
        xmb = xm.astype(BF16)
        q_ref[0, :, cs] = _dot(xc, wq_ref[i]).astype(BF16)
        kt_ref[0, cs, :] = _dot_nt(wkt_ref[i], xc).astype(BF16)
        v_ref[0, :, cs] = _dot(xmb, wv_ref[i]).astype(BF16)
        xc_ref[0, :, cs] = xc
        z_ref[0, :, cs] = _silu(_dot(h, wup_ref[:, inner + i * w:inner + (i + 1) * w])).astype(BF16)
        gacc = gacc + _dot(xc, gc_ref[cs, :]) + _dot(xmb, gm_ref[cs, :])
    gates_ref[0] = gacc
    nbuf_ref[0] = carry[pad - (width - 1):pad, :]


def _mpre(x, cbuf, g, wup, cw, cb, wq_t, wkt_t, wv_t, gc, gm, bg, tb):
    b, s, d = x.shape
    inner = cw.shape[1]
    width = cw.shape[0]
    ng = bg.shape[1]
    kern = functools.partial(_mpre_kernel, tb=tb, inner=inner, width=width)
    act = pl.BlockSpec((1, tb, inner), lambda i, t: (i, t, 0))
    blocks = _nbytes((tb, d), F32) + 5 * _nbytes((tb, inner), BF16)
    scratch = (_nbytes(wup.shape, BF16) + 3 * _nbytes(wq_t.shape, BF16) + 2 * _nbytes((inner, 128), BF16)
               + _nbytes((8, inner), F32) + 8 * _nbytes((tb, V7X_MXU), F32))
    return pl.pallas_call(
        kern,
        grid=(b, s // tb),
        in_specs=[pl.BlockSpec((1, tb, d), lambda i, t: (i, t, 0)),
                  pl.BlockSpec((1, width - 1, inner), lambda i, t: (i, 0, 0)),
                  _const_spec((1, d)), _const_spec(wup.shape), _const_spec(cw.shape), _const_spec((1, inner)),
                  _const_spec(wq_t.shape), _const_spec(wkt_t.shape), _const_spec(wv_t.shape),
                  _const_spec(gc.shape), _const_spec(gm.shape), _const_spec((1, ng))],
        out_specs=[act, pl.BlockSpec((1, inner, tb), lambda i, t: (i, 0, t)), act, act, act,
                   pl.BlockSpec((1, tb, ng), lambda i, t: (i, t, 0)),
                   pl.BlockSpec((1, width - 1, inner), lambda i, t: (i, 0, 0))],
        out_shape=[jax.ShapeDtypeStruct((b, s, inner), BF16), jax.ShapeDtypeStruct((b, inner, s), BF16),
                   jax.ShapeDtypeStruct((b, s, inner), BF16), jax.ShapeDtypeStruct((b, s, inner), BF16),
                   jax.ShapeDtypeStruct((b, s, inner), BF16), jax.ShapeDtypeStruct((b, s, ng), F32),
                   jax.ShapeDtypeStruct((b, width - 1, inner), F32)],
        scratch_shapes=[pltpu.VMEM((V7X_SUBLANES, inner), F32)],
        compiler_params=pltpu.CompilerParams(
            dimension_semantics=("arbitrary", "arbitrary"), vmem_limit_bytes=_vmem_limit(blocks, scratch)),
        name="mlstm_pre",
    )(x, cbuf, g, wup, cw, cb, wq_t, wkt_t, wv_t, gc, gm, bg)


def _mlstm_chunk(q_ref, kt_ref, v_ref, g, gt, ct_s, n_s, m_s, fill, emit_hn, fill_update, *, lc, heads, dh):
    row = lax.broadcasted_iota(jnp.int32, (lc, lc), 0)
    col = lax.broadcasted_iota(jnp.int32, (lc, lc), 1)
    causal = col <= row
    tri_l = jnp.where(causal, 1.0, 0.0).astype(BF16)
    tri_u = jnp.where(row <= col, 1.0, 0.0).astype(BF16)
    b_cols = sum(_dot(tri_l, part) for part in _split3(_log_sigmoid(g)))
    b_rows = sum(_dot(part, tri_u) for part in _split3(_log_sigmoid(gt)))

    hsl = [slice(h * dh, (h + 1) * dh) for h in range(heads)]
    st = []
    for h in range(heads):
        b_col = b_cols[:, heads + h:heads + h + 1]
        b_row = b_rows[heads + h:heads + h + 1, :]
        i_row = gt[h:h + 1, :]
        m_prev = m_s[h:h + 1, 0:1]
        dmat = jnp.where(causal, b_col - b_row + i_row, -jnp.inf)
        inter = b_col + m_prev
        m_t = jnp.maximum(inter, jnp.max(dmat, axis=1, keepdims=True))
        st.append(dict(b_col=b_col, b_row=b_row, i_row=i_row, m_prev=m_prev, m_t=m_t,
                       wts=jnp.exp(dmat - m_t), a=jnp.exp(inter - m_t)))
        fill()
    for h in range(heads):
        s = st[h]
        s['sc'] = _dot(q_ref[0, :, hsl[h]], kt_ref[0, hsl[h], :]) * s['wts']
    out_mm = lambda h: (_dot(st[h]['sc'].astype(BF16), v_ref[0, :, hsl[h]]),
                        _dot(q_ref[0, :, hsl[h]], ct_s[h].astype(BF16)))
    nxt = out_mm(0)
    for h in range(heads):
        intra, inter_mm = nxt
        if h + 1 < heads:
            nxt = out_mm(h + 1)
        fill()
        s = st[h]
        qh = q_ref[0, :, hsl[h]]
        sc = s['sc']
        a = s['a']
        num = intra + a * inter_mm
        qn = jnp.sum(qh.astype(F32) * n_s[h:h + 1, :], axis=1, keepdims=True)
        den = jnp.sum(sc, axis=1, keepdims=True) + a * qn
        hh = num / jnp.maximum(jnp.abs(den), jnp.exp(-s['m_t']))
        mu = jnp.mean(hh, axis=1, keepdims=True)
        dev = hh - mu
        var = jnp.mean(dev * dev, axis=1, keepdims=True)
        emit_hn(h, dev * lax.rsqrt(var + LN_EPS))
    def update_mm(h):
        s = st[h]
        kth = kt_ref[0, hsl[h], :]
        b_last = s['b_row'][:, lc - 1:lc]
        g_row = b_last - s['b_row'] + s['i_row']
        m_new = jnp.maximum(b_last + s['m_prev'], jnp.max(g_row, axis=1, keepdims=True))
        decay = jnp.exp(b_last + s['m_prev'] - m_new)
        wg_row = jnp.exp(g_row - m_new)
        wg_col = jnp.exp(b_last - s['b_col'] + g[:, h:h + 1] - m_new)
        wv = (v_ref[0, :, hsl[h]].astype(F32) * wg_col).astype(BF16)
        wg16 = jnp.broadcast_to(wg_row, (16, lc)).astype(BF16)
        return m_new, decay, _dot(kth, wv), _dot_nt(wg16, kth)[0:1, :]

    fill_update(0)
    nxt = update_mm(0)
    for h in range(heads):
        m_new, decay, c_upd, n_upd = nxt
        if h + 1 < heads:
            fill_update(h + 1)
            nxt = update_mm(h + 1)
        ct_s[h] = decay * ct_s[h] + c_upd
        n_s[h:h + 1, :] = decay * n_s[h:h + 1, :] + n_upd
        m_s[h:h + 1, :] = jnp.broadcast_to(m_new, (1, V7X_LANES))


def _mlstm_layer_kernel(*refs, lc, heads, inner, width, zero_init):
    (x_ref, cbuf_ref, g0_ref, wup_ref, cw_ref, cb_ref, wq_ref, wkt_ref, wv_ref, gc_ref, gm_ref, bg_ref,
     ng_ref, sk_ref, wdown_ref, g1_ref) = refs[:16]
    n_in = 16
    if not zero_init:
        c0_ref, n0_ref, m0_ref = refs[16:19]
        n_in = 19
    o_ref, cout_ref, nout_ref, mout_ref, nbuf_ref = refs[n_in:n_in + 5]
    carry, q_s, kt_s, v_s, xc_s, sz_s, act_s, ct_s, n_s, m_s = refs[n_in + 5:]
    dh = inner // heads
    pad = V7X_SUBLANES
    c = pl.program_id(1)

    @pl.when(c == 0)
    def _():
        carry[...] = jnp.zeros_like(carry)
        carry[pad - (width - 1):pad, :] = cbuf_ref[0]
        if zero_init:
            ct_s[...] = jnp.zeros_like(ct_s)
            n_s[...] = jnp.zeros_like(n_s)
            m_s[...] = jnp.zeros_like(m_s)
        else:
            for h in range(heads):
                ct_s[h] = c0_ref[0, h].T
            n_s[...] = n0_ref[0]
            m_s[...] = m0_ref[0]

    x = x_ref[0]
    hx = _rms(x, g0_ref[...]).astype(BF16)
    w = V7X_MXU
    gacc = jnp.zeros((lc, bg_ref.shape[1]), F32) + bg_ref[...]
    nslab = inner // w
    col = lambda i: slice(i * w, (i + 1) * w)

    def project(i, xc, xmb, gacc):
        q_s[0, :, col(i)] = _dot(xc, wq_ref[i]).astype(BF16)
        kt_s[0, col(i), :] = _dot_nt(wkt_ref[i], xc).astype(BF16)
        v_s[0, :, col(i)] = _dot(xmb, wv_ref[i]).astype(BF16)
        xc_s[:, col(i)] = xc
        return gacc + _dot(xc, gc_ref[col(i), :]) + _dot(xmb, gm_ref[col(i), :])

    xm_next = _dot(hx, wup_ref[:, col(0)])
    pending = None
    for i in range(nslab):
        xm = xm_next
        if i + 1 < nslab:
            xm_next = _dot(hx, wup_ref[:, col(i + 1)])
        if pending is not None:
            gacc = project(*pending, gacc)
        prev = carry[:, col(i)]
        carry[:, col(i)] = xm[lc - pad:lc, :]
        xc = _silu(_causal_dwconv(xm, prev, cw_ref, cb_ref, col(i))).astype(BF16)
        pending = (i, xc, xm.astype(BF16))
    gacc = project(*pending, gacc)
    nbuf_ref[0] = carry[pad - (width - 1):pad, :]

    slabs = iter(range(inner // w))

    def gate_slab():
        i = next(slabs, None)
        if i is not None:
            cs = slice(i * w, (i + 1) * w)
            sz_s[:, cs] = _silu(_dot(hx, wup_ref[:, inner + i * w:inner + (i + 1) * w])).astype(BF16)

    hcols = lambda h: slice(h * dh, (h + 1) * dh)
    down = []

    def gated_out(h, hn):
        hs = hn * ng_ref[:, hcols(h)] + sk_ref[:, hcols(h)] * xc_s[:, hcols(h)].astype(F32)
        act_s[:, hcols(h)] = (hs * sz_s[:, hcols(h)].astype(F32)).astype(BF16)

    def down_proj(h):
        if h == 0:
            down.append(_dot(act_s[...], wdown_ref[...]))

    _mlstm_chunk(q_s, kt_s, v_s, gacc, gacc.T[0:2 * heads, :], ct_s, n_s, m_s, gate_slab, gated_out, down_proj,
                 lc=lc, heads=heads, dh=dh)
    for _ in slabs:
        raise AssertionError("fewer fill points than output-gate slabs")
    o_ref[0] = x + _rms(down[0], g1_ref[...])

    @pl.when(c == pl.num_programs(1) - 1)
    def _():
        for h in range(heads):
            cout_ref[0, h] = ct_s[h].T
        nout_ref[0] = n_s[...]
        mout_ref[0] = m_s[...]


def _mlstm_layer(x, cbuf, state, p, g0, g1, lc):
    b, s, d = x.shape
    heads = p['heads']
    inner = p['m_cw'].shape[1]
    width = p['m_cw'].shape[0]
    dh = inner // heads
    zero_init = state is None
    kern = functools.partial(_mlstm_layer_kernel, lc=lc, heads=heads, inner=inner, width=width,
                             zero_init=zero_init)
    xs = pl.BlockSpec((1, lc, d), lambda i, c: (i, c, 0))
    bufs = pl.BlockSpec((1, width - 1, inner), lambda i, c: (i, 0, 0))
    cs = pl.BlockSpec((1, heads, dh, dh), lambda i, c: (i, 0, 0, 0))
    ns = pl.BlockSpec((1, heads, dh), lambda i, c: (i, 0, 0))
    ms = pl.BlockSpec((1, heads, V7X_LANES), lambda i, c: (i, 0, 0))
    consts = [g0, p['m_wup'], p['m_cw'], p['m_cb'], p['m_wq'], p['m_wkt'], p['m_wv'], p['m_gc'], p['m_gm'],
              p['m_bg'], p['m_ng'], p['m_skip'], p['m_wdown'], g1]
    in_specs = [xs, bufs] + [_const_spec(a.shape) for a in consts]
    args = [x, cbuf] + consts
    if not zero_init:
        in_specs += [cs, ns, ms]
        args += list(state)
    blocks = 2 * _nbytes((lc, d), F32) + (1 if zero_init else 2) * _nbytes((heads, dh, dh), F32)
    scratch = (sum(_nbytes(a.shape, a.dtype) for a in consts) + 6 * _nbytes((lc, inner), BF16)
               + _nbytes((heads, dh, dh), F32) + 8 * _nbytes((lc, max(lc, dh)), F32) + 2 * _nbytes((dh, dh), F32))
    return pl.pallas_call(
        kern,
        grid=(b, s // lc),
        in_specs=in_specs,
        out_specs=[xs, cs, ns, ms, bufs],
        out_shape=[jax.ShapeDtypeStruct((b, s, d), F32), jax.ShapeDtypeStruct((b, heads, dh, dh), F32),
                   jax.ShapeDtypeStruct((b, heads, dh), F32), jax.ShapeDtypeStruct((b, heads, V7X_LANES), F32),
                   jax.ShapeDtypeStruct((b, width - 1, inner), F32)],
        scratch_shapes=[pltpu.VMEM((V7X_SUBLANES, inner), F32),
                        pltpu.VMEM((1, lc, inner), BF16), pltpu.VMEM((1, inner, lc), BF16),
                        pltpu.VMEM((1, lc, inner), BF16), pltpu.VMEM((lc, inner), BF16),
                        pltpu.VMEM((lc, inner), BF16), pltpu.VMEM((lc, inner), BF16),
                        pltpu.VMEM((heads, dh, dh), F32), pltpu.VMEM((heads, dh), F32),
                        pltpu.VMEM((heads, V7X_LANES), F32)],
        compiler_params=pltpu.CompilerParams(
            dimension_semantics=("arbitrary", "arbitrary"), vmem_limit_bytes=_vmem_limit(blocks, scratch)),
        name="mlstm_layer",
    )(*args)


def _prep_weights(w):
    row = lambda v: v.astype(F32).reshape(1, -1)
    p = {}
    p['norm_g'] = w['norm_g'].astype(F32)
    p['s5_mats'] = _s5_matrices(w['s5_A_re'][0], w['s5_A_im'][0], w['s5_log_dt'][0], w['s5_B_re'][0],
                                w['s5_B_im'][0], w['s5_C_re'][0], w['s5_C_im'][0])
    p['s5_D'] = row(w['s5_D'][0])
    p['s5_w_glu'] = w['s5_w_glu'][0].astype(BF16)
    inner = w['mlstm_conv_w'].shape[2]
    heads = w['mlstm_b_gate'].shape[1] // 2
    dh = inner // heads
    p['m_wq'], p['m_wkt'], p['m_wv'], p['m_gc'], p['m_gm'] = _mlstm_weights(
        w['mlstm_wq'][0], w['mlstm_wk'][0], w['mlstm_wv'][0], w['mlstm_w_gate'][0], dh ** -0.5)
    bg = row(w['mlstm_b_gate'][0])
    p['m_bg'] = jnp.pad(bg, ((0, 0), (0, V7X_LANES - bg.shape[1])))
    p['m_wup'] = w['mlstm_w_up'][0].astype(BF16)
    p['m_cw'] = w['mlstm_conv_w'][0].astype(F32)
    p['m_cb'] = row(w['mlstm_conv_b'][0])
    p['m_ng'] = row(w['mlstm_norm_g'][0])
    p['m_skip'] = row(w['mlstm_skip'][0])
    p['m_wdown'] = w['mlstm_w_down'][0].astype(BF16)
    p['heads'] = heads
    p['f_wup'] = w['ffn_w_up'].astype(BF16)
    p['f_cw'] = w['ffn_conv_w'].astype(F32)
    p['f_cb'] = w['ffn_conv_b'].astype(F32)[:, None, :]
    p['f_wdown'] = w['ffn_w_down'].astype(BF16)
    return p


def _to_chunk_major(u2d, b, s):
    d = u2d.shape[1]
    g = d // S5_GROUP_CH
    u = u2d.reshape(b, s // S5_CHUNK, S5_CHUNK, g, S5_GROUP_CH)[:, :, ::-1]
    return jnp.transpose(u, (1, 0, 3, 2, 4)).reshape((s // S5_CHUNK) * b, d * S5_CHUNK)


def _to_token_major(yt, b, s):
    d = yt.shape[1] // S5_CHUNK
    g = d // S5_GROUP_CH
    y = yt.reshape(s // S5_CHUNK, b, g, S5_CHUNK, S5_GROUP_CH)
    return jnp.transpose(y, (1, 0, 3, 2, 4)).reshape(b * s, d)


def _trunk(x, st, p, tiles):
    b, s, d = x.shape
    ng = p['norm_g']
    g_of = lambda layer, k: ng[layer, k].reshape(1, d)
    zero_init = st is None
    groups = d // S5_GROUP_CH
    x2d = x.reshape(b * s, d)

    n_state = p['s5_mats'][1].shape[1] // 2
    if zero_init:
        h0re = h0im = jnp.zeros((groups // 2, b, 2 * n_state), F32)
    else:
        pair = lambda h: jnp.transpose(h.astype(F32).reshape(b, groups // 2, 2 * n_state), (1, 0, 2))
        h0re, h0im = pair(st['s5_re']), pair(st['s5_im'])
    if tiles.s5_time:
        y, hre, him = _s5_fused(x, g_of(0, 0), p['s5_mats'], h0re, h0im, tiles.s5_time)
        y2d = y.reshape(b * s, d)
    else:
        u = _norm_cast(x2d, g_of(0, 0), tiles.rows)
        yt, hre, him = _s5_core(_to_chunk_major(u, b, s), p['s5_mats'], h0re, h0im, b, tiles.s5_rows)
        y2d = _to_token_major(yt, b, s)
    unpair = lambda h: jnp.transpose(h, (1, 0, 2)).reshape(1, b, groups, n_state)
    o_re, o_im = unpair(hre), unpair(him)
    x2d = _glu(x2d, y2d, g_of(0, 0), p['s5_D'], p['s5_w_glu'], g_of(0, 1), tiles.rows)

    def ffn(x2d, layer):
        dff2 = p['f_wup'].shape[2]
        width = p['f_cw'].shape[1]
        buf = jnp.zeros((b, width - 1, dff2), F32) if zero_init else st['ffn_conv'][layer].astype(F32)
        xo, nbuf = _ffn(x2d.reshape(b, s, d), buf, g_of(layer, 2), p['f_wup'], p['f_cw'], p['f_cb'],
                        p['f_wdown'], g_of(layer, 3), layer, tiles.ffn_time)
        return xo.reshape(b * s, d), nbuf

    x2d, fbuf0 = ffn(x2d, 0)

    heads = p['heads']
    inner = p['m_cw'].shape[1]
    width = p['m_cw'].shape[0]
    cbuf = jnp.zeros((b, width - 1, inner), F32) if zero_init else st['mlstm_conv'].astype(F32)
    if zero_init:
        state = None
    else:
        state = (st['mlstm_C'].astype(F32), st['mlstm_n'].astype(F32),
                 jnp.broadcast_to(st['mlstm_m'].astype(F32)[:, :, None], (b, heads, V7X_LANES)))
    x3, c_out, n_out, m_out, ncbuf = _mlstm_layer(x2d.reshape(b, s, d), cbuf, state, p, g_of(1, 0), g_of(1, 1),
                                                  tiles.mlstm_chunk)
    x2d, fbuf1 = ffn(x3.reshape(b * s, d), 1)

    return (x2d.reshape(b, s, d), o_re, o_im, c_out[None], n_out[None], m_out[None, :, :, 0], ncbuf[None],
            jnp.stack([fbuf0, fbuf1], axis=0))


def kernel(x_prompt, x_sample, state_s5_re, state_s5_im, state_mlstm_C, state_mlstm_n, state_mlstm_m, state_mlstm_conv, state_ffn_conv, norm_g, s5_A_re, s5_A_im, s5_log_dt, s5_B_re, s5_B_im, s5_C_re, s5_C_im, s5_D, s5_w_glu, mlstm_w_up, mlstm_conv_w, mlstm_conv_b, mlstm_wq, mlstm_wk, mlstm_wv, mlstm_w_gate, mlstm_b_gate, mlstm_norm_g, mlstm_skip, mlstm_w_down, ffn_w_up, ffn_conv_w, ffn_conv_b, ffn_w_down):
    w = {
        'norm_g': norm_g, 's5_A_re': s5_A_re, 's5_A_im': s5_A_im, 's5_log_dt': s5_log_dt,
        's5_B_re': s5_B_re, 's5_B_im': s5_B_im, 's5_C_re': s5_C_re, 's5_C_im': s5_C_im, 's5_D': s5_D,
        's5_w_glu': s5_w_glu, 'mlstm_w_up': mlstm_w_up, 'mlstm_conv_w': mlstm_conv_w,
        'mlstm_conv_b': mlstm_conv_b, 'mlstm_wq': mlstm_wq, 'mlstm_wk': mlstm_wk, 'mlstm_wv': mlstm_wv,
        'mlstm_w_gate': mlstm_w_gate, 'mlstm_b_gate': mlstm_b_gate, 'mlstm_norm_g': mlstm_norm_g,
        'mlstm_skip': mlstm_skip, 'mlstm_w_down': mlstm_w_down, 'ffn_w_up': ffn_w_up,
        'ffn_conv_w': ffn_conv_w, 'ffn_conv_b': ffn_conv_b, 'ffn_w_down': ffn_w_down,
    }
    assert norm_g.shape[0] == 2 and s5_A_re.shape[0] == 1 and mlstm_w_up.shape[0] == 1
    assert x_prompt.shape[0] == V7X_SUBLANES and x_sample.shape[0] == V7X_SUBLANES
    p = _prep_weights(w)
    out_p = _trunk(x_prompt.astype(F32), None, p, _pick_tiles(*x_prompt.shape[:2]))
    st = {'s5_re': state_s5_re[0], 's5_im': state_s5_im[0], 'mlstm_C': state_mlstm_C[0],
          'mlstm_n': state_mlstm_n[0], 'mlstm_m': state_mlstm_m[0], 'mlstm_conv': state_mlstm_conv[0],
          'ffn_conv': state_ffn_conv}
    out_s = _trunk(x_sample.astype(F32), st, p, _pick_tiles(*x_sample.shape[:2]))
    return (out_p[0], out_s[0]) + tuple(out_p[1:]) + tuple(out_s[1:])
```

```python
import functools
import math
from typing import NamedTuple

import jax
import jax.numpy as jnp
from jax import lax
from jax.experimental import pallas as pl
from jax.experimental.pallas import tpu as pltpu

F32 = jnp.float32
BF16 = jnp.bfloat16
HI = lax.Precision.HIGHEST

NORM_EPS = 1e-6
LN_EPS = 1e-5
S5_GROUP_CH = 16
S5_CHUNK = 16
QKV_BLOCK = 4
S5_SLABS_PER_STEP = 2
FFN_ROW_BLOCK = 1024

V7X_LANES = 128
V7X_SUBLANES = 8
V7X_MXU = 256
V7X_VMEM_BYTES = 64 * 1024 * 1024


class Tiles(NamedTuple):
    rows: int
    time: int
    s5_rows: int
    mlstm_chunk: int
    s5_time: int
    ffn_time: int


def _pick_tiles(batch, seq):
    s5_time = (V7X_LANES // batch) * S5_CHUNK
    return Tiles(rows=min(batch * seq, 512), time=min(seq, 512),
                 s5_rows=min(batch * (seq // S5_CHUNK), 512), mlstm_chunk=min(seq, 256),
                 s5_time=s5_time if seq % s5_time == 0 else 0, ffn_time=min(seq, 1024))


def _vmem_limit(block_bytes, scratch_bytes):
    want = 2 * block_bytes + scratch_bytes + 16 * 1024 * 1024
    return int(min(want, V7X_VMEM_BYTES - 8 * 1024 * 1024))


def _nbytes(shape, dtype):
    return math.prod(shape) * jnp.dtype(dtype).itemsize


def _rms(x, g):
    return x * lax.rsqrt(jnp.mean(x * x, axis=-1, keepdims=True) + NORM_EPS) * g


def _dot(a, b):
    return jnp.dot(a, b, preferred_element_type=F32)


def _dot_nt(a, b):
    return lax.dot_general(a, b, (((1,), (1,)), ((), ())), preferred_element_type=F32)


def _silu(x):
    return (0.5 * x) * (1.0 + jnp.tanh(0.5 * x))


def _causal_dwconv(u, prev, cw_ref, cb_ref, cs):
    width = cw_ref.shape[0]
    acc = cb_ref[:, cs] + u * cw_ref[width - 1:width, cs]
    rows = lax.broadcasted_iota(jnp.int32, prev.shape, 0)
    for j in range(width - 1):
        d = width - 1 - j
        rolled = pltpu.roll(u, d, 0)
        head = jnp.where(rows < d, pltpu.roll(prev, d, 0), rolled[0:V7X_SUBLANES])
        shifted = jnp.concatenate([head, rolled[V7X_SUBLANES:]], axis=0)
        acc = acc + shifted * cw_ref[j:j + 1, cs]
    return acc


def _log_sigmoid(x):
    return jnp.minimum(x, 0.0) - jnp.log1p(jnp.exp(-jnp.abs(x)))


def _split3(x):
    hi = x.astype(BF16)
    r1 = x - hi.astype(F32)
    mid = r1.astype(BF16)
    lo = (r1 - mid.astype(F32)).astype(BF16)
    return hi, mid, lo


def _norm_cast_kernel(x_ref, g_ref, o_ref):
    o_ref[...] = _rms(x_ref[...], g_ref[...]).astype(o_ref.dtype)


def _norm_cast(x2d, g, tb):
    rows, d = x2d.shape
    return pl.pallas_call(
        _norm_cast_kernel,
        grid=(rows // tb,),
        in_specs=[pl.BlockSpec((tb, d), lambda i: (i, 0)), pl.BlockSpec((1, d), lambda i: (0, 0))],
        out_specs=pl.BlockSpec((tb, d), lambda i: (i, 0)),
        out_shape=jax.ShapeDtypeStruct((rows, d), BF16),
        name="s5_norm",
    )(x2d, g)


def _s5_kernel(u_ref, tt_ref, pt_ref, qt_ref, are_ref, aim_ref, h0re_ref, h0im_ref,
               y_ref, hre_ref, him_ref, pure_s, puim_s, hinre_s, hinim_s, stre_s, stim_s, *, nb, rb):
    r = pl.program_id(1)

    @pl.when(r == 0)
    def _():
        stre_s[...] = h0re_ref[0]
        stim_s[...] = h0im_ref[0]

    u = u_ref[...]
    w = V7X_MXU
    half = pt_ref.shape[1] // 2
    pu = [_dot_nt(u[:, i * w:(i + 1) * w], pt_ref[i]) for i in range(2)]
    pure_s[...] = jnp.concatenate([pu[0][:, :half], pu[1][:, :half]], axis=1)
    puim_s[...] = jnp.concatenate([pu[0][:, half:], pu[1][:, half:]], axis=1)
    ar = jnp.broadcast_to(are_ref[0], (nb, 2 * half))
    ai = jnp.broadcast_to(aim_ref[0], (nb, 2 * half))

    def step(i, carry):
        re, im = carry
        rows = pl.ds(pl.multiple_of(i * nb, nb), nb)
        hinre_s[rows, :] = re
        hinim_s[rows, :] = im
        return (ar * re - ai * im + pure_s[rows, :], ar * im + ai * re + puim_s[rows, :])

    re, im = lax.fori_loop(0, rb // nb, step, (stre_s[...], stim_s[...]))
    stre_s[...] = re
    stim_s[...] = im
    hre_ref[0] = re
    him_ref[0] = im

    hre = hinre_s[...]
    him = hinim_s[...]
    for i in range(2):
        hin = jnp.concatenate([hre[:, i * half:(i + 1) * half], him[:, i * half:(i + 1) * half]], axis=1)
        y_ref[:, i * w:(i + 1) * w] = (_dot_nt(u[:, i * w:(i + 1) * w], tt_ref[i])
                                       + _dot_nt(hin.astype(BF16), qt_ref[i])).astype(y_ref.dtype)


def _s5_core(ut, mats, h0re, h0im, nb, rb):
    rows, cols = ut.shape
    pairs = cols // (2 * V7X_MXU)
    tt, pt, qt, are, aim = mats
    nst = pt.shape[1]
    pair3 = lambda p, r: (p, 0, 0)
    kern = functools.partial(_s5_kernel, nb=nb, rb=rb)
    blocks = (_nbytes((rb, 512), BF16) * 2 + _nbytes((2, 256, 256), BF16) + 4 * _nbytes((256, 128), BF16))
    scratch = 4 * _nbytes((rb, 128), F32) + 2 * _nbytes((nb, 128), F32)
    return pl.pallas_call(
        kern,
        grid=(pairs, rows // rb),
        in_specs=[
            pl.BlockSpec((rb, 2 * V7X_MXU), lambda p, r: (r, p)),
            pl.BlockSpec((2, V7X_MXU, V7X_MXU), pair3),
            pl.BlockSpec((2, nst, V7X_MXU), pair3),
            pl.BlockSpec((2, V7X_MXU, nst), pair3),
            pl.BlockSpec((1, 1, 128), pair3),
            pl.BlockSpec((1, 1, 128), pair3),
            pl.BlockSpec((1, nb, 128), pair3),
            pl.BlockSpec((1, nb, 128), pair3),
        ],
        out_specs=[
            pl.BlockSpec((rb, 2 * V7X_MXU), lambda p, r: (r, p)),
            pl.BlockSpec((1, nb, 128), pair3),
            pl.BlockSpec((1, nb, 128), pair3),
        ],
        out_shape=[
            jax.ShapeDtypeStruct((rows, cols), BF16),
            jax.ShapeDtypeStruct((pairs, nb, 128), F32),
            jax.ShapeDtypeStruct((pairs, nb, 128), F32),
        ],
        scratch_shapes=[pltpu.VMEM((rb, 128), F32)] * 4 + [pltpu.VMEM((nb, 128), F32)] * 2,
        compiler_params=pltpu.CompilerParams(
            dimension_semantics=("arbitrary", "arbitrary"), vmem_limit_bytes=_vmem_limit(blocks, scratch)),
        name="s5_core",
    )(ut, tt, pt, qt, are, aim, h0re, h0im)


def _s5_fused_kernel(x_ref, g_ref, tt_ref, pt_ref, qt_ref, are_ref, aim_ref, h0re_ref, h0im_ref,
                     y_ref, hre_ref, him_ref,
                     us, ugt, pure, puim, hinre, hinim, ytmp, ys, stre, stim, *, nb, tbt, spb):
    lc = S5_CHUNK
    ch = S5_GROUP_CH
    gps = V7X_LANES // ch
    nch = tbt // lc
    cols = nb * nch
    k = pl.program_id(1)
    ppg = gps // 2
    slab = lambda sl: k * spb + sl

    @pl.when(k == 0)
    def _():
        for b in range(nb):
            u = _rms(x_ref[b], g_ref[...])
            for kk in range(us.shape[0]):
                us[kk, pl.ds(b, tbt, stride=nb), :] = u[:, kk * V7X_LANES:(kk + 1) * V7X_LANES]

    @pl.when(pl.program_id(0) == 0)
    def _():
        for sl in range(spb):
            stre[slab(sl)] = h0re_ref[sl * ppg:(sl + 1) * ppg]
            stim[slab(sl)] = h0im_ref[sl * ppg:(sl + 1) * ppg]

    tile = lambda j, s: slice((j * lc + s) * nb, (j * lc + s + 1) * nb)
    for sl in range(spb):
        for s in range(lc):
            a = jnp.concatenate([us[slab(sl), tile(j, s), :] for j in range(nch)], axis=0).T
            for g in range(gps):
                ugt[sl * gps + g, (lc - 1 - s) * ch:(lc - s) * ch, :] = a[g * ch:(g + 1) * ch, :].astype(BF16)

    half = pt_ref.shape[1] // 2
    for p in range(spb * ppg):
        put0 = _dot(pt_ref[2 * p], ugt[2 * p])
        put1 = _dot(pt_ref[2 * p + 1], ugt[2 * p + 1])
        pure[p] = jnp.concatenate([put0[:half], put1[:half]], axis=0).T
        puim[p] = jnp.concatenate([put0[half:], put1[half:]], axis=0).T

    chains = [(sl, pr) for sl in range(spb) for pr in range(ppg)]
    state = [(stre[slab(sl), pr], stim[slab(sl), pr]) for sl, pr in chains]
    coef = [(jnp.broadcast_to(are_ref[sl * ppg + pr], (nb, 2 * half)),
             jnp.broadcast_to(aim_ref[sl * ppg + pr], (nb, 2 * half))) for sl, pr in chains]
    for j in range(nch):
        rows = slice(j * nb, (j + 1) * nb)
        for c, (sl, pr) in enumerate(chains):
            p = sl * ppg + pr
            re, im = state[c]
            ar, ai = coef[c]
            hinre[p, rows, :] = re
            hinim[p, rows, :] = im
            state[c] = (ar * re - ai * im + pure[p, rows, :], ar * im + ai * re + puim[p, rows, :])
    for c, (sl, pr) in enumerate(chains):
        re, im = state[c]
        stre[slab(sl), pr] = re
        stim[slab(sl), pr] = im
        hre_ref[slab(sl) * ppg + pr] = re
        him_ref[slab(sl) * ppg + pr] = im

    for p in range(spb * ppg):
        hre_t = hinre[p].T
        him_t = hinim[p].T
        for i in range(2):
            g = 2 * p + i
            hin_t = jnp.concatenate([hre_t[i * half:(i + 1) * half], him_t[i * half:(i + 1) * half]],
                                    axis=0).astype(BF16)
            ytmp[g] = _dot(tt_ref[g], ugt[g]) + _dot(qt_ref[g], hin_t)

    for sl in range(spb):
        for t in range(lc):
            zt = jnp.concatenate([ytmp[sl * gps + g, t * ch:(t + 1) * ch, :] for g in range(gps)], axis=0).T
            for j in range(nch):
                ys[sl, tile(j, t), :] = zt[j * nb:(j + 1) * nb, :]
    for sl in range(spb):
        for b in range(nb):
            y_ref[b, :, sl * V7X_LANES:(sl + 1) * V7X_LANES] = ys[sl, pl.ds(b, tbt, stride=nb), :].astype(y_ref.dtype)


def _s5_fused(x, g, mats_t, h0re, h0im, tbt):
    nb, s, d = x.shape
    tt, pt, qt, are, aim = mats_t
    groups = tt.shape[0]
    gps = V7X_LANES // S5_GROUP_CH
    slabs = groups // gps
    w = S5_CHUNK * S5_GROUP_CH
    nst = pt.shape[1]
    cols = nb * (tbt // S5_CHUNK)
    spb = S5_SLABS_PER_STEP
    ng, npair = spb * gps, spb * gps // 2
    kern = functools.partial(_s5_fused_kernel, nb=nb, tbt=tbt, spb=spb)
    slab3 = lambda t, k: (k, 0, 0)
    blocks = (_nbytes((nb, tbt, d), F32) + _nbytes((nb, tbt, spb * V7X_LANES), BF16)
              + ng * (_nbytes((w, w), BF16) + 2 * _nbytes((w, nst), BF16)))
    scratch = (_nbytes((slabs, nb * tbt, V7X_LANES), F32) + _nbytes((ng, w, cols), BF16)
               + 4 * _nbytes((npair, cols, nst), F32) + _nbytes((ng, w, cols), F32)
               + _nbytes((spb, nb * tbt, V7X_LANES), F32))
    return pl.pallas_call(
        kern,
        grid=(s // tbt, slabs // spb),
        in_specs=[
            pl.BlockSpec((nb, tbt, d), lambda t, k: (0, t, 0)),
            pl.BlockSpec((1, d), lambda t, k: (0, 0)),
            pl.BlockSpec((ng, w, w), slab3),
            pl.BlockSpec((ng, nst, w), slab3),
            pl.BlockSpec((ng, w, nst), slab3),
            pl.BlockSpec((npair, 1, nst), slab3),
            pl.BlockSpec((npair, 1, nst), slab3),
            pl.BlockSpec((npair, nb, nst), slab3),
            pl.BlockSpec((npair, nb, nst), slab3),
        ],
        out_specs=[
            pl.BlockSpec((nb, tbt, spb * V7X_LANES), lambda t, k: (0, t, k)),
            pl.BlockSpec((groups // 2, nb, nst), lambda t, k: (0, 0, 0)),
            pl.BlockSpec((groups // 2, nb, nst), lambda t, k: (0, 0, 0)),
        ],
        out_shape=[
            jax.ShapeDtypeStruct((nb, s, d), BF16),
            jax.ShapeDtypeStruct((groups // 2, nb, nst), F32),
            jax.ShapeDtypeStruct((groups // 2, nb, nst), F32),
        ],
        scratch_shapes=[
            pltpu.VMEM((slabs, nb * tbt, V7X_LANES), F32),
            pltpu.VMEM((ng, w, cols), BF16),
            pltpu.VMEM((npair, cols, nst), F32),
            pltpu.VMEM((npair, cols, nst), F32),
            pltpu.VMEM((npair, cols, nst), F32),
            pltpu.VMEM((npair, cols, nst), F32),
            pltpu.VMEM((ng, w, cols), F32),
            pltpu.VMEM((spb, nb * tbt, V7X_LANES), F32),
            pltpu.VMEM((slabs, gps // 2, nb, nst), F32),
            pltpu.VMEM((slabs, gps // 2, nb, nst), F32),
        ],
        compiler_params=pltpu.CompilerParams(
            dimension_semantics=("arbitrary", "arbitrary"), vmem_limit_bytes=_vmem_limit(blocks, scratch)),
        name="s5_fused",
    )(x, g, tt, pt, qt, are, aim, h0re, h0im)


def _s5_matrices(a_re, a_im, log_dt, b_re, b_im, c_re, c_im):
    lc = S5_CHUNK
    groups, n = a_re.shape
    ch = S5_GROUP_CH
    f = lambda v: v.astype(F32)
    a_re, a_im, b_re, b_im, c_re, c_im = map(f, (a_re, a_im, b_re, b_im, c_re, c_im))
    dt = jnp.exp(f(log_dt))[:, None, None]
    j = jnp.arange(lc + 1, dtype=F32)
    mag = jnp.exp((a_re[:, :, None] * dt) * j)
    ang = (a_im[:, :, None] * dt) * j
    pw_re, pw_im = mag * jnp.cos(ang), mag * jnp.sin(ang)
    e_re, e_im = pw_re[:, :, 1] - 1.0, pw_im[:, :, 1]
    inv = 1.0 / (a_re * a_re + a_im * a_im)
    f_re, f_im = (e_re * a_re + e_im * a_im) * inv, (e_im * a_re - e_re * a_im) * inv
    bb_re = f_re[..., None] * b_re - f_im[..., None] * b_im
    bb_im = f_re[..., None] * b_im + f_im[..., None] * b_re
    rep = lambda v: jnp.repeat(v, ch, axis=2)
    til = lambda v: jnp.tile(v, (1, 1, lc + 1))
    lb_re = rep(pw_re) * til(bb_re) - rep(pw_im) * til(bb_im)
    lb_im = rep(pw_re) * til(bb_im) + rep(pw_im) * til(bb_re)
    kern = jnp.einsum('gcm,gmx->gcx', jnp.concatenate([c_re, c_im], axis=2),
                      jnp.concatenate([lb_re, -lb_im], axis=1), precision=HI)
    w = lc * ch
    kfwd = jnp.pad(kern[:, :, :w], ((0, 0), (0, 0), ((lc - 1) * ch, 0)))
    tt = jnp.stack([kfwd[:, :, t * ch:t * ch + w] for t in range(lc)], axis=1).reshape(groups, w, w)
    pt = jnp.concatenate([lb_re[:, :, :w], lb_im[:, :, :w]], axis=1)
    pwt_re = jnp.swapaxes(pw_re, 1, 2)[:, 1:, None, :]
    pwt_im = jnp.swapaxes(pw_im, 1, 2)[:, 1:, None, :]
    ca_re = c_re[:, None] * pwt_re - c_im[:, None] * pwt_im
    ca_im = c_re[:, None] * pwt_im + c_im[:, None] * pwt_re
    qt = jnp.concatenate([ca_re, -ca_im], axis=3).reshape(groups, w, 2 * n)
    are = pw_re[:, :, lc].reshape(groups // 2, 1, 2 * n)
    aim = pw_im[:, :, lc].reshape(groups // 2, 1, 2 * n)
    return tt.astype(BF16), pt.astype(BF16), qt.astype(BF16), are, aim


def _glu_kernel(x_ref, y_ref, g0_ref, d_ref, w_ref, g1_ref, o_ref, out_s):
    x = x_ref[...]
    d = x.shape[1]
    u = _rms(x, g0_ref[...])
    yy = y_ref[...].astype(F32) + d_ref[...] * u
    z = jax.nn.gelu(yy, approximate=True).astype(BF16)
    w = V7X_MXU
    nslab = d // w
    pair = lambda j: (_dot(z, w_ref[:, j * w:(j + 1) * w]), _dot(z, w_ref[:, d + j * w:d + (j + 1) * w]))
    nxt = pair(0)
    ssq = jnp.zeros((x.shape[0], 1), F32)
    for j in range(nslab):
        a, b = nxt
        if j + 1 < nslab:
            nxt = pair(j + 1)
        out = a * (0.5 * (1.0 + jnp.tanh(0.5 * b)))
        out_s[:, j * w:(j + 1) * w] = out
        ssq = ssq + jnp.sum(out * out, axis=-1, keepdims=True)
    scale = lax.rsqrt(ssq * (1.0 / d) + NORM_EPS)
    o_ref[...] = x + out_s[...] * scale * g1_ref[...]


def _const_spec(shape):
    nd = len(shape)
    return pl.BlockSpec(shape, lambda *_: (0,) * nd, pipeline_mode=pl.Buffered(1))


def _glu(x2d, y2d, g0, dskip, w, g1, tb):
    rows, d = x2d.shape
    blocks = 2 * _nbytes((tb, d), F32) + _nbytes((tb, d), BF16)
    scratch = _nbytes(w.shape, BF16) + 3 * _nbytes((tb, 2 * d), F32)
    row = pl.BlockSpec((tb, d), lambda i: (i, 0))
    return pl.pallas_call(
        _glu_kernel,
        grid=(rows // tb,),
        in_specs=[row, row, _const_spec((1, d)), _const_spec((1, d)), _const_spec(w.shape), _const_spec((1, d))],
        out_specs=row,
        out_shape=jax.ShapeDtypeStruct((rows, d), F32),
        scratch_shapes=[pltpu.VMEM((tb, d), F32)],
        compiler_params=pltpu.CompilerParams(
            dimension_semantics=("arbitrary",), vmem_limit_bytes=_vmem_limit(blocks, scratch)),
        name="s5_glu",
    )(x2d, y2d, g0, dskip, w, g1)


def _ffn_kernel(x_ref, buf_ref, g2_ref, wup_ref, cw_ref, cb_ref, wdown_ref, g3_ref,
                o_ref, nbuf_ref, carry, act, *, tb, sub, dff, width):
    t = pl.program_id(1)
    pad = V7X_SUBLANES

    @pl.when(t == 0)
    def _():
        carry[...] = jnp.zeros_like(carry)
        carry[pad - (width - 1):pad, :] = buf_ref[0]

    fb = V7X_MXU
    nslab = dff // fb
    cols = lambda j: (slice(j * fb, (j + 1) * fb), slice(dff + j * fb, dff + (j + 1) * fb))

    def conv(u, cs):
        prev = carry[:, cs]
        carry[:, cs] = u[sub - pad:sub, :]
        return _causal_dwconv(u, prev, cw_ref, cb_ref, cs)

    def up_phase(r):
        hn = _rms(x_ref[0, r * sub:(r + 1) * sub, :], g2_ref[...]).astype(BF16)
        up = lambda j: tuple(_dot(hn, wup_ref[:, cs]) for cs in cols(j))
        nxt = up(0)
        for j in range(nslab):
            ug, uv = nxt
            if j + 1 < nslab:
                nxt = up(j + 1)
            gate = conv(ug, cols(j)[0])
            val = conv(uv, cols(j)[1])
            act[r % act.shape[0], :, j * fb:(j + 1) * fb] = (jax.nn.gelu(gate, approximate=True) * val).astype(BF16)

    nsub = tb // sub
    up_phase(0)
    for r in range(nsub):
        f = _dot(act[r % act.shape[0]], wdown_ref[...])
        if r + 1 < nsub:
            up_phase(r + 1)
        rows = slice(r * sub, (r + 1) * sub)
        o_ref[0, rows, :] = x_ref[0, rows, :] + _rms(f, g3_ref[...])
    nbuf_ref[0] = carry[pad - (width - 1):pad, :]


def _layer_spec(shape, layer):
    nd = len(shape) - 1
    return pl.BlockSpec((None,) + tuple(shape[1:]), lambda *_: (layer,) + (0,) * nd, pipeline_mode=pl.Buffered(1))


def _ffn(x, buf, g2, wup, cw, cb, wdown, g3, layer, tb):
    b, s, d = x.shape
    dff = wdown.shape[1]
    width = cw.shape[1]
    sub = min(tb, FFN_ROW_BLOCK)
    kern = functools.partial(_ffn_kernel, tb=tb, sub=sub, dff=dff, width=width)
    blocks = 2 * _nbytes((tb, d), F32)
    scratch = (_nbytes(wup.shape[1:], BF16) + _nbytes(wdown.shape[1:], BF16) + 6 * _nbytes((sub, V7X_MXU), F32)
               + _nbytes((8, 2 * dff), F32) + 2 * _nbytes((sub, dff), BF16) + 3 * _nbytes((sub, d), F32))
    xs = pl.BlockSpec((1, tb, d), lambda i, t: (i, t, 0))
    bs = pl.BlockSpec((1, width - 1, 2 * dff), lambda i, t: (i, 0, 0))
    return pl.pallas_call(
        kern,
        grid=(b, s // tb),
        in_specs=[xs, bs, _const_spec((1, d)), _layer_spec(wup.shape, layer), _layer_spec(cw.shape, layer),
                  _layer_spec(cb.shape, layer), _layer_spec(wdown.shape, layer), _const_spec((1, d))],
        out_specs=[xs, bs],
        out_shape=[jax.ShapeDtypeStruct((b, s, d), F32), jax.ShapeDtypeStruct((b, width - 1, 2 * dff), F32)],
        scratch_shapes=[pltpu.VMEM((V7X_SUBLANES, 2 * dff), F32), pltpu.VMEM((min(2, tb // sub), sub, dff), BF16)],
        compiler_params=pltpu.CompilerParams(
            dimension_semantics=("arbitrary", "arbitrary"), vmem_limit_bytes=_vmem_limit(blocks, scratch)),
        name="conv_ffn",
    )(x, buf, g2, wup, cw, cb, wdown, g3)


def _mlstm_weights_kernel(aq_ref, ak_ref, av_ref, gq_ref, gk_ref, gv_ref, wup_ref,
                          wq_ref, wkt_ref, wv_ref, gc_ref, gh_ref, *, kscale):
    w = V7X_MXU
    kb = QKV_BLOCK
    dotp = functools.partial(jnp.dot, preferred_element_type=F32, precision=HI)
    lane = lax.broadcasted_iota(jnp.int32, (V7X_LANES, w), 1)
    src = lax.broadcasted_iota(jnp.int32, (V7X_LANES, w), 0)
    spread = jnp.where(jnp.bitwise_and(lane, kb - 1) == src, 1.0, 0.0)
    shift = kb.bit_length() - 1
    rblk = lax.shift_right_logical(lax.broadcasted_iota(jnp.int32, (w, w), 0), shift)
    cblk = lax.shift_right_logical(lax.broadcasted_iota(jnp.int32, (w, w), 1), shift)
    tile = lambda a_ref: jnp.where(rblk == cblk, dotp(a_ref[0], spread), 0.0)
    tq, tk, tv = tile(aq_ref), tile(ak_ref), tile(av_ref)
    wq_ref[0] = tq.astype(BF16)
    wkt_ref[0] = (tk.T * kscale).astype(BF16)
    wv_ref[0] = tv.astype(BF16)
    gc_ref[...] = (dotp(tq, gq_ref[...]) + dotp(tk, gk_ref[...])).astype(BF16)

    @pl.when(pl.program_id(0) == 0)
    def _():
        gh_ref[...] = jnp.zeros_like(gh_ref)

    gh_ref[...] += dotp(wup_ref[...], dotp(tv, gv_ref[...]))


def _mlstm_weights(wq, wk, wv, w_gate, w_up, kscale):
    nblk, kb, _ = wq.shape
    assert kb == QKV_BLOCK and kb & (kb - 1) == 0
    w = V7X_MXU
    per = w // kb
    nt = nblk // per
    inner = nt * w
    compact = lambda m: jnp.pad(m.astype(F32).reshape(nt, w, kb), ((0, 0), (0, 0), (0, V7X_LANES - kb)))
    wg = jnp.pad(w_gate.astype(F32), ((0, 0), (0, V7X_LANES - w_gate.shape[1])))
    cm = pl.BlockSpec((1, w, V7X_LANES), lambda i: (i, 0, 0))
    gs = [pl.BlockSpec((w, V7X_LANES), lambda i, k=k: (k * nt + i, 0)) for k in range(3)]
    tile = pl.BlockSpec((1, w, w), lambda i: (i, 0, 0))
    fold = pl.BlockSpec((w, V7X_LANES), lambda i: (i, 0))
    d = w_up.shape[0]
    return pl.pallas_call(
        functools.partial(_mlstm_weights_kernel, kscale=kscale),
        grid=(nt,),
        in_specs=[cm, cm, cm] + gs + [pl.BlockSpec((d, w), lambda i: (0, i))],
        out_specs=[tile, tile, tile, fold, pl.BlockSpec((d, V7X_LANES), lambda i: (0, 0))],
        out_shape=[jax.ShapeDtypeStruct((nt, w, w), BF16)] * 3
        + [jax.ShapeDtypeStruct((inner, V7X_LANES), BF16), jax.ShapeDtypeStruct((d, V7X_LANES), F32)],
        compiler_params=pltpu.CompilerParams(dimension_semantics=("arbitrary",)),
        name="mlstm_weights",
    )(compact(wq), compact(wk), compact(wv), wg, wg, wg, w_up.astype(F32))


def _mlstm_chunk(q_ref, kt_ref, v_ref, g, gt, ct_s, n_s, m_s, fill, emit_hn, fill_update, *, lc, heads, dh):
    row = lax.broadcasted_iota(jnp.int32, (lc, lc), 0)
    col = lax.broadcasted_iota(jnp.int32, (lc, lc), 1)
    causal = col <= row
    tri_l = jnp.where(causal, 1.0, 0.0).astype(BF16)
    tri_u = jnp.where(row <= col, 1.0, 0.0).astype(BF16)
    b_cols = sum(_dot(tri_l, part) for part in _split3(_log_sigmoid(g)))
    b_rows = sum(_dot(part, tri_u) for part in _split3(_log_sigmoid(gt)))

    hsl = [slice(h * dh, (h + 1) * dh) for h in range(heads)]
    st = []
    for h in range(heads):
        b_col = b_cols[:, heads + h:heads + h + 1]
        b_row = b_rows[heads + h:heads + h + 1, :]
        i_row = gt[h:h + 1, :]
        m_prev = m_s[h:h + 1, 0:1]
        dmat = jnp.where(causal, b_col - b_row + i_row, -jnp.inf)
        inter = b_col + m_prev
        m_t = jnp.maximum(inter, jnp.max(dmat, axis=1, keepdims=True))
        st.append(dict(b_col=b_col, b_row=b_row, i_row=i_row, m_prev=m_prev, m_t=m_t,
                       wts=jnp.exp(dmat - m_t), a=jnp.exp(inter - m_t)))
        fill()
    for h in range(heads):
        s = st[h]
        s['sc'] = _dot(q_ref[0, :, hsl[h]], kt_ref[0, hsl[h], :]) * s['wts']
    out_mm = lambda h: (_dot(st[h]['sc'].astype(BF16), v_ref[0, :, hsl[h]]),
                        _dot(q_ref[0, :, hsl[h]], ct_s[h].astype(BF16)))
    nxt = out_mm(0)
    for h in range(heads):
        intra, inter_mm = nxt
        if h + 1 < heads:
            nxt = out_mm(h + 1)
        fill()
        s = st[h]
        qh = q_ref[0, :, hsl[h]]
        sc = s['sc']
        a = s['a']
        num = intra + a * inter_mm
        qn = jnp.sum(qh.astype(F32) * n_s[h:h + 1, :], axis=1, keepdims=True)
        den = jnp.sum(sc, axis=1, keepdims=True) + a * qn
        hh = num / jnp.maximum(jnp.abs(den), jnp.exp(-s['m_t']))
        mu = jnp.mean(hh, axis=1, keepdims=True)
        dev = hh - mu
        var = jnp.mean(dev * dev, axis=1, keepdims=True)
        emit_hn(h, dev * lax.rsqrt(var + LN_EPS))
    def update_mm(h):
        s = st[h]
        kth = kt_ref[0, hsl[h], :]
        b_last = s['b_row'][:, lc - 1:lc]
        g_row = b_last - s['b_row'] + s['i_row']
        m_new = jnp.maximum(b_last + s['m_prev'], jnp.max(g_row, axis=1, keepdims=True))
        decay = jnp.exp(b_last + s['m_prev'] - m_new)
        wg_row = jnp.exp(g_row - m_new)
        wg_col = jnp.exp(b_last - s['b_col'] + g[:, h:h + 1] - m_new)
        wv = (v_ref[0, :, hsl[h]].astype(F32) * wg_col).astype(BF16)
        wg16 = jnp.broadcast_to(wg_row, (16, lc)).astype(BF16)
        return m_new, decay, _dot(kth, wv), _dot_nt(wg16, kth)[0:1, :]

    fill_update(0)
    nxt = update_mm(0)
    for h in range(heads):
        m_new, decay, c_upd, n_upd = nxt
        if h + 1 < heads:
            fill_update(h + 1)
            nxt = update_mm(h + 1)
        ct_s[h] = decay * ct_s[h] + c_upd
        n_s[h:h + 1, :] = decay * n_s[h:h + 1, :] + n_upd
        m_s[h:h + 1, :] = jnp.broadcast_to(m_new, (1, V7X_LANES))


def _mlstm_layer_kernel(*refs, lc, heads, inner, width, zero_init):
    (x_ref, cbuf_ref, g0_ref, wup_ref, cw_ref, cb_ref, wq_ref, wkt_ref, wv_ref, gc_ref, gh_ref, bg_ref,
     ng_ref, sk_ref, wdown_ref, g1_ref) = refs[:16]
    n_in = 16
    if not zero_init:
        c0_ref, n0_ref, m0_ref = refs[16:19]
        n_in = 19
    o_ref, cout_ref, nout_ref, mout_ref, nbuf_ref = refs[n_in:n_in + 5]
    carry, q_s, kt_s, v_s, xc_s, sz_s, act_s, ct_s, n_s, m_s = refs[n_in + 5:]
    dh = inner // heads
    pad = V7X_SUBLANES
    c = pl.program_id(1)

    @pl.when(c == 0)
    def _():
        carry[...] = jnp.zeros_like(carry)
        carry[pad - (width - 1):pad, :] = cbuf_ref[0]
        if zero_init:
            ct_s[...] = jnp.zeros_like(ct_s)
            n_s[...] = jnp.zeros_like(n_s)
            m_s[...] = jnp.zeros_like(m_s)
        else:
            for h in range(heads):
                ct_s[h] = c0_ref[0, h].T
            n_s[...] = n0_ref[0]
            m_s[...] = m0_ref[0]

    x = x_ref[0]
    hx = _rms(x, g0_ref[...]).astype(BF16)
    w = V7X_MXU
    gacc = bg_ref[...] + _dot(hx, gh_ref[...])
    nslab = inner // w
    col = lambda i: slice(i * w, (i + 1) * w)

    def project(i, xc, xmb, gacc):
        q_s[0, :, col(i)] = _dot(xc, wq_ref[i]).astype(BF16)
        kt_s[0, col(i), :] = _dot_nt(wkt_ref[i], xc).astype(BF16)
        v_s[0, :, col(i)] = _dot(xmb, wv_ref[i]).astype(BF16)
        xc_s[:, col(i)] = xc
        return gacc + _dot(xc, gc_ref[col(i), :])

    xm_next = _dot(hx, wup_ref[:, col(0)])
    pending = None
    for i in range(nslab):
        xm = xm_next
        if i + 1 < nslab:
            xm_next = _dot(hx, wup_ref[:, col(i + 1)])
        if pending is not None:
            gacc = project(*pending, gacc)
        prev = carry[:, col(i)]
        carry[:, col(i)] = xm[lc - pad:lc, :]
        xc = _silu(_causal_dwconv(xm, prev, cw_ref, cb_ref, col(i))).astype(BF16)
        pending = (i, xc, xm.astype(BF16))
    gacc = project(*pending, gacc)
    nbuf_ref[0] = carry[pad - (width - 1):pad, :]

    slabs = iter(range(inner // w))

    def gate_slab():
        i = next(slabs, None)
        if i is not None:
            cs = slice(i * w, (i + 1) * w)
            sz_s[:, cs] = _silu(_dot(hx, wup_ref[:, inner + i * w:inner + (i + 1) * w])).astype(BF16)

    hcols = lambda h: slice(h * dh, (h + 1) * dh)
    down = []

    def gated_out(h, hn):
        hs = hn * ng_ref[:, hcols(h)] + sk_ref[:, hcols(h)] * xc_s[:, hcols(h)].astype(F32)
        act_s[:, hcols(h)] = (hs * sz_s[:, hcols(h)].astype(F32)).astype(BF16)

    def down_proj(h):
        if h == 0:
            down.append(_dot(act_s[...], wdown_ref[...]))

    _mlstm_chunk(q_s, kt_s, v_s, gacc, gacc.T[0:2 * heads, :], ct_s, n_s, m_s, gate_slab, gated_out, down_proj,
                 lc=lc, heads=heads, dh=dh)
    for _ in slabs:
        raise AssertionError("fewer fill points than output-gate slabs")
    o_ref[0] = x + _rms(down[0], g1_ref[...])

    @pl.when(c == pl.num_programs(1) - 1)
    def _():
        for h in range(heads):
            cout_ref[0, h] = ct_s[h].T
        nout_ref[0] = n_s[...]
        mout_ref[0] = m_s[...]


def _mlstm_layer(x, cbuf, state, p, g0, g1, lc):
    b, s, d = x.shape
    heads = p['heads']
    inner = p['m_cw'].shape[1]
    width = p['m_cw'].shape[0]
    dh = inner // heads
    zero_init = state is None
    kern = functools.partial(_mlstm_layer_kernel, lc=lc, heads=heads, inner=inner, width=width,
                             zero_init=zero_init)
    xs = pl.BlockSpec((1, lc, d), lambda i, c: (i, c, 0))
    bufs = pl.BlockSpec((1, width - 1, inner), lambda i, c: (i, 0, 0))
    cs = pl.BlockSpec((1, heads, dh, dh), lambda i, c: (i, 0, 0, 0))
    ns = pl.BlockSpec((1, heads, dh), lambda i, c: (i, 0, 0))
    ms = pl.BlockSpec((1, heads, V7X_LANES), lambda i, c: (i, 0, 0))
    consts = [g0, p['m_wup'], p['m_cw'], p['m_cb'], p['m_wq'], p['m_wkt'], p['m_wv'], p['m_gc'], p['m_gh'],
              p['m_bg'], p['m_ng'], p['m_skip'], p['m_wdown'], g1]
    in_specs = [xs, bufs] + [_const_spec(a.shape) for a in consts]
    args = [x, cbuf] + consts
    if not zero_init:
        in_specs += [cs, ns, ms]
        args += list(state)
    blocks = 2 * _nbytes((lc, d), F32) + (1 if zero_init else 2) * _nbytes((heads, dh, dh), F32)
    scratch = (sum(_nbytes(a.shape, a.dtype) for a in consts) + 6 * _nbytes((lc, inner), BF16)
               + _nbytes((heads, dh, dh), F32) + 8 * _nbytes((lc, max(lc, dh)), F32) + 2 * _nbytes((dh, dh), F32))
    return pl.pallas_call(
        kern,
        grid=(b, s // lc),
        in_specs=in_specs,
        out_specs=[xs, cs, ns, ms, bufs],
        out_shape=[jax.ShapeDtypeStruct((b, s, d), F32), jax.ShapeDtypeStruct((b, heads, dh, dh), F32),
                   jax.ShapeDtypeStruct((b, heads, dh), F32), jax.ShapeDtypeStruct((b, heads, V7X_LANES), F32),
                   jax.ShapeDtypeStruct((b, width - 1, inner), F32)],
        scratch_shapes=[pltpu.VMEM((V7X_SUBLANES, inner), F32),
                        pltpu.VMEM((1, lc, inner), BF16), pltpu.VMEM((1, inner, lc), BF16),
                        pltpu.VMEM((1, lc, inner), BF16), pltpu.VMEM((lc, inner), BF16),
                        pltpu.VMEM((lc, inner), BF16), pltpu.VMEM((lc, inner), BF16),
                        pltpu.VMEM((heads, dh, dh), F32), pltpu.VMEM((heads, dh), F32),
                        pltpu.VMEM((heads, V7X_LANES), F32)],
        compiler_params=pltpu.CompilerParams(
            dimension_semantics=("arbitrary", "arbitrary"), vmem_limit_bytes=_vmem_limit(blocks, scratch)),
        name="mlstm_layer",
    )(*args)


def _prep_weights(w):
    row = lambda v: v.astype(F32).reshape(1, -1)
    p = {}
    p['norm_g'] = w['norm_g'].astype(F32)
    p['s5_mats'] = _s5_matrices(w['s5_A_re'][0], w['s5_A_im'][0], w['s5_log_dt'][0], w['s5_B_re'][0],
                                w['s5_B_im'][0], w['s5_C_re'][0], w['s5_C_im'][0])
    p['s5_D'] = row(w['s5_D'][0])
    p['s5_w_glu'] = w['s5_w_glu'][0].astype(BF16)
    inner = w['mlstm_conv_w'].shape[2]
    heads = w['mlstm_b_gate'].shape[1] // 2
    dh = inner // heads
    p['m_wq'], p['m_wkt'], p['m_wv'], p['m_gc'], gh = _mlstm_weights(
        w['mlstm_wq'][0], w['mlstm_wk'][0], w['mlstm_wv'][0], w['mlstm_w_gate'][0], w['mlstm_w_up'][0],
        dh ** -0.5)
    p['m_gh'] = gh.astype(BF16)
    bg = row(w['mlstm_b_gate'][0])
    p['m_bg'] = jnp.pad(bg, ((0, 0), (0, V7X_LANES - bg.shape[1])))
    p['m_wup'] = w['mlstm_w_up'][0].astype(BF16)
    p['m_cw'] = w['mlstm_conv_w'][0].astype(F32)
    p['m_cb'] = row(w['mlstm_conv_b'][0])
    p['m_ng'] = row(w['mlstm_norm_g'][0])
    p['m_skip'] = row(w['mlstm_skip'][0])
    p['m_wdown'] = w['mlstm_w_down'][0].astype(BF16)
    p['heads'] = heads
    p['f_wup'] = w['ffn_w_up'].astype(BF16)
    p['f_cw'] = w['ffn_conv_w'].astype(F32)
    p['f_cb'] = w['ffn_conv_b'].astype(F32)[:, None, :]
    p['f_wdown'] = w['ffn_w_down'].astype(BF16)
    return p


def _to_chunk_major(u2d, b, s):
    d = u2d.shape[1]
    g = d // S5_GROUP_CH
    u = u2d.reshape(b, s // S5_CHUNK, S5_CHUNK, g, S5_GROUP_CH)[:, :, ::-1]
    return jnp.transpose(u, (1, 0, 3, 2, 4)).reshape((s // S5_CHUNK) * b, d * S5_CHUNK)


def _to_token_major(yt, b, s):
    d = yt.shape[1] // S5_CHUNK
    g = d // S5_GROUP_CH
    y = yt.reshape(s // S5_CHUNK, b, g, S5_CHUNK, S5_GROUP_CH)
    return jnp.transpose(y, (1, 0, 3, 2, 4)).reshape(b * s, d)


def _trunk(x, st, p, tiles):
    b, s, d = x.shape
    ng = p['norm_g']
    g_of = lambda layer, k: ng[layer, k].reshape(1, d)
    zero_init = st is None
    groups = d // S5_GROUP_CH
    x2d = x.reshape(b * s, d)

    n_state = p['s5_mats'][1].shape[1] // 2
    if zero_init:
        h0re = h0im = jnp.zeros((groups // 2, b, 2 * n_state), F32)
    else:
        pair = lambda h: jnp.transpose(h.astype(F32).reshape(b, groups // 2, 2 * n_state), (1, 0, 2))
        h0re, h0im = pair(st['s5_re']), pair(st['s5_im'])
    if tiles.s5_time:
        y, hre, him = _s5_fused(x, g_of(0, 0), p['s5_mats'], h0re, h0im, tiles.s5_time)
        y2d = y.reshape(b * s, d)
    else:
        u = _norm_cast(x2d, g_of(0, 0), tiles.rows)
        yt, hre, him = _s5_core(_to_chunk_major(u, b, s), p['s5_mats'], h0re, h0im, b, tiles.s5_rows)
        y2d = _to_token_major(yt, b, s)
    unpair = lambda h: jnp.transpose(h, (1, 0, 2)).reshape(1, b, groups, n_state)
    o_re, o_im = unpair(hre), unpair(him)
    x2d = _glu(x2d, y2d, g_of(0, 0), p['s5_D'], p['s5_w_glu'], g_of(0, 1), tiles.rows)

    def ffn(x2d, layer):
        dff2 = p['f_wup'].shape[2]
        width = p['f_cw'].shape[1]
        buf = jnp.zeros((b, width - 1, dff2), F32) if zero_init else st['ffn_conv'][layer].astype(F32)
        xo, nbuf = _ffn(x2d.reshape(b, s, d), buf, g_of(layer, 2), p['f_wup'], p['f_cw'], p['f_cb'],
                        p['f_wdown'], g_of(layer, 3), layer, tiles.ffn_time)
        return xo.reshape(b * s, d), nbuf

    x2d, fbuf0 = ffn(x2d, 0)

    heads = p['heads']
    inner = p['m_cw'].shape[1]
    width = p['m_cw'].shape[0]
    cbuf = jnp.zeros((b, width - 1, inner), F32) if zero_init else st['mlstm_conv'].astype(F32)
    if zero_init:
        state = None
    else:
        state = (st['mlstm_C'].astype(F32), st['mlstm_n'].astype(F32),
                 jnp.broadcast_to(st['mlstm_m'].astype(F32)[:, :, None], (b, heads, V7X_LANES)))
    x3, c_out, n_out, m_out, ncbuf = _mlstm_layer(x2d.reshape(b, s, d), cbuf, state, p, g_of(1, 0), g_of(1, 1),
                                                  tiles.mlstm_chunk)
    x2d, fbuf1 = ffn(x3.reshape(b * s, d), 1)

    return (x2d.reshape(b, s, d), o_re, o_im, c_out[None], n_out[None], m_out[None, :, :, 0], ncbuf[None],
            jnp.stack([fbuf0, fbuf1], axis=0))


def kernel(x_prompt, x_sample, state_s5_re, state_s5_im, state_mlstm_C, state_mlstm_n, state_mlstm_m, state_mlstm_conv, state_ffn_conv, norm_g, s5_A_re, s5_A_im, s5_log_dt, s5_B_re, s5_B_im, s5_C_re, s5_C_im, s5_D, s5_w_glu, mlstm_w_up, mlstm_conv_w, mlstm_conv_b, mlstm_wq, mlstm_wk, mlstm_wv, mlstm_w_gate, mlstm_b_gate, mlstm_norm_g, mlstm_skip, mlstm_w_down, ffn_w_up, ffn_conv_w, ffn_conv_b, ffn_w_down):
    w = {
        'norm_g': norm_g, 's5_A_re': s5_A_re, 's5_A_im': s5_A_im, 's5_log_dt': s5_log_dt,
        's5_B_re': s5_B_re, 's5_B_im': s5_B_im, 's5_C_re': s5_C_re, 's5_C_im': s5_C_im, 's5_D': s5_D,
        's5_w_glu': s5_w_glu, 'mlstm_w_up': mlstm_w_up, 'mlstm_conv_w': mlstm_conv_w,
        'mlstm_conv_b': mlstm_conv_b, 'mlstm_wq': mlstm_wq, 'mlstm_wk': mlstm_wk, 'mlstm_wv': mlstm_wv,
        'mlstm_w_gate': mlstm_w_gate, 'mlstm_b_gate': mlstm_b_gate, 'mlstm_norm_g': mlstm_norm_g,
        'mlstm_skip': mlstm_skip, 'mlstm_w_down': mlstm_w_down, 'ffn_w_up': ffn_w_up,
        'ffn_conv_w': ffn_conv_w, 'ffn_conv_b': ffn_conv_b, 'ffn_w_down': ffn_w_down,
    }
    assert norm_g.shape[0] == 2 and s5_A_re.shape[0] == 1 and mlstm_w_up.shape[0] == 1
    assert x_prompt.shape[0] == V7X_SUBLANES and x_sample.shape[0] == V7X_SUBLANES
    p = _prep_weights(w)
    out_p = _trunk(x_prompt.astype(F32), None, p, _pick_tiles(*x_prompt.shape[:2]))
    st = {'s5_re': state_s5_re[0], 's5_im': state_s5_im[0], 'mlstm_C': state_mlstm_C[0],
          'mlstm_n': state_mlstm_n[0], 'mlstm_m': state_mlstm_m[0], 'mlstm_conv': state_mlstm_conv[0],
          'ffn_conv': state_ffn_conv}
    out_s = _trunk(x_sample.astype(F32), st, p, _pick_tiles(*x_sample.shape[:2]))
    return (out_p[0], out_s[0]) + tuple(out_p[1:]) + tuple(out_s[1:])
```

```python
import functools
import math
from typing import NamedTuple

import jax
import jax.numpy as jnp
from jax import lax
from jax.experimental import pallas as pl
from jax.experimental.pallas import tpu as pltpu

F32 = jnp.float32
BF16 = jnp.bfloat16
HI = lax.Precision.HIGHEST

NORM_EPS = 1e-6
LN_EPS = 1e-5
S5_GROUP_CH = 16
S5_CHUNK = 16
QKV_BLOCK = 4
S5_SLABS_PER_STEP = 2

V7X_LANES = 128
V7X_SUBLANES = 8
V7X_MXU = 256
V7X_VMEM_BYTES = 64 * 1024 * 1024


class Tiles(NamedTuple):
    rows: int
    time: int
    s5_rows: int
    mlstm_chunk: int
    s5_time: int
    ffn_time: int


def _pick_tiles(batch, seq):
    s5_time = (V7X_LANES // batch) * S5_CHUNK
    return Tiles(rows=min(batch * seq, 512), time=min(seq, 512),
                 s5_rows=min(batch * (seq // S5_CHUNK), 512), mlstm_chunk=min(seq, 256),
                 s5_time=s5_time if seq % s5_time == 0 else 0, ffn_time=min(seq, 512))


def _vmem_limit(block_bytes, scratch_bytes):
    want = 2 * block_bytes + scratch_bytes + 16 * 1024 * 1024
    return int(min(want, V7X_VMEM_BYTES - 8 * 1024 * 1024))


def _nbytes(shape, dtype):
    return math.prod(shape) * jnp.dtype(dtype).itemsize


def _rms(x, g):
    return x * lax.rsqrt(jnp.mean(x * x, axis=-1, keepdims=True) + NORM_EPS) * g


def _dot(a, b):
    return jnp.dot(a, b, preferred_element_type=F32)


def _dot_nt(a, b):
    return lax.dot_general(a, b, (((1,), (1,)), ((), ())), preferred_element_type=F32)


def _silu(x):
    return (0.5 * x) * (1.0 + jnp.tanh(0.5 * x))


def _causal_dwconv(u, prev, cw_ref, cb_ref, cs):
    width = cw_ref.shape[0]
    acc = cb_ref[:, cs] + u * cw_ref[width - 1:width, cs]
    rows = lax.broadcasted_iota(jnp.int32, prev.shape, 0)
    for j in range(width - 1):
        d = width - 1 - j
        rolled = pltpu.roll(u, d, 0)
        head = jnp.where(rows < d, pltpu.roll(prev, d, 0), rolled[0:V7X_SUBLANES])
        shifted = jnp.concatenate([head, rolled[V7X_SUBLANES:]], axis=0)
        acc = acc + shifted * cw_ref[j:j + 1, cs]
    return acc


def _log_sigmoid(x):
    return jnp.minimum(x, 0.0) - jnp.log1p(jnp.exp(-jnp.abs(x)))


def _split3(x):
    hi = x.astype(BF16)
    r1 = x - hi.astype(F32)
    mid = r1.astype(BF16)
    lo = (r1 - mid.astype(F32)).astype(BF16)
    return hi, mid, lo


def _norm_cast_kernel(x_ref, g_ref, o_ref):
    o_ref[...] = _rms(x_ref[...], g_ref[...]).astype(o_ref.dtype)


def _norm_cast(x2d, g, tb):
    rows, d = x2d.shape
    return pl.pallas_call(
        _norm_cast_kernel,
        grid=(rows // tb,),
        in_specs=[pl.BlockSpec((tb, d), lambda i: (i, 0)), pl.BlockSpec((1, d), lambda i: (0, 0))],
        out_specs=pl.BlockSpec((tb, d), lambda i: (i, 0)),
        out_shape=jax.ShapeDtypeStruct((rows, d), BF16),
        name="s5_norm",
    )(x2d, g)


def _s5_kernel(u_ref, tt_ref, pt_ref, qt_ref, are_ref, aim_ref, h0re_ref, h0im_ref,
               y_ref, hre_ref, him_ref, pure_s, puim_s, hinre_s, hinim_s, stre_s, stim_s, *, nb, rb):
    r = pl.program_id(1)

    @pl.when(r == 0)
    def _():
        stre_s[...] = h0re_ref[0]
        stim_s[...] = h0im_ref[0]

    u = u_ref[...]
    w = V7X_MXU
    half = pt_ref.shape[1] // 2
    pu = [_dot_nt(u[:, i * w:(i + 1) * w], pt_ref[i]) for i in range(2)]
    pure_s[...] = jnp.concatenate([pu[0][:, :half], pu[1][:, :half]], axis=1)
    puim_s[...] = jnp.concatenate([pu[0][:, half:], pu[1][:, half:]], axis=1)
    ar = jnp.broadcast_to(are_ref[0], (nb, 2 * half))
    ai = jnp.broadcast_to(aim_ref[0], (nb, 2 * half))

    def step(i, carry):
        re, im = carry
        rows = pl.ds(pl.multiple_of(i * nb, nb), nb)
        hinre_s[rows, :] = re
        hinim_s[rows, :] = im
        return (ar * re - ai * im + pure_s[rows, :], ar * im + ai * re + puim_s[rows, :])

    re, im = lax.fori_loop(0, rb // nb, step, (stre_s[...], stim_s[...]))
    stre_s[...] = re
    stim_s[...] = im
    hre_ref[0] = re
    him_ref[0] = im

    hre = hinre_s[...]
    him = hinim_s[...]
    for i in range(2):
        hin = jnp.concatenate([hre[:, i * half:(i + 1) * half], him[:, i * half:(i + 1) * half]], axis=1)
        y_ref[:, i * w:(i + 1) * w] = (_dot_nt(u[:, i * w:(i + 1) * w], tt_ref[i])
                                       + _dot_nt(hin.astype(BF16), qt_ref[i])).astype(y_ref.dtype)


def _s5_core(ut, mats, h0re, h0im, nb, rb):
    rows, cols = ut.shape
    pairs = cols // (2 * V7X_MXU)
    tt, pt, qt, are, aim = mats
    nst = pt.shape[1]
    pair3 = lambda p, r: (p, 0, 0)
    kern = functools.partial(_s5_kernel, nb=nb, rb=rb)
    blocks = (_nbytes((rb, 512), BF16) * 2 + _nbytes((2, 256, 256), BF16) + 4 * _nbytes((256, 128), BF16))
    scratch = 4 * _nbytes((rb, 128), F32) + 2 * _nbytes((nb, 128), F32)
    return pl.pallas_call(
        kern,
        grid=(pairs, rows // rb),
        in_specs=[
            pl.BlockSpec((rb, 2 * V7X_MXU), lambda p, r: (r, p)),
            pl.BlockSpec((2, V7X_MXU, V7X_MXU), pair3),
            pl.BlockSpec((2, nst, V7X_MXU), pair3),
            pl.BlockSpec((2, V7X_MXU, nst), pair3),
            pl.BlockSpec((1, 1, 128), pair3),
            pl.BlockSpec((1, 1, 128), pair3),
            pl.BlockSpec((1, nb, 128), pair3),
            pl.BlockSpec((1, nb, 128), pair3),
        ],
        out_specs=[
            pl.BlockSpec((rb, 2 * V7X_MXU), lambda p, r: (r, p)),
            pl.BlockSpec((1, nb, 128), pair3),
            pl.BlockSpec((1, nb, 128), pair3),
        ],
        out_shape=[
            jax.ShapeDtypeStruct((rows, cols), BF16),
            jax.ShapeDtypeStruct((pairs, nb, 128), F32),
            jax.ShapeDtypeStruct((pairs, nb, 128), F32),
        ],
        scratch_shapes=[pltpu.VMEM((rb, 128), F32)] * 4 + [pltpu.VMEM((nb, 128), F32)] * 2,
        compiler_params=pltpu.CompilerParams(
            dimension_semantics=("arbitrary", "arbitrary"), vmem_limit_bytes=_vmem_limit(blocks, scratch)),
        name="s5_core",
    )(ut, tt, pt, qt, are, aim, h0re, h0im)


def _s5_fused_kernel(x_ref, g_ref, tt_ref, pt_ref, qt_ref, are_ref, aim_ref, h0re_ref, h0im_ref,
                     y_ref, hre_ref, him_ref,
                     us, ugt, pure, puim, hinre, hinim, ytmp, ys, stre, stim, *, nb, tbt, spb):
    lc = S5_CHUNK
    ch = S5_GROUP_CH
    gps = V7X_LANES // ch
    nch = tbt // lc
    cols = nb * nch
    k = pl.program_id(1)
    ppg = gps // 2
    slab = lambda sl: k * spb + sl

    @pl.when(k == 0)
    def _():
        for b in range(nb):
            u = _rms(x_ref[b], g_ref[...])
            for kk in range(us.shape[0]):
                us[kk, pl.ds(b, tbt, stride=nb), :] = u[:, kk * V7X_LANES:(kk + 1) * V7X_LANES]

    @pl.when(pl.program_id(0) == 0)
    def _():
        for sl in range(spb):
            stre[slab(sl)] = h0re_ref[sl * ppg:(sl + 1) * ppg]
            stim[slab(sl)] = h0im_ref[sl * ppg:(sl + 1) * ppg]

    tile = lambda j, s: slice((j * lc + s) * nb, (j * lc + s + 1) * nb)
    for sl in range(spb):
        for s in range(lc):
            a = jnp.concatenate([us[slab(sl), tile(j, s), :] for j in range(nch)], axis=0).T
            for g in range(gps):
                ugt[sl * gps + g, (lc - 1 - s) * ch:(lc - s) * ch, :] = a[g * ch:(g + 1) * ch, :].astype(BF16)

    half = pt_ref.shape[1] // 2
    for p in range(spb * ppg):
        put0 = _dot(pt_ref[2 * p], ugt[2 * p])
        put1 = _dot(pt_ref[2 * p + 1], ugt[2 * p + 1])
        pure[p] = jnp.concatenate([put0[:half], put1[:half]], axis=0).T
        puim[p] = jnp.concatenate([put0[half:], put1[half:]], axis=0).T

    chains = [(sl, pr) for sl in range(spb) for pr in range(ppg)]
    state = [(stre[slab(sl), pr], stim[slab(sl), pr]) for sl, pr in chains]
    coef = [(jnp.broadcast_to(are_ref[sl * ppg + pr], (nb, 2 * half)),
             jnp.broadcast_to(aim_ref[sl * ppg + pr], (nb, 2 * half))) for sl, pr in chains]
    for j in range(nch):
        rows = slice(j * nb, (j + 1) * nb)
        for c, (sl, pr) in enumerate(chains):
            p = sl * ppg + pr
            re, im = state[c]
            ar, ai = coef[c]
            hinre[p, rows, :] = re
            hinim[p, rows, :] = im
            state[c] = (ar * re - ai * im + pure[p, rows, :], ar * im + ai * re + puim[p, rows, :])
    for c, (sl, pr) in enumerate(chains):
        re, im = state[c]
        stre[slab(sl), pr] = re
        stim[slab(sl), pr] = im
        hre_ref[slab(sl) * ppg + pr] = re
        him_ref[slab(sl) * ppg + pr] = im

    for p in range(spb * ppg):
        hre_t = hinre[p].T
        him_t = hinim[p].T
        for i in range(2):
            g = 2 * p + i
            hin_t = jnp.concatenate([hre_t[i * half:(i + 1) * half], him_t[i * half:(i + 1) * half]],
                                    axis=0).astype(BF16)
            ytmp[g] = _dot(tt_ref[g], ugt[g]) + _dot(qt_ref[g], hin_t)

    for sl in range(spb):
        for t in range(lc):
            zt = jnp.concatenate([ytmp[sl * gps + g, t * ch:(t + 1) * ch, :] for g in range(gps)], axis=0).T
            for j in range(nch):
                ys[sl, tile(j, t), :] = zt[j * nb:(j + 1) * nb, :]
    for sl in range(spb):
        for b in range(nb):
            y_ref[b, :, sl * V7X_LANES:(sl + 1) * V7X_LANES] = ys[sl, pl.ds(b, tbt, stride=nb), :].astype(y_ref.dtype)


def _s5_fused(x, g, mats_t, h0re, h0im, tbt):
    nb, s, d = x.shape
    tt, pt, qt, are, aim = mats_t
    groups = tt.shape[0]
    gps = V7X_LANES // S5_GROUP_CH
    slabs = groups // gps
    w = S5_CHUNK * S5_GROUP_CH
    nst = pt.shape[1]
    cols = nb * (tbt // S5_CHUNK)
    spb = S5_SLABS_PER_STEP
    ng, npair = spb * gps, spb * gps // 2
    kern = functools.partial(_s5_fused_kernel, nb=nb, tbt=tbt, spb=spb)
    slab3 = lambda t, k: (k, 0, 0)
    blocks = (_nbytes((nb, tbt, d), F32) + _nbytes((nb, tbt, spb * V7X_LANES), BF16)
              + ng * (_nbytes((w, w), BF16) + 2 * _nbytes((w, nst), BF16)))
    scratch = (_nbytes((slabs, nb * tbt, V7X_LANES), F32) + _nbytes((ng, w, cols), BF16)
               + 4 * _nbytes((npair, cols, nst), F32) + _nbytes((ng, w, cols), F32)
               + _nbytes((spb, nb * tbt, V7X_LANES), F32))
    return pl.pallas_call(
        kern,
        grid=(s // tbt, slabs // spb),
        in_specs=[
            pl.BlockSpec((nb, tbt, d), lambda t, k: (0, t, 0)),
            pl.BlockSpec((1, d), lambda t, k: (0, 0)),
            pl.BlockSpec((ng, w, w), slab3),
            pl.BlockSpec((ng, nst, w), slab3),
            pl.BlockSpec((ng, w, nst), slab3),
            pl.BlockSpec((npair, 1, nst), slab3),
            pl.BlockSpec((npair, 1, nst), slab3),
            pl.BlockSpec((npair, nb, nst), slab3),
            pl.BlockSpec((npair, nb, nst), slab3),
        ],
        out_specs=[
            pl.BlockSpec((nb, tbt, spb * V7X_LANES), lambda t, k: (0, t, k)),
            pl.BlockSpec((groups // 2, nb, nst), lambda t, k: (0, 0, 0)),
            pl.BlockSpec((groups // 2, nb, nst), lambda t, k: (0, 0, 0)),
        ],
        out_shape=[
            jax.ShapeDtypeStruct((nb, s, d), BF16),
            jax.ShapeDtypeStruct((groups // 2, nb, nst), F32),
            jax.ShapeDtypeStruct((groups // 2, nb, nst), F32),
        ],
        scratch_shapes=[
            pltpu.VMEM((slabs, nb * tbt, V7X_LANES), F32),
            pltpu.VMEM((ng, w, cols), BF16),
            pltpu.VMEM((npair, cols, nst), F32),
            pltpu.VMEM((npair, cols, nst), F32),
            pltpu.VMEM((npair, cols, nst), F32),
            pltpu.VMEM((npair, cols, nst), F32),
            pltpu.VMEM((ng, w, cols), F32),
            pltpu.VMEM((spb, nb * tbt, V7X_LANES), F32),
            pltpu.VMEM((slabs, gps // 2, nb, nst), F32),
            pltpu.VMEM((slabs, gps // 2, nb, nst), F32),
        ],
        compiler_params=pltpu.CompilerParams(
            dimension_semantics=("arbitrary", "arbitrary"), vmem_limit_bytes=_vmem_limit(blocks, scratch)),
        name="s5_fused",
    )(x, g, tt, pt, qt, are, aim, h0re, h0im)


def _s5_matrices(a_re, a_im, log_dt, b_re, b_im, c_re, c_im):
    lc = S5_CHUNK
    groups, n = a_re.shape
    ch = S5_GROUP_CH
    f = lambda v: v.astype(F32)
    a_re, a_im, b_re, b_im, c_re, c_im = map(f, (a_re, a_im, b_re, b_im, c_re, c_im))
    dt = jnp.exp(f(log_dt))[:, None, None]
    j = jnp.arange(lc + 1, dtype=F32)
    mag = jnp.exp((a_re[:, :, None] * dt) * j)
    ang = (a_im[:, :, None] * dt) * j
    pw_re, pw_im = mag * jnp.cos(ang), mag * jnp.sin(ang)
    e_re, e_im = pw_re[:, :, 1] - 1.0, pw_im[:, :, 1]
    inv = 1.0 / (a_re * a_re + a_im * a_im)
    f_re, f_im = (e_re * a_re + e_im * a_im) * inv, (e_im * a_re - e_re * a_im) * inv
    bb_re = f_re[..., None] * b_re - f_im[..., None] * b_im
    bb_im = f_re[..., None] * b_im + f_im[..., None] * b_re
    rep = lambda v: jnp.repeat(v, ch, axis=2)
    til = lambda v: jnp.tile(v, (1, 1, lc + 1))
    lb_re = rep(pw_re) * til(bb_re) - rep(pw_im) * til(bb_im)
    lb_im = rep(pw_re) * til(bb_im) + rep(pw_im) * til(bb_re)
    kern = jnp.einsum('gcm,gmx->gcx', jnp.concatenate([c_re, c_im], axis=2),
                      jnp.concatenate([lb_re, -lb_im], axis=1), precision=HI)
    w = lc * ch
    kfwd = jnp.pad(kern[:, :, :w], ((0, 0), (0, 0), ((lc - 1) * ch, 0)))
    tt = jnp.stack([kfwd[:, :, t * ch:t * ch + w] for t in range(lc)], axis=1).reshape(groups, w, w)
    pt = jnp.concatenate([lb_re[:, :, :w], lb_im[:, :, :w]], axis=1)
    pwt_re = jnp.swapaxes(pw_re, 1, 2)[:, 1:, None, :]
    pwt_im = jnp.swapaxes(pw_im, 1, 2)[:, 1:, None, :]
    ca_re = c_re[:, None] * pwt_re - c_im[:, None] * pwt_im
    ca_im = c_re[:, None] * pwt_im + c_im[:, None] * pwt_re
    qt = jnp.concatenate([ca_re, -ca_im], axis=3).reshape(groups, w, 2 * n)
    are = pw_re[:, :, lc].reshape(groups // 2, 1, 2 * n)
    aim = pw_im[:, :, lc].reshape(groups // 2, 1, 2 * n)
    return tt.astype(BF16), pt.astype(BF16), qt.astype(BF16), are, aim


def _glu_kernel(x_ref, y_ref, g0_ref, d_ref, w_ref, g1_ref, o_ref, out_s):
    x = x_ref[...]
    d = x.shape[1]
    u = _rms(x, g0_ref[...])
    yy = y_ref[...].astype(F32) + d_ref[...] * u
    z = jax.nn.gelu(yy, approximate=True).astype(BF16)
    w = V7X_MXU
    nslab = d // w
    pair = lambda j: (_dot(z, w_ref[:, j * w:(j + 1) * w]), _dot(z, w_ref[:, d + j * w:d + (j + 1) * w]))
    nxt = pair(0)
    ssq = jnp.zeros((x.shape[0], 1), F32)
    for j in range(nslab):
        a, b = nxt
        if j + 1 < nslab:
            nxt = pair(j + 1)
        out = a * (0.5 * (1.0 + jnp.tanh(0.5 * b)))
        out_s[:, j * w:(j + 1) * w] = out
        ssq = ssq + jnp.sum(out * out, axis=-1, keepdims=True)
    scale = lax.rsqrt(ssq * (1.0 / d) + NORM_EPS)
    o_ref[...] = x + out_s[...] * scale * g1_ref[...]


def _const_spec(shape):
    nd = len(shape)
    return pl.BlockSpec(shape, lambda *_: (0,) * nd, pipeline_mode=pl.Buffered(1))


def _glu(x2d, y2d, g0, dskip, w, g1, tb):
    rows, d = x2d.shape
    blocks = 2 * _nbytes((tb, d), F32) + _nbytes((tb, d), BF16)
    scratch = _nbytes(w.shape, BF16) + 3 * _nbytes((tb, 2 * d), F32)
    row = pl.BlockSpec((tb, d), lambda i: (i, 0))
    return pl.pallas_call(
        _glu_kernel,
        grid=(rows // tb,),
        in_specs=[row, row, _const_spec((1, d)), _const_spec((1, d)), _const_spec(w.shape), _const_spec((1, d))],
        out_specs=row,
        out_shape=jax.ShapeDtypeStruct((rows, d), F32),
        scratch_shapes=[pltpu.VMEM((tb, d), F32)],
        compiler_params=pltpu.CompilerParams(
            dimension_semantics=("arbitrary",), vmem_limit_bytes=_vmem_limit(blocks, scratch)),
        name="s5_glu",
    )(x2d, y2d, g0, dskip, w, g1)


def _ffn_kernel(x_ref, buf_ref, g2_ref, wup_ref, cw_ref, cb_ref, wdown_ref, g3_ref,
                o_ref, nbuf_ref, carry, act, xprev, *, tb, dff, width):
    t = pl.program_id(1)
    last = pl.num_programs(1) - 1
    pad = V7X_SUBLANES

    @pl.when(t == 0)
    def _():
        carry[...] = jnp.zeros_like(carry)
        carry[pad - (width - 1):pad, :] = buf_ref[0]
        act[...] = jnp.zeros_like(act)
        xprev[...] = jnp.zeros_like(xprev)

    slot = lax.rem(t, 2)
    f = _dot(act[1 - slot], wdown_ref[...])
    o_ref[0] = xprev[...] + _rms(f, g3_ref[...])

    fb = V7X_MXU
    nslab = dff // fb
    cols = lambda j: (slice(j * fb, (j + 1) * fb), slice(dff + j * fb, dff + (j + 1) * fb))

    def conv(u, cs):
        prev = carry[:, cs]
        carry[:, cs] = u[tb - pad:tb, :]
        return _causal_dwconv(u, prev, cw_ref, cb_ref, cs)

    x = x_ref[0]
    xprev[...] = x
    hn = _rms(x, g2_ref[...]).astype(BF16)
    up = lambda j: tuple(_dot(hn, wup_ref[:, cs]) for cs in cols(j))
    nxt = up(0)
    for j in range(nslab):
        ug, uv = nxt
        if j + 1 < nslab:
            nxt = up(j + 1)
        gate = conv(ug, cols(j)[0])
        val = conv(uv, cols(j)[1])
        act[slot, :, j * fb:(j + 1) * fb] = (jax.nn.gelu(gate, approximate=True) * val).astype(BF16)

    @pl.when(t == last - 1)
    def _():
        nbuf_ref[0] = carry[pad - (width - 1):pad, :]


def _layer_spec(shape, layer):
    nd = len(shape) - 1
    return pl.BlockSpec((None,) + tuple(shape[1:]), lambda *_: (layer,) + (0,) * nd, pipeline_mode=pl.Buffered(1))


def _ffn(x, buf, g2, wup, cw, cb, wdown, g3, layer, tb):
    b, s, d = x.shape
    dff = wdown.shape[1]
    width = cw.shape[1]
    nt = s // tb
    kern = functools.partial(_ffn_kernel, tb=tb, dff=dff, width=width)
    blocks = 2 * _nbytes((tb, d), F32)
    scratch = (_nbytes(wup.shape[1:], BF16) + _nbytes(wdown.shape[1:], BF16) + 6 * _nbytes((tb, V7X_MXU), F32)
               + _nbytes((8, 2 * dff), F32) + 2 * _nbytes((tb, dff), BF16) + 4 * _nbytes((tb, d), F32))
    x_in = pl.BlockSpec((1, tb, d), lambda i, t: (i, jnp.minimum(t, nt - 1), 0))
    x_out = pl.BlockSpec((1, tb, d), lambda i, t: (i, jnp.maximum(t - 1, 0), 0))
    bs = pl.BlockSpec((1, width - 1, 2 * dff), lambda i, t: (i, 0, 0))
    return pl.pallas_call(
        kern,
        grid=(b, nt + 1),
        in_specs=[x_in, bs, _const_spec((1, d)), _layer_spec(wup.shape, layer), _layer_spec(cw.shape, layer),
                  _layer_spec(cb.shape, layer), _layer_spec(wdown.shape, layer), _const_spec((1, d))],
        out_specs=[x_out, bs],
        out_shape=[jax.ShapeDtypeStruct((b, s, d), F32), jax.ShapeDtypeStruct((b, width - 1, 2 * dff), F32)],
        scratch_shapes=[pltpu.VMEM((V7X_SUBLANES, 2 * dff), F32), pltpu.VMEM((2, tb, dff), BF16),
                        pltpu.VMEM((tb, d), F32)],
        compiler_params=pltpu.CompilerParams(
            dimension_semantics=("arbitrary", "arbitrary"), vmem_limit_bytes=_vmem_limit(blocks, scratch)),
        name="conv_ffn",
    )(x, buf, g2, wup, cw, cb, wdown, g3)


def _mlstm_weights_kernel(aq_ref, ak_ref, av_ref, gq_ref, gk_ref, gv_ref, wup_ref,
                          wq_ref, wkt_ref, wv_ref, gc_ref, gh_ref, *, kscale):
    w = V7X_MXU
    kb = QKV_BLOCK
    dotp = functools.partial(jnp.dot, preferred_element_type=F32, precision=HI)
    lane = lax.broadcasted_iota(jnp.int32, (V7X_LANES, w), 1)
    src = lax.broadcasted_iota(jnp.int32, (V7X_LANES, w), 0)
    spread = jnp.where(jnp.bitwise_and(lane, kb - 1) == src, 1.0, 0.0)
    shift = kb.bit_length() - 1
    rblk = lax.shift_right_logical(lax.broadcasted_iota(jnp.int32, (w, w), 0), shift)
    cblk = lax.shift_right_logical(lax.broadcasted_iota(jnp.int32, (w, w), 1), shift)
    tile = lambda a_ref: jnp.where(rblk == cblk, dotp(a_ref[0], spread), 0.0)
    tq, tk, tv = tile(aq_ref), tile(ak_ref), tile(av_ref)
    wq_ref[0] = tq.astype(BF16)
    wkt_ref[0] = (tk.T * kscale).astype(BF16)
    wv_ref[0] = tv.astype(BF16)
    gc_ref[...] = (dotp(tq, gq_ref[...]) + dotp(tk, gk_ref[...])).astype(BF16)

    @pl.when(pl.program_id(0) == 0)
    def _():
        gh_ref[...] = jnp.zeros_like(gh_ref)

    gh_ref[...] += dotp(wup_ref[...], dotp(tv, gv_ref[...]))


def _mlstm_weights(wq, wk, wv, w_gate, w_up, kscale):
    nblk, kb, _ = wq.shape
    assert kb == QKV_BLOCK and kb & (kb - 1) == 0
    w = V7X_MXU
    per = w // kb
    nt = nblk // per
    inner = nt * w
    compact = lambda m: jnp.pad(m.astype(F32).reshape(nt, w, kb), ((0, 0), (0, 0), (0, V7X_LANES - kb)))
    wg = jnp.pad(w_gate.astype(F32), ((0, 0), (0, V7X_LANES - w_gate.shape[1])))
    cm = pl.BlockSpec((1, w, V7X_LANES), lambda i: (i, 0, 0))
    gs = [pl.BlockSpec((w, V7X_LANES), lambda i, k=k: (k * nt + i, 0)) for k in range(3)]
    tile = pl.BlockSpec((1, w, w), lambda i: (i, 0, 0))
    fold = pl.BlockSpec((w, V7X_LANES), lambda i: (i, 0))
    d = w_up.shape[0]
    return pl.pallas_call(
        functools.partial(_mlstm_weights_kernel, kscale=kscale),
        grid=(nt,),
        in_specs=[cm, cm, cm] + gs + [pl.BlockSpec((d, w), lambda i: (0, i))],
        out_specs=[tile, tile, tile, fold, pl.BlockSpec((d, V7X_LANES), lambda i: (0, 0))],
        out_shape=[jax.ShapeDtypeStruct((nt, w, w), BF16)] * 3
        + [jax.ShapeDtypeStruct((inner, V7X_LANES), BF16), jax.ShapeDtypeStruct((d, V7X_LANES), F32)],
        compiler_params=pltpu.CompilerParams(dimension_semantics=("arbitrary",)),
        name="mlstm_weights",
    )(compact(wq), compact(wk), compact(wv), wg, wg, wg, w_up.astype(F32))


def _mlstm_chunk(q_ref, kt_ref, v_ref, g, gt, ct_s, n_s, m_s, fill, emit_hn, fill_update, *, lc, heads, dh):
    row = lax.broadcasted_iota(jnp.int32, (lc, lc), 0)
    col = lax.broadcasted_iota(jnp.int32, (lc, lc), 1)
    causal = col <= row
    tri_l = jnp.where(causal, 1.0, 0.0).astype(BF16)
    tri_u = jnp.where(row <= col, 1.0, 0.0).astype(BF16)
    b_cols = sum(_dot(tri_l, part) for part in _split3(_log_sigmoid(g)))
    b_rows = sum(_dot(part, tri_u) for part in _split3(_log_sigmoid(gt)))

    hsl = [slice(h * dh, (h + 1) * dh) for h in range(heads)]
    st = []
    for h in range(heads):
        b_col = b_cols[:, heads + h:heads + h + 1]
        b_row = b_rows[heads + h:heads + h + 1, :]
        i_row = gt[h:h + 1, :]
        m_prev = m_s[h:h + 1, 0:1]
        dmat = jnp.where(causal, b_col - b_row + i_row, -jnp.inf)
        inter = b_col + m_prev
        m_t = jnp.maximum(inter, jnp.max(dmat, axis=1, keepdims=True))
        st.append(dict(b_col=b_col, b_row=b_row, i_row=i_row, m_prev=m_prev, m_t=m_t,
                       wts=jnp.exp(dmat - m_t), a=jnp.exp(inter - m_t)))
        fill()
    for h in range(heads):
        s = st[h]
        s['sc'] = _dot(q_ref[0, :, hsl[h]], kt_ref[0, hsl[h], :]) * s['wts']
    out_mm = lambda h: (_dot(st[h]['sc'].astype(BF16), v_ref[0, :, hsl[h]]),
                        _dot(q_ref[0, :, hsl[h]], ct_s[h].astype(BF16)))
    nxt = out_mm(0)
    for h in range(heads):
        intra, inter_mm = nxt
        if h + 1 < heads:
            nxt = out_mm(h + 1)
        fill()
        s = st[h]
        qh = q_ref[0, :, hsl[h]]
        sc = s['sc']
        a = s['a']
        num = intra + a * inter_mm
        qn = jnp.sum(qh.astype(F32) * n_s[h:h + 1, :], axis=1, keepdims=True)
        den = jnp.sum(sc, axis=1, keepdims=True) + a * qn
        hh = num / jnp.maximum(jnp.abs(den), jnp.exp(-s['m_t']))
        mu = jnp.mean(hh, axis=1, keepdims=True)
        dev = hh - mu
        var = jnp.mean(dev * dev, axis=1, keepdims=True)
        emit_hn(h, dev * lax.rsqrt(var + LN_EPS))
    def update_mm(h):
        s = st[h]
        kth = kt_ref[0, hsl[h], :]
        b_last = s['b_row'][:, lc - 1:lc]
        g_row = b_last - s['b_row'] + s['i_row']
        m_new = jnp.maximum(b_last + s['m_prev'], jnp.max(g_row, axis=1, keepdims=True))
        decay = jnp.exp(b_last + s['m_prev'] - m_new)
        wg_row = jnp.exp(g_row - m_new)
        wg_col = jnp.exp(b_last - s['b_col'] + g[:, h:h + 1] - m_new)
        wv = (v_ref[0, :, hsl[h]].astype(F32) * wg_col).astype(BF16)
        wg16 = jnp.broadcast_to(wg_row, (16, lc)).astype(BF16)
        return m_new, decay, _dot(kth, wv), _dot_nt(wg16, kth)[0:1, :]

    fill_update(0)
    nxt = update_mm(0)
    for h in range(heads):
        m_new, decay, c_upd, n_upd = nxt
        if h + 1 < heads:
            fill_update(h + 1)
            nxt = update_mm(h + 1)
        ct_s[h] = decay * ct_s[h] + c_upd
        n_s[h:h + 1, :] = decay * n_s[h:h + 1, :] + n_upd
        m_s[h:h + 1, :] = jnp.broadcast_to(m_new, (1, V7X_LANES))


def _mlstm_layer_kernel(*refs, lc, heads, inner, width, zero_init):
    (x_ref, cbuf_ref, g0_ref, wup_ref, cw_ref, cb_ref, wq_ref, wkt_ref, wv_ref, gc_ref, gh_ref, bg_ref,
     ng_ref, sk_ref, wdown_ref, g1_ref) = refs[:16]
    n_in = 16
    if not zero_init:
        c0_ref, n0_ref, m0_ref = refs[16:19]
        n_in = 19
    o_ref, cout_ref, nout_ref, mout_ref, nbuf_ref = refs[n_in:n_in + 5]
    carry, q_s, kt_s, v_s, xc_s, sz_s, act_s, ct_s, n_s, m_s = refs[n_in + 5:]
    dh = inner // heads
    pad = V7X_SUBLANES
    c = pl.program_id(1)

    @pl.when(c == 0)
    def _():
        carry[...] = jnp.zeros_like(carry)
        carry[pad - (width - 1):pad, :] = cbuf_ref[0]
        if zero_init:
            ct_s[...] = jnp.zeros_like(ct_s)
            n_s[...] = jnp.zeros_like(n_s)
            m_s[...] = jnp.zeros_like(m_s)
        else:
            for h in range(heads):
                ct_s[h] = c0_ref[0, h].T
            n_s[...] = n0_ref[0]
            m_s[...] = m0_ref[0]

    x = x_ref[0]
    hx = _rms(x, g0_ref[...]).astype(BF16)
    w = V7X_MXU
    gacc = bg_ref[...] + _dot(hx, gh_ref[...])
    nslab = inner // w
    col = lambda i: slice(i * w, (i + 1) * w)

    def project(i, xc, xmb, gacc):
        q_s[0, :, col(i)] = _dot(xc, wq_ref[i]).astype(BF16)
        kt_s[0, col(i), :] = _dot_nt(wkt_ref[i], xc).astype(BF16)
        v_s[0, :, col(i)] = _dot(xmb, wv_ref[i]).astype(BF16)
        xc_s[:, col(i)] = xc
        return gacc + _dot(xc, gc_ref[col(i), :])

    xm_next = _dot(hx, wup_ref[:, col(0)])
    pending = None
    for i in range(nslab):
        xm = xm_next
        if i + 1 < nslab:
            xm_next = _dot(hx, wup_ref[:, col(i + 1)])
        if pending is not None:
            gacc = project(*pending, gacc)
        prev = carry[:, col(i)]
        carry[:, col(i)] = xm[lc - pad:lc, :]
        xc = _silu(_causal_dwconv(xm, prev, cw_ref, cb_ref, col(i))).astype(BF16)
        pending = (i, xc, xm.astype(BF16))
    gacc = project(*pending, gacc)
    nbuf_ref[0] = carry[pad - (width - 1):pad, :]

    slabs = iter(range(inner // w))

    def gate_slab():
        i = next(slabs, None)
        if i is not None:
            cs = slice(i * w, (i + 1) * w)
            sz_s[:, cs] = _silu(_dot(hx, wup_ref[:, inner + i * w:inner + (i + 1) * w])).astype(BF16)

    hcols = lambda h: slice(h * dh, (h + 1) * dh)
    down = []

    def gated_out(h, hn):
        hs = hn * ng_ref[:, hcols(h)] + sk_ref[:, hcols(h)] * xc_s[:, hcols(h)].astype(F32)
        act_s[:, hcols(h)] = (hs * sz_s[:, hcols(h)].astype(F32)).astype(BF16)

    def down_proj(h):
        if h == 0:
            down.append(_dot(act_s[...], wdown_ref[...]))

    _mlstm_chunk(q_s, kt_s, v_s, gacc, gacc.T[0:2 * heads, :], ct_s, n_s, m_s, gate_slab, gated_out, down_proj,
                 lc=lc, heads=heads, dh=dh)
    for _ in slabs:
        raise AssertionError("fewer fill points than output-gate slabs")
    o_ref[0] = x + _rms(down[0], g1_ref[...])

    @pl.when(c == pl.num_programs(1) - 1)
    def _():
        for h in range(heads):
            cout_ref[0, h] = ct_s[h].T
        nout_ref[0] = n_s[...]
        mout_ref[0] = m_s[...]


def _mlstm_layer(x, cbuf, state, p, g0, g1, lc):
    b, s, d = x.shape
    heads = p['heads']
    inner = p['m_cw'].shape[1]
    width = p['m_cw'].shape[0]
    dh = inner // heads
    zero_init = state is None
    kern = functools.partial(_mlstm_layer_kernel, lc=lc, heads=heads, inner=inner, width=width,
                             zero_init=zero_init)
    xs = pl.BlockSpec((1, lc, d), lambda i, c: (i, c, 0))
    bufs = pl.BlockSpec((1, width - 1, inner), lambda i, c: (i, 0, 0))
    cs = pl.BlockSpec((1, heads, dh, dh), lambda i, c: (i, 0, 0, 0))
    ns = pl.BlockSpec((1, heads, dh), lambda i, c: (i, 0, 0))
    ms = pl.BlockSpec((1, heads, V7X_LANES), lambda i, c: (i, 0, 0))
    consts = [g0, p['m_wup'], p['m_cw'], p['m_cb'], p['m_wq'], p['m_wkt'], p['m_wv'], p['m_gc'], p['m_gh'],
              p['m_bg'], p['m_ng'], p['m_skip'], p['m_wdown'], g1]
    in_specs = [xs, bufs] + [_const_spec(a.shape) for a in consts]
    args = [x, cbuf] + consts
    if not zero_init:
        in_specs += [cs, ns, ms]
        args += list(state)
    blocks = 2 * _nbytes((lc, d), F32) + (1 if zero_init else 2) * _nbytes((heads, dh, dh), F32)
    scratch = (sum(_nbytes(a.shape, a.dtype) for a in consts) + 6 * _nbytes((lc, inner), BF16)
               + _nbytes((heads, dh, dh), F32) + 8 * _nbytes((lc, max(lc, dh)), F32) + 2 * _nbytes((dh, dh), F32))
    return pl.pallas_call(
        kern,
        grid=(b, s // lc),
        in_specs=in_specs,
        out_specs=[xs, cs, ns, ms, bufs],
        out_shape=[jax.ShapeDtypeStruct((b, s, d), F32), jax.ShapeDtypeStruct((b, heads, dh, dh), F32),
                   jax.ShapeDtypeStruct((b, heads, dh), F32), jax.ShapeDtypeStruct((b, heads, V7X_LANES), F32),
                   jax.ShapeDtypeStruct((b, width - 1, inner), F32)],
        scratch_shapes=[pltpu.VMEM((V7X_SUBLANES, inner), F32),
                        pltpu.VMEM((1, lc, inner), BF16), pltpu.VMEM((1, inner, lc), BF16),
                        pltpu.VMEM((1, lc, inner), BF16), pltpu.VMEM((lc, inner), BF16),
                        pltpu.VMEM((lc, inner), BF16), pltpu.VMEM((lc, inner), BF16),
                        pltpu.VMEM((heads, dh, dh), F32), pltpu.VMEM((heads, dh), F32),
                        pltpu.VMEM((heads, V7X_LANES), F32)],
        compiler_params=pltpu.CompilerParams(
            dimension_semantics=("arbitrary", "arbitrary"), vmem_limit_bytes=_vmem_limit(blocks, scratch)),
        name="mlstm_layer",
    )(*args)


def _prep_weights(w):
    row = lambda v: v.astype(F32).reshape(1, -1)
    p = {}
    p['norm_g'] = w['norm_g'].astype(F32)
    p['s5_mats'] = _s5_matrices(w['s5_A_re'][0], w['s5_A_im'][0], w['s5_log_dt'][0], w['s5_B_re'][0],
                                w['s5_B_im'][0], w['s5_C_re'][0], w['s5_C_im'][0])
    p['s5_D'] = row(w['s5_D'][0])
    p['s5_w_glu'] = w['s5_w_glu'][0].astype(BF16)
    inner = w['mlstm_conv_w'].shape[2]
    heads = w['mlstm_b_gate'].shape[1] // 2
    dh = inner // heads
    p['m_wq'], p['m_wkt'], p['m_wv'], p['m_gc'], gh = _mlstm_weights(
        w['mlstm_wq'][0], w['mlstm_wk'][0], w['mlstm_wv'][0], w['mlstm_w_gate'][0], w['mlstm_w_up'][0],
        dh ** -0.5)
    p['m_gh'] = gh.astype(BF16)
    bg = row(w['mlstm_b_gate'][0])
    p['m_bg'] = jnp.pad(bg, ((0, 0), (0, V7X_LANES - bg.shape[1])))
    p['m_wup'] = w['mlstm_w_up'][0].astype(BF16)
    p['m_cw'] = w['mlstm_conv_w'][0].astype(F32)
    p['m_cb'] = row(w['mlstm_conv_b'][0])
    p['m_ng'] = row(w['mlstm_norm_g'][0])
    p['m_skip'] = row(w['mlstm_skip'][0])
    p['m_wdown'] = w['mlstm_w_down'][0].astype(BF16)
    p['heads'] = heads
    p['f_wup'] = w['ffn_w_up'].astype(BF16)
    p['f_cw'] = w['ffn_conv_w'].astype(F32)
    p['f_cb'] = w['ffn_conv_b'].astype(F32)[:, None, :]
    p['f_wdown'] = w['ffn_w_down'].astype(BF16)
    return p


def _to_chunk_major(u2d, b, s):
    d = u2d.shape[1]
    g = d // S5_GROUP_CH
    u = u2d.reshape(b, s // S5_CHUNK, S5_CHUNK, g, S5_GROUP_CH)[:, :, ::-1]
    return jnp.transpose(u, (1, 0, 3, 2, 4)).reshape((s // S5_CHUNK) * b, d * S5_CHUNK)


def _to_token_major(yt, b, s):
    d = yt.shape[1] // S5_CHUNK
    g = d // S5_GROUP_CH
    y = yt.reshape(s // S5_CHUNK, b, g, S5_CHUNK, S5_GROUP_CH)
    return jnp.transpose(y, (1, 0, 3, 2, 4)).reshape(b * s, d)


def _trunk(x, st, p, tiles):
    b, s, d = x.shape
    ng = p['norm_g']
    g_of = lambda layer, k: ng[layer, k].reshape(1, d)
    zero_init = st is None
    groups = d // S5_GROUP_CH
    x2d = x.reshape(b * s, d)

    n_state = p['s5_mats'][1].shape[1] // 2
    if zero_init:
        h0re = h0im = jnp.zeros((groups // 2, b, 2 * n_state), F32)
    else:
        pair = lambda h: jnp.transpose(h.astype(F32).reshape(b, groups // 2, 2 * n_state), (1, 0, 2))
        h0re, h0im = pair(st['s5_re']), pair(st['s5_im'])
    if tiles.s5_time:
        y, hre, him = _s5_fused(x, g_of(0, 0), p['s5_mats'], h0re, h0im, tiles.s5_time)
        y2d = y.reshape(b * s, d)
    else:
        u = _norm_cast(x2d, g_of(0, 0), tiles.rows)
        yt, hre, him = _s5_core(_to_chunk_major(u, b, s), p['s5_mats'], h0re, h0im, b, tiles.s5_rows)
        y2d = _to_token_major(yt, b, s)
    unpair = lambda h: jnp.transpose(h, (1, 0, 2)).reshape(1, b, groups, n_state)
    o_re, o_im = unpair(hre), unpair(him)
    x2d = _glu(x2d, y2d, g_of(0, 0), p['s5_D'], p['s5_w_glu'], g_of(0, 1), tiles.rows)

    def ffn(x2d, layer):
        dff2 = p['f_wup'].shape[2]
        width = p['f_cw'].shape[1]
        buf = jnp.zeros((b, width - 1, dff2), F32) if zero_init else st['ffn_conv'][layer].astype(F32)
        xo, nbuf = _ffn(x2d.reshape(b, s, d), buf, g_of(layer, 2), p['f_wup'], p['f_cw'], p['f_cb'],
                        p['f_wdown'], g_of(layer, 3), layer, tiles.ffn_time)
        return xo.reshape(b * s, d), nbuf

    x2d, fbuf0 = ffn(x2d, 0)

    heads = p['heads']
    inner = p['m_cw'].shape[1]
    width = p['m_cw'].shape[0]
    cbuf = jnp.zeros((b, width - 1, inner), F32) if zero_init else st['mlstm_conv'].astype(F32)
    if zero_init:
        state = None
    else:
        state = (st['mlstm_C'].astype(F32), st['mlstm_n'].astype(F32),
                 jnp.broadcast_to(st['mlstm_m'].astype(F32)[:, :, None], (b, heads, V7X_LANES)))
    x3, c_out, n_out, m_out, ncbuf = _mlstm_layer(x2d.reshape(b, s, d), cbuf, state, p, g_of(1, 0), g_of(1, 1),
                                                  tiles.mlstm_chunk)
    x2d, fbuf1 = ffn(x3.reshape(b * s, d), 1)

    return (x2d.reshape(b, s, d), o_re, o_im, c_out[None], n_out[None], m_out[None, :, :, 0], ncbuf[None],
            jnp.stack([fbuf0, fbuf1], axis=0))


def kernel(x_prompt, x_sample, state_s5_re, state_s5_im, state_mlstm_C, state_mlstm_n, state_mlstm_m, state_mlstm_conv, state_ffn_conv, norm_g, s5_A_re, s5_A_im, s5_log_dt, s5_B_re, s5_B_im, s5_C_re, s5_C_im, s5_D, s5_w_glu, mlstm_w_up, mlstm_conv_w, mlstm_conv_b, mlstm_wq, mlstm_wk, mlstm_wv, mlstm_w_gate, mlstm_b_gate, mlstm_norm_g, mlstm_skip, mlstm_w_down, ffn_w_up, ffn_conv_w, ffn_conv_b, ffn_w_down):
    w = {
        'norm_g': norm_g, 's5_A_re': s5_A_re, 's5_A_im': s5_A_im, 's5_log_dt': s5_log_dt,
        's5_B_re': s5_B_re, 's5_B_im': s5_B_im, 's5_C_re': s5_C_re, 's5_C_im': s5_C_im, 's5_D': s5_D,
        's5_w_glu': s5_w_glu, 'mlstm_w_up': mlstm_w_up, 'mlstm_conv_w': mlstm_conv_w,
        'mlstm_conv_b': mlstm_conv_b, 'mlstm_wq': mlstm_wq, 'mlstm_wk': mlstm_wk, 'mlstm_wv': mlstm_wv,
        'mlstm_w_gate': mlstm_w_gate, 'mlstm_b_gate': mlstm_b_gate, 'mlstm_norm_g': mlstm_norm_g,
        'mlstm_skip': mlstm_skip, 'mlstm_w_down': mlstm_w_down, 'ffn_w_up': ffn_w_up,
        'ffn_conv_w': ffn_conv_w, 'ffn_conv_b': ffn_conv_b, 'ffn_w_down': ffn_w_down,
    }
    assert norm_g.shape[0] == 2 and s5_A_re.shape[0] == 1 and mlstm_w_up.shape[0] == 1
    assert x_prompt.shape[0] == V7X_SUBLANES and x_sample.shape[0] == V7X_SUBLANES
    p = _prep_weights(w)
    out_p = _trunk(x_prompt.astype(F32), None, p, _pick_tiles(*x_prompt.shape[:2]))
    st = {'s5_re': state_s5_re[0], 's5_im': state_s5_im[0], 'mlstm_C': state_mlstm_C[0],
          'mlstm_n': state_mlstm_n[0], 'mlstm_m': state_mlstm_m[0], 'mlstm_conv': state_mlstm_conv[0],
          'ffn_conv': state_ffn_conv}
    out_s = _trunk(x_sample.astype(F32), st, p, _pick_tiles(*x_sample.shape[:2]))
    return (out_p[0], out_s[0]) + tuple(out_p[1:]) + tuple(out_s[1:])
```

```python
import functools
import math
from typing import NamedTuple

import jax
import jax.numpy as jnp
from jax import lax
from jax.experimental import pallas as pl
from jax.experimental.pallas import tpu as pltpu

F32 = jnp.float32
BF16 = jnp.bfloat16
HI = lax.Precision.HIGHEST

NORM_EPS = 1e-6
LN_EPS = 1e-5
S5_GROUP_CH = 16
S5_CHUNK = 16
QKV_BLOCK = 4
S5_SLABS_PER_STEP = 2
FFN_ROW_BLOCK = 1024

V7X_LANES = 128
V7X_SUBLANES = 8
V7X_MXU = 256
V7X_VMEM_BYTES = 64 * 1024 * 1024


class Tiles(NamedTuple):
    rows: int
    time: int
    s5_rows: int
    mlstm_chunk: int
    s5_time: int
    ffn_time: int


def _pick_tiles(batch, seq):
    s5_time = (V7X_LANES // batch) * S5_CHUNK
    return Tiles(rows=min(batch * seq, 512), time=min(seq, 512),
                 s5_rows=min(batch * (seq // S5_CHUNK), 512), mlstm_chunk=min(seq, 256),
                 s5_time=s5_time if seq % s5_time == 0 else 0, ffn_time=min(seq, 1024))


def _vmem_limit(block_bytes, scratch_bytes):
    want = 2 * block_bytes + scratch_bytes + 16 * 1024 * 1024
    return int(min(want, V7X_VMEM_BYTES - 8 * 1024 * 1024))


def _nbytes(shape, dtype):
    return math.prod(shape) * jnp.dtype(dtype).itemsize


def _rms(x, g):
    return x * lax.rsqrt(jnp.mean(x * x, axis=-1, keepdims=True) + NORM_EPS) * g


def _dot(a, b):
    return jnp.dot(a, b, preferred_element_type=F32)


def _dot_nt(a, b):
    return lax.dot_general(a, b, (((1,), (1,)), ((), ())), preferred_element_type=F32)


def _silu(x):
    return (0.5 * x) * (1.0 + jnp.tanh(0.5 * x))


def _causal_dwconv(u, prev, cw_ref, cb_ref, cs):
    width = cw_ref.shape[0]
    acc = cb_ref[:, cs] + u * cw_ref[width - 1:width, cs]
    rows = lax.broadcasted_iota(jnp.int32, prev.shape, 0)
    for j in range(width - 1):
        d = width - 1 - j
        rolled = pltpu.roll(u, d, 0)
        head = jnp.where(rows < d, pltpu.roll(prev, d, 0), rolled[0:V7X_SUBLANES])
        shifted = jnp.concatenate([head, rolled[V7X_SUBLANES:]], axis=0)
        acc = acc + shifted * cw_ref[j:j + 1, cs]
    return acc


def _log_sigmoid(x):
    return jnp.minimum(x, 0.0) - jnp.log1p(jnp.exp(-jnp.abs(x)))


def _split3(x):
    hi = x.astype(BF16)
    r1 = x - hi.astype(F32)
    mid = r1.astype(BF16)
    lo = (r1 - mid.astype(F32)).astype(BF16)
    return hi, mid, lo


def _norm_cast_kernel(x_ref, g_ref, o_ref):
    o_ref[...] = _rms(x_ref[...], g_ref[...]).astype(o_ref.dtype)


def _norm_cast(x2d, g, tb):
    rows, d = x2d.shape
    return pl.pallas_call(
        _norm_cast_kernel,
        grid=(rows // tb,),
        in_specs=[pl.BlockSpec((tb, d), lambda i: (i, 0)), pl.BlockSpec((1, d), lambda i: (0, 0))],
        out_specs=pl.BlockSpec((tb, d), lambda i: (i, 0)),
        out_shape=jax.ShapeDtypeStruct((rows, d), BF16),
        name="s5_norm",
    )(x2d, g)


def _s5_kernel(u_ref, tt_ref, pt_ref, qt_ref, are_ref, aim_ref, h0re_ref, h0im_ref,
               y_ref, hre_ref, him_ref, pure_s, puim_s, hinre_s, hinim_s, stre_s, stim_s, *, nb, rb):
    r = pl.program_id(1)

    @pl.when(r == 0)
    def _():
        stre_s[...] = h0re_ref[0]
        stim_s[...] = h0im_ref[0]

    u = u_ref[...]
    w = V7X_MXU
    half = pt_ref.shape[1] // 2
    pu = [_dot_nt(u[:, i * w:(i + 1) * w], pt_ref[i]) for i in range(2)]
    pure_s[...] = jnp.concatenate([pu[0][:, :half], pu[1][:, :half]], axis=1)
    puim_s[...] = jnp.concatenate([pu[0][:, half:], pu[1][:, half:]], axis=1)
    ar = jnp.broadcast_to(are_ref[0], (nb, 2 * half))
    ai = jnp.broadcast_to(aim_ref[0], (nb, 2 * half))

    def step(i, carry):
        re, im = carry
        rows = pl.ds(pl.multiple_of(i * nb, nb), nb)
        hinre_s[rows, :] = re
        hinim_s[rows, :] = im
        return (ar * re - ai * im + pure_s[rows, :], ar * im + ai * re + puim_s[rows, :])

    re, im = lax.fori_loop(0, rb // nb, step, (stre_s[...], stim_s[...]))
    stre_s[...] = re
    stim_s[...] = im
    hre_ref[0] = re
    him_ref[0] = im

    hre = hinre_s[...]
    him = hinim_s[...]
    for i in range(2):
        hin = jnp.concatenate([hre[:, i * half:(i + 1) * half], him[:, i * half:(i + 1) * half]], axis=1)
        y_ref[:, i * w:(i + 1) * w] = (_dot_nt(u[:, i * w:(i + 1) * w], tt_ref[i])
                                       + _dot_nt(hin.astype(BF16), qt_ref[i])).astype(y_ref.dtype)


def _s5_core(ut, mats, h0re, h0im, nb, rb):
    rows, cols = ut.shape
    pairs = cols // (2 * V7X_MXU)
    tt, pt, qt, are, aim = mats
    nst = pt.shape[1]
    pair3 = lambda p, r: (p, 0, 0)
    kern = functools.partial(_s5_kernel, nb=nb, rb=rb)
    blocks = (_nbytes((rb, 512), BF16) * 2 + _nbytes((2, 256, 256), BF16) + 4 * _nbytes((256, 128), BF16))
    scratch = 4 * _nbytes((rb, 128), F32) + 2 * _nbytes((nb, 128), F32)
    return pl.pallas_call(
        kern,
        grid=(pairs, rows // rb),
        in_specs=[
            pl.BlockSpec((rb, 2 * V7X_MXU), lambda p, r: (r, p)),
            pl.BlockSpec((2, V7X_MXU, V7X_MXU), pair3),
            pl.BlockSpec((2, nst, V7X_MXU), pair3),
            pl.BlockSpec((2, V7X_MXU, nst), pair3),
            pl.BlockSpec((1, 1, 128), pair3),
            pl.BlockSpec((1, 1, 128), pair3),
            pl.BlockSpec((1, nb, 128), pair3),
            pl.BlockSpec((1, nb, 128), pair3),
        ],
        out_specs=[
            pl.BlockSpec((rb, 2 * V7X_MXU), lambda p, r: (r, p)),
            pl.BlockSpec((1, nb, 128), pair3),
            pl.BlockSpec((1, nb, 128), pair3),
        ],
        out_shape=[
            jax.ShapeDtypeStruct((rows, cols), BF16),
            jax.ShapeDtypeStruct((pairs, nb, 128), F32),
            jax.ShapeDtypeStruct((pairs, nb, 128), F32),
        ],
        scratch_shapes=[pltpu.VMEM((rb, 128), F32)] * 4 + [pltpu.VMEM((nb, 128), F32)] * 2,
        compiler_params=pltpu.CompilerParams(
            dimension_semantics=("arbitrary", "arbitrary"), vmem_limit_bytes=_vmem_limit(blocks, scratch)),
        name="s5_core",
    )(ut, tt, pt, qt, are, aim, h0re, h0im)


def _s5_fused_kernel(x_ref, g_ref, tt_ref, pt_ref, qt_ref, are_ref, aim_ref, h0re_ref, h0im_ref,
                     y_ref, hre_ref, him_ref,
                     us, ugt, pure, puim, hinre, hinim, ytmp, ys, stre, stim, *, nb, tbt, spb):
    lc = S5_CHUNK
    ch = S5_GROUP_CH
    gps = V7X_LANES // ch
    nch = tbt // lc
    cols = nb * nch
    k = pl.program_id(1)
    ppg = gps // 2
    slab = lambda sl: k * spb + sl

    @pl.when(k == 0)
    def _():
        for b in range(nb):
            u = _rms(x_ref[b], g_ref[...])
            for kk in range(us.shape[0]):
                us[kk, pl.ds(b, tbt, stride=nb), :] = u[:, kk * V7X_LANES:(kk + 1) * V7X_LANES]

    @pl.when(pl.program_id(0) == 0)
    def _():
        for sl in range(spb):
            stre[slab(sl)] = h0re_ref[sl * ppg:(sl + 1) * ppg]
            stim[slab(sl)] = h0im_ref[sl * ppg:(sl + 1) * ppg]

    tile = lambda j, s: slice((j * lc + s) * nb, (j * lc + s + 1) * nb)
    for sl in range(spb):
        for s in range(lc):
            a = jnp.concatenate([us[slab(sl), tile(j, s), :] for j in range(nch)], axis=0).T
            for g in range(gps):
                ugt[sl * gps + g, (lc - 1 - s) * ch:(lc - s) * ch, :] = a[g * ch:(g + 1) * ch, :].astype(BF16)

    half = pt_ref.shape[1] // 2
    for p in range(spb * ppg):
        put0 = _dot(pt_ref[2 * p], ugt[2 * p])
        put1 = _dot(pt_ref[2 * p + 1], ugt[2 * p + 1])
        pure[p] = jnp.concatenate([put0[:half], put1[:half]], axis=0).T
        puim[p] = jnp.concatenate([put0[half:], put1[half:]], axis=0).T

    chains = [(sl, pr) for sl in range(spb) for pr in range(ppg)]
    state = [(stre[slab(sl), pr], stim[slab(sl), pr]) for sl, pr in chains]
    coef = [(jnp.broadcast_to(are_ref[sl * ppg + pr], (nb, 2 * half)),
             jnp.broadcast_to(aim_ref[sl * ppg + pr], (nb, 2 * half))) for sl, pr in chains]
    for j in range(nch):
        rows = slice(j * nb, (j + 1) * nb)
        for c, (sl, pr) in enumerate(chains):
            p = sl * ppg + pr
            re, im = state[c]
            ar, ai = coef[c]
            hinre[p, rows, :] = re
            hinim[p, rows, :] = im
            state[c] = (ar * re - ai * im + pure[p, rows, :], ar * im + ai * re + puim[p, rows, :])
    for c, (sl, pr) in enumerate(chains):
        re, im = state[c]
        stre[slab(sl), pr] = re
        stim[slab(sl), pr] = im
        hre_ref[slab(sl) * ppg + pr] = re
        him_ref[slab(sl) * ppg + pr] = im

    for p in range(spb * ppg):
        hre_t = hinre[p].T
        him_t = hinim[p].T
        for i in range(2):
            g = 2 * p + i
            hin_t = jnp.concatenate([hre_t[i * half:(i + 1) * half], him_t[i * half:(i + 1) * half]],
                                    axis=0).astype(BF16)
            ytmp[g] = _dot(tt_ref[g], ugt[g]) + _dot(qt_ref[g], hin_t)

    for sl in range(spb):
        for t in range(lc):
            zt = jnp.concatenate([ytmp[sl * gps + g, t * ch:(t + 1) * ch, :] for g in range(gps)], axis=0).T
            for j in range(nch):
                ys[sl, tile(j, t), :] = zt[j * nb:(j + 1) * nb, :]
    for sl in range(spb):
        for b in range(nb):
            y_ref[b, :, sl * V7X_LANES:(sl + 1) * V7X_LANES] = ys[sl, pl.ds(b, tbt, stride=nb), :].astype(y_ref.dtype)


def _s5_fused(x, g, mats_t, h0re, h0im, tbt):
    nb, s, d = x.shape
    tt, pt, qt, are, aim = mats_t
    groups = tt.shape[0]
    gps = V7X_LANES // S5_GROUP_CH
    slabs = groups // gps
    w = S5_CHUNK * S5_GROUP_CH
    nst = pt.shape[1]
    cols = nb * (tbt // S5_CHUNK)
    spb = S5_SLABS_PER_STEP
    ng, npair = spb * gps, spb * gps // 2
    kern = functools.partial(_s5_fused_kernel, nb=nb, tbt=tbt, spb=spb)
    slab3 = lambda t, k: (k, 0, 0)
    blocks = (_nbytes((nb, tbt, d), F32) + _nbytes((nb, tbt, spb * V7X_LANES), BF16)
              + ng * (_nbytes((w, w), BF16) + 2 * _nbytes((w, nst), BF16)))
    scratch = (_nbytes((slabs, nb * tbt, V7X_LANES), F32) + _nbytes((ng, w, cols), BF16)
               + 4 * _nbytes((npair, cols, nst), F32) + _nbytes((ng, w, cols), F32)
               + _nbytes((spb, nb * tbt, V7X_LANES), F32))
    return pl.pallas_call(
        kern,
        grid=(s // tbt, slabs // spb),
        in_specs=[
            pl.BlockSpec((nb, tbt, d), lambda t, k: (0, t, 0)),
            pl.BlockSpec((1, d), lambda t, k: (0, 0)),
            pl.BlockSpec((ng, w, w), slab3),
            pl.BlockSpec((ng, nst, w), slab3),
            pl.BlockSpec((ng, w, nst), slab3),
            pl.BlockSpec((npair, 1, nst), slab3),
            pl.BlockSpec((npair, 1, nst), slab3),
            pl.BlockSpec((npair, nb, nst), slab3),
            pl.BlockSpec((npair, nb, nst), slab3),
        ],
        out_specs=[
            pl.BlockSpec((nb, tbt, spb * V7X_LANES), lambda t, k: (0, t, k)),
            pl.BlockSpec((groups // 2, nb, nst), lambda t, k: (0, 0, 0)),
            pl.BlockSpec((groups // 2, nb, nst), lambda t, k: (0, 0, 0)),
        ],
        out_shape=[
            jax.ShapeDtypeStruct((nb, s, d), BF16),
            jax.ShapeDtypeStruct((groups // 2, nb, nst), F32),
            jax.ShapeDtypeStruct((groups // 2, nb, nst), F32),
        ],
        scratch_shapes=[
            pltpu.VMEM((slabs, nb * tbt, V7X_LANES), F32),
            pltpu.VMEM((ng, w, cols), BF16),
            pltpu.VMEM((npair, cols, nst), F32),
            pltpu.VMEM((npair, cols, nst), F32),
            pltpu.VMEM((npair, cols, nst), F32),
            pltpu.VMEM((npair, cols, nst), F32),
            pltpu.VMEM((ng, w, cols), F32),
            pltpu.VMEM((spb, nb * tbt, V7X_LANES), F32),
            pltpu.VMEM((slabs, gps // 2, nb, nst), F32),
            pltpu.VMEM((slabs, gps // 2, nb, nst), F32),
        ],
        compiler_params=pltpu.CompilerParams(
            dimension_semantics=("arbitrary", "arbitrary"), vmem_limit_bytes=_vmem_limit(blocks, scratch)),
        name="s5_fused",
    )(x, g, tt, pt, qt, are, aim, h0re, h0im)


def _s5_matrices(a_re, a_im, log_dt, b_re, b_im, c_re, c_im):
    lc = S5_CHUNK
    groups, n = a_re.shape
    ch = S5_GROUP_CH
    f = lambda v: v.astype(F32)
    a_re, a_im, b_re, b_im, c_re, c_im = map(f, (a_re, a_im, b_re, b_im, c_re, c_im))
    dt = jnp.exp(f(log_dt))[:, None, None]
    j = jnp.arange(lc + 1, dtype=F32)
    mag = jnp.exp((a_re[:, :, None] * dt) * j)
    ang = (a_im[:, :, None] * dt) * j
    pw_re, pw_im = mag * jnp.cos(ang), mag * jnp.sin(ang)
    e_re, e_im = pw_re[:, :, 1] - 1.0, pw_im[:, :, 1]
    inv = 1.0 / (a_re * a_re + a_im * a_im)
    f_re, f_im = (e_re * a_re + e_im * a_im) * inv, (e_im * a_re - e_re * a_im) * inv
    bb_re = f_re[..., None] * b_re - f_im[..., None] * b_im
    bb_im = f_re[..., None] * b_im + f_im[..., None] * b_re
    rep = lambda v: jnp.repeat(v, ch, axis=2)
    til = lambda v: jnp.tile(v, (1, 1, lc + 1))
    lb_re = rep(pw_re) * til(bb_re) - rep(pw_im) * til(bb_im)
    lb_im = rep(pw_re) * til(bb_im) + rep(pw_im) * til(bb_re)
    kern = jnp.einsum('gcm,gmx->gcx', jnp.concatenate([c_re, c_im], axis=2),
                      jnp.concatenate([lb_re, -lb_im], axis=1), precision=HI)
    w = lc * ch
    kfwd = jnp.pad(kern[:, :, :w], ((0, 0), (0, 0), ((lc - 1) * ch, 0)))
    tt = jnp.stack([kfwd[:, :, t * ch:t * ch + w] for t in range(lc)], axis=1).reshape(groups, w, w)
    pt = jnp.concatenate([lb_re[:, :, :w], lb_im[:, :, :w]], axis=1)
    pwt_re = jnp.swapaxes(pw_re, 1, 2)[:, 1:, None, :]
    pwt_im = jnp.swapaxes(pw_im, 1, 2)[:, 1:, None, :]
    ca_re = c_re[:, None] * pwt_re - c_im[:, None] * pwt_im
    ca_im = c_re[:, None] * pwt_im + c_im[:, None] * pwt_re
    qt = jnp.concatenate([ca_re, -ca_im], axis=3).reshape(groups, w, 2 * n)
    are = pw_re[:, :, lc].reshape(groups // 2, 1, 2 * n)
    aim = pw_im[:, :, lc].reshape(groups // 2, 1, 2 * n)
    return tt.astype(BF16), pt.astype(BF16), qt.astype(BF16), are, aim


def _glu_kernel(x_ref, y_ref, g0_ref, d_ref, w_ref, g1_ref, o_ref, out_s):
    x = x_ref[...]
    d = x.shape[1]
    u = _rms(x, g0_ref[...])
    yy = y_ref[...].astype(F32) + d_ref[...] * u
    z = jax.nn.gelu(yy, approximate=True).astype(BF16)
    w = V7X_MXU
    nslab = d // w
    pair = lambda j: (_dot(z, w_ref[:, j * w:(j + 1) * w]), _dot(z, w_ref[:, d + j * w:d + (j + 1) * w]))
    nxt = pair(0)
    ssq = jnp.zeros((x.shape[0], 1), F32)
    for j in range(nslab):
        a, b = nxt
        if j + 1 < nslab:
            nxt = pair(j + 1)
        out = a * (0.5 * (1.0 + jnp.tanh(0.5 * b)))
        out_s[:, j * w:(j + 1) * w] = out
        ssq = ssq + jnp.sum(out * out, axis=-1, keepdims=True)
    scale = lax.rsqrt(ssq * (1.0 / d) + NORM_EPS)
    o_ref[...] = x + out_s[...] * scale * g1_ref[...]


def _const_spec(shape):
    nd = len(shape)
    return pl.BlockSpec(shape, lambda *_: (0,) * nd, pipeline_mode=pl.Buffered(1))


def _glu(x2d, y2d, g0, dskip, w, g1, tb):
    rows, d = x2d.shape
    blocks = 2 * _nbytes((tb, d), F32) + _nbytes((tb, d), BF16)
    scratch = _nbytes(w.shape, BF16) + 3 * _nbytes((tb, 2 * d), F32)
    row = pl.BlockSpec((tb, d), lambda i: (i, 0))
    return pl.pallas_call(
        _glu_kernel,
        grid=(rows // tb,),
        in_specs=[row, row, _const_spec((1, d)), _const_spec((1, d)), _const_spec(w.shape), _const_spec((1, d))],
        out_specs=row,
        out_shape=jax.ShapeDtypeStruct((rows, d), F32),
        scratch_shapes=[pltpu.VMEM((tb, d), F32)],
        compiler_params=pltpu.CompilerParams(
            dimension_semantics=("arbitrary",), vmem_limit_bytes=_vmem_limit(blocks, scratch)),
        name="s5_glu",
    )(x2d, y2d, g0, dskip, w, g1)


def _ffn_kernel(x_ref, buf_ref, g2_ref, wup_ref, cw_ref, cb_ref, wdown_ref, g3_ref,
                o_ref, nbuf_ref, carry, act, *, tb, sub, dff, width):
    t = pl.program_id(1)
    pad = V7X_SUBLANES

    @pl.when(t == 0)
    def _():
        carry[...] = jnp.zeros_like(carry)
        carry[pad - (width - 1):pad, :] = buf_ref[0]

    fb = V7X_MXU
    nslab = dff // fb
    cols = lambda j: (slice(j * fb, (j + 1) * fb), slice(dff + j * fb, dff + (j + 1) * fb))

    def conv(u, cs):
        prev = carry[:, cs]
        carry[:, cs] = u[sub - pad:sub, :]
        return _causal_dwconv(u, prev, cw_ref, cb_ref, cs)

    def up_phase(r):
        hn = _rms(x_ref[0, r * sub:(r + 1) * sub, :], g2_ref[...]).astype(BF16)
        up = lambda j: tuple(_dot(hn, wup_ref[:, cs]) for cs in cols(j))
        nxt = up(0)
        for j in range(nslab):
            ug, uv = nxt
            if j + 1 < nslab:
                nxt = up(j + 1)
            gate = conv(ug, cols(j)[0])
            val = conv(uv, cols(j)[1])
            act[r % act.shape[0], :, j * fb:(j + 1) * fb] = (jax.nn.gelu(gate, approximate=True) * val).astype(BF16)

    nsub = tb // sub
    up_phase(0)
    for r in range(nsub):
        f = _dot(act[r % act.shape[0]], wdown_ref[...])
        if r + 1 < nsub:
            up_phase(r + 1)
        rows = slice(r * sub, (r + 1) * sub)
        o_ref[0, rows, :] = x_ref[0, rows, :] + _rms(f, g3_ref[...])
    nbuf_ref[0] = carry[pad - (width - 1):pad, :]


def _layer_spec(shape, layer):
    nd = len(shape) - 1
    return pl.BlockSpec((None,) + tuple(shape[1:]), lambda *_: (layer,) + (0,) * nd, pipeline_mode=pl.Buffered(1))


def _ffn(x, buf, g2, wup, cw, cb, wdown, g3, layer, tb):
    b, s, d = x.shape
    dff = wdown.shape[1]
    width = cw.shape[1]
    sub = min(tb, FFN_ROW_BLOCK)
    kern = functools.partial(_ffn_kernel, tb=tb, sub=sub, dff=dff, width=width)
    blocks = 2 * _nbytes((tb, d), F32)
    scratch = (_nbytes(wup.shape[1:], BF16) + _nbytes(wdown.shape[1:], BF16) + 6 * _nbytes((sub, V7X_MXU), F32)
               + _nbytes((8, 2 * dff), F32) + 2 * _nbytes((sub, dff), BF16) + 3 * _nbytes((sub, d), F32))
    xs = pl.BlockSpec((1, tb, d), lambda i, t: (i, t, 0))
    bs = pl.BlockSpec((1, width - 1, 2 * dff), lambda i, t: (i, 0, 0))
    return pl.pallas_call(
        kern,
        grid=(b, s // tb),
        in_specs=[xs, bs, _const_spec((1, d)), _layer_spec(wup.shape, layer), _layer_spec(cw.shape, layer),
                  _layer_spec(cb.shape, layer), _layer_spec(wdown.shape, layer), _const_spec((1, d))],
        out_specs=[xs, bs],
        out_shape=[jax.ShapeDtypeStruct((b, s, d), F32), jax.ShapeDtypeStruct((b, width - 1, 2 * dff), F32)],
        scratch_shapes=[pltpu.VMEM((V7X_SUBLANES, 2 * dff), F32), pltpu.VMEM((min(2, tb // sub), sub, dff), BF16)],
        compiler_params=pltpu.CompilerParams(
            dimension_semantics=("arbitrary", "arbitrary"), vmem_limit_bytes=_vmem_limit(blocks, scratch)),
        name="conv_ffn",
    )(x, buf, g2, wup, cw, cb, wdown, g3)


def _mlstm_weights_kernel(aq_ref, ak_ref, av_ref, gq_ref, gk_ref, gv_ref, wup_ref,
                          wq_ref, wkt_ref, wv_ref, gc_ref, gh_ref, *, kscale):
    w = V7X_MXU
    kb = QKV_BLOCK
    dotp = functools.partial(jnp.dot, preferred_element_type=F32, precision=HI)
    lane = lax.broadcasted_iota(jnp.int32, (V7X_LANES, w), 1)
    src = lax.broadcasted_iota(jnp.int32, (V7X_LANES, w), 0)
    spread = jnp.where(jnp.bitwise_and(lane, kb - 1) == src, 1.0, 0.0)
    shift = kb.bit_length() - 1
    rblk = lax.shift_right_logical(lax.broadcasted_iota(jnp.int32, (w, w), 0), shift)
    cblk = lax.shift_right_logical(lax.broadcasted_iota(jnp.int32, (w, w), 1), shift)
    tile = lambda a_ref: jnp.where(rblk == cblk, dotp(a_ref[0], spread), 0.0)
    tq, tk, tv = tile(aq_ref), tile(ak_ref), tile(av_ref)
    wq_ref[0] = tq.astype(BF16)
    wkt_ref[0] = (tk.T * kscale).astype(BF16)
    wv_ref[0] = tv.astype(BF16)
    gc_ref[...] = (dotp(tq, gq_ref[...]) + dotp(tk, gk_ref[...])).astype(BF16)

    @pl.when(pl.program_id(0) == 0)
    def _():
        gh_ref[...] = jnp.zeros_like(gh_ref)

    gh_ref[...] += dotp(wup_ref[...], dotp(tv, gv_ref[...]))


def _mlstm_weights(wq, wk, wv, w_gate, w_up, kscale):
    nblk, kb, _ = wq.shape
    assert kb == QKV_BLOCK and kb & (kb - 1) == 0
    w = V7X_MXU
    per = w // kb
    nt = nblk // per
    inner = nt * w
    compact = lambda m: jnp.pad(m.astype(F32).reshape(nt, w, kb), ((0, 0), (0, 0), (0, V7X_LANES - kb)))
    wg = jnp.pad(w_gate.astype(F32), ((0, 0), (0, V7X_LANES - w_gate.shape[1])))
    cm = pl.BlockSpec((1, w, V7X_LANES), lambda i: (i, 0, 0))
    gs = [pl.BlockSpec((w, V7X_LANES), lambda i, k=k: (k * nt + i, 0)) for k in range(3)]
    tile = pl.BlockSpec((1, w, w), lambda i: (i, 0, 0))
    fold = pl.BlockSpec((w, V7X_LANES), lambda i: (i, 0))
    d = w_up.shape[0]
    return pl.pallas_call(
        functools.partial(_mlstm_weights_kernel, kscale=kscale),
        grid=(nt,),
        in_specs=[cm, cm, cm] + gs + [pl.BlockSpec((d, w), lambda i: (0, i))],
        out_specs=[tile, tile, tile, fold, pl.BlockSpec((d, V7X_LANES), lambda i: (0, 0))],
        out_shape=[jax.ShapeDtypeStruct((nt, w, w), BF16)] * 3
        + [jax.ShapeDtypeStruct((inner, V7X_LANES), BF16), jax.ShapeDtypeStruct((d, V7X_LANES), F32)],
        compiler_params=pltpu.CompilerParams(dimension_semantics=("arbitrary",)),
        name="mlstm_weights",
    )(compact(wq), compact(wk), compact(wv), wg, wg, wg, w_up.astype(F32))


def _mlstm_chunk(q_ref, kt_ref, v_ref, g, gt, ct_s, n_s, m_s, fill, emit_hn, fill_update, *, lc, heads, dh):
    row = lax.broadcasted_iota(jnp.int32, (lc, lc), 0)
    col = lax.broadcasted_iota(jnp.int32, (lc, lc), 1)
    causal = col <= row
    tri_l = jnp.where(causal, 1.0, 0.0).astype(BF16)
    tri_u = jnp.where(row <= col, 1.0, 0.0).astype(BF16)
    b_cols = sum(_dot(tri_l, part) for part in _split3(_log_sigmoid(g)))
    b_rows = sum(_dot(part, tri_u) for part in _split3(_log_sigmoid(gt)))

    hsl = [slice(h * dh, (h + 1) * dh) for h in range(heads)]
    st = []
    for h in range(heads):
        qk = _dot(q_ref[0, :, hsl[h]], kt_ref[0, hsl[h], :])
        b_col = b_cols[:, heads + h:heads + h + 1]
        b_row = b_rows[heads + h:heads + h + 1, :]
        i_row = gt[h:h + 1, :]
        m_prev = m_s[h:h + 1, 0:1]
        dmat = jnp.where(causal, b_col - b_row + i_row, -jnp.inf)
        inter = b_col + m_prev
        m_t = jnp.maximum(inter, jnp.max(dmat, axis=1, keepdims=True))
        st.append(dict(b_col=b_col, b_row=b_row, i_row=i_row, m_prev=m_prev, m_t=m_t,
                       sc=qk * jnp.exp(dmat - m_t), a=jnp.exp(inter - m_t)))
        fill()
    out_mm = lambda h: (_dot(st[h]['sc'].astype(BF16), v_ref[0, :, hsl[h]]),
                        _dot(q_ref[0, :, hsl[h]], ct_s[h].astype(BF16)))
    nxt = out_mm(0)
    for h in range(heads):
        intra, inter_mm = nxt
        if h + 1 < heads:
            nxt = out_mm(h + 1)
        fill()
        s = st[h]
        qh = q_ref[0, :, hsl[h]]
        sc = s['sc']
        a = s['a']
        num = intra + a * inter_mm
        qn = jnp.sum(qh.astype(F32) * n_s[h:h + 1, :], axis=1, keepdims=True)
        den = jnp.sum(sc, axis=1, keepdims=True) + a * qn
        hh = num / jnp.maximum(jnp.abs(den), jnp.exp(-s['m_t']))
        mu = jnp.mean(hh, axis=1, keepdims=True)
        dev = hh - mu
        var = jnp.mean(dev * dev, axis=1, keepdims=True)
        emit_hn(h, dev * lax.rsqrt(var + LN_EPS))
    def update_mm(h):
        s = st[h]
        kth = kt_ref[0, hsl[h], :]
        b_last = s['b_row'][:, lc - 1:lc]
        g_row = b_last - s['b_row'] + s['i_row']
        m_new = jnp.maximum(b_last + s['m_prev'], jnp.max(g_row, axis=1, keepdims=True))
        decay = jnp.exp(b_last + s['m_prev'] - m_new)
        wg_row = jnp.exp(g_row - m_new)
        wg_col = jnp.exp(b_last - s['b_col'] + g[:, h:h + 1] - m_new)
        wv = (v_ref[0, :, hsl[h]].astype(F32) * wg_col).astype(BF16)
        wg16 = jnp.broadcast_to(wg_row, (16, lc)).astype(BF16)
        return m_new, decay, _dot(kth, wv), _dot_nt(wg16, kth)[0:1, :]

    fill_update(0)
    nxt = update_mm(0)
    for h in range(heads):
        m_new, decay, c_upd, n_upd = nxt
        if h + 1 < heads:
            fill_update(h + 1)
            nxt = update_mm(h + 1)
        ct_s[h] = decay * ct_s[h] + c_upd
        n_s[h:h + 1, :] = decay * n_s[h:h + 1, :] + n_upd
        m_s[h:h + 1, :] = jnp.broadcast_to(m_new, (1, V7X_LANES))


def _mlstm_layer_kernel(*refs, lc, heads, inner, width, zero_init):
    (x_ref, cbuf_ref, g0_ref, wup_ref, cw_ref, cb_ref, wq_ref, wkt_ref, wv_ref, gc_ref, gh_ref, bg_ref,
     ng_ref, sk_ref, wdown_ref, g1_ref) = refs[:16]
    n_in = 16
    if not zero_init:
        c0_ref, n0_ref, m0_ref = refs[16:19]
        n_in = 19
    o_ref, cout_ref, nout_ref, mout_ref, nbuf_ref = refs[n_in:n_in + 5]
    carry, q_s, kt_s, v_s, xc_s, sz_s, act_s, ct_s, n_s, m_s = refs[n_in + 5:]
    dh = inner // heads
    pad = V7X_SUBLANES
    c = pl.program_id(1)

    @pl.when(c == 0)
    def _():
        carry[...] = jnp.zeros_like(carry)
        carry[pad - (width - 1):pad, :] = cbuf_ref[0]
        if zero_init:
            ct_s[...] = jnp.zeros_like(ct_s)
            n_s[...] = jnp.zeros_like(n_s)
            m_s[...] = jnp.zeros_like(m_s)
        else:
            for h in range(heads):
                ct_s[h] = c0_ref[0, h].T
            n_s[...] = n0_ref[0]
            m_s[...] = m0_ref[0]

    x = x_ref[0]
    hx = _rms(x, g0_ref[...]).astype(BF16)
    w = V7X_MXU
    gacc = bg_ref[...] + _dot(hx, gh_ref[...])
    nslab = inner // w
    col = lambda i: slice(i * w, (i + 1) * w)

    def project(i, xc, xmb, gacc):
        q_s[0, :, col(i)] = _dot(xc, wq_ref[i]).astype(BF16)
        kt_s[0, col(i), :] = _dot_nt(wkt_ref[i], xc).astype(BF16)
        v_s[0, :, col(i)] = _dot(xmb, wv_ref[i]).astype(BF16)
        xc_s[:, col(i)] = xc
        return gacc + _dot(xc, gc_ref[col(i), :])

    xm_next = _dot(hx, wup_ref[:, col(0)])
    pending = None
    for i in range(nslab):
        xm = xm_next
        if i + 1 < nslab:
            xm_next = _dot(hx, wup_ref[:, col(i + 1)])
        if pending is not None:
            gacc = project(*pending, gacc)
        prev = carry[:, col(i)]
        carry[:, col(i)] = xm[lc - pad:lc, :]
        xc = _silu(_causal_dwconv(xm, prev, cw_ref, cb_ref, col(i))).astype(BF16)
        pending = (i, xc, xm.astype(BF16))
    gacc = project(*pending, gacc)
    nbuf_ref[0] = carry[pad - (width - 1):pad, :]

    slabs = iter(range(inner // w))

    def gate_slab():
        i = next(slabs, None)
        if i is not None:
            cs = slice(i * w, (i + 1) * w)
            sz_s[:, cs] = _silu(_dot(hx, wup_ref[:, inner + i * w:inner + (i + 1) * w])).astype(BF16)

    hcols = lambda h: slice(h * dh, (h + 1) * dh)
    down = []

    def gated_out(h, hn):
        hs = hn * ng_ref[:, hcols(h)] + sk_ref[:, hcols(h)] * xc_s[:, hcols(h)].astype(F32)
        act_s[:, hcols(h)] = (hs * sz_s[:, hcols(h)].astype(F32)).astype(BF16)

    def down_proj(h):
        if h == 0:
            down.append(_dot(act_s[...], wdown_ref[...]))

    _mlstm_chunk(q_s, kt_s, v_s, gacc, gacc.T[0:2 * heads, :], ct_s, n_s, m_s, gate_slab, gated_out, down_proj,
                 lc=lc, heads=heads, dh=dh)
    for _ in slabs:
        raise AssertionError("fewer fill points than output-gate slabs")
    o_ref[0] = x + _rms(down[0], g1_ref[...])

    @pl.when(c == pl.num_programs(1) - 1)
    def _():
        for h in range(heads):
            cout_ref[0, h] = ct_s[h].T
        nout_ref[0] = n_s[...]
        mout_ref[0] = m_s[...]


def _mlstm_layer(x, cbuf, state, p, g0, g1, lc):
    b, s, d = x.shape
    heads = p['heads']
    inner = p['m_cw'].shape[1]
    width = p['m_cw'].shape[0]
    dh = inner // heads
    zero_init = state is None
    kern = functools.partial(_mlstm_layer_kernel, lc=lc, heads=heads, inner=inner, width=width,
                             zero_init=zero_init)
    xs = pl.BlockSpec((1, lc, d), lambda i, c: (i, c, 0))
    bufs = pl.BlockSpec((1, width - 1, inner), lambda i, c: (i, 0, 0))
    cs = pl.BlockSpec((1, heads, dh, dh), lambda i, c: (i, 0, 0, 0))
    ns = pl.BlockSpec((1, heads, dh), lambda i, c: (i, 0, 0))
    ms = pl.BlockSpec((1, heads, V7X_LANES), lambda i, c: (i, 0, 0))
    consts = [g0, p['m_wup'], p['m_cw'], p['m_cb'], p['m_wq'], p['m_wkt'], p['m_wv'], p['m_gc'], p['m_gh'],
              p['m_bg'], p['m_ng'], p['m_skip'], p['m_wdown'], g1]
    in_specs = [xs, bufs] + [_const_spec(a.shape) for a in consts]
    args = [x, cbuf] + consts
    if not zero_init:
        in_specs += [cs, ns, ms]
        args += list(state)
    blocks = 2 * _nbytes((lc, d), F32) + (1 if zero_init else 2) * _nbytes((heads, dh, dh), F32)
    scratch = (sum(_nbytes(a.shape, a.dtype) for a in consts) + 6 * _nbytes((lc, inner), BF16)
               + _nbytes((heads, dh, dh), F32) + 8 * _nbytes((lc, max(lc, dh)), F32) + 2 * _nbytes((dh, dh), F32))
    return pl.pallas_call(
        kern,
        grid=(b, s // lc),
        in_specs=in_specs,
        out_specs=[xs, cs, ns, ms, bufs],
        out_shape=[jax.ShapeDtypeStruct((b, s, d), F32), jax.ShapeDtypeStruct((b, heads, dh, dh), F32),
                   jax.ShapeDtypeStruct((b, heads, dh), F32), jax.ShapeDtypeStruct((b, heads, V7X_LANES), F32),
                   jax.ShapeDtypeStruct((b, width - 1, inner), F32)],
        scratch_shapes=[pltpu.VMEM((V7X_SUBLANES, inner), F32),
                        pltpu.VMEM((1, lc, inner), BF16), pltpu.VMEM((1, inner, lc), BF16),
                        pltpu.VMEM((1, lc, inner), BF16), pltpu.VMEM((lc, inner), BF16),
                        pltpu.VMEM((lc, inner), BF16), pltpu.VMEM((lc, inner), BF16),
                        pltpu.VMEM((heads, dh, dh), F32), pltpu.VMEM((heads, dh), F32),
                        pltpu.VMEM((heads, V7X_LANES), F32)],
        compiler_params=pltpu.CompilerParams(
            dimension_semantics=("arbitrary", "arbitrary"), vmem_limit_bytes=_vmem_limit(blocks, scratch)),
        name="mlstm_layer",
    )(*args)


def _prep_weights(w):
    row = lambda v: v.astype(F32).reshape(1, -1)
    p = {}
    p['norm_g'] = w['norm_g'].astype(F32)
    p['s5_mats'] = _s5_matrices(w['s5_A_re'][0], w['s5_A_im'][0], w['s5_log_dt'][0], w['s5_B_re'][0],
                                w['s5_B_im'][0], w['s5_C_re'][0], w['s5_C_im'][0])
    p['s5_D'] = row(w['s5_D'][0])
    p['s5_w_glu'] = w['s5_w_glu'][0].astype(BF16)
    inner = w['mlstm_conv_w'].shape[2]
    heads = w['mlstm_b_gate'].shape[1] // 2
    dh = inner // heads
    p['m_wq'], p['m_wkt'], p['m_wv'], p['m_gc'], gh = _mlstm_weights(
        w['mlstm_wq'][0], w['mlstm_wk'][0], w['mlstm_wv'][0], w['mlstm_w_gate'][0], w['mlstm_w_up'][0],
        dh ** -0.5)
    p['m_gh'] = gh.astype(BF16)
    bg = row(w['mlstm_b_gate'][0])
    p['m_bg'] = jnp.pad(bg, ((0, 0), (0, V7X_LANES - bg.shape[1])))
    p['m_wup'] = w['mlstm_w_up'][0].astype(BF16)
    p['m_cw'] = w['mlstm_conv_w'][0].astype(F32)
    p['m_cb'] = row(w['mlstm_conv_b'][0])
    p['m_ng'] = row(w['mlstm_norm_g'][0])
    p['m_skip'] = row(w['mlstm_skip'][0])
    p['m_wdown'] = w['mlstm_w_down'][0].astype(BF16)
    p['heads'] = heads
    p['f_wup'] = w['ffn_w_up'].astype(BF16)
    p['f_cw'] = w['ffn_conv_w'].astype(F32)
    p['f_cb'] = w['ffn_conv_b'].astype(F32)[:, None, :]
    p['f_wdown'] = w['ffn_w_down'].astype(BF16)
    return p


def _to_chunk_major(u2d, b, s):
    d = u2d.shape[1]
    g = d // S5_GROUP_CH
    u = u2d.reshape(b, s // S5_CHUNK, S5_CHUNK, g, S5_GROUP_CH)[:, :, ::-1]
    return jnp.transpose(u, (1, 0, 3, 2, 4)).reshape((s // S5_CHUNK) * b, d * S5_CHUNK)


def _to_token_major(yt, b, s):
    d = yt.shape[1] // S5_CHUNK
    g = d // S5_GROUP_CH
    y = yt.reshape(s // S5_CHUNK, b, g, S5_CHUNK, S5_GROUP_CH)
    return jnp.transpose(y, (1, 0, 3, 2, 4)).reshape(b * s, d)


def _trunk(x, st, p, tiles):
    b, s, d = x.shape
    ng = p['norm_g']
    g_of = lambda layer, k: ng[layer, k].reshape(1, d)
    zero_init = st is None
    groups = d // S5_GROUP_CH
    x2d = x.reshape(b * s, d)

    n_state = p['s5_mats'][1].shape[1] // 2
    if zero_init:
        h0re = h0im = jnp.zeros((groups // 2, b, 2 * n_state), F32)
    else:
        pair = lambda h: jnp.transpose(h.astype(F32).reshape(b, groups // 2, 2 * n_state), (1, 0, 2))
        h0re, h0im = pair(st['s5_re']), pair(st['s5_im'])
    if tiles.s5_time:
        y, hre, him = _s5_fused(x, g_of(0, 0), p['s5_mats'], h0re, h0im, tiles.s5_time)
        y2d = y.reshape(b * s, d)
    else:
        u = _norm_cast(x2d, g_of(0, 0), tiles.rows)
        yt, hre, him = _s5_core(_to_chunk_major(u, b, s), p['s5_mats'], h0re, h0im, b, tiles.s5_rows)
        y2d = _to_token_major(yt, b, s)
    unpair = lambda h: jnp.transpose(h, (1, 0, 2)).reshape(1, b, groups, n_state)
    o_re, o_im = unpair(hre), unpair(him)
    x2d = _glu(x2d, y2d, g_of(0, 0), p['s5_D'], p['s5_w_glu'], g_of(0, 1), tiles.rows)

    def ffn(x2d, layer):
        dff2 = p['f_wup'].shape[2]
        width = p['f_cw'].shape[1]
        buf = jnp.zeros((b, width - 1, dff2), F32) if zero_init else st['ffn_conv'][layer].astype(F32)
        xo, nbuf = _ffn(x2d.reshape(b, s, d), buf, g_of(layer, 2), p['f_wup'], p['f_cw'], p['f_cb'],
                        p['f_wdown'], g_of(layer, 3), layer, tiles.ffn_time)
        return xo.reshape(b * s, d), nbuf

    x2d, fbuf0 = ffn(x2d, 0)

    heads = p['heads']
    inner = p['m_cw'].shape[1]
    width = p['m_cw'].shape[0]
    cbuf = jnp.zeros((b, width - 1, inner), F32) if zero_init else st['mlstm_conv'].astype(F32)
    if zero_init:
        state = None
    else:
        state = (st['mlstm_C'].astype(F32), st['mlstm_n'].astype(F32),
                 jnp.broadcast_to(st['mlstm_m'].astype(F32)[:, :, None], (b, heads, V7X_LANES)))
    x3, c_out, n_out, m_out, ncbuf = _mlstm_layer(x2d.reshape(b, s, d), cbuf, state, p, g_of(1, 0), g_of(1, 1),
                                                  tiles.mlstm_chunk)
    x2d, fbuf1 = ffn(x3.reshape(b * s, d), 1)

    return (x2d.reshape(b, s, d), o_re, o_im, c_out[None], n_out[None], m_out[None, :, :, 0], ncbuf[None],
            jnp.stack([fbuf0, fbuf1], axis=0))


def kernel(x_prompt, x_sample, state_s5_re, state_s5_im, state_mlstm_C, state_mlstm_n, state_mlstm_m, state_mlstm_conv, state_ffn_conv, norm_g, s5_A_re, s5_A_im, s5_log_dt, s5_B_re, s5_B_im, s5_C_re, s5_C_im, s5_D, s5_w_glu, mlstm_w_up, mlstm_conv_w, mlstm_conv_b, mlstm_wq, mlstm_wk, mlstm_wv, mlstm_w_gate, mlstm_b_gate, mlstm_norm_g, mlstm_skip, mlstm_w_down, ffn_w_up, ffn_conv_w, ffn_conv_b, ffn_w_down):
    w = {
        'norm_g': norm_g, 's5_A_re': s5_A_re, 's5_A_im': s5_A_im, 's5_log_dt': s5_log_dt,
        's5_B_re': s5_B_re, 's5_B_im': s5_B_im, 's5_C_re': s5_C_re, 's5_C_im': s5_C_im, 's5_D': s5_D,
        's5_w_glu': s5_w_glu, 'mlstm_w_up': mlstm_w_up, 'mlstm_conv_w': mlstm_conv_w,
        'mlstm_conv_b': mlstm_conv_b, 'mlstm_wq': mlstm_wq, 'mlstm_wk': mlstm_wk, 'mlstm_wv': mlstm_wv,
        'mlstm_w_gate': mlstm_w_gate, 'mlstm_b_gate': mlstm_b_gate, 'mlstm_norm_g': mlstm_norm_g,
        'mlstm_skip': mlstm_skip, 'mlstm_w_down': mlstm_w_down, 'ffn_w_up': ffn_w_up,
        'ffn_conv_w': ffn_conv_w, 'ffn_conv_b': ffn_conv_b, 'ffn_w_down': ffn_w_down,
    }
    assert norm_g.shape[0] == 2 and s5_A_re.shape[0] == 1 and mlstm_w_up.shape[0] == 1
    assert x_prompt.shape[0] == V7X_SUBLANES and x_sample.shape[0] == V7X_SUBLANES
    p = _prep_weights(w)
    out_p = _trunk(x_prompt.astype(F32), None, p, _pick_tiles(*x_prompt.shape[:2]))
    st = {'s5_re': state_s5_re[0], 's5_im': state_s5_im[0], 'mlstm_C': state_mlstm_C[0],
          'mlstm_n': state_mlstm_n[0], 'mlstm_m': state_mlstm_m[0], 'mlstm_conv': state_mlstm_conv[0],
          'ffn_conv': state_ffn_conv}
    out_s = _trunk(x_sample.astype(F32), st, p, _pick_tiles(*x_sample.shape[:2]))
    return (out_p[0], out_s[0]) + tuple(out_p[1:]) + tuple(out_s[1:])
```

```python
import functools
import math
from typing import NamedTuple

import jax
import jax.numpy as jnp
from jax import lax
from jax.experimental import pallas as pl
from jax.experimental.pallas import tpu as pltpu

F32 = jnp.float32
BF16 = jnp.bfloat16
HI = lax.Precision.HIGHEST

NORM_EPS = 1e-6
LN_EPS = 1e-5
S5_GROUP_CH = 16
S5_CHUNK = 16
QKV_BLOCK = 4
S5_SLABS_PER_STEP = 2

V7X_LANES = 128
V7X_SUBLANES = 8
V7X_MXU = 256
V7X_VMEM_BYTES = 64 * 1024 * 1024


class Tiles(NamedTuple):
    rows: int
    s5_rows: int
    mlstm_chunk: int
    s5_time: int
    ffn_time: int


def _pick_tiles(batch, seq):
    s5_time = (V7X_LANES // batch) * S5_CHUNK
    return Tiles(rows=min(batch * seq, 512),
                 s5_rows=min(batch * (seq // S5_CHUNK), 512), mlstm_chunk=min(seq, 256),
                 s5_time=s5_time if seq % s5_time == 0 else 0, ffn_time=min(seq, 1024))


def _vmem_limit(block_bytes, scratch_bytes):
    want = 2 * block_bytes + scratch_bytes + 16 * 1024 * 1024
    return int(min(want, V7X_VMEM_BYTES - 8 * 1024 * 1024))


def _nbytes(shape, dtype):
    return math.prod(shape) * jnp.dtype(dtype).itemsize


def _rms(x, g):
    return x * lax.rsqrt(jnp.mean(x * x, axis=-1, keepdims=True) + NORM_EPS) * g


def _dot(a, b):
    return jnp.dot(a, b, preferred_element_type=F32)


def _dot_nt(a, b):
    return lax.dot_general(a, b, (((1,), (1,)), ((), ())), preferred_element_type=F32)


def _silu(x):
    return (0.5 * x) * (1.0 + jnp.tanh(0.5 * x))


def _causal_dwconv(u, prev, cw_ref, cb_ref, cs):
    width = cw_ref.shape[0]
    acc = cb_ref[:, cs] + u * cw_ref[width - 1:width, cs]
    rows = lax.broadcasted_iota(jnp.int32, prev.shape, 0)
    for j in range(width - 1):
        d = width - 1 - j
        rolled = pltpu.roll(u, d, 0)
        head = jnp.where(rows < d, pltpu.roll(prev, d, 0), rolled[0:V7X_SUBLANES])
        shifted = jnp.concatenate([head, rolled[V7X_SUBLANES:]], axis=0)
        acc = acc + shifted * cw_ref[j:j + 1, cs]
    return acc


def _log_sigmoid(x):
    return jnp.minimum(x, 0.0) - jnp.log1p(jnp.exp(-jnp.abs(x)))


def _split3(x):
    hi = x.astype(BF16)
    r1 = x - hi.astype(F32)
    mid = r1.astype(BF16)
    lo = (r1 - mid.astype(F32)).astype(BF16)
    return hi, mid, lo


def _norm_cast_kernel(x_ref, g_ref, o_ref):
    o_ref[...] = _rms(x_ref[...], g_ref[...]).astype(o_ref.dtype)


def _norm_cast(x2d, g, tb):
    rows, d = x2d.shape
    return pl.pallas_call(
        _norm_cast_kernel,
        grid=(rows // tb,),
        in_specs=[pl.BlockSpec((tb, d), lambda i: (i, 0)), pl.BlockSpec((1, d), lambda i: (0, 0))],
        out_specs=pl.BlockSpec((tb, d), lambda i: (i, 0)),
        out_shape=jax.ShapeDtypeStruct((rows, d), BF16),
        name="s5_norm",
    )(x2d, g)


def _s5_kernel(u_ref, tt_ref, pt_ref, qt_ref, are_ref, aim_ref, h0re_ref, h0im_ref,
               y_ref, hre_ref, him_ref, pure_s, puim_s, hinre_s, hinim_s, stre_s, stim_s, *, nb, rb):
    r = pl.program_id(1)

    @pl.when(r == 0)
    def _():
        stre_s[...] = h0re_ref[0]
        stim_s[...] = h0im_ref[0]

    u = u_ref[...]
    w = V7X_MXU
    half = pt_ref.shape[1] // 2
    pu = [_dot_nt(u[:, i * w:(i + 1) * w], pt_ref[i]) for i in range(2)]
    pure_s[...] = jnp.concatenate([pu[0][:, :half], pu[1][:, :half]], axis=1)
    puim_s[...] = jnp.concatenate([pu[0][:, half:], pu[1][:, half:]], axis=1)
    ar = jnp.broadcast_to(are_ref[0], (nb, 2 * half))
    ai = jnp.broadcast_to(aim_ref[0], (nb, 2 * half))

    def step(i, carry):
        re, im = carry
        rows = pl.ds(pl.multiple_of(i * nb, nb), nb)
        hinre_s[rows, :] = re
        hinim_s[rows, :] = im
        return (ar * re - ai * im + pure_s[rows, :], ar * im + ai * re + puim_s[rows, :])

    re, im = lax.fori_loop(0, rb // nb, step, (stre_s[...], stim_s[...]))
    stre_s[...] = re
    stim_s[...] = im
    hre_ref[0] = re
    him_ref[0] = im

    hre = hinre_s[...]
    him = hinim_s[...]
    for i in range(2):
        hin = jnp.concatenate([hre[:, i * half:(i + 1) * half], him[:, i * half:(i + 1) * half]], axis=1)
        y_ref[:, i * w:(i + 1) * w] = (_dot_nt(u[:, i * w:(i + 1) * w], tt_ref[i])
                                       + _dot_nt(hin.astype(BF16), qt_ref[i])).astype(y_ref.dtype)


def _s5_core(ut, mats, h0re, h0im, nb, rb):
    rows, cols = ut.shape
    pairs = cols // (2 * V7X_MXU)
    tt, pt, qt, are, aim = mats
    nst = pt.shape[1]
    pair3 = lambda p, r: (p, 0, 0)
    kern = functools.partial(_s5_kernel, nb=nb, rb=rb)
    blocks = (_nbytes((rb, 512), BF16) * 2 + _nbytes((2, 256, 256), BF16) + 4 * _nbytes((256, 128), BF16))
    scratch = 4 * _nbytes((rb, 128), F32) + 2 * _nbytes((nb, 128), F32)
    return pl.pallas_call(
        kern,
        grid=(pairs, rows // rb),
        in_specs=[
            pl.BlockSpec((rb, 2 * V7X_MXU), lambda p, r: (r, p)),
            pl.BlockSpec((2, V7X_MXU, V7X_MXU), pair3),
            pl.BlockSpec((2, nst, V7X_MXU), pair3),
            pl.BlockSpec((2, V7X_MXU, nst), pair3),
            pl.BlockSpec((1, 1, 128), pair3),
            pl.BlockSpec((1, 1, 128), pair3),
            pl.BlockSpec((1, nb, 128), pair3),
            pl.BlockSpec((1, nb, 128), pair3),
        ],
        out_specs=[
            pl.BlockSpec((rb, 2 * V7X_MXU), lambda p, r: (r, p)),
            pl.BlockSpec((1, nb, 128), pair3),
            pl.BlockSpec((1, nb, 128), pair3),
        ],
        out_shape=[
            jax.ShapeDtypeStruct((rows, cols), BF16),
            jax.ShapeDtypeStruct((pairs, nb, 128), F32),
            jax.ShapeDtypeStruct((pairs, nb, 128), F32),
        ],
        scratch_shapes=[pltpu.VMEM((rb, 128), F32)] * 4 + [pltpu.VMEM((nb, 128), F32)] * 2,
        compiler_params=pltpu.CompilerParams(
            dimension_semantics=("arbitrary", "arbitrary"), vmem_limit_bytes=_vmem_limit(blocks, scratch)),
        name="s5_core",
    )(ut, tt, pt, qt, are, aim, h0re, h0im)


def _s5_fused_kernel(x_ref, g_ref, tt_ref, pt_ref, qt_ref, are_ref, aim_ref, h0re_ref, h0im_ref,
                     y_ref, hre_ref, him_ref,
                     us, ugt, pure, puim, hinre, hinim, ytmp, ys, stre, stim, *, nb, tbt, spb):
    lc = S5_CHUNK
    ch = S5_GROUP_CH
    gps = V7X_LANES // ch
    nch = tbt // lc
    cols = nb * nch
    k = pl.program_id(1)
    ppg = gps // 2
    slab = lambda sl: k * spb + sl

    @pl.when(k == 0)
    def _():
        for b in range(nb):
            u = _rms(x_ref[b], g_ref[...])
            for kk in range(us.shape[0]):
                us[kk, pl.ds(b, tbt, stride=nb), :] = u[:, kk * V7X_LANES:(kk + 1) * V7X_LANES]

    @pl.when(pl.program_id(0) == 0)
    def _():
        for sl in range(spb):
            stre[slab(sl)] = h0re_ref[sl * ppg:(sl + 1) * ppg]
            stim[slab(sl)] = h0im_ref[sl * ppg:(sl + 1) * ppg]

    tile = lambda j, s: slice((j * lc + s) * nb, (j * lc + s + 1) * nb)
    for sl in range(spb):
        for s in range(lc):
            a = jnp.concatenate([us[slab(sl), tile(j, s), :] for j in range(nch)], axis=0).T
            for g in range(gps):
                ugt[sl * gps + g, (lc - 1 - s) * ch:(lc - s) * ch, :] = a[g * ch:(g + 1) * ch, :].astype(BF16)

    half = pt_ref.shape[1] // 2
    for p in range(spb * ppg):
        put0 = _dot(pt_ref[2 * p], ugt[2 * p])
        put1 = _dot(pt_ref[2 * p + 1], ugt[2 * p + 1])
        pure[p] = jnp.concatenate([put0[:half], put1[:half]], axis=0).T
        puim[p] = jnp.concatenate([put0[half:], put1[half:]], axis=0).T

    chains = [(sl, pr) for sl in range(spb) for pr in range(ppg)]
    state = [(stre[slab(sl), pr], stim[slab(sl), pr]) for sl, pr in chains]
    coef = [(jnp.broadcast_to(are_ref[sl * ppg + pr], (nb, 2 * half)),
             jnp.broadcast_to(aim_ref[sl * ppg + pr], (nb, 2 * half))) for sl, pr in chains]
    for j in range(nch):
        rows = slice(j * nb, (j + 1) * nb)
        for c, (sl, pr) in enumerate(chains):
            p = sl * ppg + pr
            re, im = state[c]
            ar, ai = coef[c]
            hinre[p, rows, :] = re
            hinim[p, rows, :] = im
            state[c] = (ar * re - ai * im + pure[p, rows, :], ar * im + ai * re + puim[p, rows, :])
    for c, (sl, pr) in enumerate(chains):
        re, im = state[c]
        stre[slab(sl), pr] = re
        stim[slab(sl), pr] = im
        hre_ref[slab(sl) * ppg + pr] = re
        him_ref[slab(sl) * ppg + pr] = im

    for p in range(spb * ppg):
        hre_t = hinre[p].T
        him_t = hinim[p].T
        for i in range(2):
            g = 2 * p + i
            hin_t = jnp.concatenate([hre_t[i * half:(i + 1) * half], him_t[i * half:(i + 1) * half]],
                                    axis=0).astype(BF16)
            ytmp[g] = _dot(tt_ref[g], ugt[g]) + _dot(qt_ref[g], hin_t)

    for sl in range(spb):
        for t in range(lc):
            zt = jnp.concatenate([ytmp[sl * gps + g, t * ch:(t + 1) * ch, :] for g in range(gps)], axis=0).T
            for j in range(nch):
                ys[sl, tile(j, t), :] = zt[j * nb:(j + 1) * nb, :]
    for sl in range(spb):
        for b in range(nb):
            y_ref[b, :, sl * V7X_LANES:(sl + 1) * V7X_LANES] = ys[sl, pl.ds(b, tbt, stride=nb), :].astype(y_ref.dtype)


def _s5_fused(x, g, mats_t, h0re, h0im, tbt):
    nb, s, d = x.shape
    tt, pt, qt, are, aim = mats_t
    groups = tt.shape[0]
    gps = V7X_LANES // S5_GROUP_CH
    slabs = groups // gps
    w = S5_CHUNK * S5_GROUP_CH
    nst = pt.shape[1]
    cols = nb * (tbt // S5_CHUNK)
    spb = S5_SLABS_PER_STEP
    ng, npair = spb * gps, spb * gps // 2
    kern = functools.partial(_s5_fused_kernel, nb=nb, tbt=tbt, spb=spb)
    slab3 = lambda t, k: (k, 0, 0)
    blocks = (_nbytes((nb, tbt, d), F32) + _nbytes((nb, tbt, spb * V7X_LANES), BF16)
              + ng * (_nbytes((w, w), BF16) + 2 * _nbytes((w, nst), BF16)))
    scratch = (_nbytes((slabs, nb * tbt, V7X_LANES), F32) + _nbytes((ng, w, cols), BF16)
               + 4 * _nbytes((npair, cols, nst), F32) + _nbytes((ng, w, cols), F32)
               + _nbytes((spb, nb * tbt, V7X_LANES), F32))
    return pl.pallas_call(
        kern,
        grid=(s // tbt, slabs // spb),
        in_specs=[
            pl.BlockSpec((nb, tbt, d), lambda t, k: (0, t, 0)),
            pl.BlockSpec((1, d), lambda t, k: (0, 0)),
            pl.BlockSpec((ng, w, w), slab3),
            pl.BlockSpec((ng, nst, w), slab3),
            pl.BlockSpec((ng, w, nst), slab3),
            pl.BlockSpec((npair, 1, nst), slab3),
            pl.BlockSpec((npair, 1, nst), slab3),
            pl.BlockSpec((npair, nb, nst), slab3),
            pl.BlockSpec((npair, nb, nst), slab3),
        ],
        out_specs=[
            pl.BlockSpec((nb, tbt, spb * V7X_LANES), lambda t, k: (0, t, k)),
            pl.BlockSpec((groups // 2, nb, nst), lambda t, k: (0, 0, 0)),
            pl.BlockSpec((groups // 2, nb, nst), lambda t, k: (0, 0, 0)),
        ],
        out_shape=[
            jax.ShapeDtypeStruct((nb, s, d), BF16),
            jax.ShapeDtypeStruct((groups // 2, nb, nst), F32),
            jax.ShapeDtypeStruct((groups // 2, nb, nst), F32),
        ],
        scratch_shapes=[
            pltpu.VMEM((slabs, nb * tbt, V7X_LANES), F32),
            pltpu.VMEM((ng, w, cols), BF16),
            pltpu.VMEM((npair, cols, nst), F32),
            pltpu.VMEM((npair, cols, nst), F32),
            pltpu.VMEM((npair, cols, nst), F32),
            pltpu.VMEM((npair, cols, nst), F32),
            pltpu.VMEM((ng, w, cols), F32),
            pltpu.VMEM((spb, nb * tbt, V7X_LANES), F32),
            pltpu.VMEM((slabs, gps // 2, nb, nst), F32),
            pltpu.VMEM((slabs, gps // 2, nb, nst), F32),
        ],
        compiler_params=pltpu.CompilerParams(
            dimension_semantics=("arbitrary", "arbitrary"), vmem_limit_bytes=_vmem_limit(blocks, scratch)),
        name="s5_fused",
    )(x, g, tt, pt, qt, are, aim, h0re, h0im)


def _s5_matrices(a_re, a_im, log_dt, b_re, b_im, c_re, c_im):
    lc = S5_CHUNK
    groups, n = a_re.shape
    ch = S5_GROUP_CH
    f = lambda v: v.astype(F32)
    a_re, a_im, b_re, b_im, c_re, c_im = map(f, (a_re, a_im, b_re, b_im, c_re, c_im))
    dt = jnp.exp(f(log_dt))[:, None, None]
    j = jnp.arange(lc + 1, dtype=F32)
    mag = jnp.exp((a_re[:, :, None] * dt) * j)
    ang = (a_im[:, :, None] * dt) * j
    pw_re, pw_im = mag * jnp.cos(ang), mag * jnp.sin(ang)
    e_re, e_im = pw_re[:, :, 1] - 1.0, pw_im[:, :, 1]
    inv = 1.0 / (a_re * a_re + a_im * a_im)
    f_re, f_im = (e_re * a_re + e_im * a_im) * inv, (e_im * a_re - e_re * a_im) * inv
    bb_re = f_re[..., None] * b_re - f_im[..., None] * b_im
    bb_im = f_re[..., None] * b_im + f_im[..., None] * b_re
    rep = lambda v: jnp.repeat(v, ch, axis=2)
    til = lambda v: jnp.tile(v, (1, 1, lc + 1))
    lb_re = rep(pw_re) * til(bb_re) - rep(pw_im) * til(bb_im)
    lb_im = rep(pw_re) * til(bb_im) + rep(pw_im) * til(bb_re)
    kern = jnp.einsum('gcm,gmx->gcx', jnp.concatenate([c_re, c_im], axis=2),
                      jnp.concatenate([lb_re, -lb_im], axis=1), precision=HI)
    w = lc * ch
    kfwd = jnp.pad(kern[:, :, :w], ((0, 0), (0, 0), ((lc - 1) * ch, 0)))
    tt = jnp.stack([kfwd[:, :, t * ch:t * ch + w] for t in range(lc)], axis=1).reshape(groups, w, w)
    pt = jnp.concatenate([lb_re[:, :, :w], lb_im[:, :, :w]], axis=1)
    pwt_re = jnp.swapaxes(pw_re, 1, 2)[:, 1:, None, :]
    pwt_im = jnp.swapaxes(pw_im, 1, 2)[:, 1:, None, :]
    ca_re = c_re[:, None] * pwt_re - c_im[:, None] * pwt_im
    ca_im = c_re[:, None] * pwt_im + c_im[:, None] * pwt_re
    qt = jnp.concatenate([ca_re, -ca_im], axis=3).reshape(groups, w, 2 * n)
    are = pw_re[:, :, lc].reshape(groups // 2, 1, 2 * n)
    aim = pw_im[:, :, lc].reshape(groups // 2, 1, 2 * n)
    return tt.astype(BF16), pt.astype(BF16), qt.astype(BF16), are, aim


def _glu_kernel(x_ref, y_ref, g0_ref, d_ref, w_ref, g1_ref, o_ref, out_s):
    x = x_ref[...]
    d = x.shape[1]
    u = _rms(x, g0_ref[...])
    yy = y_ref[...].astype(F32) + d_ref[...] * u
    z = jax.nn.gelu(yy, approximate=True).astype(BF16)
    w = V7X_MXU
    nslab = d // w
    pair = lambda j: (_dot(z, w_ref[:, j * w:(j + 1) * w]), _dot(z, w_ref[:, d + j * w:d + (j + 1) * w]))
    nxt = pair(0)
    ssq = jnp.zeros((x.shape[0], 1), F32)
    for j in range(nslab):
        a, b = nxt
        if j + 1 < nslab:
            nxt = pair(j + 1)
        out = a * (0.5 * (1.0 + jnp.tanh(0.5 * b)))
        out_s[:, j * w:(j + 1) * w] = out
        ssq = ssq + jnp.sum(out * out, axis=-1, keepdims=True)
    scale = lax.rsqrt(ssq * (1.0 / d) + NORM_EPS)
    o_ref[...] = x + out_s[...] * scale * g1_ref[...]


def _const_spec(shape):
    nd = len(shape)
    return pl.BlockSpec(shape, lambda *_: (0,) * nd, pipeline_mode=pl.Buffered(1))


def _glu(x2d, y2d, g0, dskip, w, g1, tb):
    rows, d = x2d.shape
    blocks = 2 * _nbytes((tb, d), F32) + _nbytes((tb, d), BF16)
    scratch = _nbytes(w.shape, BF16) + 3 * _nbytes((tb, 2 * d), F32)
    row = pl.BlockSpec((tb, d), lambda i: (i, 0))
    return pl.pallas_call(
        _glu_kernel,
        grid=(rows // tb,),
        in_specs=[row, row, _const_spec((1, d)), _const_spec((1, d)), _const_spec(w.shape), _const_spec((1, d))],
        out_specs=row,
        out_shape=jax.ShapeDtypeStruct((rows, d), F32),
        scratch_shapes=[pltpu.VMEM((tb, d), F32)],
        compiler_params=pltpu.CompilerParams(
            dimension_semantics=("arbitrary",), vmem_limit_bytes=_vmem_limit(blocks, scratch)),
        name="s5_glu",
    )(x2d, y2d, g0, dskip, w, g1)


def _ffn_kernel(x_ref, buf_ref, g2_ref, wup_ref, cw_ref, cb_ref, wdown_ref, g3_ref,
                o_ref, nbuf_ref, carry, act, *, tb, dff, width):
    t = pl.program_id(1)
    pad = V7X_SUBLANES

    @pl.when(t == 0)
    def _():
        carry[...] = jnp.zeros_like(carry)
        carry[pad - (width - 1):pad, :] = buf_ref[0]

    x = x_ref[0]
    hn = _rms(x, g2_ref[...]).astype(BF16)
    fb = V7X_MXU
    nslab = dff // fb
    cols = lambda j: (slice(j * fb, (j + 1) * fb), slice(dff + j * fb, dff + (j + 1) * fb))

    def conv(u, cs):
        prev = carry[:, cs]
        carry[:, cs] = u[tb - pad:tb, :]
        return _causal_dwconv(u, prev, cw_ref, cb_ref, cs)

    up = lambda j: tuple(_dot(hn, wup_ref[:, cs]) for cs in cols(j))
    nxt = up(0)
    for j in range(nslab):
        ug, uv = nxt
        if j + 1 < nslab:
            nxt = up(j + 1)
        gate = conv(ug, cols(j)[0])
        val = conv(uv, cols(j)[1])
        act[:, j * fb:(j + 1) * fb] = (jax.nn.gelu(gate, approximate=True) * val).astype(BF16)

    f = _dot(act[...], wdown_ref[...])
    o_ref[0] = x + _rms(f, g3_ref[...])
    nbuf_ref[0] = carry[pad - (width - 1):pad, :]


def _layer_spec(shape, layer):
    nd = len(shape) - 1
    return pl.BlockSpec((None,) + tuple(shape[1:]), lambda *_: (layer,) + (0,) * nd, pipeline_mode=pl.Buffered(1))


def _ffn(x, buf, g2, wup, cw, cb, wdown, g3, layer, tb):
    b, s, d = x.shape
    dff = wdown.shape[1]
    width = cw.shape[1]
    kern = functools.partial(_ffn_kernel, tb=tb, dff=dff, width=width)
    blocks = 2 * _nbytes((tb, d), F32)
    scratch = (_nbytes(wup.shape[1:], BF16) + _nbytes(wdown.shape[1:], BF16) + 6 * _nbytes((tb, V7X_MXU), F32)
               + _nbytes((8, 2 * dff), F32) + _nbytes((tb, dff), BF16) + 3 * _nbytes((tb, d), F32))
    xs = pl.BlockSpec((1, tb, d), lambda i, t: (i, t, 0))
    bs = pl.BlockSpec((1, width - 1, 2 * dff), lambda i, t: (i, 0, 0))
    return pl.pallas_call(
        kern,
        grid=(b, s // tb),
        in_specs=[xs, bs, _const_spec((1, d)), _layer_spec(wup.shape, layer), _layer_spec(cw.shape, layer),
                  _layer_spec(cb.shape, layer), _layer_spec(wdown.shape, layer), _const_spec((1, d))],
        out_specs=[xs, bs],
        out_shape=[jax.ShapeDtypeStruct((b, s, d), F32), jax.ShapeDtypeStruct((b, width - 1, 2 * dff), F32)],
        scratch_shapes=[pltpu.VMEM((V7X_SUBLANES, 2 * dff), F32), pltpu.VMEM((tb, dff), BF16)],
        compiler_params=pltpu.CompilerParams(
            dimension_semantics=("arbitrary", "arbitrary"), vmem_limit_bytes=_vmem_limit(blocks, scratch)),
        name="conv_ffn",
    )(x, buf, g2, wup, cw, cb, wdown, g3)


def _mlstm_weights_kernel(aq_ref, ak_ref, av_ref, gq_ref, gk_ref, gv_ref, wup_ref,
                          wq_ref, wkt_ref, wv_ref, gc_ref, gh_ref, *, kscale):
    w = V7X_MXU
    kb = QKV_BLOCK
    dotp = functools.partial(jnp.dot, preferred_element_type=F32, precision=HI)
    lane = lax.broadcasted_iota(jnp.int32, (V7X_LANES, w), 1)
    src = lax.broadcasted_iota(jnp.int32, (V7X_LANES, w), 0)
    spread = jnp.where(jnp.bitwise_and(lane, kb - 1) == src, 1.0, 0.0)
    shift = kb.bit_length() - 1
    rblk = lax.shift_right_logical(lax.broadcasted_iota(jnp.int32, (w, w), 0), shift)
    cblk = lax.shift_right_logical(lax.broadcasted_iota(jnp.int32, (w, w), 1), shift)
    tile = lambda a_ref: jnp.where(rblk == cblk, dotp(a_ref[0], spread), 0.0)
    tq, tk, tv = tile(aq_ref), tile(ak_ref), tile(av_ref)
    wq_ref[0] = tq.astype(BF16)
    wkt_ref[0] = (tk.T * kscale).astype(BF16)
    wv_ref[0] = tv.astype(BF16)
    gc_ref[...] = (dotp(tq, gq_ref[...]) + dotp(tk, gk_ref[...])).astype(BF16)

    @pl.when(pl.program_id(0) == 0)
    def _():
        gh_ref[...] = jnp.zeros_like(gh_ref)

    gh_ref[...] += dotp(wup_ref[...], dotp(tv, gv_ref[...]))


def _mlstm_weights(wq, wk, wv, w_gate, w_up, kscale):
    nblk, kb, _ = wq.shape
    assert kb == QKV_BLOCK and kb & (kb - 1) == 0
    w = V7X_MXU
    per = w // kb
    nt = nblk // per
    inner = nt * w
    compact = lambda m: jnp.pad(m.astype(F32).reshape(nt, w, kb), ((0, 0), (0, 0), (0, V7X_LANES - kb)))
    wg = jnp.pad(w_gate.astype(F32), ((0, 0), (0, V7X_LANES - w_gate.shape[1])))
    cm = pl.BlockSpec((1, w, V7X_LANES), lambda i: (i, 0, 0))
    gs = [pl.BlockSpec((w, V7X_LANES), lambda i, k=k: (k * nt + i, 0)) for k in range(3)]
    tile = pl.BlockSpec((1, w, w), lambda i: (i, 0, 0))
    fold = pl.BlockSpec((w, V7X_LANES), lambda i: (i, 0))
    d = w_up.shape[0]
    return pl.pallas_call(
        functools.partial(_mlstm_weights_kernel, kscale=kscale),
        grid=(nt,),
        in_specs=[cm, cm, cm] + gs + [pl.BlockSpec((d, w), lambda i: (0, i))],
        out_specs=[tile, tile, tile, fold, pl.BlockSpec((d, V7X_LANES), lambda i: (0, 0))],
        out_shape=[jax.ShapeDtypeStruct((nt, w, w), BF16)] * 3
        + [jax.ShapeDtypeStruct((inner, V7X_LANES), BF16), jax.ShapeDtypeStruct((d, V7X_LANES), F32)],
        compiler_params=pltpu.CompilerParams(dimension_semantics=("arbitrary",)),
        name="mlstm_weights",
    )(compact(wq), compact(wk), compact(wv), wg, wg, wg, w_up.astype(F32))


def _mlstm_chunk(q_ref, kt_ref, v_ref, g, gt, ct_s, n_s, m_s, fill, emit_hn, fill_update, *, lc, heads, dh):
    row = lax.broadcasted_iota(jnp.int32, (lc, lc), 0)
    col = lax.broadcasted_iota(jnp.int32, (lc, lc), 1)
    causal = col <= row
    tri_l = jnp.where(causal, 1.0, 0.0).astype(BF16)
    tri_u = jnp.where(row <= col, 1.0, 0.0).astype(BF16)
    b_cols = sum(_dot(tri_l, part) for part in _split3(_log_sigmoid(g)))
    b_rows = sum(_dot(part, tri_u) for part in _split3(_log_sigmoid(gt)))

    hsl = [slice(h * dh, (h + 1) * dh) for h in range(heads)]
    st = []
    for h in range(heads):
        b_col = b_cols[:, heads + h:heads + h + 1]
        b_row = b_rows[heads + h:heads + h + 1, :]
        i_row = gt[h:h + 1, :]
        m_prev = m_s[h:h + 1, 0:1]
        dmat = jnp.where(causal, b_col - b_row + i_row, -jnp.inf)
        inter = b_col + m_prev
        m_t = jnp.maximum(inter, jnp.max(dmat, axis=1, keepdims=True))
        st.append(dict(b_col=b_col, b_row=b_row, i_row=i_row, m_prev=m_prev, m_t=m_t,
                       wts=jnp.exp(dmat - m_t), a=jnp.exp(inter - m_t)))
        fill()
    for h in range(heads):
        s = st[h]
        s['sc'] = _dot(q_ref[0, :, hsl[h]], kt_ref[0, hsl[h], :]) * s['wts']
    out_mm = lambda h: (_dot(st[h]['sc'].astype(BF16), v_ref[0, :, hsl[h]]),
                        _dot(q_ref[0, :, hsl[h]], ct_s[h].astype(BF16)))
    nxt = out_mm(0)
    for h in range(heads):
        intra, inter_mm = nxt
        if h + 1 < heads:
            nxt = out_mm(h + 1)
        fill()
        s = st[h]
        qh = q_ref[0, :, hsl[h]]
        sc = s['sc']
        a = s['a']
        num = intra + a * inter_mm
        qn = jnp.sum(qh.astype(F32) * n_s[h:h + 1, :], axis=1, keepdims=True)
        den = jnp.sum(sc, axis=1, keepdims=True) + a * qn
        hh = num / jnp.maximum(jnp.abs(den), jnp.exp(-s['m_t']))
        mu = jnp.mean(hh, axis=1, keepdims=True)
        dev = hh - mu
        var = jnp.mean(dev * dev, axis=1, keepdims=True)
        emit_hn(h, dev * lax.rsqrt(var + LN_EPS))
    def update_mm(h):
        s = st[h]
        kth = kt_ref[0, hsl[h], :]
        b_last = s['b_row'][:, lc - 1:lc]
        g_row = b_last - s['b_row'] + s['i_row']
        m_new = jnp.maximum(b_last + s['m_prev'], jnp.max(g_row, axis=1, keepdims=True))
        decay = jnp.exp(b_last + s['m_prev'] - m_new)
        wg_row = jnp.exp(g_row - m_new)
        wg_col = jnp.exp(b_last - s['b_col'] + g[:, h:h + 1] - m_new)
        wv = (v_ref[0, :, hsl[h]].astype(F32) * wg_col).astype(BF16)
        wg16 = jnp.broadcast_to(wg_row, (16, lc)).astype(BF16)
        return m_new, decay, _dot(kth, wv), _dot_nt(wg16, kth)[0:1, :]

    fill_update(0)
    nxt = update_mm(0)
    for h in range(heads):
        m_new, decay, c_upd, n_upd = nxt
        if h + 1 < heads:
            fill_update(h + 1)
            nxt = update_mm(h + 1)
        ct_s[h] = decay * ct_s[h] + c_upd
        n_s[h:h + 1, :] = decay * n_s[h:h + 1, :] + n_upd
        m_s[h:h + 1, :] = jnp.broadcast_to(m_new, (1, V7X_LANES))


def _mlstm_layer_kernel(*refs, lc, heads, inner, width, zero_init):
    (x_ref, cbuf_ref, g0_ref, wup_ref, cw_ref, cb_ref, wq_ref, wkt_ref, wv_ref, gc_ref, gh_ref, bg_ref,
     ng_ref, sk_ref, wdown_ref, g1_ref) = refs[:16]
    n_in = 16
    if not zero_init:
        c0_ref, n0_ref, m0_ref = refs[16:19]
        n_in = 19
    o_ref, cout_ref, nout_ref, mout_ref, nbuf_ref = refs[n_in:n_in + 5]
    carry, q_s, kt_s, v_s, xc_s, sz_s, act_s, ct_s, n_s, m_s = refs[n_in + 5:]
    dh = inner // heads
    pad = V7X_SUBLANES
    c = pl.program_id(1)

    @pl.when(c == 0)
    def _():
        carry[...] = jnp.zeros_like(carry)
        carry[pad - (width - 1):pad, :] = cbuf_ref[0]
        if zero_init:
            ct_s[...] = jnp.zeros_like(ct_s)
            n_s[...] = jnp.zeros_like(n_s)
            m_s[...] = jnp.zeros_like(m_s)
        else:
            for h in range(heads):
                ct_s[h] = c0_ref[0, h].T
            n_s[...] = n0_ref[0]
            m_s[...] = m0_ref[0]

    x = x_ref[0]
    hx = _rms(x, g0_ref[...]).astype(BF16)
    w = V7X_MXU
    gacc = bg_ref[...] + _dot(hx, gh_ref[...])
    nslab = inner // w
    col = lambda i: slice(i * w, (i + 1) * w)

    def project(i, xc, xmb, gacc):
        q_s[0, :, col(i)] = _dot(xc, wq_ref[i]).astype(BF16)
        kt_s[0, col(i), :] = _dot_nt(wkt_ref[i], xc).astype(BF16)
        v_s[0, :, col(i)] = _dot(xmb, wv_ref[i]).astype(BF16)
        xc_s[:, col(i)] = xc
        return gacc + _dot(xc, gc_ref[col(i), :])

    xm_next = _dot(hx, wup_ref[:, col(0)])
    pending = None
    for i in range(nslab):
        xm = xm_next
        if i + 1 < nslab:
            xm_next = _dot(hx, wup_ref[:, col(i + 1)])
        if pending is not None:
            gacc = project(*pending, gacc)
        prev = carry[:, col(i)]
        carry[:, col(i)] = xm[lc - pad:lc, :]
        xc = _silu(_causal_dwconv(xm, prev, cw_ref, cb_ref, col(i))).astype(BF16)
        pending = (i, xc, xm.astype(BF16))
    gacc = project(*pending, gacc)
    nbuf_ref[0] = carry[pad - (width - 1):pad, :]

    slabs = iter(range(inner // w))

    def gate_slab():
        i = next(slabs, None)
        if i is not None:
            cs = slice(i * w, (i + 1) * w)
            sz_s[:, cs] = _silu(_dot(hx, wup_ref[:, inner + i * w:inner + (i + 1) * w])).astype(BF16)

    hcols = lambda h: slice(h * dh, (h + 1) * dh)
    down = []

    def gated_out(h, hn):
        hs = hn * ng_ref[:, hcols(h)] + sk_ref[:, hcols(h)] * xc_s[:, hcols(h)].astype(F32)
        act_s[:, hcols(h)] = (hs * sz_s[:, hcols(h)].astype(F32)).astype(BF16)

    def down_proj(h):
        if h == 0:
            down.append(_dot(act_s[...], wdown_ref[...]))

    _mlstm_chunk(q_s, kt_s, v_s, gacc, gacc.T[0:2 * heads, :], ct_s, n_s, m_s, gate_slab, gated_out, down_proj,
                 lc=lc, heads=heads, dh=dh)
    for _ in slabs:
        raise AssertionError("fewer fill points than output-gate slabs")
    o_ref[0] = x + _rms(down[0], g1_ref[...])

    @pl.when(c == pl.num_programs(1) - 1)
    def _():
        for h in range(heads):
            cout_ref[0, h] = ct_s[h].T
        nout_ref[0] = n_s[...]
        mout_ref[0] = m_s[...]


def _mlstm_layer(x, cbuf, state, p, g0, g1, lc):
    b, s, d = x.shape
    heads = p['heads']
    inner = p['m_cw'].shape[1]
    width = p['m_cw'].shape[0]
    dh = inner // heads
    zero_init = state is None
    kern = functools.partial(_mlstm_layer_kernel, lc=lc, heads=heads, inner=inner, width=width,
                             zero_init=zero_init)
    xs = pl.BlockSpec((1, lc, d), lambda i, c: (i, c, 0))
    bufs = pl.BlockSpec((1, width - 1, inner), lambda i, c: (i, 0, 0))
    cs = pl.BlockSpec((1, heads, dh, dh), lambda i, c: (i, 0, 0, 0))
    ns = pl.BlockSpec((1, heads, dh), lambda i, c: (i, 0, 0))
    ms = pl.BlockSpec((1, heads, V7X_LANES), lambda i, c: (i, 0, 0))
    consts = [g0, p['m_wup'], p['m_cw'], p['m_cb'], p['m_wq'], p['m_wkt'], p['m_wv'], p['m_gc'], p['m_gh'],
              p['m_bg'], p['m_ng'], p['m_skip'], p['m_wdown'], g1]
    in_specs = [xs, bufs] + [_const_spec(a.shape) for a in consts]
    args = [x, cbuf] + consts
    if not zero_init:
        in_specs += [cs, ns, ms]
        args += list(state)
    blocks = 2 * _nbytes((lc, d), F32) + (1 if zero_init else 2) * _nbytes((heads, dh, dh), F32)
    scratch = (sum(_nbytes(a.shape, a.dtype) for a in consts) + 6 * _nbytes((lc, inner), BF16)
               + _nbytes((heads, dh, dh), F32) + 8 * _nbytes((lc, max(lc, dh)), F32) + 2 * _nbytes((dh, dh), F32))
    return pl.pallas_call(
        kern,
        grid=(b, s // lc),
        in_specs=in_specs,
        out_specs=[xs, cs, ns, ms, bufs],
        out_shape=[jax.ShapeDtypeStruct((b, s, d), F32), jax.ShapeDtypeStruct((b, heads, dh, dh), F32),
                   jax.ShapeDtypeStruct((b, heads, dh), F32), jax.ShapeDtypeStruct((b, heads, V7X_LANES), F32),
                   jax.ShapeDtypeStruct((b, width - 1, inner), F32)],
        scratch_shapes=[pltpu.VMEM((V7X_SUBLANES, inner), F32),
                        pltpu.VMEM((1, lc, inner), BF16), pltpu.VMEM((1, inner, lc), BF16),
                        pltpu.VMEM((1, lc, inner), BF16), pltpu.VMEM((lc, inner), BF16),
                        pltpu.VMEM((lc, inner), BF16), pltpu.VMEM((lc, inner), BF16),
                        pltpu.VMEM((heads, dh, dh), F32), pltpu.VMEM((heads, dh), F32),
                        pltpu.VMEM((heads, V7X_LANES), F32)],
        compiler_params=pltpu.CompilerParams(
            dimension_semantics=("arbitrary", "arbitrary"), vmem_limit_bytes=_vmem_limit(blocks, scratch)),
        name="mlstm_layer",
    )(*args)


def _prep_weights(w):
    row = lambda v: v.astype(F32).reshape(1, -1)
    p = {}
    p['norm_g'] = w['norm_g'].astype(F32)
    p['s5_mats'] = _s5_matrices(w['s5_A_re'][0], w['s5_A_im'][0], w['s5_log_dt'][0], w['s5_B_re'][0],
                                w['s5_B_im'][0], w['s5_C_re'][0], w['s5_C_im'][0])
    p['s5_D'] = row(w['s5_D'][0])
    p['s5_w_glu'] = w['s5_w_glu'][0].astype(BF16)
    inner = w['mlstm_conv_w'].shape[2]
    heads = w['mlstm_b_gate'].shape[1] // 2
    dh = inner // heads
    p['m_wq'], p['m_wkt'], p['m_wv'], p['m_gc'], gh = _mlstm_weights(
        w['mlstm_wq'][0], w['mlstm_wk'][0], w['mlstm_wv'][0], w['mlstm_w_gate'][0], w['mlstm_w_up'][0],
        dh ** -0.5)
    p['m_gh'] = gh.astype(BF16)
    bg = row(w['mlstm_b_gate'][0])
    p['m_bg'] = jnp.pad(bg, ((0, 0), (0, V7X_LANES - bg.shape[1])))
    p['m_wup'] = w['mlstm_w_up'][0].astype(BF16)
    p['m_cw'] = w['mlstm_conv_w'][0].astype(F32)
    p['m_cb'] = row(w['mlstm_conv_b'][0])
    p['m_ng'] = row(w['mlstm_norm_g'][0])
    p['m_skip'] = row(w['mlstm_skip'][0])
    p['m_wdown'] = w['mlstm_w_down'][0].astype(BF16)
    p['heads'] = heads
    p['f_wup'] = w['ffn_w_up'].astype(BF16)
    p['f_cw'] = w['ffn_conv_w'].astype(F32)
    p['f_cb'] = w['ffn_conv_b'].astype(F32)[:, None, :]
    p['f_wdown'] = w['ffn_w_down'].astype(BF16)
    return p


def _to_chunk_major(u2d, b, s):
    d = u2d.shape[1]
    g = d // S5_GROUP_CH
    u = u2d.reshape(b, s // S5_CHUNK, S5_CHUNK, g, S5_GROUP_CH)[:, :, ::-1]
    return jnp.transpose(u, (1, 0, 3, 2, 4)).reshape((s // S5_CHUNK) * b, d * S5_CHUNK)


def _to_token_major(yt, b, s):
    d = yt.shape[1] // S5_CHUNK
    g = d // S5_GROUP_CH
    y = yt.reshape(s // S5_CHUNK, b, g, S5_CHUNK, S5_GROUP_CH)
    return jnp.transpose(y, (1, 0, 3, 2, 4)).reshape(b * s, d)


def _trunk(x, st, p, tiles):
    b, s, d = x.shape
    ng = p['norm_g']
    g_of = lambda layer, k: ng[layer, k].reshape(1, d)
    zero_init = st is None
    groups = d // S5_GROUP_CH
    x2d = x.reshape(b * s, d)

    n_state = p['s5_mats'][1].shape[1] // 2
    if zero_init:
        h0re = h0im = jnp.zeros((groups // 2, b, 2 * n_state), F32)
    else:
        pair = lambda h: jnp.transpose(h.astype(F32).reshape(b, groups // 2, 2 * n_state), (1, 0, 2))
        h0re, h0im = pair(st['s5_re']), pair(st['s5_im'])
    if tiles.s5_time:
        y, hre, him = _s5_fused(x, g_of(0, 0), p['s5_mats'], h0re, h0im, tiles.s5_time)
        y2d = y.reshape(b * s, d)
    else:
        u = _norm_cast(x2d, g_of(0, 0), tiles.rows)
        yt, hre, him = _s5_core(_to_chunk_major(u, b, s), p['s5_mats'], h0re, h0im, b, tiles.s5_rows)
        y2d = _to_token_major(yt, b, s)
    unpair = lambda h: jnp.transpose(h, (1, 0, 2)).reshape(1, b, groups, n_state)
    o_re, o_im = unpair(hre), unpair(him)
    x2d = _glu(x2d, y2d, g_of(0, 0), p['s5_D'], p['s5_w_glu'], g_of(0, 1), tiles.rows)

    def ffn(x2d, layer):
        dff2 = p['f_wup'].shape[2]
        width = p['f_cw'].shape[1]
        buf = jnp.zeros((b, width - 1, dff2), F32) if zero_init else st['ffn_conv'][layer].astype(F32)
        xo, nbuf = _ffn(x2d.reshape(b, s, d), buf, g_of(layer, 2), p['f_wup'], p['f_cw'], p['f_cb'],
                        p['f_wdown'], g_of(layer, 3), layer, tiles.ffn_time)
        return xo.reshape(b * s, d), nbuf

    x2d, fbuf0 = ffn(x2d, 0)

    heads = p['heads']
    inner = p['m_cw'].shape[1]
    width = p['m_cw'].shape[0]
    cbuf = jnp.zeros((b, width - 1, inner), F32) if zero_init else st['mlstm_conv'].astype(F32)
    if zero_init:
        state = None
    else:
        state = (st['mlstm_C'].astype(F32), st['mlstm_n'].astype(F32),
                 jnp.broadcast_to(st['mlstm_m'].astype(F32)[:, :, None], (b, heads, V7X_LANES)))
    x3, c_out, n_out, m_out, ncbuf = _mlstm_layer(x2d.reshape(b, s, d), cbuf, state, p, g_of(1, 0), g_of(1, 1),
                                                  tiles.mlstm_chunk)
    x2d, fbuf1 = ffn(x3.reshape(b * s, d), 1)

    return (x2d.reshape(b, s, d), o_re, o_im, c_out[None], n_out[None], m_out[None, :, :, 0], ncbuf[None],
            jnp.stack([fbuf0, fbuf1], axis=0))


def kernel(x_prompt, x_sample, state_s5_re, state_s5_im, state_mlstm_C, state_mlstm_n, state_mlstm_m, state_mlstm_conv, state_ffn_conv, norm_g, s5_A_re, s5_A_im, s5_log_dt, s5_B_re, s5_B_im, s5_C_re, s5_C_im, s5_D, s5_w_glu, mlstm_w_up, mlstm_conv_w, mlstm_conv_b, mlstm_wq, mlstm_wk, mlstm_wv, mlstm_w_gate, mlstm_b_gate, mlstm_norm_g, mlstm_skip, mlstm_w_down, ffn_w_up, ffn_conv_w, ffn_conv_b, ffn_w_down):
    w = {
        'norm_g': norm_g, 's5_A_re': s5_A_re, 's5_A_im': s5_A_im, 's5_log_dt': s5_log_dt,
        's5_B_re': s5_B_re, 's5_B_im': s5_B_im, 's5_C_re': s5_C_re, 's5_C_im': s5_C_im, 's5_D': s5_D,
        's5_w_glu': s5_w_glu, 'mlstm_w_up': mlstm_w_up, 'mlstm_conv_w': mlstm_conv_w,
        'mlstm_conv_b': mlstm_conv_b, 'mlstm_wq': mlstm_wq, 'mlstm_wk': mlstm_wk, 'mlstm_wv': mlstm_wv,
        'mlstm_w_gate': mlstm_w_gate, 'mlstm_b_gate': mlstm_b_gate, 'mlstm_norm_g': mlstm_norm_g,
        'mlstm_skip': mlstm_skip, 'mlstm_w_down': mlstm_w_down, 'ffn_w_up': ffn_w_up,
        'ffn_conv_w': ffn_conv_w, 'ffn_conv_b': ffn_conv_b, 'ffn_w_down': ffn_w_down,
    }
    assert norm_g.shape[0] == 2 and s5_A_re.shape[0] == 1 and mlstm_w_up.shape[0] == 1
    assert x_prompt.shape[0] == V7X_SUBLANES and x_sample.shape[0] == V7X_SUBLANES
    p = _prep_weights(w)
    out_p = _trunk(x_prompt.astype(F32), None, p, _pick_tiles(*x_prompt.shape[:2]))
    st = {'s5_re': state_s5_re[0], 's5_im': state_s5_im[0], 'mlstm_C': state_mlstm_C[0],
          'mlstm_n': state_mlstm_n[0], 'mlstm_m': state_mlstm_m[0], 'mlstm_conv': state_mlstm_conv[0],
          'ffn_conv': state_ffn_conv}
    out_s = _trunk(x_sample.astype(F32), st, p, _pick_tiles(*x_sample.shape[:2]))
    return (out_p[0], out_s[0]) + tuple(out_p[1:]) + tuple(out_s[1:])
```

```python
import functools
import math
from typing import NamedTuple

import jax
import jax.numpy as jnp
from jax import lax
from jax.experimental import pallas as pl
from jax.experimental.pallas import tpu as pltpu

F32 = jnp.float32
BF16 = jnp.bfloat16
HI = lax.Precision.HIGHEST

NORM_EPS = 1e-6
LN_EPS = 1e-5
S5_GROUP_CH = 16
S5_CHUNK = 16
QKV_BLOCK = 4
S5_SLABS_PER_STEP = 2

V7X_LANES = 128
V7X_SUBLANES = 8
V7X_MXU = 256
V7X_VMEM_BYTES = 64 * 1024 * 1024


class Tiles(NamedTuple):
    rows: int
    s5_rows: int
    mlstm_chunk: int
    s5_time: int
    ffn_time: int


def _pick_tiles(batch, seq):
    s5_time = (V7X_LANES // batch) * S5_CHUNK
    return Tiles(rows=min(batch * seq, 512),
                 s5_rows=min(batch * (seq // S5_CHUNK), 512), mlstm_chunk=min(seq, 256),
                 s5_time=s5_time if seq % s5_time == 0 else 0, ffn_time=min(seq, 1024))


def _vmem_limit(block_bytes, scratch_bytes):
    want = 2 * block_bytes + scratch_bytes + 16 * 1024 * 1024
    return int(min(want, V7X_VMEM_BYTES - 8 * 1024 * 1024))


def _nbytes(shape, dtype):
    return math.prod(shape) * jnp.dtype(dtype).itemsize


def _rms(x, g):
    return x * lax.rsqrt(jnp.mean(x * x, axis=-1, keepdims=True) + NORM_EPS) * g


def _dot(a, b):
    return jnp.dot(a, b, preferred_element_type=F32)


def _dot_nt(a, b):
    return lax.dot_general(a, b, (((1,), (1,)), ((), ())), preferred_element_type=F32)


def _silu(x):
    return (0.5 * x) * (1.0 + jnp.tanh(0.5 * x))


def _causal_dwconv(u, prev, cw_ref, cb_ref, cs):
    width = cw_ref.shape[0]
    acc = cb_ref[:, cs] + u * cw_ref[width - 1:width, cs]
    rows = lax.broadcasted_iota(jnp.int32, prev.shape, 0)
    for j in range(width - 1):
        d = width - 1 - j
        rolled = pltpu.roll(u, d, 0)
        head = jnp.where(rows < d, pltpu.roll(prev, d, 0), rolled[0:V7X_SUBLANES])
        shifted = jnp.concatenate([head, rolled[V7X_SUBLANES:]], axis=0)
        acc = acc + shifted * cw_ref[j:j + 1, cs]
    return acc


def _log_sigmoid(x):
    return jnp.minimum(x, 0.0) - jnp.log1p(jnp.exp(-jnp.abs(x)))


def _split3(x):
    hi = x.astype(BF16)
    r1 = x - hi.astype(F32)
    mid = r1.astype(BF16)
    lo = (r1 - mid.astype(F32)).astype(BF16)
    return hi, mid, lo


def _norm_cast_kernel(x_ref, g_ref, o_ref):
    o_ref[...] = _rms(x_ref[...], g_ref[...]).astype(o_ref.dtype)


def _norm_cast(x2d, g, tb):
    rows, d = x2d.shape
    return pl.pallas_call(
        _norm_cast_kernel,
        grid=(rows // tb,),
        in_specs=[pl.BlockSpec((tb, d), lambda i: (i, 0)), pl.BlockSpec((1, d), lambda i: (0, 0))],
        out_specs=pl.BlockSpec((tb, d), lambda i: (i, 0)),
        out_shape=jax.ShapeDtypeStruct((rows, d), BF16),
        name="s5_norm",
    )(x2d, g)


def _s5_kernel(u_ref, tt_ref, pt_ref, qt_ref, are_ref, aim_ref, h0re_ref, h0im_ref,
               y_ref, hre_ref, him_ref, pure_s, puim_s, hinre_s, hinim_s, stre_s, stim_s, *, nb, rb):
    r = pl.program_id(1)

    @pl.when(r == 0)
    def _():
        stre_s[...] = h0re_ref[0]
        stim_s[...] = h0im_ref[0]

    u = u_ref[...]
    w = V7X_MXU
    half = pt_ref.shape[1] // 2
    pu = [_dot_nt(u[:, i * w:(i + 1) * w], pt_ref[i]) for i in range(2)]
    pure_s[...] = jnp.concatenate([pu[0][:, :half], pu[1][:, :half]], axis=1)
    puim_s[...] = jnp.concatenate([pu[0][:, half:], pu[1][:, half:]], axis=1)
    ar = jnp.broadcast_to(are_ref[0], (nb, 2 * half))
    ai = jnp.broadcast_to(aim_ref[0], (nb, 2 * half))

    def step(i, carry):
        re, im = carry
        rows = pl.ds(pl.multiple_of(i * nb, nb), nb)
        hinre_s[rows, :] = re
        hinim_s[rows, :] = im
        return (ar * re - ai * im + pure_s[rows, :], ar * im + ai * re + puim_s[rows, :])

    re, im = lax.fori_loop(0, rb // nb, step, (stre_s[...], stim_s[...]))
    stre_s[...] = re
    stim_s[...] = im
    hre_ref[0] = re
    him_ref[0] = im

    hre = hinre_s[...]
    him = hinim_s[...]
    for i in range(2):
        hin = jnp.concatenate([hre[:, i * half:(i + 1) * half], him[:, i * half:(i + 1) * half]], axis=1)
        y_ref[:, i * w:(i + 1) * w] = (_dot_nt(u[:, i * w:(i + 1) * w], tt_ref[i])
                                       + _dot_nt(hin.astype(BF16), qt_ref[i])).astype(y_ref.dtype)


def _s5_core(ut, mats, h0re, h0im, nb, rb):
    rows, cols = ut.shape
    pairs = cols // (2 * V7X_MXU)
    tt, pt, qt, are, aim = mats
    nst = pt.shape[1]
    pair3 = lambda p, r: (p, 0, 0)
    kern = functools.partial(_s5_kernel, nb=nb, rb=rb)
    blocks = (_nbytes((rb, 512), BF16) * 2 + _nbytes((2, 256, 256), BF16) + 4 * _nbytes((256, 128), BF16))
    scratch = 4 * _nbytes((rb, 128), F32) + 2 * _nbytes((nb, 128), F32)
    return pl.pallas_call(
        kern,
        grid=(pairs, rows // rb),
        in_specs=[
            pl.BlockSpec((rb, 2 * V7X_MXU), lambda p, r: (r, p)),
            pl.BlockSpec((2, V7X_MXU, V7X_MXU), pair3),
            pl.BlockSpec((2, nst, V7X_MXU), pair3),
            pl.BlockSpec((2, V7X_MXU, nst), pair3),
            pl.BlockSpec((1, 1, 128), pair3),
            pl.BlockSpec((1, 1, 128), pair3),
            pl.BlockSpec((1, nb, 128), pair3),
            pl.BlockSpec((1, nb, 128), pair3),
        ],
        out_specs=[
            pl.BlockSpec((rb, 2 * V7X_MXU), lambda p, r: (r, p)),
            pl.BlockSpec((1, nb, 128), pair3),
            pl.BlockSpec((1, nb, 128), pair3),
        ],
        out_shape=[
            jax.ShapeDtypeStruct((rows, cols), BF16),
            jax.ShapeDtypeStruct((pairs, nb, 128), F32),
            jax.ShapeDtypeStruct((pairs, nb, 128), F32),
        ],
        scratch_shapes=[pltpu.VMEM((rb, 128), F32)] * 4 + [pltpu.VMEM((nb, 128), F32)] * 2,
        compiler_params=pltpu.CompilerParams(
            dimension_semantics=("arbitrary", "arbitrary"), vmem_limit_bytes=_vmem_limit(blocks, scratch)),
        name="s5_core",
    )(ut, tt, pt, qt, are, aim, h0re, h0im)


def _s5_fused_kernel(x_ref, g_ref, tt_ref, pt_ref, qt_ref, are_ref, aim_ref, h0re_ref, h0im_ref,
                     y_ref, hre_ref, him_ref,
                     us, ugt, pure, puim, hinre, hinim, ytmp, ys, stre, stim, *, nb, tbt, spb):
    lc = S5_CHUNK
    ch = S5_GROUP_CH
    gps = V7X_LANES // ch
    nch = tbt // lc
    cols = nb * nch
    k = pl.program_id(1)
    ppg = gps // 2
    slab = lambda sl: k * spb + sl

    @pl.when(k == 0)
    def _():
        for b in range(nb):
            u = _rms(x_ref[b], g_ref[...])
            for kk in range(us.shape[0]):
                us[kk, pl.ds(b, tbt, stride=nb), :] = u[:, kk * V7X_LANES:(kk + 1) * V7X_LANES]

    @pl.when(pl.program_id(0) == 0)
    def _():
        for sl in range(spb):
            stre[slab(sl)] = h0re_ref[sl * ppg:(sl + 1) * ppg]
            stim[slab(sl)] = h0im_ref[sl * ppg:(sl + 1) * ppg]

    tile = lambda j, s: slice((j * lc + s) * nb, (j * lc + s + 1) * nb)
    for sl in range(spb):
        for s in range(lc):
            a = jnp.concatenate([us[slab(sl), tile(j, s), :] for j in range(nch)], axis=0).T
            for g in range(gps):
                ugt[sl * gps + g, (lc - 1 - s) * ch:(lc - s) * ch, :] = a[g * ch:(g + 1) * ch, :].astype(BF16)

    half = pt_ref.shape[1] // 2
    for p in range(spb * ppg):
        put0 = _dot(pt_ref[2 * p], ugt[2 * p])
        put1 = _dot(pt_ref[2 * p + 1], ugt[2 * p + 1])
        pure[p] = jnp.concatenate([put0[:half], put1[:half]], axis=0).T
        puim[p] = jnp.concatenate([put0[half:], put1[half:]], axis=0).T

    chains = [(sl, pr) for sl in range(spb) for pr in range(ppg)]
    state = [(stre[slab(sl), pr], stim[slab(sl), pr]) for sl, pr in chains]
    coef = [(jnp.broadcast_to(are_ref[sl * ppg + pr], (nb, 2 * half)),
             jnp.broadcast_to(aim_ref[sl * ppg + pr], (nb, 2 * half))) for sl, pr in chains]
    for j in range(nch):
        rows = slice(j * nb, (j + 1) * nb)
        for c, (sl, pr) in enumerate(chains):
            p = sl * ppg + pr
            re, im = state[c]
            ar, ai = coef[c]
            hinre[p, rows, :] = re
            hinim[p, rows, :] = im
            state[c] = (ar * re - ai * im + pure[p, rows, :], ar * im + ai * re + puim[p, rows, :])
    for c, (sl, pr) in enumerate(chains):
        re, im = state[c]
        stre[slab(sl), pr] = re
        stim[slab(sl), pr] = im
        hre_ref[slab(sl) * ppg + pr] = re
        him_ref[slab(sl) * ppg + pr] = im

    for p in range(spb * ppg):
        hre_t = hinre[p].T
        him_t = hinim[p].T
        for i in range(2):
            g = 2 * p + i
            hin_t = jnp.concatenate([hre_t[i * half:(i + 1) * half], him_t[i * half:(i + 1) * half]],
                                    axis=0).astype(BF16)
            ytmp[g] = _dot(tt_ref[g], ugt[g]) + _dot(qt_ref[g], hin_t)

    for sl in range(spb):
        for t in range(lc):
            zt = jnp.concatenate([ytmp[sl * gps + g, t * ch:(t + 1) * ch, :] for g in range(gps)], axis=0).T
            for j in range(nch):
                ys[sl, tile(j, t), :] = zt[j * nb:(j + 1) * nb, :]
    for sl in range(spb):
        for b in range(nb):
            y_ref[b, :, sl * V7X_LANES:(sl + 1) * V7X_LANES] = ys[sl, pl.ds(b, tbt, stride=nb), :].astype(y_ref.dtype)


def _s5_fused(x, g, mats_t, h0re, h0im, tbt):
    nb, s, d = x.shape
    tt, pt, qt, are, aim = mats_t
    groups = tt.shape[0]
    gps = V7X_LANES // S5_GROUP_CH
    slabs = groups // gps
    w = S5_CHUNK * S5_GROUP_CH
    nst = pt.shape[1]
    cols = nb * (tbt // S5_CHUNK)
    spb = S5_SLABS_PER_STEP
    ng, npair = spb * gps, spb * gps // 2
    kern = functools.partial(_s5_fused_kernel, nb=nb, tbt=tbt, spb=spb)
    slab3 = lambda t, k: (k, 0, 0)
    blocks = (_nbytes((nb, tbt, d), F32) + _nbytes((nb, tbt, spb * V7X_LANES), BF16)
              + ng * (_nbytes((w, w), BF16) + 2 * _nbytes((w, nst), BF16)))
    scratch = (_nbytes((slabs, nb * tbt, V7X_LANES), F32) + _nbytes((ng, w, cols), BF16)
               + 4 * _nbytes((npair, cols, nst), F32) + _nbytes((ng, w, cols), F32)
               + _nbytes((spb, nb * tbt, V7X_LANES), F32))
    return pl.pallas_call(
        kern,
        grid=(s // tbt, slabs // spb),
        in_specs=[
            pl.BlockSpec((nb, tbt, d), lambda t, k: (0, t, 0)),
            pl.BlockSpec((1, d), lambda t, k: (0, 0)),
            pl.BlockSpec((ng, w, w), slab3),
            pl.BlockSpec((ng, nst, w), slab3),
            pl.BlockSpec((ng, w, nst), slab3),
            pl.BlockSpec((npair, 1, nst), slab3),
            pl.BlockSpec((npair, 1, nst), slab3),
            pl.BlockSpec((npair, nb, nst), slab3),
            pl.BlockSpec((npair, nb, nst), slab3),
        ],
        out_specs=[
            pl.BlockSpec((nb, tbt, spb * V7X_LANES), lambda t, k: (0, t, k)),
            pl.BlockSpec((groups // 2, nb, nst), lambda t, k: (0, 0, 0)),
            pl.BlockSpec((groups // 2, nb, nst), lambda t, k: (0, 0, 0)),
        ],
        out_shape=[
            jax.ShapeDtypeStruct((nb, s, d), BF16),
            jax.ShapeDtypeStruct((groups // 2, nb, nst), F32),
            jax.ShapeDtypeStruct((groups // 2, nb, nst), F32),
        ],
        scratch_shapes=[
            pltpu.VMEM((slabs, nb * tbt, V7X_LANES), F32),
            pltpu.VMEM((ng, w, cols), BF16),
            pltpu.VMEM((npair, cols, nst), F32),
            pltpu.VMEM((npair, cols, nst), F32),
            pltpu.VMEM((npair, cols, nst), F32),
            pltpu.VMEM((npair, cols, nst), F32),
            pltpu.VMEM((ng, w, cols), F32),
            pltpu.VMEM((spb, nb * tbt, V7X_LANES), F32),
            pltpu.VMEM((slabs, gps // 2, nb, nst), F32),
            pltpu.VMEM((slabs, gps // 2, nb, nst), F32),
        ],
        compiler_params=pltpu.CompilerParams(
            dimension_semantics=("arbitrary", "arbitrary"), vmem_limit_bytes=_vmem_limit(blocks, scratch)),
        name="s5_fused",
    )(x, g, tt, pt, qt, are, aim, h0re, h0im)


def _s5_matrices(a_re, a_im, log_dt, b_re, b_im, c_re, c_im):
    lc = S5_CHUNK
    groups, n = a_re.shape
    ch = S5_GROUP_CH
    f = lambda v: v.astype(F32)
    a_re, a_im, b_re, b_im, c_re, c_im = map(f, (a_re, a_im, b_re, b_im, c_re, c_im))
    dt = jnp.exp(f(log_dt))[:, None, None]
    j = jnp.arange(lc + 1, dtype=F32)
    mag = jnp.exp((a_re[:, :, None] * dt) * j)
    ang = (a_im[:, :, None] * dt) * j
    pw_re, pw_im = mag * jnp.cos(ang), mag * jnp.sin(ang)
    e_re, e_im = pw_re[:, :, 1] - 1.0, pw_im[:, :, 1]
    inv = 1.0 / (a_re * a_re + a_im * a_im)
    f_re, f_im = (e_re * a_re + e_im * a_im) * inv, (e_im * a_re - e_re * a_im) * inv
    bb_re = f_re[..., None] * b_re - f_im[..., None] * b_im
    bb_im = f_re[..., None] * b_im + f_im[..., None] * b_re
    rep = lambda v: jnp.repeat(v, ch, axis=2)
    til = lambda v: jnp.tile(v, (1, 1, lc + 1))
    lb_re = rep(pw_re) * til(bb_re) - rep(pw_im) * til(bb_im)
    lb_im = rep(pw_re) * til(bb_im) + rep(pw_im) * til(bb_re)
    kern = jnp.einsum('gcm,gmx->gcx', jnp.concatenate([c_re, c_im], axis=2),
                      jnp.concatenate([lb_re, -lb_im], axis=1), precision=HI)
    w = lc * ch
    kfwd = jnp.pad(kern[:, :, :w], ((0, 0), (0, 0), ((lc - 1) * ch, 0)))
    tt = jnp.stack([kfwd[:, :, t * ch:t * ch + w] for t in range(lc)], axis=1).reshape(groups, w, w)
    pt = jnp.concatenate([lb_re[:, :, :w], lb_im[:, :, :w]], axis=1)
    pwt_re = jnp.swapaxes(pw_re, 1, 2)[:, 1:, None, :]
    pwt_im = jnp.swapaxes(pw_im, 1, 2)[:, 1:, None, :]
    ca_re = c_re[:, None] * pwt_re - c_im[:, None] * pwt_im
    ca_im = c_re[:, None] * pwt_im + c_im[:, None] * pwt_re
    qt = jnp.concatenate([ca_re, -ca_im], axis=3).reshape(groups, w, 2 * n)
    are = pw_re[:, :, lc].reshape(groups // 2, 1, 2 * n)
    aim = pw_im[:, :, lc].reshape(groups // 2, 1, 2 * n)
    return tt.astype(BF16), pt.astype(BF16), qt.astype(BF16), are, aim


def _glu_kernel(x_ref, y_ref, g0_ref, d_ref, w_ref, g1_ref, o_ref, out_s):
    x = x_ref[...]
    d = x.shape[1]
    u = _rms(x, g0_ref[...])
    yy = y_ref[...].astype(F32) + d_ref[...] * u
    z = jax.nn.gelu(yy, approximate=True).astype(BF16)
    w = V7X_MXU
    nslab = d // w
    pair = lambda j: (_dot(z, w_ref[:, j * w:(j + 1) * w]), _dot(z, w_ref[:, d + j * w:d + (j + 1) * w]))
    nxt = pair(0)
    ssq = jnp.zeros((x.shape[0], 1), F32)
    for j in range(nslab):
        a, b = nxt
        if j + 1 < nslab:
            nxt = pair(j + 1)
        out = a * (0.5 * (1.0 + jnp.tanh(0.5 * b)))
        out_s[:, j * w:(j + 1) * w] = out
        ssq = ssq + jnp.sum(out * out, axis=-1, keepdims=True)
    scale = lax.rsqrt(ssq * (1.0 / d) + NORM_EPS)
    o_ref[...] = x + out_s[...] * scale * g1_ref[...]


def _const_spec(shape):
    nd = len(shape)
    return pl.BlockSpec(shape, lambda *_: (0,) * nd, pipeline_mode=pl.Buffered(1))


def _glu(x2d, y2d, g0, dskip, w, g1, tb):
    rows, d = x2d.shape
    blocks = 2 * _nbytes((tb, d), F32) + _nbytes((tb, d), BF16)
    scratch = _nbytes(w.shape, BF16) + 3 * _nbytes((tb, 2 * d), F32)
    row = pl.BlockSpec((tb, d), lambda i: (i, 0))
    return pl.pallas_call(
        _glu_kernel,
        grid=(rows // tb,),
        in_specs=[row, row, _const_spec((1, d)), _const_spec((1, d)), _const_spec(w.shape), _const_spec((1, d))],
        out_specs=row,
        out_shape=jax.ShapeDtypeStruct((rows, d), F32),
        scratch_shapes=[pltpu.VMEM((tb, d), F32)],
        compiler_params=pltpu.CompilerParams(
            dimension_semantics=("arbitrary",), vmem_limit_bytes=_vmem_limit(blocks, scratch)),
        name="s5_glu",
    )(x2d, y2d, g0, dskip, w, g1)


def _ffn_kernel(x_ref, buf_ref, g2_ref, wup_ref, cw_ref, cb_ref, wdown_ref, g3_ref,
                o_ref, nbuf_ref, carry, act, *, tb, dff, width):
    t = pl.program_id(1)
    pad = V7X_SUBLANES

    @pl.when(t == 0)
    def _():
        carry[...] = jnp.zeros_like(carry)
        carry[pad - (width - 1):pad, :] = buf_ref[0]

    x = x_ref[0]
    hn = _rms(x, g2_ref[...]).astype(BF16)
    fb = V7X_MXU
    nslab = dff // fb
    cols = lambda j: (slice(j * fb, (j + 1) * fb), slice(dff + j * fb, dff + (j + 1) * fb))

    def conv(u, cs):
        prev = carry[:, cs]
        carry[:, cs] = u[tb - pad:tb, :]
        return _causal_dwconv(u, prev, cw_ref, cb_ref, cs)

    up = lambda j: tuple(_dot(hn, wup_ref[:, cs]) for cs in cols(j))
    nxt = up(0)
    for j in range(nslab):
        ug, uv = nxt
        if j + 1 < nslab:
            nxt = up(j + 1)
        gate = conv(ug, cols(j)[0])
        val = conv(uv, cols(j)[1])
        act[:, j * fb:(j + 1) * fb] = (jax.nn.gelu(gate, approximate=True) * val).astype(BF16)

    a = act[...]
    f = jnp.concatenate([_dot(a, wdown_ref[:, n * fb:(n + 1) * fb]) for n in range(x.shape[1] // fb)], axis=1)
    o_ref[0] = x + _rms(f, g3_ref[...])
    nbuf_ref[0] = carry[pad - (width - 1):pad, :]


def _layer_spec(shape, layer):
    nd = len(shape) - 1
    return pl.BlockSpec((None,) + tuple(shape[1:]), lambda *_: (layer,) + (0,) * nd, pipeline_mode=pl.Buffered(1))


def _ffn(x, buf, g2, wup, cw, cb, wdown, g3, layer, tb):
    b, s, d = x.shape
    dff = wdown.shape[1]
    width = cw.shape[1]
    kern = functools.partial(_ffn_kernel, tb=tb, dff=dff, width=width)
    blocks = 2 * _nbytes((tb, d), F32)
    scratch = (_nbytes(wup.shape[1:], BF16) + _nbytes(wdown.shape[1:], BF16) + 6 * _nbytes((tb, V7X_MXU), F32)
               + _nbytes((8, 2 * dff), F32) + _nbytes((tb, dff), BF16) + 3 * _nbytes((tb, d), F32))
    xs = pl.BlockSpec((1, tb, d), lambda i, t: (i, t, 0))
    bs = pl.BlockSpec((1, width - 1, 2 * dff), lambda i, t: (i, 0, 0))
    return pl.pallas_call(
        kern,
        grid=(b, s // tb),
        in_specs=[xs, bs, _const_spec((1, d)), _layer_spec(wup.shape, layer), _layer_spec(cw.shape, layer),
                  _layer_spec(cb.shape, layer), _layer_spec(wdown.shape, layer), _const_spec((1, d))],
        out_specs=[xs, bs],
        out_shape=[jax.ShapeDtypeStruct((b, s, d), F32), jax.ShapeDtypeStruct((b, width - 1, 2 * dff), F32)],
        scratch_shapes=[pltpu.VMEM((V7X_SUBLANES, 2 * dff), F32), pltpu.VMEM((tb, dff), BF16)],
        compiler_params=pltpu.CompilerParams(
            dimension_semantics=("arbitrary", "arbitrary"), vmem_limit_bytes=_vmem_limit(blocks, scratch)),
        name="conv_ffn",
    )(x, buf, g2, wup, cw, cb, wdown, g3)


def _mlstm_weights_kernel(aq_ref, ak_ref, av_ref, gq_ref, gk_ref, gv_ref, wup_ref,
                          wq_ref, wkt_ref, wv_ref, gc_ref, gh_ref, *, kscale):
    w = V7X_MXU
    kb = QKV_BLOCK
    dotp = functools.partial(jnp.dot, preferred_element_type=F32, precision=HI)
    lane = lax.broadcasted_iota(jnp.int32, (V7X_LANES, w), 1)
    src = lax.broadcasted_iota(jnp.int32, (V7X_LANES, w), 0)
    spread = jnp.where(jnp.bitwise_and(lane, kb - 1) == src, 1.0, 0.0)
    shift = kb.bit_length() - 1
    rblk = lax.shift_right_logical(lax.broadcasted_iota(jnp.int32, (w, w), 0), shift)
    cblk = lax.shift_right_logical(lax.broadcasted_iota(jnp.int32, (w, w), 1), shift)
    tile = lambda a_ref: jnp.where(rblk == cblk, dotp(a_ref[0], spread), 0.0)
    tq, tk, tv = tile(aq_ref), tile(ak_ref), tile(av_ref)
    wq_ref[0] = tq.astype(BF16)
    wkt_ref[0] = (tk.T * kscale).astype(BF16)
    wv_ref[0] = tv.astype(BF16)
    gc_ref[...] = (dotp(tq, gq_ref[...]) + dotp(tk, gk_ref[...])).astype(BF16)

    @pl.when(pl.program_id(0) == 0)
    def _():
        gh_ref[...] = jnp.zeros_like(gh_ref)

    gh_ref[...] += dotp(wup_ref[...], dotp(tv, gv_ref[...]))


def _mlstm_weights(wq, wk, wv, w_gate, w_up, kscale):
    nblk, kb, _ = wq.shape
    assert kb == QKV_BLOCK and kb & (kb - 1) == 0
    w = V7X_MXU
    per = w // kb
    nt = nblk // per
    inner = nt * w
    compact = lambda m: jnp.pad(m.astype(F32).reshape(nt, w, kb), ((0, 0), (0, 0), (0, V7X_LANES - kb)))
    wg = jnp.pad(w_gate.astype(F32), ((0, 0), (0, V7X_LANES - w_gate.shape[1])))
    cm = pl.BlockSpec((1, w, V7X_LANES), lambda i: (i, 0, 0))
    gs = [pl.BlockSpec((w, V7X_LANES), lambda i, k=k: (k * nt + i, 0)) for k in range(3)]
    tile = pl.BlockSpec((1, w, w), lambda i: (i, 0, 0))
    fold = pl.BlockSpec((w, V7X_LANES), lambda i: (i, 0))
    d = w_up.shape[0]
    return pl.pallas_call(
        functools.partial(_mlstm_weights_kernel, kscale=kscale),
        grid=(nt,),
        in_specs=[cm, cm, cm] + gs + [pl.BlockSpec((d, w), lambda i: (0, i))],
        out_specs=[tile, tile, tile, fold, pl.BlockSpec((d, V7X_LANES), lambda i: (0, 0))],
        out_shape=[jax.ShapeDtypeStruct((nt, w, w), BF16)] * 3
        + [jax.ShapeDtypeStruct((inner, V7X_LANES), BF16), jax.ShapeDtypeStruct((d, V7X_LANES), F32)],
        compiler_params=pltpu.CompilerParams(dimension_semantics=("arbitrary",)),
        name="mlstm_weights",
    )(compact(wq), compact(wk), compact(wv), wg, wg, wg, w_up.astype(F32))


def _mlstm_chunk(q_ref, kt_ref, v_ref, g, gt, ct_s, n_s, m_s, fill, emit_hn, fill_update, *, lc, heads, dh):
    row = lax.broadcasted_iota(jnp.int32, (lc, lc), 0)
    col = lax.broadcasted_iota(jnp.int32, (lc, lc), 1)
    causal = col <= row
    tri_l = jnp.where(causal, 1.0, 0.0).astype(BF16)
    tri_u = jnp.where(row <= col, 1.0, 0.0).astype(BF16)
    b_cols = sum(_dot(tri_l, part) for part in _split3(_log_sigmoid(g)))
    b_rows = sum(_dot(part, tri_u) for part in _split3(_log_sigmoid(gt)))

    hsl = [slice(h * dh, (h + 1) * dh) for h in range(heads)]
    st = []
    for h in range(heads):
        b_col = b_cols[:, heads + h:heads + h + 1]
        b_row = b_rows[heads + h:heads + h + 1, :]
        i_row = gt[h:h + 1, :]
        m_prev = m_s[h:h + 1, 0:1]
        dmat = jnp.where(causal, b_col - b_row + i_row, -jnp.inf)
        inter = b_col + m_prev
        m_t = jnp.maximum(inter, jnp.max(dmat, axis=1, keepdims=True))
        st.append(dict(b_col=b_col, b_row=b_row, i_row=i_row, m_prev=m_prev, m_t=m_t,
                       wts=jnp.exp(dmat - m_t), a=jnp.exp(inter - m_t)))
        fill()
    for h in range(heads):
        s = st[h]
        s['sc'] = _dot(q_ref[0, :, hsl[h]], kt_ref[0, hsl[h], :]) * s['wts']
    out_mm = lambda h: (_dot(st[h]['sc'].astype(BF16), v_ref[0, :, hsl[h]]),
                        _dot(q_ref[0, :, hsl[h]], ct_s[h].astype(BF16)))
    nxt = out_mm(0)
    for h in range(heads):
        intra, inter_mm = nxt
        if h + 1 < heads:
            nxt = out_mm(h + 1)
        fill()
        s = st[h]
        qh = q_ref[0, :, hsl[h]]
        sc = s['sc']
        a = s['a']
        num = intra + a * inter_mm
        qn = jnp.sum(qh.astype(F32) * n_s[h:h + 1, :], axis=1, keepdims=True)
        den = jnp.sum(sc, axis=1, keepdims=True) + a * qn
        hh = num / jnp.maximum(jnp.abs(den), jnp.exp(-s['m_t']))
        mu = jnp.mean(hh, axis=1, keepdims=True)
        dev = hh - mu
        var = jnp.mean(dev * dev, axis=1, keepdims=True)
        emit_hn(h, dev * lax.rsqrt(var + LN_EPS))
    def update_mm(h):
        s = st[h]
        kth = kt_ref[0, hsl[h], :]
        b_last = s['b_row'][:, lc - 1:lc]
        g_row = b_last - s['b_row'] + s['i_row']
        m_new = jnp.maximum(b_last + s['m_prev'], jnp.max(g_row, axis=1, keepdims=True))
        decay = jnp.exp(b_last + s['m_prev'] - m_new)
        wg_row = jnp.exp(g_row - m_new)
        wg_col = jnp.exp(b_last - s['b_col'] + g[:, h:h + 1] - m_new)
        wv = (v_ref[0, :, hsl[h]].astype(F32) * wg_col).astype(BF16)
        wg16 = jnp.broadcast_to(wg_row, (16, lc)).astype(BF16)
        return m_new, decay, _dot(kth, wv), _dot_nt(wg16, kth)[0:1, :]

    fill_update(0)
    nxt = update_mm(0)
    for h in range(heads):
        m_new, decay, c_upd, n_upd = nxt
        if h + 1 < heads:
            fill_update(h + 1)
            nxt = update_mm(h + 1)
        ct_s[h] = decay * ct_s[h] + c_upd
        n_s[h:h + 1, :] = decay * n_s[h:h + 1, :] + n_upd
        m_s[h:h + 1, :] = jnp.broadcast_to(m_new, (1, V7X_LANES))


def _mlstm_layer_kernel(*refs, lc, heads, inner, width, zero_init):
    (x_ref, cbuf_ref, g0_ref, wup_ref, cw_ref, cb_ref, wq_ref, wkt_ref, wv_ref, gc_ref, gh_ref, bg_ref,
     ng_ref, sk_ref, wdown_ref, g1_ref) = refs[:16]
    n_in = 16
    if not zero_init:
        c0_ref, n0_ref, m0_ref = refs[16:19]
        n_in = 19
    o_ref, cout_ref, nout_ref, mout_ref, nbuf_ref = refs[n_in:n_in + 5]
    carry, q_s, kt_s, v_s, xc_s, sz_s, act_s, ct_s, n_s, m_s = refs[n_in + 5:]
    dh = inner // heads
    pad = V7X_SUBLANES
    c = pl.program_id(1)

    @pl.when(c == 0)
    def _():
        carry[...] = jnp.zeros_like(carry)
        carry[pad - (width - 1):pad, :] = cbuf_ref[0]
        if zero_init:
            ct_s[...] = jnp.zeros_like(ct_s)
            n_s[...] = jnp.zeros_like(n_s)
            m_s[...] = jnp.zeros_like(m_s)
        else:
            for h in range(heads):
                ct_s[h] = c0_ref[0, h].T
            n_s[...] = n0_ref[0]
            m_s[...] = m0_ref[0]

    x = x_ref[0]
    hx = _rms(x, g0_ref[...]).astype(BF16)
    w = V7X_MXU
    gacc = bg_ref[...] + _dot(hx, gh_ref[...])
    nslab = inner // w
    col = lambda i: slice(i * w, (i + 1) * w)

    def project(i, xc, xmb, gacc):
        q_s[0, :, col(i)] = _dot(xc, wq_ref[i]).astype(BF16)
        kt_s[0, col(i), :] = _dot_nt(wkt_ref[i], xc).astype(BF16)
        v_s[0, :, col(i)] = _dot(xmb, wv_ref[i]).astype(BF16)
        xc_s[:, col(i)] = xc
        return gacc + _dot(xc, gc_ref[col(i), :])

    xm_next = _dot(hx, wup_ref[:, col(0)])
    pending = None
    for i in range(nslab):
        xm = xm_next
        if i + 1 < nslab:
            xm_next = _dot(hx, wup_ref[:, col(i + 1)])
        if pending is not None:
            gacc = project(*pending, gacc)
        prev = carry[:, col(i)]
        carry[:, col(i)] = xm[lc - pad:lc, :]
        xc = _silu(_causal_dwconv(xm, prev, cw_ref, cb_ref, col(i))).astype(BF16)
        pending = (i, xc, xm.astype(BF16))
    gacc = project(*pending, gacc)
    nbuf_ref[0] = carry[pad - (width - 1):pad, :]

    slabs = iter(range(inner // w))

    def gate_slab():
        i = next(slabs, None)
        if i is not None:
            cs = slice(i * w, (i + 1) * w)
            sz_s[:, cs] = _silu(_dot(hx, wup_ref[:, inner + i * w:inner + (i + 1) * w])).astype(BF16)

    hcols = lambda h: slice(h * dh, (h + 1) * dh)
    down = []

    def gated_out(h, hn):
        hs = hn * ng_ref[:, hcols(h)] + sk_ref[:, hcols(h)] * xc_s[:, hcols(h)].astype(F32)
        act_s[:, hcols(h)] = (hs * sz_s[:, hcols(h)].astype(F32)).astype(BF16)

    def down_proj(h):
        if h == 0:
            down.append(_dot(act_s[...], wdown_ref[...]))

    _mlstm_chunk(q_s, kt_s, v_s, gacc, gacc.T[0:2 * heads, :], ct_s, n_s, m_s, gate_slab, gated_out, down_proj,
                 lc=lc, heads=heads, dh=dh)
    for _ in slabs:
        raise AssertionError("fewer fill points than output-gate slabs")
    o_ref[0] = x + _rms(down[0], g1_ref[...])

    @pl.when(c == pl.num_programs(1) - 1)
    def _():
        for h in range(heads):
            cout_ref[0, h] = ct_s[h].T
        nout_ref[0] = n_s[...]
        mout_ref[0] = m_s[...]


def _mlstm_layer(x, cbuf, state, p, g0, g1, lc):
    b, s, d = x.shape
    heads = p['heads']
    inner = p['m_cw'].shape[1]
    width = p['m_cw'].shape[0]
    dh = inner // heads
    zero_init = state is None
    kern = functools.partial(_mlstm_layer_kernel, lc=lc, heads=heads, inner=inner, width=width,
                             zero_init=zero_init)
    xs = pl.BlockSpec((1, lc, d), lambda i, c: (i, c, 0))
    bufs = pl.BlockSpec((1, width - 1, inner), lambda i, c: (i, 0, 0))
    cs = pl.BlockSpec((1, heads, dh, dh), lambda i, c: (i, 0, 0, 0))
    ns = pl.BlockSpec((1, heads, dh), lambda i, c: (i, 0, 0))
    ms = pl.BlockSpec((1, heads, V7X_LANES), lambda i, c: (i, 0, 0))
    consts = [g0, p['m_wup'], p['m_cw'], p['m_cb'], p['m_wq'], p['m_wkt'], p['m_wv'], p['m_gc'], p['m_gh'],
              p['m_bg'], p['m_ng'], p['m_skip'], p['m_wdown'], g1]
    in_specs = [xs, bufs] + [_const_spec(a.shape) for a in consts]
    args = [x, cbuf] + consts
    if not zero_init:
        in_specs += [cs, ns, ms]
        args += list(state)
    blocks = 2 * _nbytes((lc, d), F32) + (1 if zero_init else 2) * _nbytes((heads, dh, dh), F32)
    scratch = (sum(_nbytes(a.shape, a.dtype) for a in consts) + 6 * _nbytes((lc, inner), BF16)
               + _nbytes((heads, dh, dh), F32) + 8 * _nbytes((lc, max(lc, dh)), F32) + 2 * _nbytes((dh, dh), F32))
    return pl.pallas_call(
        kern,
        grid=(b, s // lc),
        in_specs=in_specs,
        out_specs=[xs, cs, ns, ms, bufs],
        out_shape=[jax.ShapeDtypeStruct((b, s, d), F32), jax.ShapeDtypeStruct((b, heads, dh, dh), F32),
                   jax.ShapeDtypeStruct((b, heads, dh), F32), jax.ShapeDtypeStruct((b, heads, V7X_LANES), F32),
                   jax.ShapeDtypeStruct((b, width - 1, inner), F32)],
        scratch_shapes=[pltpu.VMEM((V7X_SUBLANES, inner), F32),
                        pltpu.VMEM((1, lc, inner), BF16), pltpu.VMEM((1, inner, lc), BF16),
                        pltpu.VMEM((1, lc, inner), BF16), pltpu.VMEM((lc, inner), BF16),
                        pltpu.VMEM((lc, inner), BF16), pltpu.VMEM((lc, inner), BF16),
                        pltpu.VMEM((heads, dh, dh), F32), pltpu.VMEM((heads, dh), F32),
                        pltpu.VMEM((heads, V7X_LANES), F32)],
        compiler_params=pltpu.CompilerParams(
            dimension_semantics=("arbitrary", "arbitrary"), vmem_limit_bytes=_vmem_limit(blocks, scratch)),
        name="mlstm_layer",
    )(*args)


def _prep_weights(w):
    row = lambda v: v.astype(F32).reshape(1, -1)
    p = {}
    p['norm_g'] = w['norm_g'].astype(F32)
    p['s5_mats'] = _s5_matrices(w['s5_A_re'][0], w['s5_A_im'][0], w['s5_log_dt'][0], w['s5_B_re'][0],
                                w['s5_B_im'][0], w['s5_C_re'][0], w['s5_C_im'][0])
    p['s5_D'] = row(w['s5_D'][0])
    p['s5_w_glu'] = w['s5_w_glu'][0].astype(BF16)
    inner = w['mlstm_conv_w'].shape[2]
    heads = w['mlstm_b_gate'].shape[1] // 2
    dh = inner // heads
    p['m_wq'], p['m_wkt'], p['m_wv'], p['m_gc'], gh = _mlstm_weights(
        w['mlstm_wq'][0], w['mlstm_wk'][0], w['mlstm_wv'][0], w['mlstm_w_gate'][0], w['mlstm_w_up'][0],
        dh ** -0.5)
    p['m_gh'] = gh.astype(BF16)
    bg = row(w['mlstm_b_gate'][0])
    p['m_bg'] = jnp.pad(bg, ((0, 0), (0, V7X_LANES - bg.shape[1])))
    p['m_wup'] = w['mlstm_w_up'][0].astype(BF16)
    p['m_cw'] = w['mlstm_conv_w'][0].astype(F32)
    p['m_cb'] = row(w['mlstm_conv_b'][0])
    p['m_ng'] = row(w['mlstm_norm_g'][0])
    p['m_skip'] = row(w['mlstm_skip'][0])
    p['m_wdown'] = w['mlstm_w_down'][0].astype(BF16)
    p['heads'] = heads
    p['f_wup'] = w['ffn_w_up'].astype(BF16)
    p['f_cw'] = w['ffn_conv_w'].astype(F32)
    p['f_cb'] = w['ffn_conv_b'].astype(F32)[:, None, :]
    p['f_wdown'] = w['ffn_w_down'].astype(BF16)
    return p


def _to_chunk_major(u2d, b, s):
    d = u2d.shape[1]
    g = d // S5_GROUP_CH
    u = u2d.reshape(b, s // S5_CHUNK, S5_CHUNK, g, S5_GROUP_CH)[:, :, ::-1]
    return jnp.transpose(u, (1, 0, 3, 2, 4)).reshape((s // S5_CHUNK) * b, d * S5_CHUNK)


def _to_token_major(yt, b, s):
    d = yt.shape[1] // S5_CHUNK
    g = d // S5_GROUP_CH
    y = yt.reshape(s // S5_CHUNK, b, g, S5_CHUNK, S5_GROUP_CH)
    return jnp.transpose(y, (1, 0, 3, 2, 4)).reshape(b * s, d)


def _trunk(x, st, p, tiles):
    b, s, d = x.shape
    ng = p['norm_g']
    g_of = lambda layer, k: ng[layer, k].reshape(1, d)
    zero_init = st is None
    groups = d // S5_GROUP_CH
    x2d = x.reshape(b * s, d)

    n_state = p['s5_mats'][1].shape[1] // 2
    if zero_init:
        h0re = h0im = jnp.zeros((groups // 2, b, 2 * n_state), F32)
    else:
        pair = lambda h: jnp.transpose(h.astype(F32).reshape(b, groups // 2, 2 * n_state), (1, 0, 2))
        h0re, h0im = pair(st['s5_re']), pair(st['s5_im'])
    if tiles.s5_time:
        y, hre, him = _s5_fused(x, g_of(0, 0), p['s5_mats'], h0re, h0im, tiles.s5_time)
        y2d = y.reshape(b * s, d)
    else:
        u = _norm_cast(x2d, g_of(0, 0), tiles.rows)
        yt, hre, him = _s5_core(_to_chunk_major(u, b, s), p['s5_mats'], h0re, h0im, b, tiles.s5_rows)
        y2d = _to_token_major(yt, b, s)
    unpair = lambda h: jnp.transpose(h, (1, 0, 2)).reshape(1, b, groups, n_state)
    o_re, o_im = unpair(hre), unpair(him)
    x2d = _glu(x2d, y2d, g_of(0, 0), p['s5_D'], p['s5_w_glu'], g_of(0, 1), tiles.rows)

    def ffn(x2d, layer):
        dff2 = p['f_wup'].shape[2]
        width = p['f_cw'].shape[1]
        buf = jnp.zeros((b, width - 1, dff2), F32) if zero_init else st['ffn_conv'][layer].astype(F32)
        xo, nbuf = _ffn(x2d.reshape(b, s, d), buf, g_of(layer, 2), p['f_wup'], p['f_cw'], p['f_cb'],
                        p['f_wdown'], g_of(layer, 3), layer, tiles.ffn_time)
        return xo.reshape(b * s, d), nbuf

    x2d, fbuf0 = ffn(x2d, 0)

    heads = p['heads']
    inner = p['m_cw'].shape[1]
    width = p['m_cw'].shape[0]
    cbuf = jnp.zeros((b, width - 1, inner), F32) if zero_init else st['mlstm_conv'].astype(F32)
    if zero_init:
        state = None
    else:
        state = (st['mlstm_C'].astype(F32), st['mlstm_n'].astype(F32),
                 jnp.broadcast_to(st['mlstm_m'].astype(F32)[:, :, None], (b, heads, V7X_LANES)))
    x3, c_out, n_out, m_out, ncbuf = _mlstm_layer(x2d.reshape(b, s, d), cbuf, state, p, g_of(1, 0), g_of(1, 1),
                                                  tiles.mlstm_chunk)
    x2d, fbuf1 = ffn(x3.reshape(b * s, d), 1)

    return (x2d.reshape(b, s, d), o_re, o_im, c_out[None], n_out[None], m_out[None, :, :, 0], ncbuf[None],
            jnp.stack([fbuf0, fbuf1], axis=0))


def kernel(x_prompt, x_sample, state_s5_re, state_s5_im, state_mlstm_C, state_mlstm_n, state_mlstm_m, state_mlstm_conv, state_ffn_conv, norm_g, s5_A_re, s5_A_im, s5_log_dt, s5_B_re, s5_B_im, s5_C_re, s5_C_im, s5_D, s5_w_glu, mlstm_w_up, mlstm_conv_w, mlstm_conv_b, mlstm_wq, mlstm_wk, mlstm_wv, mlstm_w_gate, mlstm_b_gate, mlstm_norm_g, mlstm_skip, mlstm_w_down, ffn_w_up, ffn_conv_w, ffn_conv_b, ffn_w_down):
    w = {
        'norm_g': norm_g, 's5_A_re': s5_A_re, 's5_A_im': s5_A_im, 's5_log_dt': s5_log_dt,
        's5_B_re': s5_B_re, 's5_B_im': s5_B_im, 's5_C_re': s5_C_re, 's5_C_im': s5_C_im, 's5_D': s5_D,
        's5_w_glu': s5_w_glu, 'mlstm_w_up': mlstm_w_up, 'mlstm_conv_w': mlstm_conv_w,
        'mlstm_conv_b': mlstm_conv_b, 'mlstm_wq': mlstm_wq, 'mlstm_wk': mlstm_wk, 'mlstm_wv': mlstm_wv,
        'mlstm_w_gate': mlstm_w_gate, 'mlstm_b_gate': mlstm_b_gate, 'mlstm_norm_g': mlstm_norm_g,
        'mlstm_skip': mlstm_skip, 'mlstm_w_down': mlstm_w_down, 'ffn_w_up': ffn_w_up,
        'ffn_conv_w': ffn_conv_w, 'ffn_conv_b': ffn_conv_b, 'ffn_w_down': ffn_w_down,
    }
    assert norm_g.shape[0] == 2 and s5_A_re.shape[0] == 1 and mlstm_w_up.shape[0] == 1
    assert x_prompt.shape[0] == V7X_SUBLANES and x_sample.shape[0] == V7X_SUBLANES
    p = _prep_weights(w)
    out_p = _trunk(x_prompt.astype(F32), None, p, _pick_tiles(*x_prompt.shape[:2]))
    st = {'s5_re': state_s5_re[0], 's5_im': state_s5_im[0], 'mlstm_C': state_mlstm_C[0],
          'mlstm_n': state_mlstm_n[0], 'mlstm_m': state_mlstm_m[0], 'mlstm_conv': state_mlstm_conv[0],
          'ffn_conv': state_ffn_conv}
    out_s = _trunk(x_sample.astype(F32), st, p, _pick_tiles(*x_sample.shape[:2]))
    return (out_p[0], out_s[0]) + tuple(out_p[1:]) + tuple(out_s[1:])
```
